```python
import jax, jax.numpy as jnp
from jax import lax
import numpy as np

D_MODEL = 1024
BATCH = 8
SEQ = 4096
DEPTH = 4

N_MIXERS = 2
N_MLA = (DEPTH + 1) // 2
N_HGRN = DEPTH // 2

MLA_HEADS = 8
MLA_Q_LORA = 512
MLA_KV_LORA = 256
MLA_NOPE = 128
MLA_ROPE = 64
MLA_V = 128
ROPE_BASE = 10000.0
ATTN_BLOCK = 128

HGRN_HEADS = 8
HGRN_DK = D_MODEL // HGRN_HEADS
HGRN_DV = D_MODEL // HGRN_HEADS
HGRN_CHUNK = 32

D_FF = 4 * D_MODEL

EPS = 1e-6

kernel_name = "hybrid_mla_hgrn2_sqrelu_sandwich"


def rms_norm(x, g):
    xf = x.astype(jnp.float32)
    y = xf * lax.rsqrt(jnp.mean(xf * xf, axis=-1, keepdims=True) + EPS)
    return (y * g.astype(jnp.float32)).astype(x.dtype)


def rope_cos_sin(positions):
    inv_freq = jnp.power(ROPE_BASE, -jnp.arange(0, MLA_ROPE, 2, dtype=jnp.float32) / MLA_ROPE)
    ang = positions.astype(jnp.float32)[..., None] * inv_freq
    return jnp.cos(ang), jnp.sin(ang)


def apply_rope(t, cos, sin):
    tf = t.astype(jnp.float32)
    t1, t2 = jnp.split(tf, 2, axis=-1)
    return jnp.concatenate([t1 * cos - t2 * sin, t1 * sin + t2 * cos], axis=-1).astype(t.dtype)


def mla_mixer(h, cos, sin, w_in, q_norm, kv_norm, w_uq, w_ukv, w_o):
    B, S, _ = h.shape
    H = MLA_HEADS
    proj = h @ w_in
    c_q, c_kv, k_r = jnp.split(proj, [MLA_Q_LORA, MLA_Q_LORA + MLA_KV_LORA], axis=-1)
    q = (rms_norm(c_q, q_norm) @ w_uq).reshape(B, S, H, MLA_NOPE + MLA_ROPE)
    q_nope = q[..., :MLA_NOPE]
    q_rope = apply_rope(q[..., MLA_NOPE:], cos[:, :, None, :], sin[:, :, None, :])
    kv = (rms_norm(c_kv, kv_norm) @ w_ukv).reshape(B, S, H, MLA_NOPE + MLA_V)
    k_nope, v = kv[..., :MLA_NOPE], kv[..., MLA_NOPE:]
    k_rope = apply_rope(k_r, cos, sin)
    scale = (MLA_NOPE + MLA_ROPE) ** -0.5
    nb = S // ATTN_BLOCK
    qn_b = q_nope.reshape(B, nb, ATTN_BLOCK, H, MLA_NOPE).transpose(1, 0, 2, 3, 4)
    qr_b = q_rope.reshape(B, nb, ATTN_BLOCK, H, MLA_ROPE).transpose(1, 0, 2, 3, 4)
    k_pos = jnp.arange(S)

    def block(args):
        qn, qr, blk = args
        s = (jnp.einsum('bqhd,bkhd->bhqk', qn, k_nope)
             + jnp.einsum('bqhr,bkr->bhqk', qr, k_rope)).astype(jnp.float32) * scale
        q_pos = blk * ATTN_BLOCK + jnp.arange(ATTN_BLOCK)
        s = jnp.where(q_pos[:, None] >= k_pos[None, :], s, -jnp.inf)
        p = jax.nn.softmax(s, axis=-1).astype(v.dtype)
        return jnp.einsum('bhqk,bkhd->bqhd', p, v)

    o = lax.map(block, (qn_b, qr_b, jnp.arange(nb)))
    o = o.transpose(1, 0, 2, 3, 4).reshape(B, S, H * MLA_V)
    return o @ w_o


def hgrn2_mixer(h, lb, w_in, o_norm, w_o):
    B, S, _ = h.shape
    H, DK, DV, C = HGRN_HEADS, HGRN_DK, HGRN_DV, HGRN_CHUNK
    nc = S // C
    HK, HV = H * DK, H * DV
    proj = h @ w_in
    q_x, f_x, i_x, g_x = jnp.split(proj, [HK, 2 * HK, 2 * HK + HV], axis=-1)

    def heads(t, d):
        return t.astype(jnp.float32).reshape(B, nc, C, H, d).transpose(0, 3, 1, 2, 4)

    f = lb + (1.0 - lb) * jax.nn.sigmoid(f_x.astype(jnp.float32))
    q = heads(jax.nn.silu(q_x.astype(jnp.float32)), DK)
    k = heads(1.0 - f, DK)
    log_f = heads(jnp.log(f), DK)
    v = heads(i_x, DV)

    b = jnp.cumsum(log_f, axis=3)
    b_ref = b[:, :, :, C // 2:C // 2 + 1, :]
    b_last = b[:, :, :, -1:, :]
    q_rel = q * jnp.exp(b - b_ref)
    k_rel = k * jnp.exp(b_ref - b)
    causal = jnp.tril(jnp.ones((C, C), dtype=bool))
    a = jnp.where(causal, jnp.einsum('bhncd,bhnsd->bhncs', q_rel, k_rel), 0.0)
    o_intra = jnp.einsum('bhncs,bhnse->bhnce', a, v)

    q_dec = q * jnp.exp(b)
    k_dec = k * jnp.exp(b_last - b)
    chunk_decay = jnp.exp(b_last[:, :, :, 0, :])

    def step(state, xs):
        qd, kd, vc, dec = xs
        o_inter = jnp.einsum('bhcd,bhde->bhce', qd, state)
        state = dec[..., None] * state + jnp.einsum('bhcd,bhce->bhde', kd, vc)
        return state, o_inter

    mv = lambda t: jnp.moveaxis(t, 2, 0)
    s0 = jnp.zeros((B, H, DK, DV), jnp.float32)
    _, o_inter = lax.scan(step, s0, (mv(q_dec), mv(k_dec), mv(v), mv(chunk_decay)))
    o = o_intra + jnp.moveaxis(o_inter, 0, 2)
    o = o.transpose(0, 2, 3, 1, 4).reshape(B, S, H, DV)
    gate = jax.nn.silu(g_x.astype(jnp.float32)).reshape(B, S, H, DV)
    o = rms_norm(o, o_norm) * gate
    return o.reshape(B, S, HV).astype(h.dtype) @ w_o


def sq_relu_mlp(h, w1, w2):
    a = jax.nn.relu(h @ w1)
    return (a * a) @ w2


def _fwd_setup_inputs(seed: int = 0) -> dict:
    key = jax.random.key(seed)
    ks = jax.random.split(key, 16)
    f32 = jnp.float32
    nrm = lambda k, shape, fan_in: jax.random.normal(k, shape, f32) * (fan_in ** -0.5)
    mla_in_w = MLA_Q_LORA + MLA_KV_LORA + MLA_ROPE
    hgrn_in_w = 3 * HGRN_HEADS * HGRN_DK + HGRN_HEADS * HGRN_DV
    return {
        "x": jax.random.normal(ks[0], (BATCH, SEQ, D_MODEL), f32),
        "positions": jnp.broadcast_to(jnp.arange(SEQ, dtype=jnp.int32), (BATCH, SEQ)),
        "norm_gains": 1.0 + 0.1 * jax.random.normal(ks[1], (DEPTH, 4, D_MODEL), f32),
        "mla_w_in": nrm(ks[2], (N_MLA, D_MODEL, mla_in_w), D_MODEL),
        "mla_q_norm": 1.0 + 0.1 * jax.random.normal(ks[3], (N_MLA, MLA_Q_LORA), f32),
        "mla_kv_norm": 1.0 + 0.1 * jax.random.normal(ks[4], (N_MLA, MLA_KV_LORA), f32),
        "mla_w_uq": nrm(ks[5], (N_MLA, MLA_Q_LORA, MLA_HEADS * (MLA_NOPE + MLA_ROPE)), MLA_Q_LORA),
        "mla_w_ukv": nrm(ks[6], (N_MLA, MLA_KV_LORA, MLA_HEADS * (MLA_NOPE + MLA_V)), MLA_KV_LORA),
        "mla_w_o": nrm(ks[7], (N_MLA, MLA_HEADS * MLA_V, D_MODEL), MLA_HEADS * MLA_V),
        "hgrn_w_in": nrm(ks[8], (N_HGRN, D_MODEL, hgrn_in_w), D_MODEL),
        "hgrn_lb_logits": 0.1 * jax.random.normal(ks[9], (DEPTH, HGRN_HEADS * HGRN_DK), f32),
        "hgrn_o_norm": 1.0 + 0.1 * jax.random.normal(ks[10], (N_HGRN, HGRN_DV), f32),
        "hgrn_w_o": nrm(ks[11], (N_HGRN, HGRN_HEADS * HGRN_DV, D_MODEL), HGRN_HEADS * HGRN_DV),
        "mlp_w1": nrm(ks[12], (DEPTH, D_MODEL, D_FF), D_MODEL),
        "mlp_w2": nrm(ks[13], (DEPTH, D_FF, D_MODEL), D_FF),
    }


def _fwd_reference(x, positions, norm_gains, mla_w_in, mla_q_norm, mla_kv_norm, mla_w_uq, mla_w_ukv,
              mla_w_o, hgrn_w_in, hgrn_lb_logits, hgrn_o_norm, hgrn_w_o, mlp_w1, mlp_w2):
    cos, sin = rope_cos_sin(positions)
    p = jax.nn.softmax(hgrn_lb_logits.astype(jnp.float32), axis=0)
    lower_bounds = jnp.cumsum(p, axis=0) - p[0]
    h = x
    for layer in range(DEPTH):
        slot = layer // N_MIXERS
        a = rms_norm(h, norm_gains[layer, 0])
        if layer % N_MIXERS == 0:
            m = mla_mixer(a, cos, sin, mla_w_in[slot], mla_q_norm[slot], mla_kv_norm[slot],
                          mla_w_uq[slot], mla_w_ukv[slot], mla_w_o[slot])
        else:
            m = hgrn2_mixer(a, lower_bounds[layer], hgrn_w_in[slot], hgrn_o_norm[slot], hgrn_w_o[slot])
        h = h + rms_norm(m, norm_gains[layer, 1])
        a = rms_norm(h, norm_gains[layer, 2])
        h = h + rms_norm(sq_relu_mlp(a, mlp_w1[layer], mlp_w2[layer]), norm_gains[layer, 3])
    return h


import jax as _jax
import jax.numpy as _jnp

TWIN_FORMAT = 'train_step'
FWD_PARAMS = ['x', 'positions', 'norm_gains', 'mla_w_in', 'mla_q_norm', 'mla_kv_norm', 'mla_w_uq', 'mla_w_ukv', 'mla_w_o', 'hgrn_w_in', 'hgrn_lb_logits', 'hgrn_o_norm', 'hgrn_w_o', 'mlp_w1', 'mlp_w2']
TWIN_WEIGHTS = ['norm_gains', 'mla_w_in', 'mla_q_norm', 'mla_kv_norm', 'mla_w_uq', 'mla_w_ukv', 'mla_w_o', 'hgrn_w_in', 'hgrn_lb_logits', 'hgrn_o_norm', 'hgrn_w_o', 'mlp_w1', 'mlp_w2']
TWIN_DIFF_INPUT = 'x'
TWIN_INPUTS = ['x', 'positions', 'norm_gains', 'mla_w_in', 'mla_q_norm', 'mla_kv_norm', 'mla_w_uq', 'mla_w_ukv', 'mla_w_o', 'hgrn_w_in', 'hgrn_lb_logits', 'hgrn_o_norm', 'hgrn_w_o', 'mlp_w1', 'mlp_w2', 'loss_target', 'm_norm_gains', 'm_mla_w_in', 'm_mla_q_norm', 'm_mla_kv_norm', 'm_mla_w_uq', 'm_mla_w_ukv', 'm_mla_w_o', 'm_hgrn_w_in', 'm_hgrn_lb_logits', 'm_hgrn_o_norm', 'm_hgrn_w_o', 'm_mlp_w1', 'm_mlp_w2', 'v_norm_gains', 'v_mla_w_in', 'v_mla_q_norm', 'v_mla_kv_norm', 'v_mla_w_uq', 'v_mla_w_ukv', 'v_mla_w_o', 'v_hgrn_w_in', 'v_hgrn_lb_logits', 'v_hgrn_o_norm', 'v_hgrn_w_o', 'v_mlp_w1', 'v_mlp_w2']
TWIN_OUTPUTS = ['loss', 'grad_x', 'grad_norm_gains', 'grad_mla_w_in', 'grad_mla_q_norm', 'grad_mla_kv_norm', 'grad_mla_w_uq', 'grad_mla_w_ukv', 'grad_mla_w_o', 'grad_hgrn_w_in', 'grad_hgrn_lb_logits', 'grad_hgrn_o_norm', 'grad_hgrn_w_o', 'grad_mlp_w1', 'grad_mlp_w2', 'delta_norm_gains', 'delta_mla_w_in', 'delta_mla_q_norm', 'delta_mla_kv_norm', 'delta_mla_w_uq', 'delta_mla_w_ukv', 'delta_mla_w_o', 'delta_hgrn_w_in', 'delta_hgrn_lb_logits', 'delta_hgrn_o_norm', 'delta_hgrn_w_o', 'delta_mlp_w1', 'delta_mlp_w2', 'new_m_norm_gains', 'new_m_mla_w_in', 'new_m_mla_q_norm', 'new_m_mla_kv_norm', 'new_m_mla_w_uq', 'new_m_mla_w_ukv', 'new_m_mla_w_o', 'new_m_hgrn_w_in', 'new_m_hgrn_lb_logits', 'new_m_hgrn_o_norm', 'new_m_hgrn_w_o', 'new_m_mlp_w1', 'new_m_mlp_w2', 'new_v_norm_gains', 'new_v_mla_w_in', 'new_v_mla_q_norm', 'new_v_mla_kv_norm', 'new_v_mla_w_uq', 'new_v_mla_w_ukv', 'new_v_mla_w_o', 'new_v_hgrn_w_in', 'new_v_hgrn_lb_logits', 'new_v_hgrn_o_norm', 'new_v_hgrn_w_o', 'new_v_mlp_w1', 'new_v_mlp_w2']
TWIN_LEAF_KINDS = {'loss': 'loss', 'grad_x': 'grad_x', 'grad_norm_gains': 'grad_w', 'grad_mla_w_in': 'grad_w', 'grad_mla_q_norm': 'grad_w', 'grad_mla_kv_norm': 'grad_w', 'grad_mla_w_uq': 'grad_w', 'grad_mla_w_ukv': 'grad_w', 'grad_mla_w_o': 'grad_w', 'grad_hgrn_w_in': 'grad_w', 'grad_hgrn_lb_logits': 'grad_w', 'grad_hgrn_o_norm': 'grad_w', 'grad_hgrn_w_o': 'grad_w', 'grad_mlp_w1': 'grad_w', 'grad_mlp_w2': 'grad_w', 'delta_norm_gains': 'delta_w', 'delta_mla_w_in': 'delta_w', 'delta_mla_q_norm': 'delta_w', 'delta_mla_kv_norm': 'delta_w', 'delta_mla_w_uq': 'delta_w', 'delta_mla_w_ukv': 'delta_w', 'delta_mla_w_o': 'delta_w', 'delta_hgrn_w_in': 'delta_w', 'delta_hgrn_lb_logits': 'delta_w', 'delta_hgrn_o_norm': 'delta_w', 'delta_hgrn_w_o': 'delta_w', 'delta_mlp_w1': 'delta_w', 'delta_mlp_w2': 'delta_w', 'new_m_norm_gains': 'new_m', 'new_m_mla_w_in': 'new_m', 'new_m_mla_q_norm': 'new_m', 'new_m_mla_kv_norm': 'new_m', 'new_m_mla_w_uq': 'new_m', 'new_m_mla_w_ukv': 'new_m', 'new_m_mla_w_o': 'new_m', 'new_m_hgrn_w_in': 'new_m', 'new_m_hgrn_lb_logits': 'new_m', 'new_m_hgrn_o_norm': 'new_m', 'new_m_hgrn_w_o': 'new_m', 'new_m_mlp_w1': 'new_m', 'new_m_mlp_w2': 'new_m', 'new_v_norm_gains': 'new_v', 'new_v_mla_w_in': 'new_v', 'new_v_mla_q_norm': 'new_v', 'new_v_mla_kv_norm': 'new_v', 'new_v_mla_w_uq': 'new_v', 'new_v_mla_w_ukv': 'new_v', 'new_v_mla_w_o': 'new_v', 'new_v_hgrn_w_in': 'new_v', 'new_v_hgrn_lb_logits': 'new_v', 'new_v_hgrn_o_norm': 'new_v', 'new_v_hgrn_w_o': 'new_v', 'new_v_mlp_w1': 'new_v', 'new_v_mlp_w2': 'new_v'}


def _forward(args):
    return _fwd_reference(*[args[k] for k in FWD_PARAMS])


def _output_shape():
    def fwd():
        inp = _fwd_setup_inputs(0)
        return _fwd_reference(*[inp[k] for k in FWD_PARAMS])
    out = _jax.eval_shape(fwd)
    return out.shape, out.dtype

N_MICROBATCH = 1
ADAM_LR = 0.001
ADAM_B1 = 0.9
ADAM_B2 = 0.999
ADAM_EPS = 1e-08
ADAM_WD = 0.01
ADAM_STEP = 10
PER_EXAMPLE_BATCH_AXIS = {'x': 0, 'positions': 0, 'loss_target': 0}
SHARED_INPUTS = []
_WEIGHT_DTYPES = {'norm_gains': _jnp.float32, 'mla_w_in': _jnp.float32, 'mla_q_norm': _jnp.float32, 'mla_kv_norm': _jnp.float32, 'mla_w_uq': _jnp.float32, 'mla_w_ukv': _jnp.float32, 'mla_w_o': _jnp.float32, 'hgrn_w_in': _jnp.float32, 'hgrn_lb_logits': _jnp.float32, 'hgrn_o_norm': _jnp.float32, 'hgrn_w_o': _jnp.float32, 'mlp_w1': _jnp.float32, 'mlp_w2': _jnp.float32}
MOMENT_SCALE = {'norm_gains': 3.066115e+01, 'mla_w_in': 2.754907e+01, 'mla_q_norm': 2.857228e+00, 'mla_kv_norm': 4.600269e+01, 'mla_w_uq': 1.532696e+00, 'mla_w_ukv': 1.701096e+01, 'mla_w_o': 2.377219e+01, 'hgrn_w_in': 8.571503e+00, 'hgrn_lb_logits': 8.620589e-02, 'hgrn_o_norm': 3.918819e+01, 'hgrn_w_o': 1.384514e+01, 'mlp_w1': 5.648174e+00, 'mlp_w2': 2.329309e+01}


def _to_microbatches(a, axis):
    t = _jnp.moveaxis(a, axis, 0)
    t = t.reshape((N_MICROBATCH, t.shape[0] // N_MICROBATCH) + t.shape[1:])
    return _jnp.moveaxis(t, 1, axis + 1)


def setup_inputs(seed: int = 0) -> dict:
    inp = _fwd_setup_inputs(seed)
    key = _jax.random.fold_in(_jax.random.key(seed), 7919)
    shape, _ = _output_shape()
    out = dict(inp)
    out["loss_target"] = _jax.random.normal(_jax.random.fold_in(key, 0), shape, _jnp.float32)
    for i, name in enumerate(TWIN_WEIGHTS):
        w = inp[name].astype(_jnp.float32)
        if MOMENT_SCALE is None:
            s = _jnp.sqrt(_jnp.mean(_jnp.square(w)) + 1e-30)
        else:
            s = MOMENT_SCALE[name]
        km, kv = _jax.random.split(_jax.random.fold_in(key, i + 1))
        out[name] = w
        out["m_" + name] = s * _jax.random.normal(km, w.shape, _jnp.float32)
        out["v_" + name] = (s * s) * _jax.random.uniform(kv, w.shape, _jnp.float32, 0.5, 1.5)
    if N_MICROBATCH > 1:
        for name, axis in PER_EXAMPLE_BATCH_AXIS.items():
            out[name] = _to_microbatches(out[name], axis)
    return {'x': out['x'], 'positions': out['positions'], 'norm_gains': out['norm_gains'], 'mla_w_in': out['mla_w_in'], 'mla_q_norm': out['mla_q_norm'], 'mla_kv_norm': out['mla_kv_norm'], 'mla_w_uq': out['mla_w_uq'], 'mla_w_ukv': out['mla_w_ukv'], 'mla_w_o': out['mla_w_o'], 'hgrn_w_in': out['hgrn_w_in'], 'hgrn_lb_logits': out['hgrn_lb_logits'], 'hgrn_o_norm': out['hgrn_o_norm'], 'hgrn_w_o': out['hgrn_w_o'], 'mlp_w1': out['mlp_w1'], 'mlp_w2': out['mlp_w2'], 'loss_target': out['loss_target'], 'm_norm_gains': out['m_norm_gains'], 'm_mla_w_in': out['m_mla_w_in'], 'm_mla_q_norm': out['m_mla_q_norm'], 'm_mla_kv_norm': out['m_mla_kv_norm'], 'm_mla_w_uq': out['m_mla_w_uq'], 'm_mla_w_ukv': out['m_mla_w_ukv'], 'm_mla_w_o': out['m_mla_w_o'], 'm_hgrn_w_in': out['m_hgrn_w_in'], 'm_hgrn_lb_logits': out['m_hgrn_lb_logits'], 'm_hgrn_o_norm': out['m_hgrn_o_norm'], 'm_hgrn_w_o': out['m_hgrn_w_o'], 'm_mlp_w1': out['m_mlp_w1'], 'm_mlp_w2': out['m_mlp_w2'], 'v_norm_gains': out['v_norm_gains'], 'v_mla_w_in': out['v_mla_w_in'], 'v_mla_q_norm': out['v_mla_q_norm'], 'v_mla_kv_norm': out['v_mla_kv_norm'], 'v_mla_w_uq': out['v_mla_w_uq'], 'v_mla_w_ukv': out['v_mla_w_ukv'], 'v_mla_w_o': out['v_mla_w_o'], 'v_hgrn_w_in': out['v_hgrn_w_in'], 'v_hgrn_lb_logits': out['v_hgrn_lb_logits'], 'v_hgrn_o_norm': out['v_hgrn_o_norm'], 'v_hgrn_w_o': out['v_hgrn_w_o'], 'v_mlp_w1': out['v_mlp_w1'], 'v_mlp_w2': out['v_mlp_w2']}


def _loss(weights, diff, rest, loss_target):
    with _jax.named_scope("forward"):
        args = {**rest, TWIN_DIFF_INPUT: diff, **{k: w.astype(_WEIGHT_DTYPES[k]) for k, w in weights.items()}}
        y = _forward(args)
    with _jax.named_scope("loss_head"):
        err = _jnp.square(y.astype(_jnp.float32) - loss_target)
        return 0.5 * _jnp.sum(_jnp.mean(err, axis=-1)) if err.ndim else 0.5 * err


def _adamw(w, g, m, v):
    m = ADAM_B1 * m + (1.0 - ADAM_B1) * g
    v = ADAM_B2 * v + (1.0 - ADAM_B2) * _jnp.square(g)
    m_hat = m / (1.0 - ADAM_B1 ** ADAM_STEP)
    v_hat = v / (1.0 - ADAM_B2 ** ADAM_STEP)
    delta = -ADAM_LR * (m_hat / (_jnp.sqrt(v_hat) + ADAM_EPS) + ADAM_WD * w)
    return delta, m, v


def reference(x, positions, norm_gains, mla_w_in, mla_q_norm, mla_kv_norm, mla_w_uq, mla_w_ukv, mla_w_o, hgrn_w_in, hgrn_lb_logits, hgrn_o_norm, hgrn_w_o, mlp_w1, mlp_w2, loss_target, m_norm_gains, m_mla_w_in, m_mla_q_norm, m_mla_kv_norm, m_mla_w_uq, m_mla_w_ukv, m_mla_w_o, m_hgrn_w_in, m_hgrn_lb_logits, m_hgrn_o_norm, m_hgrn_w_o, m_mlp_w1, m_mlp_w2, v_norm_gains, v_mla_w_in, v_mla_q_norm, v_mla_kv_norm, v_mla_w_uq, v_mla_w_ukv, v_mla_w_o, v_hgrn_w_in, v_hgrn_lb_logits, v_hgrn_o_norm, v_hgrn_w_o, v_mlp_w1, v_mlp_w2):
    given = dict(x=x, positions=positions, norm_gains=norm_gains, mla_w_in=mla_w_in, mla_q_norm=mla_q_norm, mla_kv_norm=mla_kv_norm, mla_w_uq=mla_w_uq, mla_w_ukv=mla_w_ukv, mla_w_o=mla_w_o, hgrn_w_in=hgrn_w_in, hgrn_lb_logits=hgrn_lb_logits, hgrn_o_norm=hgrn_o_norm, hgrn_w_o=hgrn_w_o, mlp_w1=mlp_w1, mlp_w2=mlp_w2, loss_target=loss_target, m_norm_gains=m_norm_gains, m_mla_w_in=m_mla_w_in, m_mla_q_norm=m_mla_q_norm, m_mla_kv_norm=m_mla_kv_norm, m_mla_w_uq=m_mla_w_uq, m_mla_w_ukv=m_mla_w_ukv, m_mla_w_o=m_mla_w_o, m_hgrn_w_in=m_hgrn_w_in, m_hgrn_lb_logits=m_hgrn_lb_logits, m_hgrn_o_norm=m_hgrn_o_norm, m_hgrn_w_o=m_hgrn_w_o, m_mlp_w1=m_mlp_w1, m_mlp_w2=m_mlp_w2, v_norm_gains=v_norm_gains, v_mla_w_in=v_mla_w_in, v_mla_q_norm=v_mla_q_norm, v_mla_kv_norm=v_mla_kv_norm, v_mla_w_uq=v_mla_w_uq, v_mla_w_ukv=v_mla_w_ukv, v_mla_w_o=v_mla_w_o, v_hgrn_w_in=v_hgrn_w_in, v_hgrn_lb_logits=v_hgrn_lb_logits, v_hgrn_o_norm=v_hgrn_o_norm, v_hgrn_w_o=v_hgrn_w_o, v_mlp_w1=v_mlp_w1, v_mlp_w2=v_mlp_w2)
    weights = {n: given[n] for n in TWIN_WEIGHTS}
    shared = {n: given[n] for n in SHARED_INPUTS}
    per_example = {n: given[n] for n in ['x', 'positions']}
    grad_fn = _jax.value_and_grad(_loss, argnums=(0, 1))

    def one_microbatch(ex, loss_target):
        ex = dict(ex)
        diff = ex.pop(TWIN_DIFF_INPUT)
        return grad_fn(weights, diff, {**shared, **ex}, loss_target)

    if N_MICROBATCH == 1:
        loss, (grad_w, grad_x) = one_microbatch(per_example, given["loss_target"])
    else:
        def body(carry, xs):
            loss_sum, grad_sum = carry
            l_k, (gw_k, gx_k) = one_microbatch(xs[0], xs[1])
            with _jax.named_scope("update"):
                return (loss_sum + l_k, _jax.tree.map(_jnp.add, grad_sum, gw_k)), gx_k

        init = (_jnp.zeros((), _jnp.float32), _jax.tree.map(_jnp.zeros_like, weights))
        (loss, grad_w), grad_x = _jax.lax.scan(body, init, (per_example, given["loss_target"]))
    with _jax.named_scope("update"):
        delta_w, new_m, new_v = {}, {}, {}
        for n in TWIN_WEIGHTS:
            delta_w[n], new_m[n], new_v[n] = _adamw(weights[n], grad_w[n], given["m_" + n], given["v_" + n])
    return (loss, grad_x, *[grad_w[n] for n in TWIN_WEIGHTS], *[delta_w[n] for n in TWIN_WEIGHTS],
            *[new_m[n] for n in TWIN_WEIGHTS], *[new_v[n] for n in TWIN_WEIGHTS])
```

```python
import functools

import jax
import jax.numpy as jnp
from jax import lax
from jax.experimental import pallas as pl
from jax.experimental.pallas import tpu as pltpu

F32 = jnp.float32
BF16 = jnp.bfloat16
MESH = pl.DeviceIdType.MESH

DEPTH = 4
MLA_HEADS = 8
MLA_NOPE = 128
MLA_ROPE = 64
MLA_V = 128
MLA_QK_PAD = 256
ROPE_BASE = 10000.0
HGRN_HEADS = 8
HGRN_CHUNK = 32
HGRN_BLOCK = 128
EPS = 1e-6

ADAM_LR = 0.001
ADAM_B1 = 0.9
ADAM_B2 = 0.999
ADAM_EPS = 1e-08
ADAM_WD = 0.01
ADAM_STEP = 10

N_CHIPS = 4
PACK_W = 1024
PACK_ALIGN = 2560
V7X_VMEM_LIMIT = 56 * 1024 * 1024

SHARDED = (("norm_gains", 2), ("mla_w_in", 1), ("mla_w_uq", 2), ("mla_w_ukv", 2), ("mla_w_o", 1),
           ("hgrn_w_in", 2), ("hgrn_w_o", 1), ("mlp_w1", 2), ("mlp_w2", 1))
REPLICATED = ("mla_q_norm", "mla_kv_norm", "hgrn_lb_logits", "hgrn_o_norm")
WEIGHTS = ("norm_gains", "mla_w_in", "mla_q_norm", "mla_kv_norm", "mla_w_uq", "mla_w_ukv", "mla_w_o",
           "hgrn_w_in", "hgrn_lb_logits", "hgrn_o_norm", "hgrn_w_o", "mlp_w1", "mlp_w2")


def _cparams(*semantics):
    return pltpu.CompilerParams(dimension_semantics=semantics, vmem_limit_bytes=V7X_VMEM_LIMIT)


def _sigmoid(x):
    return 1.0 / (1.0 + jnp.exp(-x))


def _mm(a, b, *, ta=False, tb=False, out_dtype=F32, tm=1024, tn=1024, tk=1024, epi=None, extra=None,
        name="mm"):
    if ta:
        K, M = a.shape
    else:
        M, K = a.shape
    if tb:
        N, Kb = b.shape
    else:
        Kb, N = b.shape
    assert K == Kb, (a.shape, b.shape, ta, tb)
    tm, tn = min(tm, M), min(tn, N)
    tk = K if K <= 1024 else tk
    assert M % tm == 0 and N % tn == 0 and K % tk == 0, (M, N, K, tm, tn, tk)
    nk = K // tk
    a_spec = (pl.BlockSpec((tk, tm), lambda i, j, k: (k, i)) if ta
              else pl.BlockSpec((tm, tk), lambda i, j, k: (i, k)))
    b_spec = (pl.BlockSpec((tn, tk), lambda i, j, k: (j, k)) if tb
              else pl.BlockSpec((tk, tn), lambda i, j, k: (k, j)))
    o_spec = pl.BlockSpec((tm, tn), lambda i, j, k: (i, j))
    dims = (((0 if ta else 1,), (1 if tb else 0,)), ((), ()))
    in_specs = [a_spec, b_spec]
    operands = [a, b]
    if epi == "mul2r":
        in_specs.append(o_spec)
        operands.append(extra)
    if epi == "relu2":
        out_shape = (jax.ShapeDtypeStruct((M, N), BF16), jax.ShapeDtypeStruct((M, N), BF16))
        out_specs = (o_spec, o_spec)
    elif epi == "mul2r":
        out_shape = jax.ShapeDtypeStruct((M, N), BF16)
        out_specs = o_spec
    else:
        out_shape = jax.ShapeDtypeStruct((M, N), out_dtype)
        out_specs = o_spec

    def body(*refs):
        a_ref, b_ref = refs[0], refs[1]
        acc_ref = refs[-1]
        k = pl.program_id(2)

        @pl.when(k == 0)
        def _():
            acc_ref[...] = jnp.zeros_like(acc_ref)

        acc_ref[...] += lax.dot_general(a_ref[...], b_ref[...], dims, preferred_element_type=F32)

        @pl.when(k == nk - 1)
        def _():
            acc = acc_ref[...]
            if epi == "relu2":
                r = jnp.maximum(acc, 0.0)
                refs[2][...] = (r * r).astype(BF16)
                refs[3][...] = r.astype(BF16)
            elif epi == "mul2r":
                refs[3][...] = (acc * (2.0 * refs[2][...].astype(F32))).astype(BF16)
            else:
                refs[2][...] = acc.astype(out_dtype)

    return pl.pallas_call(
        body, name=name, grid=(M // tm, N // tn, nk), in_specs=in_specs, out_specs=out_specs,
        out_shape=out_shape, scratch_shapes=[pltpu.VMEM((tm, tn), F32)],
        compiler_params=_cparams("parallel", "parallel", "arbitrary"))(*operands)


def _rms_rstd(x):
    return lax.rsqrt(jnp.mean(x * x, axis=-1, keepdims=True) + EPS)


def _rms_bwd_tile(x, g, dy):
    r = _rms_rstd(x)
    xh = x * r
    u = dy * g
    dx = r * (u - xh * jnp.mean(u * xh, axis=-1, keepdims=True))
    dg = jnp.sum(dy * xh, axis=0, keepdims=True)
    return dx, dg


def _row_tile(T):
    return min(256, T)


def _prenorm_fwd(x, g, name="prenorm_fwd"):
    T, D = x.shape
    tm = _row_tile(T)

    def body(x_ref, g_ref, a_ref):
        xv = x_ref[...]
        a_ref[...] = (xv * _rms_rstd(xv) * g_ref[...]).astype(BF16)

    row = pl.BlockSpec((tm, D), lambda i: (i, 0))
    vec = pl.BlockSpec((1, D), lambda i: (0, 0))
    return pl.pallas_call(body, name=name, grid=(T // tm,), in_specs=[row, vec], out_specs=row,
                          out_shape=jax.ShapeDtypeStruct((T, D), BF16),
                          compiler_params=_cparams("parallel"))(x, g)


def _resnorm_fwd(h, z, g_post, g_pre, name="resnorm_fwd"):
    T, D = h.shape
    tm = _row_tile(T)

    def body(h_ref, z_ref, gp_ref, gn_ref, hn_ref, a_ref):
        zv = z_ref[...]
        hn = h_ref[...] + zv * _rms_rstd(zv) * gp_ref[...]
        hn_ref[...] = hn
        a_ref[...] = (hn * _rms_rstd(hn) * gn_ref[...]).astype(BF16)

    row = pl.BlockSpec((tm, D), lambda i: (i, 0))
    vec = pl.BlockSpec((1, D), lambda i: (0, 0))
    return pl.pallas_call(body, name=name, grid=(T // tm,), in_specs=[row, row, vec, vec],
                          out_specs=(row, row),
                          out_shape=(jax.ShapeDtypeStruct((T, D), F32), jax.ShapeDtypeStruct((T, D), BF16)),
                          compiler_params=_cparams("parallel"))(h, z, g_post, g_pre)


def _resnorm_loss(h, z, g_post, target, name="resnorm_loss"):
    T, D = h.shape
    tm = _row_tile(T)

    def body(h_ref, z_ref, gp_ref, t_ref, dy_ref, sq_ref):
        zv = z_ref[...]
        err = h_ref[...] + zv * _rms_rstd(zv) * gp_ref[...] - t_ref[...]
        dy_ref[...] = err * (1.0 / D)

        @pl.when(pl.program_id(0) == 0)
        def _():
            sq_ref[...] = jnp.zeros_like(sq_ref)

        sq_ref[...] += jnp.sum(err * err, axis=0, keepdims=True)

    row = pl.BlockSpec((tm, D), lambda i: (i, 0))
    vec = pl.BlockSpec((1, D), lambda i: (0, 0))
    return pl.pallas_call(body, name=name, grid=(T // tm,), in_specs=[row, row, vec, row],
                          out_specs=(row, vec),
                          out_shape=(jax.ShapeDtypeStruct((T, D), F32), jax.ShapeDtypeStruct((1, D), F32)),
                          compiler_params=_cparams("arbitrary"))(h, z, g_post, target)


def _resnorm_bwd(z, g_post, dh, h_new=None, da=None, g_pre=None, name="resnorm_bwd"):
    T, D = z.shape
    tm = _row_tile(T)
    has_next = h_new is not None
    row = pl.BlockSpec((tm, D), lambda i: (i, 0))
    vec = pl.BlockSpec((1, D), lambda i: (0, 0))

    if has_next:
        def body(z_ref, gp_ref, dh_ref, hn_ref, da_ref, gn_ref, t_ref, dz_ref, dgp_ref, dgn_ref):
            first = pl.program_id(0) == 0

            @pl.when(first)
            def _():
                dgp_ref[...] = jnp.zeros_like(dgp_ref)
                dgn_ref[...] = jnp.zeros_like(dgn_ref)

            dpre, dgn = _rms_bwd_tile(hn_ref[...], gn_ref[...], da_ref[...])
            t = dh_ref[...] + dpre
            t_ref[...] = t
            dz, dgp = _rms_bwd_tile(z_ref[...], gp_ref[...], t)
            dz_ref[...] = dz.astype(BF16)
            dgp_ref[...] += dgp
            dgn_ref[...] += dgn

        return pl.pallas_call(
            body, name=name, grid=(T // tm,), in_specs=[row, vec, row, row, row, vec],
            out_specs=(row, row, vec, vec),
            out_shape=(jax.ShapeDtypeStruct((T, D), F32), jax.ShapeDtypeStruct((T, D), BF16),
                       jax.ShapeDtypeStruct((1, D), F32), jax.ShapeDtypeStruct((1, D), F32)),
            compiler_params=_cparams("arbitrary"))(z, g_post, dh, h_new, da, g_pre)

    def body_last(z_ref, gp_ref, dh_ref, dz_ref, dgp_ref):
        @pl.when(pl.program_id(0) == 0)
        def _():
            dgp_ref[...] = jnp.zeros_like(dgp_ref)

        dz, dgp = _rms_bwd_tile(z_ref[...], gp_ref[...], dh_ref[...])
        dz_ref[...] = dz.astype(BF16)
        dgp_ref[...] += dgp

    return pl.pallas_call(
        body_last, name=name, grid=(T // tm,), in_specs=[row, vec, row], out_specs=(row, vec),
        out_shape=(jax.ShapeDtypeStruct((T, D), BF16), jax.ShapeDtypeStruct((1, D), F32)),
        compiler_params=_cparams("arbitrary"))(z, g_post, dh)


def _prenorm_bwd(x, g, dh, da, name="prenorm_bwd"):
    T, D = x.shape
    tm = _row_tile(T)

    def body(x_ref, g_ref, dh_ref, da_ref, dx_ref, dg_ref):
        @pl.when(pl.program_id(0) == 0)
        def _():
            dg_ref[...] = jnp.zeros_like(dg_ref)

        dpre, dg = _rms_bwd_tile(x_ref[...], g_ref[...], da_ref[...])
        dx_ref[...] = dh_ref[...] + dpre
        dg_ref[...] += dg

    row = pl.BlockSpec((tm, D), lambda i: (i, 0))
    vec = pl.BlockSpec((1, D), lambda i: (0, 0))
    return pl.pallas_call(
        body, name=name, grid=(T // tm,), in_specs=[row, vec, row, row], out_specs=(row, vec),
        out_shape=(jax.ShapeDtypeStruct((T, D), F32), jax.ShapeDtypeStruct((1, D), F32)),
        compiler_params=_cparams("arbitrary"))(x, g, dh, da)


def _mlp_fwd(a, w1, w2):
    act, r = _mm(a, w1, epi="relu2", name="mlp_up")
    u = _mm(act, w2, name="mlp_down")
    return u, (a, act, r)


def _mlp_bwd(du, saved, w1, w2):
    a, act, r = saved
    dz1 = _mm(du, w2, tb=True, epi="mul2r", extra=r, name="mlp_down_dx")
    dw2 = _mm(act, du, ta=True, out_dtype=BF16, name="mlp_down_dw")
    dw1 = _mm(a, dz1, ta=True, out_dtype=BF16, name="mlp_up_dw")
    da = _mm(dz1, w1, tb=True, name="mlp_up_dx")
    return da, dw1, dw2


def _rope_swap(t):
    n = t.shape[-1]
    lane = lax.broadcasted_iota(jnp.int32, t.shape, t.ndim - 1)
    half = MLA_ROPE // 2
    first = (lane & (MLA_ROPE - 1)) < half
    return jnp.where(first, pltpu.roll(t, n - half, t.ndim - 1), pltpu.roll(t, half, t.ndim - 1))


def _mla_mid_fwd(proj, q_norm, kv_norm, w_uq, w_ukv, cc, ss):
    T, PW = proj.shape
    QL, KVL = q_norm.shape[-1], kv_norm.shape[-1]
    H = MLA_HEADS
    assert PW == QL + KVL + 128
    tm = _row_tile(T)

    def body(p_ref, qn_ref, kn_ref, wq_ref, wkv_ref, cc_ref, ss_ref,
             cq_ref, ckv_ref, q_ref, k_ref, v_ref):
        cq = p_ref[:, 0:QL]
        ckv = p_ref[:, QL:QL + KVL]
        kr = p_ref[:, QL + KVL:QL + KVL + 128]
        c, s = cc_ref[...], ss_ref[...]
        cqn = (cq * _rms_rstd(cq) * qn_ref[...]).astype(BF16)
        ckvn = (ckv * _rms_rstd(ckv) * kn_ref[...]).astype(BF16)
        cq_ref[...] = cqn
        ckv_ref[...] = ckvn
        q = jnp.dot(cqn, wq_ref[...], preferred_element_type=F32)
        kv = jnp.dot(ckvn, wkv_ref[...], preferred_element_type=F32)
        krf = (kr * c + _rope_swap(kr) * s).astype(BF16)
        for h in range(H):
            o = h * MLA_QK_PAD
            q_ref[:, o:o + MLA_NOPE] = q[:, o:o + MLA_NOPE].astype(BF16)
            qr = q[:, o + MLA_NOPE:o + MLA_QK_PAD]
            q_ref[:, o + MLA_NOPE:o + MLA_QK_PAD] = (qr * c + _rope_swap(qr) * s).astype(BF16)
            k_ref[:, o:o + MLA_NOPE] = kv[:, o:o + MLA_NOPE].astype(BF16)
            k_ref[:, o + MLA_NOPE:o + MLA_QK_PAD] = krf
            v_ref[:, h * MLA_V:(h + 1) * MLA_V] = kv[:, o + MLA_NOPE:o + MLA_QK_PAD].astype(BF16)

    def row(w):
        return pl.BlockSpec((tm, w), lambda i: (i, 0))

    def full(shape):
        return pl.BlockSpec(shape, lambda i: (0, 0))

    return pl.pallas_call(
        body, name="mla_mid_fwd", grid=(T // tm,),
        in_specs=[row(PW), full((1, QL)), full((1, KVL)), full(w_uq.shape), full(w_ukv.shape),
                  row(128), row(128)],
        out_specs=(row(QL), row(KVL), row(H * MLA_QK_PAD), row(H * MLA_QK_PAD), row(H * MLA_V)),
        out_shape=(jax.ShapeDtypeStruct((T, QL), BF16), jax.ShapeDtypeStruct((T, KVL), BF16),
                   jax.ShapeDtypeStruct((T, H * MLA_QK_PAD), BF16),
                   jax.ShapeDtypeStruct((T, H * MLA_QK_PAD), BF16),
                   jax.ShapeDtypeStruct((T, H * MLA_V), BF16)),
        compiler_params=_cparams("parallel"))(proj, q_norm, kv_norm, w_uq, w_ukv, cc, ss)


def _mla_mid_bwd(proj, q_norm, kv_norm, w_uq, w_ukv, cc, ss, dq, dk, dv):
    T, PW = proj.shape
    QL, KVL = q_norm.shape[-1], kv_norm.shape[-1]
    H = MLA_HEADS
    tm = _row_tile(T)
    nt = (((1,), (1,)), ((), ()))

    def body(p_ref, qn_ref, kn_ref, wq_ref, wkv_ref, cc_ref, ss_ref, dq_ref, dk_ref, dv_ref,
             dqp_ref, dkv_ref, dp_ref, dqn_ref, dkn_ref):
        @pl.when(pl.program_id(0) == 0)
        def _():
            dqn_ref[...] = jnp.zeros_like(dqn_ref)
            dkn_ref[...] = jnp.zeros_like(dkn_ref)

        c, s = cc_ref[...], ss_ref[...]
        dkr = jnp.zeros((tm, 128), F32)
        for h in range(H):
            o = h * MLA_QK_PAD
            dqp_ref[:, o:o + MLA_NOPE] = dq_ref[:, o:o + MLA_NOPE].astype(BF16)
            dqr = dq_ref[:, o + MLA_NOPE:o + MLA_QK_PAD]
            dqp_ref[:, o + MLA_NOPE:o + MLA_QK_PAD] = (dqr * c + _rope_swap(dqr * s)).astype(BF16)
            dkv_ref[:, o:o + MLA_NOPE] = dk_ref[:, o:o + MLA_NOPE].astype(BF16)
            dkv_ref[:, o + MLA_NOPE:o + MLA_QK_PAD] = dv_ref[:, h * MLA_V:(h + 1) * MLA_V].astype(BF16)
            dkr = dkr + dk_ref[:, o + MLA_NOPE:o + MLA_QK_PAD]
        dcqn = lax.dot_general(dqp_ref[...], wq_ref[...], nt, preferred_element_type=F32)
        dckvn = lax.dot_general(dkv_ref[...], wkv_ref[...], nt, preferred_element_type=F32)
        dcq, dqn = _rms_bwd_tile(p_ref[:, 0:QL], qn_ref[...], dcqn)
        dckv, dkn = _rms_bwd_tile(p_ref[:, QL:QL + KVL], kn_ref[...], dckvn)
        dp_ref[:, 0:QL] = dcq.astype(BF16)
        dp_ref[:, QL:QL + KVL] = dckv.astype(BF16)
        dp_ref[:, QL + KVL:QL + KVL + 128] = (dkr * c + _rope_swap(dkr * s)).astype(BF16)
        dqn_ref[...] += dqn
        dkn_ref[...] += dkn

    def row(w):
        return pl.BlockSpec((tm, w), lambda i: (i, 0))

    def full(shape):
        return pl.BlockSpec(shape, lambda i: (0, 0))

    return pl.pallas_call(
        body, name="mla_mid_bwd", grid=(T // tm,),
        in_specs=[row(PW), full((1, QL)), full((1, KVL)), full(w_uq.shape), full(w_ukv.shape),
                  row(128), row(128), row(H * MLA_QK_PAD), row(H * MLA_QK_PAD), row(H * MLA_V)],
        out_specs=(row(H * MLA_QK_PAD), row(H * MLA_QK_PAD), row(PW), full((1, QL)), full((1, KVL))),
        out_shape=(jax.ShapeDtypeStruct((T, H * MLA_QK_PAD), BF16),
                   jax.ShapeDtypeStruct((T, H * MLA_QK_PAD), BF16),
                   jax.ShapeDtypeStruct((T, PW), BF16),
                   jax.ShapeDtypeStruct((1, QL), F32), jax.ShapeDtypeStruct((1, KVL), F32)),
        compiler_params=_cparams("arbitrary"))(proj, q_norm, kv_norm, w_uq, w_ukv, cc, ss, dq, dk, dv)


def _attn_tile(T):
    return min(512, T)


def _causal_scores(q, k, qi, ki, tq, tk, scale):
    s = lax.dot_general(q, k, (((1,), (1,)), ((), ())), preferred_element_type=F32) * scale
    rows = qi * tq + lax.broadcasted_iota(jnp.int32, (tq, tk), 0)
    cols = ki * tk + lax.broadcasted_iota(jnp.int32, (tq, tk), 1)
    return jnp.where(rows >= cols, s, -jnp.inf)


def _attn_fwd(q, k, v):
    T = q.shape[0]
    H, DQ, DV = MLA_HEADS, MLA_QK_PAD, MLA_V
    tq = tk = _attn_tile(T)
    nq = T // tq
    scale = float(MLA_NOPE + MLA_ROPE) ** -0.5

    def body(q_ref, k_ref, v_ref, o_ref, lse_ref, m_ref, l_ref, acc_ref):
        qi, ki = pl.program_id(1), pl.program_id(2)

        @pl.when(ki == 0)
        def _():
            m_ref[...] = jnp.full_like(m_ref, -jnp.inf)
            l_ref[...] = jnp.zeros_like(l_ref)
            acc_ref[...] = jnp.zeros_like(acc_ref)

        @pl.when(ki <= qi)
        def _():
            s = _causal_scores(q_ref[...], k_ref[...], qi, ki, tq, tk, scale)
            m_prev = m_ref[...]
            m_new = jnp.maximum(m_prev, jnp.max(s, axis=1, keepdims=True))
            alpha = jnp.exp(m_prev - m_new)
            p = jnp.exp(s - m_new)
            l_ref[...] = alpha * l_ref[...] + jnp.sum(p, axis=1, keepdims=True)
            acc_ref[...] = alpha * acc_ref[...] + jnp.dot(p.astype(BF16), v_ref[...],
                                                          preferred_element_type=F32)
            m_ref[...] = m_new

        @pl.when(ki == qi)
        def _():
            o_ref[...] = (acc_ref[...] / l_ref[...]).astype(BF16)
            lse_ref[0] = m_ref[...] + jnp.log(l_ref[...])

    return pl.pallas_call(
        body, name="attn_fwd", grid=(H, nq, nq),
        in_specs=[pl.BlockSpec((tq, DQ), lambda h, i, j: (i, h)),
                  pl.BlockSpec((tk, DQ), lambda h, i, j: (jnp.minimum(i, j), h)),
                  pl.BlockSpec((tk, DV), lambda h, i, j: (jnp.minimum(i, j), h))],
        out_specs=(pl.BlockSpec((tq, DV), lambda h, i, j: (i, h)),
                   pl.BlockSpec((1, tq, 1), lambda h, i, j: (h, i, 0))),
        out_shape=(jax.ShapeDtypeStruct((T, H * DV), BF16), jax.ShapeDtypeStruct((H, T, 1), F32)),
        scratch_shapes=[pltpu.VMEM((tq, 1), F32), pltpu.VMEM((tq, 1), F32), pltpu.VMEM((tq, DV), F32)],
        compiler_params=_cparams("parallel", "parallel", "arbitrary"))(q, k, v)


def _attn_bwd_dq(q, k, v, o, do, lse):
    T = q.shape[0]
    H, DQ, DV = MLA_HEADS, MLA_QK_PAD, MLA_V
    tq = tk = _attn_tile(T)
    nq = T // tq
    scale = float(MLA_NOPE + MLA_ROPE) ** -0.5

    def body(q_ref, k_ref, v_ref, o_ref, do_ref, lse_ref, dq_ref, acc_ref, delta_ref):
        qi, ki = pl.program_id(1), pl.program_id(2)

        @pl.when(ki == 0)
        def _():
            acc_ref[...] = jnp.zeros_like(acc_ref)
            delta_ref[...] = jnp.sum(do_ref[...].astype(F32) * o_ref[...].astype(F32), axis=1,
                                     keepdims=True)

        @pl.when(ki <= qi)
        def _():
            s = _causal_scores(q_ref[...], k_ref[...], qi, ki, tq, tk, scale)
            p = jnp.exp(s - lse_ref[0])
            dp = lax.dot_general(do_ref[...], v_ref[...], (((1,), (1,)), ((), ())),
                                 preferred_element_type=F32)
            ds = (p * (dp - delta_ref[...]) * scale).astype(BF16)
            acc_ref[...] += jnp.dot(ds, k_ref[...], preferred_element_type=F32)

        @pl.when(ki == qi)
        def _():
            dq_ref[...] = acc_ref[...]

    qspec = pl.BlockSpec((tq, DQ), lambda h, i, j: (i, h))
    ospec = pl.BlockSpec((tq, DV), lambda h, i, j: (i, h))
    return pl.pallas_call(
        body, name="attn_bwd_dq", grid=(H, nq, nq),
        in_specs=[qspec,
                  pl.BlockSpec((tk, DQ), lambda h, i, j: (jnp.minimum(i, j), h)),
                  pl.BlockSpec((tk, DV), lambda h, i, j: (jnp.minimum(i, j), h)),
                  ospec, ospec,
                  pl.BlockSpec((1, tq, 1), lambda h, i, j: (h, i, 0))],
        out_specs=qspec,
        out_shape=jax.ShapeDtypeStruct((T, H * DQ), F32),
        scratch_shapes=[pltpu.VMEM((tq, DQ), F32), pltpu.VMEM((tq, 1), F32)],
        compiler_params=_cparams("parallel", "parallel", "arbitrary"))(q, k, v, o, do, lse)


def _attn_bwd_dkv(q, k, v, o, do, lse):
    T = q.shape[0]
    H, DQ, DV = MLA_HEADS, MLA_QK_PAD, MLA_V
    tq = tk = _attn_tile(T)
    nq = T // tq
    scale = float(MLA_NOPE + MLA_ROPE) ** -0.5
    tn = (((0,), (0,)), ((), ()))

    def body(q_ref, k_ref, v_ref, o_ref, do_ref, lse_ref, dk_ref, dv_ref, dk_acc, dv_acc):
        ki, qi = pl.program_id(1), pl.program_id(2)

        @pl.when(qi == 0)
        def _():
            dk_acc[...] = jnp.zeros_like(dk_acc)
            dv_acc[...] = jnp.zeros_like(dv_acc)

        @pl.when(qi >= ki)
        def _():
            dof = do_ref[...]
            delta = jnp.sum(dof.astype(F32) * o_ref[...].astype(F32), axis=1, keepdims=True)
            s = _causal_scores(q_ref[...], k_ref[...], qi, ki, tq, tk, scale)
            p = jnp.exp(s - lse_ref[0])
            dp = lax.dot_general(dof, v_ref[...], (((1,), (1,)), ((), ())), preferred_element_type=F32)
            ds = (p * (dp - delta) * scale).astype(BF16)
            dv_acc[...] += lax.dot_general(p.astype(BF16), dof, tn, preferred_element_type=F32)
            dk_acc[...] += lax.dot_general(ds, q_ref[...], tn, preferred_element_type=F32)

        @pl.when(qi == nq - 1)
        def _():
            dk_ref[...] = dk_acc[...]
            dv_ref[...] = dv_acc[...]

    kspec = pl.BlockSpec((tk, DQ), lambda h, j, i: (j, h))
    vspec = pl.BlockSpec((tk, DV), lambda h, j, i: (j, h))
    ospec = pl.BlockSpec((tq, DV), lambda h, j, i: (jnp.maximum(i, j), h))
    return pl.pallas_call(
        body, name="attn_bwd_dkv", grid=(H, nq, nq),
        in_specs=[pl.BlockSpec((tq, DQ), lambda h, j, i: (jnp.maximum(i, j), h)),
                  kspec, vspec, ospec, ospec,
                  pl.BlockSpec((1, tq, 1), lambda h, j, i: (h, jnp.maximum(i, j), 0))],
        out_specs=(kspec, vspec),
        out_shape=(jax.ShapeDtypeStruct((T, H * DQ), F32), jax.ShapeDtypeStruct((T, H * DV), F32)),
        scratch_shapes=[pltpu.VMEM((tk, DQ), F32), pltpu.VMEM((tk, DV), F32)],
        compiler_params=_cparams("parallel", "parallel", "arbitrary"))(q, k, v, o, do, lse)


def _mla_fwd(a, w, cc, ss):
    proj = _mm(a, w["w_in"], name="mla_in")
    cqn, ckvn, q, k, v = _mla_mid_fwd(proj, w["q_norm"], w["kv_norm"], w["w_uq"], w["w_ukv"], cc, ss)
    o, lse = _attn_fwd(q, k, v)
    m = _mm(o, w["w_o"], name="mla_out")
    return m, (a, proj, cqn, ckvn, q, k, v, o, lse)


def _mla_bwd(dm, saved, w, cc, ss):
    a, proj, cqn, ckvn, q, k, v, o, lse = saved
    do = _mm(dm, w["w_o"], tb=True, out_dtype=BF16, name="mla_out_dx")
    dw_o = _mm(o, dm, ta=True, out_dtype=BF16, name="mla_out_dw")
    dq = _attn_bwd_dq(q, k, v, o, do, lse)
    dk, dv = _attn_bwd_dkv(q, k, v, o, do, lse)
    dqp, dkv, dproj, dqn, dkn = _mla_mid_bwd(proj, w["q_norm"], w["kv_norm"], w["w_uq"], w["w_ukv"],
                                             cc, ss, dq, dk, dv)
    dw_uq = _mm(cqn, dqp, ta=True, out_dtype=BF16, name="mla_uq_dw")
    dw_ukv = _mm(ckvn, dkv, ta=True, out_dtype=BF16, name="mla_ukv_dw")
    dw_in = _mm(a, dproj, ta=True, out_dtype=BF16, name="mla_in_dw")
    da = _mm(dproj, w["w_in"], tb=True, name="mla_in_dx")
    return da, dict(w_in=dw_in, w_uq=dw_uq, w_ukv=dw_ukv, w_o=dw_o, q_norm=dqn, kv_norm=dkn)


def _split_dot(mat, x, parts):
    acc = None
    rem = x
    for _ in range(parts):
        piece = rem.astype(BF16)
        term = jnp.dot(mat, piece, preferred_element_type=F32)
        acc = term if acc is None else acc + term
        rem = rem - piece.astype(F32)
    return acc


def _chunk_mats(tb):
    C = HGRN_CHUNK
    assert C & (C - 1) == 0
    r = lax.broadcasted_iota(jnp.int32, (tb, tb), 0)
    s = lax.broadcasted_iota(jnp.int32, (tb, tb), 1)
    start = r & ~(C - 1)
    same = start == (s & ~(C - 1))
    ref = start + C // 2
    last = start + C - 1
    one, zero = jnp.float32(1.0), jnp.float32(0.0)
    cum = jnp.where(same & (s <= r), one, zero)
    rel = cum - jnp.where(same & (s <= ref), one, zero)
    rest = jnp.where(same & (s > r) & (s <= last), one, zero)
    rev = jnp.where(same & (s >= r), one, zero)
    ones = jnp.where(same, one, zero)
    causal = same & (s <= r)
    return cum, rel, rest, rev, ones, causal


def _hgrn_gates(p_ref, lb, HK):
    qx = p_ref[:, 0:HK]
    fx = p_ref[:, HK:2 * HK]
    sf = _sigmoid(fx)
    f = lb + (1.0 - lb) * sf
    sq = _sigmoid(qx)
    return qx, sq, qx * sq, sf, f, 1.0 - f, jnp.log(f)


def _hgrn_fwd(proj, lb, o_norm):
    T = proj.shape[0]
    H, C = HGRN_HEADS, HGRN_CHUNK
    HK = proj.shape[1] // 4
    DK = HK // H
    tb = min(HGRN_BLOCK, T)
    ncb = tb // C
    nt = (((1,), (1,)), ((), ()))
    tn = (((0,), (0,)), ((), ()))

    def body(p_ref, lb_ref, on_ref, y_ref, o_ref, st_ref, state, oacc):
        @pl.when(pl.program_id(0) == 0)
        def _():
            state[...] = jnp.zeros_like(state)

        cum, rel, rest, _, _, causal = _chunk_mats(tb)
        _, _, q, _, f, k, logf = _hgrn_gates(p_ref, lb_ref[...], HK)
        b = _split_dot(cum.astype(BF16), logf, 3)
        brel = _split_dot(rel.astype(BF16), logf, 3)
        brest = _split_dot(rest.astype(BF16), logf, 3)
        eb = jnp.exp(b)
        q_rel = (q * jnp.exp(brel)).astype(BF16)
        k_rel = (k * jnp.exp(-brel)).astype(BF16)
        q_dec = (q * eb).astype(BF16)
        k_dec = (k * jnp.exp(brest)).astype(BF16)
        v = p_ref[:, 2 * HK:3 * HK].astype(BF16)
        for h in range(H):
            hs = slice(h * DK, (h + 1) * DK)
            a = lax.dot_general(q_rel[:, hs], k_rel[:, hs], nt, preferred_element_type=F32)
            a = jnp.where(causal, a, 0.0).astype(BF16)
            oacc[:, hs] = jnp.dot(a, v[:, hs], preferred_element_type=F32)
            for j in range(ncb):
                rs = slice(j * C, (j + 1) * C)
                st = state[h]
                st_ref[j, h] = st
                oacc[rs, hs] += lax.dot_general(q_dec[rs, hs], st.astype(BF16), nt,
                                                preferred_element_type=F32)
                dec = jnp.exp(jnp.sum(logf[rs, hs], axis=0, keepdims=True))
                state[h] = dec * st + lax.dot_general(v[rs, hs], k_dec[rs, hs], tn,
                                                      preferred_element_type=F32)
        o = oacc[...]
        o_ref[...] = o
        gx = p_ref[:, 3 * HK:4 * HK]
        gate = gx * _sigmoid(gx)
        for h in range(H):
            hs = slice(h * DK, (h + 1) * DK)
            oh = o[:, hs]
            y_ref[:, hs] = (oh * _rms_rstd(oh) * on_ref[...] * gate[:, hs]).astype(BF16)

    return pl.pallas_call(
        body, name="hgrn_fwd", grid=(T // tb,),
        in_specs=[pl.BlockSpec((tb, 4 * HK), lambda i: (i, 0)),
                  pl.BlockSpec((1, HK), lambda i: (0, 0)),
                  pl.BlockSpec((1, DK), lambda i: (0, 0))],
        out_specs=(pl.BlockSpec((tb, HK), lambda i: (i, 0)),
                   pl.BlockSpec((tb, HK), lambda i: (i, 0)),
                   pl.BlockSpec((ncb, H, DK, DK), lambda i: (i, 0, 0, 0))),
        out_shape=(jax.ShapeDtypeStruct((T, HK), BF16), jax.ShapeDtypeStruct((T, HK), F32),
                   jax.ShapeDtypeStruct((T // C, H, DK, DK), F32)),
        scratch_shapes=[pltpu.VMEM((H, DK, DK), F32), pltpu.VMEM((tb, HK), F32)],
        compiler_params=_cparams("arbitrary"))(proj, lb, o_norm)


def _hgrn_bwd(proj, lb, o_norm, o, states, dy):
    T = proj.shape[0]
    H, C = HGRN_HEADS, HGRN_CHUNK
    HK = proj.shape[1] // 4
    DK = HK // H
    tb = min(HGRN_BLOCK, T)
    ncb = tb // C
    nb = T // tb
    nt = (((1,), (1,)), ((), ()))
    tn = (((0,), (0,)), ((), ()))

    def body(p_ref, lb_ref, on_ref, o_ref, st_ref, dy_ref, dp_ref, dlb_ref, don_ref,
             dstate, dqr_s, dkr_s, dqd_s, dkd_s, dv_s, do_s, e_s):
        @pl.when(pl.program_id(0) == 0)
        def _():
            dstate[...] = jnp.zeros_like(dstate)
            dlb_ref[...] = jnp.zeros_like(dlb_ref)
            don_ref[...] = jnp.zeros_like(don_ref)

        cum, rel, rest, rev, ones, causal = _chunk_mats(tb)
        lb = lb_ref[...]
        qx, sq, q, sf, f, k, logf = _hgrn_gates(p_ref, lb, HK)
        b = _split_dot(cum.astype(BF16), logf, 3)
        brel = _split_dot(rel.astype(BF16), logf, 3)
        brest = _split_dot(rest.astype(BF16), logf, 3)
        eb = jnp.exp(b)
        erel = jnp.exp(brel)
        enrel = jnp.exp(-brel)
        erest = jnp.exp(brest)
        q_rel_f, k_rel_f, q_dec_f, k_dec_f = q * erel, k * enrel, q * eb, k * erest
        q_rel, k_rel = q_rel_f.astype(BF16), k_rel_f.astype(BF16)
        q_dec, k_dec = q_dec_f.astype(BF16), k_dec_f.astype(BF16)
        v = p_ref[:, 2 * HK:3 * HK].astype(BF16)

        gx = p_ref[:, 3 * HK:4 * HK]
        sg = _sigmoid(gx)
        gate = gx * sg
        dy = dy_ref[...]
        ov = o_ref[...]
        on = on_ref[...]
        don = jnp.zeros((1, DK), F32)
        for h in range(H):
            hs = slice(h * DK, (h + 1) * DK)
            oh = ov[:, hs]
            r = _rms_rstd(oh)
            xh = oh * r
            d_on = dy[:, hs] * gate[:, hs]
            don = don + jnp.sum(d_on * xh, axis=0, keepdims=True)
            u = d_on * on
            do_s[:, hs] = r * (u - xh * jnp.mean(u * xh, axis=-1, keepdims=True))
            dp_ref[:, 3 * HK + h * DK:3 * HK + (h + 1) * DK] = (
                dy[:, hs] * xh * on * (sg[:, hs] * (1.0 + gx[:, hs] * (1.0 - sg[:, hs])))).astype(BF16)
        don_ref[...] += don

        for h in range(H):
            hs = slice(h * DK, (h + 1) * DK)
            doh = do_s[:, hs].astype(BF16)
            a = lax.dot_general(q_rel[:, hs], k_rel[:, hs], nt, preferred_element_type=F32)
            a = jnp.where(causal, a, 0.0).astype(BF16)
            da = lax.dot_general(doh, v[:, hs], nt, preferred_element_type=F32)
            da = jnp.where(causal, da, 0.0).astype(BF16)
            dv_s[:, hs] = lax.dot_general(a, doh, tn, preferred_element_type=F32)
            dqr_s[:, hs] = jnp.dot(da, k_rel[:, hs], preferred_element_type=F32)
            dkr_s[:, hs] = lax.dot_general(da, q_rel[:, hs], tn, preferred_element_type=F32)
            for j in reversed(range(ncb)):
                rs = slice(j * C, (j + 1) * C)
                dst = dstate[h]
                dstb = dst.astype(BF16)
                st = st_ref[j, h]
                dkd_s[rs, hs] = jnp.dot(v[rs, hs], dstb, preferred_element_type=F32)
                dv_s[rs, hs] += lax.dot_general(k_dec[rs, hs], dstb, nt, preferred_element_type=F32)
                dec = jnp.exp(jnp.sum(logf[rs, hs], axis=0, keepdims=True))
                e_s[rs, hs] = jnp.broadcast_to(jnp.sum(dst * st, axis=0, keepdims=True) * dec, (C, DK))
                dqd_s[rs, hs] = jnp.dot(doh[rs], st.astype(BF16), preferred_element_type=F32)
                dstate[h] = dec * dst + lax.dot_general(doh[rs], q_dec[rs, hs], tn,
                                                        preferred_element_type=F32)

        dqr, dkr, dqd, dkd = dqr_s[...], dkr_s[...], dqd_s[...], dkd_s[...]
        kdk = dkd * k_dec_f
        db = dqr * q_rel_f - dkr * k_rel_f + dqd * q_dec_f - kdk
        dlogf = _split_dot(rev.astype(BF16), db, 2) + _split_dot(ones.astype(BF16), kdk, 2) + e_s[...]
        dk = dkr * enrel + dkd * erest
        df = dlogf / f - dk
        dlb_ref[...] += jnp.sum(df * (1.0 - sf), axis=0, keepdims=True)
        dq = dqr * erel + dqd * eb
        dp_ref[:, 0:HK] = (dq * (sq * (1.0 + qx * (1.0 - sq)))).astype(BF16)
        dp_ref[:, HK:2 * HK] = (df * (1.0 - lb) * sf * (1.0 - sf)).astype(BF16)
        dp_ref[:, 2 * HK:3 * HK] = dv_s[...].astype(BF16)

    rev_row = lambda w: pl.BlockSpec((tb, w), lambda i: (nb - 1 - i, 0))
    vec = lambda w: pl.BlockSpec((1, w), lambda i: (0, 0))
    scr = pltpu.VMEM((tb, HK), F32)
    return pl.pallas_call(
        body, name="hgrn_bwd", grid=(nb,),
        in_specs=[rev_row(4 * HK), vec(HK), vec(DK), rev_row(HK),
                  pl.BlockSpec((ncb, H, DK, DK), lambda i: (nb - 1 - i, 0, 0, 0)), rev_row(HK)],
        out_specs=(rev_row(4 * HK), vec(HK), vec(DK)),
        out_shape=(jax.ShapeDtypeStruct((T, 4 * HK), BF16), jax.ShapeDtypeStruct((1, HK), F32),
                   jax.ShapeDtypeStruct((1, DK), F32)),
        scratch_shapes=[pltpu.VMEM((H, DK, DK), F32), scr, scr, scr, scr, scr, scr, scr],
        compiler_params=_cparams("arbitrary"))(proj, lb, o_norm, o, states, dy)


def _hgrn_layer_fwd(a, w, lb):
    proj = _mm(a, w["w_in"], name="hgrn_in")
    y, o, states = _hgrn_fwd(proj, lb, w["o_norm"])
    m = _mm(y, w["w_o"], name="hgrn_out")
    return m, (a, proj, y, o, states)


def _hgrn_layer_bwd(dm, saved, w, lb):
    a, proj, y, o, states = saved
    dy = _mm(dm, w["w_o"], tb=True, name="hgrn_out_dx")
    dw_o = _mm(y, dm, ta=True, out_dtype=BF16, name="hgrn_out_dw")
    dproj, dlb, don = _hgrn_bwd(proj, lb, w["o_norm"], o, states, dy)
    dw_in = _mm(a, dproj, ta=True, out_dtype=BF16, name="hgrn_in_dw")
    da = _mm(dproj, w["w_in"], tb=True, name="hgrn_in_dx")
    return da, dict(w_in=dw_in, w_o=dw_o, o_norm=don, lb=dlb)


def _lower_bounds(lb_logits):
    p = jax.nn.softmax(lb_logits.astype(F32), axis=0)
    return jnp.cumsum(p, axis=0) - p[0]


def _rope_tables(positions):
    inv_freq = jnp.power(ROPE_BASE, -jnp.arange(0, MLA_ROPE, 2, dtype=F32) / MLA_ROPE)
    ang = positions.astype(F32)[:, None] * inv_freq
    cos, sin = jnp.cos(ang), jnp.sin(ang)
    zero = jnp.zeros((positions.shape[0], 128 - MLA_ROPE), F32)
    return (jnp.concatenate([cos, cos, zero], axis=-1), jnp.concatenate([-sin, sin, zero], axis=-1))


def _pad_mla_weights(w_in, w_uq):
    w_in_p = jnp.pad(w_in, ((0, 0), (0, 0), (0, 128 - MLA_ROPE)))
    n, ql, _ = w_uq.shape
    w_uq_p = jnp.pad(w_uq.reshape(n, ql, MLA_HEADS, MLA_NOPE + MLA_ROPE),
                     ((0, 0), (0, 0), (0, 0), (0, MLA_QK_PAD - MLA_NOPE - MLA_ROPE)))
    return w_in_p, w_uq_p.reshape(n, ql, MLA_HEADS * MLA_QK_PAD)


def _local_step(x, positions, target, wb, small):
    T, D = x.shape
    gains = small["norm_gains"]
    lbounds, lb_vjp = jax.vjp(_lower_bounds, small["hgrn_lb_logits"])
    cc, ss = _rope_tables(positions)
    w_in_p, w_uq_p = _pad_mla_weights(wb["mla_w_in"], wb["mla_w_uq"])

    def g(layer, i):
        return gains[layer, i][None, :]

    def mixer_weights(layer):
        slot = layer // 2
        if layer % 2 == 0:
            return dict(w_in=w_in_p[slot], w_uq=w_uq_p[slot], w_ukv=wb["mla_w_ukv"][slot],
                        w_o=wb["mla_w_o"][slot], q_norm=small["mla_q_norm"][slot][None, :],
                        kv_norm=small["mla_kv_norm"][slot][None, :])
        return dict(w_in=wb["hgrn_w_in"][slot], w_o=wb["hgrn_w_o"][slot],
                    o_norm=small["hgrn_o_norm"][slot][None, :])

    saved = []
    h = x
    a = _prenorm_fwd(x, g(0, 0))
    dy = sq = None
    for layer in range(DEPTH):
        mw = mixer_weights(layer)
        if layer % 2 == 0:
            m, mix_saved = _mla_fwd(a, mw, cc, ss)
        else:
            m, mix_saved = _hgrn_layer_fwd(a, mw, lbounds[layer][None, :])
        h1, a2 = _resnorm_fwd(h, m, g(layer, 1), g(layer, 2), name="resnorm_fwd_mix")
        u, mlp_saved = _mlp_fwd(a2, wb["mlp_w1"][layer], wb["mlp_w2"][layer])
        if layer + 1 < DEPTH:
            h2, a = _resnorm_fwd(h1, u, g(layer, 3), g(layer + 1, 0), name="resnorm_fwd_mlp")
        else:
            h2 = None
            dy, sq = _resnorm_loss(h1, u, g(layer, 3), target)
        saved.append((h, m, h1, u, mix_saved, mlp_saved))
        h = h2

    n_mla, n_hgrn = (DEPTH + 1) // 2, DEPTH // 2
    dgains = [[None] * 4 for _ in range(DEPTH)]
    gw = {k: [None] * n_mla for k in ("mla_w_in", "mla_w_uq", "mla_w_ukv", "mla_w_o", "mla_q_norm", "mla_kv_norm")}
    gw.update({k: [None] * n_hgrn for k in ("hgrn_w_in", "hgrn_w_o", "hgrn_o_norm")})
    gw["mlp_w1"] = [None] * DEPTH
    gw["mlp_w2"] = [None] * DEPTH
    dlb = [jnp.zeros((1, lbounds.shape[1]), F32) for _ in range(DEPTH)]
    dh = dy
    da_next = None
    for layer in reversed(range(DEPTH)):
        h0, m, h1, u, mix_saved, mlp_saved = saved[layer]
        slot = layer // 2
        mw = mixer_weights(layer)
        if da_next is None:
            du, dgains[layer][3] = _resnorm_bwd(u, g(layer, 3), dh, name="resnorm_bwd_last")
            t = dh
        else:
            h2 = saved[layer + 1][0]
            t, du, dgains[layer][3], dgains[layer + 1][0] = _resnorm_bwd(
                u, g(layer, 3), dh, h2, da_next, g(layer + 1, 0), name="resnorm_bwd_mlp")
        da2, gw["mlp_w1"][layer], gw["mlp_w2"][layer] = _mlp_bwd(
            du, mlp_saved, wb["mlp_w1"][layer], wb["mlp_w2"][layer])
        t, dm, dgains[layer][1], dgains[layer][2] = _resnorm_bwd(
            m, g(layer, 1), t, h1, da2, g(layer, 2), name="resnorm_bwd_mix")
        if layer % 2 == 0:
            da_next, mg = _mla_bwd(dm, mix_saved, mw, cc, ss)
            ql = mg["q_norm"].shape[-1]
            kvl = mg["kv_norm"].shape[-1]
            gw["mla_w_in"][slot] = mg["w_in"][:, :ql + kvl + MLA_ROPE]
            gw["mla_w_uq"][slot] = mg["w_uq"].reshape(ql, MLA_HEADS, MLA_QK_PAD)[
                :, :, :MLA_NOPE + MLA_ROPE].reshape(ql, MLA_HEADS * (MLA_NOPE + MLA_ROPE))
            gw["mla_w_ukv"][slot] = mg["w_ukv"]
            gw["mla_w_o"][slot] = mg["w_o"]
            gw["mla_q_norm"][slot] = mg["q_norm"][0]
            gw["mla_kv_norm"][slot] = mg["kv_norm"][0]
        else:
            da_next, hg = _hgrn_layer_bwd(dm, mix_saved, mw, lbounds[layer][None, :])
            gw["hgrn_w_in"][slot] = hg["w_in"]
            gw["hgrn_w_o"][slot] = hg["w_o"]
            gw["hgrn_o_norm"][slot] = hg["o_norm"][0]
            dlb[layer] = hg["lb"]
        dh = t
    grad_x, dgains[0][0] = _prenorm_bwd(x, g(0, 0), dh, da_next)

    grads = {k: jnp.stack(vs) for k, vs in gw.items()}
    grads["norm_gains"] = jnp.stack([jnp.concatenate(row, axis=0) for row in dgains])
    (grads["hgrn_lb_logits"],) = lb_vjp(jnp.concatenate(dlb, axis=0))
    return sq, grad_x, grads


def _size(shape):
    n = 1
    for d in shape:
        n *= d
    return n


def _packed_rows(n_elems):
    rows = -(-n_elems // PACK_W)
    return -(-rows // PACK_ALIGN) * PACK_ALIGN


def _pack(pieces, rows, dtype):
    flat = jnp.concatenate([p.astype(dtype).reshape(-1) for p in pieces])
    flat = jnp.pad(flat, (0, rows * PACK_W - flat.shape[0]))
    return flat.reshape(rows, PACK_W)


def _unpack(buf, shapes):
    flat = buf.reshape(-1)
    out, off = [], 0
    for shp in shapes:
        n = 1
        for d in shp:
            n *= d
        out.append(flat[off:off + n].reshape(shp))
        off += n
    return out


def _mesh_place():
    x, y, c = lax.axis_index("x"), lax.axis_index("y"), lax.axis_index("c")
    chips = [(1 - x, y), (x, 1 - y), (1 - x, 1 - y)]
    return x, y, c, chips


_HBM = pl.BlockSpec(memory_space=pltpu.HBM)


def _all_gather(wp):
    R, W = wp.shape
    rh = R // 2

    def body(w_ref, out_ref, send_sems, recv_sems, local_sem):
        x, y, c, chips = _mesh_place()
        me = 2 * x + y
        sibling = (x, y, 1 - c)
        mine_rows = pl.ds(pl.multiple_of(c * rh, 16), rh)
        other_rows = pl.ds(pl.multiple_of((1 - c) * rh, 16), rh)

        def copy(k, src, dst, to):
            return pltpu.make_async_remote_copy(src_ref=src, dst_ref=dst, send_sem=send_sems.at[k],
                                                recv_sem=recv_sems.at[k], device_id=to, device_id_type=MESH)

        own = pltpu.make_async_copy(w_ref, out_ref.at[me], local_sem)
        own.start()
        first = [copy(k, w_ref.at[mine_rows], out_ref.at[me, mine_rows], (px, py, c))
                 for k, (px, py) in enumerate(chips)]
        for cp in first:
            cp.start()
        passed = []
        for k, (px, py) in enumerate(chips):
            slot = out_ref.at[2 * px + py, mine_rows]
            copy(k, slot, slot, (px, py, c)).wait_recv()
            fw = copy(3 + k, slot, slot, sibling)
            fw.start()
            passed.append(fw)
        for k, (px, py) in enumerate(chips):
            slot = out_ref.at[2 * px + py, other_rows]
            copy(3 + k, slot, slot, sibling).wait_recv()
        for cp in first + passed:
            cp.wait_send()
        own.wait()

    return pl.pallas_call(
        body, name="weights_all_gather", in_specs=[_HBM], out_specs=_HBM,
        out_shape=jax.ShapeDtypeStruct((N_CHIPS, R, W), wp.dtype),
        scratch_shapes=[pltpu.SemaphoreType.DMA((6,)), pltpu.SemaphoreType.DMA((6,)),
                        pltpu.SemaphoreType.DMA],
    )(wp)


def _exchange_halves(g):
    n, _, rh, W = g.shape

    def body(g_ref, out_ref, send_sems, recv_sems):
        x, y, c, _ = _mesh_place()
        sibling = (x, y, 1 - c)
        copies = [pltpu.make_async_remote_copy(
            src_ref=g_ref.at[j, 1 - c], dst_ref=out_ref.at[j], send_sem=send_sems.at[j],
            recv_sem=recv_sems.at[j], device_id=sibling, device_id_type=MESH) for j in range(n)]
        for cp in copies:
            cp.start()
        for cp in copies:
            cp.wait()

    return pl.pallas_call(
        body, name="grads_to_sibling", in_specs=[_HBM], out_specs=_HBM,
        out_shape=jax.ShapeDtypeStruct((n, rh, W), g.dtype),
        scratch_shapes=[pltpu.SemaphoreType.DMA((n,)), pltpu.SemaphoreType.DMA((n,))],
    )(g)


def _scatter_to_owners(p):
    n, rh, W = p.shape

    def body(p_ref, out_ref, send_sems, recv_sems, local_sem):
        x, y, c, chips = _mesh_place()
        me = 2 * x + y
        own = pltpu.make_async_copy(p_ref.at[me], out_ref.at[me], local_sem)
        own.start()
        copies = [pltpu.make_async_remote_copy(
            src_ref=p_ref.at[2 * px + py], dst_ref=out_ref.at[me], send_sem=send_sems.at[k],
            recv_sem=recv_sems.at[k], device_id=(px, py, c), device_id_type=MESH)
            for k, (px, py) in enumerate(chips)]
        for cp in copies:
            cp.start()
        for cp in copies:
            cp.wait()
        own.wait()

    return pl.pallas_call(
        body, name="grads_to_owner", in_specs=[_HBM], out_specs=_HBM,
        out_shape=jax.ShapeDtypeStruct((n, rh, W), p.dtype),
        scratch_shapes=[pltpu.SemaphoreType.DMA((3,)), pltpu.SemaphoreType.DMA((3,)),
                        pltpu.SemaphoreType.DMA],
    )(p)


def _share_reduced(q):
    rh, W = q.shape

    def body(q_ref, out_ref, send_sem, recv_sem, local_sem):
        x, y, c, _ = _mesh_place()
        own = pltpu.make_async_copy(q_ref, out_ref.at[c], local_sem)
        own.start()
        cp = pltpu.make_async_remote_copy(src_ref=q_ref, dst_ref=out_ref.at[c], send_sem=send_sem,
                                          recv_sem=recv_sem, device_id=(x, y, 1 - c), device_id_type=MESH)
        cp.start()
        cp.wait()
        own.wait()

    return pl.pallas_call(
        body, name="grads_share_reduced", in_specs=[_HBM], out_specs=_HBM,
        out_shape=jax.ShapeDtypeStruct((2, rh, W), q.dtype),
        scratch_shapes=[pltpu.SemaphoreType.DMA, pltpu.SemaphoreType.DMA, pltpu.SemaphoreType.DMA],
    )(q)


PACK_TILE = 1280


def _add_sibling(g, recv, c_arr):
    n, _, rh, W = g.shape
    tr = PACK_TILE

    def body(c_ref, g_ref, r_ref, o_ref):
        o_ref[...] = (g_ref[...].astype(F32) + r_ref[...].astype(F32)).astype(BF16)

    return pl.pallas_call(
        body, name="grads_add_sibling",
        grid_spec=pltpu.PrefetchScalarGridSpec(
            num_scalar_prefetch=1, grid=(n, rh // tr),
            in_specs=[pl.BlockSpec((None, None, tr, W), lambda j, i, c_ref: (j, c_ref[0], i, 0)),
                      pl.BlockSpec((None, tr, W), lambda j, i, c_ref: (j, i, 0))],
            out_specs=pl.BlockSpec((None, tr, W), lambda j, i, c_ref: (j, i, 0))),
        out_shape=jax.ShapeDtypeStruct((n, rh, W), BF16),
        compiler_params=_cparams("parallel", "parallel"))(c_arr, g, recv)


def _sum_chips(parts):
    n, rh, W = parts.shape
    tr = PACK_TILE

    def body(p_ref, o_ref):
        acc = p_ref[0].astype(F32)
        for j in range(1, n):
            acc = acc + p_ref[j].astype(F32)
        o_ref[...] = acc

    return pl.pallas_call(
        body, name="grads_sum_chips", grid=(rh // tr,),
        in_specs=[pl.BlockSpec((n, tr, W), lambda i: (0, i, 0))],
        out_specs=pl.BlockSpec((tr, W), lambda i: (i, 0)),
        out_shape=jax.ShapeDtypeStruct((rh, W), F32),
        compiler_params=_cparams("parallel"))(parts)


def _adamw(w, g, m, v, name):
    shape = w.shape
    cols = shape[-1]
    w2, g2, m2, v2 = (t.reshape(-1, cols) for t in (w, g, m, v))
    rows = w2.shape[0]
    tr = rows
    for cand in (512, 256, 128, 64, 32, 16, 8):
        if rows > cand and rows % cand == 0:
            tr = cand
            break
    c1 = 1.0 / (1.0 - ADAM_B1 ** ADAM_STEP)
    c2 = 1.0 / (1.0 - ADAM_B2 ** ADAM_STEP)

    def body(w_ref, g_ref, m_ref, v_ref, d_ref, nm_ref, nv_ref):
        gv = g_ref[...]
        nm = ADAM_B1 * m_ref[...] + (1.0 - ADAM_B1) * gv
        nv = ADAM_B2 * v_ref[...] + (1.0 - ADAM_B2) * (gv * gv)
        nm_ref[...] = nm
        nv_ref[...] = nv
        d_ref[...] = -ADAM_LR * ((nm * c1) / (jnp.sqrt(nv * c2) + ADAM_EPS) + ADAM_WD * w_ref[...])

    blk = pl.BlockSpec((tr, cols), lambda i: (i, 0))
    sds = jax.ShapeDtypeStruct((rows, cols), F32)
    d, nm, nv = pl.pallas_call(body, name=name, grid=(rows // tr,), in_specs=[blk] * 4,
                               out_specs=(blk, blk, blk), out_shape=(sds, sds, sds),
                               compiler_params=_cparams("parallel"))(w2, g2, m2, v2)
    return d.reshape(shape), nm.reshape(shape), nv.reshape(shape)


def kernel(x, positions, norm_gains, mla_w_in, mla_q_norm, mla_kv_norm, mla_w_uq, mla_w_ukv, mla_w_o, hgrn_w_in, hgrn_lb_logits, hgrn_o_norm, hgrn_w_o, mlp_w1, mlp_w2, loss_target, m_norm_gains, m_mla_w_in, m_mla_q_norm, m_mla_kv_norm, m_mla_w_uq, m_mla_w_ukv, m_mla_w_o, m_hgrn_w_in, m_hgrn_lb_logits, m_hgrn_o_norm, m_hgrn_w_o, m_mlp_w1, m_mlp_w2, v_norm_gains, v_mla_w_in, v_mla_q_norm, v_mla_kv_norm, v_mla_w_uq, v_mla_w_ukv, v_mla_w_o, v_hgrn_w_in, v_hgrn_lb_logits, v_hgrn_o_norm, v_hgrn_w_o, v_mlp_w1, v_mlp_w2):
    w = dict(norm_gains=norm_gains, mla_w_in=mla_w_in, mla_q_norm=mla_q_norm, mla_kv_norm=mla_kv_norm,
             mla_w_uq=mla_w_uq, mla_w_ukv=mla_w_ukv, mla_w_o=mla_w_o, hgrn_w_in=hgrn_w_in,
             hgrn_lb_logits=hgrn_lb_logits, hgrn_o_norm=hgrn_o_norm, hgrn_w_o=hgrn_w_o,
             mlp_w1=mlp_w1, mlp_w2=mlp_w2)
    mom_m = dict(norm_gains=m_norm_gains, mla_w_in=m_mla_w_in, mla_q_norm=m_mla_q_norm,
                 mla_kv_norm=m_mla_kv_norm, mla_w_uq=m_mla_w_uq, mla_w_ukv=m_mla_w_ukv,
                 mla_w_o=m_mla_w_o, hgrn_w_in=m_hgrn_w_in, hgrn_lb_logits=m_hgrn_lb_logits,
                 hgrn_o_norm=m_hgrn_o_norm, hgrn_w_o=m_hgrn_w_o, mlp_w1=m_mlp_w1, mlp_w2=m_mlp_w2)
    mom_v = dict(norm_gains=v_norm_gains, mla_w_in=v_mla_w_in, mla_q_norm=v_mla_q_norm,
                 mla_kv_norm=v_mla_kv_norm, mla_w_uq=v_mla_w_uq, mla_w_ukv=v_mla_w_ukv,
                 mla_w_o=v_mla_w_o, hgrn_w_in=v_hgrn_w_in, hgrn_lb_logits=v_hgrn_lb_logits,
                 hgrn_o_norm=v_hgrn_o_norm, hgrn_w_o=v_hgrn_w_o, mlp_w1=v_mlp_w1, mlp_w2=v_mlp_w2)
    c = lax.axis_index("c")

    mats = [(name, axis) for name, axis in SHARDED if name != "norm_gains"]
    shard_shapes = [w[name].shape for name, _ in mats] + [norm_gains.shape, norm_gains.shape]
    w_rows = _packed_rows(sum(_size(s) for s in shard_shapes))
    gain_bits = lax.bitcast_convert_type(norm_gains, jnp.uint32)
    gain_hi = lax.bitcast_convert_type((gain_bits >> 16).astype(jnp.uint16), BF16)
    gain_lo = lax.bitcast_convert_type((gain_bits & 0xFFFF).astype(jnp.uint16), BF16)
    wpack = _pack([w[name] for name, _ in mats] + [gain_hi, gain_lo], w_rows, BF16)
    gathered = _all_gather(wpack)
    per_chip = [_unpack(gathered[j], shard_shapes) for j in range(N_CHIPS)]
    wb = {}
    for i, (name, axis) in enumerate(mats):
        wb[name] = jnp.concatenate([per_chip[j][i] for j in range(N_CHIPS)], axis=axis)
    got_hi, got_lo = (lax.bitcast_convert_type(
        jnp.concatenate([per_chip[j][i] for j in range(N_CHIPS)], axis=2), jnp.uint16).astype(jnp.uint32)
        for i in (-2, -1))
    gains_full = lax.bitcast_convert_type((got_hi << 16) | got_lo, F32)
    small = dict(norm_gains=gains_full, mla_q_norm=mla_q_norm, mla_kv_norm=mla_kv_norm,
                 hgrn_lb_logits=hgrn_lb_logits, hgrn_o_norm=hgrn_o_norm)

    sq, grad_x, grads = _local_step(x[0], positions[0], loss_target[0], wb, small)
    d_model = x.shape[-1]
    loss = lax.psum(0.5 * jnp.sum(sq) / d_model, ("x", "y", "c"))

    grad_shapes = [w[name].shape for name, _ in SHARDED] + [w[name].shape for name in REPLICATED]
    g_rows = _packed_rows(sum(_size(s) for s in grad_shapes))
    slots = []
    for j in range(N_CHIPS):
        pieces = [jnp.split(grads[name], N_CHIPS, axis=axis)[j] for name, axis in SHARDED]
        pieces += [grads[name] for name in REPLICATED]
        slots.append(_pack(pieces, g_rows, BF16))
    gpack = jnp.stack(slots).reshape(N_CHIPS, 2, g_rows // 2, PACK_W)
    from_sibling = _exchange_halves(gpack)
    chip_partial = _add_sibling(gpack, from_sibling, jnp.reshape(c, (1,)).astype(jnp.int32))
    from_chips = _scatter_to_owners(chip_partial)
    reduced_half = _sum_chips(from_chips)
    reduced = _share_reduced(reduced_half).reshape(g_rows, PACK_W)
    red = _unpack(reduced, grad_shapes)
    g_out = {name: red[i] for i, (name, _) in enumerate(SHARDED)}
    g_out.update({name: red[len(SHARDED) + i] for i, name in enumerate(REPLICATED)})

    deltas, new_m, new_v = {}, {}, {}
    for name in WEIGHTS:
        deltas[name], new_m[name], new_v[name] = _adamw(w[name], g_out[name], mom_m[name], mom_v[name],
                                                        name="adamw_" + name)
    return (loss, grad_x[None], *[g_out[n] for n in WEIGHTS], *[deltas[n] for n in WEIGHTS],
            *[new_m[n] for n in WEIGHTS], *[new_v[n] for n in WEIGHTS])
```

```python
import functools

import jax
import jax.numpy as jnp
from jax import lax
from jax.experimental import pallas as pl
from jax.experimental.pallas import tpu as pltpu

F32 = jnp.float32
BF16 = jnp.bfloat16
MESH = pl.DeviceIdType.MESH

DEPTH = 4
MLA_HEADS = 8
MLA_NOPE = 128
MLA_ROPE = 64
MLA_V = 128
MLA_QK_PAD = 256
ROPE_BASE = 10000.0
HGRN_HEADS = 8
HGRN_CHUNK = 32
HGRN_BLOCK = 128
EPS = 1e-6

ADAM_LR = 0.001
ADAM_B1 = 0.9
ADAM_B2 = 0.999
ADAM_EPS = 1e-08
ADAM_WD = 0.01
ADAM_STEP = 10

N_CHIPS = 4
PACK_W = 1024
PACK_ALIGN = 2560
V7X_VMEM_LIMIT = 56 * 1024 * 1024

SHARDED = (("norm_gains", 2), ("mla_w_in", 1), ("mla_w_uq", 2), ("mla_w_ukv", 2), ("mla_w_o", 1),
           ("hgrn_w_in", 2), ("hgrn_w_o", 1), ("mlp_w1", 2), ("mlp_w2", 1))
REPLICATED = ("mla_q_norm", "mla_kv_norm", "hgrn_lb_logits", "hgrn_o_norm")
WEIGHTS = ("norm_gains", "mla_w_in", "mla_q_norm", "mla_kv_norm", "mla_w_uq", "mla_w_ukv", "mla_w_o",
           "hgrn_w_in", "hgrn_lb_logits", "hgrn_o_norm", "hgrn_w_o", "mlp_w1", "mlp_w2")


def _cparams(*semantics):
    return pltpu.CompilerParams(dimension_semantics=semantics, vmem_limit_bytes=V7X_VMEM_LIMIT)


def _sigmoid(x):
    return 1.0 / (1.0 + jnp.exp(-x))


def _mm(a, b, *, ta=False, tb=False, out_dtype=F32, tm=1024, tn=1024, tk=1024, epi=None, extra=None,
        name="mm"):
    if ta:
        K, M = a.shape
    else:
        M, K = a.shape
    if tb:
        N, Kb = b.shape
    else:
        Kb, N = b.shape
    assert K == Kb, (a.shape, b.shape, ta, tb)
    tm, tn = min(tm, M), min(tn, N)
    tk = K if K <= 1024 else tk
    assert M % tm == 0 and N % tn == 0 and K % tk == 0, (M, N, K, tm, tn, tk)
    nk = K // tk
    a_spec = (pl.BlockSpec((tk, tm), lambda i, j, k: (k, i)) if ta
              else pl.BlockSpec((tm, tk), lambda i, j, k: (i, k)))
    b_spec = (pl.BlockSpec((tn, tk), lambda i, j, k: (j, k)) if tb
              else pl.BlockSpec((tk, tn), lambda i, j, k: (k, j)))
    o_spec = pl.BlockSpec((tm, tn), lambda i, j, k: (i, j))
    dims = (((0 if ta else 1,), (1 if tb else 0,)), ((), ()))
    in_specs = [a_spec, b_spec]
    operands = [a, b]
    if epi == "mul2r":
        in_specs.append(o_spec)
        operands.append(extra)
    if epi == "relu2":
        out_shape = (jax.ShapeDtypeStruct((M, N), BF16), jax.ShapeDtypeStruct((M, N), BF16))
        out_specs = (o_spec, o_spec)
    elif epi == "mul2r":
        out_shape = jax.ShapeDtypeStruct((M, N), BF16)
        out_specs = o_spec
    else:
        out_shape = jax.ShapeDtypeStruct((M, N), out_dtype)
        out_specs = o_spec

    def body(*refs):
        a_ref, b_ref = refs[0], refs[1]
        acc_ref = refs[-1]
        k = pl.program_id(2)

        @pl.when(k == 0)
        def _():
            acc_ref[...] = jnp.zeros_like(acc_ref)

        acc_ref[...] += lax.dot_general(a_ref[...], b_ref[...], dims, preferred_element_type=F32)

        @pl.when(k == nk - 1)
        def _():
            acc = acc_ref[...]
            if epi == "relu2":
                r = jnp.maximum(acc, 0.0)
                refs[2][...] = (r * r).astype(BF16)
                refs[3][...] = r.astype(BF16)
            elif epi == "mul2r":
                refs[3][...] = (acc * (2.0 * refs[2][...].astype(F32))).astype(BF16)
            else:
                refs[2][...] = acc.astype(out_dtype)

    return pl.pallas_call(
        body, name=name, grid=(M // tm, N // tn, nk), in_specs=in_specs, out_specs=out_specs,
        out_shape=out_shape, scratch_shapes=[pltpu.VMEM((tm, tn), F32)],
        compiler_params=_cparams("parallel", "parallel", "arbitrary"))(*operands)


def _rms_rstd(x):
    return lax.rsqrt(jnp.mean(x * x, axis=-1, keepdims=True) + EPS)


def _rms_bwd_tile(x, g, dy):
    r = _rms_rstd(x)
    xh = x * r
    u = dy * g
    dx = r * (u - xh * jnp.mean(u * xh, axis=-1, keepdims=True))
    dg = jnp.sum(dy * xh, axis=0, keepdims=True)
    return dx, dg


def _row_tile(T):
    return min(256, T)


def _prenorm_fwd(x, g, name="prenorm_fwd"):
    T, D = x.shape
    tm = _row_tile(T)

    def body(x_ref, g_ref, a_ref):
        xv = x_ref[...]
        a_ref[...] = (xv * _rms_rstd(xv) * g_ref[...]).astype(BF16)

    row = pl.BlockSpec((tm, D), lambda i: (i, 0))
    vec = pl.BlockSpec((1, D), lambda i: (0, 0))
    return pl.pallas_call(body, name=name, grid=(T // tm,), in_specs=[row, vec], out_specs=row,
                          out_shape=jax.ShapeDtypeStruct((T, D), BF16),
                          compiler_params=_cparams("parallel"))(x, g)


def _resnorm_fwd(h, z, g_post, g_pre, name="resnorm_fwd"):
    T, D = h.shape
    tm = _row_tile(T)

    def body(h_ref, z_ref, gp_ref, gn_ref, hn_ref, a_ref):
        zv = z_ref[...]
        hn = h_ref[...] + zv * _rms_rstd(zv) * gp_ref[...]
        hn_ref[...] = hn
        a_ref[...] = (hn * _rms_rstd(hn) * gn_ref[...]).astype(BF16)

    row = pl.BlockSpec((tm, D), lambda i: (i, 0))
    vec = pl.BlockSpec((1, D), lambda i: (0, 0))
    return pl.pallas_call(body, name=name, grid=(T // tm,), in_specs=[row, row, vec, vec],
                          out_specs=(row, row),
                          out_shape=(jax.ShapeDtypeStruct((T, D), F32), jax.ShapeDtypeStruct((T, D), BF16)),
                          compiler_params=_cparams("parallel"))(h, z, g_post, g_pre)


def _resnorm_loss(h, z, g_post, target, name="resnorm_loss"):
    T, D = h.shape
    tm = _row_tile(T)

    def body(h_ref, z_ref, gp_ref, t_ref, dy_ref, sq_ref):
        zv = z_ref[...]
        err = h_ref[...] + zv * _rms_rstd(zv) * gp_ref[...] - t_ref[...]
        dy_ref[...] = err * (1.0 / D)

        @pl.when(pl.program_id(0) == 0)
        def _():
            sq_ref[...] = jnp.zeros_like(sq_ref)

        sq_ref[...] += jnp.sum(err * err, axis=0, keepdims=True)

    row = pl.BlockSpec((tm, D), lambda i: (i, 0))
    vec = pl.BlockSpec((1, D), lambda i: (0, 0))
    return pl.pallas_call(body, name=name, grid=(T // tm,), in_specs=[row, row, vec, row],
                          out_specs=(row, vec),
                          out_shape=(jax.ShapeDtypeStruct((T, D), F32), jax.ShapeDtypeStruct((1, D), F32)),
                          compiler_params=_cparams("arbitrary"))(h, z, g_post, target)


def _resnorm_bwd(z, g_post, dh, h_new=None, da=None, g_pre=None, name="resnorm_bwd"):
    T, D = z.shape
    tm = _row_tile(T)
    has_next = h_new is not None
    row = pl.BlockSpec((tm, D), lambda i: (i, 0))
    vec = pl.BlockSpec((1, D), lambda i: (0, 0))

    if has_next:
        def body(z_ref, gp_ref, dh_ref, hn_ref, da_ref, gn_ref, t_ref, dz_ref, dgp_ref, dgn_ref):
            first = pl.program_id(0) == 0

            @pl.when(first)
            def _():
                dgp_ref[...] = jnp.zeros_like(dgp_ref)
                dgn_ref[...] = jnp.zeros_like(dgn_ref)

            dpre, dgn = _rms_bwd_tile(hn_ref[...], gn_ref[...], da_ref[...])
            t = dh_ref[...] + dpre
            t_ref[...] = t
            dz, dgp = _rms_bwd_tile(z_ref[...], gp_ref[...], t)
            dz_ref[...] = dz.astype(BF16)
            dgp_ref[...] += dgp
            dgn_ref[...] += dgn

        return pl.pallas_call(
            body, name=name, grid=(T // tm,), in_specs=[row, vec, row, row, row, vec],
            out_specs=(row, row, vec, vec),
            out_shape=(jax.ShapeDtypeStruct((T, D), F32), jax.ShapeDtypeStruct((T, D), BF16),
                       jax.ShapeDtypeStruct((1, D), F32), jax.ShapeDtypeStruct((1, D), F32)),
            compiler_params=_cparams("arbitrary"))(z, g_post, dh, h_new, da, g_pre)

    def body_last(z_ref, gp_ref, dh_ref, dz_ref, dgp_ref):
        @pl.when(pl.program_id(0) == 0)
        def _():
            dgp_ref[...] = jnp.zeros_like(dgp_ref)

        dz, dgp = _rms_bwd_tile(z_ref[...], gp_ref[...], dh_ref[...])
        dz_ref[...] = dz.astype(BF16)
        dgp_ref[...] += dgp

    return pl.pallas_call(
        body_last, name=name, grid=(T // tm,), in_specs=[row, vec, row], out_specs=(row, vec),
        out_shape=(jax.ShapeDtypeStruct((T, D), BF16), jax.ShapeDtypeStruct((1, D), F32)),
        compiler_params=_cparams("arbitrary"))(z, g_post, dh)


def _prenorm_bwd(x, g, dh, da, name="prenorm_bwd"):
    T, D = x.shape
    tm = _row_tile(T)

    def body(x_ref, g_ref, dh_ref, da_ref, dx_ref, dg_ref):
        @pl.when(pl.program_id(0) == 0)
        def _():
            dg_ref[...] = jnp.zeros_like(dg_ref)

        dpre, dg = _rms_bwd_tile(x_ref[...], g_ref[...], da_ref[...])
        dx_ref[...] = dh_ref[...] + dpre
        dg_ref[...] += dg

    row = pl.BlockSpec((tm, D), lambda i: (i, 0))
    vec = pl.BlockSpec((1, D), lambda i: (0, 0))
    return pl.pallas_call(
        body, name=name, grid=(T // tm,), in_specs=[row, vec, row, row], out_specs=(row, vec),
        out_shape=(jax.ShapeDtypeStruct((T, D), F32), jax.ShapeDtypeStruct((1, D), F32)),
        compiler_params=_cparams("arbitrary"))(x, g, dh, da)


def _mlp_fwd(a, w1, w2):
    act, r = _mm(a, w1, epi="relu2", name="mlp_up")
    u = _mm(act, w2, name="mlp_down")
    return u, (a, act, r)


def _mlp_bwd(du, saved, w1, w2):
    a, act, r = saved
    dz1 = _mm(du, w2, tb=True, epi="mul2r", extra=r, name="mlp_down_dx")
    dw2 = _mm(act, du, ta=True, out_dtype=BF16, name="mlp_down_dw")
    dw1 = _mm(a, dz1, ta=True, out_dtype=BF16, name="mlp_up_dw")
    da = _mm(dz1, w1, tb=True, name="mlp_up_dx")
    return da, dw1, dw2


def _rope_swap(t):
    n = t.shape[-1]
    lane = lax.broadcasted_iota(jnp.int32, t.shape, t.ndim - 1)
    half = MLA_ROPE // 2
    first = (lane & (MLA_ROPE - 1)) < half
    return jnp.where(first, pltpu.roll(t, n - half, t.ndim - 1), pltpu.roll(t, half, t.ndim - 1))


def _mla_mid_fwd(proj, q_norm, kv_norm, w_uq, w_ukv, cc, ss):
    T, PW = proj.shape
    QL, KVL = q_norm.shape[-1], kv_norm.shape[-1]
    H = MLA_HEADS
    assert PW == QL + KVL + 128
    tm = _row_tile(T)

    def body(p_ref, qn_ref, kn_ref, wq_ref, wkv_ref, cc_ref, ss_ref,
             cq_ref, ckv_ref, q_ref, k_ref, v_ref):
        cq = p_ref[:, 0:QL]
        ckv = p_ref[:, QL:QL + KVL]
        kr = p_ref[:, QL + KVL:QL + KVL + 128]
        c, s = cc_ref[...], ss_ref[...]
        cqn = (cq * _rms_rstd(cq) * qn_ref[...]).astype(BF16)
        ckvn = (ckv * _rms_rstd(ckv) * kn_ref[...]).astype(BF16)
        cq_ref[...] = cqn
        ckv_ref[...] = ckvn
        q = jnp.dot(cqn, wq_ref[...], preferred_element_type=F32)
        kv = jnp.dot(ckvn, wkv_ref[...], preferred_element_type=F32)
        krf = (kr * c + _rope_swap(kr) * s).astype(BF16)
        for h in range(H):
            o = h * MLA_QK_PAD
            q_ref[:, o:o + MLA_NOPE] = q[:, o:o + MLA_NOPE].astype(BF16)
            qr = q[:, o + MLA_NOPE:o + MLA_QK_PAD]
            q_ref[:, o + MLA_NOPE:o + MLA_QK_PAD] = (qr * c + _rope_swap(qr) * s).astype(BF16)
            k_ref[:, o:o + MLA_NOPE] = kv[:, o:o + MLA_NOPE].astype(BF16)
            k_ref[:, o + MLA_NOPE:o + MLA_QK_PAD] = krf
            v_ref[:, h * MLA_V:(h + 1) * MLA_V] = kv[:, o + MLA_NOPE:o + MLA_QK_PAD].astype(BF16)

    def row(w):
        return pl.BlockSpec((tm, w), lambda i: (i, 0))

    def full(shape):
        return pl.BlockSpec(shape, lambda i: (0, 0))

    return pl.pallas_call(
        body, name="mla_mid_fwd", grid=(T // tm,),
        in_specs=[row(PW), full((1, QL)), full((1, KVL)), full(w_uq.shape), full(w_ukv.shape),
                  row(128), row(128)],
        out_specs=(row(QL), row(KVL), row(H * MLA_QK_PAD), row(H * MLA_QK_PAD), row(H * MLA_V)),
        out_shape=(jax.ShapeDtypeStruct((T, QL), BF16), jax.ShapeDtypeStruct((T, KVL), BF16),
                   jax.ShapeDtypeStruct((T, H * MLA_QK_PAD), BF16),
                   jax.ShapeDtypeStruct((T, H * MLA_QK_PAD), BF16),
                   jax.ShapeDtypeStruct((T, H * MLA_V), BF16)),
        compiler_params=_cparams("parallel"))(proj, q_norm, kv_norm, w_uq, w_ukv, cc, ss)


def _mla_mid_bwd(proj, q_norm, kv_norm, w_uq, w_ukv, cc, ss, dq, dk, dv):
    T, PW = proj.shape
    QL, KVL = q_norm.shape[-1], kv_norm.shape[-1]
    H = MLA_HEADS
    tm = _row_tile(T)
    nt = (((1,), (1,)), ((), ()))

    def body(p_ref, qn_ref, kn_ref, wq_ref, wkv_ref, cc_ref, ss_ref, dq_ref, dk_ref, dv_ref,
             dqp_ref, dkv_ref, dp_ref, dqn_ref, dkn_ref):
        @pl.when(pl.program_id(0) == 0)
        def _():
            dqn_ref[...] = jnp.zeros_like(dqn_ref)
            dkn_ref[...] = jnp.zeros_like(dkn_ref)

        c, s = cc_ref[...], ss_ref[...]
        dkr = jnp.zeros((tm, 128), F32)
        for h in range(H):
            o = h * MLA_QK_PAD
            dqp_ref[:, o:o + MLA_NOPE] = dq_ref[:, o:o + MLA_NOPE].astype(BF16)
            dqr = dq_ref[:, o + MLA_NOPE:o + MLA_QK_PAD]
            dqp_ref[:, o + MLA_NOPE:o + MLA_QK_PAD] = (dqr * c + _rope_swap(dqr * s)).astype(BF16)
            dkv_ref[:, o:o + MLA_NOPE] = dk_ref[:, o:o + MLA_NOPE].astype(BF16)
            dkv_ref[:, o + MLA_NOPE:o + MLA_QK_PAD] = dv_ref[:, h * MLA_V:(h + 1) * MLA_V].astype(BF16)
            dkr = dkr + dk_ref[:, o + MLA_NOPE:o + MLA_QK_PAD]
        dcqn = lax.dot_general(dqp_ref[...], wq_ref[...], nt, preferred_element_type=F32)
        dckvn = lax.dot_general(dkv_ref[...], wkv_ref[...], nt, preferred_element_type=F32)
        dcq, dqn = _rms_bwd_tile(p_ref[:, 0:QL], qn_ref[...], dcqn)
        dckv, dkn = _rms_bwd_tile(p_ref[:, QL:QL + KVL], kn_ref[...], dckvn)
        dp_ref[:, 0:QL] = dcq.astype(BF16)
        dp_ref[:, QL:QL + KVL] = dckv.astype(BF16)
        dp_ref[:, QL + KVL:QL + KVL + 128] = (dkr * c + _rope_swap(dkr * s)).astype(BF16)
        dqn_ref[...] += dqn
        dkn_ref[...] += dkn

    def row(w):
        return pl.BlockSpec((tm, w), lambda i: (i, 0))

    def full(shape):
        return pl.BlockSpec(shape, lambda i: (0, 0))

    return pl.pallas_call(
        body, name="mla_mid_bwd", grid=(T // tm,),
        in_specs=[row(PW), full((1, QL)), full((1, KVL)), full(w_uq.shape), full(w_ukv.shape),
                  row(128), row(128), row(H * MLA_QK_PAD), row(H * MLA_QK_PAD), row(H * MLA_V)],
        out_specs=(row(H * MLA_QK_PAD), row(H * MLA_QK_PAD), row(PW), full((1, QL)), full((1, KVL))),
        out_shape=(jax.ShapeDtypeStruct((T, H * MLA_QK_PAD), BF16),
                   jax.ShapeDtypeStruct((T, H * MLA_QK_PAD), BF16),
                   jax.ShapeDtypeStruct((T, PW), BF16),
                   jax.ShapeDtypeStruct((1, QL), F32), jax.ShapeDtypeStruct((1, KVL), F32)),
        compiler_params=_cparams("arbitrary"))(proj, q_norm, kv_norm, w_uq, w_ukv, cc, ss, dq, dk, dv)


def _attn_tile(T):
    return min(512, T)


def _attn_pairs(n, by_key):
    if by_key:
        pairs = [(qi, ki) for ki in range(n) for qi in range(ki, n)]
    else:
        pairs = [(qi, ki) for qi in range(n) for ki in range(qi + 1)]
    return (jnp.asarray([p[0] for p in pairs], jnp.int32), jnp.asarray([p[1] for p in pairs], jnp.int32))


def _scores(q, k, scale, diagonal):
    s = lax.dot_general(q, k, (((1,), (1,)), ((), ())), preferred_element_type=F32) * scale
    if diagonal:
        rows = lax.broadcasted_iota(jnp.int32, s.shape, 0)
        cols = lax.broadcasted_iota(jnp.int32, s.shape, 1)
        s = jnp.where(rows >= cols, s, -jnp.inf)
    return s


def _attn_fwd(q, k, v):
    T = q.shape[0]
    H, DQ, DV = MLA_HEADS, MLA_QK_PAD, MLA_V
    tq = _attn_tile(T)
    nq = T // tq
    scale = float(MLA_NOPE + MLA_ROPE) ** -0.5
    qi_tab, ki_tab = _attn_pairs(nq, by_key=False)

    def body(qi_ref, ki_ref, q_ref, k_ref, v_ref, o_ref, lse_ref, m_ref, l_ref, acc_ref):
        p = pl.program_id(1)
        qi, ki = qi_ref[p], ki_ref[p]

        @pl.when(ki == 0)
        def _():
            m_ref[...] = jnp.full_like(m_ref, -jnp.inf)
            l_ref[...] = jnp.zeros_like(l_ref)
            acc_ref[...] = jnp.zeros_like(acc_ref)

        def update(diagonal):
            s = _scores(q_ref[...], k_ref[...], scale, diagonal)
            m_prev = m_ref[...]
            m_new = jnp.maximum(m_prev, jnp.max(s, axis=1, keepdims=True))
            alpha = jnp.exp(m_prev - m_new)
            pr = jnp.exp(s - m_new)
            l_ref[...] = alpha * l_ref[...] + jnp.sum(pr, axis=1, keepdims=True)
            acc_ref[...] = alpha * acc_ref[...] + jnp.dot(pr.astype(BF16), v_ref[...],
                                                          preferred_element_type=F32)
            m_ref[...] = m_new

        @pl.when(ki < qi)
        def _():
            update(False)

        @pl.when(ki == qi)
        def _():
            update(True)
            o_ref[...] = (acc_ref[...] / l_ref[...]).astype(BF16)
            lse_ref[0] = m_ref[...] + jnp.log(l_ref[...])

    return pl.pallas_call(
        body, name="attn_fwd",
        grid_spec=pltpu.PrefetchScalarGridSpec(
            num_scalar_prefetch=2, grid=(H, int(qi_tab.shape[0])),
            in_specs=[pl.BlockSpec((tq, DQ), lambda h, p, qt, kt: (qt[p], h)),
                      pl.BlockSpec((tq, DQ), lambda h, p, qt, kt: (kt[p], h)),
                      pl.BlockSpec((tq, DV), lambda h, p, qt, kt: (kt[p], h))],
            out_specs=(pl.BlockSpec((tq, DV), lambda h, p, qt, kt: (qt[p], h)),
                       pl.BlockSpec((1, tq, 1), lambda h, p, qt, kt: (h, qt[p], 0))),
            scratch_shapes=[pltpu.VMEM((tq, 1), F32), pltpu.VMEM((tq, 1), F32), pltpu.VMEM((tq, DV), F32)]),
        out_shape=(jax.ShapeDtypeStruct((T, H * DV), BF16), jax.ShapeDtypeStruct((H, T, 1), F32)),
        compiler_params=_cparams("parallel", "arbitrary"))(qi_tab, ki_tab, q, k, v)


def _attn_bwd(q, k, v, o, do, lse):
    T = q.shape[0]
    H, DQ, DV = MLA_HEADS, MLA_QK_PAD, MLA_V
    tq = _attn_tile(T)
    nq = T // tq
    scale = float(MLA_NOPE + MLA_ROPE) ** -0.5
    tn = (((0,), (0,)), ((), ()))
    nt = (((1,), (1,)), ((), ()))
    qi_tab, ki_tab = _attn_pairs(nq, by_key=True)

    def body(qi_ref, ki_ref, q_ref, k_ref, v_ref, o_ref, do_ref, lse_ref, dq_ref, dk_ref, dv_ref,
             dk_acc, dv_acc):
        p = pl.program_id(1)
        qi, ki = qi_ref[p], ki_ref[p]

        @pl.when(p == 0)
        def _():
            dq_ref[...] = jnp.zeros_like(dq_ref)

        @pl.when(qi == ki)
        def _():
            dk_acc[...] = jnp.zeros_like(dk_acc)
            dv_acc[...] = jnp.zeros_like(dv_acc)

        def step(diagonal):
            dof = do_ref[...]
            delta = jnp.sum(dof.astype(F32) * o_ref[...].astype(F32), axis=1, keepdims=True)
            s = _scores(q_ref[...], k_ref[...], scale, diagonal)
            pr = jnp.exp(s - lse_ref[0])
            dp = lax.dot_general(dof, v_ref[...], nt, preferred_element_type=F32)
            ds = (pr * (dp - delta) * scale).astype(BF16)
            dv_acc[...] += lax.dot_general(pr.astype(BF16), dof, tn, preferred_element_type=F32)
            dk_acc[...] += lax.dot_general(ds, q_ref[...], tn, preferred_element_type=F32)
            rows = pl.ds(pl.multiple_of(qi * tq, tq), tq)
            dq_ref[rows, :] += jnp.dot(ds, k_ref[...], preferred_element_type=F32)

        @pl.when(qi == ki)
        def _():
            step(True)

        @pl.when(qi > ki)
        def _():
            step(False)

        @pl.when(qi == nq - 1)
        def _():
            dk_ref[...] = dk_acc[...]
            dv_ref[...] = dv_acc[...]

    qspec = pl.BlockSpec((tq, DQ), lambda h, p, qt, kt: (qt[p], h))
    ospec = pl.BlockSpec((tq, DV), lambda h, p, qt, kt: (qt[p], h))
    kspec = pl.BlockSpec((tq, DQ), lambda h, p, qt, kt: (kt[p], h))
    vspec = pl.BlockSpec((tq, DV), lambda h, p, qt, kt: (kt[p], h))
    return pl.pallas_call(
        body, name="attn_bwd",
        grid_spec=pltpu.PrefetchScalarGridSpec(
            num_scalar_prefetch=2, grid=(H, int(qi_tab.shape[0])),
            in_specs=[qspec, kspec, vspec, ospec, ospec,
                      pl.BlockSpec((1, tq, 1), lambda h, p, qt, kt: (h, qt[p], 0))],
            out_specs=(pl.BlockSpec((T, DQ), lambda h, p, qt, kt: (0, h)), kspec, vspec),
            scratch_shapes=[pltpu.VMEM((tq, DQ), F32), pltpu.VMEM((tq, DV), F32)]),
        out_shape=(jax.ShapeDtypeStruct((T, H * DQ), F32), jax.ShapeDtypeStruct((T, H * DQ), F32),
                   jax.ShapeDtypeStruct((T, H * DV), F32)),
        compiler_params=_cparams("parallel", "arbitrary"))(qi_tab, ki_tab, q, k, v, o, do, lse)


def _mla_fwd(a, w, cc, ss):
    proj = _mm(a, w["w_in"], name="mla_in")
    cqn, ckvn, q, k, v = _mla_mid_fwd(proj, w["q_norm"], w["kv_norm"], w["w_uq"], w["w_ukv"], cc, ss)
    o, lse = _attn_fwd(q, k, v)
    m = _mm(o, w["w_o"], name="mla_out")
    return m, (a, proj, cqn, ckvn, q, k, v, o, lse)


def _mla_bwd(dm, saved, w, cc, ss):
    a, proj, cqn, ckvn, q, k, v, o, lse = saved
    do = _mm(dm, w["w_o"], tb=True, out_dtype=BF16, name="mla_out_dx")
    dw_o = _mm(o, dm, ta=True, out_dtype=BF16, name="mla_out_dw")
    dq, dk, dv = _attn_bwd(q, k, v, o, do, lse)
    dqp, dkv, dproj, dqn, dkn = _mla_mid_bwd(proj, w["q_norm"], w["kv_norm"], w["w_uq"], w["w_ukv"],
                                             cc, ss, dq, dk, dv)
    dw_uq = _mm(cqn, dqp, ta=True, out_dtype=BF16, name="mla_uq_dw")
    dw_ukv = _mm(ckvn, dkv, ta=True, out_dtype=BF16, name="mla_ukv_dw")
    dw_in = _mm(a, dproj, ta=True, out_dtype=BF16, name="mla_in_dw")
    da = _mm(dproj, w["w_in"], tb=True, name="mla_in_dx")
    return da, dict(w_in=dw_in, w_uq=dw_uq, w_ukv=dw_ukv, w_o=dw_o, q_norm=dqn, kv_norm=dkn)


def _split_dot(mat, x, parts):
    acc = None
    rem = x
    for _ in range(parts):
        piece = rem.astype(BF16)
        term = jnp.dot(mat, piece, preferred_element_type=F32)
        acc = term if acc is None else acc + term
        rem = rem - piece.astype(F32)
    return acc


def _chunk_mats(tb):
    C = HGRN_CHUNK
    assert C & (C - 1) == 0
    r = lax.broadcasted_iota(jnp.int32, (tb, tb), 0)
    s = lax.broadcasted_iota(jnp.int32, (tb, tb), 1)
    start = r & ~(C - 1)
    same = start == (s & ~(C - 1))
    ref = start + C // 2
    last = start + C - 1
    one, zero = jnp.float32(1.0), jnp.float32(0.0)
    cum = jnp.where(same & (s <= r), one, zero)
    rel = cum - jnp.where(same & (s <= ref), one, zero)
    rest = jnp.where(same & (s > r) & (s <= last), one, zero)
    rev = jnp.where(same & (s >= r), one, zero)
    ones = jnp.where(same, one, zero)
    causal = same & (s <= r)
    return cum, rel, rest, rev, ones, causal


def _hgrn_gates(p_ref, lb, HK):
    qx = p_ref[:, 0:HK]
    fx = p_ref[:, HK:2 * HK]
    sf = _sigmoid(fx)
    f = lb + (1.0 - lb) * sf
    sq = _sigmoid(qx)
    return qx, sq, qx * sq, sf, f, 1.0 - f, jnp.log(f)


def _hgrn_fwd(proj, lb, o_norm):
    T = proj.shape[0]
    H, C = HGRN_HEADS, HGRN_CHUNK
    HK = proj.shape[1] // 4
    DK = HK // H
    tb = min(HGRN_BLOCK, T)
    ncb = tb // C
    nt = (((1,), (1,)), ((), ()))
    tn = (((0,), (0,)), ((), ()))

    def body(p_ref, lb_ref, on_ref, y_ref, o_ref, st_ref, state, oacc):
        @pl.when(pl.program_id(0) == 0)
        def _():
            state[...] = jnp.zeros_like(state)

        cum, rel, rest, _, _, causal = _chunk_mats(tb)
        _, _, q, _, f, k, logf = _hgrn_gates(p_ref, lb_ref[...], HK)
        b = _split_dot(cum.astype(BF16), logf, 3)
        brel = _split_dot(rel.astype(BF16), logf, 3)
        brest = _split_dot(rest.astype(BF16), logf, 3)
        eb = jnp.exp(b)
        q_rel = (q * jnp.exp(brel)).astype(BF16)
        k_rel = (k * jnp.exp(-brel)).astype(BF16)
        q_dec = (q * eb).astype(BF16)
        k_dec = (k * jnp.exp(brest)).astype(BF16)
        v = p_ref[:, 2 * HK:3 * HK].astype(BF16)
        for h in range(H):
            hs = slice(h * DK, (h + 1) * DK)
            a = lax.dot_general(q_rel[:, hs], k_rel[:, hs], nt, preferred_element_type=F32)
            a = jnp.where(causal, a, 0.0).astype(BF16)
            oacc[:, hs] = jnp.dot(a, v[:, hs], preferred_element_type=F32)
            for j in range(ncb):
                rs = slice(j * C, (j + 1) * C)
                st = state[h]
                st_ref[j, h] = st
                oacc[rs, hs] += lax.dot_general(q_dec[rs, hs], st.astype(BF16), nt,
                                                preferred_element_type=F32)
                dec = jnp.exp(jnp.sum(logf[rs, hs], axis=0, keepdims=True))
                state[h] = dec * st + lax.dot_general(v[rs, hs], k_dec[rs, hs], tn,
                                                      preferred_element_type=F32)
        o = oacc[...]
        o_ref[...] = o
        gx = p_ref[:, 3 * HK:4 * HK]
        gate = gx * _sigmoid(gx)
        for h in range(H):
            hs = slice(h * DK, (h + 1) * DK)
            oh = o[:, hs]
            y_ref[:, hs] = (oh * _rms_rstd(oh) * on_ref[...] * gate[:, hs]).astype(BF16)

    return pl.pallas_call(
        body, name="hgrn_fwd", grid=(T // tb,),
        in_specs=[pl.BlockSpec((tb, 4 * HK), lambda i: (i, 0)),
                  pl.BlockSpec((1, HK), lambda i: (0, 0)),
                  pl.BlockSpec((1, DK), lambda i: (0, 0))],
        out_specs=(pl.BlockSpec((tb, HK), lambda i: (i, 0)),
                   pl.BlockSpec((tb, HK), lambda i: (i, 0)),
                   pl.BlockSpec((ncb, H, DK, DK), lambda i: (i, 0, 0, 0))),
        out_shape=(jax.ShapeDtypeStruct((T, HK), BF16), jax.ShapeDtypeStruct((T, HK), F32),
                   jax.ShapeDtypeStruct((T // C, H, DK, DK), F32)),
        scratch_shapes=[pltpu.VMEM((H, DK, DK), F32), pltpu.VMEM((tb, HK), F32)],
        compiler_params=_cparams("arbitrary"))(proj, lb, o_norm)


def _hgrn_bwd(proj, lb, o_norm, o, states, dy):
    T = proj.shape[0]
    H, C = HGRN_HEADS, HGRN_CHUNK
    HK = proj.shape[1] // 4
    DK = HK // H
    tb = min(HGRN_BLOCK, T)
    ncb = tb // C
    nb = T // tb
    nt = (((1,), (1,)), ((), ()))
    tn = (((0,), (0,)), ((), ()))

    def body(p_ref, lb_ref, on_ref, o_ref, st_ref, dy_ref, dp_ref, dlb_ref, don_ref,
             dstate, dqr_s, dkr_s, dqd_s, dkd_s, dv_s, do_s, e_s):
        @pl.when(pl.program_id(0) == 0)
        def _():
            dstate[...] = jnp.zeros_like(dstate)
            dlb_ref[...] = jnp.zeros_like(dlb_ref)
            don_ref[...] = jnp.zeros_like(don_ref)

        cum, rel, rest, rev, ones, causal = _chunk_mats(tb)
        lb = lb_ref[...]
        qx, sq, q, sf, f, k, logf = _hgrn_gates(p_ref, lb, HK)
        b = _split_dot(cum.astype(BF16), logf, 3)
        brel = _split_dot(rel.astype(BF16), logf, 3)
        brest = _split_dot(rest.astype(BF16), logf, 3)
        eb = jnp.exp(b)
        erel = jnp.exp(brel)
        enrel = jnp.exp(-brel)
        erest = jnp.exp(brest)
        q_rel_f, k_rel_f, q_dec_f, k_dec_f = q * erel, k * enrel, q * eb, k * erest
        q_rel, k_rel = q_rel_f.astype(BF16), k_rel_f.astype(BF16)
        q_dec, k_dec = q_dec_f.astype(BF16), k_dec_f.astype(BF16)
        v = p_ref[:, 2 * HK:3 * HK].astype(BF16)

        gx = p_ref[:, 3 * HK:4 * HK]
        sg = _sigmoid(gx)
        gate = gx * sg
        dy = dy_ref[...]
        ov = o_ref[...]
        on = on_ref[...]
        don = jnp.zeros((1, DK), F32)
        for h in range(H):
            hs = slice(h * DK, (h + 1) * DK)
            oh = ov[:, hs]
            r = _rms_rstd(oh)
            xh = oh * r
            d_on = dy[:, hs] * gate[:, hs]
            don = don + jnp.sum(d_on * xh, axis=0, keepdims=True)
            u = d_on * on
            do_s[:, hs] = r * (u - xh * jnp.mean(u * xh, axis=-1, keepdims=True))
            dp_ref[:, 3 * HK + h * DK:3 * HK + (h + 1) * DK] = (
                dy[:, hs] * xh * on * (sg[:, hs] * (1.0 + gx[:, hs] * (1.0 - sg[:, hs])))).astype(BF16)
        don_ref[...] += don

        for h in range(H):
            hs = slice(h * DK, (h + 1) * DK)
            doh = do_s[:, hs].astype(BF16)
            a = lax.dot_general(q_rel[:, hs], k_rel[:, hs], nt, preferred_element_type=F32)
            a = jnp.where(causal, a, 0.0).astype(BF16)
            da = lax.dot_general(doh, v[:, hs], nt, preferred_element_type=F32)
            da = jnp.where(causal, da, 0.0).astype(BF16)
            dv_s[:, hs] = lax.dot_general(a, doh, tn, preferred_element_type=F32)
            dqr_s[:, hs] = jnp.dot(da, k_rel[:, hs], preferred_element_type=F32)
            dkr_s[:, hs] = lax.dot_general(da, q_rel[:, hs], tn, preferred_element_type=F32)
            for j in reversed(range(ncb)):
                rs = slice(j * C, (j + 1) * C)
                dst = dstate[h]
                dstb = dst.astype(BF16)
                st = st_ref[j, h]
                dkd_s[rs, hs] = jnp.dot(v[rs, hs], dstb, preferred_element_type=F32)
                dv_s[rs, hs] += lax.dot_general(k_dec[rs, hs], dstb, nt, preferred_element_type=F32)
                dec = jnp.exp(jnp.sum(logf[rs, hs], axis=0, keepdims=True))
                e_s[rs, hs] = jnp.broadcast_to(jnp.sum(dst * st, axis=0, keepdims=True) * dec, (C, DK))
                dqd_s[rs, hs] = jnp.dot(doh[rs], st.astype(BF16), preferred_element_type=F32)
                dstate[h] = dec * dst + lax.dot_general(doh[rs], q_dec[rs, hs], tn,
                                                        preferred_element_type=F32)

        dqr, dkr, dqd, dkd = dqr_s[...], dkr_s[...], dqd_s[...], dkd_s[...]
        kdk = dkd * k_dec_f
        db = dqr * q_rel_f - dkr * k_rel_f + dqd * q_dec_f - kdk
        dlogf = _split_dot(rev.astype(BF16), db, 2) + _split_dot(ones.astype(BF16), kdk, 2) + e_s[...]
        dk = dkr * enrel + dkd * erest
        df = dlogf / f - dk
        dlb_ref[...] += jnp.sum(df * (1.0 - sf), axis=0, keepdims=True)
        dq = dqr * erel + dqd * eb
        dp_ref[:, 0:HK] = (dq * (sq * (1.0 + qx * (1.0 - sq)))).astype(BF16)
        dp_ref[:, HK:2 * HK] = (df * (1.0 - lb) * sf * (1.0 - sf)).astype(BF16)
        dp_ref[:, 2 * HK:3 * HK] = dv_s[...].astype(BF16)

    rev_row = lambda w: pl.BlockSpec((tb, w), lambda i: (nb - 1 - i, 0))
    vec = lambda w: pl.BlockSpec((1, w), lambda i: (0, 0))
    scr = pltpu.VMEM((tb, HK), F32)
    return pl.pallas_call(
        body, name="hgrn_bwd", grid=(nb,),
        in_specs=[rev_row(4 * HK), vec(HK), vec(DK), rev_row(HK),
                  pl.BlockSpec((ncb, H, DK, DK), lambda i: (nb - 1 - i, 0, 0, 0)), rev_row(HK)],
        out_specs=(rev_row(4 * HK), vec(HK), vec(DK)),
        out_shape=(jax.ShapeDtypeStruct((T, 4 * HK), BF16), jax.ShapeDtypeStruct((1, HK), F32),
                   jax.ShapeDtypeStruct((1, DK), F32)),
        scratch_shapes=[pltpu.VMEM((H, DK, DK), F32), scr, scr, scr, scr, scr, scr, scr],
        compiler_params=_cparams("arbitrary"))(proj, lb, o_norm, o, states, dy)


def _hgrn_layer_fwd(a, w, lb):
    proj = _mm(a, w["w_in"], name="hgrn_in")
    y, o, states = _hgrn_fwd(proj, lb, w["o_norm"])
    m = _mm(y, w["w_o"], name="hgrn_out")
    return m, (a, proj, y, o, states)


def _hgrn_layer_bwd(dm, saved, w, lb):
    a, proj, y, o, states = saved
    dy = _mm(dm, w["w_o"], tb=True, name="hgrn_out_dx")
    dw_o = _mm(y, dm, ta=True, out_dtype=BF16, name="hgrn_out_dw")
    dproj, dlb, don = _hgrn_bwd(proj, lb, w["o_norm"], o, states, dy)
    dw_in = _mm(a, dproj, ta=True, out_dtype=BF16, name="hgrn_in_dw")
    da = _mm(dproj, w["w_in"], tb=True, name="hgrn_in_dx")
    return da, dict(w_in=dw_in, w_o=dw_o, o_norm=don, lb=dlb)


def _lower_bounds(lb_logits):
    p = jax.nn.softmax(lb_logits.astype(F32), axis=0)
    return jnp.cumsum(p, axis=0) - p[0]


def _rope_tables(positions):
    inv_freq = jnp.power(ROPE_BASE, -jnp.arange(0, MLA_ROPE, 2, dtype=F32) / MLA_ROPE)
    ang = positions.astype(F32)[:, None] * inv_freq
    cos, sin = jnp.cos(ang), jnp.sin(ang)
    zero = jnp.zeros((positions.shape[0], 128 - MLA_ROPE), F32)
    return (jnp.concatenate([cos, cos, zero], axis=-1), jnp.concatenate([-sin, sin, zero], axis=-1))


def _pad_mla_weights(w_in, w_uq):
    w_in_p = jnp.pad(w_in, ((0, 0), (0, 0), (0, 128 - MLA_ROPE)))
    n, ql, _ = w_uq.shape
    w_uq_p = jnp.pad(w_uq.reshape(n, ql, MLA_HEADS, MLA_NOPE + MLA_ROPE),
                     ((0, 0), (0, 0), (0, 0), (0, MLA_QK_PAD - MLA_NOPE - MLA_ROPE)))
    return w_in_p, w_uq_p.reshape(n, ql, MLA_HEADS * MLA_QK_PAD)


def _local_step(x, positions, target, wb, small):
    T, D = x.shape
    gains = small["norm_gains"]
    lbounds, lb_vjp = jax.vjp(_lower_bounds, small["hgrn_lb_logits"])
    cc, ss = _rope_tables(positions)
    w_in_p, w_uq_p = _pad_mla_weights(wb["mla_w_in"], wb["mla_w_uq"])

    def g(layer, i):
        return gains[layer, i][None, :]

    def mixer_weights(layer):
        slot = layer // 2
        if layer % 2 == 0:
            return dict(w_in=w_in_p[slot], w_uq=w_uq_p[slot], w_ukv=wb["mla_w_ukv"][slot],
                        w_o=wb["mla_w_o"][slot], q_norm=small["mla_q_norm"][slot][None, :],
                        kv_norm=small["mla_kv_norm"][slot][None, :])
        return dict(w_in=wb["hgrn_w_in"][slot], w_o=wb["hgrn_w_o"][slot],
                    o_norm=small["hgrn_o_norm"][slot][None, :])

    saved = []
    h = x
    a = _prenorm_fwd(x, g(0, 0))
    dy = sq = None
    for layer in range(DEPTH):
        mw = mixer_weights(layer)
        if layer % 2 == 0:
            m, mix_saved = _mla_fwd(a, mw, cc, ss)
        else:
            m, mix_saved = _hgrn_layer_fwd(a, mw, lbounds[layer][None, :])
        h1, a2 = _resnorm_fwd(h, m, g(layer, 1), g(layer, 2), name="resnorm_fwd_mix")
        u, mlp_saved = _mlp_fwd(a2, wb["mlp_w1"][layer], wb["mlp_w2"][layer])
        if layer + 1 < DEPTH:
            h2, a = _resnorm_fwd(h1, u, g(layer, 3), g(layer + 1, 0), name="resnorm_fwd_mlp")
        else:
            h2 = None
            dy, sq = _resnorm_loss(h1, u, g(layer, 3), target)
        saved.append((h, m, h1, u, mix_saved, mlp_saved))
        h = h2

    n_mla, n_hgrn = (DEPTH + 1) // 2, DEPTH // 2
    dgains = [[None] * 4 for _ in range(DEPTH)]
    gw = {k: [None] * n_mla for k in ("mla_w_in", "mla_w_uq", "mla_w_ukv", "mla_w_o", "mla_q_norm", "mla_kv_norm")}
    gw.update({k: [None] * n_hgrn for k in ("hgrn_w_in", "hgrn_w_o", "hgrn_o_norm")})
    gw["mlp_w1"] = [None] * DEPTH
    gw["mlp_w2"] = [None] * DEPTH
    dlb = [jnp.zeros((1, lbounds.shape[1]), F32) for _ in range(DEPTH)]
    dh = dy
    da_next = None
    for layer in reversed(range(DEPTH)):
        h0, m, h1, u, mix_saved, mlp_saved = saved[layer]
        slot = layer // 2
        mw = mixer_weights(layer)
        if da_next is None:
            du, dgains[layer][3] = _resnorm_bwd(u, g(layer, 3), dh, name="resnorm_bwd_last")
            t = dh
        else:
            h2 = saved[layer + 1][0]
            t, du, dgains[layer][3], dgains[layer + 1][0] = _resnorm_bwd(
                u, g(layer, 3), dh, h2, da_next, g(layer + 1, 0), name="resnorm_bwd_mlp")
        da2, gw["mlp_w1"][layer], gw["mlp_w2"][layer] = _mlp_bwd(
            du, mlp_saved, wb["mlp_w1"][layer], wb["mlp_w2"][layer])
        t, dm, dgains[layer][1], dgains[layer][2] = _resnorm_bwd(
            m, g(layer, 1), t, h1, da2, g(layer, 2), name="resnorm_bwd_mix")
        if layer % 2 == 0:
            da_next, mg = _mla_bwd(dm, mix_saved, mw, cc, ss)
            ql = mg["q_norm"].shape[-1]
            kvl = mg["kv_norm"].shape[-1]
            gw["mla_w_in"][slot] = mg["w_in"][:, :ql + kvl + MLA_ROPE]
            gw["mla_w_uq"][slot] = mg["w_uq"].reshape(ql, MLA_HEADS, MLA_QK_PAD)[
                :, :, :MLA_NOPE + MLA_ROPE].reshape(ql, MLA_HEADS * (MLA_NOPE + MLA_ROPE))
            gw["mla_w_ukv"][slot] = mg["w_ukv"]
            gw["mla_w_o"][slot] = mg["w_o"]
            gw["mla_q_norm"][slot] = mg["q_norm"][0]
            gw["mla_kv_norm"][slot] = mg["kv_norm"][0]
        else:
            da_next, hg = _hgrn_layer_bwd(dm, mix_saved, mw, lbounds[layer][None, :])
            gw["hgrn_w_in"][slot] = hg["w_in"]
            gw["hgrn_w_o"][slot] = hg["w_o"]
            gw["hgrn_o_norm"][slot] = hg["o_norm"][0]
            dlb[layer] = hg["lb"]
        dh = t
    grad_x, dgains[0][0] = _prenorm_bwd(x, g(0, 0), dh, da_next)

    grads = {k: jnp.stack(vs) for k, vs in gw.items()}
    grads["norm_gains"] = jnp.stack([jnp.concatenate(row, axis=0) for row in dgains])
    (grads["hgrn_lb_logits"],) = lb_vjp(jnp.concatenate(dlb, axis=0))
    return sq, grad_x, grads


def _size(shape):
    n = 1
    for d in shape:
        n *= d
    return n


def _piece_rows(shape):
    return -(-_size(shape) // PACK_W)


def _packed_rows(shapes):
    rows = sum(_piece_rows(s) for s in shapes)
    return -(-rows // PACK_ALIGN) * PACK_ALIGN


def _pack_blocks(pieces, rows, dtype):
    blocks, used = [], 0
    for p in pieces:
        flat = p.astype(dtype).reshape(-1)
        r = _piece_rows(p.shape)
        if r * PACK_W != flat.shape[0]:
            flat = jnp.pad(flat, (0, r * PACK_W - flat.shape[0]))
        blocks.append(flat.reshape(r, PACK_W))
        used += r
    if rows > used:
        blocks.append(jnp.zeros((rows - used, PACK_W), dtype))
    return blocks


def _unpack(buf, shapes):
    out, off = [], 0
    for shp in shapes:
        r = _piece_rows(shp)
        piece = buf[off:off + r]
        if r * PACK_W != _size(shp):
            piece = piece.reshape(-1)[:_size(shp)]
        out.append(piece.reshape(shp))
        off += r
    return out


def _mesh_place():
    x, y, c = lax.axis_index("x"), lax.axis_index("y"), lax.axis_index("c")
    chips = [(1 - x, y), (x, 1 - y), (1 - x, 1 - y)]
    return x, y, c, chips


_HBM = pl.BlockSpec(memory_space=pltpu.HBM)


def _all_gather(wp):
    R, W = wp.shape
    rh = R // 2
    rq = rh // 2
    assert rq % 16 == 0

    def body(w_ref, out_ref, send_sems, recv_sems, local_sem):
        x, y, c, _ = _mesh_place()
        me, jx, jy, jd = 2 * x + y, 2 * (1 - x) + y, 2 * x + (1 - y), 2 * (1 - x) + (1 - y)
        to_x, to_y, sibling = (1 - x, y, c), (x, 1 - y, c), (x, y, 1 - c)

        def rows(core, quarter):
            return pl.ds(pl.multiple_of(core * rh + quarter * rq, 16), rq)

        def slot(j, core, quarter):
            return out_ref.at[j, rows(core, quarter)]

        def copy(k, src, dst, to):
            return pltpu.make_async_remote_copy(src_ref=src, dst_ref=dst, send_sem=send_sems.at[k],
                                                recv_sem=recv_sems.at[k], device_id=to, device_id_type=MESH)

        own = pltpu.make_async_copy(w_ref, out_ref.at[me], local_sem)
        own.start()
        sends = [copy(0, w_ref.at[rows(c, 0)], slot(me, c, 0), to_x),
                 copy(2, w_ref.at[rows(c, 1)], slot(me, c, 1), to_y),
                 copy(1, w_ref.at[rows(c, 1)], slot(me, c, 1), to_x),
                 copy(3, w_ref.at[rows(c, 0)], slot(me, c, 0), to_y)]
        for cp in sends:
            cp.start()
        arrivals = [(0, slot(jx, c, 0), 4, to_y, 6), (2, slot(jy, c, 1), 5, to_x, 7),
                    (1, slot(jx, c, 1), None, None, 8), (3, slot(jy, c, 0), None, None, 9),
                    (4, slot(jd, c, 0), None, None, 10), (5, slot(jd, c, 1), None, None, 11)]
        for k, landed, k_on, to_on, k_sib in arrivals:
            copy(k, landed, landed, sibling).wait_recv()
            if k_on is not None:
                cp = copy(k_on, landed, landed, to_on)
                cp.start()
                sends.append(cp)
            cp = copy(k_sib, landed, landed, sibling)
            cp.start()
            sends.append(cp)
        for k_sib, j, quarter in ((6, jx, 0), (7, jy, 1), (8, jx, 1), (9, jy, 0), (10, jd, 0), (11, jd, 1)):
            landed = slot(j, 1 - c, quarter)
            copy(k_sib, landed, landed, sibling).wait_recv()
        for cp in sends:
            cp.wait_send()
        own.wait()

    return pl.pallas_call(
        body, name="weights_all_gather", in_specs=[_HBM], out_specs=_HBM,
        out_shape=jax.ShapeDtypeStruct((N_CHIPS, R, W), wp.dtype),
        scratch_shapes=[pltpu.SemaphoreType.DMA((12,)), pltpu.SemaphoreType.DMA((12,)),
                        pltpu.SemaphoreType.DMA],
    )(wp)


def _exchange_halves(g):
    n, _, rh, W = g.shape

    def body(g_ref, out_ref, send_sems, recv_sems):
        x, y, c, _ = _mesh_place()
        sibling = (x, y, 1 - c)
        copies = [pltpu.make_async_remote_copy(
            src_ref=g_ref.at[j, 1 - c], dst_ref=out_ref.at[j], send_sem=send_sems.at[j],
            recv_sem=recv_sems.at[j], device_id=sibling, device_id_type=MESH) for j in range(n)]
        for cp in copies:
            cp.start()
        for cp in copies:
            cp.wait()

    return pl.pallas_call(
        body, name="grads_to_sibling", in_specs=[_HBM], out_specs=_HBM,
        out_shape=jax.ShapeDtypeStruct((n, rh, W), g.dtype),
        scratch_shapes=[pltpu.SemaphoreType.DMA((n,)), pltpu.SemaphoreType.DMA((n,))],
    )(g)


def _scatter_to_owners(p):
    n, rh, W = p.shape
    rq = rh // 2
    assert rq % 16 == 0

    def body(p_ref, out_ref, stage_ref, send_sems, recv_sems, local_sem):
        x, y, c, _ = _mesh_place()
        me, jx, jy, jd = 2 * x + y, 2 * (1 - x) + y, 2 * x + (1 - y), 2 * (1 - x) + (1 - y)
        to_x, to_y = (1 - x, y, c), (x, 1 - y, c)

        def quarter(ref, j, q):
            return ref.at[j, pl.ds(q * rq, rq)]

        def copy(k, src, dst, to):
            return pltpu.make_async_remote_copy(src_ref=src, dst_ref=dst, send_sem=send_sems.at[k],
                                                recv_sem=recv_sems.at[k], device_id=to, device_id_type=MESH)

        own = pltpu.make_async_copy(p_ref.at[me], out_ref.at[me], local_sem)
        own.start()
        sends = [copy(2, quarter(p_ref, jd, 0), stage_ref.at[0], to_x),
                 copy(3, quarter(p_ref, jd, 1), stage_ref.at[1], to_y),
                 copy(0, p_ref.at[jx], out_ref.at[me], to_x),
                 copy(1, p_ref.at[jy], out_ref.at[me], to_y)]
        for cp in sends:
            cp.start()
        copy(2, stage_ref.at[0], stage_ref.at[0], to_x).wait_recv()
        relay = copy(4, stage_ref.at[0], quarter(out_ref, jx, 0), to_y)
        relay.start()
        sends.append(relay)
        copy(3, stage_ref.at[1], stage_ref.at[1], to_y).wait_recv()
        relay = copy(5, stage_ref.at[1], quarter(out_ref, jy, 1), to_x)
        relay.start()
        sends.append(relay)
        copy(0, out_ref.at[jx], out_ref.at[jx], to_x).wait_recv()
        copy(1, out_ref.at[jy], out_ref.at[jy], to_y).wait_recv()
        copy(4, quarter(out_ref, jd, 0), quarter(out_ref, jd, 0), to_y).wait_recv()
        copy(5, quarter(out_ref, jd, 1), quarter(out_ref, jd, 1), to_x).wait_recv()
        for cp in sends:
            cp.wait_send()
        own.wait()

    out, _ = pl.pallas_call(
        body, name="grads_to_owner", in_specs=[_HBM], out_specs=(_HBM, _HBM),
        out_shape=(jax.ShapeDtypeStruct((n, rh, W), p.dtype), jax.ShapeDtypeStruct((2, rq, W), p.dtype)),
        scratch_shapes=[pltpu.SemaphoreType.DMA((6,)), pltpu.SemaphoreType.DMA((6,)),
                        pltpu.SemaphoreType.DMA],
    )(p)
    return out


def _share_reduced(q):
    rh, W = q.shape

    def body(q_ref, out_ref, send_sem, recv_sem, local_sem):
        x, y, c, _ = _mesh_place()
        own = pltpu.make_async_copy(q_ref, out_ref.at[c], local_sem)
        own.start()
        cp = pltpu.make_async_remote_copy(src_ref=q_ref, dst_ref=out_ref.at[c], send_sem=send_sem,
                                          recv_sem=recv_sem, device_id=(x, y, 1 - c), device_id_type=MESH)
        cp.start()
        cp.wait()
        own.wait()

    return pl.pallas_call(
        body, name="grads_share_reduced", in_specs=[_HBM], out_specs=_HBM,
        out_shape=jax.ShapeDtypeStruct((2, rh, W), q.dtype),
        scratch_shapes=[pltpu.SemaphoreType.DMA, pltpu.SemaphoreType.DMA, pltpu.SemaphoreType.DMA],
    )(q)


PACK_TILE = 1280


def _add_sibling(g, recv, c_arr):
    n, _, rh, W = g.shape
    tr = PACK_TILE

    def body(c_ref, g_ref, r_ref, o_ref):
        o_ref[...] = (g_ref[...].astype(F32) + r_ref[...].astype(F32)).astype(BF16)

    return pl.pallas_call(
        body, name="grads_add_sibling",
        grid_spec=pltpu.PrefetchScalarGridSpec(
            num_scalar_prefetch=1, grid=(n, rh // tr),
            in_specs=[pl.BlockSpec((None, None, tr, W), lambda j, i, c_ref: (j, c_ref[0], i, 0)),
                      pl.BlockSpec((None, tr, W), lambda j, i, c_ref: (j, i, 0))],
            out_specs=pl.BlockSpec((None, tr, W), lambda j, i, c_ref: (j, i, 0))),
        out_shape=jax.ShapeDtypeStruct((n, rh, W), BF16),
        compiler_params=_cparams("parallel", "parallel"))(c_arr, g, recv)


def _sum_chips(parts):
    n, rh, W = parts.shape
    tr = PACK_TILE

    def body(p_ref, o_ref):
        acc = p_ref[0].astype(F32)
        for j in range(1, n):
            acc = acc + p_ref[j].astype(F32)
        o_ref[...] = acc

    return pl.pallas_call(
        body, name="grads_sum_chips", grid=(rh // tr,),
        in_specs=[pl.BlockSpec((n, tr, W), lambda i: (0, i, 0))],
        out_specs=pl.BlockSpec((tr, W), lambda i: (i, 0)),
        out_shape=jax.ShapeDtypeStruct((rh, W), F32),
        compiler_params=_cparams("parallel"))(parts)


def _adamw(w, g, m, v, name):
    shape = w.shape
    cols = shape[-1]
    w2, g2, m2, v2 = (t.reshape(-1, cols) for t in (w, g, m, v))
    rows = w2.shape[0]
    tr = rows
    for cand in (512, 256, 128, 64, 32, 16, 8):
        if rows > cand and rows % cand == 0:
            tr = cand
            break
    c1 = 1.0 / (1.0 - ADAM_B1 ** ADAM_STEP)
    c2 = 1.0 / (1.0 - ADAM_B2 ** ADAM_STEP)

    def body(w_ref, g_ref, m_ref, v_ref, d_ref, nm_ref, nv_ref):
        gv = g_ref[...]
        nm = ADAM_B1 * m_ref[...] + (1.0 - ADAM_B1) * gv
        nv = ADAM_B2 * v_ref[...] + (1.0 - ADAM_B2) * (gv * gv)
        nm_ref[...] = nm
        nv_ref[...] = nv
        d_ref[...] = -ADAM_LR * ((nm * c1) / (jnp.sqrt(nv * c2) + ADAM_EPS) + ADAM_WD * w_ref[...])

    blk = pl.BlockSpec((tr, cols), lambda i: (i, 0))
    sds = jax.ShapeDtypeStruct((rows, cols), F32)
    d, nm, nv = pl.pallas_call(body, name=name, grid=(rows // tr,), in_specs=[blk] * 4,
                               out_specs=(blk, blk, blk), out_shape=(sds, sds, sds),
                               compiler_params=_cparams("parallel"))(w2, g2, m2, v2)
    return d.reshape(shape), nm.reshape(shape), nv.reshape(shape)


def kernel(x, positions, norm_gains, mla_w_in, mla_q_norm, mla_kv_norm, mla_w_uq, mla_w_ukv, mla_w_o, hgrn_w_in, hgrn_lb_logits, hgrn_o_norm, hgrn_w_o, mlp_w1, mlp_w2, loss_target, m_norm_gains, m_mla_w_in, m_mla_q_norm, m_mla_kv_norm, m_mla_w_uq, m_mla_w_ukv, m_mla_w_o, m_hgrn_w_in, m_hgrn_lb_logits, m_hgrn_o_norm, m_hgrn_w_o, m_mlp_w1, m_mlp_w2, v_norm_gains, v_mla_w_in, v_mla_q_norm, v_mla_kv_norm, v_mla_w_uq, v_mla_w_ukv, v_mla_w_o, v_hgrn_w_in, v_hgrn_lb_logits, v_hgrn_o_norm, v_hgrn_w_o, v_mlp_w1, v_mlp_w2):
    w = dict(norm_gains=norm_gains, mla_w_in=mla_w_in, mla_q_norm=mla_q_norm, mla_kv_norm=mla_kv_norm,
             mla_w_uq=mla_w_uq, mla_w_ukv=mla_w_ukv, mla_w_o=mla_w_o, hgrn_w_in=hgrn_w_in,
             hgrn_lb_logits=hgrn_lb_logits, hgrn_o_norm=hgrn_o_norm, hgrn_w_o=hgrn_w_o,
             mlp_w1=mlp_w1, mlp_w2=mlp_w2)
    mom_m = dict(norm_gains=m_norm_gains, mla_w_in=m_mla_w_in, mla_q_norm=m_mla_q_norm,
                 mla_kv_norm=m_mla_kv_norm, mla_w_uq=m_mla_w_uq, mla_w_ukv=m_mla_w_ukv,
                 mla_w_o=m_mla_w_o, hgrn_w_in=m_hgrn_w_in, hgrn_lb_logits=m_hgrn_lb_logits,
                 hgrn_o_norm=m_hgrn_o_norm, hgrn_w_o=m_hgrn_w_o, mlp_w1=m_mlp_w1, mlp_w2=m_mlp_w2)
    mom_v = dict(norm_gains=v_norm_gains, mla_w_in=v_mla_w_in, mla_q_norm=v_mla_q_norm,
                 mla_kv_norm=v_mla_kv_norm, mla_w_uq=v_mla_w_uq, mla_w_ukv=v_mla_w_ukv,
                 mla_w_o=v_mla_w_o, hgrn_w_in=v_hgrn_w_in, hgrn_lb_logits=v_hgrn_lb_logits,
                 hgrn_o_norm=v_hgrn_o_norm, hgrn_w_o=v_hgrn_w_o, mlp_w1=v_mlp_w1, mlp_w2=v_mlp_w2)
    c = lax.axis_index("c")

    mats = [(name, axis) for name, axis in SHARDED if name != "norm_gains"]
    shard_shapes = [w[name].shape for name, _ in mats] + [norm_gains.shape, norm_gains.shape]
    w_rows = _packed_rows(shard_shapes)
    gain_bits = lax.bitcast_convert_type(norm_gains, jnp.uint32)
    gain_hi = lax.bitcast_convert_type((gain_bits >> 16).astype(jnp.uint16), BF16)
    gain_lo = lax.bitcast_convert_type((gain_bits & 0xFFFF).astype(jnp.uint16), BF16)
    wpack = jnp.concatenate(
        _pack_blocks([w[name] for name, _ in mats] + [gain_hi, gain_lo], w_rows, BF16), axis=0)
    gathered = _all_gather(wpack)
    per_chip = [_unpack(gathered[j], shard_shapes) for j in range(N_CHIPS)]
    wb = {}
    for i, (name, axis) in enumerate(mats):
        wb[name] = jnp.concatenate([per_chip[j][i] for j in range(N_CHIPS)], axis=axis)
    got_hi, got_lo = (lax.bitcast_convert_type(
        jnp.concatenate([per_chip[j][i] for j in range(N_CHIPS)], axis=2), jnp.uint16).astype(jnp.uint32)
        for i in (-2, -1))
    gains_full = lax.bitcast_convert_type((got_hi << 16) | got_lo, F32)
    small = dict(norm_gains=gains_full, mla_q_norm=mla_q_norm, mla_kv_norm=mla_kv_norm,
                 hgrn_lb_logits=hgrn_lb_logits, hgrn_o_norm=hgrn_o_norm)

    sq, grad_x, grads = _local_step(x[0], positions[0], loss_target[0], wb, small)
    d_model = x.shape[-1]
    loss = lax.psum(0.5 * jnp.sum(sq) / d_model, ("x", "y", "c"))

    grad_shapes = [w[name].shape for name, _ in SHARDED] + [w[name].shape for name in REPLICATED]
    g_rows = _packed_rows(grad_shapes)
    blocks = []
    for j in range(N_CHIPS):
        pieces = [jnp.split(grads[name], N_CHIPS, axis=axis)[j] for name, axis in SHARDED]
        pieces += [grads[name] for name in REPLICATED]
        blocks += _pack_blocks(pieces, g_rows, BF16)
    gpack = jnp.concatenate(blocks, axis=0).reshape(N_CHIPS, 2, g_rows // 2, PACK_W)
    from_sibling = _exchange_halves(gpack)
    chip_partial = _add_sibling(gpack, from_sibling, jnp.reshape(c, (1,)).astype(jnp.int32))
    from_chips = _scatter_to_owners(chip_partial)
    reduced_half = _sum_chips(from_chips)
    reduced = _share_reduced(reduced_half).reshape(g_rows, PACK_W)
    red = _unpack(reduced, grad_shapes)
    g_out = {name: red[i] for i, (name, _) in enumerate(SHARDED)}
    g_out.update({name: red[len(SHARDED) + i] for i, name in enumerate(REPLICATED)})

    deltas, new_m, new_v = {}, {}, {}
    for name in WEIGHTS:
        deltas[name], new_m[name], new_v[name] = _adamw(w[name], g_out[name], mom_m[name], mom_v[name],
                                                        name="adamw_" + name)
    return (loss, grad_x[None], *[g_out[n] for n in WEIGHTS], *[deltas[n] for n in WEIGHTS],
            *[new_m[n] for n in WEIGHTS], *[new_v[n] for n in WEIGHTS])
```

```python
import functools

import jax
import jax.numpy as jnp
from jax import lax
from jax.experimental import pallas as pl
from jax.experimental.pallas import tpu as pltpu

F32 = jnp.float32
BF16 = jnp.bfloat16
MESH = pl.DeviceIdType.MESH

DEPTH = 4
MLA_HEADS = 8
MLA_NOPE = 128
MLA_ROPE = 64
MLA_V = 128
MLA_QK_PAD = 256
MLA_HEADS_PER_STEP = 2
ROPE_BASE = 10000.0
HGRN_HEADS = 8
HGRN_CHUNK = 32
HGRN_BLOCK = 128
EPS = 1e-6

ADAM_LR = 0.001
ADAM_B1 = 0.9
ADAM_B2 = 0.999
ADAM_EPS = 1e-08
ADAM_WD = 0.01
ADAM_STEP = 10

N_CHIPS = 4
PACK_W = 1024
PACK_ALIGN = 2560
V7X_VMEM_LIMIT = 56 * 1024 * 1024

SHARDED = (("norm_gains", 2), ("mla_w_in", 1), ("mla_w_uq", 2), ("mla_w_ukv", 2), ("mla_w_o", 1),
           ("hgrn_w_in", 2), ("hgrn_w_o", 1), ("mlp_w1", 2), ("mlp_w2", 1))
REPLICATED = ("mla_q_norm", "mla_kv_norm", "hgrn_lb_logits", "hgrn_o_norm")
WEIGHTS = ("norm_gains", "mla_w_in", "mla_q_norm", "mla_kv_norm", "mla_w_uq", "mla_w_ukv", "mla_w_o",
           "hgrn_w_in", "hgrn_lb_logits", "hgrn_o_norm", "hgrn_w_o", "mlp_w1", "mlp_w2")


def _cparams(*semantics):
    return pltpu.CompilerParams(dimension_semantics=semantics, vmem_limit_bytes=V7X_VMEM_LIMIT)


def _sigmoid(x):
    return 1.0 / (1.0 + jnp.exp(-x))


def _mm(a, b, *, ta=False, tb=False, out_dtype=F32, tm=1024, tn=1024, tk=1024, epi=None, extra=None,
        name="mm"):
    if ta:
        K, M = a.shape
    else:
        M, K = a.shape
    if tb:
        N, Kb = b.shape
    else:
        Kb, N = b.shape
    assert K == Kb, (a.shape, b.shape, ta, tb)
    tm, tn = min(tm, M), min(tn, N)
    tk = K if K <= 1024 else tk
    assert M % tm == 0 and N % tn == 0 and K % tk == 0, (M, N, K, tm, tn, tk)
    nk = K // tk
    a_spec = (pl.BlockSpec((tk, tm), lambda i, j, k: (k, i)) if ta
              else pl.BlockSpec((tm, tk), lambda i, j, k: (i, k)))
    b_spec = (pl.BlockSpec((tn, tk), lambda i, j, k: (j, k)) if tb
              else pl.BlockSpec((tk, tn), lambda i, j, k: (k, j)))
    o_spec = pl.BlockSpec((tm, tn), lambda i, j, k: (i, j))
    dims = (((0 if ta else 1,), (1 if tb else 0,)), ((), ()))
    in_specs = [a_spec, b_spec]
    operands = [a, b]
    if epi == "mul2r":
        in_specs.append(o_spec)
        operands.append(extra)
    if epi == "relu2":
        out_shape = (jax.ShapeDtypeStruct((M, N), BF16), jax.ShapeDtypeStruct((M, N), BF16))
        out_specs = (o_spec, o_spec)
    elif epi == "mul2r":
        out_shape = jax.ShapeDtypeStruct((M, N), BF16)
        out_specs = o_spec
    else:
        out_shape = jax.ShapeDtypeStruct((M, N), out_dtype)
        out_specs = o_spec

    def body(*refs):
        a_ref, b_ref = refs[0], refs[1]
        acc_ref = refs[-1]
        k = pl.program_id(2)

        @pl.when(k == 0)
        def _():
            acc_ref[...] = jnp.zeros_like(acc_ref)

        acc_ref[...] += lax.dot_general(a_ref[...], b_ref[...], dims, preferred_element_type=F32)

        @pl.when(k == nk - 1)
        def _():
            acc = acc_ref[...]
            if epi == "relu2":
                r = jnp.maximum(acc, 0.0)
                refs[2][...] = (r * r).astype(BF16)
                refs[3][...] = r.astype(BF16)
            elif epi == "mul2r":
                refs[3][...] = (acc * (2.0 * refs[2][...].astype(F32))).astype(BF16)
            else:
                refs[2][...] = acc.astype(out_dtype)

    return pl.pallas_call(
        body, name=name, grid=(M // tm, N // tn, nk), in_specs=in_specs, out_specs=out_specs,
        out_shape=out_shape, scratch_shapes=[pltpu.VMEM((tm, tn), F32)],
        compiler_params=_cparams("parallel", "parallel", "arbitrary"))(*operands)


def _rms_rstd(x):
    return lax.rsqrt(jnp.mean(x * x, axis=-1, keepdims=True) + EPS)


def _rms_bwd_tile(x, g, dy):
    r = _rms_rstd(x)
    xh = x * r
    u = dy * g
    dx = r * (u - xh * jnp.mean(u * xh, axis=-1, keepdims=True))
    dg = jnp.sum(dy * xh, axis=0, keepdims=True)
    return dx, dg


def _row_tile(T):
    return min(256, T)


def _prenorm_fwd(x, g, name="prenorm_fwd"):
    T, D = x.shape
    tm = _row_tile(T)

    def body(x_ref, g_ref, a_ref):
        xv = x_ref[...]
        a_ref[...] = (xv * _rms_rstd(xv) * g_ref[...]).astype(BF16)

    row = pl.BlockSpec((tm, D), lambda i: (i, 0))
    vec = pl.BlockSpec((1, D), lambda i: (0, 0))
    return pl.pallas_call(body, name=name, grid=(T // tm,), in_specs=[row, vec], out_specs=row,
                          out_shape=jax.ShapeDtypeStruct((T, D), BF16),
                          compiler_params=_cparams("parallel"))(x, g)


def _resnorm_fwd(h, z, g_post, g_pre, name="resnorm_fwd"):
    T, D = h.shape
    tm = _row_tile(T)

    def body(h_ref, z_ref, gp_ref, gn_ref, hn_ref, a_ref):
        zv = z_ref[...]
        hn = h_ref[...] + zv * _rms_rstd(zv) * gp_ref[...]
        hn_ref[...] = hn
        a_ref[...] = (hn * _rms_rstd(hn) * gn_ref[...]).astype(BF16)

    row = pl.BlockSpec((tm, D), lambda i: (i, 0))
    vec = pl.BlockSpec((1, D), lambda i: (0, 0))
    return pl.pallas_call(body, name=name, grid=(T // tm,), in_specs=[row, row, vec, vec],
                          out_specs=(row, row),
                          out_shape=(jax.ShapeDtypeStruct((T, D), F32), jax.ShapeDtypeStruct((T, D), BF16)),
                          compiler_params=_cparams("parallel"))(h, z, g_post, g_pre)


def _resnorm_loss(h, z, g_post, target, name="resnorm_loss"):
    T, D = h.shape
    tm = _row_tile(T)

    def body(h_ref, z_ref, gp_ref, t_ref, dy_ref, sq_ref):
        zv = z_ref[...]
        err = h_ref[...] + zv * _rms_rstd(zv) * gp_ref[...] - t_ref[...]
        dy_ref[...] = err * (1.0 / D)

        @pl.when(pl.program_id(0) == 0)
        def _():
            sq_ref[...] = jnp.zeros_like(sq_ref)

        sq_ref[...] += jnp.sum(err * err, axis=0, keepdims=True)

    row = pl.BlockSpec((tm, D), lambda i: (i, 0))
    vec = pl.BlockSpec((1, D), lambda i: (0, 0))
    return pl.pallas_call(body, name=name, grid=(T // tm,), in_specs=[row, row, vec, row],
                          out_specs=(row, vec),
                          out_shape=(jax.ShapeDtypeStruct((T, D), F32), jax.ShapeDtypeStruct((1, D), F32)),
                          compiler_params=_cparams("arbitrary"))(h, z, g_post, target)


def _resnorm_bwd(z, g_post, dh, h_new=None, da=None, g_pre=None, name="resnorm_bwd"):
    T, D = z.shape
    tm = _row_tile(T)
    has_next = h_new is not None
    row = pl.BlockSpec((tm, D), lambda i: (i, 0))
    vec = pl.BlockSpec((1, D), lambda i: (0, 0))

    if has_next:
        def body(z_ref, gp_ref, dh_ref, hn_ref, da_ref, gn_ref, t_ref, dz_ref, dgp_ref, dgn_ref):
            first = pl.program_id(0) == 0

            @pl.when(first)
            def _():
                dgp_ref[...] = jnp.zeros_like(dgp_ref)
                dgn_ref[...] = jnp.zeros_like(dgn_ref)

            dpre, dgn = _rms_bwd_tile(hn_ref[...], gn_ref[...], da_ref[...])
            t = dh_ref[...] + dpre
            t_ref[...] = t
            dz, dgp = _rms_bwd_tile(z_ref[...], gp_ref[...], t)
            dz_ref[...] = dz.astype(BF16)
            dgp_ref[...] += dgp
            dgn_ref[...] += dgn

        return pl.pallas_call(
            body, name=name, grid=(T // tm,), in_specs=[row, vec, row, row, row, vec],
            out_specs=(row, row, vec, vec),
            out_shape=(jax.ShapeDtypeStruct((T, D), F32), jax.ShapeDtypeStruct((T, D), BF16),
                       jax.ShapeDtypeStruct((1, D), F32), jax.ShapeDtypeStruct((1, D), F32)),
            compiler_params=_cparams("arbitrary"))(z, g_post, dh, h_new, da, g_pre)

    def body_last(z_ref, gp_ref, dh_ref, dz_ref, dgp_ref):
        @pl.when(pl.program_id(0) == 0)
        def _():
            dgp_ref[...] = jnp.zeros_like(dgp_ref)

        dz, dgp = _rms_bwd_tile(z_ref[...], gp_ref[...], dh_ref[...])
        dz_ref[...] = dz.astype(BF16)
        dgp_ref[...] += dgp

    return pl.pallas_call(
        body_last, name=name, grid=(T // tm,), in_specs=[row, vec, row], out_specs=(row, vec),
        out_shape=(jax.ShapeDtypeStruct((T, D), BF16), jax.ShapeDtypeStruct((1, D), F32)),
        compiler_params=_cparams("arbitrary"))(z, g_post, dh)


def _prenorm_bwd(x, g, dh, da, name="prenorm_bwd"):
    T, D = x.shape
    tm = _row_tile(T)

    def body(x_ref, g_ref, dh_ref, da_ref, dx_ref, dg_ref):
        @pl.when(pl.program_id(0) == 0)
        def _():
            dg_ref[...] = jnp.zeros_like(dg_ref)

        dpre, dg = _rms_bwd_tile(x_ref[...], g_ref[...], da_ref[...])
        dx_ref[...] = dh_ref[...] + dpre
        dg_ref[...] += dg

    row = pl.BlockSpec((tm, D), lambda i: (i, 0))
    vec = pl.BlockSpec((1, D), lambda i: (0, 0))
    return pl.pallas_call(
        body, name=name, grid=(T // tm,), in_specs=[row, vec, row, row], out_specs=(row, vec),
        out_shape=(jax.ShapeDtypeStruct((T, D), F32), jax.ShapeDtypeStruct((1, D), F32)),
        compiler_params=_cparams("arbitrary"))(x, g, dh, da)


def _mlp_fwd(a, w1, w2):
    act, r = _mm(a, w1, epi="relu2", name="mlp_up")
    u = _mm(act, w2, name="mlp_down")
    return u, (a, act, r)


def _mlp_bwd(du, saved, w1, w2):
    a, act, r = saved
    dz1 = _mm(du, w2, tb=True, epi="mul2r", extra=r, name="mlp_down_dx")
    dw2 = _mm(act, du, ta=True, out_dtype=BF16, name="mlp_down_dw")
    dw1 = _mm(a, dz1, ta=True, out_dtype=BF16, name="mlp_up_dw")
    da = _mm(dz1, w1, tb=True, name="mlp_up_dx")
    return da, dw1, dw2


def _rope_swap(t):
    n = t.shape[-1]
    lane = lax.broadcasted_iota(jnp.int32, t.shape, t.ndim - 1)
    half = MLA_ROPE // 2
    first = (lane & (MLA_ROPE - 1)) < half
    return jnp.where(first, pltpu.roll(t, n - half, t.ndim - 1), pltpu.roll(t, half, t.ndim - 1))


def _mla_mid_fwd(proj, q_norm, kv_norm, w_uq, w_ukv, cc, ss):
    T, PW = proj.shape
    QL, KVL = q_norm.shape[-1], kv_norm.shape[-1]
    H = MLA_HEADS
    assert PW == QL + KVL + 128
    tm = _row_tile(T)

    def body(p_ref, qn_ref, kn_ref, wq_ref, wkv_ref, cc_ref, ss_ref,
             cq_ref, ckv_ref, q_ref, k_ref, v_ref):
        cq = p_ref[:, 0:QL]
        ckv = p_ref[:, QL:QL + KVL]
        kr = p_ref[:, QL + KVL:QL + KVL + 128]
        c, s = cc_ref[...], ss_ref[...]
        cqn = (cq * _rms_rstd(cq) * qn_ref[...]).astype(BF16)
        ckvn = (ckv * _rms_rstd(ckv) * kn_ref[...]).astype(BF16)
        cq_ref[...] = cqn
        ckv_ref[...] = ckvn
        q = jnp.dot(cqn, wq_ref[...], preferred_element_type=F32)
        kv = jnp.dot(ckvn, wkv_ref[...], preferred_element_type=F32)
        krf = (kr * c + _rope_swap(kr) * s).astype(BF16)
        for h in range(H):
            o = h * MLA_QK_PAD
            q_ref[:, o:o + MLA_NOPE] = q[:, o:o + MLA_NOPE].astype(BF16)
            qr = q[:, o + MLA_NOPE:o + MLA_QK_PAD]
            q_ref[:, o + MLA_NOPE:o + MLA_QK_PAD] = (qr * c + _rope_swap(qr) * s).astype(BF16)
            k_ref[:, o:o + MLA_NOPE] = kv[:, o:o + MLA_NOPE].astype(BF16)
            k_ref[:, o + MLA_NOPE:o + MLA_QK_PAD] = krf
            v_ref[:, h * MLA_V:(h + 1) * MLA_V] = kv[:, o + MLA_NOPE:o + MLA_QK_PAD].astype(BF16)

    def row(w):
        return pl.BlockSpec((tm, w), lambda i: (i, 0))

    def full(shape):
        return pl.BlockSpec(shape, lambda i: (0, 0))

    return pl.pallas_call(
        body, name="mla_mid_fwd", grid=(T // tm,),
        in_specs=[row(PW), full((1, QL)), full((1, KVL)), full(w_uq.shape), full(w_ukv.shape),
                  row(128), row(128)],
        out_specs=(row(QL), row(KVL), row(H * MLA_QK_PAD), row(H * MLA_QK_PAD), row(H * MLA_V)),
        out_shape=(jax.ShapeDtypeStruct((T, QL), BF16), jax.ShapeDtypeStruct((T, KVL), BF16),
                   jax.ShapeDtypeStruct((T, H * MLA_QK_PAD), BF16),
                   jax.ShapeDtypeStruct((T, H * MLA_QK_PAD), BF16),
                   jax.ShapeDtypeStruct((T, H * MLA_V), BF16)),
        compiler_params=_cparams("parallel"))(proj, q_norm, kv_norm, w_uq, w_ukv, cc, ss)


def _mla_mid_bwd(proj, q_norm, kv_norm, w_uq, w_ukv, cc, ss, dq, dk, dv):
    T, PW = proj.shape
    QL, KVL = q_norm.shape[-1], kv_norm.shape[-1]
    H = MLA_HEADS
    tm = _row_tile(T)
    nt = (((1,), (1,)), ((), ()))

    def body(p_ref, qn_ref, kn_ref, wq_ref, wkv_ref, cc_ref, ss_ref, dq_ref, dk_ref, dv_ref,
             dqp_ref, dkv_ref, dp_ref, dqn_ref, dkn_ref):
        @pl.when(pl.program_id(0) == 0)
        def _():
            dqn_ref[...] = jnp.zeros_like(dqn_ref)
            dkn_ref[...] = jnp.zeros_like(dkn_ref)

        c, s = cc_ref[...], ss_ref[...]
        dkr = jnp.zeros((tm, 128), F32)
        for h in range(H):
            o = h * MLA_QK_PAD
            dqp_ref[:, o:o + MLA_NOPE] = dq_ref[:, o:o + MLA_NOPE].astype(BF16)
            dqr = dq_ref[:, o + MLA_NOPE:o + MLA_QK_PAD]
            dqp_ref[:, o + MLA_NOPE:o + MLA_QK_PAD] = (dqr * c + _rope_swap(dqr * s)).astype(BF16)
            dkv_ref[:, o:o + MLA_NOPE] = dk_ref[:, o:o + MLA_NOPE].astype(BF16)
            dkv_ref[:, o + MLA_NOPE:o + MLA_QK_PAD] = dv_ref[:, h * MLA_V:(h + 1) * MLA_V].astype(BF16)
            dkr = dkr + dk_ref[:, o + MLA_NOPE:o + MLA_QK_PAD]
        dcqn = lax.dot_general(dqp_ref[...], wq_ref[...], nt, preferred_element_type=F32)
        dckvn = lax.dot_general(dkv_ref[...], wkv_ref[...], nt, preferred_element_type=F32)
        dcq, dqn = _rms_bwd_tile(p_ref[:, 0:QL], qn_ref[...], dcqn)
        dckv, dkn = _rms_bwd_tile(p_ref[:, QL:QL + KVL], kn_ref[...], dckvn)
        dp_ref[:, 0:QL] = dcq.astype(BF16)
        dp_ref[:, QL:QL + KVL] = dckv.astype(BF16)
        dp_ref[:, QL + KVL:QL + KVL + 128] = (dkr * c + _rope_swap(dkr * s)).astype(BF16)
        dqn_ref[...] += dqn
        dkn_ref[...] += dkn

    def row(w):
        return pl.BlockSpec((tm, w), lambda i: (i, 0))

    def full(shape):
        return pl.BlockSpec(shape, lambda i: (0, 0))

    return pl.pallas_call(
        body, name="mla_mid_bwd", grid=(T // tm,),
        in_specs=[row(PW), full((1, QL)), full((1, KVL)), full(w_uq.shape), full(w_ukv.shape),
                  row(128), row(128), row(H * MLA_QK_PAD), row(H * MLA_QK_PAD), row(H * MLA_V)],
        out_specs=(row(H * MLA_QK_PAD), row(H * MLA_QK_PAD), row(PW), full((1, QL)), full((1, KVL))),
        out_shape=(jax.ShapeDtypeStruct((T, H * MLA_QK_PAD), BF16),
                   jax.ShapeDtypeStruct((T, H * MLA_QK_PAD), BF16),
                   jax.ShapeDtypeStruct((T, PW), BF16),
                   jax.ShapeDtypeStruct((1, QL), F32), jax.ShapeDtypeStruct((1, KVL), F32)),
        compiler_params=_cparams("arbitrary"))(proj, q_norm, kv_norm, w_uq, w_ukv, cc, ss, dq, dk, dv)


def _attn_tile(T):
    return min(1024, T)


def _attn_pairs(n, by_key):
    if by_key:
        pairs = [(qi, ki) for ki in range(n) for qi in range(ki, n)]
    else:
        pairs = [(qi, ki) for qi in range(n) for ki in range(qi + 1)]
    return (jnp.asarray([p[0] for p in pairs], jnp.int32), jnp.asarray([p[1] for p in pairs], jnp.int32))


def _scores(q, k, scale, diagonal):
    s = lax.dot_general(q, k, (((1,), (1,)), ((), ())), preferred_element_type=F32) * scale
    if diagonal:
        rows = lax.broadcasted_iota(jnp.int32, s.shape, 0)
        cols = lax.broadcasted_iota(jnp.int32, s.shape, 1)
        s = jnp.where(rows >= cols, s, -jnp.inf)
    return s


def _attn_fwd(q, k, v):
    T = q.shape[0]
    H, DQ, DV = MLA_HEADS, MLA_QK_PAD, MLA_V
    tq = _attn_tile(T)
    nq = T // tq
    scale = float(MLA_NOPE + MLA_ROPE) ** -0.5
    G = MLA_HEADS_PER_STEP
    qi_tab, ki_tab = _attn_pairs(nq, by_key=False)

    def body(qi_ref, ki_ref, q_ref, k_ref, v_ref, o_ref, lse_ref, *scratch):
        m_refs, l_refs, acc_refs = scratch[0:G], scratch[G:2 * G], scratch[2 * G:3 * G]
        p = pl.program_id(1)
        qi, ki = qi_ref[p], ki_ref[p]

        @pl.when(ki == 0)
        def _():
            for g in range(G):
                m_refs[g][...] = jnp.full_like(m_refs[g], -jnp.inf)
                l_refs[g][...] = jnp.zeros_like(l_refs[g])
                acc_refs[g][...] = jnp.zeros_like(acc_refs[g])

        def update(diagonal):
            for g in range(G):
                qs, vs = slice(g * DQ, (g + 1) * DQ), slice(g * DV, (g + 1) * DV)
                s = _scores(q_ref[:, qs], k_ref[:, qs], scale, diagonal)
                m_prev = m_refs[g][...]
                m_new = jnp.maximum(m_prev, jnp.max(s, axis=1, keepdims=True))
                alpha = jnp.exp(m_prev - m_new)
                pr = jnp.exp(s - m_new)
                l_refs[g][...] = alpha * l_refs[g][...] + jnp.sum(pr, axis=1, keepdims=True)
                acc_refs[g][...] = alpha * acc_refs[g][...] + jnp.dot(pr.astype(BF16), v_ref[:, vs],
                                                                      preferred_element_type=F32)
                m_refs[g][...] = m_new

        @pl.when(ki < qi)
        def _():
            update(False)

        @pl.when(ki == qi)
        def _():
            update(True)
            for g in range(G):
                vs = slice(g * DV, (g + 1) * DV)
                o_ref[:, vs] = (acc_refs[g][...] / l_refs[g][...]).astype(BF16)
                lse_ref[g] = m_refs[g][...] + jnp.log(l_refs[g][...])

    return pl.pallas_call(
        body, name="attn_fwd",
        grid_spec=pltpu.PrefetchScalarGridSpec(
            num_scalar_prefetch=2, grid=(H // G, int(qi_tab.shape[0])),
            in_specs=[pl.BlockSpec((tq, G * DQ), lambda h, p, qt, kt: (qt[p], h)),
                      pl.BlockSpec((tq, G * DQ), lambda h, p, qt, kt: (kt[p], h)),
                      pl.BlockSpec((tq, G * DV), lambda h, p, qt, kt: (kt[p], h))],
            out_specs=(pl.BlockSpec((tq, G * DV), lambda h, p, qt, kt: (qt[p], h)),
                       pl.BlockSpec((G, tq, 1), lambda h, p, qt, kt: (h, qt[p], 0))),
            scratch_shapes=([pltpu.VMEM((tq, 1), F32)] * (2 * G) + [pltpu.VMEM((tq, DV), F32)] * G)),
        out_shape=(jax.ShapeDtypeStruct((T, H * DV), BF16), jax.ShapeDtypeStruct((H, T, 1), F32)),
        compiler_params=_cparams("parallel", "arbitrary"))(qi_tab, ki_tab, q, k, v)


def _attn_bwd(q, k, v, o, do, lse):
    T = q.shape[0]
    H, DQ, DV = MLA_HEADS, MLA_QK_PAD, MLA_V
    tq = _attn_tile(T)
    nq = T // tq
    scale = float(MLA_NOPE + MLA_ROPE) ** -0.5
    tn = (((0,), (0,)), ((), ()))
    nt = (((1,), (1,)), ((), ()))
    G = MLA_HEADS_PER_STEP
    qi_tab, ki_tab = _attn_pairs(nq, by_key=True)

    def body(qi_ref, ki_ref, q_ref, k_ref, v_ref, o_ref, do_ref, lse_ref, dq_ref, dk_ref, dv_ref,
             dk_acc, dv_acc):
        p = pl.program_id(1)
        qi, ki = qi_ref[p], ki_ref[p]

        @pl.when(p == 0)
        def _():
            dq_ref[...] = jnp.zeros_like(dq_ref)

        @pl.when(qi == ki)
        def _():
            dk_acc[...] = jnp.zeros_like(dk_acc)
            dv_acc[...] = jnp.zeros_like(dv_acc)

        def step(diagonal):
            rows = pl.ds(pl.multiple_of(qi * tq, tq), tq)
            for g in range(G):
                qs, vs = slice(g * DQ, (g + 1) * DQ), slice(g * DV, (g + 1) * DV)
                dof = do_ref[:, vs]
                delta = jnp.sum(dof.astype(F32) * o_ref[:, vs].astype(F32), axis=1, keepdims=True)
                s = _scores(q_ref[:, qs], k_ref[:, qs], scale, diagonal)
                pr = jnp.exp(s - lse_ref[g])
                dp = lax.dot_general(dof, v_ref[:, vs], nt, preferred_element_type=F32)
                ds = (pr * (dp - delta) * scale).astype(BF16)
                dv_acc[:, vs] += lax.dot_general(pr.astype(BF16), dof, tn, preferred_element_type=F32)
                dk_acc[:, qs] += lax.dot_general(ds, q_ref[:, qs], tn, preferred_element_type=F32)
                dq_ref[rows, qs] += jnp.dot(ds, k_ref[:, qs], preferred_element_type=F32)

        @pl.when(qi == ki)
        def _():
            step(True)

        @pl.when(qi > ki)
        def _():
            step(False)

        @pl.when(qi == nq - 1)
        def _():
            dk_ref[...] = dk_acc[...]
            dv_ref[...] = dv_acc[...]

    qspec = pl.BlockSpec((tq, G * DQ), lambda h, p, qt, kt: (qt[p], h))
    ospec = pl.BlockSpec((tq, G * DV), lambda h, p, qt, kt: (qt[p], h))
    kspec = pl.BlockSpec((tq, G * DQ), lambda h, p, qt, kt: (kt[p], h))
    vspec = pl.BlockSpec((tq, G * DV), lambda h, p, qt, kt: (kt[p], h))
    return pl.pallas_call(
        body, name="attn_bwd",
        grid_spec=pltpu.PrefetchScalarGridSpec(
            num_scalar_prefetch=2, grid=(H // G, int(qi_tab.shape[0])),
            in_specs=[qspec, kspec, vspec, ospec, ospec,
                      pl.BlockSpec((G, tq, 1), lambda h, p, qt, kt: (h, qt[p], 0))],
            out_specs=(pl.BlockSpec((T, G * DQ), lambda h, p, qt, kt: (0, h)), kspec, vspec),
            scratch_shapes=[pltpu.VMEM((tq, G * DQ), F32), pltpu.VMEM((tq, G * DV), F32)]),
        out_shape=(jax.ShapeDtypeStruct((T, H * DQ), F32), jax.ShapeDtypeStruct((T, H * DQ), F32),
                   jax.ShapeDtypeStruct((T, H * DV), F32)),
        compiler_params=_cparams("parallel", "arbitrary"))(qi_tab, ki_tab, q, k, v, o, do, lse)


def _mla_fwd(a, w, cc, ss):
    proj = _mm(a, w["w_in"], name="mla_in")
    cqn, ckvn, q, k, v = _mla_mid_fwd(proj, w["q_norm"], w["kv_norm"], w["w_uq"], w["w_ukv"], cc, ss)
    o, lse = _attn_fwd(q, k, v)
    m = _mm(o, w["w_o"], name="mla_out")
    return m, (a, proj, cqn, ckvn, q, k, v, o, lse)


def _mla_bwd(dm, saved, w, cc, ss):
    a, proj, cqn, ckvn, q, k, v, o, lse = saved
    do = _mm(dm, w["w_o"], tb=True, out_dtype=BF16, name="mla_out_dx")
    dw_o = _mm(o, dm, ta=True, out_dtype=BF16, name="mla_out_dw")
    dq, dk, dv = _attn_bwd(q, k, v, o, do, lse)
    dqp, dkv, dproj, dqn, dkn = _mla_mid_bwd(proj, w["q_norm"], w["kv_norm"], w["w_uq"], w["w_ukv"],
                                             cc, ss, dq, dk, dv)
    dw_uq = _mm(cqn, dqp, ta=True, out_dtype=BF16, name="mla_uq_dw")
    dw_ukv = _mm(ckvn, dkv, ta=True, out_dtype=BF16, name="mla_ukv_dw")
    dw_in = _mm(a, dproj, ta=True, out_dtype=BF16, name="mla_in_dw")
    da = _mm(dproj, w["w_in"], tb=True, name="mla_in_dx")
    return da, dict(w_in=dw_in, w_uq=dw_uq, w_ukv=dw_ukv, w_o=dw_o, q_norm=dqn, kv_norm=dkn)


def _split_dot(mat, x, parts):
    acc = None
    rem = x
    for _ in range(parts):
        piece = rem.astype(BF16)
        term = jnp.dot(mat, piece, preferred_element_type=F32)
        acc = term if acc is None else acc + term
        rem = rem - piece.astype(F32)
    return acc


def _chunk_mats(tb):
    C = HGRN_CHUNK
    assert C & (C - 1) == 0
    r = lax.broadcasted_iota(jnp.int32, (tb, tb), 0)
    s = lax.broadcasted_iota(jnp.int32, (tb, tb), 1)
    start = r & ~(C - 1)
    same = start == (s & ~(C - 1))
    ref = start + C // 2
    last = start + C - 1
    one, zero = jnp.float32(1.0), jnp.float32(0.0)
    cum = jnp.where(same & (s <= r), one, zero)
    rel = cum - jnp.where(same & (s <= ref), one, zero)
    rest = jnp.where(same & (s > r) & (s <= last), one, zero)
    rev = jnp.where(same & (s >= r), one, zero)
    ones = jnp.where(same, one, zero)
    causal = same & (s <= r)
    return cum, rel, rest, rev, ones, causal


def _hgrn_gates(p_ref, lb, HK):
    qx = p_ref[:, 0:HK]
    fx = p_ref[:, HK:2 * HK]
    sf = _sigmoid(fx)
    f = lb + (1.0 - lb) * sf
    sq = _sigmoid(qx)
    return qx, sq, qx * sq, sf, f, 1.0 - f, jnp.log(f)


def _hgrn_fwd(proj, lb, o_norm):
    T = proj.shape[0]
    H, C = HGRN_HEADS, HGRN_CHUNK
    HK = proj.shape[1] // 4
    DK = HK // H
    tb = min(HGRN_BLOCK, T)
    ncb = tb // C
    nt = (((1,), (1,)), ((), ()))
    tn = (((0,), (0,)), ((), ()))

    def body(p_ref, lb_ref, on_ref, y_ref, o_ref, st_ref, state, oacc):
        @pl.when(pl.program_id(0) == 0)
        def _():
            state[...] = jnp.zeros_like(state)

        cum, rel, rest, _, _, causal = _chunk_mats(tb)
        _, _, q, _, f, k, logf = _hgrn_gates(p_ref, lb_ref[...], HK)
        b = _split_dot(cum.astype(BF16), logf, 3)
        brel = _split_dot(rel.astype(BF16), logf, 3)
        brest = _split_dot(rest.astype(BF16), logf, 3)
        eb = jnp.exp(b)
        q_rel = (q * jnp.exp(brel)).astype(BF16)
        k_rel = (k * jnp.exp(-brel)).astype(BF16)
        q_dec = (q * eb).astype(BF16)
        k_dec = (k * jnp.exp(brest)).astype(BF16)
        v = p_ref[:, 2 * HK:3 * HK].astype(BF16)
        for h in range(H):
            hs = slice(h * DK, (h + 1) * DK)
            a = lax.dot_general(q_rel[:, hs], k_rel[:, hs], nt, preferred_element_type=F32)
            a = jnp.where(causal, a, 0.0).astype(BF16)
            oacc[:, hs] = jnp.dot(a, v[:, hs], preferred_element_type=F32)
            for j in range(ncb):
                rs = slice(j * C, (j + 1) * C)
                st = state[h]
                st_ref[j, h] = st
                oacc[rs, hs] += lax.dot_general(q_dec[rs, hs], st.astype(BF16), nt,
                                                preferred_element_type=F32)
                dec = jnp.exp(jnp.sum(logf[rs, hs], axis=0, keepdims=True))
                state[h] = dec * st + lax.dot_general(v[rs, hs], k_dec[rs, hs], tn,
                                                      preferred_element_type=F32)
        o = oacc[...]
        o_ref[...] = o
        gx = p_ref[:, 3 * HK:4 * HK]
        gate = gx * _sigmoid(gx)
        for h in range(H):
            hs = slice(h * DK, (h + 1) * DK)
            oh = o[:, hs]
            y_ref[:, hs] = (oh * _rms_rstd(oh) * on_ref[...] * gate[:, hs]).astype(BF16)

    return pl.pallas_call(
        body, name="hgrn_fwd", grid=(T // tb,),
        in_specs=[pl.BlockSpec((tb, 4 * HK), lambda i: (i, 0)),
                  pl.BlockSpec((1, HK), lambda i: (0, 0)),
                  pl.BlockSpec((1, DK), lambda i: (0, 0))],
        out_specs=(pl.BlockSpec((tb, HK), lambda i: (i, 0)),
                   pl.BlockSpec((tb, HK), lambda i: (i, 0)),
                   pl.BlockSpec((ncb, H, DK, DK), lambda i: (i, 0, 0, 0))),
        out_shape=(jax.ShapeDtypeStruct((T, HK), BF16), jax.ShapeDtypeStruct((T, HK), F32),
                   jax.ShapeDtypeStruct((T // C, H, DK, DK), F32)),
        scratch_shapes=[pltpu.VMEM((H, DK, DK), F32), pltpu.VMEM((tb, HK), F32)],
        compiler_params=_cparams("arbitrary"))(proj, lb, o_norm)


def _hgrn_bwd(proj, lb, o_norm, o, states, dy):
    T = proj.shape[0]
    H, C = HGRN_HEADS, HGRN_CHUNK
    HK = proj.shape[1] // 4
    DK = HK // H
    tb = min(HGRN_BLOCK, T)
    ncb = tb // C
    nb = T // tb
    nt = (((1,), (1,)), ((), ()))
    tn = (((0,), (0,)), ((), ()))

    def body(p_ref, lb_ref, on_ref, o_ref, st_ref, dy_ref, dp_ref, dlb_ref, don_ref,
             dstate, dqr_s, dkr_s, dqd_s, dkd_s, dv_s, do_s, e_s):
        @pl.when(pl.program_id(0) == 0)
        def _():
            dstate[...] = jnp.zeros_like(dstate)
            dlb_ref[...] = jnp.zeros_like(dlb_ref)
            don_ref[...] = jnp.zeros_like(don_ref)

        cum, rel, rest, rev, ones, causal = _chunk_mats(tb)
        lb = lb_ref[...]
        qx, sq, q, sf, f, k, logf = _hgrn_gates(p_ref, lb, HK)
        b = _split_dot(cum.astype(BF16), logf, 3)
        brel = _split_dot(rel.astype(BF16), logf, 3)
        brest = _split_dot(rest.astype(BF16), logf, 3)
        eb = jnp.exp(b)
        erel = jnp.exp(brel)
        enrel = jnp.exp(-brel)
        erest = jnp.exp(brest)
        q_rel_f, k_rel_f, q_dec_f, k_dec_f = q * erel, k * enrel, q * eb, k * erest
        q_rel, k_rel = q_rel_f.astype(BF16), k_rel_f.astype(BF16)
        q_dec, k_dec = q_dec_f.astype(BF16), k_dec_f.astype(BF16)
        v = p_ref[:, 2 * HK:3 * HK].astype(BF16)

        gx = p_ref[:, 3 * HK:4 * HK]
        sg = _sigmoid(gx)
        gate = gx * sg
        dy = dy_ref[...]
        ov = o_ref[...]
        on = on_ref[...]
        don = jnp.zeros((1, DK), F32)
        for h in range(H):
            hs = slice(h * DK, (h + 1) * DK)
            oh = ov[:, hs]
            r = _rms_rstd(oh)
            xh = oh * r
            d_on = dy[:, hs] * gate[:, hs]
            don = don + jnp.sum(d_on * xh, axis=0, keepdims=True)
            u = d_on * on
            do_s[:, hs] = r * (u - xh * jnp.mean(u * xh, axis=-1, keepdims=True))
            dp_ref[:, 3 * HK + h * DK:3 * HK + (h + 1) * DK] = (
                dy[:, hs] * xh * on * (sg[:, hs] * (1.0 + gx[:, hs] * (1.0 - sg[:, hs])))).astype(BF16)
        don_ref[...] += don

        for h in range(H):
            hs = slice(h * DK, (h + 1) * DK)
            doh = do_s[:, hs].astype(BF16)
            a = lax.dot_general(q_rel[:, hs], k_rel[:, hs], nt, preferred_element_type=F32)
            a = jnp.where(causal, a, 0.0).astype(BF16)
            da = lax.dot_general(doh, v[:, hs], nt, preferred_element_type=F32)
            da = jnp.where(causal, da, 0.0).astype(BF16)
            dv_s[:, hs] = lax.dot_general(a, doh, tn, preferred_element_type=F32)
            dqr_s[:, hs] = jnp.dot(da, k_rel[:, hs], preferred_element_type=F32)
            dkr_s[:, hs] = lax.dot_general(da, q_rel[:, hs], tn, preferred_element_type=F32)
            for j in reversed(range(ncb)):
                rs = slice(j * C, (j + 1) * C)
                dst = dstate[h]
                dstb = dst.astype(BF16)
                st = st_ref[j, h]
                dkd_s[rs, hs] = jnp.dot(v[rs, hs], dstb, preferred_element_type=F32)
                dv_s[rs, hs] += lax.dot_general(k_dec[rs, hs], dstb, nt, preferred_element_type=F32)
                dec = jnp.exp(jnp.sum(logf[rs, hs], axis=0, keepdims=True))
                e_s[rs, hs] = jnp.broadcast_to(jnp.sum(dst * st, axis=0, keepdims=True) * dec, (C, DK))
                dqd_s[rs, hs] = jnp.dot(doh[rs], st.astype(BF16), preferred_element_type=F32)
                dstate[h] = dec * dst + lax.dot_general(doh[rs], q_dec[rs, hs], tn,
                                                        preferred_element_type=F32)

        dqr, dkr, dqd, dkd = dqr_s[...], dkr_s[...], dqd_s[...], dkd_s[...]
        kdk = dkd * k_dec_f
        db = dqr * q_rel_f - dkr * k_rel_f + dqd * q_dec_f - kdk
        dlogf = _split_dot(rev.astype(BF16), db, 2) + _split_dot(ones.astype(BF16), kdk, 2) + e_s[...]
        dk = dkr * enrel + dkd * erest
        df = dlogf / f - dk
        dlb_ref[...] += jnp.sum(df * (1.0 - sf), axis=0, keepdims=True)
        dq = dqr * erel + dqd * eb
        dp_ref[:, 0:HK] = (dq * (sq * (1.0 + qx * (1.0 - sq)))).astype(BF16)
        dp_ref[:, HK:2 * HK] = (df * (1.0 - lb) * sf * (1.0 - sf)).astype(BF16)
        dp_ref[:, 2 * HK:3 * HK] = dv_s[...].astype(BF16)

    rev_row = lambda w: pl.BlockSpec((tb, w), lambda i: (nb - 1 - i, 0))
    vec = lambda w: pl.BlockSpec((1, w), lambda i: (0, 0))
    scr = pltpu.VMEM((tb, HK), F32)
    return pl.pallas_call(
        body, name="hgrn_bwd", grid=(nb,),
        in_specs=[rev_row(4 * HK), vec(HK), vec(DK), rev_row(HK),
                  pl.BlockSpec((ncb, H, DK, DK), lambda i: (nb - 1 - i, 0, 0, 0)), rev_row(HK)],
        out_specs=(rev_row(4 * HK), vec(HK), vec(DK)),
        out_shape=(jax.ShapeDtypeStruct((T, 4 * HK), BF16), jax.ShapeDtypeStruct((1, HK), F32),
                   jax.ShapeDtypeStruct((1, DK), F32)),
        scratch_shapes=[pltpu.VMEM((H, DK, DK), F32), scr, scr, scr, scr, scr, scr, scr],
        compiler_params=_cparams("arbitrary"))(proj, lb, o_norm, o, states, dy)


def _hgrn_layer_fwd(a, w, lb):
    proj = _mm(a, w["w_in"], name="hgrn_in")
    y, o, states = _hgrn_fwd(proj, lb, w["o_norm"])
    m = _mm(y, w["w_o"], name="hgrn_out")
    return m, (a, proj, y, o, states)


def _hgrn_layer_bwd(dm, saved, w, lb):
    a, proj, y, o, states = saved
    dy = _mm(dm, w["w_o"], tb=True, name="hgrn_out_dx")
    dw_o = _mm(y, dm, ta=True, out_dtype=BF16, name="hgrn_out_dw")
    dproj, dlb, don = _hgrn_bwd(proj, lb, w["o_norm"], o, states, dy)
    dw_in = _mm(a, dproj, ta=True, out_dtype=BF16, name="hgrn_in_dw")
    da = _mm(dproj, w["w_in"], tb=True, name="hgrn_in_dx")
    return da, dict(w_in=dw_in, w_o=dw_o, o_norm=don, lb=dlb)


def _lower_bounds(lb_logits):
    p = jax.nn.softmax(lb_logits.astype(F32), axis=0)
    return jnp.cumsum(p, axis=0) - p[0]


def _rope_tables(positions):
    inv_freq = jnp.power(ROPE_BASE, -jnp.arange(0, MLA_ROPE, 2, dtype=F32) / MLA_ROPE)
    ang = positions.astype(F32)[:, None] * inv_freq
    cos, sin = jnp.cos(ang), jnp.sin(ang)
    zero = jnp.zeros((positions.shape[0], 128 - MLA_ROPE), F32)
    return (jnp.concatenate([cos, cos, zero], axis=-1), jnp.concatenate([-sin, sin, zero], axis=-1))


def _pad_mla_weights(w_in, w_uq):
    w_in_p = jnp.pad(w_in, ((0, 0), (0, 0), (0, 128 - MLA_ROPE)))
    n, ql, _ = w_uq.shape
    w_uq_p = jnp.pad(w_uq.reshape(n, ql, MLA_HEADS, MLA_NOPE + MLA_ROPE),
                     ((0, 0), (0, 0), (0, 0), (0, MLA_QK_PAD - MLA_NOPE - MLA_ROPE)))
    return w_in_p, w_uq_p.reshape(n, ql, MLA_HEADS * MLA_QK_PAD)


def _local_step(x, positions, target, wb, small):
    T, D = x.shape
    gains = small["norm_gains"]
    lbounds, lb_vjp = jax.vjp(_lower_bounds, small["hgrn_lb_logits"])
    cc, ss = _rope_tables(positions)
    w_in_p, w_uq_p = _pad_mla_weights(wb["mla_w_in"], wb["mla_w_uq"])

    def g(layer, i):
        return gains[layer, i][None, :]

    def mixer_weights(layer):
        slot = layer // 2
        if layer % 2 == 0:
            return dict(w_in=w_in_p[slot], w_uq=w_uq_p[slot], w_ukv=wb["mla_w_ukv"][slot],
                        w_o=wb["mla_w_o"][slot], q_norm=small["mla_q_norm"][slot][None, :],
                        kv_norm=small["mla_kv_norm"][slot][None, :])
        return dict(w_in=wb["hgrn_w_in"][slot], w_o=wb["hgrn_w_o"][slot],
                    o_norm=small["hgrn_o_norm"][slot][None, :])

    saved = []
    h = x
    a = _prenorm_fwd(x, g(0, 0))
    dy = sq = None
    for layer in range(DEPTH):
        mw = mixer_weights(layer)
        if layer % 2 == 0:
            m, mix_saved = _mla_fwd(a, mw, cc, ss)
        else:
            m, mix_saved = _hgrn_layer_fwd(a, mw, lbounds[layer][None, :])
        h1, a2 = _resnorm_fwd(h, m, g(layer, 1), g(layer, 2), name="resnorm_fwd_mix")
        u, mlp_saved = _mlp_fwd(a2, wb["mlp_w1"][layer], wb["mlp_w2"][layer])
        if layer + 1 < DEPTH:
            h2, a = _resnorm_fwd(h1, u, g(layer, 3), g(layer + 1, 0), name="resnorm_fwd_mlp")
        else:
            h2 = None
            dy, sq = _resnorm_loss(h1, u, g(layer, 3), target)
        saved.append((h, m, h1, u, mix_saved, mlp_saved))
        h = h2

    n_mla, n_hgrn = (DEPTH + 1) // 2, DEPTH // 2
    dgains = [[None] * 4 for _ in range(DEPTH)]
    gw = {k: [None] * n_mla for k in ("mla_w_in", "mla_w_uq", "mla_w_ukv", "mla_w_o", "mla_q_norm", "mla_kv_norm")}
    gw.update({k: [None] * n_hgrn for k in ("hgrn_w_in", "hgrn_w_o", "hgrn_o_norm")})
    gw["mlp_w1"] = [None] * DEPTH
    gw["mlp_w2"] = [None] * DEPTH
    dlb = [jnp.zeros((1, lbounds.shape[1]), F32) for _ in range(DEPTH)]
    dh = dy
    da_next = None
    for layer in reversed(range(DEPTH)):
        h0, m, h1, u, mix_saved, mlp_saved = saved[layer]
        slot = layer // 2
        mw = mixer_weights(layer)
        if da_next is None:
            du, dgains[layer][3] = _resnorm_bwd(u, g(layer, 3), dh, name="resnorm_bwd_last")
            t = dh
        else:
            h2 = saved[layer + 1][0]
            t, du, dgains[layer][3], dgains[layer + 1][0] = _resnorm_bwd(
                u, g(layer, 3), dh, h2, da_next, g(layer + 1, 0), name="resnorm_bwd_mlp")
        da2, gw["mlp_w1"][layer], gw["mlp_w2"][layer] = _mlp_bwd(
            du, mlp_saved, wb["mlp_w1"][layer], wb["mlp_w2"][layer])
        t, dm, dgains[layer][1], dgains[layer][2] = _resnorm_bwd(
            m, g(layer, 1), t, h1, da2, g(layer, 2), name="resnorm_bwd_mix")
        if layer % 2 == 0:
            da_next, mg = _mla_bwd(dm, mix_saved, mw, cc, ss)
            ql = mg["q_norm"].shape[-1]
            kvl = mg["kv_norm"].shape[-1]
            gw["mla_w_in"][slot] = mg["w_in"][:, :ql + kvl + MLA_ROPE]
            gw["mla_w_uq"][slot] = mg["w_uq"].reshape(ql, MLA_HEADS, MLA_QK_PAD)[
                :, :, :MLA_NOPE + MLA_ROPE].reshape(ql, MLA_HEADS * (MLA_NOPE + MLA_ROPE))
            gw["mla_w_ukv"][slot] = mg["w_ukv"]
            gw["mla_w_o"][slot] = mg["w_o"]
            gw["mla_q_norm"][slot] = mg["q_norm"][0]
            gw["mla_kv_norm"][slot] = mg["kv_norm"][0]
        else:
            da_next, hg = _hgrn_layer_bwd(dm, mix_saved, mw, lbounds[layer][None, :])
            gw["hgrn_w_in"][slot] = hg["w_in"]
            gw["hgrn_w_o"][slot] = hg["w_o"]
            gw["hgrn_o_norm"][slot] = hg["o_norm"][0]
            dlb[layer] = hg["lb"]
        dh = t
    grad_x, dgains[0][0] = _prenorm_bwd(x, g(0, 0), dh, da_next)

    grads = {k: jnp.stack(vs) for k, vs in gw.items()}
    grads["norm_gains"] = jnp.stack([jnp.concatenate(row, axis=0) for row in dgains])
    (grads["hgrn_lb_logits"],) = lb_vjp(jnp.concatenate(dlb, axis=0))
    return sq, grad_x, grads


def _size(shape):
    n = 1
    for d in shape:
        n *= d
    return n


def _piece_rows(shape):
    return -(-_size(shape) // PACK_W)


def _packed_rows(shapes):
    rows = sum(_piece_rows(s) for s in shapes)
    return -(-rows // PACK_ALIGN) * PACK_ALIGN


def _pack_blocks(pieces, rows, dtype):
    blocks, used = [], 0
    for p in pieces:
        flat = p.astype(dtype).reshape(-1)
        r = _piece_rows(p.shape)
        if r * PACK_W != flat.shape[0]:
            flat = jnp.pad(flat, (0, r * PACK_W - flat.shape[0]))
        blocks.append(flat.reshape(r, PACK_W))
        used += r
    if rows > used:
        blocks.append(jnp.zeros((rows - used, PACK_W), dtype))
    return blocks


def _unpack(buf, shapes):
    out, off = [], 0
    for shp in shapes:
        r = _piece_rows(shp)
        piece = buf[off:off + r]
        if r * PACK_W != _size(shp):
            piece = piece.reshape(-1)[:_size(shp)]
        out.append(piece.reshape(shp))
        off += r
    return out


def _mesh_place():
    x, y, c = lax.axis_index("x"), lax.axis_index("y"), lax.axis_index("c")
    chips = [(1 - x, y), (x, 1 - y), (1 - x, 1 - y)]
    return x, y, c, chips


_HBM = pl.BlockSpec(memory_space=pltpu.HBM)


def _all_gather(wp):
    R, W = wp.shape
    rh = R // 2
    rq = rh // 2
    assert rq % 16 == 0

    def body(w_ref, out_ref, send_sems, recv_sems):
        x, y, c, _ = _mesh_place()
        me, jx, jy, jd = 2 * x + y, 2 * (1 - x) + y, 2 * x + (1 - y), 2 * (1 - x) + (1 - y)
        to_x, to_y, sibling = (1 - x, y, c), (x, 1 - y, c), (x, y, 1 - c)

        def rows(core, quarter):
            return pl.ds(pl.multiple_of(core * rh + quarter * rq, 16), rq)

        def slot(j, core, quarter):
            return out_ref.at[j, rows(core, quarter)]

        def copy(k, src, dst, to):
            return pltpu.make_async_remote_copy(src_ref=src, dst_ref=dst, send_sem=send_sems.at[k],
                                                recv_sem=recv_sems.at[k], device_id=to, device_id_type=MESH)

        sends = [copy(0, w_ref.at[rows(c, 0)], slot(me, c, 0), to_x),
                 copy(2, w_ref.at[rows(c, 1)], slot(me, c, 1), to_y),
                 copy(1, w_ref.at[rows(c, 1)], slot(me, c, 1), to_x),
                 copy(3, w_ref.at[rows(c, 0)], slot(me, c, 0), to_y)]
        for cp in sends:
            cp.start()
        arrivals = [(0, slot(jx, c, 0), 4, to_y, 6), (2, slot(jy, c, 1), 5, to_x, 7),
                    (1, slot(jx, c, 1), None, None, 8), (3, slot(jy, c, 0), None, None, 9),
                    (4, slot(jd, c, 0), None, None, 10), (5, slot(jd, c, 1), None, None, 11)]
        for k, landed, k_on, to_on, k_sib in arrivals:
            copy(k, landed, landed, sibling).wait_recv()
            if k_on is not None:
                cp = copy(k_on, landed, landed, to_on)
                cp.start()
                sends.append(cp)
            cp = copy(k_sib, landed, landed, sibling)
            cp.start()
            sends.append(cp)
        for k_sib, j, quarter in ((6, jx, 0), (7, jy, 1), (8, jx, 1), (9, jy, 0), (10, jd, 0), (11, jd, 1)):
            landed = slot(j, 1 - c, quarter)
            copy(k_sib, landed, landed, sibling).wait_recv()
        for cp in sends:
            cp.wait_send()

    out = pl.pallas_call(
        body, name="weights_all_gather", in_specs=[_HBM], out_specs=_HBM,
        out_shape=jax.ShapeDtypeStruct((N_CHIPS, R, W), wp.dtype),
        scratch_shapes=[pltpu.SemaphoreType.DMA((12,)), pltpu.SemaphoreType.DMA((12,))],
    )(wp)
    me = 2 * lax.axis_index("x") + lax.axis_index("y")
    return lax.dynamic_update_slice(out, wp[None], (me, 0, 0))


def _exchange_halves(g):
    n, _, rh, W = g.shape

    def body(g_ref, out_ref, send_sems, recv_sems):
        x, y, c, _ = _mesh_place()
        sibling = (x, y, 1 - c)
        copies = [pltpu.make_async_remote_copy(
            src_ref=g_ref.at[j, 1 - c], dst_ref=out_ref.at[j], send_sem=send_sems.at[j],
            recv_sem=recv_sems.at[j], device_id=sibling, device_id_type=MESH) for j in range(n)]
        for cp in copies:
            cp.start()
        for cp in copies:
            cp.wait()

    return pl.pallas_call(
        body, name="grads_to_sibling", in_specs=[_HBM], out_specs=_HBM,
        out_shape=jax.ShapeDtypeStruct((n, rh, W), g.dtype),
        scratch_shapes=[pltpu.SemaphoreType.DMA((n,)), pltpu.SemaphoreType.DMA((n,))],
    )(g)


def _scatter_to_owners(p):
    n, rh, W = p.shape
    rq = rh // 2
    assert rq % 16 == 0

    def body(p_ref, out_ref, stage_ref, send_sems, recv_sems):
        x, y, c, _ = _mesh_place()
        me, jx, jy, jd = 2 * x + y, 2 * (1 - x) + y, 2 * x + (1 - y), 2 * (1 - x) + (1 - y)
        to_x, to_y = (1 - x, y, c), (x, 1 - y, c)

        def quarter(ref, j, q):
            return ref.at[j, pl.ds(q * rq, rq)]

        def copy(k, src, dst, to):
            return pltpu.make_async_remote_copy(src_ref=src, dst_ref=dst, send_sem=send_sems.at[k],
                                                recv_sem=recv_sems.at[k], device_id=to, device_id_type=MESH)

        sends = [copy(2, quarter(p_ref, jd, 0), stage_ref.at[0], to_x),
                 copy(3, quarter(p_ref, jd, 1), stage_ref.at[1], to_y),
                 copy(0, p_ref.at[jx], out_ref.at[me], to_x),
                 copy(1, p_ref.at[jy], out_ref.at[me], to_y)]
        for cp in sends:
            cp.start()
        copy(2, stage_ref.at[0], stage_ref.at[0], to_x).wait_recv()
        relay = copy(4, stage_ref.at[0], quarter(out_ref, jx, 0), to_y)
        relay.start()
        sends.append(relay)
        copy(3, stage_ref.at[1], stage_ref.at[1], to_y).wait_recv()
        relay = copy(5, stage_ref.at[1], quarter(out_ref, jy, 1), to_x)
        relay.start()
        sends.append(relay)
        copy(0, out_ref.at[jx], out_ref.at[jx], to_x).wait_recv()
        copy(1, out_ref.at[jy], out_ref.at[jy], to_y).wait_recv()
        copy(4, quarter(out_ref, jd, 0), quarter(out_ref, jd, 0), to_y).wait_recv()
        copy(5, quarter(out_ref, jd, 1), quarter(out_ref, jd, 1), to_x).wait_recv()
        for cp in sends:
            cp.wait_send()

    out, _ = pl.pallas_call(
        body, name="grads_to_owner", in_specs=[_HBM], out_specs=(_HBM, _HBM),
        out_shape=(jax.ShapeDtypeStruct((n, rh, W), p.dtype), jax.ShapeDtypeStruct((2, rq, W), p.dtype)),
        scratch_shapes=[pltpu.SemaphoreType.DMA((6,)), pltpu.SemaphoreType.DMA((6,))],
    )(p)
    me = 2 * lax.axis_index("x") + lax.axis_index("y")
    mine = lax.dynamic_index_in_dim(p, me, axis=0, keepdims=True)
    return lax.dynamic_update_slice(out, mine, (me, 0, 0))


def _share_reduced(q):
    rh, W = q.shape

    def body(q_ref, out_ref, send_sem, recv_sem):
        x, y, c, _ = _mesh_place()
        cp = pltpu.make_async_remote_copy(src_ref=q_ref, dst_ref=out_ref.at[c], send_sem=send_sem,
                                          recv_sem=recv_sem, device_id=(x, y, 1 - c), device_id_type=MESH)
        cp.start()
        cp.wait()

    out = pl.pallas_call(
        body, name="grads_share_reduced", in_specs=[_HBM], out_specs=_HBM,
        out_shape=jax.ShapeDtypeStruct((2, rh, W), q.dtype),
        scratch_shapes=[pltpu.SemaphoreType.DMA, pltpu.SemaphoreType.DMA],
    )(q)
    return lax.dynamic_update_slice(out, q[None], (lax.axis_index("c"), 0, 0))


PACK_TILE = 1280


def _add_sibling(g, recv, c_arr):
    n, _, rh, W = g.shape
    tr = PACK_TILE

    def body(c_ref, g_ref, r_ref, o_ref):
        o_ref[...] = (g_ref[...].astype(F32) + r_ref[...].astype(F32)).astype(BF16)

    return pl.pallas_call(
        body, name="grads_add_sibling",
        grid_spec=pltpu.PrefetchScalarGridSpec(
            num_scalar_prefetch=1, grid=(n, rh // tr),
            in_specs=[pl.BlockSpec((None, None, tr, W), lambda j, i, c_ref: (j, c_ref[0], i, 0)),
                      pl.BlockSpec((None, tr, W), lambda j, i, c_ref: (j, i, 0))],
            out_specs=pl.BlockSpec((None, tr, W), lambda j, i, c_ref: (j, i, 0))),
        out_shape=jax.ShapeDtypeStruct((n, rh, W), BF16),
        compiler_params=_cparams("parallel", "parallel"))(c_arr, g, recv)


def _sum_chips(parts):
    n, rh, W = parts.shape
    tr = PACK_TILE

    def body(p_ref, o_ref):
        acc = p_ref[0].astype(F32)
        for j in range(1, n):
            acc = acc + p_ref[j].astype(F32)
        o_ref[...] = acc

    return pl.pallas_call(
        body, name="grads_sum_chips", grid=(rh // tr,),
        in_specs=[pl.BlockSpec((n, tr, W), lambda i: (0, i, 0))],
        out_specs=pl.BlockSpec((tr, W), lambda i: (i, 0)),
        out_shape=jax.ShapeDtypeStruct((rh, W), F32),
        compiler_params=_cparams("parallel"))(parts)


def _adamw(w, g, m, v, name):
    shape = w.shape
    cols = shape[-1]
    w2, g2, m2, v2 = (t.reshape(-1, cols) for t in (w, g, m, v))
    rows = w2.shape[0]
    tr = rows
    for cand in (512, 256, 128, 64, 32, 16, 8):
        if rows > cand and rows % cand == 0:
            tr = cand
            break
    c1 = 1.0 / (1.0 - ADAM_B1 ** ADAM_STEP)
    c2 = 1.0 / (1.0 - ADAM_B2 ** ADAM_STEP)

    def body(w_ref, g_ref, m_ref, v_ref, d_ref, nm_ref, nv_ref):
        gv = g_ref[...]
        nm = ADAM_B1 * m_ref[...] + (1.0 - ADAM_B1) * gv
        nv = ADAM_B2 * v_ref[...] + (1.0 - ADAM_B2) * (gv * gv)
        nm_ref[...] = nm
        nv_ref[...] = nv
        d_ref[...] = -ADAM_LR * ((nm * c1) / (jnp.sqrt(nv * c2) + ADAM_EPS) + ADAM_WD * w_ref[...])

    blk = pl.BlockSpec((tr, cols), lambda i: (i, 0))
    sds = jax.ShapeDtypeStruct((rows, cols), F32)
    d, nm, nv = pl.pallas_call(body, name=name, grid=(rows // tr,), in_specs=[blk] * 4,
                               out_specs=(blk, blk, blk), out_shape=(sds, sds, sds),
                               compiler_params=_cparams("parallel"))(w2, g2, m2, v2)
    return d.reshape(shape), nm.reshape(shape), nv.reshape(shape)


def kernel(x, positions, norm_gains, mla_w_in, mla_q_norm, mla_kv_norm, mla_w_uq, mla_w_ukv, mla_w_o, hgrn_w_in, hgrn_lb_logits, hgrn_o_norm, hgrn_w_o, mlp_w1, mlp_w2, loss_target, m_norm_gains, m_mla_w_in, m_mla_q_norm, m_mla_kv_norm, m_mla_w_uq, m_mla_w_ukv, m_mla_w_o, m_hgrn_w_in, m_hgrn_lb_logits, m_hgrn_o_norm, m_hgrn_w_o, m_mlp_w1, m_mlp_w2, v_norm_gains, v_mla_w_in, v_mla_q_norm, v_mla_kv_norm, v_mla_w_uq, v_mla_w_ukv, v_mla_w_o, v_hgrn_w_in, v_hgrn_lb_logits, v_hgrn_o_norm, v_hgrn_w_o, v_mlp_w1, v_mlp_w2):
    w = dict(norm_gains=norm_gains, mla_w_in=mla_w_in, mla_q_norm=mla_q_norm, mla_kv_norm=mla_kv_norm,
             mla_w_uq=mla_w_uq, mla_w_ukv=mla_w_ukv, mla_w_o=mla_w_o, hgrn_w_in=hgrn_w_in,
             hgrn_lb_logits=hgrn_lb_logits, hgrn_o_norm=hgrn_o_norm, hgrn_w_o=hgrn_w_o,
             mlp_w1=mlp_w1, mlp_w2=mlp_w2)
    mom_m = dict(norm_gains=m_norm_gains, mla_w_in=m_mla_w_in, mla_q_norm=m_mla_q_norm,
                 mla_kv_norm=m_mla_kv_norm, mla_w_uq=m_mla_w_uq, mla_w_ukv=m_mla_w_ukv,
                 mla_w_o=m_mla_w_o, hgrn_w_in=m_hgrn_w_in, hgrn_lb_logits=m_hgrn_lb_logits,
                 hgrn_o_norm=m_hgrn_o_norm, hgrn_w_o=m_hgrn_w_o, mlp_w1=m_mlp_w1, mlp_w2=m_mlp_w2)
    mom_v = dict(norm_gains=v_norm_gains, mla_w_in=v_mla_w_in, mla_q_norm=v_mla_q_norm,
                 mla_kv_norm=v_mla_kv_norm, mla_w_uq=v_mla_w_uq, mla_w_ukv=v_mla_w_ukv,
                 mla_w_o=v_mla_w_o, hgrn_w_in=v_hgrn_w_in, hgrn_lb_logits=v_hgrn_lb_logits,
                 hgrn_o_norm=v_hgrn_o_norm, hgrn_w_o=v_hgrn_w_o, mlp_w1=v_mlp_w1, mlp_w2=v_mlp_w2)
    c = lax.axis_index("c")

    mats = [(name, axis) for name, axis in SHARDED if name != "norm_gains"]
    shard_shapes = [w[name].shape for name, _ in mats] + [norm_gains.shape, norm_gains.shape]
    w_rows = _packed_rows(shard_shapes)
    gain_bits = lax.bitcast_convert_type(norm_gains, jnp.uint32)
    gain_hi = lax.bitcast_convert_type((gain_bits >> 16).astype(jnp.uint16), BF16)
    gain_lo = lax.bitcast_convert_type((gain_bits & 0xFFFF).astype(jnp.uint16), BF16)
    wpack = jnp.concatenate(
        _pack_blocks([w[name] for name, _ in mats] + [gain_hi, gain_lo], w_rows, BF16), axis=0)
    gathered = _all_gather(wpack)
    per_chip = [_unpack(gathered[j], shard_shapes) for j in range(N_CHIPS)]
    wb = {}
    for i, (name, axis) in enumerate(mats):
        wb[name] = jnp.concatenate([per_chip[j][i] for j in range(N_CHIPS)], axis=axis)
    got_hi, got_lo = (lax.bitcast_convert_type(
        jnp.concatenate([per_chip[j][i] for j in range(N_CHIPS)], axis=2), jnp.uint16).astype(jnp.uint32)
        for i in (-2, -1))
    gains_full = lax.bitcast_convert_type((got_hi << 16) | got_lo, F32)
    small = dict(norm_gains=gains_full, mla_q_norm=mla_q_norm, mla_kv_norm=mla_kv_norm,
                 hgrn_lb_logits=hgrn_lb_logits, hgrn_o_norm=hgrn_o_norm)

    sq, grad_x, grads = _local_step(x[0], positions[0], loss_target[0], wb, small)
    d_model = x.shape[-1]
    loss = lax.psum(0.5 * jnp.sum(sq) / d_model, ("x", "y", "c"))

    grad_shapes = [w[name].shape for name, _ in SHARDED] + [w[name].shape for name in REPLICATED]
    g_rows = _packed_rows(grad_shapes)
    blocks = []
    for j in range(N_CHIPS):
        pieces = [jnp.split(grads[name], N_CHIPS, axis=axis)[j] for name, axis in SHARDED]
        pieces += [grads[name] for name in REPLICATED]
        blocks += _pack_blocks(pieces, g_rows, BF16)
    gpack = jnp.concatenate(blocks, axis=0).reshape(N_CHIPS, 2, g_rows // 2, PACK_W)
    from_sibling = _exchange_halves(gpack)
    chip_partial = _add_sibling(gpack, from_sibling, jnp.reshape(c, (1,)).astype(jnp.int32))
    from_chips = _scatter_to_owners(chip_partial)
    reduced_half = _sum_chips(from_chips)
    reduced = _share_reduced(reduced_half).reshape(g_rows, PACK_W)
    red = _unpack(reduced, grad_shapes)
    g_out = {name: red[i] for i, (name, _) in enumerate(SHARDED)}
    g_out.update({name: red[len(SHARDED) + i] for i, name in enumerate(REPLICATED)})

    deltas, new_m, new_v = {}, {}, {}
    for name in WEIGHTS:
        deltas[name], new_m[name], new_v[name] = _adamw(w[name], g_out[name], mom_m[name], mom_v[name],
                                                        name="adamw_" + name)
    return (loss, grad_x[None], *[g_out[n] for n in WEIGHTS], *[deltas[n] for n in WEIGHTS],
            *[new_m[n] for n in WEIGHTS], *[new_v[n] for n in WEIGHTS])
```

```python
import functools

import jax
import jax.numpy as jnp
from jax import lax
from jax.experimental import pallas as pl
from jax.experimental.pallas import tpu as pltpu

F32 = jnp.float32
BF16 = jnp.bfloat16
MESH = pl.DeviceIdType.MESH

DEPTH = 4
MLA_HEADS = 8
MLA_NOPE = 128
MLA_ROPE = 64
MLA_V = 128
MLA_QK_PAD = 256
MLA_HEADS_PER_STEP = 2
ROPE_BASE = 10000.0
HGRN_HEADS = 8
HGRN_CHUNK = 32
HGRN_BLOCK = 128
EPS = 1e-6

ADAM_LR = 0.001
ADAM_B1 = 0.9
ADAM_B2 = 0.999
ADAM_EPS = 1e-08
ADAM_WD = 0.01
ADAM_STEP = 10

N_CHIPS = 4
PACK_W = 1024
PACK_ALIGN = 1024
PACK_TILE = 512
V7X_VMEM_LIMIT = 56 * 1024 * 1024

SHARDED = (("norm_gains", 2), ("mla_w_in", 1), ("mla_w_uq", 2), ("mla_w_ukv", 2), ("mla_w_o", 1),
           ("hgrn_w_in", 2), ("hgrn_w_o", 1), ("mlp_w1", 2), ("mlp_w2", 1))
REPLICATED = ("mla_q_norm", "mla_kv_norm", "hgrn_lb_logits", "hgrn_o_norm")
WEIGHTS = ("norm_gains", "mla_w_in", "mla_q_norm", "mla_kv_norm", "mla_w_uq", "mla_w_ukv", "mla_w_o",
           "hgrn_w_in", "hgrn_lb_logits", "hgrn_o_norm", "hgrn_w_o", "mlp_w1", "mlp_w2")


def _cparams(*semantics):
    return pltpu.CompilerParams(dimension_semantics=semantics, vmem_limit_bytes=V7X_VMEM_LIMIT)


def _sigmoid(x):
    return 1.0 / (1.0 + jnp.exp(-x))


def _mm(a, b, *, ta=False, tb=False, out_dtype=F32, tm=1024, tn=1024, tk=1024, epi=None, extra=None,
        name="mm", n=None, b_map=None, into=None, o_map=None):
    if ta:
        K, M = a.shape
    else:
        M, K = a.shape
    if b_map is not None:
        N = n
    elif tb:
        N, Kb = b.shape
    else:
        Kb, N = b.shape
    assert b_map is not None or K == Kb, (a.shape, b.shape, ta, tb)
    tm, tn = min(tm, M), min(tn, N)
    tk = K if (K <= 1024 and b_map is None) else min(tk, K)
    assert M % tm == 0 and N % tn == 0 and K % tk == 0, (M, N, K, tm, tn, tk)
    nk = K // tk
    a_spec = (pl.BlockSpec((tk, tm), lambda i, j, k: (k, i)) if ta
              else pl.BlockSpec((tm, tk), lambda i, j, k: (i, k)))
    if b_map is None:
        b_map = (lambda i, j, k: (j, k)) if tb else (lambda i, j, k: (k, j))
    b_spec = pl.BlockSpec((tn, tk) if tb else (tk, tn), b_map)
    o_spec = pl.BlockSpec((tm, tn), lambda i, j, k: (i, j))
    dims = (((0 if ta else 1,), (1 if tb else 0,)), ((), ()))
    in_specs = [a_spec, b_spec]
    operands = [a, b]
    aliases = {}
    if epi == "mul2r":
        in_specs.append(o_spec)
        operands.append(extra)
    if into is not None:
        assert epi is None
        in_specs.append(pl.BlockSpec(memory_space=pl.ANY))
        operands.append(into)
        aliases = {2: 0}
        out_dtype = into.dtype
        out_shape = jax.ShapeDtypeStruct(into.shape, into.dtype)
        out_specs = pl.BlockSpec((tm, tn), o_map)
    elif epi == "relu2":
        out_shape = (jax.ShapeDtypeStruct((M, N), BF16), jax.ShapeDtypeStruct((M, N), BF16))
        out_specs = (o_spec, o_spec)
    elif epi == "mul2r":
        out_shape = jax.ShapeDtypeStruct((M, N), BF16)
        out_specs = o_spec
    else:
        out_shape = jax.ShapeDtypeStruct((M, N), out_dtype)
        out_specs = o_spec
    n_in = len(operands)

    def body(*refs):
        a_ref, b_ref = refs[0], refs[1]
        outs = refs[n_in:-1]
        acc_ref = refs[-1]
        k = pl.program_id(2)

        @pl.when(k == 0)
        def _():
            acc_ref[...] = jnp.zeros_like(acc_ref)

        acc_ref[...] += lax.dot_general(a_ref[...], b_ref[...], dims, preferred_element_type=F32)

        @pl.when(k == nk - 1)
        def _():
            acc = acc_ref[...]
            if epi == "relu2":
                r = jnp.maximum(acc, 0.0)
                outs[0][...] = (r * r).astype(BF16)
                outs[1][...] = r.astype(BF16)
            elif epi == "mul2r":
                outs[0][...] = (acc * (2.0 * refs[2][...].astype(F32))).astype(BF16)
            else:
                outs[0][...] = acc.astype(out_dtype)

    return pl.pallas_call(
        body, name=name, grid=(M // tm, N // tn, nk), in_specs=in_specs, out_specs=out_specs,
        out_shape=out_shape, scratch_shapes=[pltpu.VMEM((tm, tn), F32)], input_output_aliases=aliases,
        compiler_params=_cparams("parallel", "parallel", "arbitrary"))(*operands)


def _rms_rstd(x):
    return lax.rsqrt(jnp.mean(x * x, axis=-1, keepdims=True) + EPS)


def _rms_bwd_tile(x, g, dy):
    r = _rms_rstd(x)
    xh = x * r
    u = dy * g
    dx = r * (u - xh * jnp.mean(u * xh, axis=-1, keepdims=True))
    dg = jnp.sum(dy * xh, axis=0, keepdims=True)
    return dx, dg


def _row_tile(T):
    return min(256, T)


def _prenorm_fwd(x, g, name="prenorm_fwd"):
    T, D = x.shape
    tm = _row_tile(T)

    def body(x_ref, g_ref, a_ref):
        xv = x_ref[...]
        a_ref[...] = (xv * _rms_rstd(xv) * g_ref[...]).astype(BF16)

    row = pl.BlockSpec((tm, D), lambda i: (i, 0))
    vec = pl.BlockSpec((1, D), lambda i: (0, 0))
    return pl.pallas_call(body, name=name, grid=(T // tm,), in_specs=[row, vec], out_specs=row,
                          out_shape=jax.ShapeDtypeStruct((T, D), BF16),
                          compiler_params=_cparams("parallel"))(x, g)


def _resnorm_fwd(h, z, g_post, g_pre, name="resnorm_fwd"):
    T, D = h.shape
    tm = _row_tile(T)

    def body(h_ref, z_ref, gp_ref, gn_ref, hn_ref, a_ref):
        zv = z_ref[...]
        hn = h_ref[...] + zv * _rms_rstd(zv) * gp_ref[...]
        hn_ref[...] = hn
        a_ref[...] = (hn * _rms_rstd(hn) * gn_ref[...]).astype(BF16)

    row = pl.BlockSpec((tm, D), lambda i: (i, 0))
    vec = pl.BlockSpec((1, D), lambda i: (0, 0))
    return pl.pallas_call(body, name=name, grid=(T // tm,), in_specs=[row, row, vec, vec],
                          out_specs=(row, row),
                          out_shape=(jax.ShapeDtypeStruct((T, D), F32), jax.ShapeDtypeStruct((T, D), BF16)),
                          compiler_params=_cparams("parallel"))(h, z, g_post, g_pre)


def _resnorm_loss(h, z, g_post, target, name="resnorm_loss"):
    T, D = h.shape
    tm = _row_tile(T)

    def body(h_ref, z_ref, gp_ref, t_ref, dy_ref, sq_ref):
        zv = z_ref[...]
        err = h_ref[...] + zv * _rms_rstd(zv) * gp_ref[...] - t_ref[...]
        dy_ref[...] = err * (1.0 / D)

        @pl.when(pl.program_id(0) == 0)
        def _():
            sq_ref[...] = jnp.zeros_like(sq_ref)

        sq_ref[...] += jnp.sum(err * err, axis=0, keepdims=True)

    row = pl.BlockSpec((tm, D), lambda i: (i, 0))
    vec = pl.BlockSpec((1, D), lambda i: (0, 0))
    return pl.pallas_call(body, name=name, grid=(T // tm,), in_specs=[row, row, vec, row],
                          out_specs=(row, vec),
                          out_shape=(jax.ShapeDtypeStruct((T, D), F32), jax.ShapeDtypeStruct((1, D), F32)),
                          compiler_params=_cparams("arbitrary"))(h, z, g_post, target)


def _resnorm_bwd(z, g_post, dh, h_new=None, da=None, g_pre=None, name="resnorm_bwd"):
    T, D = z.shape
    tm = _row_tile(T)
    has_next = h_new is not None
    row = pl.BlockSpec((tm, D), lambda i: (i, 0))
    vec = pl.BlockSpec((1, D), lambda i: (0, 0))

    if has_next:
        def body(z_ref, gp_ref, dh_ref, hn_ref, da_ref, gn_ref, t_ref, dz_ref, dgp_ref, dgn_ref):
            first = pl.program_id(0) == 0

            @pl.when(first)
            def _():
                dgp_ref[...] = jnp.zeros_like(dgp_ref)
                dgn_ref[...] = jnp.zeros_like(dgn_ref)

            dpre, dgn = _rms_bwd_tile(hn_ref[...], gn_ref[...], da_ref[...])
            t = dh_ref[...] + dpre
            t_ref[...] = t
            dz, dgp = _rms_bwd_tile(z_ref[...], gp_ref[...], t)
            dz_ref[...] = dz.astype(BF16)
            dgp_ref[...] += dgp
            dgn_ref[...] += dgn

        return pl.pallas_call(
            body, name=name, grid=(T // tm,), in_specs=[row, vec, row, row, row, vec],
            out_specs=(row, row, vec, vec),
            out_shape=(jax.ShapeDtypeStruct((T, D), F32), jax.ShapeDtypeStruct((T, D), BF16),
                       jax.ShapeDtypeStruct((1, D), F32), jax.ShapeDtypeStruct((1, D), F32)),
            compiler_params=_cparams("arbitrary"))(z, g_post, dh, h_new, da, g_pre)

    def body_last(z_ref, gp_ref, dh_ref, dz_ref, dgp_ref):
        @pl.when(pl.program_id(0) == 0)
        def _():
            dgp_ref[...] = jnp.zeros_like(dgp_ref)

        dz, dgp = _rms_bwd_tile(z_ref[...], gp_ref[...], dh_ref[...])
        dz_ref[...] = dz.astype(BF16)
        dgp_ref[...] += dgp

    return pl.pallas_call(
        body_last, name=name, grid=(T // tm,), in_specs=[row, vec, row], out_specs=(row, vec),
        out_shape=(jax.ShapeDtypeStruct((T, D), BF16), jax.ShapeDtypeStruct((1, D), F32)),
        compiler_params=_cparams("arbitrary"))(z, g_post, dh)


def _prenorm_bwd(x, g, dh, da, name="prenorm_bwd"):
    T, D = x.shape
    tm = _row_tile(T)

    def body(x_ref, g_ref, dh_ref, da_ref, dx_ref, dg_ref):
        @pl.when(pl.program_id(0) == 0)
        def _():
            dg_ref[...] = jnp.zeros_like(dg_ref)

        dpre, dg = _rms_bwd_tile(x_ref[...], g_ref[...], da_ref[...])
        dx_ref[...] = dh_ref[...] + dpre
        dg_ref[...] += dg

    row = pl.BlockSpec((tm, D), lambda i: (i, 0))
    vec = pl.BlockSpec((1, D), lambda i: (0, 0))
    return pl.pallas_call(
        body, name=name, grid=(T // tm,), in_specs=[row, vec, row, row], out_specs=(row, vec),
        out_shape=(jax.ShapeDtypeStruct((T, D), F32), jax.ShapeDtypeStruct((1, D), F32)),
        compiler_params=_cparams("arbitrary"))(x, g, dh, da)


class _Packed:
    BIG = ("mlp_w1", "mlp_w2", "hgrn_w_in", "hgrn_w_o", "mla_w_o")

    def __init__(self, shard_shapes, misc_shapes):
        self.width = shard_shapes["mlp_w1"][-1]
        self.off, self.per_layer = {}, {}
        r = 0
        for name in self.BIG:
            shp = shard_shapes[name]
            assert shp[-1] == self.width, (name, shp)
            self.off[name], self.per_layer[name] = r, shp[1]
            r += shp[0] * shp[1]
        self.misc = r
        self.misc_rows = sum(_piece_rows(s) for s in misc_shapes)
        self.rows = -(-(r + self.misc_rows) // PACK_ALIGN) * PACK_ALIGN

    def block(self, name, layer, unit):
        r = self.off[name] + layer * self.per_layer[name]
        assert r % unit == 0 and self.rows % unit == 0
        return r // unit, self.rows // unit


def _col_sharded(pk, name, layer, unit):
    base, stride = pk.block(name, layer, unit)
    return (lambda i, j, k: (j * stride + base, 0)), (lambda i, j, k: (k * stride + base, 0))


def _row_sharded(pk, name, layer, unit):
    base, stride = pk.block(name, layer, unit)
    return ((lambda i, j, k: (k * stride + base, 0)), (lambda i, j, k: (j * stride + base, 0)),
            (lambda i, j, k: (i * stride + base, 0)))


def _mlp_fwd(a, wbuf, pk, layer):
    D = a.shape[1]
    by_n, _ = _col_sharded(pk, "mlp_w1", layer, D)
    by_k, _, _ = _row_sharded(pk, "mlp_w2", layer, D)
    act, r = _mm(a, wbuf, n=4 * D, b_map=by_n, tk=D, tn=D, epi="relu2", name="mlp_up")
    u = _mm(act, wbuf, n=D, b_map=by_k, tk=D, tn=D, name="mlp_down")
    return u, (a, act, r)


def _mlp_bwd(du, saved, wbuf, gbuf, pk, layer):
    a, act, r = saved
    D = a.shape[1]
    w1_by_n, w1_by_k = _col_sharded(pk, "mlp_w1", layer, D)
    _, w2_by_n, w2_by_m = _row_sharded(pk, "mlp_w2", layer, D)
    dz1 = _mm(du, wbuf, tb=True, n=4 * D, b_map=w2_by_n, tn=D, tk=D, epi="mul2r", extra=r, name="mlp_down_dx")
    gbuf = _mm(act, du, ta=True, into=gbuf, o_map=w2_by_m, tm=D, tn=D, name="mlp_down_dw")
    gbuf = _mm(a, dz1, ta=True, into=gbuf, o_map=w1_by_n, tm=D, tn=D, name="mlp_up_dw")
    da = _mm(dz1, wbuf, tb=True, n=D, b_map=w1_by_k, tn=D, tk=D, name="mlp_up_dx")
    return da, gbuf


def _rope_swap(t):
    n = t.shape[-1]
    lane = lax.broadcasted_iota(jnp.int32, t.shape, t.ndim - 1)
    half = MLA_ROPE // 2
    first = (lane & (MLA_ROPE - 1)) < half
    return jnp.where(first, pltpu.roll(t, n - half, t.ndim - 1), pltpu.roll(t, half, t.ndim - 1))


def _mla_mid_fwd(proj, q_norm, kv_norm, w_uq, w_ukv, cc, ss):
    T, PW = proj.shape
    QL, KVL = q_norm.shape[-1], kv_norm.shape[-1]
    H = MLA_HEADS
    assert PW == QL + KVL + 128
    tm = _row_tile(T)

    def body(p_ref, qn_ref, kn_ref, wq_ref, wkv_ref, cc_ref, ss_ref,
             cq_ref, ckv_ref, q_ref, k_ref, v_ref):
        cq = p_ref[:, 0:QL]
        ckv = p_ref[:, QL:QL + KVL]
        kr = p_ref[:, QL + KVL:QL + KVL + 128]
        c, s = cc_ref[...], ss_ref[...]
        cqn = (cq * _rms_rstd(cq) * qn_ref[...]).astype(BF16)
        ckvn = (ckv * _rms_rstd(ckv) * kn_ref[...]).astype(BF16)
        cq_ref[...] = cqn
        ckv_ref[...] = ckvn
        q = jnp.dot(cqn, wq_ref[...], preferred_element_type=F32)
        kv = jnp.dot(ckvn, wkv_ref[...], preferred_element_type=F32)
        krf = (kr * c + _rope_swap(kr) * s).astype(BF16)
        for h in range(H):
            o = h * MLA_QK_PAD
            q_ref[:, o:o + MLA_NOPE] = q[:, o:o + MLA_NOPE].astype(BF16)
            qr = q[:, o + MLA_NOPE:o + MLA_QK_PAD]
            q_ref[:, o + MLA_NOPE:o + MLA_QK_PAD] = (qr * c + _rope_swap(qr) * s).astype(BF16)
            k_ref[:, o:o + MLA_NOPE] = kv[:, o:o + MLA_NOPE].astype(BF16)
            k_ref[:, o + MLA_NOPE:o + MLA_QK_PAD] = krf
            v_ref[:, h * MLA_V:(h + 1) * MLA_V] = kv[:, o + MLA_NOPE:o + MLA_QK_PAD].astype(BF16)

    def row(w):
        return pl.BlockSpec((tm, w), lambda i: (i, 0))

    def full(shape):
        return pl.BlockSpec(shape, lambda i: (0, 0))

    return pl.pallas_call(
        body, name="mla_mid_fwd", grid=(T // tm,),
        in_specs=[row(PW), full((1, QL)), full((1, KVL)), full(w_uq.shape), full(w_ukv.shape),
                  row(128), row(128)],
        out_specs=(row(QL), row(KVL), row(H * MLA_QK_PAD), row(H * MLA_QK_PAD), row(H * MLA_V)),
        out_shape=(jax.ShapeDtypeStruct((T, QL), BF16), jax.ShapeDtypeStruct((T, KVL), BF16),
                   jax.ShapeDtypeStruct((T, H * MLA_QK_PAD), BF16),
                   jax.ShapeDtypeStruct((T, H * MLA_QK_PAD), BF16),
                   jax.ShapeDtypeStruct((T, H * MLA_V), BF16)),
        compiler_params=_cparams("parallel"))(proj, q_norm, kv_norm, w_uq, w_ukv, cc, ss)


def _mla_mid_bwd(proj, q_norm, kv_norm, w_uq, w_ukv, cc, ss, dq, dk, dv):
    T, PW = proj.shape
    QL, KVL = q_norm.shape[-1], kv_norm.shape[-1]
    H = MLA_HEADS
    tm = _row_tile(T)
    nt = (((1,), (1,)), ((), ()))

    def body(p_ref, qn_ref, kn_ref, wq_ref, wkv_ref, cc_ref, ss_ref, dq_ref, dk_ref, dv_ref,
             dqp_ref, dkv_ref, dp_ref, dqn_ref, dkn_ref):
        @pl.when(pl.program_id(0) == 0)
        def _():
            dqn_ref[...] = jnp.zeros_like(dqn_ref)
            dkn_ref[...] = jnp.zeros_like(dkn_ref)

        c, s = cc_ref[...], ss_ref[...]
        dkr = jnp.zeros((tm, 128), F32)
        for h in range(H):
            o = h * MLA_QK_PAD
            dqp_ref[:, o:o + MLA_NOPE] = dq_ref[:, o:o + MLA_NOPE].astype(BF16)
            dqr = dq_ref[:, o + MLA_NOPE:o + MLA_QK_PAD]
            dqp_ref[:, o + MLA_NOPE:o + MLA_QK_PAD] = (dqr * c + _rope_swap(dqr * s)).astype(BF16)
            dkv_ref[:, o:o + MLA_NOPE] = dk_ref[:, o:o + MLA_NOPE].astype(BF16)
            dkv_ref[:, o + MLA_NOPE:o + MLA_QK_PAD] = dv_ref[:, h * MLA_V:(h + 1) * MLA_V].astype(BF16)
            dkr = dkr + dk_ref[:, o + MLA_NOPE:o + MLA_QK_PAD]
        dcqn = lax.dot_general(dqp_ref[...], wq_ref[...], nt, preferred_element_type=F32)
        dckvn = lax.dot_general(dkv_ref[...], wkv_ref[...], nt, preferred_element_type=F32)
        dcq, dqn = _rms_bwd_tile(p_ref[:, 0:QL], qn_ref[...], dcqn)
        dckv, dkn = _rms_bwd_tile(p_ref[:, QL:QL + KVL], kn_ref[...], dckvn)
        dp_ref[:, 0:QL] = dcq.astype(BF16)
        dp_ref[:, QL:QL + KVL] = dckv.astype(BF16)
        dp_ref[:, QL + KVL:QL + KVL + 128] = (dkr * c + _rope_swap(dkr * s)).astype(BF16)
        dqn_ref[...] += dqn
        dkn_ref[...] += dkn

    def row(w):
        return pl.BlockSpec((tm, w), lambda i: (i, 0))

    def full(shape):
        return pl.BlockSpec(shape, lambda i: (0, 0))

    return pl.pallas_call(
        body, name="mla_mid_bwd", grid=(T // tm,),
        in_specs=[row(PW), full((1, QL)), full((1, KVL)), full(w_uq.shape), full(w_ukv.shape),
                  row(128), row(128), row(H * MLA_QK_PAD), row(H * MLA_QK_PAD), row(H * MLA_V)],
        out_specs=(row(H * MLA_QK_PAD), row(H * MLA_QK_PAD), row(PW), full((1, QL)), full((1, KVL))),
        out_shape=(jax.ShapeDtypeStruct((T, H * MLA_QK_PAD), BF16),
                   jax.ShapeDtypeStruct((T, H * MLA_QK_PAD), BF16),
                   jax.ShapeDtypeStruct((T, PW), BF16),
                   jax.ShapeDtypeStruct((1, QL), F32), jax.ShapeDtypeStruct((1, KVL), F32)),
        compiler_params=_cparams("arbitrary"))(proj, q_norm, kv_norm, w_uq, w_ukv, cc, ss, dq, dk, dv)


def _attn_tile(T):
    return min(1024, T)


def _attn_pairs(n, by_key):
    if by_key:
        pairs = [(qi, ki) for ki in range(n) for qi in range(ki, n)]
    else:
        pairs = [(qi, ki) for qi in range(n) for ki in range(qi + 1)]
    return (jnp.asarray([p[0] for p in pairs], jnp.int32), jnp.asarray([p[1] for p in pairs], jnp.int32))


def _scores(q, k, scale, diagonal):
    s = lax.dot_general(q, k, (((1,), (1,)), ((), ())), preferred_element_type=F32) * scale
    if diagonal:
        rows = lax.broadcasted_iota(jnp.int32, s.shape, 0)
        cols = lax.broadcasted_iota(jnp.int32, s.shape, 1)
        s = jnp.where(rows >= cols, s, -jnp.inf)
    return s


def _attn_fwd(q, k, v):
    T = q.shape[0]
    H, DQ, DV = MLA_HEADS, MLA_QK_PAD, MLA_V
    tq = _attn_tile(T)
    nq = T // tq
    scale = float(MLA_NOPE + MLA_ROPE) ** -0.5
    G = MLA_HEADS_PER_STEP
    qi_tab, ki_tab = _attn_pairs(nq, by_key=False)

    def body(qi_ref, ki_ref, q_ref, k_ref, v_ref, o_ref, lse_ref, *scratch):
        m_refs, l_refs, acc_refs = scratch[0:G], scratch[G:2 * G], scratch[2 * G:3 * G]
        p = pl.program_id(1)
        qi, ki = qi_ref[p], ki_ref[p]

        @pl.when(ki == 0)
        def _():
            for g in range(G):
                m_refs[g][...] = jnp.full_like(m_refs[g], -jnp.inf)
                l_refs[g][...] = jnp.zeros_like(l_refs[g])
                acc_refs[g][...] = jnp.zeros_like(acc_refs[g])

        def update(diagonal):
            for g in range(G):
                qs, vs = slice(g * DQ, (g + 1) * DQ), slice(g * DV, (g + 1) * DV)
                s = _scores(q_ref[:, qs], k_ref[:, qs], scale, diagonal)
                m_prev = m_refs[g][...]
                m_new = jnp.maximum(m_prev, jnp.max(s, axis=1, keepdims=True))
                alpha = jnp.exp(m_prev - m_new)
                pr = jnp.exp(s - m_new)
                l_refs[g][...] = alpha * l_refs[g][...] + jnp.sum(pr, axis=1, keepdims=True)
                acc_refs[g][...] = alpha * acc_refs[g][...] + jnp.dot(pr.astype(BF16), v_ref[:, vs],
                                                                      preferred_element_type=F32)
                m_refs[g][...] = m_new

        @pl.when(ki < qi)
        def _():
            update(False)

        @pl.when(ki == qi)
        def _():
            update(True)
            for g in range(G):
                vs = slice(g * DV, (g + 1) * DV)
                o_ref[:, vs] = (acc_refs[g][...] / l_refs[g][...]).astype(BF16)
                lse_ref[g] = m_refs[g][...] + jnp.log(l_refs[g][...])

    return pl.pallas_call(
        body, name="attn_fwd",
        grid_spec=pltpu.PrefetchScalarGridSpec(
            num_scalar_prefetch=2, grid=(H // G, int(qi_tab.shape[0])),
            in_specs=[pl.BlockSpec((tq, G * DQ), lambda h, p, qt, kt: (qt[p], h)),
                      pl.BlockSpec((tq, G * DQ), lambda h, p, qt, kt: (kt[p], h)),
                      pl.BlockSpec((tq, G * DV), lambda h, p, qt, kt: (kt[p], h))],
            out_specs=(pl.BlockSpec((tq, G * DV), lambda h, p, qt, kt: (qt[p], h)),
                       pl.BlockSpec((G, tq, 1), lambda h, p, qt, kt: (h, qt[p], 0))),
            scratch_shapes=([pltpu.VMEM((tq, 1), F32)] * (2 * G) + [pltpu.VMEM((tq, DV), F32)] * G)),
        out_shape=(jax.ShapeDtypeStruct((T, H * DV), BF16), jax.ShapeDtypeStruct((H, T, 1), F32)),
        compiler_params=_cparams("parallel", "arbitrary"))(qi_tab, ki_tab, q, k, v)


def _attn_bwd(q, k, v, o, do, lse):
    T = q.shape[0]
    H, DQ, DV = MLA_HEADS, MLA_QK_PAD, MLA_V
    tq = _attn_tile(T)
    nq = T // tq
    scale = float(MLA_NOPE + MLA_ROPE) ** -0.5
    tn = (((0,), (0,)), ((), ()))
    nt = (((1,), (1,)), ((), ()))
    G = MLA_HEADS_PER_STEP
    qi_tab, ki_tab = _attn_pairs(nq, by_key=True)

    def body(qi_ref, ki_ref, q_ref, k_ref, v_ref, o_ref, do_ref, lse_ref, dq_ref, dk_ref, dv_ref,
             dk_acc, dv_acc):
        p = pl.program_id(1)
        qi, ki = qi_ref[p], ki_ref[p]

        @pl.when(p == 0)
        def _():
            dq_ref[...] = jnp.zeros_like(dq_ref)

        @pl.when(qi == ki)
        def _():
            dk_acc[...] = jnp.zeros_like(dk_acc)
            dv_acc[...] = jnp.zeros_like(dv_acc)

        def step(diagonal):
            rows = pl.ds(pl.multiple_of(qi * tq, tq), tq)
            for g in range(G):
                qs, vs = slice(g * DQ, (g + 1) * DQ), slice(g * DV, (g + 1) * DV)
                dof = do_ref[:, vs]
                delta = jnp.sum(dof.astype(F32) * o_ref[:, vs].astype(F32), axis=1, keepdims=True)
                s = _scores(q_ref[:, qs], k_ref[:, qs], scale, diagonal)
                pr = jnp.exp(s - lse_ref[g])
                dp = lax.dot_general(dof, v_ref[:, vs], nt, preferred_element_type=F32)
                ds = (pr * (dp - delta) * scale).astype(BF16)
                dv_acc[:, vs] += lax.dot_general(pr.astype(BF16), dof, tn, preferred_element_type=F32)
                dk_acc[:, qs] += lax.dot_general(ds, q_ref[:, qs], tn, preferred_element_type=F32)
                dq_ref[rows, qs] += jnp.dot(ds, k_ref[:, qs], preferred_element_type=F32)

        @pl.when(qi == ki)
        def _():
            step(True)

        @pl.when(qi > ki)
        def _():
            step(False)

        @pl.when(qi == nq - 1)
        def _():
            dk_ref[...] = dk_acc[...]
            dv_ref[...] = dv_acc[...]

    qspec = pl.BlockSpec((tq, G * DQ), lambda h, p, qt, kt: (qt[p], h))
    ospec = pl.BlockSpec((tq, G * DV), lambda h, p, qt, kt: (qt[p], h))
    kspec = pl.BlockSpec((tq, G * DQ), lambda h, p, qt, kt: (kt[p], h))
    vspec = pl.BlockSpec((tq, G * DV), lambda h, p, qt, kt: (kt[p], h))
    return pl.pallas_call(
        body, name="attn_bwd",
        grid_spec=pltpu.PrefetchScalarGridSpec(
            num_scalar_prefetch=2, grid=(H // G, int(qi_tab.shape[0])),
            in_specs=[qspec, kspec, vspec, ospec, ospec,
                      pl.BlockSpec((G, tq, 1), lambda h, p, qt, kt: (h, qt[p], 0))],
            out_specs=(pl.BlockSpec((T, G * DQ), lambda h, p, qt, kt: (0, h)), kspec, vspec),
            scratch_shapes=[pltpu.VMEM((tq, G * DQ), F32), pltpu.VMEM((tq, G * DV), F32)]),
        out_shape=(jax.ShapeDtypeStruct((T, H * DQ), F32), jax.ShapeDtypeStruct((T, H * DQ), F32),
                   jax.ShapeDtypeStruct((T, H * DV), F32)),
        compiler_params=_cparams("parallel", "arbitrary"))(qi_tab, ki_tab, q, k, v, o, do, lse)


def _mla_fwd(a, w, cc, ss, wbuf, pk, slot):
    D = a.shape[1]
    by_k, _, _ = _row_sharded(pk, "mla_w_o", slot, D // N_CHIPS)
    proj = _mm(a, w["w_in"], name="mla_in")
    cqn, ckvn, q, k, v = _mla_mid_fwd(proj, w["q_norm"], w["kv_norm"], w["w_uq"], w["w_ukv"], cc, ss)
    o, lse = _attn_fwd(q, k, v)
    m = _mm(o, wbuf, n=D, b_map=by_k, tk=D // N_CHIPS, tn=D, name="mla_out")
    return m, (a, proj, cqn, ckvn, q, k, v, o, lse)


def _mla_bwd(dm, saved, w, cc, ss, wbuf, gbuf, pk, slot):
    a, proj, cqn, ckvn, q, k, v, o, lse = saved
    D = a.shape[1]
    _, by_n, by_m = _row_sharded(pk, "mla_w_o", slot, D // N_CHIPS)
    do = _mm(dm, wbuf, tb=True, n=o.shape[1], b_map=by_n, tn=D // N_CHIPS, tk=D, out_dtype=BF16,
             name="mla_out_dx")
    gbuf = _mm(o, dm, ta=True, into=gbuf, o_map=by_m, tm=D // N_CHIPS, tn=D, name="mla_out_dw")
    dq, dk, dv = _attn_bwd(q, k, v, o, do, lse)
    dqp, dkv, dproj, dqn, dkn = _mla_mid_bwd(proj, w["q_norm"], w["kv_norm"], w["w_uq"], w["w_ukv"],
                                             cc, ss, dq, dk, dv)
    dw_uq = _mm(cqn, dqp, ta=True, out_dtype=BF16, name="mla_uq_dw")
    dw_ukv = _mm(ckvn, dkv, ta=True, out_dtype=BF16, name="mla_ukv_dw")
    dw_in = _mm(a, dproj, ta=True, out_dtype=BF16, name="mla_in_dw")
    da = _mm(dproj, w["w_in"], tb=True, name="mla_in_dx")
    return da, gbuf, dict(w_in=dw_in, w_uq=dw_uq, w_ukv=dw_ukv, q_norm=dqn, kv_norm=dkn)


def _split_dot(mat, x, parts):
    acc = None
    rem = x
    for _ in range(parts):
        piece = rem.astype(BF16)
        term = jnp.dot(mat, piece, preferred_element_type=F32)
        acc = term if acc is None else acc + term
        rem = rem - piece.astype(F32)
    return acc


def _chunk_mats(tb):
    C = HGRN_CHUNK
    assert C & (C - 1) == 0
    r = lax.broadcasted_iota(jnp.int32, (tb, tb), 0)
    s = lax.broadcasted_iota(jnp.int32, (tb, tb), 1)
    start = r & ~(C - 1)
    same = start == (s & ~(C - 1))
    ref = start + C // 2
    last = start + C - 1
    one, zero = jnp.float32(1.0), jnp.float32(0.0)
    cum = jnp.where(same & (s <= r), one, zero)
    rel = cum - jnp.where(same & (s <= ref), one, zero)
    rest = jnp.where(same & (s > r) & (s <= last), one, zero)
    rev = jnp.where(same & (s >= r), one, zero)
    ones = jnp.where(same, one, zero)
    causal = same & (s <= r)
    return cum, rel, rest, rev, ones, causal


def _hgrn_gates(p_ref, lb, HK):
    qx = p_ref[:, 0:HK]
    fx = p_ref[:, HK:2 * HK]
    sf = _sigmoid(fx)
    f = lb + (1.0 - lb) * sf
    sq = _sigmoid(qx)
    return qx, sq, qx * sq, sf, f, 1.0 - f, jnp.log(f)


def _hgrn_fwd(proj, lb, o_norm):
    T = proj.shape[0]
    H, C = HGRN_HEADS, HGRN_CHUNK
    HK = proj.shape[1] // 4
    DK = HK // H
    tb = min(HGRN_BLOCK, T)
    ncb = tb // C
    nt = (((1,), (1,)), ((), ()))
    tn = (((0,), (0,)), ((), ()))

    def body(p_ref, lb_ref, on_ref, y_ref, o_ref, st_ref, state, oacc):
        @pl.when(pl.program_id(0) == 0)
        def _():
            state[...] = jnp.zeros_like(state)

        cum, rel, rest, _, _, causal = _chunk_mats(tb)
        _, _, q, _, f, k, logf = _hgrn_gates(p_ref, lb_ref[...], HK)
        b = _split_dot(cum.astype(BF16), logf, 3)
        brel = _split_dot(rel.astype(BF16), logf, 3)
        brest = _split_dot(rest.astype(BF16), logf, 3)
        eb = jnp.exp(b)
        q_rel = (q * jnp.exp(brel)).astype(BF16)
        k_rel = (k * jnp.exp(-brel)).astype(BF16)
        q_dec = (q * eb).astype(BF16)
        k_dec = (k * jnp.exp(brest)).astype(BF16)
        v = p_ref[:, 2 * HK:3 * HK].astype(BF16)
        for h in range(H):
            hs = slice(h * DK, (h + 1) * DK)
            a = lax.dot_general(q_rel[:, hs], k_rel[:, hs], nt, preferred_element_type=F32)
            a = jnp.where(causal, a, 0.0).astype(BF16)
            oacc[:, hs] = jnp.dot(a, v[:, hs], preferred_element_type=F32)
            for j in range(ncb):
                rs = slice(j * C, (j + 1) * C)
                st = state[h]
                st_ref[j, h] = st
                oacc[rs, hs] += lax.dot_general(q_dec[rs, hs], st.astype(BF16), nt,
                                                preferred_element_type=F32)
                dec = jnp.exp(jnp.sum(logf[rs, hs], axis=0, keepdims=True))
                state[h] = dec * st + lax.dot_general(v[rs, hs], k_dec[rs, hs], tn,
                                                      preferred_element_type=F32)
        o = oacc[...]
        o_ref[...] = o
        gx = p_ref[:, 3 * HK:4 * HK]
        gate = gx * _sigmoid(gx)
        for h in range(H):
            hs = slice(h * DK, (h + 1) * DK)
            oh = o[:, hs]
            y_ref[:, hs] = (oh * _rms_rstd(oh) * on_ref[...] * gate[:, hs]).astype(BF16)

    return pl.pallas_call(
        body, name="hgrn_fwd", grid=(T // tb,),
        in_specs=[pl.BlockSpec((tb, 4 * HK), lambda i: (i, 0)),
                  pl.BlockSpec((1, HK), lambda i: (0, 0)),
                  pl.BlockSpec((1, DK), lambda i: (0, 0))],
        out_specs=(pl.BlockSpec((tb, HK), lambda i: (i, 0)),
                   pl.BlockSpec((tb, HK), lambda i: (i, 0)),
                   pl.BlockSpec((ncb, H, DK, DK), lambda i: (i, 0, 0, 0))),
        out_shape=(jax.ShapeDtypeStruct((T, HK), BF16), jax.ShapeDtypeStruct((T, HK), F32),
                   jax.ShapeDtypeStruct((T // C, H, DK, DK), F32)),
        scratch_shapes=[pltpu.VMEM((H, DK, DK), F32), pltpu.VMEM((tb, HK), F32)],
        compiler_params=_cparams("arbitrary"))(proj, lb, o_norm)


def _hgrn_bwd(proj, lb, o_norm, o, states, dy):
    T = proj.shape[0]
    H, C = HGRN_HEADS, HGRN_CHUNK
    HK = proj.shape[1] // 4
    DK = HK // H
    tb = min(HGRN_BLOCK, T)
    ncb = tb // C
    nb = T // tb
    nt = (((1,), (1,)), ((), ()))
    tn = (((0,), (0,)), ((), ()))

    def body(p_ref, lb_ref, on_ref, o_ref, st_ref, dy_ref, dp_ref, dlb_ref, don_ref,
             dstate, dqr_s, dkr_s, dqd_s, dkd_s, dv_s, do_s, e_s):
        @pl.when(pl.program_id(0) == 0)
        def _():
            dstate[...] = jnp.zeros_like(dstate)
            dlb_ref[...] = jnp.zeros_like(dlb_ref)
            don_ref[...] = jnp.zeros_like(don_ref)

        cum, rel, rest, rev, ones, causal = _chunk_mats(tb)
        lb = lb_ref[...]
        qx, sq, q, sf, f, k, logf = _hgrn_gates(p_ref, lb, HK)
        b = _split_dot(cum.astype(BF16), logf, 3)
        brel = _split_dot(rel.astype(BF16), logf, 3)
        brest = _split_dot(rest.astype(BF16), logf, 3)
        eb = jnp.exp(b)
        erel = jnp.exp(brel)
        enrel = jnp.exp(-brel)
        erest = jnp.exp(brest)
        q_rel_f, k_rel_f, q_dec_f, k_dec_f = q * erel, k * enrel, q * eb, k * erest
        q_rel, k_rel = q_rel_f.astype(BF16), k_rel_f.astype(BF16)
        q_dec, k_dec = q_dec_f.astype(BF16), k_dec_f.astype(BF16)
        v = p_ref[:, 2 * HK:3 * HK].astype(BF16)

        gx = p_ref[:, 3 * HK:4 * HK]
        sg = _sigmoid(gx)
        gate = gx * sg
        dy = dy_ref[...]
        ov = o_ref[...]
        on = on_ref[...]
        don = jnp.zeros((1, DK), F32)
        for h in range(H):
            hs = slice(h * DK, (h + 1) * DK)
            oh = ov[:, hs]
            r = _rms_rstd(oh)
            xh = oh * r
            d_on = dy[:, hs] * gate[:, hs]
            don = don + jnp.sum(d_on * xh, axis=0, keepdims=True)
            u = d_on * on
            do_s[:, hs] = r * (u - xh * jnp.mean(u * xh, axis=-1, keepdims=True))
            dp_ref[:, 3 * HK + h * DK:3 * HK + (h + 1) * DK] = (
                dy[:, hs] * xh * on * (sg[:, hs] * (1.0 + gx[:, hs] * (1.0 - sg[:, hs])))).astype(BF16)
        don_ref[...] += don

        for h in range(H):
            hs = slice(h * DK, (h + 1) * DK)
            doh = do_s[:, hs].astype(BF16)
            a = lax.dot_general(q_rel[:, hs], k_rel[:, hs], nt, preferred_element_type=F32)
            a = jnp.where(causal, a, 0.0).astype(BF16)
            da = lax.dot_general(doh, v[:, hs], nt, preferred_element_type=F32)
            da = jnp.where(causal, da, 0.0).astype(BF16)
            dv_s[:, hs] = lax.dot_general(a, doh, tn, preferred_element_type=F32)
            dqr_s[:, hs] = jnp.dot(da, k_rel[:, hs], preferred_element_type=F32)
            dkr_s[:, hs] = lax.dot_general(da, q_rel[:, hs], tn, preferred_element_type=F32)
            for j in reversed(range(ncb)):
                rs = slice(j * C, (j + 1) * C)
                dst = dstate[h]
                dstb = dst.astype(BF16)
                st = st_ref[j, h]
                dkd_s[rs, hs] = jnp.dot(v[rs, hs], dstb, preferred_element_type=F32)
                dv_s[rs, hs] += lax.dot_general(k_dec[rs, hs], dstb, nt, preferred_element_type=F32)
                dec = jnp.exp(jnp.sum(logf[rs, hs], axis=0, keepdims=True))
                e_s[rs, hs] = jnp.broadcast_to(jnp.sum(dst * st, axis=0, keepdims=True) * dec, (C, DK))
                dqd_s[rs, hs] = jnp.dot(doh[rs], st.astype(BF16), preferred_element_type=F32)
                dstate[h] = dec * dst + lax.dot_general(doh[rs], q_dec[rs, hs], tn,
                                                        preferred_element_type=F32)

        dqr, dkr, dqd, dkd = dqr_s[...], dkr_s[...], dqd_s[...], dkd_s[...]
        kdk = dkd * k_dec_f
        db = dqr * q_rel_f - dkr * k_rel_f + dqd * q_dec_f - kdk
        dlogf = _split_dot(rev.astype(BF16), db, 2) + _split_dot(ones.astype(BF16), kdk, 2) + e_s[...]
        dk = dkr * enrel + dkd * erest
        df = dlogf / f - dk
        dlb_ref[...] += jnp.sum(df * (1.0 - sf), axis=0, keepdims=True)
        dq = dqr * erel + dqd * eb
        dp_ref[:, 0:HK] = (dq * (sq * (1.0 + qx * (1.0 - sq)))).astype(BF16)
        dp_ref[:, HK:2 * HK] = (df * (1.0 - lb) * sf * (1.0 - sf)).astype(BF16)
        dp_ref[:, 2 * HK:3 * HK] = dv_s[...].astype(BF16)

    rev_row = lambda w: pl.BlockSpec((tb, w), lambda i: (nb - 1 - i, 0))
    vec = lambda w: pl.BlockSpec((1, w), lambda i: (0, 0))
    scr = pltpu.VMEM((tb, HK), F32)
    return pl.pallas_call(
        body, name="hgrn_bwd", grid=(nb,),
        in_specs=[rev_row(4 * HK), vec(HK), vec(DK), rev_row(HK),
                  pl.BlockSpec((ncb, H, DK, DK), lambda i: (nb - 1 - i, 0, 0, 0)), rev_row(HK)],
        out_specs=(rev_row(4 * HK), vec(HK), vec(DK)),
        out_shape=(jax.ShapeDtypeStruct((T, 4 * HK), BF16), jax.ShapeDtypeStruct((1, HK), F32),
                   jax.ShapeDtypeStruct((1, DK), F32)),
        scratch_shapes=[pltpu.VMEM((H, DK, DK), F32), scr, scr, scr, scr, scr, scr, scr],
        compiler_params=_cparams("arbitrary"))(proj, lb, o_norm, o, states, dy)


def _hgrn_layer_fwd(a, o_norm, lb, wbuf, pk, slot):
    D = a.shape[1]
    in_by_n, _ = _col_sharded(pk, "hgrn_w_in", slot, D)
    out_by_k, _, _ = _row_sharded(pk, "hgrn_w_o", slot, D // N_CHIPS)
    proj = _mm(a, wbuf, n=4 * D, b_map=in_by_n, tk=D, tn=D, name="hgrn_in")
    y, o, states = _hgrn_fwd(proj, lb, o_norm)
    m = _mm(y, wbuf, n=D, b_map=out_by_k, tk=D // N_CHIPS, tn=D, name="hgrn_out")
    return m, (a, proj, y, o, states)


def _hgrn_layer_bwd(dm, saved, o_norm, lb, wbuf, gbuf, pk, slot):
    a, proj, y, o, states = saved
    D = a.shape[1]
    in_by_n, in_by_k = _col_sharded(pk, "hgrn_w_in", slot, D)
    _, out_by_n, out_by_m = _row_sharded(pk, "hgrn_w_o", slot, D // N_CHIPS)
    dy = _mm(dm, wbuf, tb=True, n=y.shape[1], b_map=out_by_n, tn=D // N_CHIPS, tk=D, name="hgrn_out_dx")
    gbuf = _mm(y, dm, ta=True, into=gbuf, o_map=out_by_m, tm=D // N_CHIPS, tn=D, name="hgrn_out_dw")
    dproj, dlb, don = _hgrn_bwd(proj, lb, o_norm, o, states, dy)
    gbuf = _mm(a, dproj, ta=True, into=gbuf, o_map=in_by_n, tm=D, tn=D, name="hgrn_in_dw")
    da = _mm(dproj, wbuf, tb=True, n=D, b_map=in_by_k, tn=D, tk=D, name="hgrn_in_dx")
    return da, gbuf, dict(o_norm=don, lb=dlb)


def _lower_bounds(lb_logits):
    p = jax.nn.softmax(lb_logits.astype(F32), axis=0)
    return jnp.cumsum(p, axis=0) - p[0]


def _rope_tables(positions):
    inv_freq = jnp.power(ROPE_BASE, -jnp.arange(0, MLA_ROPE, 2, dtype=F32) / MLA_ROPE)
    ang = positions.astype(F32)[:, None] * inv_freq
    cos, sin = jnp.cos(ang), jnp.sin(ang)
    zero = jnp.zeros((positions.shape[0], 128 - MLA_ROPE), F32)
    return (jnp.concatenate([cos, cos, zero], axis=-1), jnp.concatenate([-sin, sin, zero], axis=-1))


def _pad_mla_weights(w_in, w_uq):
    w_in_p = jnp.pad(w_in, ((0, 0), (0, 0), (0, 128 - MLA_ROPE)))
    n, ql, _ = w_uq.shape
    w_uq_p = jnp.pad(w_uq.reshape(n, ql, MLA_HEADS, MLA_NOPE + MLA_ROPE),
                     ((0, 0), (0, 0), (0, 0), (0, MLA_QK_PAD - MLA_NOPE - MLA_ROPE)))
    return w_in_p, w_uq_p.reshape(n, ql, MLA_HEADS * MLA_QK_PAD)


def _local_step(x, positions, target, wbuf, gbuf, pk, wb, small):
    T, D = x.shape
    gains = small["norm_gains"]
    lbounds, lb_vjp = jax.vjp(_lower_bounds, small["hgrn_lb_logits"])
    cc, ss = _rope_tables(positions)
    w_in_p, w_uq_p = _pad_mla_weights(wb["mla_w_in"], wb["mla_w_uq"])

    def g(layer, i):
        return gains[layer, i][None, :]

    def mla_weights(slot):
        return dict(w_in=w_in_p[slot], w_uq=w_uq_p[slot], w_ukv=wb["mla_w_ukv"][slot],
                    q_norm=small["mla_q_norm"][slot][None, :], kv_norm=small["mla_kv_norm"][slot][None, :])

    saved = []
    h = x
    a = _prenorm_fwd(x, g(0, 0))
    dy = sq = None
    for layer in range(DEPTH):
        slot = layer // 2
        if layer % 2 == 0:
            m, mix_saved = _mla_fwd(a, mla_weights(slot), cc, ss, wbuf, pk, slot)
        else:
            m, mix_saved = _hgrn_layer_fwd(a, small["hgrn_o_norm"][slot][None, :], lbounds[layer][None, :],
                                           wbuf, pk, slot)
        h1, a2 = _resnorm_fwd(h, m, g(layer, 1), g(layer, 2), name="resnorm_fwd_mix")
        u, mlp_saved = _mlp_fwd(a2, wbuf, pk, layer)
        if layer + 1 < DEPTH:
            h2, a = _resnorm_fwd(h1, u, g(layer, 3), g(layer + 1, 0), name="resnorm_fwd_mlp")
        else:
            h2 = None
            dy, sq = _resnorm_loss(h1, u, g(layer, 3), target)
        saved.append((h, m, h1, u, mix_saved, mlp_saved))
        h = h2

    n_mla, n_hgrn = (DEPTH + 1) // 2, DEPTH // 2
    dgains = [[None] * 4 for _ in range(DEPTH)]
    gw = {k: [None] * n_mla for k in ("mla_w_in", "mla_w_uq", "mla_w_ukv", "mla_q_norm", "mla_kv_norm")}
    gw["hgrn_o_norm"] = [None] * n_hgrn
    dlb = [jnp.zeros((1, lbounds.shape[1]), F32) for _ in range(DEPTH)]
    dh = dy
    da_next = None
    for layer in reversed(range(DEPTH)):
        h0, m, h1, u, mix_saved, mlp_saved = saved[layer]
        slot = layer // 2
        if da_next is None:
            du, dgains[layer][3] = _resnorm_bwd(u, g(layer, 3), dh, name="resnorm_bwd_last")
            t = dh
        else:
            h2 = saved[layer + 1][0]
            t, du, dgains[layer][3], dgains[layer + 1][0] = _resnorm_bwd(
                u, g(layer, 3), dh, h2, da_next, g(layer + 1, 0), name="resnorm_bwd_mlp")
        da2, gbuf = _mlp_bwd(du, mlp_saved, wbuf, gbuf, pk, layer)
        t, dm, dgains[layer][1], dgains[layer][2] = _resnorm_bwd(
            m, g(layer, 1), t, h1, da2, g(layer, 2), name="resnorm_bwd_mix")
        if layer % 2 == 0:
            da_next, gbuf, mg = _mla_bwd(dm, mix_saved, mla_weights(slot), cc, ss, wbuf, gbuf, pk, slot)
            ql = mg["q_norm"].shape[-1]
            kvl = mg["kv_norm"].shape[-1]
            gw["mla_w_in"][slot] = mg["w_in"][:, :ql + kvl + MLA_ROPE]
            gw["mla_w_uq"][slot] = mg["w_uq"].reshape(ql, MLA_HEADS, MLA_QK_PAD)[
                :, :, :MLA_NOPE + MLA_ROPE].reshape(ql, MLA_HEADS * (MLA_NOPE + MLA_ROPE))
            gw["mla_w_ukv"][slot] = mg["w_ukv"]
            gw["mla_q_norm"][slot] = mg["q_norm"][0]
            gw["mla_kv_norm"][slot] = mg["kv_norm"][0]
        else:
            da_next, gbuf, hg = _hgrn_layer_bwd(dm, mix_saved, small["hgrn_o_norm"][slot][None, :],
                                                lbounds[layer][None, :], wbuf, gbuf, pk, slot)
            gw["hgrn_o_norm"][slot] = hg["o_norm"][0]
            dlb[layer] = hg["lb"]
        dh = t
    grad_x, dgains[0][0] = _prenorm_bwd(x, g(0, 0), dh, da_next)

    grads = {k: jnp.stack(vs) for k, vs in gw.items()}
    grads["norm_gains"] = jnp.stack([jnp.concatenate(row, axis=0) for row in dgains])
    (grads["hgrn_lb_logits"],) = lb_vjp(jnp.concatenate(dlb, axis=0))
    return sq, grad_x, gbuf, grads


def _size(shape):
    n = 1
    for d in shape:
        n *= d
    return n


def _piece_rows(shape):
    return -(-_size(shape) // PACK_W)


def _packed_misc_rows(shapes):
    return sum(_piece_rows(s) for s in shapes)


def _cast_into(src, buf, row, name):
    rows, W = src.shape
    tr = min(256, rows)
    assert rows % tr == 0 and row % tr == 0

    def body(s_ref, b_ref, o_ref):
        o_ref[...] = s_ref[...].astype(BF16)

    return pl.pallas_call(
        body, name=name, grid=(rows // tr,),
        in_specs=[pl.BlockSpec((tr, W), lambda i: (i, 0)), pl.BlockSpec(memory_space=pl.ANY)],
        out_specs=pl.BlockSpec((tr, W), lambda i: (row // tr + i, 0)),
        out_shape=jax.ShapeDtypeStruct(buf.shape, buf.dtype), input_output_aliases={1: 0},
        compiler_params=_cparams("parallel"))(src, buf)


def _pack_blocks(pieces, rows, dtype):
    blocks, used = [], 0
    for p in pieces:
        flat = p.astype(dtype).reshape(-1)
        r = _piece_rows(p.shape)
        if r * PACK_W != flat.shape[0]:
            flat = jnp.pad(flat, (0, r * PACK_W - flat.shape[0]))
        blocks.append(flat.reshape(r, PACK_W))
        used += r
    if rows > used:
        blocks.append(jnp.zeros((rows - used, PACK_W), dtype))
    return blocks


def _unpack(buf, shapes):
    out, off = [], 0
    for shp in shapes:
        r = _piece_rows(shp)
        piece = buf[off:off + r]
        if r * PACK_W != _size(shp):
            piece = piece.reshape(-1)[:_size(shp)]
        out.append(piece.reshape(shp))
        off += r
    return out


def _mesh_place():
    x, y, c = lax.axis_index("x"), lax.axis_index("y"), lax.axis_index("c")
    chips = [(1 - x, y), (x, 1 - y), (1 - x, 1 - y)]
    return x, y, c, chips


_HBM = pl.BlockSpec(memory_space=pltpu.HBM)


def _all_gather(wp):
    R, W = wp.shape
    rh = R // 2
    rq = rh // 2
    assert rq % 16 == 0

    def body(w_ref, out_ref, send_sems, recv_sems):
        x, y, c, _ = _mesh_place()
        me, jx, jy, jd = 2 * x + y, 2 * (1 - x) + y, 2 * x + (1 - y), 2 * (1 - x) + (1 - y)
        to_x, to_y, sibling = (1 - x, y, c), (x, 1 - y, c), (x, y, 1 - c)

        def rows(core, quarter):
            return pl.ds(pl.multiple_of(core * rh + quarter * rq, 16), rq)

        def slot(j, core, quarter):
            return out_ref.at[j, rows(core, quarter)]

        def copy(k, src, dst, to):
            return pltpu.make_async_remote_copy(src_ref=src, dst_ref=dst, send_sem=send_sems.at[k],
                                                recv_sem=recv_sems.at[k], device_id=to, device_id_type=MESH)

        sends = [copy(0, w_ref.at[rows(c, 0)], slot(me, c, 0), to_x),
                 copy(2, w_ref.at[rows(c, 1)], slot(me, c, 1), to_y),
                 copy(1, w_ref.at[rows(c, 1)], slot(me, c, 1), to_x),
                 copy(3, w_ref.at[rows(c, 0)], slot(me, c, 0), to_y)]
        for cp in sends:
            cp.start()
        arrivals = [(0, slot(jx, c, 0), 4, to_y, 6), (2, slot(jy, c, 1), 5, to_x, 7),
                    (1, slot(jx, c, 1), None, None, 8), (3, slot(jy, c, 0), None, None, 9),
                    (4, slot(jd, c, 0), None, None, 10), (5, slot(jd, c, 1), None, None, 11)]
        for k, landed, k_on, to_on, k_sib in arrivals:
            copy(k, landed, landed, sibling).wait_recv()
            if k_on is not None:
                cp = copy(k_on, landed, landed, to_on)
                cp.start()
                sends.append(cp)
            cp = copy(k_sib, landed, landed, sibling)
            cp.start()
            sends.append(cp)
        for k_sib, j, quarter in ((6, jx, 0), (7, jy, 1), (8, jx, 1), (9, jy, 0), (10, jd, 0), (11, jd, 1)):
            landed = slot(j, 1 - c, quarter)
            copy(k_sib, landed, landed, sibling).wait_recv()
        for cp in sends:
            cp.wait_send()

    out = pl.pallas_call(
        body, name="weights_all_gather", in_specs=[_HBM], out_specs=_HBM,
        out_shape=jax.ShapeDtypeStruct((N_CHIPS, R, W), wp.dtype),
        scratch_shapes=[pltpu.SemaphoreType.DMA((12,)), pltpu.SemaphoreType.DMA((12,))],
    )(wp)
    me = 2 * lax.axis_index("x") + lax.axis_index("y")
    return lax.dynamic_update_slice(out, wp[None], (me, 0, 0))


def _exchange_halves(g):
    n, _, rh, W = g.shape

    def body(g_ref, out_ref, send_sems, recv_sems):
        x, y, c, _ = _mesh_place()
        sibling = (x, y, 1 - c)
        copies = [pltpu.make_async_remote_copy(
            src_ref=g_ref.at[j, 1 - c], dst_ref=out_ref.at[j], send_sem=send_sems.at[j],
            recv_sem=recv_sems.at[j], device_id=sibling, device_id_type=MESH) for j in range(n)]
        for cp in copies:
            cp.start()
        for cp in copies:
            cp.wait()

    return pl.pallas_call(
        body, name="grads_to_sibling", in_specs=[_HBM], out_specs=_HBM,
        out_shape=jax.ShapeDtypeStruct((n, rh, W), g.dtype),
        scratch_shapes=[pltpu.SemaphoreType.DMA((n,)), pltpu.SemaphoreType.DMA((n,))],
    )(g)


def _scatter_to_owners(p):
    n, rh, W = p.shape
    rq = rh // 2
    assert rq % 16 == 0

    def body(p_ref, out_ref, stage_ref, send_sems, recv_sems):
        x, y, c, _ = _mesh_place()
        me, jx, jy, jd = 2 * x + y, 2 * (1 - x) + y, 2 * x + (1 - y), 2 * (1 - x) + (1 - y)
        to_x, to_y = (1 - x, y, c), (x, 1 - y, c)

        def quarter(ref, j, q):
            return ref.at[j, pl.ds(q * rq, rq)]

        def copy(k, src, dst, to):
            return pltpu.make_async_remote_copy(src_ref=src, dst_ref=dst, send_sem=send_sems.at[k],
                                                recv_sem=recv_sems.at[k], device_id=to, device_id_type=MESH)

        sends = [copy(2, quarter(p_ref, jd, 0), stage_ref.at[0], to_x),
                 copy(3, quarter(p_ref, jd, 1), stage_ref.at[1], to_y),
                 copy(0, p_ref.at[jx], out_ref.at[me], to_x),
                 copy(1, p_ref.at[jy], out_ref.at[me], to_y)]
        for cp in sends:
            cp.start()
        copy(2, stage_ref.at[0], stage_ref.at[0], to_x).wait_recv()
        relay = copy(4, stage_ref.at[0], quarter(out_ref, jx, 0), to_y)
        relay.start()
        sends.append(relay)
        copy(3, stage_ref.at[1], stage_ref.at[1], to_y).wait_recv()
        relay = copy(5, stage_ref.at[1], quarter(out_ref, jy, 1), to_x)
        relay.start()
        sends.append(relay)
        copy(0, out_ref.at[jx], out_ref.at[jx], to_x).wait_recv()
        copy(1, out_ref.at[jy], out_ref.at[jy], to_y).wait_recv()
        copy(4, quarter(out_ref, jd, 0), quarter(out_ref, jd, 0), to_y).wait_recv()
        copy(5, quarter(out_ref, jd, 1), quarter(out_ref, jd, 1), to_x).wait_recv()
        for cp in sends:
            cp.wait_send()

    out, _ = pl.pallas_call(
        body, name="grads_to_owner", in_specs=[_HBM], out_specs=(_HBM, _HBM),
        out_shape=(jax.ShapeDtypeStruct((n, rh, W), p.dtype), jax.ShapeDtypeStruct((2, rq, W), p.dtype)),
        scratch_shapes=[pltpu.SemaphoreType.DMA((6,)), pltpu.SemaphoreType.DMA((6,))],
    )(p)
    me = 2 * lax.axis_index("x") + lax.axis_index("y")
    mine = lax.dynamic_index_in_dim(p, me, axis=0, keepdims=True)
    return lax.dynamic_update_slice(out, mine, (me, 0, 0))


def _share_reduced(q):
    rh, W = q.shape

    def body(q_ref, out_ref, send_sem, recv_sem):
        x, y, c, _ = _mesh_place()
        cp = pltpu.make_async_remote_copy(src_ref=q_ref, dst_ref=out_ref.at[c], send_sem=send_sem,
                                          recv_sem=recv_sem, device_id=(x, y, 1 - c), device_id_type=MESH)
        cp.start()
        cp.wait()

    out = pl.pallas_call(
        body, name="grads_share_reduced", in_specs=[_HBM], out_specs=_HBM,
        out_shape=jax.ShapeDtypeStruct((2, rh, W), q.dtype),
        scratch_shapes=[pltpu.SemaphoreType.DMA, pltpu.SemaphoreType.DMA],
    )(q)
    return lax.dynamic_update_slice(out, q[None], (lax.axis_index("c"), 0, 0))


def _add_sibling(g, recv, c_arr):
    n, _, rh, W = g.shape
    tr = PACK_TILE

    def body(c_ref, g_ref, r_ref, o_ref):
        o_ref[...] = (g_ref[...].astype(F32) + r_ref[...].astype(F32)).astype(BF16)

    return pl.pallas_call(
        body, name="grads_add_sibling",
        grid_spec=pltpu.PrefetchScalarGridSpec(
            num_scalar_prefetch=1, grid=(n, rh // tr),
            in_specs=[pl.BlockSpec((None, None, tr, W), lambda j, i, c_ref: (j, c_ref[0], i, 0)),
                      pl.BlockSpec((None, tr, W), lambda j, i, c_ref: (j, i, 0))],
            out_specs=pl.BlockSpec((None, tr, W), lambda j, i, c_ref: (j, i, 0))),
        out_shape=jax.ShapeDtypeStruct((n, rh, W), BF16),
        compiler_params=_cparams("parallel", "parallel"))(c_arr, g, recv)


def _sum_chips(parts):
    n, rh, W = parts.shape
    tr = PACK_TILE

    def body(p_ref, o_ref):
        acc = p_ref[0].astype(F32)
        for j in range(1, n):
            acc = acc + p_ref[j].astype(F32)
        o_ref[...] = acc

    return pl.pallas_call(
        body, name="grads_sum_chips", grid=(rh // tr,),
        in_specs=[pl.BlockSpec((n, tr, W), lambda i: (0, i, 0))],
        out_specs=pl.BlockSpec((tr, W), lambda i: (i, 0)),
        out_shape=jax.ShapeDtypeStruct((rh, W), F32),
        compiler_params=_cparams("parallel"))(parts)


def _adamw(w, g, m, v, name):
    shape = w.shape
    cols = shape[-1]
    w2, g2, m2, v2 = (t.reshape(-1, cols) for t in (w, g, m, v))
    rows = w2.shape[0]
    tr = rows
    for cand in (512, 256, 128, 64, 32, 16, 8):
        if rows > cand and rows % cand == 0:
            tr = cand
            break
    c1 = 1.0 / (1.0 - ADAM_B1 ** ADAM_STEP)
    c2 = 1.0 / (1.0 - ADAM_B2 ** ADAM_STEP)

    def body(w_ref, g_ref, m_ref, v_ref, d_ref, nm_ref, nv_ref):
        gv = g_ref[...]
        nm = ADAM_B1 * m_ref[...] + (1.0 - ADAM_B1) * gv
        nv = ADAM_B2 * v_ref[...] + (1.0 - ADAM_B2) * (gv * gv)
        nm_ref[...] = nm
        nv_ref[...] = nv
        d_ref[...] = -ADAM_LR * ((nm * c1) / (jnp.sqrt(nv * c2) + ADAM_EPS) + ADAM_WD * w_ref[...])

    blk = pl.BlockSpec((tr, cols), lambda i: (i, 0))
    sds = jax.ShapeDtypeStruct((rows, cols), F32)
    d, nm, nv = pl.pallas_call(body, name=name, grid=(rows // tr,), in_specs=[blk] * 4,
                               out_specs=(blk, blk, blk), out_shape=(sds, sds, sds),
                               compiler_params=_cparams("parallel"))(w2, g2, m2, v2)
    return d.reshape(shape), nm.reshape(shape), nv.reshape(shape)


def kernel(x, positions, norm_gains, mla_w_in, mla_q_norm, mla_kv_norm, mla_w_uq, mla_w_ukv, mla_w_o, hgrn_w_in, hgrn_lb_logits, hgrn_o_norm, hgrn_w_o, mlp_w1, mlp_w2, loss_target, m_norm_gains, m_mla_w_in, m_mla_q_norm, m_mla_kv_norm, m_mla_w_uq, m_mla_w_ukv, m_mla_w_o, m_hgrn_w_in, m_hgrn_lb_logits, m_hgrn_o_norm, m_hgrn_w_o, m_mlp_w1, m_mlp_w2, v_norm_gains, v_mla_w_in, v_mla_q_norm, v_mla_kv_norm, v_mla_w_uq, v_mla_w_ukv, v_mla_w_o, v_hgrn_w_in, v_hgrn_lb_logits, v_hgrn_o_norm, v_hgrn_w_o, v_mlp_w1, v_mlp_w2):
    w = dict(norm_gains=norm_gains, mla_w_in=mla_w_in, mla_q_norm=mla_q_norm, mla_kv_norm=mla_kv_norm,
             mla_w_uq=mla_w_uq, mla_w_ukv=mla_w_ukv, mla_w_o=mla_w_o, hgrn_w_in=hgrn_w_in,
             hgrn_lb_logits=hgrn_lb_logits, hgrn_o_norm=hgrn_o_norm, hgrn_w_o=hgrn_w_o,
             mlp_w1=mlp_w1, mlp_w2=mlp_w2)
    mom_m = dict(norm_gains=m_norm_gains, mla_w_in=m_mla_w_in, mla_q_norm=m_mla_q_norm,
                 mla_kv_norm=m_mla_kv_norm, mla_w_uq=m_mla_w_uq, mla_w_ukv=m_mla_w_ukv,
                 mla_w_o=m_mla_w_o, hgrn_w_in=m_hgrn_w_in, hgrn_lb_logits=m_hgrn_lb_logits,
                 hgrn_o_norm=m_hgrn_o_norm, hgrn_w_o=m_hgrn_w_o, mlp_w1=m_mlp_w1, mlp_w2=m_mlp_w2)
    mom_v = dict(norm_gains=v_norm_gains, mla_w_in=v_mla_w_in, mla_q_norm=v_mla_q_norm,
                 mla_kv_norm=v_mla_kv_norm, mla_w_uq=v_mla_w_uq, mla_w_ukv=v_mla_w_ukv,
                 mla_w_o=v_mla_w_o, hgrn_w_in=v_hgrn_w_in, hgrn_lb_logits=v_hgrn_lb_logits,
                 hgrn_o_norm=v_hgrn_o_norm, hgrn_w_o=v_hgrn_w_o, mlp_w1=v_mlp_w1, mlp_w2=v_mlp_w2)
    c = lax.axis_index("c")

    axis_of = dict(SHARDED)
    shard_shapes = {name: w[name].shape for name, _ in SHARDED}
    misc_w = ("mla_w_in", "mla_w_uq", "mla_w_ukv")
    misc_w_shapes = [shard_shapes[n] for n in misc_w] + [norm_gains.shape, norm_gains.shape]
    misc_g = misc_w + ("norm_gains",) + REPLICATED
    misc_g_shapes = [w[n].shape for n in misc_g]
    pk = _Packed(shard_shapes, misc_g_shapes)
    assert pk.width == PACK_W and _packed_misc_rows(misc_w_shapes) <= pk.misc_rows
    gain_bits = lax.bitcast_convert_type(norm_gains, jnp.uint32)
    gain_hi = lax.bitcast_convert_type((gain_bits >> 16).astype(jnp.uint16), BF16)
    gain_lo = lax.bitcast_convert_type((gain_bits & 0xFFFF).astype(jnp.uint16), BF16)
    wpack = jnp.zeros((pk.rows, PACK_W), BF16)
    for name in pk.BIG:
        wpack = _cast_into(w[name].reshape(-1, PACK_W), wpack, pk.off[name], name="pack_" + name)
    misc_block = jnp.concatenate(_pack_blocks([w[n] for n in misc_w] + [gain_hi, gain_lo], 0, BF16), axis=0)
    wpack = lax.dynamic_update_slice(wpack, misc_block, (pk.misc, 0))
    gathered = _all_gather(wpack)
    per_chip = [_unpack(gathered[j, pk.misc:pk.misc + misc_block.shape[0]], misc_w_shapes)
                for j in range(N_CHIPS)]
    wb = {name: jnp.concatenate([per_chip[j][i] for j in range(N_CHIPS)], axis=axis_of[name])
          for i, name in enumerate(misc_w)}
    got_hi, got_lo = (lax.bitcast_convert_type(
        jnp.concatenate([per_chip[j][i] for j in range(N_CHIPS)], axis=2), jnp.uint16).astype(jnp.uint32)
        for i in (-2, -1))
    gains_full = lax.bitcast_convert_type((got_hi << 16) | got_lo, F32)
    small = dict(norm_gains=gains_full, mla_q_norm=mla_q_norm, mla_kv_norm=mla_kv_norm,
                 hgrn_lb_logits=hgrn_lb_logits, hgrn_o_norm=hgrn_o_norm)

    gbuf = jnp.zeros((N_CHIPS * pk.rows, PACK_W), BF16)
    sq, grad_x, gbuf, grads = _local_step(x[0], positions[0], loss_target[0],
                                          gathered.reshape(N_CHIPS * pk.rows, PACK_W), gbuf, pk, wb, small)
    d_model = x.shape[-1]
    loss = lax.psum(0.5 * jnp.sum(sq) / d_model, ("x", "y", "c"))

    for j in range(N_CHIPS):
        pieces = [jnp.split(grads[n], N_CHIPS, axis=axis_of[n])[j] if n in axis_of else grads[n] for n in misc_g]
        block = jnp.concatenate(_pack_blocks(pieces, 0, BF16), axis=0)
        gbuf = lax.dynamic_update_slice(gbuf, block, (j * pk.rows + pk.misc, 0))
    gpack = gbuf.reshape(N_CHIPS, 2, pk.rows // 2, PACK_W)
    from_sibling = _exchange_halves(gpack)
    chip_partial = _add_sibling(gpack, from_sibling, jnp.reshape(c, (1,)).astype(jnp.int32))
    from_chips = _scatter_to_owners(chip_partial)
    reduced_half = _sum_chips(from_chips)
    reduced = _share_reduced(reduced_half).reshape(pk.rows, PACK_W)
    g_out = {name: reduced[pk.off[name]:pk.off[name] + _piece_rows(shard_shapes[name])].reshape(
        shard_shapes[name]) for name in pk.BIG}
    red = _unpack(reduced[pk.misc:pk.misc + pk.misc_rows], misc_g_shapes)
    g_out.update({name: red[i] for i, name in enumerate(misc_g)})

    deltas, new_m, new_v = {}, {}, {}
    for name in WEIGHTS:
        deltas[name], new_m[name], new_v[name] = _adamw(w[name], g_out[name], mom_m[name], mom_v[name],
                                                        name="adamw_" + name)
    return (loss, grad_x[None], *[g_out[n] for n in WEIGHTS], *[deltas[n] for n in WEIGHTS],
            *[new_m[n] for n in WEIGHTS], *[new_v[n] for n in WEIGHTS])
```

```python
import functools

import jax
import jax.numpy as jnp
from jax import lax
from jax.experimental import pallas as pl
from jax.experimental.pallas import tpu as pltpu

F32 = jnp.float32
BF16 = jnp.bfloat16
MESH = pl.DeviceIdType.MESH

DEPTH = 4
MLA_HEADS = 8
MLA_NOPE = 128
MLA_ROPE = 64
MLA_V = 128
MLA_QK_PAD = 256
MLA_HEADS_PER_STEP = 2
ROPE_BASE = 10000.0
HGRN_HEADS = 8
HGRN_CHUNK = 32
HGRN_BLOCK = 128
EPS = 1e-6

ADAM_LR = 0.001
ADAM_B1 = 0.9
ADAM_B2 = 0.999
ADAM_EPS = 1e-08
ADAM_WD = 0.01
ADAM_STEP = 10

N_CHIPS = 4
PACK_W = 1024
PACK_ALIGN = 1024
PACK_TILE = 512
V7X_VMEM_LIMIT = 56 * 1024 * 1024

SHARDED = (("norm_gains", 2), ("mla_w_in", 1), ("mla_w_uq", 2), ("mla_w_ukv", 2), ("mla_w_o", 1),
           ("hgrn_w_in", 2), ("hgrn_w_o", 1), ("mlp_w1", 2), ("mlp_w2", 1))
REPLICATED = ("mla_q_norm", "mla_kv_norm", "hgrn_lb_logits", "hgrn_o_norm")
WEIGHTS = ("norm_gains", "mla_w_in", "mla_q_norm", "mla_kv_norm", "mla_w_uq", "mla_w_ukv", "mla_w_o",
           "hgrn_w_in", "hgrn_lb_logits", "hgrn_o_norm", "hgrn_w_o", "mlp_w1", "mlp_w2")


def _cparams(*semantics):
    return pltpu.CompilerParams(dimension_semantics=semantics, vmem_limit_bytes=V7X_VMEM_LIMIT)


def _sigmoid(x):
    return 1.0 / (1.0 + jnp.exp(-x))


def _mm(a, b, *, ta=False, tb=False, out_dtype=F32, tm=1024, tn=1024, tk=1024, epi=None, extra=None,
        name="mm", n=None, b_map=None, into=None, o_map=None):
    if ta:
        K, M = a.shape
    else:
        M, K = a.shape
    if b_map is not None:
        N = n
    elif tb:
        N, Kb = b.shape
    else:
        Kb, N = b.shape
    assert b_map is not None or K == Kb, (a.shape, b.shape, ta, tb)
    tm, tn = min(tm, M), min(tn, N)
    tk = K if (K <= 1024 and b_map is None) else min(tk, K)
    assert M % tm == 0 and N % tn == 0 and K % tk == 0, (M, N, K, tm, tn, tk)
    nk = K // tk
    a_spec = (pl.BlockSpec((tk, tm), lambda i, j, k: (k, i)) if ta
              else pl.BlockSpec((tm, tk), lambda i, j, k: (i, k)))
    if b_map is None:
        b_map = (lambda i, j, k: (j, k)) if tb else (lambda i, j, k: (k, j))
    b_spec = pl.BlockSpec((tn, tk) if tb else (tk, tn), b_map)
    o_spec = pl.BlockSpec((tm, tn), lambda i, j, k: (i, j))
    dims = (((0 if ta else 1,), (1 if tb else 0,)), ((), ()))
    in_specs = [a_spec, b_spec]
    operands = [a, b]
    aliases = {}
    if epi == "mul2r":
        in_specs.append(o_spec)
        operands.append(extra)
    if into is not None:
        assert epi is None
        in_specs.append(pl.BlockSpec(memory_space=pl.ANY))
        operands.append(into)
        aliases = {2: 0}
        out_dtype = into.dtype
        out_shape = jax.ShapeDtypeStruct(into.shape, into.dtype)
        out_specs = pl.BlockSpec((tm, tn), o_map)
    elif epi == "relu2":
        out_shape = (jax.ShapeDtypeStruct((M, N), BF16), jax.ShapeDtypeStruct((M, N), BF16))
        out_specs = (o_spec, o_spec)
    elif epi == "mul2r":
        out_shape = jax.ShapeDtypeStruct((M, N), BF16)
        out_specs = o_spec
    else:
        out_shape = jax.ShapeDtypeStruct((M, N), out_dtype)
        out_specs = o_spec
    n_in = len(operands)

    def body(*refs):
        a_ref, b_ref = refs[0], refs[1]
        outs = refs[n_in:n_in + (2 if epi == "relu2" else 1)]
        k = pl.program_id(2)

        def finish(acc):
            if epi == "relu2":
                r = jnp.maximum(acc, 0.0)
                outs[0][...] = (r * r).astype(BF16)
                outs[1][...] = r.astype(BF16)
            elif epi == "mul2r":
                outs[0][...] = (acc * (2.0 * refs[2][...].astype(F32))).astype(BF16)
            else:
                outs[0][...] = acc.astype(out_dtype)

        part = lax.dot_general(a_ref[...], b_ref[...], dims, preferred_element_type=F32)
        if nk == 1:
            finish(part)
            return
        acc_ref = refs[-1]

        @pl.when(k == 0)
        def _():
            acc_ref[...] = part

        @pl.when((k > 0) & (k < nk - 1))
        def _():
            acc_ref[...] += part

        @pl.when(k == nk - 1)
        def _():
            finish(acc_ref[...] + part)

    return pl.pallas_call(
        body, name=name, grid=(M // tm, N // tn, nk), in_specs=in_specs, out_specs=out_specs,
        out_shape=out_shape, scratch_shapes=[pltpu.VMEM((tm, tn), F32)] if nk > 1 else [],
        input_output_aliases=aliases,
        compiler_params=_cparams("parallel", "parallel", "arbitrary"))(*operands)


def _rms_rstd(x):
    return lax.rsqrt(jnp.mean(x * x, axis=-1, keepdims=True) + EPS)


def _rms_bwd_tile(x, g, dy):
    r = _rms_rstd(x)
    xh = x * r
    u = dy * g
    dx = r * (u - xh * jnp.mean(u * xh, axis=-1, keepdims=True))
    dg = jnp.sum(dy * xh, axis=0, keepdims=True)
    return dx, dg


def _row_tile(T):
    return min(256, T)


def _prenorm_fwd(x, g, name="prenorm_fwd"):
    T, D = x.shape
    tm = _row_tile(T)

    def body(x_ref, g_ref, a_ref):
        xv = x_ref[...]
        a_ref[...] = (xv * _rms_rstd(xv) * g_ref[...]).astype(BF16)

    row = pl.BlockSpec((tm, D), lambda i: (i, 0))
    vec = pl.BlockSpec((1, D), lambda i: (0, 0))
    return pl.pallas_call(body, name=name, grid=(T // tm,), in_specs=[row, vec], out_specs=row,
                          out_shape=jax.ShapeDtypeStruct((T, D), BF16),
                          compiler_params=_cparams("parallel"))(x, g)


def _resnorm_fwd(h, z, g_post, g_pre, name="resnorm_fwd"):
    T, D = h.shape
    tm = _row_tile(T)

    def body(h_ref, z_ref, gp_ref, gn_ref, hn_ref, a_ref):
        zv = z_ref[...]
        hn = h_ref[...] + zv * _rms_rstd(zv) * gp_ref[...]
        hn_ref[...] = hn
        a_ref[...] = (hn * _rms_rstd(hn) * gn_ref[...]).astype(BF16)

    row = pl.BlockSpec((tm, D), lambda i: (i, 0))
    vec = pl.BlockSpec((1, D), lambda i: (0, 0))
    return pl.pallas_call(body, name=name, grid=(T // tm,), in_specs=[row, row, vec, vec],
                          out_specs=(row, row),
                          out_shape=(jax.ShapeDtypeStruct((T, D), F32), jax.ShapeDtypeStruct((T, D), BF16)),
                          compiler_params=_cparams("parallel"))(h, z, g_post, g_pre)


def _resnorm_loss(h, z, g_post, target, name="resnorm_loss"):
    T, D = h.shape
    tm = _row_tile(T)

    def body(h_ref, z_ref, gp_ref, t_ref, dy_ref, sq_ref):
        zv = z_ref[...]
        err = h_ref[...] + zv * _rms_rstd(zv) * gp_ref[...] - t_ref[...]
        dy_ref[...] = err * (1.0 / D)

        @pl.when(pl.program_id(0) == 0)
        def _():
            sq_ref[...] = jnp.zeros_like(sq_ref)

        sq_ref[...] += jnp.sum(err * err, axis=0, keepdims=True)

    row = pl.BlockSpec((tm, D), lambda i: (i, 0))
    vec = pl.BlockSpec((1, D), lambda i: (0, 0))
    return pl.pallas_call(body, name=name, grid=(T // tm,), in_specs=[row, row, vec, row],
                          out_specs=(row, vec),
                          out_shape=(jax.ShapeDtypeStruct((T, D), F32), jax.ShapeDtypeStruct((1, D), F32)),
                          compiler_params=_cparams("arbitrary"))(h, z, g_post, target)


def _resnorm_bwd(z, g_post, dh, h_new=None, da=None, g_pre=None, name="resnorm_bwd"):
    T, D = z.shape
    tm = _row_tile(T)
    has_next = h_new is not None
    row = pl.BlockSpec((tm, D), lambda i: (i, 0))
    vec = pl.BlockSpec((1, D), lambda i: (0, 0))

    if has_next:
        def body(z_ref, gp_ref, dh_ref, hn_ref, da_ref, gn_ref, t_ref, dz_ref, dgp_ref, dgn_ref):
            first = pl.program_id(0) == 0

            @pl.when(first)
            def _():
                dgp_ref[...] = jnp.zeros_like(dgp_ref)
                dgn_ref[...] = jnp.zeros_like(dgn_ref)

            dpre, dgn = _rms_bwd_tile(hn_ref[...], gn_ref[...], da_ref[...])
            t = dh_ref[...] + dpre
            t_ref[...] = t
            dz, dgp = _rms_bwd_tile(z_ref[...], gp_ref[...], t)
            dz_ref[...] = dz.astype(BF16)
            dgp_ref[...] += dgp
            dgn_ref[...] += dgn

        return pl.pallas_call(
            body, name=name, grid=(T // tm,), in_specs=[row, vec, row, row, row, vec],
            out_specs=(row, row, vec, vec),
            out_shape=(jax.ShapeDtypeStruct((T, D), F32), jax.ShapeDtypeStruct((T, D), BF16),
                       jax.ShapeDtypeStruct((1, D), F32), jax.ShapeDtypeStruct((1, D), F32)),
            compiler_params=_cparams("arbitrary"))(z, g_post, dh, h_new, da, g_pre)

    def body_last(z_ref, gp_ref, dh_ref, dz_ref, dgp_ref):
        @pl.when(pl.program_id(0) == 0)
        def _():
            dgp_ref[...] = jnp.zeros_like(dgp_ref)

        dz, dgp = _rms_bwd_tile(z_ref[...], gp_ref[...], dh_ref[...])
        dz_ref[...] = dz.astype(BF16)
        dgp_ref[...] += dgp

    return pl.pallas_call(
        body_last, name=name, grid=(T // tm,), in_specs=[row, vec, row], out_specs=(row, vec),
        out_shape=(jax.ShapeDtypeStruct((T, D), BF16), jax.ShapeDtypeStruct((1, D), F32)),
        compiler_params=_cparams("arbitrary"))(z, g_post, dh)


def _prenorm_bwd(x, g, dh, da, name="prenorm_bwd"):
    T, D = x.shape
    tm = _row_tile(T)

    def body(x_ref, g_ref, dh_ref, da_ref, dx_ref, dg_ref):
        @pl.when(pl.program_id(0) == 0)
        def _():
            dg_ref[...] = jnp.zeros_like(dg_ref)

        dpre, dg = _rms_bwd_tile(x_ref[...], g_ref[...], da_ref[...])
        dx_ref[...] = dh_ref[...] + dpre
        dg_ref[...] += dg

    row = pl.BlockSpec((tm, D), lambda i: (i, 0))
    vec = pl.BlockSpec((1, D), lambda i: (0, 0))
    return pl.pallas_call(
        body, name=name, grid=(T // tm,), in_specs=[row, vec, row, row], out_specs=(row, vec),
        out_shape=(jax.ShapeDtypeStruct((T, D), F32), jax.ShapeDtypeStruct((1, D), F32)),
        compiler_params=_cparams("arbitrary"))(x, g, dh, da)


class _Packed:
    def __init__(self, big, misc_rows):
        self.big = tuple(big)
        self.off = {}
        r = 0
        for name, rows in big:
            self.off[name] = r
            r += rows
        self.misc, self.misc_rows = r, misc_rows
        self.rows = -(-(r + misc_rows) // PACK_ALIGN) * PACK_ALIGN

    def block(self, name, layer, unit):
        r = self.off[name]
        assert r % unit == 0 and self.rows % unit == 0
        return r // unit, self.rows // unit


def _col_sharded(pk, name, layer, unit):
    base, stride = pk.block(name, layer, unit)
    return (lambda i, j, k: (j * stride + base, 0)), (lambda i, j, k: (k * stride + base, 0))


def _row_sharded(pk, name, layer, unit):
    base, stride = pk.block(name, layer, unit)
    return ((lambda i, j, k: (k * stride + base, 0)), (lambda i, j, k: (j * stride + base, 0)),
            (lambda i, j, k: (i * stride + base, 0)))


def _mlp_fwd(a, wbuf, pk, layer):
    D = a.shape[1]
    by_n, _ = _col_sharded(pk, "mlp_w1", layer, D)
    by_k, _, _ = _row_sharded(pk, "mlp_w2", layer, D)
    act, r = _mm(a, wbuf, n=4 * D, b_map=by_n, tk=D, tn=D, epi="relu2", name="mlp_up")
    u = _mm(act, wbuf, n=D, b_map=by_k, tk=D, tn=D, name="mlp_down")
    return u, (a, act, r)


def _mlp_bwd(du, saved, wbuf, gbuf, pk, layer):
    a, act, r = saved
    D = a.shape[1]
    w1_by_n, w1_by_k = _col_sharded(pk, "mlp_w1", layer, D)
    _, w2_by_n, w2_by_m = _row_sharded(pk, "mlp_w2", layer, D)
    dz1 = _mm(du, wbuf, tb=True, n=4 * D, b_map=w2_by_n, tn=D, tk=D, epi="mul2r", extra=r, name="mlp_down_dx")
    gbuf = _mm(act, du, ta=True, into=gbuf, o_map=w2_by_m, tm=D, tn=D, name="mlp_down_dw")
    gbuf = _mm(a, dz1, ta=True, into=gbuf, o_map=w1_by_n, tm=D, tn=D, name="mlp_up_dw")
    da = _mm(dz1, wbuf, tb=True, n=D, b_map=w1_by_k, tn=D, tk=D, name="mlp_up_dx")
    return da, gbuf


def _rope_swap(t):
    n = t.shape[-1]
    lane = lax.broadcasted_iota(jnp.int32, t.shape, t.ndim - 1)
    half = MLA_ROPE // 2
    first = (lane & (MLA_ROPE - 1)) < half
    return jnp.where(first, pltpu.roll(t, n - half, t.ndim - 1), pltpu.roll(t, half, t.ndim - 1))


def _mla_mid_fwd(proj, q_norm, kv_norm, w_uq, w_ukv, cc, ss):
    T, PW = proj.shape
    QL, KVL = q_norm.shape[-1], kv_norm.shape[-1]
    H = MLA_HEADS
    assert PW == QL + KVL + 128
    tm = _row_tile(T)

    def body(p_ref, qn_ref, kn_ref, wq_ref, wkv_ref, cc_ref, ss_ref,
             cq_ref, ckv_ref, q_ref, k_ref, v_ref):
        cq = p_ref[:, 0:QL]
        ckv = p_ref[:, QL:QL + KVL]
        kr = p_ref[:, QL + KVL:QL + KVL + 128]
        c, s = cc_ref[...], ss_ref[...]
        cqn = (cq * _rms_rstd(cq) * qn_ref[...]).astype(BF16)
        ckvn = (ckv * _rms_rstd(ckv) * kn_ref[...]).astype(BF16)
        cq_ref[...] = cqn
        ckv_ref[...] = ckvn
        q = jnp.dot(cqn, wq_ref[...], preferred_element_type=F32)
        kv = jnp.dot(ckvn, wkv_ref[...], preferred_element_type=F32)
        krf = (kr * c + _rope_swap(kr) * s).astype(BF16)
        for h in range(H):
            o = h * MLA_QK_PAD
            q_ref[:, o:o + MLA_NOPE] = q[:, o:o + MLA_NOPE].astype(BF16)
            qr = q[:, o + MLA_NOPE:o + MLA_QK_PAD]
            q_ref[:, o + MLA_NOPE:o + MLA_QK_PAD] = (qr * c + _rope_swap(qr) * s).astype(BF16)
            k_ref[:, o:o + MLA_NOPE] = kv[:, o:o + MLA_NOPE].astype(BF16)
            k_ref[:, o + MLA_NOPE:o + MLA_QK_PAD] = krf
            v_ref[:, h * MLA_V:(h + 1) * MLA_V] = kv[:, o + MLA_NOPE:o + MLA_QK_PAD].astype(BF16)

    def row(w):
        return pl.BlockSpec((tm, w), lambda i: (i, 0))

    def full(shape):
        return pl.BlockSpec(shape, lambda i: (0, 0))

    return pl.pallas_call(
        body, name="mla_mid_fwd", grid=(T // tm,),
        in_specs=[row(PW), full((1, QL)), full((1, KVL)), full(w_uq.shape), full(w_ukv.shape),
                  row(128), row(128)],
        out_specs=(row(QL), row(KVL), row(H * MLA_QK_PAD), row(H * MLA_QK_PAD), row(H * MLA_V)),
        out_shape=(jax.ShapeDtypeStruct((T, QL), BF16), jax.ShapeDtypeStruct((T, KVL), BF16),
                   jax.ShapeDtypeStruct((T, H * MLA_QK_PAD), BF16),
                   jax.ShapeDtypeStruct((T, H * MLA_QK_PAD), BF16),
                   jax.ShapeDtypeStruct((T, H * MLA_V), BF16)),
        compiler_params=_cparams("parallel"))(proj, q_norm, kv_norm, w_uq, w_ukv, cc, ss)


def _mla_mid_bwd(proj, q_norm, kv_norm, w_uq, w_ukv, cc, ss, dq, dk, dv):
    T, PW = proj.shape
    QL, KVL = q_norm.shape[-1], kv_norm.shape[-1]
    H = MLA_HEADS
    tm = _row_tile(T)
    nt = (((1,), (1,)), ((), ()))

    def body(p_ref, qn_ref, kn_ref, wq_ref, wkv_ref, cc_ref, ss_ref, dq_ref, dk_ref, dv_ref,
             dqp_ref, dkv_ref, dp_ref, dqn_ref, dkn_ref):
        @pl.when(pl.program_id(0) == 0)
        def _():
            dqn_ref[...] = jnp.zeros_like(dqn_ref)
            dkn_ref[...] = jnp.zeros_like(dkn_ref)

        c, s = cc_ref[...], ss_ref[...]
        dkr = jnp.zeros((tm, 128), F32)
        for h in range(H):
            o = h * MLA_QK_PAD
            dqp_ref[:, o:o + MLA_NOPE] = dq_ref[:, o:o + MLA_NOPE].astype(BF16)
            dqr = dq_ref[:, o + MLA_NOPE:o + MLA_QK_PAD]
            dqp_ref[:, o + MLA_NOPE:o + MLA_QK_PAD] = (dqr * c + _rope_swap(dqr * s)).astype(BF16)
            dkv_ref[:, o:o + MLA_NOPE] = dk_ref[:, o:o + MLA_NOPE].astype(BF16)
            dkv_ref[:, o + MLA_NOPE:o + MLA_QK_PAD] = dv_ref[:, h * MLA_V:(h + 1) * MLA_V].astype(BF16)
            dkr = dkr + dk_ref[:, o + MLA_NOPE:o + MLA_QK_PAD]
        dcqn = lax.dot_general(dqp_ref[...], wq_ref[...], nt, preferred_element_type=F32)
        dckvn = lax.dot_general(dkv_ref[...], wkv_ref[...], nt, preferred_element_type=F32)
        dcq, dqn = _rms_bwd_tile(p_ref[:, 0:QL], qn_ref[...], dcqn)
        dckv, dkn = _rms_bwd_tile(p_ref[:, QL:QL + KVL], kn_ref[...], dckvn)
        dp_ref[:, 0:QL] = dcq.astype(BF16)
        dp_ref[:, QL:QL + KVL] = dckv.astype(BF16)
        dp_ref[:, QL + KVL:QL + KVL + 128] = (dkr * c + _rope_swap(dkr * s)).astype(BF16)
        dqn_ref[...] += dqn
        dkn_ref[...] += dkn

    def row(w):
        return pl.BlockSpec((tm, w), lambda i: (i, 0))

    def full(shape):
        return pl.BlockSpec(shape, lambda i: (0, 0))

    return pl.pallas_call(
        body, name="mla_mid_bwd", grid=(T // tm,),
        in_specs=[row(PW), full((1, QL)), full((1, KVL)), full(w_uq.shape), full(w_ukv.shape),
                  row(128), row(128), row(H * MLA_QK_PAD), row(H * MLA_QK_PAD), row(H * MLA_V)],
        out_specs=(row(H * MLA_QK_PAD), row(H * MLA_QK_PAD), row(PW), full((1, QL)), full((1, KVL))),
        out_shape=(jax.ShapeDtypeStruct((T, H * MLA_QK_PAD), BF16),
                   jax.ShapeDtypeStruct((T, H * MLA_QK_PAD), BF16),
                   jax.ShapeDtypeStruct((T, PW), BF16),
                   jax.ShapeDtypeStruct((1, QL), F32), jax.ShapeDtypeStruct((1, KVL), F32)),
        compiler_params=_cparams("arbitrary"))(proj, q_norm, kv_norm, w_uq, w_ukv, cc, ss, dq, dk, dv)


def _attn_tile(T):
    return min(1024, T)


def _attn_pairs(n, by_key):
    if by_key:
        pairs = [(qi, ki) for ki in range(n) for qi in range(ki, n)]
    else:
        pairs = [(qi, ki) for qi in range(n) for ki in range(qi + 1)]
    return (jnp.asarray([p[0] for p in pairs], jnp.int32), jnp.asarray([p[1] for p in pairs], jnp.int32))


def _scores(q, k, scale, diagonal):
    s = lax.dot_general(q, k, (((1,), (1,)), ((), ())), preferred_element_type=F32) * scale
    if diagonal:
        rows = lax.broadcasted_iota(jnp.int32, s.shape, 0)
        cols = lax.broadcasted_iota(jnp.int32, s.shape, 1)
        s = jnp.where(rows >= cols, s, -jnp.inf)
    return s


def _attn_fwd(q, k, v):
    T = q.shape[0]
    H, DQ, DV = MLA_HEADS, MLA_QK_PAD, MLA_V
    tq = _attn_tile(T)
    nq = T // tq
    scale = float(MLA_NOPE + MLA_ROPE) ** -0.5
    G = MLA_HEADS_PER_STEP
    qi_tab, ki_tab = _attn_pairs(nq, by_key=False)

    def body(qi_ref, ki_ref, q_ref, k_ref, v_ref, o_ref, lse_ref, *scratch):
        m_refs, l_refs, acc_refs = scratch[0:G], scratch[G:2 * G], scratch[2 * G:3 * G]
        p = pl.program_id(1)
        qi, ki = qi_ref[p], ki_ref[p]

        @pl.when(ki == 0)
        def _():
            for g in range(G):
                m_refs[g][...] = jnp.full_like(m_refs[g], -jnp.inf)
                l_refs[g][...] = jnp.zeros_like(l_refs[g])
                acc_refs[g][...] = jnp.zeros_like(acc_refs[g])

        def update(diagonal):
            for g in range(G):
                qs, vs = slice(g * DQ, (g + 1) * DQ), slice(g * DV, (g + 1) * DV)
                s = _scores(q_ref[:, qs], k_ref[:, qs], scale, diagonal)
                m_prev = m_refs[g][...]
                m_new = jnp.maximum(m_prev, jnp.max(s, axis=1, keepdims=True))
                alpha = jnp.exp(m_prev - m_new)
                pr = jnp.exp(s - m_new)
                l_refs[g][...] = alpha * l_refs[g][...] + jnp.sum(pr, axis=1, keepdims=True)
                acc_refs[g][...] = alpha * acc_refs[g][...] + jnp.dot(pr.astype(BF16), v_ref[:, vs],
                                                                      preferred_element_type=F32)
                m_refs[g][...] = m_new

        @pl.when(ki < qi)
        def _():
            update(False)

        @pl.when(ki == qi)
        def _():
            update(True)
            for g in range(G):
                vs = slice(g * DV, (g + 1) * DV)
                o_ref[:, vs] = (acc_refs[g][...] / l_refs[g][...]).astype(BF16)
                lse_ref[g] = m_refs[g][...] + jnp.log(l_refs[g][...])

    return pl.pallas_call(
        body, name="attn_fwd",
        grid_spec=pltpu.PrefetchScalarGridSpec(
            num_scalar_prefetch=2, grid=(H // G, int(qi_tab.shape[0])),
            in_specs=[pl.BlockSpec((tq, G * DQ), lambda h, p, qt, kt: (qt[p], h)),
                      pl.BlockSpec((tq, G * DQ), lambda h, p, qt, kt: (kt[p], h)),
                      pl.BlockSpec((tq, G * DV), lambda h, p, qt, kt: (kt[p], h))],
            out_specs=(pl.BlockSpec((tq, G * DV), lambda h, p, qt, kt: (qt[p], h)),
                       pl.BlockSpec((G, tq, 1), lambda h, p, qt, kt: (h, qt[p], 0))),
            scratch_shapes=([pltpu.VMEM((tq, 1), F32)] * (2 * G) + [pltpu.VMEM((tq, DV), F32)] * G)),
        out_shape=(jax.ShapeDtypeStruct((T, H * DV), BF16), jax.ShapeDtypeStruct((H, T, 1), F32)),
        compiler_params=_cparams("parallel", "arbitrary"))(qi_tab, ki_tab, q, k, v)


def _attn_bwd(q, k, v, o, do, lse):
    T = q.shape[0]
    H, DQ, DV = MLA_HEADS, MLA_QK_PAD, MLA_V
    tq = _attn_tile(T)
    nq = T // tq
    scale = float(MLA_NOPE + MLA_ROPE) ** -0.5
    tn = (((0,), (0,)), ((), ()))
    nt = (((1,), (1,)), ((), ()))
    G = MLA_HEADS_PER_STEP
    qi_tab, ki_tab = _attn_pairs(nq, by_key=True)

    def body(qi_ref, ki_ref, q_ref, k_ref, v_ref, o_ref, do_ref, lse_ref, dq_ref, dk_ref, dv_ref,
             dk_acc, dv_acc):
        p = pl.program_id(1)
        qi, ki = qi_ref[p], ki_ref[p]

        @pl.when(p == 0)
        def _():
            dq_ref[...] = jnp.zeros_like(dq_ref)

        @pl.when(qi == ki)
        def _():
            dk_acc[...] = jnp.zeros_like(dk_acc)
            dv_acc[...] = jnp.zeros_like(dv_acc)

        def step(diagonal):
            rows = pl.ds(pl.multiple_of(qi * tq, tq), tq)
            for g in range(G):
                qs, vs = slice(g * DQ, (g + 1) * DQ), slice(g * DV, (g + 1) * DV)
                dof = do_ref[:, vs]
                delta = jnp.sum(dof.astype(F32) * o_ref[:, vs].astype(F32), axis=1, keepdims=True)
                s = _scores(q_ref[:, qs], k_ref[:, qs], scale, diagonal)
                pr = jnp.exp(s - lse_ref[g])
                dp = lax.dot_general(dof, v_ref[:, vs], nt, preferred_element_type=F32)
                ds = (pr * (dp - delta) * scale).astype(BF16)
                dv_acc[:, vs] += lax.dot_general(pr.astype(BF16), dof, tn, preferred_element_type=F32)
                dk_acc[:, qs] += lax.dot_general(ds, q_ref[:, qs], tn, preferred_element_type=F32)
                dq_ref[rows, qs] += jnp.dot(ds, k_ref[:, qs], preferred_element_type=F32)

        @pl.when(qi == ki)
        def _():
            step(True)

        @pl.when(qi > ki)
        def _():
            step(False)

        @pl.when(qi == nq - 1)
        def _():
            dk_ref[...] = dk_acc[...]
            dv_ref[...] = dv_acc[...]

    qspec = pl.BlockSpec((tq, G * DQ), lambda h, p, qt, kt: (qt[p], h))
    ospec = pl.BlockSpec((tq, G * DV), lambda h, p, qt, kt: (qt[p], h))
    kspec = pl.BlockSpec((tq, G * DQ), lambda h, p, qt, kt: (kt[p], h))
    vspec = pl.BlockSpec((tq, G * DV), lambda h, p, qt, kt: (kt[p], h))
    return pl.pallas_call(
        body, name="attn_bwd",
        grid_spec=pltpu.PrefetchScalarGridSpec(
            num_scalar_prefetch=2, grid=(H // G, int(qi_tab.shape[0])),
            in_specs=[qspec, kspec, vspec, ospec, ospec,
                      pl.BlockSpec((G, tq, 1), lambda h, p, qt, kt: (h, qt[p], 0))],
            out_specs=(pl.BlockSpec((T, G * DQ), lambda h, p, qt, kt: (0, h)), kspec, vspec),
            scratch_shapes=[pltpu.VMEM((tq, G * DQ), F32), pltpu.VMEM((tq, G * DV), F32)]),
        out_shape=(jax.ShapeDtypeStruct((T, H * DQ), F32), jax.ShapeDtypeStruct((T, H * DQ), F32),
                   jax.ShapeDtypeStruct((T, H * DV), F32)),
        compiler_params=_cparams("parallel", "arbitrary"))(qi_tab, ki_tab, q, k, v, o, do, lse)


def _mla_fwd(a, w, cc, ss, wbuf, pk, slot):
    D = a.shape[1]
    by_k, _, _ = _row_sharded(pk, "mla_w_o", slot, D // N_CHIPS)
    proj = _mm(a, w["w_in"], name="mla_in")
    cqn, ckvn, q, k, v = _mla_mid_fwd(proj, w["q_norm"], w["kv_norm"], w["w_uq"], w["w_ukv"], cc, ss)
    o, lse = _attn_fwd(q, k, v)
    m = _mm(o, wbuf, n=D, b_map=by_k, tm=2048, tk=D // N_CHIPS, tn=D, name="mla_out")
    return m, (a, proj, cqn, ckvn, q, k, v, o, lse)


def _mla_bwd(dm, saved, w, cc, ss, wbuf, gbuf, pk, slot):
    a, proj, cqn, ckvn, q, k, v, o, lse = saved
    D = a.shape[1]
    _, by_n, by_m = _row_sharded(pk, "mla_w_o", slot, D // N_CHIPS)
    do = _mm(dm, wbuf, tb=True, n=o.shape[1], b_map=by_n, tm=2048, tn=D // N_CHIPS, tk=D, out_dtype=BF16,
             name="mla_out_dx")
    gbuf = _mm(o, dm, ta=True, into=gbuf, o_map=by_m, tm=D // N_CHIPS, tn=D, tk=2048, name="mla_out_dw")
    dq, dk, dv = _attn_bwd(q, k, v, o, do, lse)
    dqp, dkv, dproj, dqn, dkn = _mla_mid_bwd(proj, w["q_norm"], w["kv_norm"], w["w_uq"], w["w_ukv"],
                                             cc, ss, dq, dk, dv)
    dw_uq = _mm(cqn, dqp, ta=True, out_dtype=BF16, name="mla_uq_dw")
    dw_ukv = _mm(ckvn, dkv, ta=True, out_dtype=BF16, name="mla_ukv_dw")
    dw_in = _mm(a, dproj, ta=True, out_dtype=BF16, name="mla_in_dw")
    da = _mm(dproj, w["w_in"], tb=True, name="mla_in_dx")
    return da, gbuf, dict(w_in=dw_in, w_uq=dw_uq, w_ukv=dw_ukv, q_norm=dqn, kv_norm=dkn)


def _split_dot(mat, x, parts):
    acc = None
    rem = x
    for _ in range(parts):
        piece = rem.astype(BF16)
        term = jnp.dot(mat, piece, preferred_element_type=F32)
        acc = term if acc is None else acc + term
        rem = rem - piece.astype(F32)
    return acc


def _chunk_mats(tb):
    C = HGRN_CHUNK
    assert C & (C - 1) == 0
    r = lax.broadcasted_iota(jnp.int32, (tb, tb), 0)
    s = lax.broadcasted_iota(jnp.int32, (tb, tb), 1)
    start = r & ~(C - 1)
    same = start == (s & ~(C - 1))
    ref = start + C // 2
    last = start + C - 1
    one, zero = jnp.float32(1.0), jnp.float32(0.0)
    cum = jnp.where(same & (s <= r), one, zero)
    rel = cum - jnp.where(same & (s <= ref), one, zero)
    rest = jnp.where(same & (s > r) & (s <= last), one, zero)
    rev = jnp.where(same & (s >= r), one, zero)
    ones = jnp.where(same, one, zero)
    causal = same & (s <= r)
    return cum, rel, rest, rev, ones, causal


def _hgrn_gates(p_ref, lb, HK):
    qx = p_ref[:, 0:HK]
    fx = p_ref[:, HK:2 * HK]
    sf = _sigmoid(fx)
    f = lb + (1.0 - lb) * sf
    sq = _sigmoid(qx)
    return qx, sq, qx * sq, sf, f, 1.0 - f, jnp.log(f)


def _hgrn_fwd(proj, lb, o_norm):
    T = proj.shape[0]
    H, C = HGRN_HEADS, HGRN_CHUNK
    HK = proj.shape[1] // 4
    DK = HK // H
    tb = min(HGRN_BLOCK, T)
    ncb = tb // C
    nt = (((1,), (1,)), ((), ()))
    tn = (((0,), (0,)), ((), ()))

    def body(p_ref, lb_ref, on_ref, y_ref, o_ref, st_ref, state, oacc):
        @pl.when(pl.program_id(0) == 0)
        def _():
            state[...] = jnp.zeros_like(state)

        cum, rel, rest, _, _, causal = _chunk_mats(tb)
        _, _, q, _, f, k, logf = _hgrn_gates(p_ref, lb_ref[...], HK)
        b = _split_dot(cum.astype(BF16), logf, 3)
        brel = _split_dot(rel.astype(BF16), logf, 3)
        brest = _split_dot(rest.astype(BF16), logf, 3)
        eb = jnp.exp(b)
        q_rel = (q * jnp.exp(brel)).astype(BF16)
        k_rel = (k * jnp.exp(-brel)).astype(BF16)
        q_dec = (q * eb).astype(BF16)
        k_dec = (k * jnp.exp(brest)).astype(BF16)
        v = p_ref[:, 2 * HK:3 * HK].astype(BF16)
        for h in range(H):
            hs = slice(h * DK, (h + 1) * DK)
            a = lax.dot_general(q_rel[:, hs], k_rel[:, hs], nt, preferred_element_type=F32)
            a = jnp.where(causal, a, 0.0).astype(BF16)
            oacc[:, hs] = jnp.dot(a, v[:, hs], preferred_element_type=F32)
            for j in range(ncb):
                rs = slice(j * C, (j + 1) * C)
                st = state[h]
                st_ref[j, h] = st
                oacc[rs, hs] += lax.dot_general(q_dec[rs, hs], st.astype(BF16), nt,
                                                preferred_element_type=F32)
                dec = jnp.exp(jnp.sum(logf[rs, hs], axis=0, keepdims=True))
                state[h] = dec * st + lax.dot_general(v[rs, hs], k_dec[rs, hs], tn,
                                                      preferred_element_type=F32)
        o = oacc[...]
        o_ref[...] = o
        gx = p_ref[:, 3 * HK:4 * HK]
        gate = gx * _sigmoid(gx)
        for h in range(H):
            hs = slice(h * DK, (h + 1) * DK)
            oh = o[:, hs]
            y_ref[:, hs] = (oh * _rms_rstd(oh) * on_ref[...] * gate[:, hs]).astype(BF16)

    return pl.pallas_call(
        body, name="hgrn_fwd", grid=(T // tb,),
        in_specs=[pl.BlockSpec((tb, 4 * HK), lambda i: (i, 0)),
                  pl.BlockSpec((1, HK), lambda i: (0, 0)),
                  pl.BlockSpec((1, DK), lambda i: (0, 0))],
        out_specs=(pl.BlockSpec((tb, HK), lambda i: (i, 0)),
                   pl.BlockSpec((tb, HK), lambda i: (i, 0)),
                   pl.BlockSpec((ncb, H, DK, DK), lambda i: (i, 0, 0, 0))),
        out_shape=(jax.ShapeDtypeStruct((T, HK), BF16), jax.ShapeDtypeStruct((T, HK), F32),
                   jax.ShapeDtypeStruct((T // C, H, DK, DK), F32)),
        scratch_shapes=[pltpu.VMEM((H, DK, DK), F32), pltpu.VMEM((tb, HK), F32)],
        compiler_params=_cparams("arbitrary"))(proj, lb, o_norm)


def _hgrn_bwd(proj, lb, o_norm, o, states, dy):
    T = proj.shape[0]
    H, C = HGRN_HEADS, HGRN_CHUNK
    HK = proj.shape[1] // 4
    DK = HK // H
    tb = min(HGRN_BLOCK, T)
    ncb = tb // C
    nb = T // tb
    nt = (((1,), (1,)), ((), ()))
    tn = (((0,), (0,)), ((), ()))

    def body(p_ref, lb_ref, on_ref, o_ref, st_ref, dy_ref, dp_ref, dlb_ref, don_ref,
             dstate, dqr_s, dkr_s, dqd_s, dkd_s, dv_s, do_s, e_s):
        @pl.when(pl.program_id(0) == 0)
        def _():
            dstate[...] = jnp.zeros_like(dstate)
            dlb_ref[...] = jnp.zeros_like(dlb_ref)
            don_ref[...] = jnp.zeros_like(don_ref)

        cum, rel, rest, rev, ones, causal = _chunk_mats(tb)
        lb = lb_ref[...]
        qx, sq, q, sf, f, k, logf = _hgrn_gates(p_ref, lb, HK)
        b = _split_dot(cum.astype(BF16), logf, 3)
        brel = _split_dot(rel.astype(BF16), logf, 3)
        brest = _split_dot(rest.astype(BF16), logf, 3)
        eb = jnp.exp(b)
        erel = jnp.exp(brel)
        enrel = jnp.exp(-brel)
        erest = jnp.exp(brest)
        q_rel_f, k_rel_f, q_dec_f, k_dec_f = q * erel, k * enrel, q * eb, k * erest
        q_rel, k_rel = q_rel_f.astype(BF16), k_rel_f.astype(BF16)
        q_dec, k_dec = q_dec_f.astype(BF16), k_dec_f.astype(BF16)
        v = p_ref[:, 2 * HK:3 * HK].astype(BF16)

        gx = p_ref[:, 3 * HK:4 * HK]
        sg = _sigmoid(gx)
        gate = gx * sg
        dy = dy_ref[...]
        ov = o_ref[...]
        on = on_ref[...]
        don = jnp.zeros((1, DK), F32)
        for h in range(H):
            hs = slice(h * DK, (h + 1) * DK)
            oh = ov[:, hs]
            r = _rms_rstd(oh)
            xh = oh * r
            d_on = dy[:, hs] * gate[:, hs]
            don = don + jnp.sum(d_on * xh, axis=0, keepdims=True)
            u = d_on * on
            do_s[:, hs] = r * (u - xh * jnp.mean(u * xh, axis=-1, keepdims=True))
            dp_ref[:, 3 * HK + h * DK:3 * HK + (h + 1) * DK] = (
                dy[:, hs] * xh * on * (sg[:, hs] * (1.0 + gx[:, hs] * (1.0 - sg[:, hs])))).astype(BF16)
        don_ref[...] += don

        for h in range(H):
            hs = slice(h * DK, (h + 1) * DK)
            doh = do_s[:, hs].astype(BF16)
            a = lax.dot_general(q_rel[:, hs], k_rel[:, hs], nt, preferred_element_type=F32)
            a = jnp.where(causal, a, 0.0).astype(BF16)
            da = lax.dot_general(doh, v[:, hs], nt, preferred_element_type=F32)
            da = jnp.where(causal, da, 0.0).astype(BF16)
            dv_s[:, hs] = lax.dot_general(a, doh, tn, preferred_element_type=F32)
            dqr_s[:, hs] = jnp.dot(da, k_rel[:, hs], preferred_element_type=F32)
            dkr_s[:, hs] = lax.dot_general(da, q_rel[:, hs], tn, preferred_element_type=F32)
            for j in reversed(range(ncb)):
                rs = slice(j * C, (j + 1) * C)
                dst = dstate[h]
                dstb = dst.astype(BF16)
                st = st_ref[j, h]
                dkd_s[rs, hs] = jnp.dot(v[rs, hs], dstb, preferred_element_type=F32)
                dv_s[rs, hs] += lax.dot_general(k_dec[rs, hs], dstb, nt, preferred_element_type=F32)
                dec = jnp.exp(jnp.sum(logf[rs, hs], axis=0, keepdims=True))
                e_s[rs, hs] = jnp.broadcast_to(jnp.sum(dst * st, axis=0, keepdims=True) * dec, (C, DK))
                dqd_s[rs, hs] = jnp.dot(doh[rs], st.astype(BF16), preferred_element_type=F32)
                dstate[h] = dec * dst + lax.dot_general(doh[rs], q_dec[rs, hs], tn,
                                                        preferred_element_type=F32)

        dqr, dkr, dqd, dkd = dqr_s[...], dkr_s[...], dqd_s[...], dkd_s[...]
        kdk = dkd * k_dec_f
        db = dqr * q_rel_f - dkr * k_rel_f + dqd * q_dec_f - kdk
        dlogf = _split_dot(rev.astype(BF16), db, 2) + _split_dot(ones.astype(BF16), kdk, 2) + e_s[...]
        dk = dkr * enrel + dkd * erest
        df = dlogf / f - dk
        dlb_ref[...] += jnp.sum(df * (1.0 - sf), axis=0, keepdims=True)
        dq = dqr * erel + dqd * eb
        dp_ref[:, 0:HK] = (dq * (sq * (1.0 + qx * (1.0 - sq)))).astype(BF16)
        dp_ref[:, HK:2 * HK] = (df * (1.0 - lb) * sf * (1.0 - sf)).astype(BF16)
        dp_ref[:, 2 * HK:3 * HK] = dv_s[...].astype(BF16)

    rev_row = lambda w: pl.BlockSpec((tb, w), lambda i: (nb - 1 - i, 0))
    vec = lambda w: pl.BlockSpec((1, w), lambda i: (0, 0))
    scr = pltpu.VMEM((tb, HK), F32)
    return pl.pallas_call(
        body, name="hgrn_bwd", grid=(nb,),
        in_specs=[rev_row(4 * HK), vec(HK), vec(DK), rev_row(HK),
                  pl.BlockSpec((ncb, H, DK, DK), lambda i: (nb - 1 - i, 0, 0, 0)), rev_row(HK)],
        out_specs=(rev_row(4 * HK), vec(HK), vec(DK)),
        out_shape=(jax.ShapeDtypeStruct((T, 4 * HK), BF16), jax.ShapeDtypeStruct((1, HK), F32),
                   jax.ShapeDtypeStruct((1, DK), F32)),
        scratch_shapes=[pltpu.VMEM((H, DK, DK), F32), scr, scr, scr, scr, scr, scr, scr],
        compiler_params=_cparams("arbitrary"))(proj, lb, o_norm, o, states, dy)


def _hgrn_layer_fwd(a, o_norm, lb, wbuf, pk, slot):
    D = a.shape[1]
    in_by_n, _ = _col_sharded(pk, "hgrn_w_in", slot, D)
    out_by_k, _, _ = _row_sharded(pk, "hgrn_w_o", slot, D // N_CHIPS)
    proj = _mm(a, wbuf, n=4 * D, b_map=in_by_n, tk=D, tn=D, name="hgrn_in")
    y, o, states = _hgrn_fwd(proj, lb, o_norm)
    m = _mm(y, wbuf, n=D, b_map=out_by_k, tm=2048, tk=D // N_CHIPS, tn=D, name="hgrn_out")
    return m, (a, proj, y, o, states)


def _hgrn_layer_bwd(dm, saved, o_norm, lb, wbuf, gbuf, pk, slot):
    a, proj, y, o, states = saved
    D = a.shape[1]
    in_by_n, in_by_k = _col_sharded(pk, "hgrn_w_in", slot, D)
    _, out_by_n, out_by_m = _row_sharded(pk, "hgrn_w_o", slot, D // N_CHIPS)
    dy = _mm(dm, wbuf, tb=True, n=y.shape[1], b_map=out_by_n, tm=2048, tn=D // N_CHIPS, tk=D,
             name="hgrn_out_dx")
    gbuf = _mm(y, dm, ta=True, into=gbuf, o_map=out_by_m, tm=D // N_CHIPS, tn=D, tk=2048, name="hgrn_out_dw")
    dproj, dlb, don = _hgrn_bwd(proj, lb, o_norm, o, states, dy)
    gbuf = _mm(a, dproj, ta=True, into=gbuf, o_map=in_by_n, tm=D, tn=D, name="hgrn_in_dw")
    da = _mm(dproj, wbuf, tb=True, n=D, b_map=in_by_k, tn=D, tk=D, name="hgrn_in_dx")
    return da, gbuf, dict(o_norm=don, lb=dlb)


def _lower_bounds(lb_logits):
    p = jax.nn.softmax(lb_logits.astype(F32), axis=0)
    return jnp.cumsum(p, axis=0) - p[0]


def _rope_tables(positions):
    inv_freq = jnp.power(ROPE_BASE, -jnp.arange(0, MLA_ROPE, 2, dtype=F32) / MLA_ROPE)
    ang = positions.astype(F32)[:, None] * inv_freq
    cos, sin = jnp.cos(ang), jnp.sin(ang)
    zero = jnp.zeros((positions.shape[0], 128 - MLA_ROPE), F32)
    return (jnp.concatenate([cos, cos, zero], axis=-1), jnp.concatenate([-sin, sin, zero], axis=-1))


def _pad_mla_weights(w_in, w_uq):
    w_in_p = jnp.pad(w_in, ((0, 0), (0, 0), (0, 128 - MLA_ROPE)))
    n, ql, _ = w_uq.shape
    w_uq_p = jnp.pad(w_uq.reshape(n, ql, MLA_HEADS, MLA_NOPE + MLA_ROPE),
                     ((0, 0), (0, 0), (0, 0), (0, MLA_QK_PAD - MLA_NOPE - MLA_ROPE)))
    return w_in_p, w_uq_p.reshape(n, ql, MLA_HEADS * MLA_QK_PAD)


def _local_step(x, positions, target, small, fetch, gbufs, emit):
    T, D = x.shape
    lbounds, lb_vjp = jax.vjp(_lower_bounds, small["hgrn_lb_logits"])
    cc, ss = _rope_tables(positions)
    fetched = {0: fetch(0, None)}
    gains = fetched[0]["gains"]
    tick = [jnp.zeros((), F32)]

    def g(layer, i):
        return gains[layer, i][None, :] + tick[0]

    def mla_weights(layer):
        f = fetched[layer]
        w_in_p, w_uq_p = _pad_mla_weights(f["w_in"][None], f["w_uq"][None])
        slot = layer // 2
        return dict(w_in=w_in_p[0], w_uq=w_uq_p[0], w_ukv=f["w_ukv"],
                    q_norm=small["mla_q_norm"][slot][None, :], kv_norm=small["mla_kv_norm"][slot][None, :])

    saved = []
    h = x
    a = _prenorm_fwd(x, g(0, 0))
    dy = sq = None
    for layer in range(DEPTH):
        slot = layer // 2
        if layer not in fetched:
            fetched[layer] = fetch(layer, a)
        wbuf, pk = fetched[layer]["wbuf"], fetched[layer]["pk"]
        if layer % 2 == 0:
            m, mix_saved = _mla_fwd(a, mla_weights(layer), cc, ss, wbuf, pk, slot)
        else:
            m, mix_saved = _hgrn_layer_fwd(a, small["hgrn_o_norm"][slot][None, :], lbounds[layer][None, :],
                                           wbuf, pk, slot)
        h1, a2 = _resnorm_fwd(h, m, g(layer, 1), g(layer, 2), name="resnorm_fwd_mix")
        u, mlp_saved = _mlp_fwd(a2, wbuf, pk, layer)
        if layer + 1 < DEPTH:
            h2, a = _resnorm_fwd(h1, u, g(layer, 3), g(layer + 1, 0), name="resnorm_fwd_mlp")
        else:
            h2 = None
            dy, sq = _resnorm_loss(h1, u, g(layer, 3), target)
        saved.append((h, m, h1, u, mix_saved, mlp_saved))
        h = h2

    n_mla, n_hgrn = (DEPTH + 1) // 2, DEPTH // 2
    dgains = [[None] * 4 for _ in range(DEPTH)]
    gw = {k: [None] * n_mla for k in ("mla_w_in", "mla_w_uq", "mla_w_ukv", "mla_q_norm", "mla_kv_norm")}
    gw["hgrn_o_norm"] = [None] * n_hgrn
    dlb = [jnp.zeros((1, lbounds.shape[1]), F32) for _ in range(DEPTH)]
    dh = dy
    da_next = None
    for layer in reversed(range(DEPTH)):
        h0, m, h1, u, mix_saved, mlp_saved = saved[layer]
        slot = layer // 2
        wbuf, pk, gbuf = fetched[layer]["wbuf"], fetched[layer]["pk"], gbufs[layer]
        if da_next is None:
            du, dgains[layer][3] = _resnorm_bwd(u, g(layer, 3), dh, name="resnorm_bwd_last")
            t = dh
        else:
            h2 = saved[layer + 1][0]
            t, du, dgains[layer][3], dgains[layer + 1][0] = _resnorm_bwd(
                u, g(layer, 3), dh, h2, da_next, g(layer + 1, 0), name="resnorm_bwd_mlp")
        da2, gbuf = _mlp_bwd(du, mlp_saved, wbuf, gbuf, pk, layer)
        t, dm, dgains[layer][1], dgains[layer][2] = _resnorm_bwd(
            m, g(layer, 1), t, h1, da2, g(layer, 2), name="resnorm_bwd_mix")
        if layer % 2 == 0:
            da_next, gbuf, mg = _mla_bwd(dm, mix_saved, mla_weights(layer), cc, ss, wbuf, gbuf, pk, slot)
            ql = mg["q_norm"].shape[-1]
            kvl = mg["kv_norm"].shape[-1]
            gw["mla_w_in"][slot] = mg["w_in"][:, :ql + kvl + MLA_ROPE]
            gw["mla_w_uq"][slot] = mg["w_uq"].reshape(ql, MLA_HEADS, MLA_QK_PAD)[
                :, :, :MLA_NOPE + MLA_ROPE].reshape(ql, MLA_HEADS * (MLA_NOPE + MLA_ROPE))
            gw["mla_w_ukv"][slot] = mg["w_ukv"]
            gw["mla_q_norm"][slot] = mg["q_norm"][0]
            gw["mla_kv_norm"][slot] = mg["kv_norm"][0]
        else:
            da_next, gbuf, hg = _hgrn_layer_bwd(dm, mix_saved, small["hgrn_o_norm"][slot][None, :],
                                                lbounds[layer][None, :], wbuf, gbuf, pk, slot)
            gw["hgrn_o_norm"][slot] = hg["o_norm"][0]
            dlb[layer] = hg["lb"]
        dh = t
        if layer > 0:
            mine = ({k: gw[k][slot] for k in ("mla_w_in", "mla_w_uq", "mla_w_ukv")} if layer % 2 == 0 else {})
            tick[0] = emit(layer, gbuf, mine)
        else:
            gbuf0 = gbuf
    grad_x, dgains[0][0] = _prenorm_bwd(x, g(0, 0), dh, da_next)

    last = {k: gw[k][0] for k in ("mla_w_in", "mla_w_uq", "mla_w_ukv")}
    last.update({k: jnp.stack(gw[k]) for k in ("mla_q_norm", "mla_kv_norm", "hgrn_o_norm")})
    last["norm_gains"] = jnp.stack([jnp.concatenate(row, axis=0) for row in dgains])
    (last["hgrn_lb_logits"],) = lb_vjp(jnp.concatenate(dlb, axis=0))
    emit(0, gbuf0, last)
    return sq, grad_x


def _size(shape):
    n = 1
    for d in shape:
        n *= d
    return n


def _piece_rows(shape):
    return -(-_size(shape) // PACK_W)


def _packed_misc_rows(shapes):
    return sum(_piece_rows(s) for s in shapes)


def _cast_into(src, buf, row, name):
    rows, W = src.shape
    tr = min(256, rows)
    assert rows % tr == 0 and row % tr == 0

    def body(s_ref, b_ref, o_ref):
        o_ref[...] = s_ref[...].astype(BF16)

    return pl.pallas_call(
        body, name=name, grid=(rows // tr,),
        in_specs=[pl.BlockSpec((tr, W), lambda i: (i, 0)), pl.BlockSpec(memory_space=pl.ANY)],
        out_specs=pl.BlockSpec((tr, W), lambda i: (row // tr + i, 0)),
        out_shape=jax.ShapeDtypeStruct(buf.shape, buf.dtype), input_output_aliases={1: 0},
        compiler_params=_cparams("parallel"))(src, buf)


def _pack_blocks(pieces, rows, dtype):
    blocks, used = [], 0
    for p in pieces:
        flat = p.astype(dtype).reshape(-1)
        r = _piece_rows(p.shape)
        if r * PACK_W != flat.shape[0]:
            flat = jnp.pad(flat, (0, r * PACK_W - flat.shape[0]))
        blocks.append(flat.reshape(r, PACK_W))
        used += r
    if rows > used:
        blocks.append(jnp.zeros((rows - used, PACK_W), dtype))
    return blocks


def _unpack(buf, shapes):
    out, off = [], 0
    for shp in shapes:
        r = _piece_rows(shp)
        piece = buf[off:off + r]
        if r * PACK_W != _size(shp):
            piece = piece.reshape(-1)[:_size(shp)]
        out.append(piece.reshape(shp))
        off += r
    return out


def _mesh_place():
    x, y, c = lax.axis_index("x"), lax.axis_index("y"), lax.axis_index("c")
    chips = [(1 - x, y), (x, 1 - y), (1 - x, 1 - y)]
    return x, y, c, chips


_HBM = pl.BlockSpec(memory_space=pltpu.HBM)


def _all_gather(wp):
    R, W = wp.shape
    rh = R // 2
    rq = rh // 2
    assert rq % 16 == 0

    def body(w_ref, out_ref, send_sems, recv_sems):
        x, y, c, _ = _mesh_place()
        me, jx, jy, jd = 2 * x + y, 2 * (1 - x) + y, 2 * x + (1 - y), 2 * (1 - x) + (1 - y)
        to_x, to_y, sibling = (1 - x, y, c), (x, 1 - y, c), (x, y, 1 - c)

        def rows(core, quarter):
            return pl.ds(pl.multiple_of(core * rh + quarter * rq, 16), rq)

        def slot(j, core, quarter):
            return out_ref.at[j, rows(core, quarter)]

        def copy(k, src, dst, to):
            return pltpu.make_async_remote_copy(src_ref=src, dst_ref=dst, send_sem=send_sems.at[k],
                                                recv_sem=recv_sems.at[k], device_id=to, device_id_type=MESH)

        sends = [copy(0, w_ref.at[rows(c, 0)], slot(me, c, 0), to_x),
                 copy(2, w_ref.at[rows(c, 1)], slot(me, c, 1), to_y),
                 copy(1, w_ref.at[rows(c, 1)], slot(me, c, 1), to_x),
                 copy(3, w_ref.at[rows(c, 0)], slot(me, c, 0), to_y)]
        for cp in sends:
            cp.start()
        arrivals = [(0, slot(jx, c, 0), 4, to_y, 6), (2, slot(jy, c, 1), 5, to_x, 7),
                    (1, slot(jx, c, 1), None, None, 8), (3, slot(jy, c, 0), None, None, 9),
                    (4, slot(jd, c, 0), None, None, 10), (5, slot(jd, c, 1), None, None, 11)]
        for k, landed, k_on, to_on, k_sib in arrivals:
            copy(k, landed, landed, sibling).wait_recv()
            if k_on is not None:
                cp = copy(k_on, landed, landed, to_on)
                cp.start()
                sends.append(cp)
            cp = copy(k_sib, landed, landed, sibling)
            cp.start()
            sends.append(cp)
        for k_sib, j, quarter in ((6, jx, 0), (7, jy, 1), (8, jx, 1), (9, jy, 0), (10, jd, 0), (11, jd, 1)):
            landed = slot(j, 1 - c, quarter)
            copy(k_sib, landed, landed, sibling).wait_recv()
        for cp in sends:
            cp.wait_send()

    out = pl.pallas_call(
        body, name="weights_all_gather", in_specs=[_HBM], out_specs=_HBM,
        out_shape=jax.ShapeDtypeStruct((N_CHIPS, R, W), wp.dtype),
        scratch_shapes=[pltpu.SemaphoreType.DMA((12,)), pltpu.SemaphoreType.DMA((12,))],
    )(wp)
    me = 2 * lax.axis_index("x") + lax.axis_index("y")
    return lax.dynamic_update_slice(out, wp[None], (me, 0, 0))


def _exchange_halves(g):
    n, _, rh, W = g.shape

    def body(g_ref, out_ref, send_sems, recv_sems):
        x, y, c, _ = _mesh_place()
        sibling = (x, y, 1 - c)
        copies = [pltpu.make_async_remote_copy(
            src_ref=g_ref.at[j, 1 - c], dst_ref=out_ref.at[j], send_sem=send_sems.at[j],
            recv_sem=recv_sems.at[j], device_id=sibling, device_id_type=MESH) for j in range(n)]
        for cp in copies:
            cp.start()
        for cp in copies:
            cp.wait()

    return pl.pallas_call(
        body, name="grads_to_sibling", in_specs=[_HBM], out_specs=_HBM,
        out_shape=jax.ShapeDtypeStruct((n, rh, W), g.dtype),
        scratch_shapes=[pltpu.SemaphoreType.DMA((n,)), pltpu.SemaphoreType.DMA((n,))],
    )(g)


def _scatter_to_owners(p):
    n, rh, W = p.shape
    rq = rh // 2
    assert rq % 16 == 0

    def body(p_ref, out_ref, stage_ref, send_sems, recv_sems):
        x, y, c, _ = _mesh_place()
        me, jx, jy, jd = 2 * x + y, 2 * (1 - x) + y, 2 * x + (1 - y), 2 * (1 - x) + (1 - y)
        to_x, to_y = (1 - x, y, c), (x, 1 - y, c)

        def quarter(ref, j, q):
            return ref.at[j, pl.ds(q * rq, rq)]

        def copy(k, src, dst, to):
            return pltpu.make_async_remote_copy(src_ref=src, dst_ref=dst, send_sem=send_sems.at[k],
                                                recv_sem=recv_sems.at[k], device_id=to, device_id_type=MESH)

        sends = [copy(2, quarter(p_ref, jd, 0), stage_ref.at[0], to_x),
                 copy(3, quarter(p_ref, jd, 1), stage_ref.at[1], to_y),
                 copy(0, p_ref.at[jx], out_ref.at[me], to_x),
                 copy(1, p_ref.at[jy], out_ref.at[me], to_y)]
        for cp in sends:
            cp.start()
        copy(2, stage_ref.at[0], stage_ref.at[0], to_x).wait_recv()
        relay = copy(4, stage_ref.at[0], quarter(out_ref, jx, 0), to_y)
        relay.start()
        sends.append(relay)
        copy(3, stage_ref.at[1], stage_ref.at[1], to_y).wait_recv()
        relay = copy(5, stage_ref.at[1], quarter(out_ref, jy, 1), to_x)
        relay.start()
        sends.append(relay)
        copy(0, out_ref.at[jx], out_ref.at[jx], to_x).wait_recv()
        copy(1, out_ref.at[jy], out_ref.at[jy], to_y).wait_recv()
        copy(4, quarter(out_ref, jd, 0), quarter(out_ref, jd, 0), to_y).wait_recv()
        copy(5, quarter(out_ref, jd, 1), quarter(out_ref, jd, 1), to_x).wait_recv()
        for cp in sends:
            cp.wait_send()

    out, _ = pl.pallas_call(
        body, name="grads_to_owner", in_specs=[_HBM], out_specs=(_HBM, _HBM),
        out_shape=(jax.ShapeDtypeStruct((n, rh, W), p.dtype), jax.ShapeDtypeStruct((2, rq, W), p.dtype)),
        scratch_shapes=[pltpu.SemaphoreType.DMA((6,)), pltpu.SemaphoreType.DMA((6,))],
    )(p)
    me = 2 * lax.axis_index("x") + lax.axis_index("y")
    mine = lax.dynamic_index_in_dim(p, me, axis=0, keepdims=True)
    return lax.dynamic_update_slice(out, mine, (me, 0, 0))


def _share_reduced(q, name="grads_share_reduced"):
    rh, W = q.shape

    def body(q_ref, out_ref, send_sem, recv_sem):
        x, y, c, _ = _mesh_place()
        cp = pltpu.make_async_remote_copy(src_ref=q_ref, dst_ref=out_ref.at[c], send_sem=send_sem,
                                          recv_sem=recv_sem, device_id=(x, y, 1 - c), device_id_type=MESH)
        cp.start()
        cp.wait()

    out = pl.pallas_call(
        body, name=name, in_specs=[_HBM], out_specs=_HBM,
        out_shape=jax.ShapeDtypeStruct((2, rh, W), q.dtype),
        scratch_shapes=[pltpu.SemaphoreType.DMA, pltpu.SemaphoreType.DMA],
    )(q)
    return lax.dynamic_update_slice(out, q[None], (lax.axis_index("c"), 0, 0))


def _add_sibling(g, recv, c_arr):
    n, _, rh, W = g.shape
    tr = PACK_TILE

    def body(c_ref, g_ref, r_ref, o_ref):
        o_ref[...] = (g_ref[...].astype(F32) + r_ref[...].astype(F32)).astype(BF16)

    return pl.pallas_call(
        body, name="grads_add_sibling",
        grid_spec=pltpu.PrefetchScalarGridSpec(
            num_scalar_prefetch=1, grid=(n, rh // tr),
            in_specs=[pl.BlockSpec((None, None, tr, W), lambda j, i, c_ref: (j, c_ref[0], i, 0)),
                      pl.BlockSpec((None, tr, W), lambda j, i, c_ref: (j, i, 0))],
            out_specs=pl.BlockSpec((None, tr, W), lambda j, i, c_ref: (j, i, 0))),
        out_shape=jax.ShapeDtypeStruct((n, rh, W), BF16),
        compiler_params=_cparams("parallel", "parallel"))(c_arr, g, recv)


def _sum_chips(parts, name="grads_sum_chips"):
    n, rh, W = parts.shape
    tr = PACK_TILE

    def body(p_ref, o_ref):
        acc = p_ref[0].astype(F32)
        for j in range(1, n):
            acc = acc + p_ref[j].astype(F32)
        o_ref[...] = acc

    return pl.pallas_call(
        body, name=name, grid=(rh // tr,),
        in_specs=[pl.BlockSpec((n, tr, W), lambda i: (0, i, 0))],
        out_specs=pl.BlockSpec((tr, W), lambda i: (i, 0)),
        out_shape=jax.ShapeDtypeStruct((rh, W), F32),
        compiler_params=_cparams("parallel"))(parts)


_SEM = pl.BlockSpec(memory_space=pltpu.SEMAPHORE)
_ASYNC = pltpu.CompilerParams(has_side_effects=pltpu.SideEffectType.DATAFLOW_SIDE_EFFECTING)


def _hbm(a):
    return pltpu.with_memory_space_constraint(a, pltpu.HBM)


def _gather_copies(w_ref, land_ref, send_sems, recv_sems):
    x, y, c, chips = _mesh_place()
    me = 2 * x + y
    rh = w_ref.shape[0] // 2
    rows = pl.ds(pl.multiple_of(c * rh, 16), rh)
    return [pltpu.make_async_remote_copy(
        src_ref=w_ref.at[rows], dst_ref=land_ref.at[me, rows], send_sem=send_sems.at[2 * r + core],
        recv_sem=recv_sems.at[2 * r + c], device_id=(px, py, core), device_id_type=MESH)
        for r, (px, py) in enumerate(chips) for core in range(2)]


def _scatter_copies(g_ref, land_ref, send_sems, recv_sems):
    x, y, c, chips = _mesh_place()
    me = 2 * x + y
    return [pltpu.make_async_remote_copy(
        src_ref=g_ref.at[2 * px + py], dst_ref=land_ref.at[me], send_sem=send_sems.at[r],
        recv_sem=recv_sems.at[r], device_id=(px, py, c), device_id_type=MESH)
        for r, (px, py) in enumerate(chips)]


def _gather_start(wp, name):
    R, W = wp.shape

    def body(w_ref, land_ref, send_sems, recv_sems, w_thru, land_thru, token):
        for cp in _gather_copies(w_ref, land_ref, send_sems, recv_sems):
            cp.start()
        token[...] = jnp.zeros_like(token)

    return pl.pallas_call(
        body, name=name,
        out_shape=(pltpu.SemaphoreType.DMA((6,)), pltpu.SemaphoreType.DMA((6,)), pltpu.HBM(wp.shape, wp.dtype),
                   pltpu.HBM((N_CHIPS, R, W), wp.dtype), jax.ShapeDtypeStruct((8, 128), F32)),
        in_specs=(_HBM, _HBM),
        out_specs=(_SEM, _SEM, _HBM, _HBM, pl.BlockSpec(memory_space=pltpu.VMEM)),
        input_output_aliases={0: 2, 1: 3}, compiler_params=_ASYNC,
    )(_hbm(wp), _hbm(lax.empty((N_CHIPS, R, W), wp.dtype)))


def _gather_wait(send_sems, recv_sems, w_thru, land_thru, after, name):
    R, W = w_thru.shape
    rh = R // 2

    def body(w_ref, land_ref, send_sems, recv_sems, after_ref, w_dead, got_ref):
        x, y, c, _ = _mesh_place()
        half = land_ref.at[0, pl.ds(0, rh)]
        for k in range(6):
            cp = pltpu.make_async_remote_copy(src_ref=half, dst_ref=half, send_sem=send_sems.at[k],
                                              recv_sem=recv_sems.at[k], device_id=(x, y, 1 - c),
                                              device_id_type=MESH)
            cp.wait_send()
            cp.wait_recv()

    return pl.pallas_call(
        body, name=name,
        out_shape=(pltpu.HBM(w_thru.shape, w_thru.dtype), pltpu.HBM(land_thru.shape, land_thru.dtype)),
        in_specs=(_HBM, _HBM, _SEM, _SEM, pl.BlockSpec(memory_space=pl.ANY)), out_specs=(_HBM, _HBM),
        input_output_aliases={0: 0, 1: 1}, compiler_params=_ASYNC,
    )(w_thru, land_thru, send_sems, recv_sems, after)


def _scatter_start(g, name):
    n, R, W = g.shape

    def body(g_ref, land_ref, send_sems, recv_sems, g_thru, land_thru, token):
        for cp in _scatter_copies(g_ref, land_ref, send_sems, recv_sems):
            cp.start()
        token[...] = jnp.zeros_like(token)

    return pl.pallas_call(
        body, name=name,
        out_shape=(pltpu.SemaphoreType.DMA((3,)), pltpu.SemaphoreType.DMA((3,)), pltpu.HBM(g.shape, g.dtype),
                   pltpu.HBM(g.shape, g.dtype), jax.ShapeDtypeStruct((8, 128), F32)),
        in_specs=(_HBM, _HBM),
        out_specs=(_SEM, _SEM, _HBM, _HBM, pl.BlockSpec(memory_space=pltpu.VMEM)),
        input_output_aliases={0: 2, 1: 3}, compiler_params=_ASYNC,
    )(_hbm(g), _hbm(lax.empty(g.shape, g.dtype)))


def _scatter_wait(send_sems, recv_sems, g_thru, land_thru, after, name):
    def body(g_ref, land_ref, send_sems, recv_sems, after_ref, g_out, got_ref):
        x, y, c, _ = _mesh_place()
        for k in range(3):
            cp = pltpu.make_async_remote_copy(src_ref=g_ref.at[0], dst_ref=land_ref.at[0], send_sem=send_sems.at[k],
                                              recv_sem=recv_sems.at[k], device_id=(x, y, 1 - c),
                                              device_id_type=MESH)
            cp.wait_send()
            cp.wait_recv()

    return pl.pallas_call(
        body, name=name,
        out_shape=(pltpu.HBM(g_thru.shape, g_thru.dtype), pltpu.HBM(land_thru.shape, land_thru.dtype)),
        in_specs=(_HBM, _HBM, _SEM, _SEM, pl.BlockSpec(memory_space=pl.ANY)), out_specs=(_HBM, _HBM),
        input_output_aliases={0: 0, 1: 1}, compiler_params=_ASYNC,
    )(g_thru, land_thru, send_sems, recv_sems, after)


def _adamw(w, g, m, v, name):
    shape = w.shape
    cols = shape[-1]
    w2, g2, m2, v2 = (t.reshape(-1, cols) for t in (w, g, m, v))
    rows = w2.shape[0]
    tr = rows
    for cand in (512, 256, 128, 64, 32, 16, 8):
        if rows > cand and rows % cand == 0:
            tr = cand
            break
    c1 = 1.0 / (1.0 - ADAM_B1 ** ADAM_STEP)
    c2 = 1.0 / (1.0 - ADAM_B2 ** ADAM_STEP)

    def body(w_ref, g_ref, m_ref, v_ref, d_ref, nm_ref, nv_ref):
        gv = g_ref[...]
        nm = ADAM_B1 * m_ref[...] + (1.0 - ADAM_B1) * gv
        nv = ADAM_B2 * v_ref[...] + (1.0 - ADAM_B2) * (gv * gv)
        nm_ref[...] = nm
        nv_ref[...] = nv
        d_ref[...] = -ADAM_LR * ((nm * c1) / (jnp.sqrt(nv * c2) + ADAM_EPS) + ADAM_WD * w_ref[...])

    blk = pl.BlockSpec((tr, cols), lambda i: (i, 0))
    sds = jax.ShapeDtypeStruct((rows, cols), F32)
    d, nm, nv = pl.pallas_call(body, name=name, grid=(rows // tr,), in_specs=[blk] * 4,
                               out_specs=(blk, blk, blk), out_shape=(sds, sds, sds),
                               compiler_params=_cparams("parallel"))(w2, g2, m2, v2)
    return d.reshape(shape), nm.reshape(shape), nv.reshape(shape)


def kernel(x, positions, norm_gains, mla_w_in, mla_q_norm, mla_kv_norm, mla_w_uq, mla_w_ukv, mla_w_o, hgrn_w_in, hgrn_lb_logits, hgrn_o_norm, hgrn_w_o, mlp_w1, mlp_w2, loss_target, m_norm_gains, m_mla_w_in, m_mla_q_norm, m_mla_kv_norm, m_mla_w_uq, m_mla_w_ukv, m_mla_w_o, m_hgrn_w_in, m_hgrn_lb_logits, m_hgrn_o_norm, m_hgrn_w_o, m_mlp_w1, m_mlp_w2, v_norm_gains, v_mla_w_in, v_mla_q_norm, v_mla_kv_norm, v_mla_w_uq, v_mla_w_ukv, v_mla_w_o, v_hgrn_w_in, v_hgrn_lb_logits, v_hgrn_o_norm, v_hgrn_w_o, v_mlp_w1, v_mlp_w2):
    w = dict(norm_gains=norm_gains, mla_w_in=mla_w_in, mla_q_norm=mla_q_norm, mla_kv_norm=mla_kv_norm,
             mla_w_uq=mla_w_uq, mla_w_ukv=mla_w_ukv, mla_w_o=mla_w_o, hgrn_w_in=hgrn_w_in,
             hgrn_lb_logits=hgrn_lb_logits, hgrn_o_norm=hgrn_o_norm, hgrn_w_o=hgrn_w_o,
             mlp_w1=mlp_w1, mlp_w2=mlp_w2)
    mom_m = dict(norm_gains=m_norm_gains, mla_w_in=m_mla_w_in, mla_q_norm=m_mla_q_norm,
                 mla_kv_norm=m_mla_kv_norm, mla_w_uq=m_mla_w_uq, mla_w_ukv=m_mla_w_ukv,
                 mla_w_o=m_mla_w_o, hgrn_w_in=m_hgrn_w_in, hgrn_lb_logits=m_hgrn_lb_logits,
                 hgrn_o_norm=m_hgrn_o_norm, hgrn_w_o=m_hgrn_w_o, mlp_w1=m_mlp_w1, mlp_w2=m_mlp_w2)
    mom_v = dict(norm_gains=v_norm_gains, mla_w_in=v_mla_w_in, mla_q_norm=v_mla_q_norm,
                 mla_kv_norm=v_mla_kv_norm, mla_w_uq=v_mla_w_uq, mla_w_ukv=v_mla_w_ukv,
                 mla_w_o=v_mla_w_o, hgrn_w_in=v_hgrn_w_in, hgrn_lb_logits=v_hgrn_lb_logits,
                 hgrn_o_norm=v_hgrn_o_norm, hgrn_w_o=v_hgrn_w_o, mlp_w1=v_mlp_w1, mlp_w2=v_mlp_w2)
    c = lax.axis_index("c")

    axis_of = dict(SHARDED)
    me = 2 * lax.axis_index("x") + lax.axis_index("y")
    gain_bits = lax.bitcast_convert_type(norm_gains, jnp.uint32)
    gain_hi = lax.bitcast_convert_type((gain_bits >> 16).astype(jnp.uint16), BF16)
    gain_lo = lax.bitcast_convert_type((gain_bits & 0xFFFF).astype(jnp.uint16), BF16)

    layers = []
    for l in range(DEPTH):
        s = l // 2
        if l % 2 == 0:
            big = [("mlp_w1", l), ("mlp_w2", l), ("mla_w_o", s)]
            tail = [("mla_w_in", s), ("mla_w_uq", s), ("mla_w_ukv", s)]
        else:
            big = [("hgrn_w_in", s), ("mlp_w1", l), ("mlp_w2", l), ("hgrn_w_o", s)]
            tail = []
        w_tail = [w[n][i] for n, i in tail] + ([gain_hi, gain_lo] if l == 0 else [])
        g_tail = tail + ([("norm_gains", None)] + [(n, None) for n in REPLICATED] if l == 0 else [])
        g_shapes = [w[n].shape if i is None else w[n][i].shape for n, i in g_tail]
        tail_rows = max(_packed_misc_rows([t.shape for t in w_tail]), _packed_misc_rows(g_shapes))
        pk = _Packed([(n, w[n].shape[1]) for n, _ in big], tail_rows)
        wpack = jnp.zeros((pk.rows, PACK_W), BF16)
        for n, i in big:
            assert w[n].shape[2] == PACK_W
            wpack = _cast_into(w[n][i], wpack, pk.off[n], name="pack_%s_%d" % (n, l))
        if w_tail:
            wpack = lax.dynamic_update_slice(
                wpack, jnp.concatenate(_pack_blocks(w_tail, 0, BF16), axis=0), (pk.misc, 0))
        layers.append(dict(pk=pk, big=big, tail=tail, w_tail=w_tail, g_tail=g_tail, g_shapes=g_shapes,
                           wpack=wpack))

    for l, lay in enumerate(layers):
        lay["gather"] = _gather_start(lay["wpack"], name="gather_start_%d" % l)

    def fetch(l, after):
        lay = layers[l]
        pk = lay["pk"]
        send_sems, recv_sems, w_thru, land_thru, _ = lay["gather"]
        if after is None:
            after = sum(layers[k]["gather"][4] for k in range(1, DEPTH))
        w_back, land = _gather_wait(send_sems, recv_sems, w_thru, land_thru, after, name="gather_wait_%d" % l)
        land = lax.dynamic_update_slice(land, w_back[None], (me, 0, 0))
        out = dict(wbuf=land.reshape(N_CHIPS * pk.rows, PACK_W), pk=pk)
        if lay["w_tail"]:
            rows = _packed_misc_rows([t.shape for t in lay["w_tail"]])
            per_chip = [_unpack(land[j, pk.misc:pk.misc + rows], [t.shape for t in lay["w_tail"]])
                        for j in range(N_CHIPS)]
            for i, (n, _) in enumerate(lay["tail"]):
                out[n[4:]] = jnp.concatenate([per_chip[j][i] for j in range(N_CHIPS)], axis=axis_of[n] - 1)
            if l == 0:
                got_hi, got_lo = (lax.bitcast_convert_type(
                    jnp.concatenate([per_chip[j][i] for j in range(N_CHIPS)], axis=2),
                    jnp.uint16).astype(jnp.uint32) for i in (-2, -1))
                out["gains"] = lax.bitcast_convert_type((got_hi << 16) | got_lo, F32)
        return out

    def emit(l, gbuf, grads):
        lay = layers[l]
        pk = lay["pk"]
        if lay["g_tail"]:
            for j in range(N_CHIPS):
                pieces = []
                for n, i in lay["g_tail"]:
                    if n not in axis_of:
                        pieces.append(grads[n])
                    else:
                        pieces.append(jnp.split(grads[n], N_CHIPS, axis=axis_of[n] - (0 if i is None else 1))[j])
                block = jnp.concatenate(_pack_blocks(pieces, 0, BF16), axis=0)
                gbuf = lax.dynamic_update_slice(gbuf, block, (j * pk.rows + pk.misc, 0))
        lay["scatter"] = _scatter_start(gbuf.reshape(N_CHIPS, pk.rows, PACK_W), name="scatter_start_%d" % l)
        return lay["scatter"][4][0, 0]

    small = dict(mla_q_norm=mla_q_norm, mla_kv_norm=mla_kv_norm, hgrn_lb_logits=hgrn_lb_logits,
                 hgrn_o_norm=hgrn_o_norm)
    gbufs = [jnp.zeros((N_CHIPS * lay["pk"].rows, PACK_W), BF16) for lay in layers]
    sq, grad_x = _local_step(x[0], positions[0], loss_target[0], small, fetch, gbufs, emit)
    d_model = x.shape[-1]
    loss = lax.psum(0.5 * jnp.sum(sq) / d_model, ("x", "y", "c"))

    per_name = {}
    for l, lay in enumerate(layers):
        pk = lay["pk"]
        send_sems, recv_sems, g_thru, land_thru, _ = lay["scatter"]
        g_back, land = _scatter_wait(send_sems, recv_sems, g_thru, land_thru, grad_x, name="scatter_wait_%d" % l)
        land = lax.dynamic_update_slice(land, lax.dynamic_slice_in_dim(g_back, me, 1, axis=0), (me, 0, 0))
        mine = _sum_chips(land, name="grads_sum_chips_%d" % l)
        red = _sum_chips(_share_reduced(mine, name="grads_share_%d" % l), name="grads_sum_cores_%d" % l)
        for n, i in lay["big"]:
            per_name.setdefault(n, {})[i] = red[pk.off[n]:pk.off[n] + w[n].shape[1]]
        for (n, i), piece in zip(lay["g_tail"], _unpack(red[pk.misc:pk.misc + pk.misc_rows], lay["g_shapes"])):
            per_name.setdefault(n, {})[i] = piece
    g_out = {n: (parts[None] if None in parts else jnp.stack([parts[i] for i in sorted(parts)]))
             for n, parts in per_name.items()}

    deltas, new_m, new_v = {}, {}, {}
    for name in WEIGHTS:
        deltas[name], new_m[name], new_v[name] = _adamw(w[name], g_out[name], mom_m[name], mom_v[name],
                                                        name="adamw_" + name)
    return (loss, grad_x[None], *[g_out[n] for n in WEIGHTS], *[deltas[n] for n in WEIGHTS],
            *[new_m[n] for n in WEIGHTS], *[new_v[n] for n in WEIGHTS])
```

```python
import functools

import jax
import jax.numpy as jnp
from jax import lax
from jax.experimental import pallas as pl
from jax.experimental.pallas import tpu as pltpu

F32 = jnp.float32
BF16 = jnp.bfloat16
MESH = pl.DeviceIdType.MESH

DEPTH = 4
MLA_HEADS = 8
MLA_NOPE = 128
MLA_ROPE = 64
MLA_V = 128
MLA_QK_PAD = 256
MLA_HEADS_PER_STEP = 2
ROPE_BASE = 10000.0
HGRN_HEADS = 8
HGRN_CHUNK = 32
HGRN_BLOCK = 128
EPS = 1e-6

ADAM_LR = 0.001
ADAM_B1 = 0.9
ADAM_B2 = 0.999
ADAM_EPS = 1e-08
ADAM_WD = 0.01
ADAM_STEP = 10

N_CHIPS = 4
PACK_W = 1024
PACK_ALIGN = 1024
PACK_TILE = 512
V7X_VMEM_LIMIT = 56 * 1024 * 1024

SHARDED = (("norm_gains", 2), ("mla_w_in", 1), ("mla_w_uq", 2), ("mla_w_ukv", 2), ("mla_w_o", 1),
           ("hgrn_w_in", 2), ("hgrn_w_o", 1), ("mlp_w1", 2), ("mlp_w2", 1))
REPLICATED = ("mla_q_norm", "mla_kv_norm", "hgrn_lb_logits", "hgrn_o_norm")
WEIGHTS = ("norm_gains", "mla_w_in", "mla_q_norm", "mla_kv_norm", "mla_w_uq", "mla_w_ukv", "mla_w_o",
           "hgrn_w_in", "hgrn_lb_logits", "hgrn_o_norm", "hgrn_w_o", "mlp_w1", "mlp_w2")


def _cparams(*semantics):
    return pltpu.CompilerParams(dimension_semantics=semantics, vmem_limit_bytes=V7X_VMEM_LIMIT)


def _sigmoid(x):
    return 1.0 / (1.0 + jnp.exp(-x))


def _mm(a, b, *, ta=False, tb=False, out_dtype=F32, tm=1024, tn=1024, tk=1024, epi=None, extra=None,
        name="mm", n=None, b_map=None, into=None, o_map=None):
    if ta:
        K, M = a.shape
    else:
        M, K = a.shape
    if b_map is not None:
        N = n
    elif tb:
        N, Kb = b.shape
    else:
        Kb, N = b.shape
    assert b_map is not None or K == Kb, (a.shape, b.shape, ta, tb)
    tm, tn = min(tm, M), min(tn, N)
    tk = K if (K <= 1024 and b_map is None) else min(tk, K)
    assert M % tm == 0 and N % tn == 0 and K % tk == 0, (M, N, K, tm, tn, tk)
    nk = K // tk
    a_spec = (pl.BlockSpec((tk, tm), lambda i, j, k: (k, i)) if ta
              else pl.BlockSpec((tm, tk), lambda i, j, k: (i, k)))
    if b_map is None:
        b_map = (lambda i, j, k: (j, k)) if tb else (lambda i, j, k: (k, j))
    b_spec = pl.BlockSpec((tn, tk) if tb else (tk, tn), b_map)
    o_spec = pl.BlockSpec((tm, tn), lambda i, j, k: (i, j))
    dims = (((0 if ta else 1,), (1 if tb else 0,)), ((), ()))
    in_specs = [a_spec, b_spec]
    operands = [a, b]
    aliases = {}
    if epi == "mul2r":
        in_specs.append(o_spec)
        operands.append(extra)
    if into is not None:
        assert epi is None
        in_specs.append(pl.BlockSpec(memory_space=pl.ANY))
        operands.append(into)
        aliases = {2: 0}
        out_dtype = into.dtype
        out_shape = jax.ShapeDtypeStruct(into.shape, into.dtype)
        out_specs = pl.BlockSpec((tm, tn), o_map)
    elif epi == "relu2":
        out_shape = (jax.ShapeDtypeStruct((M, N), BF16), jax.ShapeDtypeStruct((M, N), BF16))
        out_specs = (o_spec, o_spec)
    elif epi == "mul2r":
        out_shape = jax.ShapeDtypeStruct((M, N), BF16)
        out_specs = o_spec
    else:
        out_shape = jax.ShapeDtypeStruct((M, N), out_dtype)
        out_specs = o_spec
    n_in = len(operands)

    def body(*refs):
        a_ref, b_ref = refs[0], refs[1]
        outs = refs[n_in:n_in + (2 if epi == "relu2" else 1)]
        k = pl.program_id(2)

        def finish(acc):
            if epi == "relu2":
                r = jnp.maximum(acc, 0.0)
                outs[0][...] = (r * r).astype(BF16)
                outs[1][...] = r.astype(BF16)
            elif epi == "mul2r":
                outs[0][...] = (acc * (2.0 * refs[2][...].astype(F32))).astype(BF16)
            else:
                outs[0][...] = acc.astype(out_dtype)

        part = lax.dot_general(a_ref[...], b_ref[...], dims, preferred_element_type=F32)
        if nk == 1:
            finish(part)
            return
        acc_ref = refs[-1]

        @pl.when(k == 0)
        def _():
            acc_ref[...] = part

        @pl.when((k > 0) & (k < nk - 1))
        def _():
            acc_ref[...] += part

        @pl.when(k == nk - 1)
        def _():
            finish(acc_ref[...] + part)

    return pl.pallas_call(
        body, name=name, grid=(M // tm, N // tn, nk), in_specs=in_specs, out_specs=out_specs,
        out_shape=out_shape, scratch_shapes=[pltpu.VMEM((tm, tn), F32)] if nk > 1 else [],
        input_output_aliases=aliases,
        compiler_params=_cparams("parallel", "parallel", "arbitrary"))(*operands)


def _rms_rstd(x):
    return lax.rsqrt(jnp.mean(x * x, axis=-1, keepdims=True) + EPS)


def _rms_bwd_tile(x, g, dy):
    r = _rms_rstd(x)
    xh = x * r
    u = dy * g
    dx = r * (u - xh * jnp.mean(u * xh, axis=-1, keepdims=True))
    dg = jnp.sum(dy * xh, axis=0, keepdims=True)
    return dx, dg


def _row_tile(T):
    return min(256, T)


def _prenorm_fwd(x, g, name="prenorm_fwd"):
    T, D = x.shape
    tm = _row_tile(T)

    def body(x_ref, g_ref, a_ref):
        xv = x_ref[...]
        a_ref[...] = (xv * _rms_rstd(xv) * g_ref[...]).astype(BF16)

    row = pl.BlockSpec((tm, D), lambda i: (i, 0))
    vec = pl.BlockSpec((1, D), lambda i: (0, 0))
    return pl.pallas_call(body, name=name, grid=(T // tm,), in_specs=[row, vec], out_specs=row,
                          out_shape=jax.ShapeDtypeStruct((T, D), BF16),
                          compiler_params=_cparams("parallel"))(x, g)


def _resnorm_fwd(h, z, g_post, g_pre, name="resnorm_fwd"):
    T, D = h.shape
    tm = _row_tile(T)

    def body(h_ref, z_ref, gp_ref, gn_ref, hn_ref, a_ref):
        zv = z_ref[...]
        hn = h_ref[...] + zv * _rms_rstd(zv) * gp_ref[...]
        hn_ref[...] = hn
        a_ref[...] = (hn * _rms_rstd(hn) * gn_ref[...]).astype(BF16)

    row = pl.BlockSpec((tm, D), lambda i: (i, 0))
    vec = pl.BlockSpec((1, D), lambda i: (0, 0))
    return pl.pallas_call(body, name=name, grid=(T // tm,), in_specs=[row, row, vec, vec],
                          out_specs=(row, row),
                          out_shape=(jax.ShapeDtypeStruct((T, D), F32), jax.ShapeDtypeStruct((T, D), BF16)),
                          compiler_params=_cparams("parallel"))(h, z, g_post, g_pre)


def _resnorm_loss(h, z, g_post, target, name="resnorm_loss"):
    T, D = h.shape
    tm = _row_tile(T)

    def body(h_ref, z_ref, gp_ref, t_ref, dy_ref, sq_ref):
        zv = z_ref[...]
        err = h_ref[...] + zv * _rms_rstd(zv) * gp_ref[...] - t_ref[...]
        dy_ref[...] = err * (1.0 / D)

        @pl.when(pl.program_id(0) == 0)
        def _():
            sq_ref[...] = jnp.zeros_like(sq_ref)

        sq_ref[...] += jnp.sum(err * err, axis=0, keepdims=True)

    row = pl.BlockSpec((tm, D), lambda i: (i, 0))
    vec = pl.BlockSpec((1, D), lambda i: (0, 0))
    return pl.pallas_call(body, name=name, grid=(T // tm,), in_specs=[row, row, vec, row],
                          out_specs=(row, vec),
                          out_shape=(jax.ShapeDtypeStruct((T, D), F32), jax.ShapeDtypeStruct((1, D), F32)),
                          compiler_params=_cparams("arbitrary"))(h, z, g_post, target)


def _resnorm_bwd(z, g_post, dh, h_new=None, da=None, g_pre=None, name="resnorm_bwd"):
    T, D = z.shape
    tm = _row_tile(T)
    has_next = h_new is not None
    row = pl.BlockSpec((tm, D), lambda i: (i, 0))
    vec = pl.BlockSpec((1, D), lambda i: (0, 0))

    if has_next:
        def body(z_ref, gp_ref, dh_ref, hn_ref, da_ref, gn_ref, t_ref, dz_ref, dgp_ref, dgn_ref):
            first = pl.program_id(0) == 0

            @pl.when(first)
            def _():
                dgp_ref[...] = jnp.zeros_like(dgp_ref)
                dgn_ref[...] = jnp.zeros_like(dgn_ref)

            dpre, dgn = _rms_bwd_tile(hn_ref[...], gn_ref[...], da_ref[...])
            t = dh_ref[...] + dpre
            t_ref[...] = t
            dz, dgp = _rms_bwd_tile(z_ref[...], gp_ref[...], t)
            dz_ref[...] = dz.astype(BF16)
            dgp_ref[...] += dgp
            dgn_ref[...] += dgn

        return pl.pallas_call(
            body, name=name, grid=(T // tm,), in_specs=[row, vec, row, row, row, vec],
            out_specs=(row, row, vec, vec),
            out_shape=(jax.ShapeDtypeStruct((T, D), F32), jax.ShapeDtypeStruct((T, D), BF16),
                       jax.ShapeDtypeStruct((1, D), F32), jax.ShapeDtypeStruct((1, D), F32)),
            compiler_params=_cparams("arbitrary"))(z, g_post, dh, h_new, da, g_pre)

    def body_last(z_ref, gp_ref, dh_ref, dz_ref, dgp_ref):
        @pl.when(pl.program_id(0) == 0)
        def _():
            dgp_ref[...] = jnp.zeros_like(dgp_ref)

        dz, dgp = _rms_bwd_tile(z_ref[...], gp_ref[...], dh_ref[...])
        dz_ref[...] = dz.astype(BF16)
        dgp_ref[...] += dgp

    return pl.pallas_call(
        body_last, name=name, grid=(T // tm,), in_specs=[row, vec, row], out_specs=(row, vec),
        out_shape=(jax.ShapeDtypeStruct((T, D), BF16), jax.ShapeDtypeStruct((1, D), F32)),
        compiler_params=_cparams("arbitrary"))(z, g_post, dh)


def _prenorm_bwd(x, g, dh, da, name="prenorm_bwd"):
    T, D = x.shape
    tm = _row_tile(T)

    def body(x_ref, g_ref, dh_ref, da_ref, dx_ref, dg_ref):
        @pl.when(pl.program_id(0) == 0)
        def _():
            dg_ref[...] = jnp.zeros_like(dg_ref)

        dpre, dg = _rms_bwd_tile(x_ref[...], g_ref[...], da_ref[...])
        dx_ref[...] = dh_ref[...] + dpre
        dg_ref[...] += dg

    row = pl.BlockSpec((tm, D), lambda i: (i, 0))
    vec = pl.BlockSpec((1, D), lambda i: (0, 0))
    return pl.pallas_call(
        body, name=name, grid=(T // tm,), in_specs=[row, vec, row, row], out_specs=(row, vec),
        out_shape=(jax.ShapeDtypeStruct((T, D), F32), jax.ShapeDtypeStruct((1, D), F32)),
        compiler_params=_cparams("arbitrary"))(x, g, dh, da)


class _Packed:
    def __init__(self, big, misc_rows):
        self.big = tuple(big)
        self.off = {}
        r = 0
        for name, rows in big:
            self.off[name] = r
            r += rows
        self.misc, self.misc_rows = r, misc_rows
        self.rows = -(-(r + misc_rows) // PACK_ALIGN) * PACK_ALIGN

    def block(self, name, layer, unit):
        r = self.off[name]
        assert r % unit == 0 and self.rows % unit == 0
        return r // unit, self.rows // unit


def _col_sharded(pk, name, layer, unit):
    base, stride = pk.block(name, layer, unit)
    return (lambda i, j, k: (j * stride + base, 0)), (lambda i, j, k: (k * stride + base, 0))


def _row_sharded(pk, name, layer, unit):
    base, stride = pk.block(name, layer, unit)
    return ((lambda i, j, k: (k * stride + base, 0)), (lambda i, j, k: (j * stride + base, 0)),
            (lambda i, j, k: (i * stride + base, 0)))


def _mlp_fwd(a, wbuf, pk, layer):
    D = a.shape[1]
    by_n, _ = _col_sharded(pk, "mlp_w1", layer, D)
    by_k, _, _ = _row_sharded(pk, "mlp_w2", layer, D)
    act, r = _mm(a, wbuf, n=4 * D, b_map=by_n, tk=D, tn=D, epi="relu2", name="mlp_up")
    u = _mm(act, wbuf, n=D, b_map=by_k, tk=D, tn=D, name="mlp_down")
    return u, (a, act, r)


def _mlp_bwd(du, saved, wbuf, gbuf, pk, layer):
    a, act, r = saved
    D = a.shape[1]
    w1_by_n, w1_by_k = _col_sharded(pk, "mlp_w1", layer, D)
    _, w2_by_n, w2_by_m = _row_sharded(pk, "mlp_w2", layer, D)
    dz1 = _mm(du, wbuf, tb=True, n=4 * D, b_map=w2_by_n, tn=D, tk=D, epi="mul2r", extra=r, name="mlp_down_dx")
    gbuf = _mm(act, du, ta=True, into=gbuf, o_map=w2_by_m, tm=D, tn=D, name="mlp_down_dw")
    gbuf = _mm(a, dz1, ta=True, into=gbuf, o_map=w1_by_n, tm=D, tn=D, name="mlp_up_dw")
    da = _mm(dz1, wbuf, tb=True, n=D, b_map=w1_by_k, tn=D, tk=D, name="mlp_up_dx")
    return da, gbuf


def _rope_swap(t):
    n = t.shape[-1]
    lane = lax.broadcasted_iota(jnp.int32, t.shape, t.ndim - 1)
    half = MLA_ROPE // 2
    first = (lane & (MLA_ROPE - 1)) < half
    return jnp.where(first, pltpu.roll(t, n - half, t.ndim - 1), pltpu.roll(t, half, t.ndim - 1))


def _mla_mid_fwd(proj, q_norm, kv_norm, w_uq, w_ukv, cc, ss):
    T, PW = proj.shape
    QL, KVL = q_norm.shape[-1], kv_norm.shape[-1]
    H = MLA_HEADS
    assert PW == QL + KVL + 128
    tm = _row_tile(T)

    def body(p_ref, qn_ref, kn_ref, wq_ref, wkv_ref, cc_ref, ss_ref,
             cq_ref, ckv_ref, q_ref, k_ref, v_ref):
        cq = p_ref[:, 0:QL]
        ckv = p_ref[:, QL:QL + KVL]
        kr = p_ref[:, QL + KVL:QL + KVL + 128]
        c, s = cc_ref[...], ss_ref[...]
        cqn = (cq * _rms_rstd(cq) * qn_ref[...]).astype(BF16)
        ckvn = (ckv * _rms_rstd(ckv) * kn_ref[...]).astype(BF16)
        cq_ref[...] = cqn
        ckv_ref[...] = ckvn
        q = jnp.dot(cqn, wq_ref[...], preferred_element_type=F32)
        kv = jnp.dot(ckvn, wkv_ref[...], preferred_element_type=F32)
        krf = (kr * c + _rope_swap(kr) * s).astype(BF16)
        for h in range(H):
            o = h * MLA_QK_PAD
            q_ref[:, o:o + MLA_NOPE] = q[:, o:o + MLA_NOPE].astype(BF16)
            qr = q[:, o + MLA_NOPE:o + MLA_QK_PAD]
            q_ref[:, o + MLA_NOPE:o + MLA_QK_PAD] = (qr * c + _rope_swap(qr) * s).astype(BF16)
            k_ref[:, o:o + MLA_NOPE] = kv[:, o:o + MLA_NOPE].astype(BF16)
            k_ref[:, o + MLA_NOPE:o + MLA_QK_PAD] = krf
            v_ref[:, h * MLA_V:(h + 1) * MLA_V] = kv[:, o + MLA_NOPE:o + MLA_QK_PAD].astype(BF16)

    def row(w):
        return pl.BlockSpec((tm, w), lambda i: (i, 0))

    def full(shape):
        return pl.BlockSpec(shape, lambda i: (0, 0))

    return pl.pallas_call(
        body, name="mla_mid_fwd", grid=(T // tm,),
        in_specs=[row(PW), full((1, QL)), full((1, KVL)), full(w_uq.shape), full(w_ukv.shape),
                  row(128), row(128)],
        out_specs=(row(QL), row(KVL), row(H * MLA_QK_PAD), row(H * MLA_QK_PAD), row(H * MLA_V)),
        out_shape=(jax.ShapeDtypeStruct((T, QL), BF16), jax.ShapeDtypeStruct((T, KVL), BF16),
                   jax.ShapeDtypeStruct((T, H * MLA_QK_PAD), BF16),
                   jax.ShapeDtypeStruct((T, H * MLA_QK_PAD), BF16),
                   jax.ShapeDtypeStruct((T, H * MLA_V), BF16)),
        compiler_params=_cparams("parallel"))(proj, q_norm, kv_norm, w_uq, w_ukv, cc, ss)


def _mla_mid_bwd(proj, q_norm, kv_norm, w_uq, w_ukv, cc, ss, dq, dk, dv):
    T, PW = proj.shape
    QL, KVL = q_norm.shape[-1], kv_norm.shape[-1]
    H = MLA_HEADS
    tm = _row_tile(T)
    nt = (((1,), (1,)), ((), ()))

    def body(p_ref, qn_ref, kn_ref, wq_ref, wkv_ref, cc_ref, ss_ref, dq_ref, dk_ref, dv_ref,
             dqp_ref, dkv_ref, dp_ref, dqn_ref, dkn_ref):
        @pl.when(pl.program_id(0) == 0)
        def _():
            dqn_ref[...] = jnp.zeros_like(dqn_ref)
            dkn_ref[...] = jnp.zeros_like(dkn_ref)

        c, s = cc_ref[...], ss_ref[...]
        dkr = jnp.zeros((tm, 128), F32)
        for h in range(H):
            o = h * MLA_QK_PAD
            dqp_ref[:, o:o + MLA_NOPE] = dq_ref[:, o:o + MLA_NOPE].astype(BF16)
            dqr = dq_ref[:, o + MLA_NOPE:o + MLA_QK_PAD]
            dqp_ref[:, o + MLA_NOPE:o + MLA_QK_PAD] = (dqr * c + _rope_swap(dqr * s)).astype(BF16)
            dkv_ref[:, o:o + MLA_NOPE] = dk_ref[:, o:o + MLA_NOPE].astype(BF16)
            dkv_ref[:, o + MLA_NOPE:o + MLA_QK_PAD] = dv_ref[:, h * MLA_V:(h + 1) * MLA_V].astype(BF16)
            dkr = dkr + dk_ref[:, o + MLA_NOPE:o + MLA_QK_PAD]
        dcqn = lax.dot_general(dqp_ref[...], wq_ref[...], nt, preferred_element_type=F32)
        dckvn = lax.dot_general(dkv_ref[...], wkv_ref[...], nt, preferred_element_type=F32)
        dcq, dqn = _rms_bwd_tile(p_ref[:, 0:QL], qn_ref[...], dcqn)
        dckv, dkn = _rms_bwd_tile(p_ref[:, QL:QL + KVL], kn_ref[...], dckvn)
        dp_ref[:, 0:QL] = dcq.astype(BF16)
        dp_ref[:, QL:QL + KVL] = dckv.astype(BF16)
        dp_ref[:, QL + KVL:QL + KVL + 128] = (dkr * c + _rope_swap(dkr * s)).astype(BF16)
        dqn_ref[...] += dqn
        dkn_ref[...] += dkn

    def row(w):
        return pl.BlockSpec((tm, w), lambda i: (i, 0))

    def full(shape):
        return pl.BlockSpec(shape, lambda i: (0, 0))

    return pl.pallas_call(
        body, name="mla_mid_bwd", grid=(T // tm,),
        in_specs=[row(PW), full((1, QL)), full((1, KVL)), full(w_uq.shape), full(w_ukv.shape),
                  row(128), row(128), row(H * MLA_QK_PAD), row(H * MLA_QK_PAD), row(H * MLA_V)],
        out_specs=(row(H * MLA_QK_PAD), row(H * MLA_QK_PAD), row(PW), full((1, QL)), full((1, KVL))),
        out_shape=(jax.ShapeDtypeStruct((T, H * MLA_QK_PAD), BF16),
                   jax.ShapeDtypeStruct((T, H * MLA_QK_PAD), BF16),
                   jax.ShapeDtypeStruct((T, PW), BF16),
                   jax.ShapeDtypeStruct((1, QL), F32), jax.ShapeDtypeStruct((1, KVL), F32)),
        compiler_params=_cparams("arbitrary"))(proj, q_norm, kv_norm, w_uq, w_ukv, cc, ss, dq, dk, dv)


def _attn_tile(T):
    return min(1024, T)


def _attn_pairs(n, by_key):
    if by_key:
        pairs = [(qi, ki) for ki in range(n) for qi in range(ki, n)]
    else:
        pairs = [(qi, ki) for qi in range(n) for ki in range(qi + 1)]
    return (jnp.asarray([p[0] for p in pairs], jnp.int32), jnp.asarray([p[1] for p in pairs], jnp.int32))


def _scores(q, k, scale, diagonal):
    s = lax.dot_general(q, k, (((1,), (1,)), ((), ())), preferred_element_type=F32) * scale
    if diagonal:
        rows = lax.broadcasted_iota(jnp.int32, s.shape, 0)
        cols = lax.broadcasted_iota(jnp.int32, s.shape, 1)
        s = jnp.where(rows >= cols, s, -jnp.inf)
    return s


def _attn_fwd(q, k, v):
    T = q.shape[0]
    H, DQ, DV = MLA_HEADS, MLA_QK_PAD, MLA_V
    tq = _attn_tile(T)
    nq = T // tq
    scale = float(MLA_NOPE + MLA_ROPE) ** -0.5
    G = MLA_HEADS_PER_STEP
    qi_tab, ki_tab = _attn_pairs(nq, by_key=False)

    def body(qi_ref, ki_ref, q_ref, k_ref, v_ref, o_ref, lse_ref, *scratch):
        m_refs, l_refs, acc_refs = scratch[0:G], scratch[G:2 * G], scratch[2 * G:3 * G]
        p = pl.program_id(1)
        qi, ki = qi_ref[p], ki_ref[p]

        @pl.when(ki == 0)
        def _():
            for g in range(G):
                m_refs[g][...] = jnp.full_like(m_refs[g], -jnp.inf)
                l_refs[g][...] = jnp.zeros_like(l_refs[g])
                acc_refs[g][...] = jnp.zeros_like(acc_refs[g])

        def update(diagonal):
            for g in range(G):
                qs, vs = slice(g * DQ, (g + 1) * DQ), slice(g * DV, (g + 1) * DV)
                s = _scores(q_ref[:, qs], k_ref[:, qs], scale, diagonal)
                m_prev = m_refs[g][...]
                m_new = jnp.maximum(m_prev, jnp.max(s, axis=1, keepdims=True))
                alpha = jnp.exp(m_prev - m_new)
                pr = jnp.exp(s - m_new)
                l_refs[g][...] = alpha * l_refs[g][...] + jnp.sum(pr, axis=1, keepdims=True)
                acc_refs[g][...] = alpha * acc_refs[g][...] + jnp.dot(pr.astype(BF16), v_ref[:, vs],
                                                                      preferred_element_type=F32)
                m_refs[g][...] = m_new

        @pl.when(ki < qi)
        def _():
            update(False)

        @pl.when(ki == qi)
        def _():
            update(True)
            for g in range(G):
                vs = slice(g * DV, (g + 1) * DV)
                o_ref[:, vs] = (acc_refs[g][...] / l_refs[g][...]).astype(BF16)
                lse_ref[g] = m_refs[g][...] + jnp.log(l_refs[g][...])

    return pl.pallas_call(
        body, name="attn_fwd",
        grid_spec=pltpu.PrefetchScalarGridSpec(
            num_scalar_prefetch=2, grid=(H // G, int(qi_tab.shape[0])),
            in_specs=[pl.BlockSpec((tq, G * DQ), lambda h, p, qt, kt: (qt[p], h)),
                      pl.BlockSpec((tq, G * DQ), lambda h, p, qt, kt: (kt[p], h)),
                      pl.BlockSpec((tq, G * DV), lambda h, p, qt, kt: (kt[p], h))],
            out_specs=(pl.BlockSpec((tq, G * DV), lambda h, p, qt, kt: (qt[p], h)),
                       pl.BlockSpec((G, tq, 1), lambda h, p, qt, kt: (h, qt[p], 0))),
            scratch_shapes=([pltpu.VMEM((tq, 1), F32)] * (2 * G) + [pltpu.VMEM((tq, DV), F32)] * G)),
        out_shape=(jax.ShapeDtypeStruct((T, H * DV), BF16), jax.ShapeDtypeStruct((H, T, 1), F32)),
        compiler_params=_cparams("parallel", "arbitrary"))(qi_tab, ki_tab, q, k, v)


def _attn_bwd(q, k, v, o, do, lse):
    T = q.shape[0]
    H, DQ, DV = MLA_HEADS, MLA_QK_PAD, MLA_V
    tq = _attn_tile(T)
    nq = T // tq
    scale = float(MLA_NOPE + MLA_ROPE) ** -0.5
    tn = (((0,), (0,)), ((), ()))
    nt = (((1,), (1,)), ((), ()))
    G = MLA_HEADS_PER_STEP
    qi_tab, ki_tab = _attn_pairs(nq, by_key=True)

    def body(qi_ref, ki_ref, q_ref, k_ref, v_ref, o_ref, do_ref, lse_ref, dq_ref, dk_ref, dv_ref,
             dk_acc, dv_acc):
        p = pl.program_id(1)
        qi, ki = qi_ref[p], ki_ref[p]

        @pl.when(p == 0)
        def _():
            dq_ref[...] = jnp.zeros_like(dq_ref)

        @pl.when(qi == ki)
        def _():
            dk_acc[...] = jnp.zeros_like(dk_acc)
            dv_acc[...] = jnp.zeros_like(dv_acc)

        def step(diagonal):
            rows = pl.ds(pl.multiple_of(qi * tq, tq), tq)
            for g in range(G):
                qs, vs = slice(g * DQ, (g + 1) * DQ), slice(g * DV, (g + 1) * DV)
                dof = do_ref[:, vs]
                delta = jnp.sum(dof.astype(F32) * o_ref[:, vs].astype(F32), axis=1, keepdims=True)
                s = _scores(q_ref[:, qs], k_ref[:, qs], scale, diagonal)
                pr = jnp.exp(s - lse_ref[g])
                dp = lax.dot_general(dof, v_ref[:, vs], nt, preferred_element_type=F32)
                ds = (pr * (dp - delta) * scale).astype(BF16)
                dv_acc[:, vs] += lax.dot_general(pr.astype(BF16), dof, tn, preferred_element_type=F32)
                dk_acc[:, qs] += lax.dot_general(ds, q_ref[:, qs], tn, preferred_element_type=F32)
                dq_ref[rows, qs] += jnp.dot(ds, k_ref[:, qs], preferred_element_type=F32)

        @pl.when(qi == ki)
        def _():
            step(True)

        @pl.when(qi > ki)
        def _():
            step(False)

        @pl.when(qi == nq - 1)
        def _():
            dk_ref[...] = dk_acc[...]
            dv_ref[...] = dv_acc[...]

    qspec = pl.BlockSpec((tq, G * DQ), lambda h, p, qt, kt: (qt[p], h))
    ospec = pl.BlockSpec((tq, G * DV), lambda h, p, qt, kt: (qt[p], h))
    kspec = pl.BlockSpec((tq, G * DQ), lambda h, p, qt, kt: (kt[p], h))
    vspec = pl.BlockSpec((tq, G * DV), lambda h, p, qt, kt: (kt[p], h))
    return pl.pallas_call(
        body, name="attn_bwd",
        grid_spec=pltpu.PrefetchScalarGridSpec(
            num_scalar_prefetch=2, grid=(H // G, int(qi_tab.shape[0])),
            in_specs=[qspec, kspec, vspec, ospec, ospec,
                      pl.BlockSpec((G, tq, 1), lambda h, p, qt, kt: (h, qt[p], 0))],
            out_specs=(pl.BlockSpec((T, G * DQ), lambda h, p, qt, kt: (0, h)), kspec, vspec),
            scratch_shapes=[pltpu.VMEM((tq, G * DQ), F32), pltpu.VMEM((tq, G * DV), F32)]),
        out_shape=(jax.ShapeDtypeStruct((T, H * DQ), F32), jax.ShapeDtypeStruct((T, H * DQ), F32),
                   jax.ShapeDtypeStruct((T, H * DV), F32)),
        compiler_params=_cparams("parallel", "arbitrary"))(qi_tab, ki_tab, q, k, v, o, do, lse)


def _mla_fwd(a, w, cc, ss, wbuf, pk, slot):
    D = a.shape[1]
    by_k, _, _ = _row_sharded(pk, "mla_w_o", slot, D // N_CHIPS)
    proj = _mm(a, w["w_in"], name="mla_in")
    cqn, ckvn, q, k, v = _mla_mid_fwd(proj, w["q_norm"], w["kv_norm"], w["w_uq"], w["w_ukv"], cc, ss)
    o, lse = _attn_fwd(q, k, v)
    m = _mm(o, wbuf, n=D, b_map=by_k, tm=2048, tk=D // N_CHIPS, tn=D, name="mla_out")
    return m, (a, proj, cqn, ckvn, q, k, v, o, lse)


def _mla_bwd(dm, saved, w, cc, ss, wbuf, gbuf, pk, slot):
    a, proj, cqn, ckvn, q, k, v, o, lse = saved
    D = a.shape[1]
    _, by_n, by_m = _row_sharded(pk, "mla_w_o", slot, D // N_CHIPS)
    do = _mm(dm, wbuf, tb=True, n=o.shape[1], b_map=by_n, tm=2048, tn=D // N_CHIPS, tk=D, out_dtype=BF16,
             name="mla_out_dx")
    gbuf = _mm(o, dm, ta=True, into=gbuf, o_map=by_m, tm=D // N_CHIPS, tn=D, tk=2048, name="mla_out_dw")
    dq, dk, dv = _attn_bwd(q, k, v, o, do, lse)
    dqp, dkv, dproj, dqn, dkn = _mla_mid_bwd(proj, w["q_norm"], w["kv_norm"], w["w_uq"], w["w_ukv"],
                                             cc, ss, dq, dk, dv)
    dw_uq = _mm(cqn, dqp, ta=True, out_dtype=BF16, name="mla_uq_dw")
    dw_ukv = _mm(ckvn, dkv, ta=True, out_dtype=BF16, name="mla_ukv_dw")
    dw_in = _mm(a, dproj, ta=True, out_dtype=BF16, name="mla_in_dw")
    da = _mm(dproj, w["w_in"], tb=True, name="mla_in_dx")
    return da, gbuf, dict(w_in=dw_in, w_uq=dw_uq, w_ukv=dw_ukv, q_norm=dqn, kv_norm=dkn)


def _split_dot(mat, x, parts):
    acc = None
    rem = x
    for _ in range(parts):
        piece = rem.astype(BF16)
        term = jnp.dot(mat, piece, preferred_element_type=F32)
        acc = term if acc is None else acc + term
        rem = rem - piece.astype(F32)
    return acc


def _chunk_mats(tb):
    C = HGRN_CHUNK
    assert C & (C - 1) == 0
    r = lax.broadcasted_iota(jnp.int32, (tb, tb), 0)
    s = lax.broadcasted_iota(jnp.int32, (tb, tb), 1)
    start = r & ~(C - 1)
    same = start == (s & ~(C - 1))
    ref = start + C // 2
    last = start + C - 1
    one, zero = jnp.float32(1.0), jnp.float32(0.0)
    cum = jnp.where(same & (s <= r), one, zero)
    rel = cum - jnp.where(same & (s <= ref), one, zero)
    rest = jnp.where(same & (s > r) & (s <= last), one, zero)
    rev = jnp.where(same & (s >= r), one, zero)
    ones = jnp.where(same, one, zero)
    causal = same & (s <= r)
    return cum, rel, rest, rev, ones, causal


def _hgrn_gates(p_ref, lb, HK):
    qx = p_ref[:, 0:HK]
    fx = p_ref[:, HK:2 * HK]
    sf = _sigmoid(fx)
    f = lb + (1.0 - lb) * sf
    sq = _sigmoid(qx)
    return qx, sq, qx * sq, sf, f, 1.0 - f, jnp.log(f)


def _hgrn_fwd(proj, lb, o_norm):
    T = proj.shape[0]
    H, C = HGRN_HEADS, HGRN_CHUNK
    HK = proj.shape[1] // 4
    DK = HK // H
    tb = min(HGRN_BLOCK, T)
    ncb = tb // C
    nt = (((1,), (1,)), ((), ()))
    tn = (((0,), (0,)), ((), ()))

    def body(p_ref, lb_ref, on_ref, y_ref, o_ref, st_ref, state, oacc):
        @pl.when(pl.program_id(0) == 0)
        def _():
            state[...] = jnp.zeros_like(state)

        cum, rel, rest, _, _, causal = _chunk_mats(tb)
        _, _, q, _, f, k, logf = _hgrn_gates(p_ref, lb_ref[...], HK)
        b = _split_dot(cum.astype(BF16), logf, 3)
        brel = _split_dot(rel.astype(BF16), logf, 3)
        brest = _split_dot(rest.astype(BF16), logf, 3)
        eb = jnp.exp(b)
        q_rel = (q * jnp.exp(brel)).astype(BF16)
        k_rel = (k * jnp.exp(-brel)).astype(BF16)
        q_dec = (q * eb).astype(BF16)
        k_dec = (k * jnp.exp(brest)).astype(BF16)
        v = p_ref[:, 2 * HK:3 * HK].astype(BF16)
        for h in range(H):
            hs = slice(h * DK, (h + 1) * DK)
            a = lax.dot_general(q_rel[:, hs], k_rel[:, hs], nt, preferred_element_type=F32)
            a = jnp.where(causal, a, 0.0).astype(BF16)
            oacc[:, hs] = jnp.dot(a, v[:, hs], preferred_element_type=F32)
            for j in range(ncb):
                rs = slice(j * C, (j + 1) * C)
                st = state[h]
                st_ref[j, h] = st
                oacc[rs, hs] += lax.dot_general(q_dec[rs, hs], st.astype(BF16), nt,
                                                preferred_element_type=F32)
                dec = jnp.exp(jnp.sum(logf[rs, hs], axis=0, keepdims=True))
                state[h] = dec * st + lax.dot_general(v[rs, hs], k_dec[rs, hs], tn,
                                                      preferred_element_type=F32)
        o = oacc[...]
        o_ref[...] = o
        gx = p_ref[:, 3 * HK:4 * HK]
        gate = gx * _sigmoid(gx)
        for h in range(H):
            hs = slice(h * DK, (h + 1) * DK)
            oh = o[:, hs]
            y_ref[:, hs] = (oh * _rms_rstd(oh) * on_ref[...] * gate[:, hs]).astype(BF16)

    return pl.pallas_call(
        body, name="hgrn_fwd", grid=(T // tb,),
        in_specs=[pl.BlockSpec((tb, 4 * HK), lambda i: (i, 0)),
                  pl.BlockSpec((1, HK), lambda i: (0, 0)),
                  pl.BlockSpec((1, DK), lambda i: (0, 0))],
        out_specs=(pl.BlockSpec((tb, HK), lambda i: (i, 0)),
                   pl.BlockSpec((tb, HK), lambda i: (i, 0)),
                   pl.BlockSpec((ncb, H, DK, DK), lambda i: (i, 0, 0, 0))),
        out_shape=(jax.ShapeDtypeStruct((T, HK), BF16), jax.ShapeDtypeStruct((T, HK), F32),
                   jax.ShapeDtypeStruct((T // C, H, DK, DK), F32)),
        scratch_shapes=[pltpu.VMEM((H, DK, DK), F32), pltpu.VMEM((tb, HK), F32)],
        compiler_params=_cparams("arbitrary"))(proj, lb, o_norm)


def _hgrn_bwd(proj, lb, o_norm, o, states, dy):
    T = proj.shape[0]
    H, C = HGRN_HEADS, HGRN_CHUNK
    HK = proj.shape[1] // 4
    DK = HK // H
    tb = min(HGRN_BLOCK, T)
    ncb = tb // C
    nb = T // tb
    nt = (((1,), (1,)), ((), ()))
    tn = (((0,), (0,)), ((), ()))

    def body(p_ref, lb_ref, on_ref, o_ref, st_ref, dy_ref, dp_ref, dlb_ref, don_ref,
             dstate, dqr_s, dkr_s, dqd_s, dkd_s, dv_s, do_s, e_s):
        @pl.when(pl.program_id(0) == 0)
        def _():
            dstate[...] = jnp.zeros_like(dstate)
            dlb_ref[...] = jnp.zeros_like(dlb_ref)
            don_ref[...] = jnp.zeros_like(don_ref)

        cum, rel, rest, rev, ones, causal = _chunk_mats(tb)
        lb = lb_ref[...]
        qx, sq, q, sf, f, k, logf = _hgrn_gates(p_ref, lb, HK)
        b = _split_dot(cum.astype(BF16), logf, 3)
        brel = _split_dot(rel.astype(BF16), logf, 3)
        brest = _split_dot(rest.astype(BF16), logf, 3)
        eb = jnp.exp(b)
        erel = jnp.exp(brel)
        enrel = jnp.exp(-brel)
        erest = jnp.exp(brest)
        q_rel_f, k_rel_f, q_dec_f, k_dec_f = q * erel, k * enrel, q * eb, k * erest
        q_rel, k_rel = q_rel_f.astype(BF16), k_rel_f.astype(BF16)
        q_dec, k_dec = q_dec_f.astype(BF16), k_dec_f.astype(BF16)
        v = p_ref[:, 2 * HK:3 * HK].astype(BF16)

        gx = p_ref[:, 3 * HK:4 * HK]
        sg = _sigmoid(gx)
        gate = gx * sg
        dy = dy_ref[...]
        ov = o_ref[...]
        on = on_ref[...]
        don = jnp.zeros((1, DK), F32)
        for h in range(H):
            hs = slice(h * DK, (h + 1) * DK)
            oh = ov[:, hs]
            r = _rms_rstd(oh)
            xh = oh * r
            d_on = dy[:, hs] * gate[:, hs]
            don = don + jnp.sum(d_on * xh, axis=0, keepdims=True)
            u = d_on * on
            do_s[:, hs] = r * (u - xh * jnp.mean(u * xh, axis=-1, keepdims=True))
            dp_ref[:, 3 * HK + h * DK:3 * HK + (h + 1) * DK] = (
                dy[:, hs] * xh * on * (sg[:, hs] * (1.0 + gx[:, hs] * (1.0 - sg[:, hs])))).astype(BF16)
        don_ref[...] += don

        for h in range(H):
            hs = slice(h * DK, (h + 1) * DK)
            doh = do_s[:, hs].astype(BF16)
            a = lax.dot_general(q_rel[:, hs], k_rel[:, hs], nt, preferred_element_type=F32)
            a = jnp.where(causal, a, 0.0).astype(BF16)
            da = lax.dot_general(doh, v[:, hs], nt, preferred_element_type=F32)
            da = jnp.where(causal, da, 0.0).astype(BF16)
            dv_s[:, hs] = lax.dot_general(a, doh, tn, preferred_element_type=F32)
            dqr_s[:, hs] = jnp.dot(da, k_rel[:, hs], preferred_element_type=F32)
            dkr_s[:, hs] = lax.dot_general(da, q_rel[:, hs], tn, preferred_element_type=F32)
            for j in reversed(range(ncb)):
                rs = slice(j * C, (j + 1) * C)
                dst = dstate[h]
                dstb = dst.astype(BF16)
                st = st_ref[j, h]
                dkd_s[rs, hs] = jnp.dot(v[rs, hs], dstb, preferred_element_type=F32)
                dv_s[rs, hs] += lax.dot_general(k_dec[rs, hs], dstb, nt, preferred_element_type=F32)
                dec = jnp.exp(jnp.sum(logf[rs, hs], axis=0, keepdims=True))
                e_s[rs, hs] = jnp.broadcast_to(jnp.sum(dst * st, axis=0, keepdims=True) * dec, (C, DK))
                dqd_s[rs, hs] = jnp.dot(doh[rs], st.astype(BF16), preferred_element_type=F32)
                dstate[h] = dec * dst + lax.dot_general(doh[rs], q_dec[rs, hs], tn,
                                                        preferred_element_type=F32)

        dqr, dkr, dqd, dkd = dqr_s[...], dkr_s[...], dqd_s[...], dkd_s[...]
        kdk = dkd * k_dec_f
        db = dqr * q_rel_f - dkr * k_rel_f + dqd * q_dec_f - kdk
        dlogf = _split_dot(rev.astype(BF16), db, 2) + _split_dot(ones.astype(BF16), kdk, 2) + e_s[...]
        dk = dkr * enrel + dkd * erest
        df = dlogf / f - dk
        dlb_ref[...] += jnp.sum(df * (1.0 - sf), axis=0, keepdims=True)
        dq = dqr * erel + dqd * eb
        dp_ref[:, 0:HK] = (dq * (sq * (1.0 + qx * (1.0 - sq)))).astype(BF16)
        dp_ref[:, HK:2 * HK] = (df * (1.0 - lb) * sf * (1.0 - sf)).astype(BF16)
        dp_ref[:, 2 * HK:3 * HK] = dv_s[...].astype(BF16)

    rev_row = lambda w: pl.BlockSpec((tb, w), lambda i: (nb - 1 - i, 0))
    vec = lambda w: pl.BlockSpec((1, w), lambda i: (0, 0))
    scr = pltpu.VMEM((tb, HK), F32)
    return pl.pallas_call(
        body, name="hgrn_bwd", grid=(nb,),
        in_specs=[rev_row(4 * HK), vec(HK), vec(DK), rev_row(HK),
                  pl.BlockSpec((ncb, H, DK, DK), lambda i: (nb - 1 - i, 0, 0, 0)), rev_row(HK)],
        out_specs=(rev_row(4 * HK), vec(HK), vec(DK)),
        out_shape=(jax.ShapeDtypeStruct((T, 4 * HK), BF16), jax.ShapeDtypeStruct((1, HK), F32),
                   jax.ShapeDtypeStruct((1, DK), F32)),
        scratch_shapes=[pltpu.VMEM((H, DK, DK), F32), scr, scr, scr, scr, scr, scr, scr],
        compiler_params=_cparams("arbitrary"))(proj, lb, o_norm, o, states, dy)


def _hgrn_layer_fwd(a, o_norm, lb, wbuf, pk, slot):
    D = a.shape[1]
    in_by_n, _ = _col_sharded(pk, "hgrn_w_in", slot, D)
    out_by_k, _, _ = _row_sharded(pk, "hgrn_w_o", slot, D // N_CHIPS)
    proj = _mm(a, wbuf, n=4 * D, b_map=in_by_n, tk=D, tn=D, name="hgrn_in")
    y, o, states = _hgrn_fwd(proj, lb, o_norm)
    m = _mm(y, wbuf, n=D, b_map=out_by_k, tm=2048, tk=D // N_CHIPS, tn=D, name="hgrn_out")
    return m, (a, proj, y, o, states)


def _hgrn_layer_bwd(dm, saved, o_norm, lb, wbuf, gbuf, pk, slot):
    a, proj, y, o, states = saved
    D = a.shape[1]
    in_by_n, in_by_k = _col_sharded(pk, "hgrn_w_in", slot, D)
    _, out_by_n, out_by_m = _row_sharded(pk, "hgrn_w_o", slot, D // N_CHIPS)
    dy = _mm(dm, wbuf, tb=True, n=y.shape[1], b_map=out_by_n, tm=2048, tn=D // N_CHIPS, tk=D,
             name="hgrn_out_dx")
    gbuf = _mm(y, dm, ta=True, into=gbuf, o_map=out_by_m, tm=D // N_CHIPS, tn=D, tk=2048, name="hgrn_out_dw")
    dproj, dlb, don = _hgrn_bwd(proj, lb, o_norm, o, states, dy)
    gbuf = _mm(a, dproj, ta=True, into=gbuf, o_map=in_by_n, tm=D, tn=D, name="hgrn_in_dw")
    da = _mm(dproj, wbuf, tb=True, n=D, b_map=in_by_k, tn=D, tk=D, name="hgrn_in_dx")
    return da, gbuf, dict(o_norm=don, lb=dlb)


def _lower_bounds(lb_logits):
    p = jax.nn.softmax(lb_logits.astype(F32), axis=0)
    return jnp.cumsum(p, axis=0) - p[0]


def _rope_tables(positions):
    inv_freq = jnp.power(ROPE_BASE, -jnp.arange(0, MLA_ROPE, 2, dtype=F32) / MLA_ROPE)
    ang = positions.astype(F32)[:, None] * inv_freq
    cos, sin = jnp.cos(ang), jnp.sin(ang)
    zero = jnp.zeros((positions.shape[0], 128 - MLA_ROPE), F32)
    return (jnp.concatenate([cos, cos, zero], axis=-1), jnp.concatenate([-sin, sin, zero], axis=-1))


def _pad_mla_weights(w_in, w_uq):
    w_in_p = jnp.pad(w_in, ((0, 0), (0, 0), (0, 128 - MLA_ROPE)))
    n, ql, _ = w_uq.shape
    w_uq_p = jnp.pad(w_uq.reshape(n, ql, MLA_HEADS, MLA_NOPE + MLA_ROPE),
                     ((0, 0), (0, 0), (0, 0), (0, MLA_QK_PAD - MLA_NOPE - MLA_ROPE)))
    return w_in_p, w_uq_p.reshape(n, ql, MLA_HEADS * MLA_QK_PAD)


def _local_step(x, positions, target, small, fetch, gbufs, emit, emit_mlp):
    T, D = x.shape
    lbounds, lb_vjp = jax.vjp(_lower_bounds, small["hgrn_lb_logits"])
    cc, ss = _rope_tables(positions)
    fetched = {0: fetch(0, None)}
    gains = fetched[0]["gains"]
    tick = [jnp.zeros((), F32)]

    def g(layer, i):
        return gains[layer, i][None, :] + tick[0]

    def mla_weights(layer):
        f = fetched[layer]
        w_in_p, w_uq_p = _pad_mla_weights(f["w_in"][None], f["w_uq"][None])
        slot = layer // 2
        return dict(w_in=w_in_p[0], w_uq=w_uq_p[0], w_ukv=f["w_ukv"],
                    q_norm=small["mla_q_norm"][slot][None, :], kv_norm=small["mla_kv_norm"][slot][None, :])

    saved = []
    h = x
    a = _prenorm_fwd(x, g(0, 0))
    dy = sq = None
    for layer in range(DEPTH):
        slot = layer // 2
        if layer not in fetched:
            fetched[layer] = fetch(layer, a)
        wbuf, pk = fetched[layer]["wbuf"], fetched[layer]["pk"]
        if layer % 2 == 0:
            m, mix_saved = _mla_fwd(a, mla_weights(layer), cc, ss, wbuf, pk, slot)
        else:
            m, mix_saved = _hgrn_layer_fwd(a, small["hgrn_o_norm"][slot][None, :], lbounds[layer][None, :],
                                           wbuf, pk, slot)
        h1, a2 = _resnorm_fwd(h, m, g(layer, 1), g(layer, 2), name="resnorm_fwd_mix")
        u, mlp_saved = _mlp_fwd(a2, wbuf, pk, layer)
        if layer + 1 < DEPTH:
            h2, a = _resnorm_fwd(h1, u, g(layer, 3), g(layer + 1, 0), name="resnorm_fwd_mlp")
        else:
            h2 = None
            dy, sq = _resnorm_loss(h1, u, g(layer, 3), target)
        saved.append((h, m, h1, u, mix_saved, mlp_saved))
        h = h2

    n_mla, n_hgrn = (DEPTH + 1) // 2, DEPTH // 2
    dgains = [[None] * 4 for _ in range(DEPTH)]
    gw = {k: [None] * n_mla for k in ("mla_w_in", "mla_w_uq", "mla_w_ukv", "mla_q_norm", "mla_kv_norm")}
    gw["hgrn_o_norm"] = [None] * n_hgrn
    dlb = [jnp.zeros((1, lbounds.shape[1]), F32) for _ in range(DEPTH)]
    dh = dy
    da_next = None
    for layer in reversed(range(DEPTH)):
        h0, m, h1, u, mix_saved, mlp_saved = saved[layer]
        slot = layer // 2
        wbuf, pk, gbuf = fetched[layer]["wbuf"], fetched[layer]["pk"], gbufs[layer]
        if da_next is None:
            du, dgains[layer][3] = _resnorm_bwd(u, g(layer, 3), dh, name="resnorm_bwd_last")
            t = dh
        else:
            h2 = saved[layer + 1][0]
            t, du, dgains[layer][3], dgains[layer + 1][0] = _resnorm_bwd(
                u, g(layer, 3), dh, h2, da_next, g(layer + 1, 0), name="resnorm_bwd_mlp")
        da2, gbuf = _mlp_bwd(du, mlp_saved, wbuf, gbuf, pk, layer)
        if layer == 0:
            gbuf = emit_mlp(layer, gbuf)
        t, dm, dgains[layer][1], dgains[layer][2] = _resnorm_bwd(
            m, g(layer, 1), t, h1, da2, g(layer, 2), name="resnorm_bwd_mix")
        if layer % 2 == 0:
            da_next, gbuf, mg = _mla_bwd(dm, mix_saved, mla_weights(layer), cc, ss, wbuf, gbuf, pk, slot)
            ql = mg["q_norm"].shape[-1]
            kvl = mg["kv_norm"].shape[-1]
            gw["mla_w_in"][slot] = mg["w_in"][:, :ql + kvl + MLA_ROPE]
            gw["mla_w_uq"][slot] = mg["w_uq"].reshape(ql, MLA_HEADS, MLA_QK_PAD)[
                :, :, :MLA_NOPE + MLA_ROPE].reshape(ql, MLA_HEADS * (MLA_NOPE + MLA_ROPE))
            gw["mla_w_ukv"][slot] = mg["w_ukv"]
            gw["mla_q_norm"][slot] = mg["q_norm"][0]
            gw["mla_kv_norm"][slot] = mg["kv_norm"][0]
        else:
            da_next, gbuf, hg = _hgrn_layer_bwd(dm, mix_saved, small["hgrn_o_norm"][slot][None, :],
                                                lbounds[layer][None, :], wbuf, gbuf, pk, slot)
            gw["hgrn_o_norm"][slot] = hg["o_norm"][0]
            dlb[layer] = hg["lb"]
        dh = t
        if layer > 0:
            mine = ({k: gw[k][slot] for k in ("mla_w_in", "mla_w_uq", "mla_w_ukv")} if layer % 2 == 0 else {})
            tick[0] = emit(layer, gbuf, mine)
        else:
            gbuf0 = gbuf
    grad_x, dgains[0][0] = _prenorm_bwd(x, g(0, 0), dh, da_next)

    last = {k: gw[k][0] for k in ("mla_w_in", "mla_w_uq", "mla_w_ukv")}
    last.update({k: jnp.stack(gw[k]) for k in ("mla_q_norm", "mla_kv_norm", "hgrn_o_norm")})
    last["norm_gains"] = jnp.stack([jnp.concatenate(row, axis=0) for row in dgains])
    (last["hgrn_lb_logits"],) = lb_vjp(jnp.concatenate(dlb, axis=0))
    emit(0, gbuf0, last)
    return sq, grad_x


def _size(shape):
    n = 1
    for d in shape:
        n *= d
    return n


def _piece_rows(shape):
    return -(-_size(shape) // PACK_W)


def _packed_misc_rows(shapes):
    return sum(_piece_rows(s) for s in shapes)


def _cast_into(src, buf, row, name):
    rows, W = src.shape
    tr = min(256, rows)
    assert rows % tr == 0 and row % tr == 0

    def body(s_ref, b_ref, o_ref):
        o_ref[...] = s_ref[...].astype(BF16)

    return pl.pallas_call(
        body, name=name, grid=(rows // tr,),
        in_specs=[pl.BlockSpec((tr, W), lambda i: (i, 0)), pl.BlockSpec(memory_space=pl.ANY)],
        out_specs=pl.BlockSpec((tr, W), lambda i: (row // tr + i, 0)),
        out_shape=jax.ShapeDtypeStruct(buf.shape, buf.dtype), input_output_aliases={1: 0},
        compiler_params=_cparams("parallel"))(src, buf)


def _pack_blocks(pieces, rows, dtype):
    blocks, used = [], 0
    for p in pieces:
        flat = p.astype(dtype).reshape(-1)
        r = _piece_rows(p.shape)
        if r * PACK_W != flat.shape[0]:
            flat = jnp.pad(flat, (0, r * PACK_W - flat.shape[0]))
        blocks.append(flat.reshape(r, PACK_W))
        used += r
    if rows > used:
        blocks.append(jnp.zeros((rows - used, PACK_W), dtype))
    return blocks


def _unpack(buf, shapes):
    out, off = [], 0
    for shp in shapes:
        r = _piece_rows(shp)
        piece = buf[off:off + r]
        if r * PACK_W != _size(shp):
            piece = piece.reshape(-1)[:_size(shp)]
        out.append(piece.reshape(shp))
        off += r
    return out


def _mesh_place():
    x, y, c = lax.axis_index("x"), lax.axis_index("y"), lax.axis_index("c")
    chips = [(1 - x, y), (x, 1 - y), (1 - x, 1 - y)]
    return x, y, c, chips


_HBM = pl.BlockSpec(memory_space=pltpu.HBM)


def _all_gather(wp):
    R, W = wp.shape
    rh = R // 2
    rq = rh // 2
    assert rq % 16 == 0

    def body(w_ref, out_ref, send_sems, recv_sems):
        x, y, c, _ = _mesh_place()
        me, jx, jy, jd = 2 * x + y, 2 * (1 - x) + y, 2 * x + (1 - y), 2 * (1 - x) + (1 - y)
        to_x, to_y, sibling = (1 - x, y, c), (x, 1 - y, c), (x, y, 1 - c)

        def rows(core, quarter):
            return pl.ds(pl.multiple_of(core * rh + quarter * rq, 16), rq)

        def slot(j, core, quarter):
            return out_ref.at[j, rows(core, quarter)]

        def copy(k, src, dst, to):
            return pltpu.make_async_remote_copy(src_ref=src, dst_ref=dst, send_sem=send_sems.at[k],
                                                recv_sem=recv_sems.at[k], device_id=to, device_id_type=MESH)

        sends = [copy(0, w_ref.at[rows(c, 0)], slot(me, c, 0), to_x),
                 copy(2, w_ref.at[rows(c, 1)], slot(me, c, 1), to_y),
                 copy(1, w_ref.at[rows(c, 1)], slot(me, c, 1), to_x),
                 copy(3, w_ref.at[rows(c, 0)], slot(me, c, 0), to_y)]
        for cp in sends:
            cp.start()
        arrivals = [(0, slot(jx, c, 0), 4, to_y, 6), (2, slot(jy, c, 1), 5, to_x, 7),
                    (1, slot(jx, c, 1), None, None, 8), (3, slot(jy, c, 0), None, None, 9),
                    (4, slot(jd, c, 0), None, None, 10), (5, slot(jd, c, 1), None, None, 11)]
        for k, landed, k_on, to_on, k_sib in arrivals:
            copy(k, landed, landed, sibling).wait_recv()
            if k_on is not None:
                cp = copy(k_on, landed, landed, to_on)
                cp.start()
                sends.append(cp)
            cp = copy(k_sib, landed, landed, sibling)
            cp.start()
            sends.append(cp)
        for k_sib, j, quarter in ((6, jx, 0), (7, jy, 1), (8, jx, 1), (9, jy, 0), (10, jd, 0), (11, jd, 1)):
            landed = slot(j, 1 - c, quarter)
            copy(k_sib, landed, landed, sibling).wait_recv()
        for cp in sends:
            cp.wait_send()

    out = pl.pallas_call(
        body, name="weights_all_gather", in_specs=[_HBM], out_specs=_HBM,
        out_shape=jax.ShapeDtypeStruct((N_CHIPS, R, W), wp.dtype),
        scratch_shapes=[pltpu.SemaphoreType.DMA((12,)), pltpu.SemaphoreType.DMA((12,))],
    )(wp)
    me = 2 * lax.axis_index("x") + lax.axis_index("y")
    return lax.dynamic_update_slice(out, wp[None], (me, 0, 0))


def _exchange_halves(g):
    n, _, rh, W = g.shape

    def body(g_ref, out_ref, send_sems, recv_sems):
        x, y, c, _ = _mesh_place()
        sibling = (x, y, 1 - c)
        copies = [pltpu.make_async_remote_copy(
            src_ref=g_ref.at[j, 1 - c], dst_ref=out_ref.at[j], send_sem=send_sems.at[j],
            recv_sem=recv_sems.at[j], device_id=sibling, device_id_type=MESH) for j in range(n)]
        for cp in copies:
            cp.start()
        for cp in copies:
            cp.wait()

    return pl.pallas_call(
        body, name="grads_to_sibling", in_specs=[_HBM], out_specs=_HBM,
        out_shape=jax.ShapeDtypeStruct((n, rh, W), g.dtype),
        scratch_shapes=[pltpu.SemaphoreType.DMA((n,)), pltpu.SemaphoreType.DMA((n,))],
    )(g)


def _scatter_to_owners(p):
    n, rh, W = p.shape
    rq = rh // 2
    assert rq % 16 == 0

    def body(p_ref, out_ref, stage_ref, send_sems, recv_sems):
        x, y, c, _ = _mesh_place()
        me, jx, jy, jd = 2 * x + y, 2 * (1 - x) + y, 2 * x + (1 - y), 2 * (1 - x) + (1 - y)
        to_x, to_y = (1 - x, y, c), (x, 1 - y, c)

        def quarter(ref, j, q):
            return ref.at[j, pl.ds(q * rq, rq)]

        def copy(k, src, dst, to):
            return pltpu.make_async_remote_copy(src_ref=src, dst_ref=dst, send_sem=send_sems.at[k],
                                                recv_sem=recv_sems.at[k], device_id=to, device_id_type=MESH)

        sends = [copy(2, quarter(p_ref, jd, 0), stage_ref.at[0], to_x),
                 copy(3, quarter(p_ref, jd, 1), stage_ref.at[1], to_y),
                 copy(0, p_ref.at[jx], out_ref.at[me], to_x),
                 copy(1, p_ref.at[jy], out_ref.at[me], to_y)]
        for cp in sends:
            cp.start()
        copy(2, stage_ref.at[0], stage_ref.at[0], to_x).wait_recv()
        relay = copy(4, stage_ref.at[0], quarter(out_ref, jx, 0), to_y)
        relay.start()
        sends.append(relay)
        copy(3, stage_ref.at[1], stage_ref.at[1], to_y).wait_recv()
        relay = copy(5, stage_ref.at[1], quarter(out_ref, jy, 1), to_x)
        relay.start()
        sends.append(relay)
        copy(0, out_ref.at[jx], out_ref.at[jx], to_x).wait_recv()
        copy(1, out_ref.at[jy], out_ref.at[jy], to_y).wait_recv()
        copy(4, quarter(out_ref, jd, 0), quarter(out_ref, jd, 0), to_y).wait_recv()
        copy(5, quarter(out_ref, jd, 1), quarter(out_ref, jd, 1), to_x).wait_recv()
        for cp in sends:
            cp.wait_send()

    out, _ = pl.pallas_call(
        body, name="grads_to_owner", in_specs=[_HBM], out_specs=(_HBM, _HBM),
        out_shape=(jax.ShapeDtypeStruct((n, rh, W), p.dtype), jax.ShapeDtypeStruct((2, rq, W), p.dtype)),
        scratch_shapes=[pltpu.SemaphoreType.DMA((6,)), pltpu.SemaphoreType.DMA((6,))],
    )(p)
    me = 2 * lax.axis_index("x") + lax.axis_index("y")
    mine = lax.dynamic_index_in_dim(p, me, axis=0, keepdims=True)
    return lax.dynamic_update_slice(out, mine, (me, 0, 0))


def _share_reduced(q, name="grads_share_reduced"):
    rh, W = q.shape

    def body(q_ref, out_ref, send_sem, recv_sem):
        x, y, c, _ = _mesh_place()
        cp = pltpu.make_async_remote_copy(src_ref=q_ref, dst_ref=out_ref.at[c], send_sem=send_sem,
                                          recv_sem=recv_sem, device_id=(x, y, 1 - c), device_id_type=MESH)
        cp.start()
        cp.wait()

    out = pl.pallas_call(
        body, name=name, in_specs=[_HBM], out_specs=_HBM,
        out_shape=jax.ShapeDtypeStruct((2, rh, W), q.dtype),
        scratch_shapes=[pltpu.SemaphoreType.DMA, pltpu.SemaphoreType.DMA],
    )(q)
    return lax.dynamic_update_slice(out, q[None], (lax.axis_index("c"), 0, 0))


def _add_sibling(g, recv, c_arr):
    n, _, rh, W = g.shape
    tr = PACK_TILE

    def body(c_ref, g_ref, r_ref, o_ref):
        o_ref[...] = (g_ref[...].astype(F32) + r_ref[...].astype(F32)).astype(BF16)

    return pl.pallas_call(
        body, name="grads_add_sibling",
        grid_spec=pltpu.PrefetchScalarGridSpec(
            num_scalar_prefetch=1, grid=(n, rh // tr),
            in_specs=[pl.BlockSpec((None, None, tr, W), lambda j, i, c_ref: (j, c_ref[0], i, 0)),
                      pl.BlockSpec((None, tr, W), lambda j, i, c_ref: (j, i, 0))],
            out_specs=pl.BlockSpec((None, tr, W), lambda j, i, c_ref: (j, i, 0))),
        out_shape=jax.ShapeDtypeStruct((n, rh, W), BF16),
        compiler_params=_cparams("parallel", "parallel"))(c_arr, g, recv)


def _sum_chips(parts, name="grads_sum_chips"):
    n, rh, W = parts.shape
    tr = PACK_TILE

    def body(p_ref, o_ref):
        acc = p_ref[0].astype(F32)
        for j in range(1, n):
            acc = acc + p_ref[j].astype(F32)
        o_ref[...] = acc

    return pl.pallas_call(
        body, name=name, grid=(rh // tr,),
        in_specs=[pl.BlockSpec((n, tr, W), lambda i: (0, i, 0))],
        out_specs=pl.BlockSpec((tr, W), lambda i: (i, 0)),
        out_shape=jax.ShapeDtypeStruct((rh, W), F32),
        compiler_params=_cparams("parallel"))(parts)


_SEM = pl.BlockSpec(memory_space=pltpu.SEMAPHORE)
_ASYNC = pltpu.CompilerParams(has_side_effects=pltpu.SideEffectType.DATAFLOW_SIDE_EFFECTING)


def _hbm(a):
    return pltpu.with_memory_space_constraint(a, pltpu.HBM)


def _gather_copies(w_ref, land_ref, send_sems, recv_sems):
    x, y, c, chips = _mesh_place()
    me = 2 * x + y
    rh = w_ref.shape[0] // 2
    rows = pl.ds(pl.multiple_of(c * rh, 16), rh)
    return [pltpu.make_async_remote_copy(
        src_ref=w_ref.at[rows], dst_ref=land_ref.at[me, rows], send_sem=send_sems.at[r],
        recv_sem=recv_sems.at[r], device_id=(px, py, c), device_id_type=MESH)
        for r, (px, py) in enumerate(chips)]


def _scatter_copies(g_ref, land_ref, send_sems, recv_sems, row0):
    x, y, c, chips = _mesh_place()
    me = 2 * x + y
    rows = pl.ds(row0, land_ref.shape[1])
    return [pltpu.make_async_remote_copy(
        src_ref=g_ref.at[2 * px + py, rows], dst_ref=land_ref.at[me], send_sem=send_sems.at[r],
        recv_sem=recv_sems.at[r], device_id=(px, py, c), device_id_type=MESH)
        for r, (px, py) in enumerate(chips)]


def _halves_to_sibling(land, name):
    n, R, W = land.shape
    rh = R // 2

    def body(l_ref, o_ref, send_sems, recv_sems):
        x, y, c, chips = _mesh_place()
        rows = pl.ds(pl.multiple_of(c * rh, 16), rh)
        copies = [pltpu.make_async_remote_copy(
            src_ref=o_ref.at[2 * px + py, rows], dst_ref=o_ref.at[2 * px + py, rows], send_sem=send_sems.at[r],
            recv_sem=recv_sems.at[r], device_id=(x, y, 1 - c), device_id_type=MESH)
            for r, (px, py) in enumerate(chips)]
        for cp in copies:
            cp.start()
        for cp in copies:
            cp.wait()

    return pl.pallas_call(
        body, name=name, in_specs=[_HBM], out_specs=_HBM, out_shape=jax.ShapeDtypeStruct(land.shape, land.dtype),
        scratch_shapes=[pltpu.SemaphoreType.DMA((3,)), pltpu.SemaphoreType.DMA((3,))],
        input_output_aliases={0: 0})(land)


def _gather_start(wp, name):
    R, W = wp.shape

    def body(w_ref, land_ref, send_sems, recv_sems, w_thru, land_thru, token):
        for cp in _gather_copies(w_ref, land_ref, send_sems, recv_sems):
            cp.start()
        token[...] = jnp.zeros_like(token)

    return pl.pallas_call(
        body, name=name,
        out_shape=(pltpu.SemaphoreType.DMA((3,)), pltpu.SemaphoreType.DMA((3,)), pltpu.HBM(wp.shape, wp.dtype),
                   pltpu.HBM((N_CHIPS, R, W), wp.dtype), jax.ShapeDtypeStruct((8, 128), F32)),
        in_specs=(_HBM, _HBM),
        out_specs=(_SEM, _SEM, _HBM, _HBM, pl.BlockSpec(memory_space=pltpu.VMEM)),
        input_output_aliases={0: 2, 1: 3}, compiler_params=_ASYNC,
    )(_hbm(wp), _hbm(lax.empty((N_CHIPS, R, W), wp.dtype)))


def _gather_wait(send_sems, recv_sems, w_thru, land_thru, after, name):
    R, W = w_thru.shape
    rh = R // 2

    def body(w_ref, land_ref, send_sems, recv_sems, after_ref, w_dead, got_ref):
        x, y, c, _ = _mesh_place()
        half = land_ref.at[0, pl.ds(0, rh)]
        for k in range(3):
            cp = pltpu.make_async_remote_copy(src_ref=half, dst_ref=half, send_sem=send_sems.at[k],
                                              recv_sem=recv_sems.at[k], device_id=(x, y, 1 - c),
                                              device_id_type=MESH)
            cp.wait_send()
            cp.wait_recv()

    return pl.pallas_call(
        body, name=name,
        out_shape=(pltpu.HBM(w_thru.shape, w_thru.dtype), pltpu.HBM(land_thru.shape, land_thru.dtype)),
        in_specs=(_HBM, _HBM, _SEM, _SEM, pl.BlockSpec(memory_space=pl.ANY)), out_specs=(_HBM, _HBM),
        input_output_aliases={0: 0, 1: 1}, compiler_params=_ASYNC,
    )(w_thru, land_thru, send_sems, recv_sems, after)


def _scatter_start(g, row0, nrows, name):
    n, R, W = g.shape
    land_shape = (n, nrows, W)

    def body(g_ref, land_ref, send_sems, recv_sems, g_thru, land_thru, token):
        for cp in _scatter_copies(g_ref, land_ref, send_sems, recv_sems, row0):
            cp.start()
        token[...] = jnp.zeros_like(token)

    return pl.pallas_call(
        body, name=name,
        out_shape=(pltpu.SemaphoreType.DMA((3,)), pltpu.SemaphoreType.DMA((3,)), pltpu.HBM(g.shape, g.dtype),
                   pltpu.HBM(land_shape, g.dtype), jax.ShapeDtypeStruct((8, 128), F32)),
        in_specs=(_HBM, _HBM),
        out_specs=(_SEM, _SEM, _HBM, _HBM, pl.BlockSpec(memory_space=pltpu.VMEM)),
        input_output_aliases={0: 2, 1: 3}, compiler_params=_ASYNC,
    )(_hbm(g), _hbm(lax.empty(land_shape, g.dtype)))


def _scatter_wait(send_sems, recv_sems, g_thru, land_thru, after, name):
    def body(g_ref, land_ref, send_sems, recv_sems, after_ref, g_out, got_ref):
        x, y, c, _ = _mesh_place()
        for k in range(3):
            cp = pltpu.make_async_remote_copy(src_ref=land_ref.at[0], dst_ref=land_ref.at[0], send_sem=send_sems.at[k],
                                              recv_sem=recv_sems.at[k], device_id=(x, y, 1 - c),
                                              device_id_type=MESH)
            cp.wait_send()
            cp.wait_recv()

    return pl.pallas_call(
        body, name=name,
        out_shape=(pltpu.HBM(g_thru.shape, g_thru.dtype), pltpu.HBM(land_thru.shape, land_thru.dtype)),
        in_specs=(_HBM, _HBM, _SEM, _SEM, pl.BlockSpec(memory_space=pl.ANY)), out_specs=(_HBM, _HBM),
        input_output_aliases={0: 0, 1: 1}, compiler_params=_ASYNC,
    )(g_thru, land_thru, send_sems, recv_sems, after)


def _adamw(w, g, m, v, name):
    shape = w.shape
    cols = shape[-1]
    w2, g2, m2, v2 = (t.reshape(-1, cols) for t in (w, g, m, v))
    rows = w2.shape[0]
    tr = rows
    for cand in (512, 256, 128, 64, 32, 16, 8):
        if rows > cand and rows % cand == 0:
            tr = cand
            break
    c1 = 1.0 / (1.0 - ADAM_B1 ** ADAM_STEP)
    c2 = 1.0 / (1.0 - ADAM_B2 ** ADAM_STEP)

    def body(w_ref, g_ref, m_ref, v_ref, d_ref, nm_ref, nv_ref):
        gv = g_ref[...]
        nm = ADAM_B1 * m_ref[...] + (1.0 - ADAM_B1) * gv
        nv = ADAM_B2 * v_ref[...] + (1.0 - ADAM_B2) * (gv * gv)
        nm_ref[...] = nm
        nv_ref[...] = nv
        d_ref[...] = -ADAM_LR * ((nm * c1) / (jnp.sqrt(nv * c2) + ADAM_EPS) + ADAM_WD * w_ref[...])

    blk = pl.BlockSpec((tr, cols), lambda i: (i, 0))
    sds = jax.ShapeDtypeStruct((rows, cols), F32)
    d, nm, nv = pl.pallas_call(body, name=name, grid=(rows // tr,), in_specs=[blk] * 4,
                               out_specs=(blk, blk, blk), out_shape=(sds, sds, sds),
                               compiler_params=_cparams("parallel"))(w2, g2, m2, v2)
    return d.reshape(shape), nm.reshape(shape), nv.reshape(shape)


def kernel(x, positions, norm_gains, mla_w_in, mla_q_norm, mla_kv_norm, mla_w_uq, mla_w_ukv, mla_w_o, hgrn_w_in, hgrn_lb_logits, hgrn_o_norm, hgrn_w_o, mlp_w1, mlp_w2, loss_target, m_norm_gains, m_mla_w_in, m_mla_q_norm, m_mla_kv_norm, m_mla_w_uq, m_mla_w_ukv, m_mla_w_o, m_hgrn_w_in, m_hgrn_lb_logits, m_hgrn_o_norm, m_hgrn_w_o, m_mlp_w1, m_mlp_w2, v_norm_gains, v_mla_w_in, v_mla_q_norm, v_mla_kv_norm, v_mla_w_uq, v_mla_w_ukv, v_mla_w_o, v_hgrn_w_in, v_hgrn_lb_logits, v_hgrn_o_norm, v_hgrn_w_o, v_mlp_w1, v_mlp_w2):
    w = dict(norm_gains=norm_gains, mla_w_in=mla_w_in, mla_q_norm=mla_q_norm, mla_kv_norm=mla_kv_norm,
             mla_w_uq=mla_w_uq, mla_w_ukv=mla_w_ukv, mla_w_o=mla_w_o, hgrn_w_in=hgrn_w_in,
             hgrn_lb_logits=hgrn_lb_logits, hgrn_o_norm=hgrn_o_norm, hgrn_w_o=hgrn_w_o,
             mlp_w1=mlp_w1, mlp_w2=mlp_w2)
    mom_m = dict(norm_gains=m_norm_gains, mla_w_in=m_mla_w_in, mla_q_norm=m_mla_q_norm,
                 mla_kv_norm=m_mla_kv_norm, mla_w_uq=m_mla_w_uq, mla_w_ukv=m_mla_w_ukv,
                 mla_w_o=m_mla_w_o, hgrn_w_in=m_hgrn_w_in, hgrn_lb_logits=m_hgrn_lb_logits,
                 hgrn_o_norm=m_hgrn_o_norm, hgrn_w_o=m_hgrn_w_o, mlp_w1=m_mlp_w1, mlp_w2=m_mlp_w2)
    mom_v = dict(norm_gains=v_norm_gains, mla_w_in=v_mla_w_in, mla_q_norm=v_mla_q_norm,
                 mla_kv_norm=v_mla_kv_norm, mla_w_uq=v_mla_w_uq, mla_w_ukv=v_mla_w_ukv,
                 mla_w_o=v_mla_w_o, hgrn_w_in=v_hgrn_w_in, hgrn_lb_logits=v_hgrn_lb_logits,
                 hgrn_o_norm=v_hgrn_o_norm, hgrn_w_o=v_hgrn_w_o, mlp_w1=v_mlp_w1, mlp_w2=v_mlp_w2)
    c = lax.axis_index("c")

    axis_of = dict(SHARDED)
    me = 2 * lax.axis_index("x") + lax.axis_index("y")
    gain_bits = lax.bitcast_convert_type(norm_gains, jnp.uint32)
    gain_hi = lax.bitcast_convert_type((gain_bits >> 16).astype(jnp.uint16), BF16)
    gain_lo = lax.bitcast_convert_type((gain_bits & 0xFFFF).astype(jnp.uint16), BF16)

    layers = []
    for l in range(DEPTH):
        s = l // 2
        if l % 2 == 0:
            big = [("mlp_w1", l), ("mlp_w2", l), ("mla_w_o", s)]
            tail = [("mla_w_in", s), ("mla_w_uq", s), ("mla_w_ukv", s)]
        else:
            big = [("hgrn_w_in", s), ("mlp_w1", l), ("mlp_w2", l), ("hgrn_w_o", s)]
            tail = []
        w_tail = [w[n][i] for n, i in tail] + ([gain_hi, gain_lo] if l == 0 else [])
        g_tail = tail + ([("norm_gains", None)] + [(n, None) for n in REPLICATED] if l == 0 else [])
        g_shapes = [w[n].shape if i is None else w[n][i].shape for n, i in g_tail]
        tail_rows = max(_packed_misc_rows([t.shape for t in w_tail]), _packed_misc_rows(g_shapes))
        pk = _Packed([(n, w[n].shape[1]) for n, _ in big], tail_rows)
        wpack = jnp.zeros((pk.rows, PACK_W), BF16)
        for n, i in big:
            assert w[n].shape[2] == PACK_W
            wpack = _cast_into(w[n][i], wpack, pk.off[n], name="pack_%s_%d" % (n, l))
        if w_tail:
            wpack = lax.dynamic_update_slice(
                wpack, jnp.concatenate(_pack_blocks(w_tail, 0, BF16), axis=0), (pk.misc, 0))
        layers.append(dict(pk=pk, big=big, tail=tail, w_tail=w_tail, g_tail=g_tail, g_shapes=g_shapes,
                           wpack=wpack))

    for l, lay in enumerate(layers):
        lay["gather"] = _gather_start(lay["wpack"], name="gather_start_%d" % l)

    def fetch(l, after):
        lay = layers[l]
        pk = lay["pk"]
        send_sems, recv_sems, w_thru, land_thru, _ = lay["gather"]
        if after is None:
            after = sum(layers[k]["gather"][4] for k in range(1, DEPTH))
        w_back, land = _gather_wait(send_sems, recv_sems, w_thru, land_thru, after, name="gather_wait_%d" % l)
        land = _halves_to_sibling(land, name="gather_halves_%d" % l)
        land = lax.dynamic_update_slice(land, w_back[None], (me, 0, 0))
        out = dict(wbuf=land.reshape(N_CHIPS * pk.rows, PACK_W), pk=pk)
        if lay["w_tail"]:
            rows = _packed_misc_rows([t.shape for t in lay["w_tail"]])
            per_chip = [_unpack(land[j, pk.misc:pk.misc + rows], [t.shape for t in lay["w_tail"]])
                        for j in range(N_CHIPS)]
            for i, (n, _) in enumerate(lay["tail"]):
                out[n[4:]] = jnp.concatenate([per_chip[j][i] for j in range(N_CHIPS)], axis=axis_of[n] - 1)
            if l == 0:
                got_hi, got_lo = (lax.bitcast_convert_type(
                    jnp.concatenate([per_chip[j][i] for j in range(N_CHIPS)], axis=2),
                    jnp.uint16).astype(jnp.uint32) for i in (-2, -1))
                out["gains"] = lax.bitcast_convert_type((got_hi << 16) | got_lo, F32)
        return out

    def emit(l, gbuf, grads):
        lay = layers[l]
        pk = lay["pk"]
        if lay["g_tail"]:
            for j in range(N_CHIPS):
                pieces = []
                for n, i in lay["g_tail"]:
                    if n not in axis_of:
                        pieces.append(grads[n])
                    else:
                        pieces.append(jnp.split(grads[n], N_CHIPS, axis=axis_of[n] - (0 if i is None else 1))[j])
                block = jnp.concatenate(_pack_blocks(pieces, 0, BF16), axis=0)
                gbuf = lax.dynamic_update_slice(gbuf, block, (j * pk.rows + pk.misc, 0))
        row0 = lay.get("early_rows", 0)
        lay["scatter"] = _scatter_start(gbuf.reshape(N_CHIPS, pk.rows, PACK_W), row0, pk.rows - row0,
                                        name="scatter_start_%d" % l)
        return lay["scatter"][4][0, 0]

    def emit_mlp(l, gbuf):
        lay = layers[l]
        pk = lay["pk"]
        assert pk.off["mlp_w1"] == 0 and pk.off["mlp_w2"] == w["mlp_w1"].shape[1]
        lay["early_rows"] = w["mlp_w1"].shape[1] + w["mlp_w2"].shape[1]
        lay["scatter_early"] = _scatter_start(gbuf.reshape(N_CHIPS, pk.rows, PACK_W), 0, lay["early_rows"],
                                              name="scatter_start_%d_mlp" % l)
        return lay["scatter_early"][2].reshape(N_CHIPS * pk.rows, PACK_W)

    small = dict(mla_q_norm=mla_q_norm, mla_kv_norm=mla_kv_norm, hgrn_lb_logits=hgrn_lb_logits,
                 hgrn_o_norm=hgrn_o_norm)
    gbufs = [jnp.zeros((N_CHIPS * lay["pk"].rows, PACK_W), BF16) for lay in layers]
    sq, grad_x = _local_step(x[0], positions[0], loss_target[0], small, fetch, gbufs, emit, emit_mlp)
    d_model = x.shape[-1]
    loss = lax.psum(0.5 * jnp.sum(sq) / d_model, ("x", "y", "c"))

    per_name = {}
    for l, lay in enumerate(layers):
        pk = lay["pk"]
        send_sems, recv_sems, g_thru, land_thru, _ = lay["scatter"]
        row0 = lay.get("early_rows", 0)
        g_back, land = _scatter_wait(send_sems, recv_sems, g_thru, land_thru, grad_x, name="scatter_wait_%d" % l)
        own = lax.dynamic_slice_in_dim(g_back, me, 1, axis=0)
        land = lax.dynamic_update_slice(land, own[:, row0:], (me, 0, 0))
        mine = _sum_chips(land, name="grads_sum_chips_%d" % l)
        if row0:
            send_sems, recv_sems, _, land_thru, _ = lay["scatter_early"]
            g_back, land = _scatter_wait(send_sems, recv_sems, g_back, land_thru, grad_x,
                                         name="scatter_wait_%d_mlp" % l)
            land = lax.dynamic_update_slice(land, own[:, :row0], (me, 0, 0))
            mine = jnp.concatenate([_sum_chips(land, name="grads_sum_chips_%d_mlp" % l), mine], axis=0)
        red = _sum_chips(_share_reduced(mine, name="grads_share_%d" % l), name="grads_sum_cores_%d" % l)
        for n, i in lay["big"]:
            per_name.setdefault(n, {})[i] = red[pk.off[n]:pk.off[n] + w[n].shape[1]]
        for (n, i), piece in zip(lay["g_tail"], _unpack(red[pk.misc:pk.misc + pk.misc_rows], lay["g_shapes"])):
            per_name.setdefault(n, {})[i] = piece
    g_out = {n: (parts[None] if None in parts else jnp.stack([parts[i] for i in sorted(parts)]))
             for n, parts in per_name.items()}

    deltas, new_m, new_v = {}, {}, {}
    for name in WEIGHTS:
        deltas[name], new_m[name], new_v[name] = _adamw(w[name], g_out[name], mom_m[name], mom_v[name],
                                                        name="adamw_" + name)
    return (loss, grad_x[None], *[g_out[n] for n in WEIGHTS], *[deltas[n] for n in WEIGHTS],
            *[new_m[n] for n in WEIGHTS], *[new_v[n] for n in WEIGHTS])
```

```python
import functools

import jax
import jax.numpy as jnp
from jax import lax
from jax.experimental import pallas as pl
from jax.experimental.pallas import tpu as pltpu

F32 = jnp.float32
BF16 = jnp.bfloat16
MESH = pl.DeviceIdType.MESH

DEPTH = 4
MLA_HEADS = 8
MLA_NOPE = 128
MLA_ROPE = 64
MLA_V = 128
MLA_QK_PAD = 256
MLA_HEADS_PER_STEP = 2
MLA_SCALE = float(MLA_NOPE + MLA_ROPE) ** -0.5
ROPE_BASE = 10000.0
HGRN_HEADS = 8
HGRN_CHUNK = 32
HGRN_BLOCK = 128
EPS = 1e-6

ADAM_LR = 0.001
ADAM_B1 = 0.9
ADAM_B2 = 0.999
ADAM_EPS = 1e-08
ADAM_WD = 0.01
ADAM_STEP = 10

N_CHIPS = 4
PACK_W = 1024
PACK_ALIGN = 1024
PACK_TILE = 512
V7X_VMEM_LIMIT = 56 * 1024 * 1024

SHARDED = (("norm_gains", 2), ("mla_w_in", 1), ("mla_w_uq", 2), ("mla_w_ukv", 2), ("mla_w_o", 1),
           ("hgrn_w_in", 2), ("hgrn_w_o", 1), ("mlp_w1", 2), ("mlp_w2", 1))
REPLICATED = ("mla_q_norm", "mla_kv_norm", "hgrn_lb_logits", "hgrn_o_norm")
WEIGHTS = ("norm_gains", "mla_w_in", "mla_q_norm", "mla_kv_norm", "mla_w_uq", "mla_w_ukv", "mla_w_o",
           "hgrn_w_in", "hgrn_lb_logits", "hgrn_o_norm", "hgrn_w_o", "mlp_w1", "mlp_w2")


def _cparams(*semantics):
    return pltpu.CompilerParams(dimension_semantics=semantics, vmem_limit_bytes=V7X_VMEM_LIMIT)


def _sigmoid(x):
    return 1.0 / (1.0 + jnp.exp(-x))


def _mm(a, b, *, ta=False, tb=False, out_dtype=F32, tm=2048, tn=1024, tk=1024, epi=None, extra=None,
        name="mm", n=None, b_map=None, into=None, o_map=None):
    if ta:
        K, M = a.shape
    else:
        M, K = a.shape
    if b_map is not None:
        N = n
    elif tb:
        N, Kb = b.shape
    else:
        Kb, N = b.shape
    assert b_map is not None or K == Kb, (a.shape, b.shape, ta, tb)
    tm, tn = min(tm, M), min(tn, N)
    tk = K if (K <= 1024 and b_map is None) else min(tk, K)
    assert M % tm == 0 and N % tn == 0 and K % tk == 0, (M, N, K, tm, tn, tk)
    nk = K // tk
    a_spec = (pl.BlockSpec((tk, tm), lambda i, j, k: (k, i)) if ta
              else pl.BlockSpec((tm, tk), lambda i, j, k: (i, k)))
    if b_map is None:
        b_map = (lambda i, j, k: (j, k)) if tb else (lambda i, j, k: (k, j))
    b_spec = pl.BlockSpec((tn, tk) if tb else (tk, tn), b_map)
    o_spec = pl.BlockSpec((tm, tn), lambda i, j, k: (i, j))
    dims = (((0 if ta else 1,), (1 if tb else 0,)), ((), ()))
    in_specs = [a_spec, b_spec]
    operands = [a, b]
    aliases = {}
    if epi == "mul2r":
        in_specs.append(o_spec)
        operands.append(extra)
    if into is not None:
        assert epi is None
        in_specs.append(pl.BlockSpec(memory_space=pl.ANY))
        operands.append(into)
        aliases = {2: 0}
        out_dtype = into.dtype
        out_shape = jax.ShapeDtypeStruct(into.shape, into.dtype)
        out_specs = pl.BlockSpec((tm, tn), o_map)
    elif epi == "relu2":
        out_shape = (jax.ShapeDtypeStruct((M, N), BF16), jax.ShapeDtypeStruct((M, N), BF16))
        out_specs = (o_spec, o_spec)
    elif epi == "mul2r":
        out_shape = jax.ShapeDtypeStruct((M, N), BF16)
        out_specs = o_spec
    else:
        out_shape = jax.ShapeDtypeStruct((M, N), out_dtype)
        out_specs = o_spec
    n_in = len(operands)

    def body(*refs):
        a_ref, b_ref = refs[0], refs[1]
        outs = refs[n_in:n_in + (2 if epi == "relu2" else 1)]
        k = pl.program_id(2)

        def finish(acc):
            if epi == "relu2":
                r = jnp.maximum(acc, 0.0)
                outs[0][...] = (r * r).astype(BF16)
                outs[1][...] = r.astype(BF16)
            elif epi == "mul2r":
                outs[0][...] = (acc * (2.0 * refs[2][...].astype(F32))).astype(BF16)
            else:
                outs[0][...] = acc.astype(out_dtype)

        part = lax.dot_general(a_ref[...], b_ref[...], dims, preferred_element_type=F32)
        if nk == 1:
            finish(part)
            return
        acc_ref = refs[-1]

        @pl.when(k == 0)
        def _():
            acc_ref[...] = part

        @pl.when((k > 0) & (k < nk - 1))
        def _():
            acc_ref[...] += part

        @pl.when(k == nk - 1)
        def _():
            finish(acc_ref[...] + part)

    return pl.pallas_call(
        body, name=name, grid=(M // tm, N // tn, nk), in_specs=in_specs, out_specs=out_specs,
        out_shape=out_shape, scratch_shapes=[pltpu.VMEM((tm, tn), F32)] if nk > 1 else [],
        input_output_aliases=aliases,
        compiler_params=_cparams("parallel", "parallel", "arbitrary"))(*operands)


def _rms_rstd(x):
    return lax.rsqrt(jnp.mean(x * x, axis=-1, keepdims=True) + EPS)


def _rms_bwd_tile(x, g, dy):
    r = _rms_rstd(x)
    xh = x * r
    u = dy * g
    dx = r * (u - xh * jnp.mean(u * xh, axis=-1, keepdims=True))
    dg = jnp.sum(dy * xh, axis=0, keepdims=True)
    return dx, dg


def _row_tile(T):
    return min(256, T)


def _prenorm_fwd(x, g, name="prenorm_fwd"):
    T, D = x.shape
    tm = _row_tile(T)

    def body(x_ref, g_ref, a_ref):
        xv = x_ref[...]
        a_ref[...] = (xv * _rms_rstd(xv) * g_ref[...]).astype(BF16)

    row = pl.BlockSpec((tm, D), lambda i: (i, 0))
    vec = pl.BlockSpec((1, D), lambda i: (0, 0))
    return pl.pallas_call(body, name=name, grid=(T // tm,), in_specs=[row, vec], out_specs=row,
                          out_shape=jax.ShapeDtypeStruct((T, D), BF16),
                          compiler_params=_cparams("parallel"))(x, g)


def _resnorm_fwd(h, z, g_post, g_pre, name="resnorm_fwd"):
    T, D = h.shape
    tm = _row_tile(T)

    def body(h_ref, z_ref, gp_ref, gn_ref, hn_ref, a_ref):
        zv = z_ref[...]
        hn = h_ref[...] + zv * _rms_rstd(zv) * gp_ref[...]
        hn_ref[...] = hn
        a_ref[...] = (hn * _rms_rstd(hn) * gn_ref[...]).astype(BF16)

    row = pl.BlockSpec((tm, D), lambda i: (i, 0))
    vec = pl.BlockSpec((1, D), lambda i: (0, 0))
    return pl.pallas_call(body, name=name, grid=(T // tm,), in_specs=[row, row, vec, vec],
                          out_specs=(row, row),
                          out_shape=(jax.ShapeDtypeStruct((T, D), F32), jax.ShapeDtypeStruct((T, D), BF16)),
                          compiler_params=_cparams("parallel"))(h, z, g_post, g_pre)


def _resnorm_loss(h, z, g_post, target, name="resnorm_loss"):
    T, D = h.shape
    tm = _row_tile(T)

    def body(h_ref, z_ref, gp_ref, t_ref, dy_ref, sq_ref):
        zv = z_ref[...]
        err = h_ref[...] + zv * _rms_rstd(zv) * gp_ref[...] - t_ref[...]
        dy_ref[...] = err * (1.0 / D)

        @pl.when(pl.program_id(0) == 0)
        def _():
            sq_ref[...] = jnp.zeros_like(sq_ref)

        sq_ref[...] += jnp.sum(err * err, axis=0, keepdims=True)

    row = pl.BlockSpec((tm, D), lambda i: (i, 0))
    vec = pl.BlockSpec((1, D), lambda i: (0, 0))
    return pl.pallas_call(body, name=name, grid=(T // tm,), in_specs=[row, row, vec, row],
                          out_specs=(row, vec),
                          out_shape=(jax.ShapeDtypeStruct((T, D), F32), jax.ShapeDtypeStruct((1, D), F32)),
                          compiler_params=_cparams("arbitrary"))(h, z, g_post, target)


def _resnorm_bwd(z, g_post, dh, h_new=None, da=None, g_pre=None, name="resnorm_bwd"):
    T, D = z.shape
    tm = _row_tile(T)
    has_next = h_new is not None
    row = pl.BlockSpec((tm, D), lambda i: (i, 0))
    vec = pl.BlockSpec((1, D), lambda i: (0, 0))

    if has_next:
        def body(z_ref, gp_ref, dh_ref, hn_ref, da_ref, gn_ref, t_ref, dz_ref, dgp_ref, dgn_ref):
            first = pl.program_id(0) == 0

            @pl.when(first)
            def _():
                dgp_ref[...] = jnp.zeros_like(dgp_ref)
                dgn_ref[...] = jnp.zeros_like(dgn_ref)

            dpre, dgn = _rms_bwd_tile(hn_ref[...], gn_ref[...], da_ref[...])
            t = dh_ref[...] + dpre
            t_ref[...] = t
            dz, dgp = _rms_bwd_tile(z_ref[...], gp_ref[...], t)
            dz_ref[...] = dz.astype(BF16)
            dgp_ref[...] += dgp
            dgn_ref[...] += dgn

        return pl.pallas_call(
            body, name=name, grid=(T // tm,), in_specs=[row, vec, row, row, row, vec],
            out_specs=(row, row, vec, vec),
            out_shape=(jax.ShapeDtypeStruct((T, D), F32), jax.ShapeDtypeStruct((T, D), BF16),
                       jax.ShapeDtypeStruct((1, D), F32), jax.ShapeDtypeStruct((1, D), F32)),
            compiler_params=_cparams("arbitrary"))(z, g_post, dh, h_new, da, g_pre)

    def body_last(z_ref, gp_ref, dh_ref, dz_ref, dgp_ref):
        @pl.when(pl.program_id(0) == 0)
        def _():
            dgp_ref[...] = jnp.zeros_like(dgp_ref)

        dz, dgp = _rms_bwd_tile(z_ref[...], gp_ref[...], dh_ref[...])
        dz_ref[...] = dz.astype(BF16)
        dgp_ref[...] += dgp

    return pl.pallas_call(
        body_last, name=name, grid=(T // tm,), in_specs=[row, vec, row], out_specs=(row, vec),
        out_shape=(jax.ShapeDtypeStruct((T, D), BF16), jax.ShapeDtypeStruct((1, D), F32)),
        compiler_params=_cparams("arbitrary"))(z, g_post, dh)


def _prenorm_bwd(x, g, dh, da, name="prenorm_bwd"):
    T, D = x.shape
    tm = _row_tile(T)

    def body(x_ref, g_ref, dh_ref, da_ref, dx_ref, dg_ref):
        @pl.when(pl.program_id(0) == 0)
        def _():
            dg_ref[...] = jnp.zeros_like(dg_ref)

        dpre, dg = _rms_bwd_tile(x_ref[...], g_ref[...], da_ref[...])
        dx_ref[...] = dh_ref[...] + dpre
        dg_ref[...] += dg

    row = pl.BlockSpec((tm, D), lambda i: (i, 0))
    vec = pl.BlockSpec((1, D), lambda i: (0, 0))
    return pl.pallas_call(
        body, name=name, grid=(T // tm,), in_specs=[row, vec, row, row], out_specs=(row, vec),
        out_shape=(jax.ShapeDtypeStruct((T, D), F32), jax.ShapeDtypeStruct((1, D), F32)),
        compiler_params=_cparams("arbitrary"))(x, g, dh, da)


class _Packed:
    def __init__(self, big, misc_rows):
        self.big = tuple(big)
        self.off = {}
        r = 0
        for name, rows in big:
            self.off[name] = r
            r += rows
        self.misc, self.misc_rows = r, misc_rows
        self.rows = -(-(r + misc_rows) // PACK_ALIGN) * PACK_ALIGN

    def block(self, name, layer, unit):
        r = self.off[name]
        assert r % unit == 0 and self.rows % unit == 0
        return r // unit, self.rows // unit


def _col_sharded(pk, name, layer, unit):
    base, stride = pk.block(name, layer, unit)
    return (lambda i, j, k: (j * stride + base, 0)), (lambda i, j, k: (k * stride + base, 0))


def _row_sharded(pk, name, layer, unit):
    base, stride = pk.block(name, layer, unit)
    return ((lambda i, j, k: (k * stride + base, 0)), (lambda i, j, k: (j * stride + base, 0)),
            (lambda i, j, k: (i * stride + base, 0)))


def _mlp_fwd(a, wbuf, pk, layer):
    D = a.shape[1]
    by_n, _ = _col_sharded(pk, "mlp_w1", layer, D)
    by_k, _, _ = _row_sharded(pk, "mlp_w2", layer, D)
    act, r = _mm(a, wbuf, n=4 * D, b_map=by_n, tk=D, tn=D, epi="relu2", name="mlp_up")
    u = _mm(act, wbuf, n=D, b_map=by_k, tk=D, tn=D, name="mlp_down")
    return u, (a, act, r)


def _mlp_bwd(du, saved, wbuf, gbuf, pk, layer):
    a, act, r = saved
    D = a.shape[1]
    w1_by_n, w1_by_k = _col_sharded(pk, "mlp_w1", layer, D)
    _, w2_by_n, w2_by_m = _row_sharded(pk, "mlp_w2", layer, D)
    dz1 = _mm(du, wbuf, tb=True, n=4 * D, b_map=w2_by_n, tn=D, tk=D, epi="mul2r", extra=r, name="mlp_down_dx")
    gbuf = _mm(act, du, ta=True, into=gbuf, o_map=w2_by_m, tm=D, tn=D, name="mlp_down_dw")
    gbuf = _mm(a, dz1, ta=True, into=gbuf, o_map=w1_by_n, tm=D, tn=D, name="mlp_up_dw")
    da = _mm(dz1, wbuf, tb=True, n=D, b_map=w1_by_k, tn=D, tk=D, name="mlp_up_dx")
    return da, gbuf


def _rope_swap(t):
    n = t.shape[-1]
    lane = lax.broadcasted_iota(jnp.int32, t.shape, t.ndim - 1)
    half = MLA_ROPE // 2
    first = (lane & (MLA_ROPE - 1)) < half
    return jnp.where(first, pltpu.roll(t, n - half, t.ndim - 1), pltpu.roll(t, half, t.ndim - 1))


def _mla_mid_fwd(proj, q_norm, kv_norm, w_uq, w_ukv, cc, ss):
    T, PW = proj.shape
    QL, KVL = q_norm.shape[-1], kv_norm.shape[-1]
    H = MLA_HEADS
    assert PW == QL + KVL + 128
    tm = _row_tile(T)

    def body(p_ref, qn_ref, kn_ref, wq_ref, wkv_ref, cc_ref, ss_ref,
             cq_ref, ckv_ref, q_ref, k_ref, v_ref):
        cq = p_ref[:, 0:QL]
        ckv = p_ref[:, QL:QL + KVL]
        kr = p_ref[:, QL + KVL:QL + KVL + 128]
        c, s = cc_ref[...], ss_ref[...]
        cqn = (cq * _rms_rstd(cq) * qn_ref[...]).astype(BF16)
        ckvn = (ckv * _rms_rstd(ckv) * kn_ref[...]).astype(BF16)
        cq_ref[...] = cqn
        ckv_ref[...] = ckvn
        q = jnp.dot(cqn, wq_ref[...], preferred_element_type=F32)
        kv = jnp.dot(ckvn, wkv_ref[...], preferred_element_type=F32)
        krf = (kr * c + _rope_swap(kr) * s).astype(BF16)
        for h in range(H):
            o = h * MLA_QK_PAD
            q_ref[:, o:o + MLA_NOPE] = (q[:, o:o + MLA_NOPE] * MLA_SCALE).astype(BF16)
            qr = q[:, o + MLA_NOPE:o + MLA_QK_PAD]
            q_ref[:, o + MLA_NOPE:o + MLA_QK_PAD] = ((qr * c + _rope_swap(qr) * s) * MLA_SCALE).astype(BF16)
            k_ref[:, o:o + MLA_NOPE] = kv[:, o:o + MLA_NOPE].astype(BF16)
            k_ref[:, o + MLA_NOPE:o + MLA_QK_PAD] = krf
            v_ref[:, h * MLA_V:(h + 1) * MLA_V] = kv[:, o + MLA_NOPE:o + MLA_QK_PAD].astype(BF16)

    def row(w):
        return pl.BlockSpec((tm, w), lambda i: (i, 0))

    def full(shape):
        return pl.BlockSpec(shape, lambda i: (0, 0))

    return pl.pallas_call(
        body, name="mla_mid_fwd", grid=(T // tm,),
        in_specs=[row(PW), full((1, QL)), full((1, KVL)), full(w_uq.shape), full(w_ukv.shape),
                  row(128), row(128)],
        out_specs=(row(QL), row(KVL), row(H * MLA_QK_PAD), row(H * MLA_QK_PAD), row(H * MLA_V)),
        out_shape=(jax.ShapeDtypeStruct((T, QL), BF16), jax.ShapeDtypeStruct((T, KVL), BF16),
                   jax.ShapeDtypeStruct((T, H * MLA_QK_PAD), BF16),
                   jax.ShapeDtypeStruct((T, H * MLA_QK_PAD), BF16),
                   jax.ShapeDtypeStruct((T, H * MLA_V), BF16)),
        compiler_params=_cparams("parallel"))(proj, q_norm, kv_norm, w_uq, w_ukv, cc, ss)


def _mla_mid_bwd(proj, q_norm, kv_norm, w_uq, w_ukv, cc, ss, dq, dk, dv):
    T, PW = proj.shape
    QL, KVL = q_norm.shape[-1], kv_norm.shape[-1]
    H = MLA_HEADS
    tm = _row_tile(T)
    nt = (((1,), (1,)), ((), ()))

    def body(p_ref, qn_ref, kn_ref, wq_ref, wkv_ref, cc_ref, ss_ref, dq_ref, dk_ref, dv_ref,
             dqp_ref, dkv_ref, dp_ref, dqn_ref, dkn_ref):
        @pl.when(pl.program_id(0) == 0)
        def _():
            dqn_ref[...] = jnp.zeros_like(dqn_ref)
            dkn_ref[...] = jnp.zeros_like(dkn_ref)

        c, s = cc_ref[...], ss_ref[...]
        dkr = jnp.zeros((tm, 128), F32)
        for h in range(H):
            o = h * MLA_QK_PAD
            dqp_ref[:, o:o + MLA_NOPE] = (dq_ref[:, o:o + MLA_NOPE] * MLA_SCALE).astype(BF16)
            dqr = dq_ref[:, o + MLA_NOPE:o + MLA_QK_PAD] * MLA_SCALE
            dqp_ref[:, o + MLA_NOPE:o + MLA_QK_PAD] = (dqr * c + _rope_swap(dqr * s)).astype(BF16)
            dkv_ref[:, o:o + MLA_NOPE] = dk_ref[:, o:o + MLA_NOPE].astype(BF16)
            dkv_ref[:, o + MLA_NOPE:o + MLA_QK_PAD] = dv_ref[:, h * MLA_V:(h + 1) * MLA_V].astype(BF16)
            dkr = dkr + dk_ref[:, o + MLA_NOPE:o + MLA_QK_PAD]
        dcqn = lax.dot_general(dqp_ref[...], wq_ref[...], nt, preferred_element_type=F32)
        dckvn = lax.dot_general(dkv_ref[...], wkv_ref[...], nt, preferred_element_type=F32)
        dcq, dqn = _rms_bwd_tile(p_ref[:, 0:QL], qn_ref[...], dcqn)
        dckv, dkn = _rms_bwd_tile(p_ref[:, QL:QL + KVL], kn_ref[...], dckvn)
        dp_ref[:, 0:QL] = dcq.astype(BF16)
        dp_ref[:, QL:QL + KVL] = dckv.astype(BF16)
        dp_ref[:, QL + KVL:QL + KVL + 128] = (dkr * c + _rope_swap(dkr * s)).astype(BF16)
        dqn_ref[...] += dqn
        dkn_ref[...] += dkn

    def row(w):
        return pl.BlockSpec((tm, w), lambda i: (i, 0))

    def full(shape):
        return pl.BlockSpec(shape, lambda i: (0, 0))

    return pl.pallas_call(
        body, name="mla_mid_bwd", grid=(T // tm,),
        in_specs=[row(PW), full((1, QL)), full((1, KVL)), full(w_uq.shape), full(w_ukv.shape),
                  row(128), row(128), row(H * MLA_QK_PAD), row(H * MLA_QK_PAD), row(H * MLA_V)],
        out_specs=(row(H * MLA_QK_PAD), row(H * MLA_QK_PAD), row(PW), full((1, QL)), full((1, KVL))),
        out_shape=(jax.ShapeDtypeStruct((T, H * MLA_QK_PAD), BF16),
                   jax.ShapeDtypeStruct((T, H * MLA_QK_PAD), BF16),
                   jax.ShapeDtypeStruct((T, PW), BF16),
                   jax.ShapeDtypeStruct((1, QL), F32), jax.ShapeDtypeStruct((1, KVL), F32)),
        compiler_params=_cparams("arbitrary"))(proj, q_norm, kv_norm, w_uq, w_ukv, cc, ss, dq, dk, dv)


def _attn_tile(T):
    return min(1024, T)


def _attn_pairs(n, by_key):
    if by_key:
        pairs = [(qi, ki) for ki in range(n) for qi in range(ki, n)]
    else:
        pairs = [(qi, ki) for qi in range(n) for ki in range(qi + 1)]
    return (jnp.asarray([p[0] for p in pairs], jnp.int32), jnp.asarray([p[1] for p in pairs], jnp.int32))


def _scores(q, k, diagonal):
    s = lax.dot_general(q, k, (((1,), (1,)), ((), ())), preferred_element_type=F32)
    if diagonal:
        rows = lax.broadcasted_iota(jnp.int32, s.shape, 0)
        cols = lax.broadcasted_iota(jnp.int32, s.shape, 1)
        s = jnp.where(rows >= cols, s, -jnp.inf)
    return s


def _attn_fwd(q, k, v):
    T = q.shape[0]
    H, DQ, DV = MLA_HEADS, MLA_QK_PAD, MLA_V
    tq = _attn_tile(T)
    nq = T // tq
    scale = float(MLA_NOPE + MLA_ROPE) ** -0.5
    G = MLA_HEADS_PER_STEP
    qi_tab, ki_tab = _attn_pairs(nq, by_key=False)

    def body(qi_ref, ki_ref, q_ref, k_ref, v_ref, o_ref, lse_ref, *scratch):
        m_refs, l_refs, acc_refs = scratch[0:G], scratch[G:2 * G], scratch[2 * G:3 * G]
        p = pl.program_id(1)
        qi, ki = qi_ref[p], ki_ref[p]

        @pl.when(ki == 0)
        def _():
            for g in range(G):
                m_refs[g][...] = jnp.full_like(m_refs[g], -jnp.inf)
                l_refs[g][...] = jnp.zeros_like(l_refs[g])
                acc_refs[g][...] = jnp.zeros_like(acc_refs[g])

        def update(diagonal):
            for g in range(G):
                qs, vs = slice(g * DQ, (g + 1) * DQ), slice(g * DV, (g + 1) * DV)
                s = _scores(q_ref[:, qs], k_ref[:, qs], diagonal)
                m_prev = m_refs[g][...]
                m_new = jnp.maximum(m_prev, jnp.max(s, axis=1, keepdims=True))
                alpha = jnp.exp(m_prev - m_new)
                pr = jnp.exp(s - m_new)
                l_refs[g][...] = alpha * l_refs[g][...] + jnp.sum(pr, axis=1, keepdims=True)
                acc_refs[g][...] = alpha * acc_refs[g][...] + jnp.dot(pr.astype(BF16), v_ref[:, vs],
                                                                      preferred_element_type=F32)
                m_refs[g][...] = m_new

        @pl.when(ki < qi)
        def _():
            update(False)

        @pl.when(ki == qi)
        def _():
            update(True)
            for g in range(G):
                vs = slice(g * DV, (g + 1) * DV)
                o_ref[:, vs] = (acc_refs[g][...] / l_refs[g][...]).astype(BF16)
                lse_ref[g] = m_refs[g][...] + jnp.log(l_refs[g][...])

    return pl.pallas_call(
        body, name="attn_fwd",
        grid_spec=pltpu.PrefetchScalarGridSpec(
            num_scalar_prefetch=2, grid=(H // G, int(qi_tab.shape[0])),
            in_specs=[pl.BlockSpec((tq, G * DQ), lambda h, p, qt, kt: (qt[p], h)),
                      pl.BlockSpec((tq, G * DQ), lambda h, p, qt, kt: (kt[p], h)),
                      pl.BlockSpec((tq, G * DV), lambda h, p, qt, kt: (kt[p], h))],
            out_specs=(pl.BlockSpec((tq, G * DV), lambda h, p, qt, kt: (qt[p], h)),
                       pl.BlockSpec((G, tq, 1), lambda h, p, qt, kt: (h, qt[p], 0))),
            scratch_shapes=([pltpu.VMEM((tq, 1), F32)] * (2 * G) + [pltpu.VMEM((tq, DV), F32)] * G)),
        out_shape=(jax.ShapeDtypeStruct((T, H * DV), BF16), jax.ShapeDtypeStruct((H, T, 1), F32)),
        compiler_params=_cparams("parallel", "arbitrary"))(qi_tab, ki_tab, q, k, v)


def _attn_bwd(q, k, v, o, do, lse):
    T = q.shape[0]
    H, DQ, DV = MLA_HEADS, MLA_QK_PAD, MLA_V
    tq = _attn_tile(T)
    nq = T // tq
    scale = float(MLA_NOPE + MLA_ROPE) ** -0.5
    tn = (((0,), (0,)), ((), ()))
    nt = (((1,), (1,)), ((), ()))
    G = MLA_HEADS_PER_STEP
    qi_tab, ki_tab = _attn_pairs(nq, by_key=True)

    def body(qi_ref, ki_ref, q_ref, k_ref, v_ref, o_ref, do_ref, lse_ref, dq_ref, dk_ref, dv_ref,
             dk_acc, dv_acc):
        p = pl.program_id(1)
        qi, ki = qi_ref[p], ki_ref[p]

        @pl.when(p == 0)
        def _():
            dq_ref[...] = jnp.zeros_like(dq_ref)

        @pl.when(qi == ki)
        def _():
            dk_acc[...] = jnp.zeros_like(dk_acc)
            dv_acc[...] = jnp.zeros_like(dv_acc)

        def step(diagonal):
            rows = pl.ds(pl.multiple_of(qi * tq, tq), tq)
            for g in range(G):
                qs, vs = slice(g * DQ, (g + 1) * DQ), slice(g * DV, (g + 1) * DV)
                dof = do_ref[:, vs]
                delta = jnp.sum(dof.astype(F32) * o_ref[:, vs].astype(F32), axis=1, keepdims=True)
                s = _scores(q_ref[:, qs], k_ref[:, qs], diagonal)
                pr = jnp.exp(s - lse_ref[g])
                dp = lax.dot_general(dof, v_ref[:, vs], nt, preferred_element_type=F32)
                ds = (pr * (dp - delta)).astype(BF16)
                dv_acc[:, vs] += lax.dot_general(pr.astype(BF16), dof, tn, preferred_element_type=F32)
                dk_acc[:, qs] += lax.dot_general(ds, q_ref[:, qs], tn, preferred_element_type=F32)
                dq_ref[rows, qs] += jnp.dot(ds, k_ref[:, qs], preferred_element_type=F32)

        @pl.when(qi == ki)
        def _():
            step(True)

        @pl.when(qi > ki)
        def _():
            step(False)

        @pl.when(qi == nq - 1)
        def _():
            dk_ref[...] = dk_acc[...]
            dv_ref[...] = dv_acc[...]

    qspec = pl.BlockSpec((tq, G * DQ), lambda h, p, qt, kt: (qt[p], h))
    ospec = pl.BlockSpec((tq, G * DV), lambda h, p, qt, kt: (qt[p], h))
    kspec = pl.BlockSpec((tq, G * DQ), lambda h, p, qt, kt: (kt[p], h))
    vspec = pl.BlockSpec((tq, G * DV), lambda h, p, qt, kt: (kt[p], h))
    return pl.pallas_call(
        body, name="attn_bwd",
        grid_spec=pltpu.PrefetchScalarGridSpec(
            num_scalar_prefetch=2, grid=(H // G, int(qi_tab.shape[0])),
            in_specs=[qspec, kspec, vspec, ospec, ospec,
                      pl.BlockSpec((G, tq, 1), lambda h, p, qt, kt: (h, qt[p], 0))],
            out_specs=(pl.BlockSpec((T, G * DQ), lambda h, p, qt, kt: (0, h)), kspec, vspec),
            scratch_shapes=[pltpu.VMEM((tq, G * DQ), F32), pltpu.VMEM((tq, G * DV), F32)]),
        out_shape=(jax.ShapeDtypeStruct((T, H * DQ), F32), jax.ShapeDtypeStruct((T, H * DQ), F32),
                   jax.ShapeDtypeStruct((T, H * DV), F32)),
        compiler_params=_cparams("parallel", "arbitrary"))(qi_tab, ki_tab, q, k, v, o, do, lse)


def _mla_fwd(a, w, cc, ss, wbuf, pk, slot):
    D = a.shape[1]
    by_k, _, _ = _row_sharded(pk, "mla_w_o", slot, D // N_CHIPS)
    proj = _mm(a, w["w_in"], name="mla_in")
    cqn, ckvn, q, k, v = _mla_mid_fwd(proj, w["q_norm"], w["kv_norm"], w["w_uq"], w["w_ukv"], cc, ss)
    o, lse = _attn_fwd(q, k, v)
    m = _mm(o, wbuf, n=D, b_map=by_k, tm=2048, tk=D // N_CHIPS, tn=D, name="mla_out")
    return m, (a, proj, cqn, ckvn, q, k, v, o, lse)


def _mla_bwd(dm, saved, w, cc, ss, wbuf, gbuf, pk, slot):
    a, proj, cqn, ckvn, q, k, v, o, lse = saved
    D = a.shape[1]
    _, by_n, by_m = _row_sharded(pk, "mla_w_o", slot, D // N_CHIPS)
    do = _mm(dm, wbuf, tb=True, n=o.shape[1], b_map=by_n, tm=2048, tn=D // N_CHIPS, tk=D, out_dtype=BF16,
             name="mla_out_dx")
    gbuf = _mm(o, dm, ta=True, into=gbuf, o_map=by_m, tm=D // N_CHIPS, tn=D, tk=2048, name="mla_out_dw")
    dq, dk, dv = _attn_bwd(q, k, v, o, do, lse)
    dqp, dkv, dproj, dqn, dkn = _mla_mid_bwd(proj, w["q_norm"], w["kv_norm"], w["w_uq"], w["w_ukv"],
                                             cc, ss, dq, dk, dv)
    dw_uq = _mm(cqn, dqp, ta=True, out_dtype=BF16, name="mla_uq_dw")
    dw_ukv = _mm(ckvn, dkv, ta=True, out_dtype=BF16, name="mla_ukv_dw")
    dw_in = _mm(a, dproj, ta=True, out_dtype=BF16, name="mla_in_dw")
    da = _mm(dproj, w["w_in"], tb=True, name="mla_in_dx")
    return da, gbuf, dict(w_in=dw_in, w_uq=dw_uq, w_ukv=dw_ukv, q_norm=dqn, kv_norm=dkn)


def _split_dot(mat, x, parts):
    acc = None
    rem = x
    for _ in range(parts):
        piece = rem.astype(BF16)
        term = jnp.dot(mat, piece, preferred_element_type=F32)
        acc = term if acc is None else acc + term
        rem = rem - piece.astype(F32)
    return acc


def _chunk_mats(tb):
    C = HGRN_CHUNK
    assert C & (C - 1) == 0
    r = lax.broadcasted_iota(jnp.int32, (tb, tb), 0)
    s = lax.broadcasted_iota(jnp.int32, (tb, tb), 1)
    start = r & ~(C - 1)
    same = start == (s & ~(C - 1))
    ref = start + C // 2
    last = start + C - 1
    one, zero = jnp.float32(1.0), jnp.float32(0.0)
    cum = jnp.where(same & (s <= r), one, zero)
    rel = cum - jnp.where(same & (s <= ref), one, zero)
    rest = jnp.where(same & (s > r) & (s <= last), one, zero)
    rev = jnp.where(same & (s >= r), one, zero)
    ones = jnp.where(same, one, zero)
    causal = same & (s <= r)
    return cum, rel, rest, rev, ones, causal


def _hgrn_gates(p_ref, lb, HK):
    qx = p_ref[:, 0:HK]
    fx = p_ref[:, HK:2 * HK]
    sf = _sigmoid(fx)
    f = lb + (1.0 - lb) * sf
    sq = _sigmoid(qx)
    return qx, sq, qx * sq, sf, f, 1.0 - f, jnp.log(f)


def _hgrn_fwd(proj, lb, o_norm):
    T = proj.shape[0]
    H, C = HGRN_HEADS, HGRN_CHUNK
    HK = proj.shape[1] // 4
    DK = HK // H
    tb = min(HGRN_BLOCK, T)
    ncb = tb // C
    nt = (((1,), (1,)), ((), ()))
    tn = (((0,), (0,)), ((), ()))

    def body(p_ref, lb_ref, on_ref, y_ref, o_ref, st_ref, state, oacc):
        @pl.when(pl.program_id(0) == 0)
        def _():
            state[...] = jnp.zeros_like(state)

        cum, rel, rest, _, _, causal = _chunk_mats(tb)
        _, _, q, _, f, k, logf = _hgrn_gates(p_ref, lb_ref[...], HK)
        b = _split_dot(cum.astype(BF16), logf, 3)
        brel = _split_dot(rel.astype(BF16), logf, 3)
        brest = _split_dot(rest.astype(BF16), logf, 3)
        eb = jnp.exp(b)
        q_rel = (q * jnp.exp(brel)).astype(BF16)
        k_rel = (k * jnp.exp(-brel)).astype(BF16)
        q_dec = (q * eb).astype(BF16)
        k_dec = (k * jnp.exp(brest)).astype(BF16)
        v = p_ref[:, 2 * HK:3 * HK].astype(BF16)
        for h in range(H):
            hs = slice(h * DK, (h + 1) * DK)
            a = lax.dot_general(q_rel[:, hs], k_rel[:, hs], nt, preferred_element_type=F32)
            a = jnp.where(causal, a, 0.0).astype(BF16)
            oacc[:, hs] = jnp.dot(a, v[:, hs], preferred_element_type=F32)
            for j in range(ncb):
                rs = slice(j * C, (j + 1) * C)
                st = state[h]
                st_ref[j, h] = st
                oacc[rs, hs] += lax.dot_general(q_dec[rs, hs], st.astype(BF16), nt,
                                                preferred_element_type=F32)
                dec = jnp.exp(jnp.sum(logf[rs, hs], axis=0, keepdims=True))
                state[h] = dec * st + lax.dot_general(v[rs, hs], k_dec[rs, hs], tn,
                                                      preferred_element_type=F32)
        o = oacc[...]
        o_ref[...] = o
        gx = p_ref[:, 3 * HK:4 * HK]
        gate = gx * _sigmoid(gx)
        for h in range(H):
            hs = slice(h * DK, (h + 1) * DK)
            oh = o[:, hs]
            y_ref[:, hs] = (oh * _rms_rstd(oh) * on_ref[...] * gate[:, hs]).astype(BF16)

    return pl.pallas_call(
        body, name="hgrn_fwd", grid=(T // tb,),
        in_specs=[pl.BlockSpec((tb, 4 * HK), lambda i: (i, 0)),
                  pl.BlockSpec((1, HK), lambda i: (0, 0)),
                  pl.BlockSpec((1, DK), lambda i: (0, 0))],
        out_specs=(pl.BlockSpec((tb, HK), lambda i: (i, 0)),
                   pl.BlockSpec((tb, HK), lambda i: (i, 0)),
                   pl.BlockSpec((ncb, H, DK, DK), lambda i: (i, 0, 0, 0))),
        out_shape=(jax.ShapeDtypeStruct((T, HK), BF16), jax.ShapeDtypeStruct((T, HK), F32),
                   jax.ShapeDtypeStruct((T // C, H, DK, DK), F32)),
        scratch_shapes=[pltpu.VMEM((H, DK, DK), F32), pltpu.VMEM((tb, HK), F32)],
        compiler_params=_cparams("arbitrary"))(proj, lb, o_norm)


def _hgrn_bwd(proj, lb, o_norm, o, states, dy):
    T = proj.shape[0]
    H, C = HGRN_HEADS, HGRN_CHUNK
    HK = proj.shape[1] // 4
    DK = HK // H
    tb = min(HGRN_BLOCK, T)
    ncb = tb // C
    nb = T // tb
    nt = (((1,), (1,)), ((), ()))
    tn = (((0,), (0,)), ((), ()))

    def body(p_ref, lb_ref, on_ref, o_ref, st_ref, dy_ref, dp_ref, dlb_ref, don_ref,
             dstate, dqr_s, dkr_s, dqd_s, dkd_s, dv_s, do_s, e_s):
        @pl.when(pl.program_id(0) == 0)
        def _():
            dstate[...] = jnp.zeros_like(dstate)
            dlb_ref[...] = jnp.zeros_like(dlb_ref)
            don_ref[...] = jnp.zeros_like(don_ref)

        cum, rel, rest, rev, ones, causal = _chunk_mats(tb)
        lb = lb_ref[...]
        qx, sq, q, sf, f, k, logf = _hgrn_gates(p_ref, lb, HK)
        b = _split_dot(cum.astype(BF16), logf, 3)
        brel = _split_dot(rel.astype(BF16), logf, 3)
        brest = _split_dot(rest.astype(BF16), logf, 3)
        eb = jnp.exp(b)
        erel = jnp.exp(brel)
        enrel = jnp.exp(-brel)
        erest = jnp.exp(brest)
        q_rel_f, k_rel_f, q_dec_f, k_dec_f = q * erel, k * enrel, q * eb, k * erest
        q_rel, k_rel = q_rel_f.astype(BF16), k_rel_f.astype(BF16)
        q_dec, k_dec = q_dec_f.astype(BF16), k_dec_f.astype(BF16)
        v = p_ref[:, 2 * HK:3 * HK].astype(BF16)

        gx = p_ref[:, 3 * HK:4 * HK]
        sg = _sigmoid(gx)
        gate = gx * sg
        dy = dy_ref[...]
        ov = o_ref[...]
        on = on_ref[...]
        don = jnp.zeros((1, DK), F32)
        for h in range(H):
            hs = slice(h * DK, (h + 1) * DK)
            oh = ov[:, hs]
            r = _rms_rstd(oh)
            xh = oh * r
            d_on = dy[:, hs] * gate[:, hs]
            don = don + jnp.sum(d_on * xh, axis=0, keepdims=True)
            u = d_on * on
            do_s[:, hs] = r * (u - xh * jnp.mean(u * xh, axis=-1, keepdims=True))
            dp_ref[:, 3 * HK + h * DK:3 * HK + (h + 1) * DK] = (
                dy[:, hs] * xh * on * (sg[:, hs] * (1.0 + gx[:, hs] * (1.0 - sg[:, hs])))).astype(BF16)
        don_ref[...] += don

        for h in range(H):
            hs = slice(h * DK, (h + 1) * DK)
            doh = do_s[:, hs].astype(BF16)
            a = lax.dot_general(q_rel[:, hs], k_rel[:, hs], nt, preferred_element_type=F32)
            a = jnp.where(causal, a, 0.0).astype(BF16)
            da = lax.dot_general(doh, v[:, hs], nt, preferred_element_type=F32)
            da = jnp.where(causal, da, 0.0).astype(BF16)
            dv_s[:, hs] = lax.dot_general(a, doh, tn, preferred_element_type=F32)
            dqr_s[:, hs] = jnp.dot(da, k_rel[:, hs], preferred_element_type=F32)
            dkr_s[:, hs] = lax.dot_general(da, q_rel[:, hs], tn, preferred_element_type=F32)
            for j in reversed(range(ncb)):
                rs = slice(j * C, (j + 1) * C)
                dst = dstate[h]
                dstb = dst.astype(BF16)
                st = st_ref[j, h]
                dkd_s[rs, hs] = jnp.dot(v[rs, hs], dstb, preferred_element_type=F32)
                dv_s[rs, hs] += lax.dot_general(k_dec[rs, hs], dstb, nt, preferred_element_type=F32)
                dec = jnp.exp(jnp.sum(logf[rs, hs], axis=0, keepdims=True))
                e_s[rs, hs] = jnp.broadcast_to(jnp.sum(dst * st, axis=0, keepdims=True) * dec, (C, DK))
                dqd_s[rs, hs] = jnp.dot(doh[rs], st.astype(BF16), preferred_element_type=F32)
                dstate[h] = dec * dst + lax.dot_general(doh[rs], q_dec[rs, hs], tn,
                                                        preferred_element_type=F32)

        dqr, dkr, dqd, dkd = dqr_s[...], dkr_s[...], dqd_s[...], dkd_s[...]
        kdk = dkd * k_dec_f
        db = dqr * q_rel_f - dkr * k_rel_f + dqd * q_dec_f - kdk
        dlogf = _split_dot(rev.astype(BF16), db, 2) + _split_dot(ones.astype(BF16), kdk, 2) + e_s[...]
        dk = dkr * enrel + dkd * erest
        df = dlogf / f - dk
        dlb_ref[...] += jnp.sum(df * (1.0 - sf), axis=0, keepdims=True)
        dq = dqr * erel + dqd * eb
        dp_ref[:, 0:HK] = (dq * (sq * (1.0 + qx * (1.0 - sq)))).astype(BF16)
        dp_ref[:, HK:2 * HK] = (df * (1.0 - lb) * sf * (1.0 - sf)).astype(BF16)
        dp_ref[:, 2 * HK:3 * HK] = dv_s[...].astype(BF16)

    rev_row = lambda w: pl.BlockSpec((tb, w), lambda i: (nb - 1 - i, 0))
    vec = lambda w: pl.BlockSpec((1, w), lambda i: (0, 0))
    scr = pltpu.VMEM((tb, HK), F32)
    return pl.pallas_call(
        body, name="hgrn_bwd", grid=(nb,),
        in_specs=[rev_row(4 * HK), vec(HK), vec(DK), rev_row(HK),
                  pl.BlockSpec((ncb, H, DK, DK), lambda i: (nb - 1 - i, 0, 0, 0)), rev_row(HK)],
        out_specs=(rev_row(4 * HK), vec(HK), vec(DK)),
        out_shape=(jax.ShapeDtypeStruct((T, 4 * HK), BF16), jax.ShapeDtypeStruct((1, HK), F32),
                   jax.ShapeDtypeStruct((1, DK), F32)),
        scratch_shapes=[pltpu.VMEM((H, DK, DK), F32), scr, scr, scr, scr, scr, scr, scr],
        compiler_params=_cparams("arbitrary"))(proj, lb, o_norm, o, states, dy)


def _hgrn_layer_fwd(a, o_norm, lb, wbuf, pk, slot):
    D = a.shape[1]
    in_by_n, _ = _col_sharded(pk, "hgrn_w_in", slot, D)
    out_by_k, _, _ = _row_sharded(pk, "hgrn_w_o", slot, D // N_CHIPS)
    proj = _mm(a, wbuf, n=4 * D, b_map=in_by_n, tk=D, tn=D, name="hgrn_in")
    y, o, states = _hgrn_fwd(proj, lb, o_norm)
    m = _mm(y, wbuf, n=D, b_map=out_by_k, tm=2048, tk=D // N_CHIPS, tn=D, name="hgrn_out")
    return m, (a, proj, y, o, states)


def _hgrn_layer_bwd(dm, saved, o_norm, lb, wbuf, gbuf, pk, slot):
    a, proj, y, o, states = saved
    D = a.shape[1]
    in_by_n, in_by_k = _col_sharded(pk, "hgrn_w_in", slot, D)
    _, out_by_n, out_by_m = _row_sharded(pk, "hgrn_w_o", slot, D // N_CHIPS)
    dy = _mm(dm, wbuf, tb=True, n=y.shape[1], b_map=out_by_n, tm=2048, tn=D // N_CHIPS, tk=D,
             name="hgrn_out_dx")
    gbuf = _mm(y, dm, ta=True, into=gbuf, o_map=out_by_m, tm=D // N_CHIPS, tn=D, tk=2048, name="hgrn_out_dw")
    dproj, dlb, don = _hgrn_bwd(proj, lb, o_norm, o, states, dy)
    gbuf = _mm(a, dproj, ta=True, into=gbuf, o_map=in_by_n, tm=D, tn=D, name="hgrn_in_dw")
    da = _mm(dproj, wbuf, tb=True, n=D, b_map=in_by_k, tn=D, tk=D, name="hgrn_in_dx")
    return da, gbuf, dict(o_norm=don, lb=dlb)


def _lower_bounds(lb_logits):
    p = jax.nn.softmax(lb_logits.astype(F32), axis=0)
    return jnp.cumsum(p, axis=0) - p[0]


def _rope_tables(positions):
    inv_freq = jnp.power(ROPE_BASE, -jnp.arange(0, MLA_ROPE, 2, dtype=F32) / MLA_ROPE)
    ang = positions.astype(F32)[:, None] * inv_freq
    cos, sin = jnp.cos(ang), jnp.sin(ang)
    zero = jnp.zeros((positions.shape[0], 128 - MLA_ROPE), F32)
    return (jnp.concatenate([cos, cos, zero], axis=-1), jnp.concatenate([-sin, sin, zero], axis=-1))


def _pad_mla_weights(w_in, w_uq):
    w_in_p = jnp.pad(w_in, ((0, 0), (0, 0), (0, 128 - MLA_ROPE)))
    n, ql, _ = w_uq.shape
    w_uq_p = jnp.pad(w_uq.reshape(n, ql, MLA_HEADS, MLA_NOPE + MLA_ROPE),
                     ((0, 0), (0, 0), (0, 0), (0, MLA_QK_PAD - MLA_NOPE - MLA_ROPE)))
    return w_in_p, w_uq_p.reshape(n, ql, MLA_HEADS * MLA_QK_PAD)


def _local_step(x, positions, target, small, fetch, gbufs, emit, emit_mlp):
    T, D = x.shape
    lbounds, lb_vjp = jax.vjp(_lower_bounds, small["hgrn_lb_logits"])
    cc, ss = _rope_tables(positions)
    fetched = {0: fetch(0, None)}
    gains = fetched[0]["gains"]
    tick = [jnp.zeros((), F32)]

    def g(layer, i):
        return gains[layer, i][None, :] + tick[0]

    def mla_weights(layer):
        f = fetched[layer]
        w_in_p, w_uq_p = _pad_mla_weights(f["w_in"][None], f["w_uq"][None])
        slot = layer // 2
        return dict(w_in=w_in_p[0], w_uq=w_uq_p[0], w_ukv=f["w_ukv"],
                    q_norm=small["mla_q_norm"][slot][None, :], kv_norm=small["mla_kv_norm"][slot][None, :])

    saved = []
    h = x
    a = _prenorm_fwd(x, g(0, 0))
    dy = sq = None
    for layer in range(DEPTH):
        slot = layer // 2
        if layer not in fetched:
            fetched[layer] = fetch(layer, a)
        wbuf, pk = fetched[layer]["wbuf"], fetched[layer]["pk"]
        if layer % 2 == 0:
            m, mix_saved = _mla_fwd(a, mla_weights(layer), cc, ss, wbuf, pk, slot)
        else:
            m, mix_saved = _hgrn_layer_fwd(a, small["hgrn_o_norm"][slot][None, :], lbounds[layer][None, :],
                                           wbuf, pk, slot)
        h1, a2 = _resnorm_fwd(h, m, g(layer, 1), g(layer, 2), name="resnorm_fwd_mix")
        u, mlp_saved = _mlp_fwd(a2, wbuf, pk, layer)
        if layer + 1 < DEPTH:
            h2, a = _resnorm_fwd(h1, u, g(layer, 3), g(layer + 1, 0), name="resnorm_fwd_mlp")
        else:
            h2 = None
            dy, sq = _resnorm_loss(h1, u, g(layer, 3), target)
        saved.append((h, m, h1, u, mix_saved, mlp_saved))
        h = h2

    n_mla, n_hgrn = (DEPTH + 1) // 2, DEPTH // 2
    dgains = [[None] * 4 for _ in range(DEPTH)]
    gw = {k: [None] * n_mla for k in ("mla_w_in", "mla_w_uq", "mla_w_ukv", "mla_q_norm", "mla_kv_norm")}
    gw["hgrn_o_norm"] = [None] * n_hgrn
    dlb = [jnp.zeros((1, lbounds.shape[1]), F32) for _ in range(DEPTH)]
    dh = dy
    da_next = None
    for layer in reversed(range(DEPTH)):
        h0, m, h1, u, mix_saved, mlp_saved = saved[layer]
        slot = layer // 2
        wbuf, pk, gbuf = fetched[layer]["wbuf"], fetched[layer]["pk"], gbufs[layer]
        if da_next is None:
            du, dgains[layer][3] = _resnorm_bwd(u, g(layer, 3), dh, name="resnorm_bwd_last")
            t = dh
        else:
            h2 = saved[layer + 1][0]
            t, du, dgains[layer][3], dgains[layer + 1][0] = _resnorm_bwd(
                u, g(layer, 3), dh, h2, da_next, g(layer + 1, 0), name="resnorm_bwd_mlp")
        da2, gbuf = _mlp_bwd(du, mlp_saved, wbuf, gbuf, pk, layer)
        if layer == 0:
            gbuf = emit_mlp(layer, gbuf)
        t, dm, dgains[layer][1], dgains[layer][2] = _resnorm_bwd(
            m, g(layer, 1), t, h1, da2, g(layer, 2), name="resnorm_bwd_mix")
        if layer % 2 == 0:
            da_next, gbuf, mg = _mla_bwd(dm, mix_saved, mla_weights(layer), cc, ss, wbuf, gbuf, pk, slot)
            ql = mg["q_norm"].shape[-1]
            kvl = mg["kv_norm"].shape[-1]
            gw["mla_w_in"][slot] = mg["w_in"][:, :ql + kvl + MLA_ROPE]
            gw["mla_w_uq"][slot] = mg["w_uq"].reshape(ql, MLA_HEADS, MLA_QK_PAD)[
                :, :, :MLA_NOPE + MLA_ROPE].reshape(ql, MLA_HEADS * (MLA_NOPE + MLA_ROPE))
            gw["mla_w_ukv"][slot] = mg["w_ukv"]
            gw["mla_q_norm"][slot] = mg["q_norm"][0]
            gw["mla_kv_norm"][slot] = mg["kv_norm"][0]
        else:
            da_next, gbuf, hg = _hgrn_layer_bwd(dm, mix_saved, small["hgrn_o_norm"][slot][None, :],
                                                lbounds[layer][None, :], wbuf, gbuf, pk, slot)
            gw["hgrn_o_norm"][slot] = hg["o_norm"][0]
            dlb[layer] = hg["lb"]
        dh = t
        if layer > 0:
            mine = ({k: gw[k][slot] for k in ("mla_w_in", "mla_w_uq", "mla_w_ukv")} if layer % 2 == 0 else {})
            tick[0] = emit(layer, gbuf, mine)
        else:
            gbuf0 = gbuf
    grad_x, dgains[0][0] = _prenorm_bwd(x, g(0, 0), dh, da_next)

    last = {k: gw[k][0] for k in ("mla_w_in", "mla_w_uq", "mla_w_ukv")}
    last.update({k: jnp.stack(gw[k]) for k in ("mla_q_norm", "mla_kv_norm", "hgrn_o_norm")})
    last["norm_gains"] = jnp.stack([jnp.concatenate(row, axis=0) for row in dgains])
    (last["hgrn_lb_logits"],) = lb_vjp(jnp.concatenate(dlb, axis=0))
    emit(0, gbuf0, last)
    return sq, grad_x


def _size(shape):
    n = 1
    for d in shape:
        n *= d
    return n


def _piece_rows(shape):
    return -(-_size(shape) // PACK_W)


def _packed_misc_rows(shapes):
    return sum(_piece_rows(s) for s in shapes)


def _cast_into(src, buf, row, name):
    rows, W = src.shape
    tr = min(256, rows)
    assert rows % tr == 0 and row % tr == 0

    def body(s_ref, b_ref, o_ref):
        o_ref[...] = s_ref[...].astype(BF16)

    return pl.pallas_call(
        body, name=name, grid=(rows // tr,),
        in_specs=[pl.BlockSpec((tr, W), lambda i: (i, 0)), pl.BlockSpec(memory_space=pl.ANY)],
        out_specs=pl.BlockSpec((tr, W), lambda i: (row // tr + i, 0)),
        out_shape=jax.ShapeDtypeStruct(buf.shape, buf.dtype), input_output_aliases={1: 0},
        compiler_params=_cparams("parallel"))(src, buf)


def _pack_blocks(pieces, rows, dtype):
    blocks, used = [], 0
    for p in pieces:
        flat = p.astype(dtype).reshape(-1)
        r = _piece_rows(p.shape)
        if r * PACK_W != flat.shape[0]:
            flat = jnp.pad(flat, (0, r * PACK_W - flat.shape[0]))
        blocks.append(flat.reshape(r, PACK_W))
        used += r
    if rows > used:
        blocks.append(jnp.zeros((rows - used, PACK_W), dtype))
    return blocks


def _unpack(buf, shapes):
    out, off = [], 0
    for shp in shapes:
        r = _piece_rows(shp)
        piece = buf[off:off + r]
        if r * PACK_W != _size(shp):
            piece = piece.reshape(-1)[:_size(shp)]
        out.append(piece.reshape(shp))
        off += r
    return out


def _mesh_place():
    x, y, c = lax.axis_index("x"), lax.axis_index("y"), lax.axis_index("c")
    chips = [(1 - x, y), (x, 1 - y), (1 - x, 1 - y)]
    return x, y, c, chips


_HBM = pl.BlockSpec(memory_space=pltpu.HBM)


def _all_gather(wp):
    R, W = wp.shape
    rh = R // 2
    rq = rh // 2
    assert rq % 16 == 0

    def body(w_ref, out_ref, send_sems, recv_sems):
        x, y, c, _ = _mesh_place()
        me, jx, jy, jd = 2 * x + y, 2 * (1 - x) + y, 2 * x + (1 - y), 2 * (1 - x) + (1 - y)
        to_x, to_y, sibling = (1 - x, y, c), (x, 1 - y, c), (x, y, 1 - c)

        def rows(core, quarter):
            return pl.ds(pl.multiple_of(core * rh + quarter * rq, 16), rq)

        def slot(j, core, quarter):
            return out_ref.at[j, rows(core, quarter)]

        def copy(k, src, dst, to):
            return pltpu.make_async_remote_copy(src_ref=src, dst_ref=dst, send_sem=send_sems.at[k],
                                                recv_sem=recv_sems.at[k], device_id=to, device_id_type=MESH)

        sends = [copy(0, w_ref.at[rows(c, 0)], slot(me, c, 0), to_x),
                 copy(2, w_ref.at[rows(c, 1)], slot(me, c, 1), to_y),
                 copy(1, w_ref.at[rows(c, 1)], slot(me, c, 1), to_x),
                 copy(3, w_ref.at[rows(c, 0)], slot(me, c, 0), to_y)]
        for cp in sends:
            cp.start()
        arrivals = [(0, slot(jx, c, 0), 4, to_y, 6), (2, slot(jy, c, 1), 5, to_x, 7),
                    (1, slot(jx, c, 1), None, None, 8), (3, slot(jy, c, 0), None, None, 9),
                    (4, slot(jd, c, 0), None, None, 10), (5, slot(jd, c, 1), None, None, 11)]
        for k, landed, k_on, to_on, k_sib in arrivals:
            copy(k, landed, landed, sibling).wait_recv()
            if k_on is not None:
                cp = copy(k_on, landed, landed, to_on)
                cp.start()
                sends.append(cp)
            cp = copy(k_sib, landed, landed, sibling)
            cp.start()
            sends.append(cp)
        for k_sib, j, quarter in ((6, jx, 0), (7, jy, 1), (8, jx, 1), (9, jy, 0), (10, jd, 0), (11, jd, 1)):
            landed = slot(j, 1 - c, quarter)
            copy(k_sib, landed, landed, sibling).wait_recv()
        for cp in sends:
            cp.wait_send()

    out = pl.pallas_call(
        body, name="weights_all_gather", in_specs=[_HBM], out_specs=_HBM,
        out_shape=jax.ShapeDtypeStruct((N_CHIPS, R, W), wp.dtype),
        scratch_shapes=[pltpu.SemaphoreType.DMA((12,)), pltpu.SemaphoreType.DMA((12,))],
    )(wp)
    me = 2 * lax.axis_index("x") + lax.axis_index("y")
    return lax.dynamic_update_slice(out, wp[None], (me, 0, 0))


def _exchange_halves(g):
    n, _, rh, W = g.shape

    def body(g_ref, out_ref, send_sems, recv_sems):
        x, y, c, _ = _mesh_place()
        sibling = (x, y, 1 - c)
        copies = [pltpu.make_async_remote_copy(
            src_ref=g_ref.at[j, 1 - c], dst_ref=out_ref.at[j], send_sem=send_sems.at[j],
            recv_sem=recv_sems.at[j], device_id=sibling, device_id_type=MESH) for j in range(n)]
        for cp in copies:
            cp.start()
        for cp in copies:
            cp.wait()

    return pl.pallas_call(
        body, name="grads_to_sibling", in_specs=[_HBM], out_specs=_HBM,
        out_shape=jax.ShapeDtypeStruct((n, rh, W), g.dtype),
        scratch_shapes=[pltpu.SemaphoreType.DMA((n,)), pltpu.SemaphoreType.DMA((n,))],
    )(g)


def _scatter_to_owners(p):
    n, rh, W = p.shape
    rq = rh // 2
    assert rq % 16 == 0

    def body(p_ref, out_ref, stage_ref, send_sems, recv_sems):
        x, y, c, _ = _mesh_place()
        me, jx, jy, jd = 2 * x + y, 2 * (1 - x) + y, 2 * x + (1 - y), 2 * (1 - x) + (1 - y)
        to_x, to_y = (1 - x, y, c), (x, 1 - y, c)

        def quarter(ref, j, q):
            return ref.at[j, pl.ds(q * rq, rq)]

        def copy(k, src, dst, to):
            return pltpu.make_async_remote_copy(src_ref=src, dst_ref=dst, send_sem=send_sems.at[k],
                                                recv_sem=recv_sems.at[k], device_id=to, device_id_type=MESH)

        sends = [copy(2, quarter(p_ref, jd, 0), stage_ref.at[0], to_x),
                 copy(3, quarter(p_ref, jd, 1), stage_ref.at[1], to_y),
                 copy(0, p_ref.at[jx], out_ref.at[me], to_x),
                 copy(1, p_ref.at[jy], out_ref.at[me], to_y)]
        for cp in sends:
            cp.start()
        copy(2, stage_ref.at[0], stage_ref.at[0], to_x).wait_recv()
        relay = copy(4, stage_ref.at[0], quarter(out_ref, jx, 0), to_y)
        relay.start()
        sends.append(relay)
        copy(3, stage_ref.at[1], stage_ref.at[1], to_y).wait_recv()
        relay = copy(5, stage_ref.at[1], quarter(out_ref, jy, 1), to_x)
        relay.start()
        sends.append(relay)
        copy(0, out_ref.at[jx], out_ref.at[jx], to_x).wait_recv()
        copy(1, out_ref.at[jy], out_ref.at[jy], to_y).wait_recv()
        copy(4, quarter(out_ref, jd, 0), quarter(out_ref, jd, 0), to_y).wait_recv()
        copy(5, quarter(out_ref, jd, 1), quarter(out_ref, jd, 1), to_x).wait_recv()
        for cp in sends:
            cp.wait_send()

    out, _ = pl.pallas_call(
        body, name="grads_to_owner", in_specs=[_HBM], out_specs=(_HBM, _HBM),
        out_shape=(jax.ShapeDtypeStruct((n, rh, W), p.dtype), jax.ShapeDtypeStruct((2, rq, W), p.dtype)),
        scratch_shapes=[pltpu.SemaphoreType.DMA((6,)), pltpu.SemaphoreType.DMA((6,))],
    )(p)
    me = 2 * lax.axis_index("x") + lax.axis_index("y")
    mine = lax.dynamic_index_in_dim(p, me, axis=0, keepdims=True)
    return lax.dynamic_update_slice(out, mine, (me, 0, 0))


def _share_reduced(q, name="grads_share_reduced"):
    rh, W = q.shape

    def body(q_ref, out_ref, send_sem, recv_sem):
        x, y, c, _ = _mesh_place()
        cp = pltpu.make_async_remote_copy(src_ref=q_ref, dst_ref=out_ref.at[c], send_sem=send_sem,
                                          recv_sem=recv_sem, device_id=(x, y, 1 - c), device_id_type=MESH)
        cp.start()
        cp.wait()

    out = pl.pallas_call(
        body, name=name, in_specs=[_HBM], out_specs=_HBM,
        out_shape=jax.ShapeDtypeStruct((2, rh, W), q.dtype),
        scratch_shapes=[pltpu.SemaphoreType.DMA, pltpu.SemaphoreType.DMA],
    )(q)
    return lax.dynamic_update_slice(out, q[None], (lax.axis_index("c"), 0, 0))


def _add_sibling(g, recv, c_arr):
    n, _, rh, W = g.shape
    tr = PACK_TILE

    def body(c_ref, g_ref, r_ref, o_ref):
        o_ref[...] = (g_ref[...].astype(F32) + r_ref[...].astype(F32)).astype(BF16)

    return pl.pallas_call(
        body, name="grads_add_sibling",
        grid_spec=pltpu.PrefetchScalarGridSpec(
            num_scalar_prefetch=1, grid=(n, rh // tr),
            in_specs=[pl.BlockSpec((None, None, tr, W), lambda j, i, c_ref: (j, c_ref[0], i, 0)),
                      pl.BlockSpec((None, tr, W), lambda j, i, c_ref: (j, i, 0))],
            out_specs=pl.BlockSpec((None, tr, W), lambda j, i, c_ref: (j, i, 0))),
        out_shape=jax.ShapeDtypeStruct((n, rh, W), BF16),
        compiler_params=_cparams("parallel", "parallel"))(c_arr, g, recv)


def _sum_chips(parts, name="grads_sum_chips"):
    n, rh, W = parts.shape
    tr = PACK_TILE

    def body(p_ref, o_ref):
        acc = p_ref[0].astype(F32)
        for j in range(1, n):
            acc = acc + p_ref[j].astype(F32)
        o_ref[...] = acc

    return pl.pallas_call(
        body, name=name, grid=(rh // tr,),
        in_specs=[pl.BlockSpec((n, tr, W), lambda i: (0, i, 0))],
        out_specs=pl.BlockSpec((tr, W), lambda i: (i, 0)),
        out_shape=jax.ShapeDtypeStruct((rh, W), F32),
        compiler_params=_cparams("parallel"))(parts)


_SEM = pl.BlockSpec(memory_space=pltpu.SEMAPHORE)
_ASYNC = pltpu.CompilerParams(has_side_effects=pltpu.SideEffectType.DATAFLOW_SIDE_EFFECTING)


def _hbm(a):
    return pltpu.with_memory_space_constraint(a, pltpu.HBM)


def _gather_copies(w_ref, land_ref, send_sems, recv_sems):
    x, y, c, chips = _mesh_place()
    me = 2 * x + y
    rh = w_ref.shape[0] // 2
    rows = pl.ds(pl.multiple_of(c * rh, 16), rh)
    return [pltpu.make_async_remote_copy(
        src_ref=w_ref.at[rows], dst_ref=land_ref.at[me, rows], send_sem=send_sems.at[r],
        recv_sem=recv_sems.at[r], device_id=(px, py, c), device_id_type=MESH)
        for r, (px, py) in enumerate(chips)]


def _scatter_copies(g_ref, land_ref, send_sems, recv_sems, row0):
    x, y, c, chips = _mesh_place()
    me = 2 * x + y
    rows = pl.ds(row0, land_ref.shape[1])
    return [pltpu.make_async_remote_copy(
        src_ref=g_ref.at[2 * px + py, rows], dst_ref=land_ref.at[me], send_sem=send_sems.at[r],
        recv_sem=recv_sems.at[r], device_id=(px, py, c), device_id_type=MESH)
        for r, (px, py) in enumerate(chips)]


def _halves_to_sibling(land, name):
    n, R, W = land.shape
    rh = R // 2

    def body(l_ref, o_ref, send_sems, recv_sems):
        x, y, c, chips = _mesh_place()
        rows = pl.ds(pl.multiple_of(c * rh, 16), rh)
        copies = [pltpu.make_async_remote_copy(
            src_ref=o_ref.at[2 * px + py, rows], dst_ref=o_ref.at[2 * px + py, rows], send_sem=send_sems.at[r],
            recv_sem=recv_sems.at[r], device_id=(x, y, 1 - c), device_id_type=MESH)
            for r, (px, py) in enumerate(chips)]
        for cp in copies:
            cp.start()
        for cp in copies:
            cp.wait()

    return pl.pallas_call(
        body, name=name, in_specs=[_HBM], out_specs=_HBM, out_shape=jax.ShapeDtypeStruct(land.shape, land.dtype),
        scratch_shapes=[pltpu.SemaphoreType.DMA((3,)), pltpu.SemaphoreType.DMA((3,))],
        input_output_aliases={0: 0})(land)


def _gather_start(wp, name):
    R, W = wp.shape

    def body(w_ref, land_ref, send_sems, recv_sems, w_thru, land_thru, token):
        for cp in _gather_copies(w_ref, land_ref, send_sems, recv_sems):
            cp.start()
        token[...] = jnp.zeros_like(token)

    return pl.pallas_call(
        body, name=name,
        out_shape=(pltpu.SemaphoreType.DMA((3,)), pltpu.SemaphoreType.DMA((3,)), pltpu.HBM(wp.shape, wp.dtype),
                   pltpu.HBM((N_CHIPS, R, W), wp.dtype), jax.ShapeDtypeStruct((8, 128), F32)),
        in_specs=(_HBM, _HBM),
        out_specs=(_SEM, _SEM, _HBM, _HBM, pl.BlockSpec(memory_space=pltpu.VMEM)),
        input_output_aliases={0: 2, 1: 3}, compiler_params=_ASYNC,
    )(_hbm(wp), _hbm(lax.empty((N_CHIPS, R, W), wp.dtype)))


def _gather_wait(send_sems, recv_sems, w_thru, land_thru, after, name):
    R, W = w_thru.shape
    rh = R // 2

    def body(w_ref, land_ref, send_sems, recv_sems, after_ref, w_dead, got_ref):
        x, y, c, _ = _mesh_place()
        half = land_ref.at[0, pl.ds(0, rh)]
        for k in range(3):
            cp = pltpu.make_async_remote_copy(src_ref=half, dst_ref=half, send_sem=send_sems.at[k],
                                              recv_sem=recv_sems.at[k], device_id=(x, y, 1 - c),
                                              device_id_type=MESH)
            cp.wait_send()
            cp.wait_recv()

    return pl.pallas_call(
        body, name=name,
        out_shape=(pltpu.HBM(w_thru.shape, w_thru.dtype), pltpu.HBM(land_thru.shape, land_thru.dtype)),
        in_specs=(_HBM, _HBM, _SEM, _SEM, pl.BlockSpec(memory_space=pl.ANY)), out_specs=(_HBM, _HBM),
        input_output_aliases={0: 0, 1: 1}, compiler_params=_ASYNC,
    )(w_thru, land_thru, send_sems, recv_sems, after)


def _scatter_start(g, row0, nrows, name):
    n, R, W = g.shape
    land_shape = (n, nrows, W)

    def body(g_ref, land_ref, send_sems, recv_sems, g_thru, land_thru, token):
        for cp in _scatter_copies(g_ref, land_ref, send_sems, recv_sems, row0):
            cp.start()
        token[...] = jnp.zeros_like(token)

    return pl.pallas_call(
        body, name=name,
        out_shape=(pltpu.SemaphoreType.DMA((3,)), pltpu.SemaphoreType.DMA((3,)), pltpu.HBM(g.shape, g.dtype),
                   pltpu.HBM(land_shape, g.dtype), jax.ShapeDtypeStruct((8, 128), F32)),
        in_specs=(_HBM, _HBM),
        out_specs=(_SEM, _SEM, _HBM, _HBM, pl.BlockSpec(memory_space=pltpu.VMEM)),
        input_output_aliases={0: 2, 1: 3}, compiler_params=_ASYNC,
    )(_hbm(g), _hbm(lax.empty(land_shape, g.dtype)))


def _scatter_wait(send_sems, recv_sems, g_thru, land_thru, after, name):
    def body(g_ref, land_ref, send_sems, recv_sems, after_ref, g_out, got_ref):
        x, y, c, _ = _mesh_place()
        for k in range(3):
            cp = pltpu.make_async_remote_copy(src_ref=land_ref.at[0], dst_ref=land_ref.at[0], send_sem=send_sems.at[k],
                                              recv_sem=recv_sems.at[k], device_id=(x, y, 1 - c),
                                              device_id_type=MESH)
            cp.wait_send()
            cp.wait_recv()

    return pl.pallas_call(
        body, name=name,
        out_shape=(pltpu.HBM(g_thru.shape, g_thru.dtype), pltpu.HBM(land_thru.shape, land_thru.dtype)),
        in_specs=(_HBM, _HBM, _SEM, _SEM, pl.BlockSpec(memory_space=pl.ANY)), out_specs=(_HBM, _HBM),
        input_output_aliases={0: 0, 1: 1}, compiler_params=_ASYNC,
    )(g_thru, land_thru, send_sems, recv_sems, after)


def _adamw(w, g, m, v, name):
    shape = w.shape
    cols = shape[-1]
    w2, g2, m2, v2 = (t.reshape(-1, cols) for t in (w, g, m, v))
    rows = w2.shape[0]
    tr = rows
    for cand in (512, 256, 128, 64, 32, 16, 8):
        if rows > cand and rows % cand == 0:
            tr = cand
            break
    c1 = 1.0 / (1.0 - ADAM_B1 ** ADAM_STEP)
    c2 = 1.0 / (1.0 - ADAM_B2 ** ADAM_STEP)

    def body(w_ref, g_ref, m_ref, v_ref, d_ref, nm_ref, nv_ref):
        gv = g_ref[...]
        nm = ADAM_B1 * m_ref[...] + (1.0 - ADAM_B1) * gv
        nv = ADAM_B2 * v_ref[...] + (1.0 - ADAM_B2) * (gv * gv)
        nm_ref[...] = nm
        nv_ref[...] = nv
        d_ref[...] = -ADAM_LR * ((nm * c1) / (jnp.sqrt(nv * c2) + ADAM_EPS) + ADAM_WD * w_ref[...])

    blk = pl.BlockSpec((tr, cols), lambda i: (i, 0))
    sds = jax.ShapeDtypeStruct((rows, cols), F32)
    d, nm, nv = pl.pallas_call(body, name=name, grid=(rows // tr,), in_specs=[blk] * 4,
                               out_specs=(blk, blk, blk), out_shape=(sds, sds, sds),
                               compiler_params=_cparams("parallel"))(w2, g2, m2, v2)
    return d.reshape(shape), nm.reshape(shape), nv.reshape(shape)


def kernel(x, positions, norm_gains, mla_w_in, mla_q_norm, mla_kv_norm, mla_w_uq, mla_w_ukv, mla_w_o, hgrn_w_in, hgrn_lb_logits, hgrn_o_norm, hgrn_w_o, mlp_w1, mlp_w2, loss_target, m_norm_gains, m_mla_w_in, m_mla_q_norm, m_mla_kv_norm, m_mla_w_uq, m_mla_w_ukv, m_mla_w_o, m_hgrn_w_in, m_hgrn_lb_logits, m_hgrn_o_norm, m_hgrn_w_o, m_mlp_w1, m_mlp_w2, v_norm_gains, v_mla_w_in, v_mla_q_norm, v_mla_kv_norm, v_mla_w_uq, v_mla_w_ukv, v_mla_w_o, v_hgrn_w_in, v_hgrn_lb_logits, v_hgrn_o_norm, v_hgrn_w_o, v_mlp_w1, v_mlp_w2):
    w = dict(norm_gains=norm_gains, mla_w_in=mla_w_in, mla_q_norm=mla_q_norm, mla_kv_norm=mla_kv_norm,
             mla_w_uq=mla_w_uq, mla_w_ukv=mla_w_ukv, mla_w_o=mla_w_o, hgrn_w_in=hgrn_w_in,
             hgrn_lb_logits=hgrn_lb_logits, hgrn_o_norm=hgrn_o_norm, hgrn_w_o=hgrn_w_o,
             mlp_w1=mlp_w1, mlp_w2=mlp_w2)
    mom_m = dict(norm_gains=m_norm_gains, mla_w_in=m_mla_w_in, mla_q_norm=m_mla_q_norm,
                 mla_kv_norm=m_mla_kv_norm, mla_w_uq=m_mla_w_uq, mla_w_ukv=m_mla_w_ukv,
                 mla_w_o=m_mla_w_o, hgrn_w_in=m_hgrn_w_in, hgrn_lb_logits=m_hgrn_lb_logits,
                 hgrn_o_norm=m_hgrn_o_norm, hgrn_w_o=m_hgrn_w_o, mlp_w1=m_mlp_w1, mlp_w2=m_mlp_w2)
    mom_v = dict(norm_gains=v_norm_gains, mla_w_in=v_mla_w_in, mla_q_norm=v_mla_q_norm,
                 mla_kv_norm=v_mla_kv_norm, mla_w_uq=v_mla_w_uq, mla_w_ukv=v_mla_w_ukv,
                 mla_w_o=v_mla_w_o, hgrn_w_in=v_hgrn_w_in, hgrn_lb_logits=v_hgrn_lb_logits,
                 hgrn_o_norm=v_hgrn_o_norm, hgrn_w_o=v_hgrn_w_o, mlp_w1=v_mlp_w1, mlp_w2=v_mlp_w2)
    c = lax.axis_index("c")

    axis_of = dict(SHARDED)
    me = 2 * lax.axis_index("x") + lax.axis_index("y")
    gain_bits = lax.bitcast_convert_type(norm_gains, jnp.uint32)
    gain_hi = lax.bitcast_convert_type((gain_bits >> 16).astype(jnp.uint16), BF16)
    gain_lo = lax.bitcast_convert_type((gain_bits & 0xFFFF).astype(jnp.uint16), BF16)

    layers = []
    for l in range(DEPTH):
        s = l // 2
        if l % 2 == 0:
            big = [("mlp_w1", l), ("mlp_w2", l), ("mla_w_o", s)]
            tail = [("mla_w_in", s), ("mla_w_uq", s), ("mla_w_ukv", s)]
        else:
            big = [("hgrn_w_in", s), ("mlp_w1", l), ("mlp_w2", l), ("hgrn_w_o", s)]
            tail = []
        w_tail = [w[n][i] for n, i in tail] + ([gain_hi, gain_lo] if l == 0 else [])
        g_tail = tail + ([("norm_gains", None)] + [(n, None) for n in REPLICATED] if l == 0 else [])
        g_shapes = [w[n].shape if i is None else w[n][i].shape for n, i in g_tail]
        tail_rows = max(_packed_misc_rows([t.shape for t in w_tail]), _packed_misc_rows(g_shapes))
        pk = _Packed([(n, w[n].shape[1]) for n, _ in big], tail_rows)
        wpack = jnp.zeros((pk.rows, PACK_W), BF16)
        for n, i in big:
            assert w[n].shape[2] == PACK_W
            wpack = _cast_into(w[n][i], wpack, pk.off[n], name="pack_%s_%d" % (n, l))
        if w_tail:
            wpack = lax.dynamic_update_slice(
                wpack, jnp.concatenate(_pack_blocks(w_tail, 0, BF16), axis=0), (pk.misc, 0))
        layers.append(dict(pk=pk, big=big, tail=tail, w_tail=w_tail, g_tail=g_tail, g_shapes=g_shapes,
                           wpack=wpack))

    for l, lay in enumerate(layers):
        lay["gather"] = _gather_start(lay["wpack"], name="gather_start_%d" % l)

    def fetch(l, after):
        lay = layers[l]
        pk = lay["pk"]
        send_sems, recv_sems, w_thru, land_thru, _ = lay["gather"]
        if after is None:
            after = sum(layers[k]["gather"][4] for k in range(1, DEPTH))
        w_back, land = _gather_wait(send_sems, recv_sems, w_thru, land_thru, after, name="gather_wait_%d" % l)
        land = _halves_to_sibling(land, name="gather_halves_%d" % l)
        land = lax.dynamic_update_slice(land, w_back[None], (me, 0, 0))
        out = dict(wbuf=land.reshape(N_CHIPS * pk.rows, PACK_W), pk=pk)
        if lay["w_tail"]:
            rows = _packed_misc_rows([t.shape for t in lay["w_tail"]])
            per_chip = [_unpack(land[j, pk.misc:pk.misc + rows], [t.shape for t in lay["w_tail"]])
                        for j in range(N_CHIPS)]
            for i, (n, _) in enumerate(lay["tail"]):
                out[n[4:]] = jnp.concatenate([per_chip[j][i] for j in range(N_CHIPS)], axis=axis_of[n] - 1)
            if l == 0:
                got_hi, got_lo = (lax.bitcast_convert_type(
                    jnp.concatenate([per_chip[j][i] for j in range(N_CHIPS)], axis=2),
                    jnp.uint16).astype(jnp.uint32) for i in (-2, -1))
                out["gains"] = lax.bitcast_convert_type((got_hi << 16) | got_lo, F32)
        return out

    def emit(l, gbuf, grads):
        lay = layers[l]
        pk = lay["pk"]
        if lay["g_tail"]:
            for j in range(N_CHIPS):
                pieces = []
                for n, i in lay["g_tail"]:
                    if n not in axis_of:
                        pieces.append(grads[n])
                    else:
                        pieces.append(jnp.split(grads[n], N_CHIPS, axis=axis_of[n] - (0 if i is None else 1))[j])
                block = jnp.concatenate(_pack_blocks(pieces, 0, BF16), axis=0)
                gbuf = lax.dynamic_update_slice(gbuf, block, (j * pk.rows + pk.misc, 0))
        row0 = lay.get("early_rows", 0)
        lay["scatter"] = _scatter_start(gbuf.reshape(N_CHIPS, pk.rows, PACK_W), row0, pk.rows - row0,
                                        name="scatter_start_%d" % l)
        return lay["scatter"][4][0, 0]

    def emit_mlp(l, gbuf):
        lay = layers[l]
        pk = lay["pk"]
        assert pk.off["mlp_w1"] == 0 and pk.off["mlp_w2"] == w["mlp_w1"].shape[1]
        lay["early_rows"] = w["mlp_w1"].shape[1] + w["mlp_w2"].shape[1]
        lay["scatter_early"] = _scatter_start(gbuf.reshape(N_CHIPS, pk.rows, PACK_W), 0, lay["early_rows"],
                                              name="scatter_start_%d_mlp" % l)
        return lay["scatter_early"][2].reshape(N_CHIPS * pk.rows, PACK_W)

    small = dict(mla_q_norm=mla_q_norm, mla_kv_norm=mla_kv_norm, hgrn_lb_logits=hgrn_lb_logits,
                 hgrn_o_norm=hgrn_o_norm)
    gbufs = [jnp.zeros((N_CHIPS * lay["pk"].rows, PACK_W), BF16) for lay in layers]
    sq, grad_x = _local_step(x[0], positions[0], loss_target[0], small, fetch, gbufs, emit, emit_mlp)
    d_model = x.shape[-1]
    loss = lax.psum(0.5 * jnp.sum(sq) / d_model, ("x", "y", "c"))

    per_name = {}
    for l, lay in enumerate(layers):
        pk = lay["pk"]
        send_sems, recv_sems, g_thru, land_thru, _ = lay["scatter"]
        row0 = lay.get("early_rows", 0)
        g_back, land = _scatter_wait(send_sems, recv_sems, g_thru, land_thru, grad_x, name="scatter_wait_%d" % l)
        own = lax.dynamic_slice_in_dim(g_back, me, 1, axis=0)
        land = lax.dynamic_update_slice(land, own[:, row0:], (me, 0, 0))
        mine = _sum_chips(land, name="grads_sum_chips_%d" % l)
        if row0:
            send_sems, recv_sems, _, land_thru, _ = lay["scatter_early"]
            g_back, land = _scatter_wait(send_sems, recv_sems, g_back, land_thru, grad_x,
                                         name="scatter_wait_%d_mlp" % l)
            land = lax.dynamic_update_slice(land, own[:, :row0], (me, 0, 0))
            mine = jnp.concatenate([_sum_chips(land, name="grads_sum_chips_%d_mlp" % l), mine], axis=0)
        red = _sum_chips(_share_reduced(mine, name="grads_share_%d" % l), name="grads_sum_cores_%d" % l)
        for n, i in lay["big"]:
            per_name.setdefault(n, {})[i] = red[pk.off[n]:pk.off[n] + w[n].shape[1]]
        for (n, i), piece in zip(lay["g_tail"], _unpack(red[pk.misc:pk.misc + pk.misc_rows], lay["g_shapes"])):
            per_name.setdefault(n, {})[i] = piece
    g_out = {n: (parts[None] if None in parts else jnp.stack([parts[i] for i in sorted(parts)]))
             for n, parts in per_name.items()}

    deltas, new_m, new_v = {}, {}, {}
    for name in WEIGHTS:
        deltas[name], new_m[name], new_v[name] = _adamw(w[name], g_out[name], mom_m[name], mom_v[name],
                                                        name="adamw_" + name)
    return (loss, grad_x[None], *[g_out[n] for n in WEIGHTS], *[deltas[n] for n in WEIGHTS],
            *[new_m[n] for n in WEIGHTS], *[new_v[n] for n in WEIGHTS])
```

```python
import functools

import jax
import jax.numpy as jnp
from jax import lax
from jax.experimental import pallas as pl
from jax.experimental.pallas import tpu as pltpu

F32 = jnp.float32
BF16 = jnp.bfloat16
MESH = pl.DeviceIdType.MESH

DEPTH = 4
MLA_HEADS = 8
MLA_NOPE = 128
MLA_ROPE = 64
MLA_V = 128
MLA_QK_PAD = 256
MLA_HEADS_PER_STEP = 2
MLA_SCALE = float(MLA_NOPE + MLA_ROPE) ** -0.5
ROPE_BASE = 10000.0
HGRN_HEADS = 8
HGRN_CHUNK = 32
HGRN_BLOCK = 128
EPS = 1e-6

ADAM_LR = 0.001
ADAM_B1 = 0.9
ADAM_B2 = 0.999
ADAM_EPS = 1e-08
ADAM_WD = 0.01
ADAM_STEP = 10

N_CHIPS = 4
PACK_W = 1024
PACK_ALIGN = 1024
PACK_TILE = 512
V7X_VMEM_LIMIT = 56 * 1024 * 1024

SHARDED = (("norm_gains", 2), ("mla_w_in", 1), ("mla_w_uq", 2), ("mla_w_ukv", 2), ("mla_w_o", 1),
           ("hgrn_w_in", 2), ("hgrn_w_o", 1), ("mlp_w1", 2), ("mlp_w2", 1))
REPLICATED = ("mla_q_norm", "mla_kv_norm", "hgrn_lb_logits", "hgrn_o_norm")
WEIGHTS = ("norm_gains", "mla_w_in", "mla_q_norm", "mla_kv_norm", "mla_w_uq", "mla_w_ukv", "mla_w_o",
           "hgrn_w_in", "hgrn_lb_logits", "hgrn_o_norm", "hgrn_w_o", "mlp_w1", "mlp_w2")


def _cparams(*semantics):
    return pltpu.CompilerParams(dimension_semantics=semantics, vmem_limit_bytes=V7X_VMEM_LIMIT)


def _sigmoid(x):
    return 1.0 / (1.0 + jnp.exp(-x))


def _mm(a, b, *, ta=False, tb=False, out_dtype=F32, tm=2048, tn=1024, tk=1024, epi=None, extra=None,
        name="mm", n=None, b_map=None, into=None, o_map=None):
    if ta:
        K, M = a.shape
    else:
        M, K = a.shape
    if b_map is not None:
        N = n
    elif tb:
        N, Kb = b.shape
    else:
        Kb, N = b.shape
    assert b_map is not None or K == Kb, (a.shape, b.shape, ta, tb)
    tm, tn = min(tm, M), min(tn, N)
    tk = K if (K <= 1024 and b_map is None) else min(tk, K)
    assert M % tm == 0 and N % tn == 0 and K % tk == 0, (M, N, K, tm, tn, tk)
    nk = K // tk
    a_spec = (pl.BlockSpec((tk, tm), lambda i, j, k: (k, i)) if ta
              else pl.BlockSpec((tm, tk), lambda i, j, k: (i, k)))
    if b_map is None:
        b_map = (lambda i, j, k: (j, k)) if tb else (lambda i, j, k: (k, j))
    b_spec = pl.BlockSpec((tn, tk) if tb else (tk, tn), b_map)
    o_spec = pl.BlockSpec((tm, tn), lambda i, j, k: (i, j))
    dims = (((0 if ta else 1,), (1 if tb else 0,)), ((), ()))
    in_specs = [a_spec, b_spec]
    operands = [a, b]
    aliases = {}
    if epi == "mul2r":
        in_specs.append(o_spec)
        operands.append(extra)
    if into is not None:
        assert epi is None
        in_specs.append(pl.BlockSpec(memory_space=pl.ANY))
        operands.append(into)
        aliases = {2: 0}
        out_dtype = into.dtype
        out_shape = jax.ShapeDtypeStruct(into.shape, into.dtype)
        out_specs = pl.BlockSpec((tm, tn), o_map)
    elif epi == "relu2":
        out_shape = (jax.ShapeDtypeStruct((M, N), BF16), jax.ShapeDtypeStruct((M, N), BF16))
        out_specs = (o_spec, o_spec)
    elif epi == "mul2r":
        out_shape = jax.ShapeDtypeStruct((M, N), BF16)
        out_specs = o_spec
    else:
        out_shape = jax.ShapeDtypeStruct((M, N), out_dtype)
        out_specs = o_spec
    n_in = len(operands)

    def body(*refs):
        a_ref, b_ref = refs[0], refs[1]
        outs = refs[n_in:n_in + (2 if epi == "relu2" else 1)]
        k = pl.program_id(2)

        def finish(acc):
            if epi == "relu2":
                r = jnp.maximum(acc, 0.0)
                outs[0][...] = (r * r).astype(BF16)
                outs[1][...] = r.astype(BF16)
            elif epi == "mul2r":
                outs[0][...] = (acc * (2.0 * refs[2][...].astype(F32))).astype(BF16)
            else:
                outs[0][...] = acc.astype(out_dtype)

        part = lax.dot_general(a_ref[...], b_ref[...], dims, preferred_element_type=F32)
        if nk == 1:
            finish(part)
            return
        acc_ref = refs[-1]

        @pl.when(k == 0)
        def _():
            acc_ref[...] = part

        @pl.when((k > 0) & (k < nk - 1))
        def _():
            acc_ref[...] += part

        @pl.when(k == nk - 1)
        def _():
            finish(acc_ref[...] + part)

    return pl.pallas_call(
        body, name=name, grid=(M // tm, N // tn, nk), in_specs=in_specs, out_specs=out_specs,
        out_shape=out_shape, scratch_shapes=[pltpu.VMEM((tm, tn), F32)] if nk > 1 else [],
        input_output_aliases=aliases,
        compiler_params=_cparams("parallel", "parallel", "arbitrary"))(*operands)


def _rms_rstd(x):
    return lax.rsqrt(jnp.mean(x * x, axis=-1, keepdims=True) + EPS)


def _rms_bwd_tile(x, g, dy):
    r = _rms_rstd(x)
    xh = x * r
    u = dy * g
    dx = r * (u - xh * jnp.mean(u * xh, axis=-1, keepdims=True))
    dg = jnp.sum(dy * xh, axis=0, keepdims=True)
    return dx, dg


def _row_tile(T):
    return min(256, T)


def _prenorm_fwd(x, g, name="prenorm_fwd"):
    T, D = x.shape
    tm = _row_tile(T)

    def body(x_ref, g_ref, a_ref):
        xv = x_ref[...]
        a_ref[...] = (xv * _rms_rstd(xv) * g_ref[...]).astype(BF16)

    row = pl.BlockSpec((tm, D), lambda i: (i, 0))
    vec = pl.BlockSpec((1, D), lambda i: (0, 0))
    return pl.pallas_call(body, name=name, grid=(T // tm,), in_specs=[row, vec], out_specs=row,
                          out_shape=jax.ShapeDtypeStruct((T, D), BF16),
                          compiler_params=_cparams("parallel"))(x, g)


def _resnorm_fwd(h, z, g_post, g_pre, name="resnorm_fwd"):
    T, D = h.shape
    tm = _row_tile(T)

    def body(h_ref, z_ref, gp_ref, gn_ref, hn_ref, a_ref):
        zv = z_ref[...]
        hn = h_ref[...] + zv * _rms_rstd(zv) * gp_ref[...]
        hn_ref[...] = hn
        a_ref[...] = (hn * _rms_rstd(hn) * gn_ref[...]).astype(BF16)

    row = pl.BlockSpec((tm, D), lambda i: (i, 0))
    vec = pl.BlockSpec((1, D), lambda i: (0, 0))
    return pl.pallas_call(body, name=name, grid=(T // tm,), in_specs=[row, row, vec, vec],
                          out_specs=(row, row),
                          out_shape=(jax.ShapeDtypeStruct((T, D), F32), jax.ShapeDtypeStruct((T, D), BF16)),
                          compiler_params=_cparams("parallel"))(h, z, g_post, g_pre)


def _resnorm_loss(h, z, g_post, target, name="resnorm_loss"):
    T, D = h.shape
    tm = _row_tile(T)

    def body(h_ref, z_ref, gp_ref, t_ref, dy_ref, sq_ref):
        zv = z_ref[...]
        err = h_ref[...] + zv * _rms_rstd(zv) * gp_ref[...] - t_ref[...]
        dy_ref[...] = err * (1.0 / D)

        @pl.when(pl.program_id(0) == 0)
        def _():
            sq_ref[...] = jnp.zeros_like(sq_ref)

        sq_ref[...] += jnp.sum(err * err, axis=0, keepdims=True)

    row = pl.BlockSpec((tm, D), lambda i: (i, 0))
    vec = pl.BlockSpec((1, D), lambda i: (0, 0))
    return pl.pallas_call(body, name=name, grid=(T // tm,), in_specs=[row, row, vec, row],
                          out_specs=(row, vec),
                          out_shape=(jax.ShapeDtypeStruct((T, D), F32), jax.ShapeDtypeStruct((1, D), F32)),
                          compiler_params=_cparams("arbitrary"))(h, z, g_post, target)


def _resnorm_bwd(z, g_post, dh, h_new=None, da=None, g_pre=None, name="resnorm_bwd"):
    T, D = z.shape
    tm = _row_tile(T)
    has_next = h_new is not None
    row = pl.BlockSpec((tm, D), lambda i: (i, 0))
    vec = pl.BlockSpec((1, D), lambda i: (0, 0))

    if has_next:
        def body(z_ref, gp_ref, dh_ref, hn_ref, da_ref, gn_ref, t_ref, dz_ref, dgp_ref, dgn_ref):
            first = pl.program_id(0) == 0

            @pl.when(first)
            def _():
                dgp_ref[...] = jnp.zeros_like(dgp_ref)
                dgn_ref[...] = jnp.zeros_like(dgn_ref)

            dpre, dgn = _rms_bwd_tile(hn_ref[...], gn_ref[...], da_ref[...])
            t = dh_ref[...] + dpre
            t_ref[...] = t
            dz, dgp = _rms_bwd_tile(z_ref[...], gp_ref[...], t)
            dz_ref[...] = dz.astype(BF16)
            dgp_ref[...] += dgp
            dgn_ref[...] += dgn

        return pl.pallas_call(
            body, name=name, grid=(T // tm,), in_specs=[row, vec, row, row, row, vec],
            out_specs=(row, row, vec, vec),
            out_shape=(jax.ShapeDtypeStruct((T, D), F32), jax.ShapeDtypeStruct((T, D), BF16),
                       jax.ShapeDtypeStruct((1, D), F32), jax.ShapeDtypeStruct((1, D), F32)),
            compiler_params=_cparams("arbitrary"))(z, g_post, dh, h_new, da, g_pre)

    def body_last(z_ref, gp_ref, dh_ref, dz_ref, dgp_ref):
        @pl.when(pl.program_id(0) == 0)
        def _():
            dgp_ref[...] = jnp.zeros_like(dgp_ref)

        dz, dgp = _rms_bwd_tile(z_ref[...], gp_ref[...], dh_ref[...])
        dz_ref[...] = dz.astype(BF16)
        dgp_ref[...] += dgp

    return pl.pallas_call(
        body_last, name=name, grid=(T // tm,), in_specs=[row, vec, row], out_specs=(row, vec),
        out_shape=(jax.ShapeDtypeStruct((T, D), BF16), jax.ShapeDtypeStruct((1, D), F32)),
        compiler_params=_cparams("arbitrary"))(z, g_post, dh)


def _prenorm_bwd(x, g, dh, da, name="prenorm_bwd"):
    T, D = x.shape
    tm = _row_tile(T)

    def body(x_ref, g_ref, dh_ref, da_ref, dx_ref, dg_ref):
        @pl.when(pl.program_id(0) == 0)
        def _():
            dg_ref[...] = jnp.zeros_like(dg_ref)

        dpre, dg = _rms_bwd_tile(x_ref[...], g_ref[...], da_ref[...])
        dx_ref[...] = dh_ref[...] + dpre
        dg_ref[...] += dg

    row = pl.BlockSpec((tm, D), lambda i: (i, 0))
    vec = pl.BlockSpec((1, D), lambda i: (0, 0))
    return pl.pallas_call(
        body, name=name, grid=(T // tm,), in_specs=[row, vec, row, row], out_specs=(row, vec),
        out_shape=(jax.ShapeDtypeStruct((T, D), F32), jax.ShapeDtypeStruct((1, D), F32)),
        compiler_params=_cparams("arbitrary"))(x, g, dh, da)


class _Packed:
    def __init__(self, big, misc_rows):
        self.big = tuple(big)
        self.off = {}
        r = 0
        for name, rows in big:
            self.off[name] = r
            r += rows
        self.misc, self.misc_rows = r, misc_rows
        self.rows = -(-(r + misc_rows) // PACK_ALIGN) * PACK_ALIGN

    def block(self, name, layer, unit):
        r = self.off[name]
        assert r % unit == 0 and self.rows % unit == 0
        return r // unit, self.rows // unit


def _col_sharded(pk, name, layer, unit):
    base, stride = pk.block(name, layer, unit)
    return (lambda i, j, k: (j * stride + base, 0)), (lambda i, j, k: (k * stride + base, 0))


def _row_sharded(pk, name, layer, unit):
    base, stride = pk.block(name, layer, unit)
    return ((lambda i, j, k: (k * stride + base, 0)), (lambda i, j, k: (j * stride + base, 0)),
            (lambda i, j, k: (i * stride + base, 0)))


def _mlp_fwd(a, wbuf, pk, layer):
    D = a.shape[1]
    by_n, _ = _col_sharded(pk, "mlp_w1", layer, D)
    by_k, _, _ = _row_sharded(pk, "mlp_w2", layer, D)
    act, r = _mm(a, wbuf, n=4 * D, b_map=by_n, tk=D, tn=D, epi="relu2", name="mlp_up")
    u = _mm(act, wbuf, n=D, b_map=by_k, tk=D, tn=D, name="mlp_down")
    return u, (a, act, r)


def _mlp_bwd(du, saved, wbuf, gbuf, pk, layer):
    a, act, r = saved
    D = a.shape[1]
    w1_by_n, w1_by_k = _col_sharded(pk, "mlp_w1", layer, D)
    _, w2_by_n, w2_by_m = _row_sharded(pk, "mlp_w2", layer, D)
    dz1 = _mm(du, wbuf, tb=True, n=4 * D, b_map=w2_by_n, tn=D, tk=D, epi="mul2r", extra=r, name="mlp_down_dx")
    gbuf = _mm(act, du, ta=True, into=gbuf, o_map=w2_by_m, tm=D, tn=D, name="mlp_down_dw")
    gbuf = _mm(a, dz1, ta=True, into=gbuf, o_map=w1_by_n, tm=D, tn=D, name="mlp_up_dw")
    da = _mm(dz1, wbuf, tb=True, n=D, b_map=w1_by_k, tn=D, tk=D, name="mlp_up_dx")
    return da, gbuf


def _rope_swap(t):
    n = t.shape[-1]
    lane = lax.broadcasted_iota(jnp.int32, t.shape, t.ndim - 1)
    half = MLA_ROPE // 2
    first = (lane & (MLA_ROPE - 1)) < half
    return jnp.where(first, pltpu.roll(t, n - half, t.ndim - 1), pltpu.roll(t, half, t.ndim - 1))


def _mla_mid_fwd(proj, q_norm, kv_norm, w_uq, w_ukv, cc, ss):
    T, PW = proj.shape
    QL, KVL = q_norm.shape[-1], kv_norm.shape[-1]
    H = MLA_HEADS
    assert PW == QL + KVL + 128
    tm = _row_tile(T)

    def body(p_ref, qn_ref, kn_ref, wq_ref, wkv_ref, cc_ref, ss_ref,
             cq_ref, ckv_ref, q_ref, k_ref, v_ref):
        cq = p_ref[:, 0:QL]
        ckv = p_ref[:, QL:QL + KVL]
        kr = p_ref[:, QL + KVL:QL + KVL + 128]
        c, s = cc_ref[...], ss_ref[...]
        cqn = (cq * _rms_rstd(cq) * qn_ref[...]).astype(BF16)
        ckvn = (ckv * _rms_rstd(ckv) * kn_ref[...]).astype(BF16)
        cq_ref[...] = cqn
        ckv_ref[...] = ckvn
        q = jnp.dot(cqn, wq_ref[...], preferred_element_type=F32)
        kv = jnp.dot(ckvn, wkv_ref[...], preferred_element_type=F32)
        krf = (kr * c + _rope_swap(kr) * s).astype(BF16)
        for h in range(H):
            o = h * MLA_QK_PAD
            q_ref[:, o:o + MLA_NOPE] = (q[:, o:o + MLA_NOPE] * MLA_SCALE).astype(BF16)
            qr = q[:, o + MLA_NOPE:o + MLA_QK_PAD]
            q_ref[:, o + MLA_NOPE:o + MLA_QK_PAD] = ((qr * c + _rope_swap(qr) * s) * MLA_SCALE).astype(BF16)
            k_ref[:, o:o + MLA_NOPE] = kv[:, o:o + MLA_NOPE].astype(BF16)
            k_ref[:, o + MLA_NOPE:o + MLA_QK_PAD] = krf
            v_ref[:, h * MLA_V:(h + 1) * MLA_V] = kv[:, o + MLA_NOPE:o + MLA_QK_PAD].astype(BF16)

    def row(w):
        return pl.BlockSpec((tm, w), lambda i: (i, 0))

    def full(shape):
        return pl.BlockSpec(shape, lambda i: (0, 0))

    return pl.pallas_call(
        body, name="mla_mid_fwd", grid=(T // tm,),
        in_specs=[row(PW), full((1, QL)), full((1, KVL)), full(w_uq.shape), full(w_ukv.shape),
                  row(128), row(128)],
        out_specs=(row(QL), row(KVL), row(H * MLA_QK_PAD), row(H * MLA_QK_PAD), row(H * MLA_V)),
        out_shape=(jax.ShapeDtypeStruct((T, QL), BF16), jax.ShapeDtypeStruct((T, KVL), BF16),
                   jax.ShapeDtypeStruct((T, H * MLA_QK_PAD), BF16),
                   jax.ShapeDtypeStruct((T, H * MLA_QK_PAD), BF16),
                   jax.ShapeDtypeStruct((T, H * MLA_V), BF16)),
        compiler_params=_cparams("parallel"))(proj, q_norm, kv_norm, w_uq, w_ukv, cc, ss)


def _mla_mid_bwd(proj, q_norm, kv_norm, w_uq, w_ukv, cc, ss, dq, dk, dv):
    T, PW = proj.shape
    QL, KVL = q_norm.shape[-1], kv_norm.shape[-1]
    H = MLA_HEADS
    tm = _row_tile(T)
    nt = (((1,), (1,)), ((), ()))

    def body(p_ref, qn_ref, kn_ref, wq_ref, wkv_ref, cc_ref, ss_ref, dq_ref, dk_ref, dv_ref,
             dqp_ref, dkv_ref, dp_ref, dqn_ref, dkn_ref):
        @pl.when(pl.program_id(0) == 0)
        def _():
            dqn_ref[...] = jnp.zeros_like(dqn_ref)
            dkn_ref[...] = jnp.zeros_like(dkn_ref)

        c, s = cc_ref[...], ss_ref[...]
        dkr = jnp.zeros((tm, 128), F32)
        for h in range(H):
            o = h * MLA_QK_PAD
            dqp_ref[:, o:o + MLA_NOPE] = (dq_ref[:, o:o + MLA_NOPE] * MLA_SCALE).astype(BF16)
            dqr = dq_ref[:, o + MLA_NOPE:o + MLA_QK_PAD] * MLA_SCALE
            dqp_ref[:, o + MLA_NOPE:o + MLA_QK_PAD] = (dqr * c + _rope_swap(dqr * s)).astype(BF16)
            dkv_ref[:, o:o + MLA_NOPE] = dk_ref[:, o:o + MLA_NOPE].astype(BF16)
            dkv_ref[:, o + MLA_NOPE:o + MLA_QK_PAD] = dv_ref[:, h * MLA_V:(h + 1) * MLA_V].astype(BF16)
            dkr = dkr + dk_ref[:, o + MLA_NOPE:o + MLA_QK_PAD]
        dcqn = lax.dot_general(dqp_ref[...], wq_ref[...], nt, preferred_element_type=F32)
        dckvn = lax.dot_general(dkv_ref[...], wkv_ref[...], nt, preferred_element_type=F32)
        dcq, dqn = _rms_bwd_tile(p_ref[:, 0:QL], qn_ref[...], dcqn)
        dckv, dkn = _rms_bwd_tile(p_ref[:, QL:QL + KVL], kn_ref[...], dckvn)
        dp_ref[:, 0:QL] = dcq.astype(BF16)
        dp_ref[:, QL:QL + KVL] = dckv.astype(BF16)
        dp_ref[:, QL + KVL:QL + KVL + 128] = (dkr * c + _rope_swap(dkr * s)).astype(BF16)
        dqn_ref[...] += dqn
        dkn_ref[...] += dkn

    def row(w):
        return pl.BlockSpec((tm, w), lambda i: (i, 0))

    def full(shape):
        return pl.BlockSpec(shape, lambda i: (0, 0))

    return pl.pallas_call(
        body, name="mla_mid_bwd", grid=(T // tm,),
        in_specs=[row(PW), full((1, QL)), full((1, KVL)), full(w_uq.shape), full(w_ukv.shape),
                  row(128), row(128), row(H * MLA_QK_PAD), row(H * MLA_QK_PAD), row(H * MLA_V)],
        out_specs=(row(H * MLA_QK_PAD), row(H * MLA_QK_PAD), row(PW), full((1, QL)), full((1, KVL))),
        out_shape=(jax.ShapeDtypeStruct((T, H * MLA_QK_PAD), BF16),
                   jax.ShapeDtypeStruct((T, H * MLA_QK_PAD), BF16),
                   jax.ShapeDtypeStruct((T, PW), BF16),
                   jax.ShapeDtypeStruct((1, QL), F32), jax.ShapeDtypeStruct((1, KVL), F32)),
        compiler_params=_cparams("arbitrary"))(proj, q_norm, kv_norm, w_uq, w_ukv, cc, ss, dq, dk, dv)


def _attn_tile(T):
    return min(1024, T)


def _attn_pairs(n, by_key):
    if by_key:
        pairs = [(qi, ki) for ki in range(n) for qi in range(ki, n)]
    else:
        pairs = [(qi, ki) for qi in range(n) for ki in range(qi + 1)]
    return (jnp.asarray([p[0] for p in pairs], jnp.int32), jnp.asarray([p[1] for p in pairs], jnp.int32))


def _scores(q, k, diagonal):
    s = lax.dot_general(q, k, (((1,), (1,)), ((), ())), preferred_element_type=F32)
    if diagonal:
        rows = lax.broadcasted_iota(jnp.int32, s.shape, 0)
        cols = lax.broadcasted_iota(jnp.int32, s.shape, 1)
        s = jnp.where(rows >= cols, s, -jnp.inf)
    return s


def _attn_fwd(q, k, v):
    T = q.shape[0]
    H, DQ, DV = MLA_HEADS, MLA_QK_PAD, MLA_V
    tq = _attn_tile(T)
    nq = T // tq
    scale = float(MLA_NOPE + MLA_ROPE) ** -0.5
    G = MLA_HEADS_PER_STEP
    qi_tab, ki_tab = _attn_pairs(nq, by_key=False)

    def body(qi_ref, ki_ref, q_ref, k_ref, v_ref, o_ref, lse_ref, *scratch):
        m_refs, l_refs, acc_refs = scratch[0:G], scratch[G:2 * G], scratch[2 * G:3 * G]
        p = pl.program_id(1)
        qi, ki = qi_ref[p], ki_ref[p]

        @pl.when(ki == 0)
        def _():
            for g in range(G):
                m_refs[g][...] = jnp.full_like(m_refs[g], -jnp.inf)
                l_refs[g][...] = jnp.zeros_like(l_refs[g])
                acc_refs[g][...] = jnp.zeros_like(acc_refs[g])

        def update(diagonal):
            for g in range(G):
                qs, vs = slice(g * DQ, (g + 1) * DQ), slice(g * DV, (g + 1) * DV)
                s = _scores(q_ref[:, qs], k_ref[:, qs], diagonal)
                m_prev = m_refs[g][...]
                m_new = jnp.maximum(m_prev, jnp.max(s, axis=1, keepdims=True))
                alpha = jnp.exp(m_prev - m_new)
                pr = jnp.exp(s - m_new)
                l_refs[g][...] = alpha * l_refs[g][...] + jnp.sum(pr, axis=1, keepdims=True)
                acc_refs[g][...] = alpha * acc_refs[g][...] + jnp.dot(pr.astype(BF16), v_ref[:, vs],
                                                                      preferred_element_type=F32)
                m_refs[g][...] = m_new

        @pl.when(ki < qi)
        def _():
            update(False)

        @pl.when(ki == qi)
        def _():
            update(True)
            for g in range(G):
                vs = slice(g * DV, (g + 1) * DV)
                o_ref[:, vs] = (acc_refs[g][...] / l_refs[g][...]).astype(BF16)
                lse_ref[g] = m_refs[g][...] + jnp.log(l_refs[g][...])

    return pl.pallas_call(
        body, name="attn_fwd",
        grid_spec=pltpu.PrefetchScalarGridSpec(
            num_scalar_prefetch=2, grid=(H // G, int(qi_tab.shape[0])),
            in_specs=[pl.BlockSpec((tq, G * DQ), lambda h, p, qt, kt: (qt[p], h)),
                      pl.BlockSpec((tq, G * DQ), lambda h, p, qt, kt: (kt[p], h)),
                      pl.BlockSpec((tq, G * DV), lambda h, p, qt, kt: (kt[p], h))],
            out_specs=(pl.BlockSpec((tq, G * DV), lambda h, p, qt, kt: (qt[p], h)),
                       pl.BlockSpec((G, tq, 1), lambda h, p, qt, kt: (h, qt[p], 0))),
            scratch_shapes=([pltpu.VMEM((tq, 1), F32)] * (2 * G) + [pltpu.VMEM((tq, DV), F32)] * G)),
        out_shape=(jax.ShapeDtypeStruct((T, H * DV), BF16), jax.ShapeDtypeStruct((H, T, 1), F32)),
        compiler_params=_cparams("parallel", "arbitrary"))(qi_tab, ki_tab, q, k, v)


def _attn_bwd(q, k, v, o, do, lse):
    T = q.shape[0]
    H, DQ, DV = MLA_HEADS, MLA_QK_PAD, MLA_V
    tq = _attn_tile(T)
    nq = T // tq
    scale = float(MLA_NOPE + MLA_ROPE) ** -0.5
    tn = (((0,), (0,)), ((), ()))
    nt = (((1,), (1,)), ((), ()))
    G = MLA_HEADS_PER_STEP
    qi_tab, ki_tab = _attn_pairs(nq, by_key=True)

    def body(qi_ref, ki_ref, q_ref, k_ref, v_ref, o_ref, do_ref, lse_ref, dq_ref, dk_ref, dv_ref,
             dk_acc, dv_acc):
        p = pl.program_id(1)
        qi, ki = qi_ref[p], ki_ref[p]

        @pl.when(p == 0)
        def _():
            dq_ref[...] = jnp.zeros_like(dq_ref)

        @pl.when(qi == ki)
        def _():
            dk_acc[...] = jnp.zeros_like(dk_acc)
            dv_acc[...] = jnp.zeros_like(dv_acc)

        def step(diagonal):
            rows = pl.ds(pl.multiple_of(qi * tq, tq), tq)
            for g in range(G):
                qs, vs = slice(g * DQ, (g + 1) * DQ), slice(g * DV, (g + 1) * DV)
                dof = do_ref[:, vs]
                delta = jnp.sum(dof.astype(F32) * o_ref[:, vs].astype(F32), axis=1, keepdims=True)
                s = _scores(q_ref[:, qs], k_ref[:, qs], diagonal)
                pr = jnp.exp(s - lse_ref[g])
                dp = lax.dot_general(dof, v_ref[:, vs], nt, preferred_element_type=F32)
                ds = (pr * (dp - delta)).astype(BF16)
                dv_acc[:, vs] += lax.dot_general(pr.astype(BF16), dof, tn, preferred_element_type=F32)
                dk_acc[:, qs] += lax.dot_general(ds, q_ref[:, qs], tn, preferred_element_type=F32)
                dq_ref[rows, qs] += jnp.dot(ds, k_ref[:, qs], preferred_element_type=F32)

        @pl.when(qi == ki)
        def _():
            step(True)

        @pl.when(qi > ki)
        def _():
            step(False)

        @pl.when(qi == nq - 1)
        def _():
            dk_ref[...] = dk_acc[...]
            dv_ref[...] = dv_acc[...]

    qspec = pl.BlockSpec((tq, G * DQ), lambda h, p, qt, kt: (qt[p], h))
    ospec = pl.BlockSpec((tq, G * DV), lambda h, p, qt, kt: (qt[p], h))
    kspec = pl.BlockSpec((tq, G * DQ), lambda h, p, qt, kt: (kt[p], h))
    vspec = pl.BlockSpec((tq, G * DV), lambda h, p, qt, kt: (kt[p], h))
    return pl.pallas_call(
        body, name="attn_bwd",
        grid_spec=pltpu.PrefetchScalarGridSpec(
            num_scalar_prefetch=2, grid=(H // G, int(qi_tab.shape[0])),
            in_specs=[qspec, kspec, vspec, ospec, ospec,
                      pl.BlockSpec((G, tq, 1), lambda h, p, qt, kt: (h, qt[p], 0))],
            out_specs=(pl.BlockSpec((T, G * DQ), lambda h, p, qt, kt: (0, h)), kspec, vspec),
            scratch_shapes=[pltpu.VMEM((tq, G * DQ), F32), pltpu.VMEM((tq, G * DV), F32)]),
        out_shape=(jax.ShapeDtypeStruct((T, H * DQ), F32), jax.ShapeDtypeStruct((T, H * DQ), F32),
                   jax.ShapeDtypeStruct((T, H * DV), F32)),
        compiler_params=_cparams("parallel", "arbitrary"))(qi_tab, ki_tab, q, k, v, o, do, lse)


def _mla_fwd(a, w, cc, ss, wbuf, pk, slot):
    D = a.shape[1]
    by_k, _, _ = _row_sharded(pk, "mla_w_o", slot, D // N_CHIPS)
    proj = _mm(a, w["w_in"], name="mla_in")
    cqn, ckvn, q, k, v = _mla_mid_fwd(proj, w["q_norm"], w["kv_norm"], w["w_uq"], w["w_ukv"], cc, ss)
    o, lse = _attn_fwd(q, k, v)
    m = _mm(o, wbuf, n=D, b_map=by_k, tm=2048, tk=D // N_CHIPS, tn=D, name="mla_out")
    return m, (a, proj, cqn, ckvn, q, k, v, o, lse)


def _mla_bwd(dm, saved, w, cc, ss, wbuf, gbuf, pk, slot):
    a, proj, cqn, ckvn, q, k, v, o, lse = saved
    D = a.shape[1]
    _, by_n, by_m = _row_sharded(pk, "mla_w_o", slot, D // N_CHIPS)
    do = _mm(dm, wbuf, tb=True, n=o.shape[1], b_map=by_n, tm=2048, tn=D // N_CHIPS, tk=D, out_dtype=BF16,
             name="mla_out_dx")
    gbuf = _mm(o, dm, ta=True, into=gbuf, o_map=by_m, tm=D // N_CHIPS, tn=D, tk=2048, name="mla_out_dw")
    dq, dk, dv = _attn_bwd(q, k, v, o, do, lse)
    dqp, dkv, dproj, dqn, dkn = _mla_mid_bwd(proj, w["q_norm"], w["kv_norm"], w["w_uq"], w["w_ukv"],
                                             cc, ss, dq, dk, dv)
    dw_uq = _mm(cqn, dqp, ta=True, out_dtype=BF16, name="mla_uq_dw")
    dw_ukv = _mm(ckvn, dkv, ta=True, out_dtype=BF16, name="mla_ukv_dw")
    dw_in = _mm(a, dproj, ta=True, out_dtype=BF16, name="mla_in_dw")
    da = _mm(dproj, w["w_in"], tb=True, name="mla_in_dx")
    return da, gbuf, dict(w_in=dw_in, w_uq=dw_uq, w_ukv=dw_ukv, q_norm=dqn, kv_norm=dkn)


def _split_dot(mat, x, parts):
    acc = None
    rem = x
    for _ in range(parts):
        piece = rem.astype(BF16)
        term = jnp.dot(mat, piece, preferred_element_type=F32)
        acc = term if acc is None else acc + term
        rem = rem - piece.astype(F32)
    return acc


def _chunk_mats(tb):
    C = HGRN_CHUNK
    assert C & (C - 1) == 0
    r = lax.broadcasted_iota(jnp.int32, (tb, tb), 0)
    s = lax.broadcasted_iota(jnp.int32, (tb, tb), 1)
    start = r & ~(C - 1)
    same = start == (s & ~(C - 1))
    ref = start + C // 2
    last = start + C - 1
    one, zero = jnp.float32(1.0), jnp.float32(0.0)
    cum = jnp.where(same & (s <= r), one, zero)
    rel = cum - jnp.where(same & (s <= ref), one, zero)
    rest = jnp.where(same & (s > r) & (s <= last), one, zero)
    rev = jnp.where(same & (s >= r), one, zero)
    ones = jnp.where(same, one, zero)
    causal = same & (s <= r)
    return cum, rel, rest, rev, ones, causal


def _hgrn_gates(p_ref, lb, HK):
    qx = p_ref[:, 0:HK]
    fx = p_ref[:, HK:2 * HK]
    sf = _sigmoid(fx)
    f = lb + (1.0 - lb) * sf
    sq = _sigmoid(qx)
    return qx, sq, qx * sq, sf, f, 1.0 - f, jnp.log(f)


def _hgrn_fwd(proj, lb, o_norm):
    T = proj.shape[0]
    H, C = HGRN_HEADS, HGRN_CHUNK
    HK = proj.shape[1] // 4
    DK = HK // H
    tb = min(HGRN_BLOCK, T)
    ncb = tb // C
    nt = (((1,), (1,)), ((), ()))
    tn = (((0,), (0,)), ((), ()))

    def body(p_ref, lb_ref, on_ref, y_ref, o_ref, st_ref, state, oacc):
        @pl.when(pl.program_id(0) == 0)
        def _():
            state[...] = jnp.zeros_like(state)

        cum, rel, rest, _, _, causal = _chunk_mats(tb)
        _, _, q, _, f, k, logf = _hgrn_gates(p_ref, lb_ref[...], HK)
        b = _split_dot(cum.astype(BF16), logf, 3)
        brel = _split_dot(rel.astype(BF16), logf, 3)
        brest = _split_dot(rest.astype(BF16), logf, 3)
        eb = jnp.exp(b)
        q_rel = (q * jnp.exp(brel)).astype(BF16)
        k_rel = (k * jnp.exp(-brel)).astype(BF16)
        q_dec = (q * eb).astype(BF16)
        k_dec = (k * jnp.exp(brest)).astype(BF16)
        v = p_ref[:, 2 * HK:3 * HK].astype(BF16)
        for h in range(H):
            hs = slice(h * DK, (h + 1) * DK)
            a = lax.dot_general(q_rel[:, hs], k_rel[:, hs], nt, preferred_element_type=F32)
            a = jnp.where(causal, a, 0.0).astype(BF16)
            oacc[:, hs] = jnp.dot(a, v[:, hs], preferred_element_type=F32)
            for j in range(ncb):
                rs = slice(j * C, (j + 1) * C)
                st = state[h]
                st_ref[j, h] = st
                oacc[rs, hs] += lax.dot_general(q_dec[rs, hs], st.astype(BF16), nt,
                                                preferred_element_type=F32)
                dec = jnp.exp(jnp.sum(logf[rs, hs], axis=0, keepdims=True))
                state[h] = dec * st + lax.dot_general(v[rs, hs], k_dec[rs, hs], tn,
                                                      preferred_element_type=F32)
        o = oacc[...]
        o_ref[...] = o
        gx = p_ref[:, 3 * HK:4 * HK]
        gate = gx * _sigmoid(gx)
        for h in range(H):
            hs = slice(h * DK, (h + 1) * DK)
            oh = o[:, hs]
            y_ref[:, hs] = (oh * _rms_rstd(oh) * on_ref[...] * gate[:, hs]).astype(BF16)

    return pl.pallas_call(
        body, name="hgrn_fwd", grid=(T // tb,),
        in_specs=[pl.BlockSpec((tb, 4 * HK), lambda i: (i, 0)),
                  pl.BlockSpec((1, HK), lambda i: (0, 0)),
                  pl.BlockSpec((1, DK), lambda i: (0, 0))],
        out_specs=(pl.BlockSpec((tb, HK), lambda i: (i, 0)),
                   pl.BlockSpec((tb, HK), lambda i: (i, 0)),
                   pl.BlockSpec((ncb, H, DK, DK), lambda i: (i, 0, 0, 0))),
        out_shape=(jax.ShapeDtypeStruct((T, HK), BF16), jax.ShapeDtypeStruct((T, HK), F32),
                   jax.ShapeDtypeStruct((T // C, H, DK, DK), F32)),
        scratch_shapes=[pltpu.VMEM((H, DK, DK), F32), pltpu.VMEM((tb, HK), F32)],
        compiler_params=_cparams("arbitrary"))(proj, lb, o_norm)


def _hgrn_bwd(proj, lb, o_norm, o, states, dy):
    T = proj.shape[0]
    H, C = HGRN_HEADS, HGRN_CHUNK
    HK = proj.shape[1] // 4
    DK = HK // H
    tb = min(HGRN_BLOCK, T)
    ncb = tb // C
    nb = T // tb
    nt = (((1,), (1,)), ((), ()))
    tn = (((0,), (0,)), ((), ()))

    def body(p_ref, lb_ref, on_ref, o_ref, st_ref, dy_ref, dp_ref, dlb_ref, don_ref,
             dstate, dqr_s, dkr_s, dqd_s, dkd_s, dv_s, do_s, e_s):
        @pl.when(pl.program_id(0) == 0)
        def _():
            dstate[...] = jnp.zeros_like(dstate)
            dlb_ref[...] = jnp.zeros_like(dlb_ref)
            don_ref[...] = jnp.zeros_like(don_ref)

        cum, rel, rest, rev, ones, causal = _chunk_mats(tb)
        lb = lb_ref[...]
        qx, sq, q, sf, f, k, logf = _hgrn_gates(p_ref, lb, HK)
        b = _split_dot(cum.astype(BF16), logf, 3)
        brel = _split_dot(rel.astype(BF16), logf, 3)
        brest = _split_dot(rest.astype(BF16), logf, 3)
        eb = jnp.exp(b)
        erel = jnp.exp(brel)
        enrel = jnp.exp(-brel)
        erest = jnp.exp(brest)
        q_rel_f, k_rel_f, q_dec_f, k_dec_f = q * erel, k * enrel, q * eb, k * erest
        q_rel, k_rel = q_rel_f.astype(BF16), k_rel_f.astype(BF16)
        q_dec, k_dec = q_dec_f.astype(BF16), k_dec_f.astype(BF16)
        v = p_ref[:, 2 * HK:3 * HK].astype(BF16)

        gx = p_ref[:, 3 * HK:4 * HK]
        sg = _sigmoid(gx)
        gate = gx * sg
        dy = dy_ref[...]
        ov = o_ref[...]
        on = on_ref[...]
        don = jnp.zeros((1, DK), F32)
        for h in range(H):
            hs = slice(h * DK, (h + 1) * DK)
            oh = ov[:, hs]
            r = _rms_rstd(oh)
            xh = oh * r
            d_on = dy[:, hs] * gate[:, hs]
            don = don + jnp.sum(d_on * xh, axis=0, keepdims=True)
            u = d_on * on
            do_s[:, hs] = r * (u - xh * jnp.mean(u * xh, axis=-1, keepdims=True))
            dp_ref[:, 3 * HK + h * DK:3 * HK + (h + 1) * DK] = (
                dy[:, hs] * xh * on * (sg[:, hs] * (1.0 + gx[:, hs] * (1.0 - sg[:, hs])))).astype(BF16)
        don_ref[...] += don

        for h in range(H):
            hs = slice(h * DK, (h + 1) * DK)
            doh = do_s[:, hs].astype(BF16)
            a = lax.dot_general(q_rel[:, hs], k_rel[:, hs], nt, preferred_element_type=F32)
            a = jnp.where(causal, a, 0.0).astype(BF16)
            da = lax.dot_general(doh, v[:, hs], nt, preferred_element_type=F32)
            da = jnp.where(causal, da, 0.0).astype(BF16)
            dv_s[:, hs] = lax.dot_general(a, doh, tn, preferred_element_type=F32)
            dqr_s[:, hs] = jnp.dot(da, k_rel[:, hs], preferred_element_type=F32)
            dkr_s[:, hs] = lax.dot_general(da, q_rel[:, hs], tn, preferred_element_type=F32)
            for j in reversed(range(ncb)):
                rs = slice(j * C, (j + 1) * C)
                dst = dstate[h]
                dstb = dst.astype(BF16)
                st = st_ref[j, h]
                dkd_s[rs, hs] = jnp.dot(v[rs, hs], dstb, preferred_element_type=F32)
                dv_s[rs, hs] += lax.dot_general(k_dec[rs, hs], dstb, nt, preferred_element_type=F32)
                dec = jnp.exp(jnp.sum(logf[rs, hs], axis=0, keepdims=True))
                e_s[rs, hs] = jnp.broadcast_to(jnp.sum(dst * st, axis=0, keepdims=True) * dec, (C, DK))
                dqd_s[rs, hs] = jnp.dot(doh[rs], st.astype(BF16), preferred_element_type=F32)
                dstate[h] = dec * dst + lax.dot_general(doh[rs], q_dec[rs, hs], tn,
                                                        preferred_element_type=F32)

        dqr, dkr, dqd, dkd = dqr_s[...], dkr_s[...], dqd_s[...], dkd_s[...]
        kdk = dkd * k_dec_f
        db = dqr * q_rel_f - dkr * k_rel_f + dqd * q_dec_f - kdk
        dlogf = _split_dot(rev.astype(BF16), db, 2) + _split_dot(ones.astype(BF16), kdk, 2) + e_s[...]
        dk = dkr * enrel + dkd * erest
        df = dlogf / f - dk
        dlb_ref[...] += jnp.sum(df * (1.0 - sf), axis=0, keepdims=True)
        dq = dqr * erel + dqd * eb
        dp_ref[:, 0:HK] = (dq * (sq * (1.0 + qx * (1.0 - sq)))).astype(BF16)
        dp_ref[:, HK:2 * HK] = (df * (1.0 - lb) * sf * (1.0 - sf)).astype(BF16)
        dp_ref[:, 2 * HK:3 * HK] = dv_s[...].astype(BF16)

    rev_row = lambda w: pl.BlockSpec((tb, w), lambda i: (nb - 1 - i, 0))
    vec = lambda w: pl.BlockSpec((1, w), lambda i: (0, 0))
    scr = pltpu.VMEM((tb, HK), F32)
    return pl.pallas_call(
        body, name="hgrn_bwd", grid=(nb,),
        in_specs=[rev_row(4 * HK), vec(HK), vec(DK), rev_row(HK),
                  pl.BlockSpec((ncb, H, DK, DK), lambda i: (nb - 1 - i, 0, 0, 0)), rev_row(HK)],
        out_specs=(rev_row(4 * HK), vec(HK), vec(DK)),
        out_shape=(jax.ShapeDtypeStruct((T, 4 * HK), BF16), jax.ShapeDtypeStruct((1, HK), F32),
                   jax.ShapeDtypeStruct((1, DK), F32)),
        scratch_shapes=[pltpu.VMEM((H, DK, DK), F32), scr, scr, scr, scr, scr, scr, scr],
        compiler_params=_cparams("arbitrary"))(proj, lb, o_norm, o, states, dy)


def _hgrn_layer_fwd(a, o_norm, lb, wbuf, pk, slot):
    D = a.shape[1]
    in_by_n, _ = _col_sharded(pk, "hgrn_w_in", slot, D)
    out_by_k, _, _ = _row_sharded(pk, "hgrn_w_o", slot, D // N_CHIPS)
    proj = _mm(a, wbuf, n=4 * D, b_map=in_by_n, tk=D, tn=D, name="hgrn_in")
    y, o, states = _hgrn_fwd(proj, lb, o_norm)
    m = _mm(y, wbuf, n=D, b_map=out_by_k, tm=2048, tk=D // N_CHIPS, tn=D, name="hgrn_out")
    return m, (a, proj, y, o, states)


def _hgrn_layer_bwd(dm, saved, o_norm, lb, wbuf, gbuf, pk, slot):
    a, proj, y, o, states = saved
    D = a.shape[1]
    in_by_n, in_by_k = _col_sharded(pk, "hgrn_w_in", slot, D)
    _, out_by_n, out_by_m = _row_sharded(pk, "hgrn_w_o", slot, D // N_CHIPS)
    dy = _mm(dm, wbuf, tb=True, n=y.shape[1], b_map=out_by_n, tm=2048, tn=D // N_CHIPS, tk=D,
             name="hgrn_out_dx")
    gbuf = _mm(y, dm, ta=True, into=gbuf, o_map=out_by_m, tm=D // N_CHIPS, tn=D, tk=2048, name="hgrn_out_dw")
    dproj, dlb, don = _hgrn_bwd(proj, lb, o_norm, o, states, dy)
    gbuf = _mm(a, dproj, ta=True, into=gbuf, o_map=in_by_n, tm=D, tn=D, name="hgrn_in_dw")
    da = _mm(dproj, wbuf, tb=True, n=D, b_map=in_by_k, tn=D, tk=D, name="hgrn_in_dx")
    return da, gbuf, dict(o_norm=don, lb=dlb)


def _lower_bounds(lb_logits):
    p = jax.nn.softmax(lb_logits.astype(F32), axis=0)
    return jnp.cumsum(p, axis=0) - p[0]


def _rope_tables(positions):
    inv_freq = jnp.power(ROPE_BASE, -jnp.arange(0, MLA_ROPE, 2, dtype=F32) / MLA_ROPE)
    ang = positions.astype(F32)[:, None] * inv_freq
    cos, sin = jnp.cos(ang), jnp.sin(ang)
    zero = jnp.zeros((positions.shape[0], 128 - MLA_ROPE), F32)
    return (jnp.concatenate([cos, cos, zero], axis=-1), jnp.concatenate([-sin, sin, zero], axis=-1))


def _pad_mla_weights(w_in, w_uq):
    w_in_p = jnp.pad(w_in, ((0, 0), (0, 0), (0, 128 - MLA_ROPE)))
    n, ql, _ = w_uq.shape
    w_uq_p = jnp.pad(w_uq.reshape(n, ql, MLA_HEADS, MLA_NOPE + MLA_ROPE),
                     ((0, 0), (0, 0), (0, 0), (0, MLA_QK_PAD - MLA_NOPE - MLA_ROPE)))
    return w_in_p, w_uq_p.reshape(n, ql, MLA_HEADS * MLA_QK_PAD)


def _local_step(x, positions, target, small, fetch, gbufs, emit, emit_mlp):
    T, D = x.shape
    lbounds, lb_vjp = jax.vjp(_lower_bounds, small["hgrn_lb_logits"])
    cc, ss = _rope_tables(positions)
    fetched = {0: fetch(0, None)}
    gains = fetched[0]["gains"]
    tick = [jnp.zeros((), F32)]

    def g(layer, i):
        return gains[layer, i][None, :] + tick[0]

    def mla_weights(layer):
        f = fetched[layer]
        w_in_p, w_uq_p = _pad_mla_weights(f["w_in"][None], f["w_uq"][None])
        slot = layer // 2
        return dict(w_in=w_in_p[0], w_uq=w_uq_p[0], w_ukv=f["w_ukv"],
                    q_norm=small["mla_q_norm"][slot][None, :], kv_norm=small["mla_kv_norm"][slot][None, :])

    saved = []
    h = x
    a = _prenorm_fwd(x, g(0, 0))
    dy = sq = None
    for layer in range(DEPTH):
        slot = layer // 2
        if layer not in fetched:
            fetched[layer] = fetch(layer, a)
        wbuf, pk = fetched[layer]["wbuf"], fetched[layer]["pk"]
        if layer % 2 == 0:
            m, mix_saved = _mla_fwd(a, mla_weights(layer), cc, ss, wbuf, pk, slot)
        else:
            m, mix_saved = _hgrn_layer_fwd(a, small["hgrn_o_norm"][slot][None, :], lbounds[layer][None, :],
                                           wbuf, pk, slot)
        h1, a2 = _resnorm_fwd(h, m, g(layer, 1), g(layer, 2), name="resnorm_fwd_mix")
        u, mlp_saved = _mlp_fwd(a2, wbuf, pk, layer)
        if layer + 1 < DEPTH:
            h2, a = _resnorm_fwd(h1, u, g(layer, 3), g(layer + 1, 0), name="resnorm_fwd_mlp")
        else:
            h2 = None
            dy, sq = _resnorm_loss(h1, u, g(layer, 3), target)
        saved.append((h, m, h1, u, mix_saved, mlp_saved))
        h = h2

    n_mla, n_hgrn = (DEPTH + 1) // 2, DEPTH // 2
    dgains = [[None] * 4 for _ in range(DEPTH)]
    gw = {k: [None] * n_mla for k in ("mla_w_in", "mla_w_uq", "mla_w_ukv", "mla_q_norm", "mla_kv_norm")}
    gw["hgrn_o_norm"] = [None] * n_hgrn
    dlb = [jnp.zeros((1, lbounds.shape[1]), F32) for _ in range(DEPTH)]
    dh = dy
    da_next = None
    for layer in reversed(range(DEPTH)):
        h0, m, h1, u, mix_saved, mlp_saved = saved[layer]
        slot = layer // 2
        wbuf, pk, gbuf = fetched[layer]["wbuf"], fetched[layer]["pk"], gbufs[layer]
        if da_next is None:
            du, dgains[layer][3] = _resnorm_bwd(u, g(layer, 3), dh, name="resnorm_bwd_last")
            t = dh
        else:
            h2 = saved[layer + 1][0]
            t, du, dgains[layer][3], dgains[layer + 1][0] = _resnorm_bwd(
                u, g(layer, 3), dh, h2, da_next, g(layer + 1, 0), name="resnorm_bwd_mlp")
        da2, gbuf = _mlp_bwd(du, mlp_saved, wbuf, gbuf, pk, layer)
        if layer == 0:
            gbuf = emit_mlp(layer, gbuf)
        t, dm, dgains[layer][1], dgains[layer][2] = _resnorm_bwd(
            m, g(layer, 1), t, h1, da2, g(layer, 2), name="resnorm_bwd_mix")
        if layer % 2 == 0:
            da_next, gbuf, mg = _mla_bwd(dm, mix_saved, mla_weights(layer), cc, ss, wbuf, gbuf, pk, slot)
            ql = mg["q_norm"].shape[-1]
            kvl = mg["kv_norm"].shape[-1]
            gw["mla_w_in"][slot] = mg["w_in"][:, :ql + kvl + MLA_ROPE]
            gw["mla_w_uq"][slot] = mg["w_uq"].reshape(ql, MLA_HEADS, MLA_QK_PAD)[
                :, :, :MLA_NOPE + MLA_ROPE].reshape(ql, MLA_HEADS * (MLA_NOPE + MLA_ROPE))
            gw["mla_w_ukv"][slot] = mg["w_ukv"]
            gw["mla_q_norm"][slot] = mg["q_norm"][0]
            gw["mla_kv_norm"][slot] = mg["kv_norm"][0]
        else:
            da_next, gbuf, hg = _hgrn_layer_bwd(dm, mix_saved, small["hgrn_o_norm"][slot][None, :],
                                                lbounds[layer][None, :], wbuf, gbuf, pk, slot)
            gw["hgrn_o_norm"][slot] = hg["o_norm"][0]
            dlb[layer] = hg["lb"]
        dh = t
        if layer > 0:
            mine = ({k: gw[k][slot] for k in ("mla_w_in", "mla_w_uq", "mla_w_ukv")} if layer % 2 == 0 else {})
            tick[0] = emit(layer, gbuf, mine)
        else:
            gbuf0 = gbuf
    grad_x, dgains[0][0] = _prenorm_bwd(x, g(0, 0), dh, da_next)

    last = {k: gw[k][0] for k in ("mla_w_in", "mla_w_uq", "mla_w_ukv")}
    last.update({k: jnp.stack(gw[k]) for k in ("mla_q_norm", "mla_kv_norm", "hgrn_o_norm")})
    last["norm_gains"] = jnp.stack([jnp.concatenate(row, axis=0) for row in dgains])
    (last["hgrn_lb_logits"],) = lb_vjp(jnp.concatenate(dlb, axis=0))
    emit(0, gbuf0, last)
    return sq, grad_x


def _size(shape):
    n = 1
    for d in shape:
        n *= d
    return n


def _piece_rows(shape):
    return -(-_size(shape) // PACK_W)


def _packed_misc_rows(shapes):
    return sum(_piece_rows(s) for s in shapes)


def _cast_into(src, buf, row, name):
    rows, W = src.shape
    tr = min(256, rows)
    assert rows % tr == 0 and row % tr == 0

    def body(s_ref, b_ref, o_ref):
        o_ref[...] = s_ref[...].astype(BF16)

    return pl.pallas_call(
        body, name=name, grid=(rows // tr,),
        in_specs=[pl.BlockSpec((tr, W), lambda i: (i, 0)), pl.BlockSpec(memory_space=pl.ANY)],
        out_specs=pl.BlockSpec((tr, W), lambda i: (row // tr + i, 0)),
        out_shape=jax.ShapeDtypeStruct(buf.shape, buf.dtype), input_output_aliases={1: 0},
        compiler_params=_cparams("parallel"))(src, buf)


def _pack_blocks(pieces, rows, dtype):
    blocks, used = [], 0
    for p in pieces:
        flat = p.astype(dtype).reshape(-1)
        r = _piece_rows(p.shape)
        if r * PACK_W != flat.shape[0]:
            flat = jnp.pad(flat, (0, r * PACK_W - flat.shape[0]))
        blocks.append(flat.reshape(r, PACK_W))
        used += r
    if rows > used:
        blocks.append(jnp.zeros((rows - used, PACK_W), dtype))
    return blocks


def _unpack(buf, shapes):
    out, off = [], 0
    for shp in shapes:
        r = _piece_rows(shp)
        piece = buf[off:off + r]
        if r * PACK_W != _size(shp):
            piece = piece.reshape(-1)[:_size(shp)]
        out.append(piece.reshape(shp))
        off += r
    return out


def _mesh_place():
    x, y, c = lax.axis_index("x"), lax.axis_index("y"), lax.axis_index("c")
    chips = [(1 - x, y), (x, 1 - y), (1 - x, 1 - y)]
    return x, y, c, chips


_HBM = pl.BlockSpec(memory_space=pltpu.HBM)


def _all_gather(wp):
    R, W = wp.shape
    rh = R // 2
    rq = rh // 2
    assert rq % 16 == 0

    def body(w_ref, out_ref, send_sems, recv_sems):
        x, y, c, _ = _mesh_place()
        me, jx, jy, jd = 2 * x + y, 2 * (1 - x) + y, 2 * x + (1 - y), 2 * (1 - x) + (1 - y)
        to_x, to_y, sibling = (1 - x, y, c), (x, 1 - y, c), (x, y, 1 - c)

        def rows(core, quarter):
            return pl.ds(pl.multiple_of(core * rh + quarter * rq, 16), rq)

        def slot(j, core, quarter):
            return out_ref.at[j, rows(core, quarter)]

        def copy(k, src, dst, to):
            return pltpu.make_async_remote_copy(src_ref=src, dst_ref=dst, send_sem=send_sems.at[k],
                                                recv_sem=recv_sems.at[k], device_id=to, device_id_type=MESH)

        sends = [copy(0, w_ref.at[rows(c, 0)], slot(me, c, 0), to_x),
                 copy(2, w_ref.at[rows(c, 1)], slot(me, c, 1), to_y),
                 copy(1, w_ref.at[rows(c, 1)], slot(me, c, 1), to_x),
                 copy(3, w_ref.at[rows(c, 0)], slot(me, c, 0), to_y)]
        for cp in sends:
            cp.start()
        arrivals = [(0, slot(jx, c, 0), 4, to_y, 6), (2, slot(jy, c, 1), 5, to_x, 7),
                    (1, slot(jx, c, 1), None, None, 8), (3, slot(jy, c, 0), None, None, 9),
                    (4, slot(jd, c, 0), None, None, 10), (5, slot(jd, c, 1), None, None, 11)]
        for k, landed, k_on, to_on, k_sib in arrivals:
            copy(k, landed, landed, sibling).wait_recv()
            if k_on is not None:
                cp = copy(k_on, landed, landed, to_on)
                cp.start()
                sends.append(cp)
            cp = copy(k_sib, landed, landed, sibling)
            cp.start()
            sends.append(cp)
        for k_sib, j, quarter in ((6, jx, 0), (7, jy, 1), (8, jx, 1), (9, jy, 0), (10, jd, 0), (11, jd, 1)):
            landed = slot(j, 1 - c, quarter)
            copy(k_sib, landed, landed, sibling).wait_recv()
        for cp in sends:
            cp.wait_send()

    out = pl.pallas_call(
        body, name="weights_all_gather", in_specs=[_HBM], out_specs=_HBM,
        out_shape=jax.ShapeDtypeStruct((N_CHIPS, R, W), wp.dtype),
        scratch_shapes=[pltpu.SemaphoreType.DMA((12,)), pltpu.SemaphoreType.DMA((12,))],
    )(wp)
    me = 2 * lax.axis_index("x") + lax.axis_index("y")
    return lax.dynamic_update_slice(out, wp[None], (me, 0, 0))


def _exchange_halves(g):
    n, _, rh, W = g.shape

    def body(g_ref, out_ref, send_sems, recv_sems):
        x, y, c, _ = _mesh_place()
        sibling = (x, y, 1 - c)
        copies = [pltpu.make_async_remote_copy(
            src_ref=g_ref.at[j, 1 - c], dst_ref=out_ref.at[j], send_sem=send_sems.at[j],
            recv_sem=recv_sems.at[j], device_id=sibling, device_id_type=MESH) for j in range(n)]
        for cp in copies:
            cp.start()
        for cp in copies:
            cp.wait()

    return pl.pallas_call(
        body, name="grads_to_sibling", in_specs=[_HBM], out_specs=_HBM,
        out_shape=jax.ShapeDtypeStruct((n, rh, W), g.dtype),
        scratch_shapes=[pltpu.SemaphoreType.DMA((n,)), pltpu.SemaphoreType.DMA((n,))],
    )(g)


def _scatter_to_owners(p):
    n, rh, W = p.shape
    rq = rh // 2
    assert rq % 16 == 0

    def body(p_ref, out_ref, stage_ref, send_sems, recv_sems):
        x, y, c, _ = _mesh_place()
        me, jx, jy, jd = 2 * x + y, 2 * (1 - x) + y, 2 * x + (1 - y), 2 * (1 - x) + (1 - y)
        to_x, to_y = (1 - x, y, c), (x, 1 - y, c)

        def quarter(ref, j, q):
            return ref.at[j, pl.ds(q * rq, rq)]

        def copy(k, src, dst, to):
            return pltpu.make_async_remote_copy(src_ref=src, dst_ref=dst, send_sem=send_sems.at[k],
                                                recv_sem=recv_sems.at[k], device_id=to, device_id_type=MESH)

        sends = [copy(2, quarter(p_ref, jd, 0), stage_ref.at[0], to_x),
                 copy(3, quarter(p_ref, jd, 1), stage_ref.at[1], to_y),
                 copy(0, p_ref.at[jx], out_ref.at[me], to_x),
                 copy(1, p_ref.at[jy], out_ref.at[me], to_y)]
        for cp in sends:
            cp.start()
        copy(2, stage_ref.at[0], stage_ref.at[0], to_x).wait_recv()
        relay = copy(4, stage_ref.at[0], quarter(out_ref, jx, 0), to_y)
        relay.start()
        sends.append(relay)
        copy(3, stage_ref.at[1], stage_ref.at[1], to_y).wait_recv()
        relay = copy(5, stage_ref.at[1], quarter(out_ref, jy, 1), to_x)
        relay.start()
        sends.append(relay)
        copy(0, out_ref.at[jx], out_ref.at[jx], to_x).wait_recv()
        copy(1, out_ref.at[jy], out_ref.at[jy], to_y).wait_recv()
        copy(4, quarter(out_ref, jd, 0), quarter(out_ref, jd, 0), to_y).wait_recv()
        copy(5, quarter(out_ref, jd, 1), quarter(out_ref, jd, 1), to_x).wait_recv()
        for cp in sends:
            cp.wait_send()

    out, _ = pl.pallas_call(
        body, name="grads_to_owner", in_specs=[_HBM], out_specs=(_HBM, _HBM),
        out_shape=(jax.ShapeDtypeStruct((n, rh, W), p.dtype), jax.ShapeDtypeStruct((2, rq, W), p.dtype)),
        scratch_shapes=[pltpu.SemaphoreType.DMA((6,)), pltpu.SemaphoreType.DMA((6,))],
    )(p)
    me = 2 * lax.axis_index("x") + lax.axis_index("y")
    mine = lax.dynamic_index_in_dim(p, me, axis=0, keepdims=True)
    return lax.dynamic_update_slice(out, mine, (me, 0, 0))


def _share_reduced(q, name="grads_share_reduced"):
    rh, W = q.shape

    def body(q_ref, out_ref, send_sem, recv_sem):
        x, y, c, _ = _mesh_place()
        cp = pltpu.make_async_remote_copy(src_ref=q_ref, dst_ref=out_ref.at[c], send_sem=send_sem,
                                          recv_sem=recv_sem, device_id=(x, y, 1 - c), device_id_type=MESH)
        cp.start()
        cp.wait()

    out = pl.pallas_call(
        body, name=name, in_specs=[_HBM], out_specs=_HBM,
        out_shape=jax.ShapeDtypeStruct((2, rh, W), q.dtype),
        scratch_shapes=[pltpu.SemaphoreType.DMA, pltpu.SemaphoreType.DMA],
    )(q)
    return lax.dynamic_update_slice(out, q[None], (lax.axis_index("c"), 0, 0))


def _add_sibling(g, recv, c_arr):
    n, _, rh, W = g.shape
    tr = PACK_TILE

    def body(c_ref, g_ref, r_ref, o_ref):
        o_ref[...] = (g_ref[...].astype(F32) + r_ref[...].astype(F32)).astype(BF16)

    return pl.pallas_call(
        body, name="grads_add_sibling",
        grid_spec=pltpu.PrefetchScalarGridSpec(
            num_scalar_prefetch=1, grid=(n, rh // tr),
            in_specs=[pl.BlockSpec((None, None, tr, W), lambda j, i, c_ref: (j, c_ref[0], i, 0)),
                      pl.BlockSpec((None, tr, W), lambda j, i, c_ref: (j, i, 0))],
            out_specs=pl.BlockSpec((None, tr, W), lambda j, i, c_ref: (j, i, 0))),
        out_shape=jax.ShapeDtypeStruct((n, rh, W), BF16),
        compiler_params=_cparams("parallel", "parallel"))(c_arr, g, recv)


def _sum_chips(parts, name="grads_sum_chips"):
    n, rh, W = parts.shape
    tr = PACK_TILE

    def body(p_ref, o_ref):
        acc = p_ref[0].astype(F32)
        for j in range(1, n):
            acc = acc + p_ref[j].astype(F32)
        o_ref[...] = acc

    return pl.pallas_call(
        body, name=name, grid=(rh // tr,),
        in_specs=[pl.BlockSpec((n, tr, W), lambda i: (0, i, 0))],
        out_specs=pl.BlockSpec((tr, W), lambda i: (i, 0)),
        out_shape=jax.ShapeDtypeStruct((rh, W), F32),
        compiler_params=_cparams("parallel"))(parts)


_SEM = pl.BlockSpec(memory_space=pltpu.SEMAPHORE)
_ASYNC = pltpu.CompilerParams(has_side_effects=pltpu.SideEffectType.DATAFLOW_SIDE_EFFECTING)


def _hbm(a):
    return pltpu.with_memory_space_constraint(a, pltpu.HBM)


def _gather_copies(w_ref, land_ref, send_sems, recv_sems):
    x, y, c, chips = _mesh_place()
    me = 2 * x + y
    rh = w_ref.shape[0] // 2
    rows = pl.ds(pl.multiple_of(c * rh, 16), rh)
    return [pltpu.make_async_remote_copy(
        src_ref=w_ref.at[rows], dst_ref=land_ref.at[me, rows], send_sem=send_sems.at[r],
        recv_sem=recv_sems.at[r], device_id=(px, py, c), device_id_type=MESH)
        for r, (px, py) in enumerate(chips)]


def _scatter_copies(g_ref, land_ref, send_sems, recv_sems, row0):
    x, y, c, chips = _mesh_place()
    me = 2 * x + y
    rows = pl.ds(row0, land_ref.shape[1])
    return [pltpu.make_async_remote_copy(
        src_ref=g_ref.at[2 * px + py, rows], dst_ref=land_ref.at[me], send_sem=send_sems.at[r],
        recv_sem=recv_sems.at[r], device_id=(px, py, c), device_id_type=MESH)
        for r, (px, py) in enumerate(chips)]


def _halves_to_sibling(land, name):
    n, R, W = land.shape
    rh = R // 2

    def body(l_ref, o_ref, send_sems, recv_sems):
        x, y, c, chips = _mesh_place()
        rows = pl.ds(pl.multiple_of(c * rh, 16), rh)
        copies = [pltpu.make_async_remote_copy(
            src_ref=o_ref.at[2 * px + py, rows], dst_ref=o_ref.at[2 * px + py, rows], send_sem=send_sems.at[r],
            recv_sem=recv_sems.at[r], device_id=(x, y, 1 - c), device_id_type=MESH)
            for r, (px, py) in enumerate(chips)]
        for cp in copies:
            cp.start()
        for cp in copies:
            cp.wait()

    return pl.pallas_call(
        body, name=name, in_specs=[_HBM], out_specs=_HBM, out_shape=jax.ShapeDtypeStruct(land.shape, land.dtype),
        scratch_shapes=[pltpu.SemaphoreType.DMA((3,)), pltpu.SemaphoreType.DMA((3,))],
        input_output_aliases={0: 0})(land)


def _gather_start(wp, name):
    R, W = wp.shape

    def body(w_ref, land_ref, send_sems, recv_sems, w_thru, land_thru, token):
        for cp in _gather_copies(w_ref, land_ref, send_sems, recv_sems):
            cp.start()
        token[...] = jnp.zeros_like(token)

    return pl.pallas_call(
        body, name=name,
        out_shape=(pltpu.SemaphoreType.DMA((3,)), pltpu.SemaphoreType.DMA((3,)), pltpu.HBM(wp.shape, wp.dtype),
                   pltpu.HBM((N_CHIPS, R, W), wp.dtype), jax.ShapeDtypeStruct((8, 128), F32)),
        in_specs=(_HBM, _HBM),
        out_specs=(_SEM, _SEM, _HBM, _HBM, pl.BlockSpec(memory_space=pltpu.VMEM)),
        input_output_aliases={0: 2, 1: 3}, compiler_params=_ASYNC,
    )(_hbm(wp), _hbm(lax.empty((N_CHIPS, R, W), wp.dtype)))


def _gather_wait(send_sems, recv_sems, w_thru, land_thru, after, name):
    R, W = w_thru.shape
    rh = R // 2

    def body(w_ref, land_ref, send_sems, recv_sems, after_ref, w_dead, got_ref):
        x, y, c, _ = _mesh_place()
        half = land_ref.at[0, pl.ds(0, rh)]
        for k in range(3):
            cp = pltpu.make_async_remote_copy(src_ref=half, dst_ref=half, send_sem=send_sems.at[k],
                                              recv_sem=recv_sems.at[k], device_id=(x, y, 1 - c),
                                              device_id_type=MESH)
            cp.wait_send()
            cp.wait_recv()

    return pl.pallas_call(
        body, name=name,
        out_shape=(pltpu.HBM(w_thru.shape, w_thru.dtype), pltpu.HBM(land_thru.shape, land_thru.dtype)),
        in_specs=(_HBM, _HBM, _SEM, _SEM, pl.BlockSpec(memory_space=pl.ANY)), out_specs=(_HBM, _HBM),
        input_output_aliases={0: 0, 1: 1}, compiler_params=_ASYNC,
    )(w_thru, land_thru, send_sems, recv_sems, after)


def _scatter_start(g, row0, nrows, name):
    n, R, W = g.shape
    land_shape = (n, nrows, W)

    def body(g_ref, land_ref, send_sems, recv_sems, g_thru, land_thru, token):
        for cp in _scatter_copies(g_ref, land_ref, send_sems, recv_sems, row0):
            cp.start()
        token[...] = jnp.zeros_like(token)

    return pl.pallas_call(
        body, name=name,
        out_shape=(pltpu.SemaphoreType.DMA((3,)), pltpu.SemaphoreType.DMA((3,)), pltpu.HBM(g.shape, g.dtype),
                   pltpu.HBM(land_shape, g.dtype), jax.ShapeDtypeStruct((8, 128), F32)),
        in_specs=(_HBM, _HBM),
        out_specs=(_SEM, _SEM, _HBM, _HBM, pl.BlockSpec(memory_space=pltpu.VMEM)),
        input_output_aliases={0: 2, 1: 3}, compiler_params=_ASYNC,
    )(_hbm(g), _hbm(lax.empty(land_shape, g.dtype)))


def _scatter_wait(send_sems, recv_sems, g_thru, land_thru, after, name):
    def body(g_ref, land_ref, send_sems, recv_sems, after_ref, g_out, got_ref):
        x, y, c, _ = _mesh_place()
        for k in range(3):
            cp = pltpu.make_async_remote_copy(src_ref=land_ref.at[0], dst_ref=land_ref.at[0], send_sem=send_sems.at[k],
                                              recv_sem=recv_sems.at[k], device_id=(x, y, 1 - c),
                                              device_id_type=MESH)
            cp.wait_send()
            cp.wait_recv()

    return pl.pallas_call(
        body, name=name,
        out_shape=(pltpu.HBM(g_thru.shape, g_thru.dtype), pltpu.HBM(land_thru.shape, land_thru.dtype)),
        in_specs=(_HBM, _HBM, _SEM, _SEM, pl.BlockSpec(memory_space=pl.ANY)), out_specs=(_HBM, _HBM),
        input_output_aliases={0: 0, 1: 1}, compiler_params=_ASYNC,
    )(g_thru, land_thru, send_sems, recv_sems, after)


def _adamw(w, g, m, v, name):
    shape = w.shape
    cols = shape[-1]
    w2, g2, m2, v2 = (t.reshape(-1, cols) for t in (w, g, m, v))
    rows = w2.shape[0]
    tr = rows
    for cand in (512, 256, 128, 64, 32, 16, 8):
        if rows > cand and rows % cand == 0:
            tr = cand
            break
    c1 = 1.0 / (1.0 - ADAM_B1 ** ADAM_STEP)
    c2 = 1.0 / (1.0 - ADAM_B2 ** ADAM_STEP)

    def body(w_ref, g_ref, m_ref, v_ref, d_ref, nm_ref, nv_ref):
        gv = g_ref[...]
        nm = ADAM_B1 * m_ref[...] + (1.0 - ADAM_B1) * gv
        nv = ADAM_B2 * v_ref[...] + (1.0 - ADAM_B2) * (gv * gv)
        nm_ref[...] = nm
        nv_ref[...] = nv
        d_ref[...] = -ADAM_LR * ((nm * c1) / (jnp.sqrt(nv * c2) + ADAM_EPS) + ADAM_WD * w_ref[...])

    blk = pl.BlockSpec((tr, cols), lambda i: (i, 0))
    sds = jax.ShapeDtypeStruct((rows, cols), F32)
    d, nm, nv = pl.pallas_call(body, name=name, grid=(rows // tr,), in_specs=[blk] * 4,
                               out_specs=(blk, blk, blk), out_shape=(sds, sds, sds),
                               compiler_params=_cparams("parallel"))(w2, g2, m2, v2)
    return d.reshape(shape), nm.reshape(shape), nv.reshape(shape)


def kernel(x, positions, norm_gains, mla_w_in, mla_q_norm, mla_kv_norm, mla_w_uq, mla_w_ukv, mla_w_o, hgrn_w_in, hgrn_lb_logits, hgrn_o_norm, hgrn_w_o, mlp_w1, mlp_w2, loss_target, m_norm_gains, m_mla_w_in, m_mla_q_norm, m_mla_kv_norm, m_mla_w_uq, m_mla_w_ukv, m_mla_w_o, m_hgrn_w_in, m_hgrn_lb_logits, m_hgrn_o_norm, m_hgrn_w_o, m_mlp_w1, m_mlp_w2, v_norm_gains, v_mla_w_in, v_mla_q_norm, v_mla_kv_norm, v_mla_w_uq, v_mla_w_ukv, v_mla_w_o, v_hgrn_w_in, v_hgrn_lb_logits, v_hgrn_o_norm, v_hgrn_w_o, v_mlp_w1, v_mlp_w2):
    w = dict(norm_gains=norm_gains, mla_w_in=mla_w_in, mla_q_norm=mla_q_norm, mla_kv_norm=mla_kv_norm,
             mla_w_uq=mla_w_uq, mla_w_ukv=mla_w_ukv, mla_w_o=mla_w_o, hgrn_w_in=hgrn_w_in,
             hgrn_lb_logits=hgrn_lb_logits, hgrn_o_norm=hgrn_o_norm, hgrn_w_o=hgrn_w_o,
             mlp_w1=mlp_w1, mlp_w2=mlp_w2)
    mom_m = dict(norm_gains=m_norm_gains, mla_w_in=m_mla_w_in, mla_q_norm=m_mla_q_norm,
                 mla_kv_norm=m_mla_kv_norm, mla_w_uq=m_mla_w_uq, mla_w_ukv=m_mla_w_ukv,
                 mla_w_o=m_mla_w_o, hgrn_w_in=m_hgrn_w_in, hgrn_lb_logits=m_hgrn_lb_logits,
                 hgrn_o_norm=m_hgrn_o_norm, hgrn_w_o=m_hgrn_w_o, mlp_w1=m_mlp_w1, mlp_w2=m_mlp_w2)
    mom_v = dict(norm_gains=v_norm_gains, mla_w_in=v_mla_w_in, mla_q_norm=v_mla_q_norm,
                 mla_kv_norm=v_mla_kv_norm, mla_w_uq=v_mla_w_uq, mla_w_ukv=v_mla_w_ukv,
                 mla_w_o=v_mla_w_o, hgrn_w_in=v_hgrn_w_in, hgrn_lb_logits=v_hgrn_lb_logits,
                 hgrn_o_norm=v_hgrn_o_norm, hgrn_w_o=v_hgrn_w_o, mlp_w1=v_mlp_w1, mlp_w2=v_mlp_w2)
    c = lax.axis_index("c")

    axis_of = dict(SHARDED)
    me = 2 * lax.axis_index("x") + lax.axis_index("y")
    gain_bits = lax.bitcast_convert_type(norm_gains, jnp.uint32)
    gain_hi = lax.bitcast_convert_type((gain_bits >> 16).astype(jnp.uint16), BF16)
    gain_lo = lax.bitcast_convert_type((gain_bits & 0xFFFF).astype(jnp.uint16), BF16)

    layers = []
    for l in range(DEPTH):
        s = l // 2
        if l % 2 == 0:
            big = [("mlp_w1", l), ("mlp_w2", l), ("mla_w_o", s)]
            tail = [("mla_w_in", s), ("mla_w_uq", s), ("mla_w_ukv", s)]
        else:
            big = [("hgrn_w_in", s), ("mlp_w1", l), ("mlp_w2", l), ("hgrn_w_o", s)]
            tail = []
        w_tail = [w[n][i] for n, i in tail] + ([gain_hi, gain_lo] if l == 0 else [])
        g_tail = tail + ([("norm_gains", None)] + [(n, None) for n in REPLICATED] if l == 0 else [])
        g_shapes = [w[n].shape if i is None else w[n][i].shape for n, i in g_tail]
        tail_rows = max(_packed_misc_rows([t.shape for t in w_tail]), _packed_misc_rows(g_shapes))
        pk = _Packed([(n, w[n].shape[1]) for n, _ in big], tail_rows)
        wpack = jnp.zeros((pk.rows, PACK_W), BF16)
        for n, i in big:
            assert w[n].shape[2] == PACK_W
            wpack = _cast_into(w[n][i], wpack, pk.off[n], name="pack_%s_%d" % (n, l))
        if w_tail:
            wpack = lax.dynamic_update_slice(
                wpack, jnp.concatenate(_pack_blocks(w_tail, 0, BF16), axis=0), (pk.misc, 0))
        layers.append(dict(pk=pk, big=big, tail=tail, w_tail=w_tail, g_tail=g_tail, g_shapes=g_shapes,
                           gather=_gather_start(wpack, name="gather_start_%d" % l)))

    def fetch(l, after):
        lay = layers[l]
        pk = lay["pk"]
        send_sems, recv_sems, w_thru, land_thru, _ = lay["gather"]
        if after is None:
            after = sum(layers[k]["gather"][4] for k in range(1, DEPTH))
        w_back, land = _gather_wait(send_sems, recv_sems, w_thru, land_thru, after, name="gather_wait_%d" % l)
        land = _halves_to_sibling(land, name="gather_halves_%d" % l)
        land = lax.dynamic_update_slice(land, w_back[None], (me, 0, 0))
        out = dict(wbuf=land.reshape(N_CHIPS * pk.rows, PACK_W), pk=pk)
        if lay["w_tail"]:
            rows = _packed_misc_rows([t.shape for t in lay["w_tail"]])
            per_chip = [_unpack(land[j, pk.misc:pk.misc + rows], [t.shape for t in lay["w_tail"]])
                        for j in range(N_CHIPS)]
            for i, (n, _) in enumerate(lay["tail"]):
                out[n[4:]] = jnp.concatenate([per_chip[j][i] for j in range(N_CHIPS)], axis=axis_of[n] - 1)
            if l == 0:
                got_hi, got_lo = (lax.bitcast_convert_type(
                    jnp.concatenate([per_chip[j][i] for j in range(N_CHIPS)], axis=2),
                    jnp.uint16).astype(jnp.uint32) for i in (-2, -1))
                out["gains"] = lax.bitcast_convert_type((got_hi << 16) | got_lo, F32)
        return out

    def emit(l, gbuf, grads):
        lay = layers[l]
        pk = lay["pk"]
        if lay["g_tail"]:
            for j in range(N_CHIPS):
                pieces = []
                for n, i in lay["g_tail"]:
                    if n not in axis_of:
                        pieces.append(grads[n])
                    else:
                        pieces.append(jnp.split(grads[n], N_CHIPS, axis=axis_of[n] - (0 if i is None else 1))[j])
                block = jnp.concatenate(_pack_blocks(pieces, 0, BF16), axis=0)
                gbuf = lax.dynamic_update_slice(gbuf, block, (j * pk.rows + pk.misc, 0))
        row0 = lay.get("early_rows", 0)
        lay["scatter"] = _scatter_start(gbuf.reshape(N_CHIPS, pk.rows, PACK_W), row0, pk.rows - row0,
                                        name="scatter_start_%d" % l)
        return lay["scatter"][4][0, 0]

    def emit_mlp(l, gbuf):
        lay = layers[l]
        pk = lay["pk"]
        assert pk.off["mlp_w1"] == 0 and pk.off["mlp_w2"] == w["mlp_w1"].shape[1]
        lay["early_rows"] = w["mlp_w1"].shape[1] + w["mlp_w2"].shape[1]
        lay["scatter_early"] = _scatter_start(gbuf.reshape(N_CHIPS, pk.rows, PACK_W), 0, lay["early_rows"],
                                              name="scatter_start_%d_mlp" % l)
        return lay["scatter_early"][2].reshape(N_CHIPS * pk.rows, PACK_W)

    small = dict(mla_q_norm=mla_q_norm, mla_kv_norm=mla_kv_norm, hgrn_lb_logits=hgrn_lb_logits,
                 hgrn_o_norm=hgrn_o_norm)
    gbufs = [jnp.zeros((N_CHIPS * lay["pk"].rows, PACK_W), BF16) for lay in layers]
    sq, grad_x = _local_step(x[0], positions[0], loss_target[0], small, fetch, gbufs, emit, emit_mlp)
    d_model = x.shape[-1]
    loss = lax.psum(0.5 * jnp.sum(sq) / d_model, ("x", "y", "c"))

    per_name = {}
    for l, lay in reversed(list(enumerate(layers))):
        pk = lay["pk"]
        send_sems, recv_sems, g_thru, land_thru, _ = lay["scatter"]
        row0 = lay.get("early_rows", 0)
        g_back, land = _scatter_wait(send_sems, recv_sems, g_thru, land_thru, grad_x, name="scatter_wait_%d" % l)
        own = lax.dynamic_slice_in_dim(g_back, me, 1, axis=0)
        land = lax.dynamic_update_slice(land, own[:, row0:], (me, 0, 0))
        mine = _sum_chips(land, name="grads_sum_chips_%d" % l)
        if row0:
            send_sems, recv_sems, _, land_thru, _ = lay["scatter_early"]
            g_back, land = _scatter_wait(send_sems, recv_sems, g_back, land_thru, grad_x,
                                         name="scatter_wait_%d_mlp" % l)
            land = lax.dynamic_update_slice(land, own[:, :row0], (me, 0, 0))
            mine = jnp.concatenate([_sum_chips(land, name="grads_sum_chips_%d_mlp" % l), mine], axis=0)
        red = _sum_chips(_share_reduced(mine, name="grads_share_%d" % l), name="grads_sum_cores_%d" % l)
        for n, i in lay["big"]:
            per_name.setdefault(n, {})[i] = red[pk.off[n]:pk.off[n] + w[n].shape[1]]
        for (n, i), piece in zip(lay["g_tail"], _unpack(red[pk.misc:pk.misc + pk.misc_rows], lay["g_shapes"])):
            per_name.setdefault(n, {})[i] = piece
    g_out = {n: (parts[None] if None in parts else jnp.stack([parts[i] for i in sorted(parts)]))
             for n, parts in per_name.items()}

    deltas, new_m, new_v = {}, {}, {}
    for name in WEIGHTS:
        deltas[name], new_m[name], new_v[name] = _adamw(w[name], g_out[name], mom_m[name], mom_v[name],
                                                        name="adamw_" + name)
    return (loss, grad_x[None], *[g_out[n] for n in WEIGHTS], *[deltas[n] for n in WEIGHTS],
            *[new_m[n] for n in WEIGHTS], *[new_v[n] for n in WEIGHTS])
```

```python
import functools

import jax
import jax.numpy as jnp
from jax import lax
from jax.experimental import pallas as pl
from jax.experimental.pallas import tpu as pltpu

F32 = jnp.float32
BF16 = jnp.bfloat16
MESH = pl.DeviceIdType.MESH

DEPTH = 4
MLA_HEADS = 8
MLA_NOPE = 128
MLA_ROPE = 64
MLA_V = 128
MLA_QK_PAD = 256
MLA_HEADS_PER_STEP = 2
MLA_SCALE = float(MLA_NOPE + MLA_ROPE) ** -0.5
ROPE_BASE = 10000.0
HGRN_HEADS = 8
HGRN_CHUNK = 32
HGRN_BLOCK = 128
EPS = 1e-6

ADAM_LR = 0.001
ADAM_B1 = 0.9
ADAM_B2 = 0.999
ADAM_EPS = 1e-08
ADAM_WD = 0.01
ADAM_STEP = 10

N_CHIPS = 4
PACK_W = 1024
PACK_ALIGN = 1024
PACK_TILE = 512
V7X_VMEM_LIMIT = 56 * 1024 * 1024

SHARDED = (("norm_gains", 2), ("mla_w_in", 1), ("mla_w_uq", 2), ("mla_w_ukv", 2), ("mla_w_o", 1),
           ("hgrn_w_in", 2), ("hgrn_w_o", 1), ("mlp_w1", 2), ("mlp_w2", 1))
REPLICATED = ("mla_q_norm", "mla_kv_norm", "hgrn_lb_logits", "hgrn_o_norm")
WEIGHTS = ("norm_gains", "mla_w_in", "mla_q_norm", "mla_kv_norm", "mla_w_uq", "mla_w_ukv", "mla_w_o",
           "hgrn_w_in", "hgrn_lb_logits", "hgrn_o_norm", "hgrn_w_o", "mlp_w1", "mlp_w2")


def _cparams(*semantics):
    return pltpu.CompilerParams(dimension_semantics=semantics, vmem_limit_bytes=V7X_VMEM_LIMIT)


def _sigmoid(x):
    return 1.0 / (1.0 + jnp.exp(-x))


def _mm(a, b, *, ta=False, tb=False, out_dtype=F32, tm=2048, tn=1024, tk=1024, epi=None, extra=None,
        name="mm", n=None, b_map=None, into=None, o_map=None):
    if ta:
        K, M = a.shape
    else:
        M, K = a.shape
    if b_map is not None:
        N = n
    elif tb:
        N, Kb = b.shape
    else:
        Kb, N = b.shape
    assert b_map is not None or K == Kb, (a.shape, b.shape, ta, tb)
    tm, tn = min(tm, M), min(tn, N)
    tk = K if (K <= 1024 and b_map is None) else min(tk, K)
    assert M % tm == 0 and N % tn == 0 and K % tk == 0, (M, N, K, tm, tn, tk)
    nk = K // tk
    a_spec = (pl.BlockSpec((tk, tm), lambda i, j, k: (k, i)) if ta
              else pl.BlockSpec((tm, tk), lambda i, j, k: (i, k)))
    if b_map is None:
        b_map = (lambda i, j, k: (j, k)) if tb else (lambda i, j, k: (k, j))
    b_spec = pl.BlockSpec((tn, tk) if tb else (tk, tn), b_map)
    o_spec = pl.BlockSpec((tm, tn), lambda i, j, k: (i, j))
    dims = (((0 if ta else 1,), (1 if tb else 0,)), ((), ()))
    in_specs = [a_spec, b_spec]
    operands = [a, b]
    aliases = {}
    if epi == "mul2r":
        in_specs.append(o_spec)
        operands.append(extra)
    if into is not None:
        assert epi is None
        in_specs.append(pl.BlockSpec(memory_space=pl.ANY))
        operands.append(into)
        aliases = {2: 0}
        out_dtype = into.dtype
        out_shape = jax.ShapeDtypeStruct(into.shape, into.dtype)
        out_specs = pl.BlockSpec((tm, tn), o_map)
    elif epi == "relu2":
        out_shape = (jax.ShapeDtypeStruct((M, N), BF16), jax.ShapeDtypeStruct((M, N), BF16))
        out_specs = (o_spec, o_spec)
    elif epi == "mul2r":
        out_shape = jax.ShapeDtypeStruct((M, N), BF16)
        out_specs = o_spec
    else:
        out_shape = jax.ShapeDtypeStruct((M, N), out_dtype)
        out_specs = o_spec
    n_in = len(operands)

    def body(*refs):
        a_ref, b_ref = refs[0], refs[1]
        outs = refs[n_in:n_in + (2 if epi == "relu2" else 1)]
        k = pl.program_id(2)

        def finish(acc):
            if epi == "relu2":
                r = jnp.maximum(acc, 0.0)
                outs[0][...] = (r * r).astype(BF16)
                outs[1][...] = r.astype(BF16)
            elif epi == "mul2r":
                outs[0][...] = (acc * (2.0 * refs[2][...].astype(F32))).astype(BF16)
            else:
                outs[0][...] = acc.astype(out_dtype)

        part = lax.dot_general(a_ref[...], b_ref[...], dims, preferred_element_type=F32)
        if nk == 1:
            finish(part)
            return
        acc_ref = refs[-1]

        @pl.when(k == 0)
        def _():
            acc_ref[...] = part

        @pl.when((k > 0) & (k < nk - 1))
        def _():
            acc_ref[...] += part

        @pl.when(k == nk - 1)
        def _():
            finish(acc_ref[...] + part)

    return pl.pallas_call(
        body, name=name, grid=(M // tm, N // tn, nk), in_specs=in_specs, out_specs=out_specs,
        out_shape=out_shape, scratch_shapes=[pltpu.VMEM((tm, tn), F32)] if nk > 1 else [],
        input_output_aliases=aliases,
        compiler_params=_cparams("parallel", "parallel", "arbitrary"))(*operands)


def _rms_rstd(x):
    return lax.rsqrt(jnp.mean(x * x, axis=-1, keepdims=True) + EPS)


def _rms_bwd_tile(x, g, dy):
    r = _rms_rstd(x)
    xh = x * r
    u = dy * g
    dx = r * (u - xh * jnp.mean(u * xh, axis=-1, keepdims=True))
    dg = jnp.sum(dy * xh, axis=0, keepdims=True)
    return dx, dg


def _row_tile(T):
    return min(256, T)


def _prenorm_fwd(x, g, name="prenorm_fwd"):
    T, D = x.shape
    tm = _row_tile(T)

    def body(x_ref, g_ref, a_ref):
        xv = x_ref[...]
        a_ref[...] = (xv * _rms_rstd(xv) * g_ref[...]).astype(BF16)

    row = pl.BlockSpec((tm, D), lambda i: (i, 0))
    vec = pl.BlockSpec((1, D), lambda i: (0, 0))
    return pl.pallas_call(body, name=name, grid=(T // tm,), in_specs=[row, vec], out_specs=row,
                          out_shape=jax.ShapeDtypeStruct((T, D), BF16),
                          compiler_params=_cparams("parallel"))(x, g)


def _resnorm_fwd(h, z, g_post, g_pre, name="resnorm_fwd"):
    T, D = h.shape
    tm = _row_tile(T)

    def body(h_ref, z_ref, gp_ref, gn_ref, hn_ref, a_ref):
        zv = z_ref[...]
        hn = h_ref[...] + zv * _rms_rstd(zv) * gp_ref[...]
        hn_ref[...] = hn
        a_ref[...] = (hn * _rms_rstd(hn) * gn_ref[...]).astype(BF16)

    row = pl.BlockSpec((tm, D), lambda i: (i, 0))
    vec = pl.BlockSpec((1, D), lambda i: (0, 0))
    return pl.pallas_call(body, name=name, grid=(T // tm,), in_specs=[row, row, vec, vec],
                          out_specs=(row, row),
                          out_shape=(jax.ShapeDtypeStruct((T, D), F32), jax.ShapeDtypeStruct((T, D), BF16)),
                          compiler_params=_cparams("parallel"))(h, z, g_post, g_pre)


def _resnorm_loss(h, z, g_post, target, name="resnorm_loss"):
    T, D = h.shape
    tm = _row_tile(T)

    def body(h_ref, z_ref, gp_ref, t_ref, dy_ref, sq_ref):
        zv = z_ref[...]
        err = h_ref[...] + zv * _rms_rstd(zv) * gp_ref[...] - t_ref[...]
        dy_ref[...] = err * (1.0 / D)

        @pl.when(pl.program_id(0) == 0)
        def _():
            sq_ref[...] = jnp.zeros_like(sq_ref)

        sq_ref[...] += jnp.sum(err * err, axis=0, keepdims=True)

    row = pl.BlockSpec((tm, D), lambda i: (i, 0))
    vec = pl.BlockSpec((1, D), lambda i: (0, 0))
    return pl.pallas_call(body, name=name, grid=(T // tm,), in_specs=[row, row, vec, row],
                          out_specs=(row, vec),
                          out_shape=(jax.ShapeDtypeStruct((T, D), F32), jax.ShapeDtypeStruct((1, D), F32)),
                          compiler_params=_cparams("arbitrary"))(h, z, g_post, target)


def _resnorm_bwd(z, g_post, dh, h_new=None, da=None, g_pre=None, name="resnorm_bwd"):
    T, D = z.shape
    tm = _row_tile(T)
    has_next = h_new is not None
    row = pl.BlockSpec((tm, D), lambda i: (i, 0))
    vec = pl.BlockSpec((1, D), lambda i: (0, 0))

    if has_next:
        def body(z_ref, gp_ref, dh_ref, hn_ref, da_ref, gn_ref, t_ref, dz_ref, dgp_ref, dgn_ref):
            first = pl.program_id(0) == 0

            @pl.when(first)
            def _():
                dgp_ref[...] = jnp.zeros_like(dgp_ref)
                dgn_ref[...] = jnp.zeros_like(dgn_ref)

            dpre, dgn = _rms_bwd_tile(hn_ref[...], gn_ref[...], da_ref[...])
            t = dh_ref[...] + dpre
            t_ref[...] = t
            dz, dgp = _rms_bwd_tile(z_ref[...], gp_ref[...], t)
            dz_ref[...] = dz.astype(BF16)
            dgp_ref[...] += dgp
            dgn_ref[...] += dgn

        return pl.pallas_call(
            body, name=name, grid=(T // tm,), in_specs=[row, vec, row, row, row, vec],
            out_specs=(row, row, vec, vec),
            out_shape=(jax.ShapeDtypeStruct((T, D), F32), jax.ShapeDtypeStruct((T, D), BF16),
                       jax.ShapeDtypeStruct((1, D), F32), jax.ShapeDtypeStruct((1, D), F32)),
            compiler_params=_cparams("arbitrary"))(z, g_post, dh, h_new, da, g_pre)

    def body_last(z_ref, gp_ref, dh_ref, dz_ref, dgp_ref):
        @pl.when(pl.program_id(0) == 0)
        def _():
            dgp_ref[...] = jnp.zeros_like(dgp_ref)

        dz, dgp = _rms_bwd_tile(z_ref[...], gp_ref[...], dh_ref[...])
        dz_ref[...] = dz.astype(BF16)
        dgp_ref[...] += dgp

    return pl.pallas_call(
        body_last, name=name, grid=(T // tm,), in_specs=[row, vec, row], out_specs=(row, vec),
        out_shape=(jax.ShapeDtypeStruct((T, D), BF16), jax.ShapeDtypeStruct((1, D), F32)),
        compiler_params=_cparams("arbitrary"))(z, g_post, dh)


def _prenorm_bwd(x, g, dh, da, name="prenorm_bwd"):
    T, D = x.shape
    tm = _row_tile(T)

    def body(x_ref, g_ref, dh_ref, da_ref, dx_ref, dg_ref):
        @pl.when(pl.program_id(0) == 0)
        def _():
            dg_ref[...] = jnp.zeros_like(dg_ref)

        dpre, dg = _rms_bwd_tile(x_ref[...], g_ref[...], da_ref[...])
        dx_ref[...] = dh_ref[...] + dpre
        dg_ref[...] += dg

    row = pl.BlockSpec((tm, D), lambda i: (i, 0))
    vec = pl.BlockSpec((1, D), lambda i: (0, 0))
    return pl.pallas_call(
        body, name=name, grid=(T // tm,), in_specs=[row, vec, row, row], out_specs=(row, vec),
        out_shape=(jax.ShapeDtypeStruct((T, D), F32), jax.ShapeDtypeStruct((1, D), F32)),
        compiler_params=_cparams("arbitrary"))(x, g, dh, da)


class _Packed:
    def __init__(self, big, misc_rows):
        self.big = tuple(big)
        self.off = {}
        r = 0
        for name, rows in big:
            self.off[name] = r
            r += rows
        self.misc, self.misc_rows = r, misc_rows
        self.rows = -(-(r + misc_rows) // PACK_ALIGN) * PACK_ALIGN

    def block(self, name, layer, unit):
        r = self.off[name]
        assert r % unit == 0 and self.rows % unit == 0
        return r // unit, self.rows // unit


def _col_sharded(pk, name, layer, unit):
    base, stride = pk.block(name, layer, unit)
    return (lambda i, j, k: (j * stride + base, 0)), (lambda i, j, k: (k * stride + base, 0))


def _row_sharded(pk, name, layer, unit):
    base, stride = pk.block(name, layer, unit)
    return ((lambda i, j, k: (k * stride + base, 0)), (lambda i, j, k: (j * stride + base, 0)),
            (lambda i, j, k: (i * stride + base, 0)))


def _mlp_fwd(a, wbuf, pk, layer):
    D = a.shape[1]
    by_n, _ = _col_sharded(pk, "mlp_w1", layer, D)
    by_k, _, _ = _row_sharded(pk, "mlp_w2", layer, D)
    act, r = _mm(a, wbuf, n=4 * D, b_map=by_n, tk=D, tn=D, epi="relu2", name="mlp_up")
    u = _mm(act, wbuf, n=D, b_map=by_k, tk=D, tn=D, name="mlp_down")
    return u, (a, act, r)


def _mlp_bwd(du, saved, wbuf, gbuf, pk, layer):
    a, act, r = saved
    D = a.shape[1]
    w1_by_n, w1_by_k = _col_sharded(pk, "mlp_w1", layer, D)
    _, w2_by_n, w2_by_m = _row_sharded(pk, "mlp_w2", layer, D)
    dz1 = _mm(du, wbuf, tb=True, n=4 * D, b_map=w2_by_n, tn=D, tk=D, epi="mul2r", extra=r, name="mlp_down_dx")
    gbuf = _mm(act, du, ta=True, into=gbuf, o_map=w2_by_m, tm=D, tn=D, name="mlp_down_dw")
    gbuf = _mm(a, dz1, ta=True, into=gbuf, o_map=w1_by_n, tm=D, tn=D, name="mlp_up_dw")
    da = _mm(dz1, wbuf, tb=True, n=D, b_map=w1_by_k, tn=D, tk=D, name="mlp_up_dx")
    return da, gbuf


def _rope_swap(t):
    n = t.shape[-1]
    lane = lax.broadcasted_iota(jnp.int32, t.shape, t.ndim - 1)
    half = MLA_ROPE // 2
    first = (lane & (MLA_ROPE - 1)) < half
    return jnp.where(first, pltpu.roll(t, n - half, t.ndim - 1), pltpu.roll(t, half, t.ndim - 1))


def _mla_mid_fwd(proj, q_norm, kv_norm, w_uq, w_ukv, cc, ss):
    T, PW = proj.shape
    QL, KVL = q_norm.shape[-1], kv_norm.shape[-1]
    H = MLA_HEADS
    assert PW == QL + KVL + 128
    tm = _row_tile(T)

    def body(p_ref, qn_ref, kn_ref, wq_ref, wkv_ref, cc_ref, ss_ref,
             cq_ref, ckv_ref, q_ref, k_ref, v_ref):
        cq = p_ref[:, 0:QL]
        ckv = p_ref[:, QL:QL + KVL]
        kr = p_ref[:, QL + KVL:QL + KVL + 128]
        c, s = cc_ref[...], ss_ref[...]
        cqn = (cq * _rms_rstd(cq) * qn_ref[...]).astype(BF16)
        ckvn = (ckv * _rms_rstd(ckv) * kn_ref[...]).astype(BF16)
        cq_ref[...] = cqn
        ckv_ref[...] = ckvn
        q = jnp.dot(cqn, wq_ref[...], preferred_element_type=F32)
        kv = jnp.dot(ckvn, wkv_ref[...], preferred_element_type=F32)
        krf = (kr * c + _rope_swap(kr) * s).astype(BF16)
        for h in range(H):
            o = h * MLA_QK_PAD
            q_ref[:, o:o + MLA_NOPE] = (q[:, o:o + MLA_NOPE] * MLA_SCALE).astype(BF16)
            qr = q[:, o + MLA_NOPE:o + MLA_QK_PAD]
            q_ref[:, o + MLA_NOPE:o + MLA_QK_PAD] = ((qr * c + _rope_swap(qr) * s) * MLA_SCALE).astype(BF16)
            k_ref[:, o:o + MLA_NOPE] = kv[:, o:o + MLA_NOPE].astype(BF16)
            k_ref[:, o + MLA_NOPE:o + MLA_QK_PAD] = krf
            v_ref[:, h * MLA_V:(h + 1) * MLA_V] = kv[:, o + MLA_NOPE:o + MLA_QK_PAD].astype(BF16)

    def row(w):
        return pl.BlockSpec((tm, w), lambda i: (i, 0))

    def full(shape):
        return pl.BlockSpec(shape, lambda i: (0, 0))

    return pl.pallas_call(
        body, name="mla_mid_fwd", grid=(T // tm,),
        in_specs=[row(PW), full((1, QL)), full((1, KVL)), full(w_uq.shape), full(w_ukv.shape),
                  row(128), row(128)],
        out_specs=(row(QL), row(KVL), row(H * MLA_QK_PAD), row(H * MLA_QK_PAD), row(H * MLA_V)),
        out_shape=(jax.ShapeDtypeStruct((T, QL), BF16), jax.ShapeDtypeStruct((T, KVL), BF16),
                   jax.ShapeDtypeStruct((T, H * MLA_QK_PAD), BF16),
                   jax.ShapeDtypeStruct((T, H * MLA_QK_PAD), BF16),
                   jax.ShapeDtypeStruct((T, H * MLA_V), BF16)),
        compiler_params=_cparams("parallel"))(proj, q_norm, kv_norm, w_uq, w_ukv, cc, ss)


def _mla_mid_bwd(proj, q_norm, kv_norm, w_uq, w_ukv, cc, ss, dq, dk, dv):
    T, PW = proj.shape
    QL, KVL = q_norm.shape[-1], kv_norm.shape[-1]
    H = MLA_HEADS
    tm = _row_tile(T)
    nt = (((1,), (1,)), ((), ()))

    def body(p_ref, qn_ref, kn_ref, wq_ref, wkv_ref, cc_ref, ss_ref, dq_ref, dk_ref, dv_ref,
             dqp_ref, dkv_ref, dp_ref, dqn_ref, dkn_ref):
        @pl.when(pl.program_id(0) == 0)
        def _():
            dqn_ref[...] = jnp.zeros_like(dqn_ref)
            dkn_ref[...] = jnp.zeros_like(dkn_ref)

        c, s = cc_ref[...], ss_ref[...]
        dkr = jnp.zeros((tm, 128), F32)
        for h in range(H):
            o = h * MLA_QK_PAD
            dqp_ref[:, o:o + MLA_NOPE] = (dq_ref[:, o:o + MLA_NOPE] * MLA_SCALE).astype(BF16)
            dqr = dq_ref[:, o + MLA_NOPE:o + MLA_QK_PAD] * MLA_SCALE
            dqp_ref[:, o + MLA_NOPE:o + MLA_QK_PAD] = (dqr * c + _rope_swap(dqr * s)).astype(BF16)
            dkv_ref[:, o:o + MLA_NOPE] = dk_ref[:, o:o + MLA_NOPE].astype(BF16)
            dkv_ref[:, o + MLA_NOPE:o + MLA_QK_PAD] = dv_ref[:, h * MLA_V:(h + 1) * MLA_V].astype(BF16)
            dkr = dkr + dk_ref[:, o + MLA_NOPE:o + MLA_QK_PAD]
        dcqn = lax.dot_general(dqp_ref[...], wq_ref[...], nt, preferred_element_type=F32)
        dckvn = lax.dot_general(dkv_ref[...], wkv_ref[...], nt, preferred_element_type=F32)
        dcq, dqn = _rms_bwd_tile(p_ref[:, 0:QL], qn_ref[...], dcqn)
        dckv, dkn = _rms_bwd_tile(p_ref[:, QL:QL + KVL], kn_ref[...], dckvn)
        dp_ref[:, 0:QL] = dcq.astype(BF16)
        dp_ref[:, QL:QL + KVL] = dckv.astype(BF16)
        dp_ref[:, QL + KVL:QL + KVL + 128] = (dkr * c + _rope_swap(dkr * s)).astype(BF16)
        dqn_ref[...] += dqn
        dkn_ref[...] += dkn

    def row(w):
        return pl.BlockSpec((tm, w), lambda i: (i, 0))

    def full(shape):
        return pl.BlockSpec(shape, lambda i: (0, 0))

    return pl.pallas_call(
        body, name="mla_mid_bwd", grid=(T // tm,),
        in_specs=[row(PW), full((1, QL)), full((1, KVL)), full(w_uq.shape), full(w_ukv.shape),
                  row(128), row(128), row(H * MLA_QK_PAD), row(H * MLA_QK_PAD), row(H * MLA_V)],
        out_specs=(row(H * MLA_QK_PAD), row(H * MLA_QK_PAD), row(PW), full((1, QL)), full((1, KVL))),
        out_shape=(jax.ShapeDtypeStruct((T, H * MLA_QK_PAD), BF16),
                   jax.ShapeDtypeStruct((T, H * MLA_QK_PAD), BF16),
                   jax.ShapeDtypeStruct((T, PW), BF16),
                   jax.ShapeDtypeStruct((1, QL), F32), jax.ShapeDtypeStruct((1, KVL), F32)),
        compiler_params=_cparams("arbitrary"))(proj, q_norm, kv_norm, w_uq, w_ukv, cc, ss, dq, dk, dv)


def _attn_tile(T):
    return min(1024, T)


def _attn_pairs(n, by_key):
    if by_key:
        pairs = [(qi, ki) for ki in range(n) for qi in range(ki, n)]
    else:
        pairs = [(qi, ki) for qi in range(n) for ki in range(qi + 1)]
    return (jnp.asarray([p[0] for p in pairs], jnp.int32), jnp.asarray([p[1] for p in pairs], jnp.int32))


def _scores(q, k, diagonal):
    s = lax.dot_general(q, k, (((1,), (1,)), ((), ())), preferred_element_type=F32)
    if diagonal:
        rows = lax.broadcasted_iota(jnp.int32, s.shape, 0)
        cols = lax.broadcasted_iota(jnp.int32, s.shape, 1)
        s = jnp.where(rows >= cols, s, -jnp.inf)
    return s


def _attn_fwd(q, k, v):
    T = q.shape[0]
    H, DQ, DV = MLA_HEADS, MLA_QK_PAD, MLA_V
    tq = _attn_tile(T)
    nq = T // tq
    scale = float(MLA_NOPE + MLA_ROPE) ** -0.5
    G = MLA_HEADS_PER_STEP
    qi_tab, ki_tab = _attn_pairs(nq, by_key=False)

    def body(qi_ref, ki_ref, q_ref, k_ref, v_ref, o_ref, lse_ref, *scratch):
        m_refs, l_refs, acc_refs = scratch[0:G], scratch[G:2 * G], scratch[2 * G:3 * G]
        p = pl.program_id(1)
        qi, ki = qi_ref[p], ki_ref[p]

        @pl.when(ki == 0)
        def _():
            for g in range(G):
                m_refs[g][...] = jnp.full_like(m_refs[g], -jnp.inf)
                l_refs[g][...] = jnp.zeros_like(l_refs[g])
                acc_refs[g][...] = jnp.zeros_like(acc_refs[g])

        def update(diagonal):
            for g in range(G):
                qs, vs = slice(g * DQ, (g + 1) * DQ), slice(g * DV, (g + 1) * DV)
                s = _scores(q_ref[:, qs], k_ref[:, qs], diagonal)
                m_prev = m_refs[g][...]
                m_new = jnp.maximum(m_prev, jnp.max(s, axis=1, keepdims=True))
                alpha = jnp.exp(m_prev - m_new)
                pr = jnp.exp(s - m_new)
                l_refs[g][...] = alpha * l_refs[g][...] + jnp.sum(pr, axis=1, keepdims=True)
                acc_refs[g][...] = alpha * acc_refs[g][...] + jnp.dot(pr.astype(BF16), v_ref[:, vs],
                                                                      preferred_element_type=F32)
                m_refs[g][...] = m_new

        @pl.when(ki < qi)
        def _():
            update(False)

        @pl.when(ki == qi)
        def _():
            update(True)
            for g in range(G):
                vs = slice(g * DV, (g + 1) * DV)
                o_ref[:, vs] = (acc_refs[g][...] / l_refs[g][...]).astype(BF16)
                lse_ref[g] = m_refs[g][...] + jnp.log(l_refs[g][...])

    return pl.pallas_call(
        body, name="attn_fwd",
        grid_spec=pltpu.PrefetchScalarGridSpec(
            num_scalar_prefetch=2, grid=(H // G, int(qi_tab.shape[0])),
            in_specs=[pl.BlockSpec((tq, G * DQ), lambda h, p, qt, kt: (qt[p], h)),
                      pl.BlockSpec((tq, G * DQ), lambda h, p, qt, kt: (kt[p], h)),
                      pl.BlockSpec((tq, G * DV), lambda h, p, qt, kt: (kt[p], h))],
            out_specs=(pl.BlockSpec((tq, G * DV), lambda h, p, qt, kt: (qt[p], h)),
                       pl.BlockSpec((G, tq, 1), lambda h, p, qt, kt: (h, qt[p], 0))),
            scratch_shapes=([pltpu.VMEM((tq, 1), F32)] * (2 * G) + [pltpu.VMEM((tq, DV), F32)] * G)),
        out_shape=(jax.ShapeDtypeStruct((T, H * DV), BF16), jax.ShapeDtypeStruct((H, T, 1), F32)),
        compiler_params=_cparams("parallel", "arbitrary"))(qi_tab, ki_tab, q, k, v)


def _attn_bwd(q, k, v, o, do, lse):
    T = q.shape[0]
    H, DQ, DV = MLA_HEADS, MLA_QK_PAD, MLA_V
    tq = _attn_tile(T)
    nq = T // tq
    scale = float(MLA_NOPE + MLA_ROPE) ** -0.5
    tn = (((0,), (0,)), ((), ()))
    nt = (((1,), (1,)), ((), ()))
    G = MLA_HEADS_PER_STEP
    qi_tab, ki_tab = _attn_pairs(nq, by_key=True)

    def body(qi_ref, ki_ref, q_ref, k_ref, v_ref, o_ref, do_ref, lse_ref, dq_ref, dk_ref, dv_ref,
             dk_acc, dv_acc):
        p = pl.program_id(1)
        qi, ki = qi_ref[p], ki_ref[p]

        @pl.when(p == 0)
        def _():
            dq_ref[...] = jnp.zeros_like(dq_ref)

        @pl.when(qi == ki)
        def _():
            dk_acc[...] = jnp.zeros_like(dk_acc)
            dv_acc[...] = jnp.zeros_like(dv_acc)

        def step(diagonal):
            rows = pl.ds(pl.multiple_of(qi * tq, tq), tq)
            for g in range(G):
                qs, vs = slice(g * DQ, (g + 1) * DQ), slice(g * DV, (g + 1) * DV)
                dof = do_ref[:, vs]
                delta = jnp.sum(dof.astype(F32) * o_ref[:, vs].astype(F32), axis=1, keepdims=True)
                s = _scores(q_ref[:, qs], k_ref[:, qs], diagonal)
                pr = jnp.exp(s - lse_ref[g])
                dp = lax.dot_general(dof, v_ref[:, vs], nt, preferred_element_type=F32)
                ds = (pr * (dp - delta)).astype(BF16)
                dv_acc[:, vs] += lax.dot_general(pr.astype(BF16), dof, tn, preferred_element_type=F32)
                dk_acc[:, qs] += lax.dot_general(ds, q_ref[:, qs], tn, preferred_element_type=F32)
                dq_ref[rows, qs] += jnp.dot(ds, k_ref[:, qs], preferred_element_type=F32)

        @pl.when(qi == ki)
        def _():
            step(True)

        @pl.when(qi > ki)
        def _():
            step(False)

        @pl.when(qi == nq - 1)
        def _():
            dk_ref[...] = dk_acc[...]
            dv_ref[...] = dv_acc[...]

    qspec = pl.BlockSpec((tq, G * DQ), lambda h, p, qt, kt: (qt[p], h))
    ospec = pl.BlockSpec((tq, G * DV), lambda h, p, qt, kt: (qt[p], h))
    kspec = pl.BlockSpec((tq, G * DQ), lambda h, p, qt, kt: (kt[p], h))
    vspec = pl.BlockSpec((tq, G * DV), lambda h, p, qt, kt: (kt[p], h))
    return pl.pallas_call(
        body, name="attn_bwd",
        grid_spec=pltpu.PrefetchScalarGridSpec(
            num_scalar_prefetch=2, grid=(H // G, int(qi_tab.shape[0])),
            in_specs=[qspec, kspec, vspec, ospec, ospec,
                      pl.BlockSpec((G, tq, 1), lambda h, p, qt, kt: (h, qt[p], 0))],
            out_specs=(pl.BlockSpec((T, G * DQ), lambda h, p, qt, kt: (0, h)), kspec, vspec),
            scratch_shapes=[pltpu.VMEM((tq, G * DQ), F32), pltpu.VMEM((tq, G * DV), F32)]),
        out_shape=(jax.ShapeDtypeStruct((T, H * DQ), F32), jax.ShapeDtypeStruct((T, H * DQ), F32),
                   jax.ShapeDtypeStruct((T, H * DV), F32)),
        compiler_params=_cparams("parallel", "arbitrary"))(qi_tab, ki_tab, q, k, v, o, do, lse)


def _mla_fwd(a, w, cc, ss, wbuf, pk, slot):
    D = a.shape[1]
    by_k, _, _ = _row_sharded(pk, "mla_w_o", slot, D // N_CHIPS)
    proj = _mm(a, w["w_in"], name="mla_in")
    cqn, ckvn, q, k, v = _mla_mid_fwd(proj, w["q_norm"], w["kv_norm"], w["w_uq"], w["w_ukv"], cc, ss)
    o, lse = _attn_fwd(q, k, v)
    m = _mm(o, wbuf, n=D, b_map=by_k, tm=2048, tk=D // N_CHIPS, tn=D, name="mla_out")
    return m, (a, proj, cqn, ckvn, q, k, v, o, lse)


def _mla_bwd(dm, saved, w, cc, ss, wbuf, gbuf, pk, slot):
    a, proj, cqn, ckvn, q, k, v, o, lse = saved
    D = a.shape[1]
    _, by_n, by_m = _row_sharded(pk, "mla_w_o", slot, D // N_CHIPS)
    do = _mm(dm, wbuf, tb=True, n=o.shape[1], b_map=by_n, tm=2048, tn=D // N_CHIPS, tk=D, out_dtype=BF16,
             name="mla_out_dx")
    gbuf = _mm(o, dm, ta=True, into=gbuf, o_map=by_m, tm=D // N_CHIPS, tn=D, tk=2048, name="mla_out_dw")
    dq, dk, dv = _attn_bwd(q, k, v, o, do, lse)
    dqp, dkv, dproj, dqn, dkn = _mla_mid_bwd(proj, w["q_norm"], w["kv_norm"], w["w_uq"], w["w_ukv"],
                                             cc, ss, dq, dk, dv)
    dw_uq = _mm(cqn, dqp, ta=True, out_dtype=BF16, name="mla_uq_dw")
    dw_ukv = _mm(ckvn, dkv, ta=True, out_dtype=BF16, name="mla_ukv_dw")
    dw_in = _mm(a, dproj, ta=True, out_dtype=BF16, name="mla_in_dw")
    da = _mm(dproj, w["w_in"], tb=True, name="mla_in_dx")
    return da, gbuf, dict(w_in=dw_in, w_uq=dw_uq, w_ukv=dw_ukv, q_norm=dqn, kv_norm=dkn)


def _split_dot(mat, x, parts):
    acc = None
    rem = x
    for _ in range(parts):
        piece = rem.astype(BF16)
        term = jnp.dot(mat, piece, preferred_element_type=F32)
        acc = term if acc is None else acc + term
        rem = rem - piece.astype(F32)
    return acc


def _chunk_mats(tb):
    C = HGRN_CHUNK
    assert C & (C - 1) == 0
    r = lax.broadcasted_iota(jnp.int32, (tb, tb), 0)
    s = lax.broadcasted_iota(jnp.int32, (tb, tb), 1)
    start = r & ~(C - 1)
    same = start == (s & ~(C - 1))
    ref = start + C // 2
    last = start + C - 1
    one, zero = jnp.float32(1.0), jnp.float32(0.0)
    cum = jnp.where(same & (s <= r), one, zero)
    rel = cum - jnp.where(same & (s <= ref), one, zero)
    rest = jnp.where(same & (s > r) & (s <= last), one, zero)
    rev = jnp.where(same & (s >= r), one, zero)
    ones = jnp.where(same, one, zero)
    causal = same & (s <= r)
    return cum, rel, rest, rev, ones, causal


def _hgrn_gates(p_ref, lb, HK):
    qx = p_ref[:, 0:HK]
    fx = p_ref[:, HK:2 * HK]
    sf = _sigmoid(fx)
    f = lb + (1.0 - lb) * sf
    sq = _sigmoid(qx)
    return qx, sq, qx * sq, sf, f, 1.0 - f, jnp.log(f)


def _hgrn_fwd(proj, lb, o_norm):
    T = proj.shape[0]
    H, C = HGRN_HEADS, HGRN_CHUNK
    HK = proj.shape[1] // 4
    DK = HK // H
    tb = min(HGRN_BLOCK, T)
    ncb = tb // C
    nt = (((1,), (1,)), ((), ()))
    tn = (((0,), (0,)), ((), ()))

    def body(p_ref, lb_ref, on_ref, y_ref, o_ref, st_ref, state, oacc):
        @pl.when(pl.program_id(0) == 0)
        def _():
            state[...] = jnp.zeros_like(state)

        cum, rel, rest, _, _, causal = _chunk_mats(tb)
        _, _, q, _, f, k, logf = _hgrn_gates(p_ref, lb_ref[...], HK)
        b = _split_dot(cum.astype(BF16), logf, 3)
        brel = _split_dot(rel.astype(BF16), logf, 3)
        brest = _split_dot(rest.astype(BF16), logf, 3)
        eb = jnp.exp(b)
        q_rel = (q * jnp.exp(brel)).astype(BF16)
        k_rel = (k * jnp.exp(-brel)).astype(BF16)
        q_dec = (q * eb).astype(BF16)
        k_dec = (k * jnp.exp(brest)).astype(BF16)
        v = p_ref[:, 2 * HK:3 * HK].astype(BF16)
        for h in range(H):
            hs = slice(h * DK, (h + 1) * DK)
            a = lax.dot_general(q_rel[:, hs], k_rel[:, hs], nt, preferred_element_type=F32)
            a = jnp.where(causal, a, 0.0).astype(BF16)
            oacc[:, hs] = jnp.dot(a, v[:, hs], preferred_element_type=F32)
            for j in range(ncb):
                rs = slice(j * C, (j + 1) * C)
                st = state[h]
                st_ref[j, h] = st
                oacc[rs, hs] += lax.dot_general(q_dec[rs, hs], st.astype(BF16), nt,
                                                preferred_element_type=F32)
                dec = jnp.exp(jnp.sum(logf[rs, hs], axis=0, keepdims=True))
                state[h] = dec * st + lax.dot_general(v[rs, hs], k_dec[rs, hs], tn,
                                                      preferred_element_type=F32)
        o = oacc[...]
        o_ref[...] = o
        gx = p_ref[:, 3 * HK:4 * HK]
        gate = gx * _sigmoid(gx)
        for h in range(H):
            hs = slice(h * DK, (h + 1) * DK)
            oh = o[:, hs]
            y_ref[:, hs] = (oh * _rms_rstd(oh) * on_ref[...] * gate[:, hs]).astype(BF16)

    return pl.pallas_call(
        body, name="hgrn_fwd", grid=(T // tb,),
        in_specs=[pl.BlockSpec((tb, 4 * HK), lambda i: (i, 0)),
                  pl.BlockSpec((1, HK), lambda i: (0, 0)),
                  pl.BlockSpec((1, DK), lambda i: (0, 0))],
        out_specs=(pl.BlockSpec((tb, HK), lambda i: (i, 0)),
                   pl.BlockSpec((tb, HK), lambda i: (i, 0)),
                   pl.BlockSpec((ncb, H, DK, DK), lambda i: (i, 0, 0, 0))),
        out_shape=(jax.ShapeDtypeStruct((T, HK), BF16), jax.ShapeDtypeStruct((T, HK), F32),
                   jax.ShapeDtypeStruct((T // C, H, DK, DK), F32)),
        scratch_shapes=[pltpu.VMEM((H, DK, DK), F32), pltpu.VMEM((tb, HK), F32)],
        compiler_params=_cparams("arbitrary"))(proj, lb, o_norm)


def _hgrn_bwd(proj, lb, o_norm, o, states, dy):
    T = proj.shape[0]
    H, C = HGRN_HEADS, HGRN_CHUNK
    HK = proj.shape[1] // 4
    DK = HK // H
    tb = min(HGRN_BLOCK, T)
    ncb = tb // C
    nb = T // tb
    nt = (((1,), (1,)), ((), ()))
    tn = (((0,), (0,)), ((), ()))

    def body(p_ref, lb_ref, on_ref, o_ref, st_ref, dy_ref, dp_ref, dlb_ref, don_ref,
             dstate, dqr_s, dkr_s, dqd_s, dkd_s, dv_s, do_s, e_s):
        @pl.when(pl.program_id(0) == 0)
        def _():
            dstate[...] = jnp.zeros_like(dstate)
            dlb_ref[...] = jnp.zeros_like(dlb_ref)
            don_ref[...] = jnp.zeros_like(don_ref)

        cum, rel, rest, rev, ones, causal = _chunk_mats(tb)
        lb = lb_ref[...]
        qx, sq, q, sf, f, k, logf = _hgrn_gates(p_ref, lb, HK)
        b = _split_dot(cum.astype(BF16), logf, 3)
        brel = _split_dot(rel.astype(BF16), logf, 3)
        brest = _split_dot(rest.astype(BF16), logf, 3)
        eb = jnp.exp(b)
        erel = jnp.exp(brel)
        enrel = jnp.exp(-brel)
        erest = jnp.exp(brest)
        q_rel_f, k_rel_f, q_dec_f, k_dec_f = q * erel, k * enrel, q * eb, k * erest
        q_rel, k_rel = q_rel_f.astype(BF16), k_rel_f.astype(BF16)
        q_dec, k_dec = q_dec_f.astype(BF16), k_dec_f.astype(BF16)
        v = p_ref[:, 2 * HK:3 * HK].astype(BF16)

        gx = p_ref[:, 3 * HK:4 * HK]
        sg = _sigmoid(gx)
        gate = gx * sg
        dy = dy_ref[...]
        ov = o_ref[...]
        on = on_ref[...]
        don = jnp.zeros((1, DK), F32)
        for h in range(H):
            hs = slice(h * DK, (h + 1) * DK)
            oh = ov[:, hs]
            r = _rms_rstd(oh)
            xh = oh * r
            d_on = dy[:, hs] * gate[:, hs]
            don = don + jnp.sum(d_on * xh, axis=0, keepdims=True)
            u = d_on * on
            do_s[:, hs] = r * (u - xh * jnp.mean(u * xh, axis=-1, keepdims=True))
            dp_ref[:, 3 * HK + h * DK:3 * HK + (h + 1) * DK] = (
                dy[:, hs] * xh * on * (sg[:, hs] * (1.0 + gx[:, hs] * (1.0 - sg[:, hs])))).astype(BF16)
        don_ref[...] += don

        for h in range(H):
            hs = slice(h * DK, (h + 1) * DK)
            doh = do_s[:, hs].astype(BF16)
            a = lax.dot_general(q_rel[:, hs], k_rel[:, hs], nt, preferred_element_type=F32)
            a = jnp.where(causal, a, 0.0).astype(BF16)
            da = lax.dot_general(doh, v[:, hs], nt, preferred_element_type=F32)
            da = jnp.where(causal, da, 0.0).astype(BF16)
            dv_s[:, hs] = lax.dot_general(a, doh, tn, preferred_element_type=F32)
            dqr_s[:, hs] = jnp.dot(da, k_rel[:, hs], preferred_element_type=F32)
            dkr_s[:, hs] = lax.dot_general(da, q_rel[:, hs], tn, preferred_element_type=F32)
            for j in reversed(range(ncb)):
                rs = slice(j * C, (j + 1) * C)
                dst = dstate[h]
                dstb = dst.astype(BF16)
                st = st_ref[j, h]
                dkd_s[rs, hs] = jnp.dot(v[rs, hs], dstb, preferred_element_type=F32)
                dv_s[rs, hs] += lax.dot_general(k_dec[rs, hs], dstb, nt, preferred_element_type=F32)
                dec = jnp.exp(jnp.sum(logf[rs, hs], axis=0, keepdims=True))
                e_s[rs, hs] = jnp.broadcast_to(jnp.sum(dst * st, axis=0, keepdims=True) * dec, (C, DK))
                dqd_s[rs, hs] = jnp.dot(doh[rs], st.astype(BF16), preferred_element_type=F32)
                dstate[h] = dec * dst + lax.dot_general(doh[rs], q_dec[rs, hs], tn,
                                                        preferred_element_type=F32)

        dqr, dkr, dqd, dkd = dqr_s[...], dkr_s[...], dqd_s[...], dkd_s[...]
        kdk = dkd * k_dec_f
        db = dqr * q_rel_f - dkr * k_rel_f + dqd * q_dec_f - kdk
        dlogf = _split_dot(rev.astype(BF16), db, 2) + _split_dot(ones.astype(BF16), kdk, 2) + e_s[...]
        dk = dkr * enrel + dkd * erest
        df = dlogf / f - dk
        dlb_ref[...] += jnp.sum(df * (1.0 - sf), axis=0, keepdims=True)
        dq = dqr * erel + dqd * eb
        dp_ref[:, 0:HK] = (dq * (sq * (1.0 + qx * (1.0 - sq)))).astype(BF16)
        dp_ref[:, HK:2 * HK] = (df * (1.0 - lb) * sf * (1.0 - sf)).astype(BF16)
        dp_ref[:, 2 * HK:3 * HK] = dv_s[...].astype(BF16)

    rev_row = lambda w: pl.BlockSpec((tb, w), lambda i: (nb - 1 - i, 0))
    vec = lambda w: pl.BlockSpec((1, w), lambda i: (0, 0))
    scr = pltpu.VMEM((tb, HK), F32)
    return pl.pallas_call(
        body, name="hgrn_bwd", grid=(nb,),
        in_specs=[rev_row(4 * HK), vec(HK), vec(DK), rev_row(HK),
                  pl.BlockSpec((ncb, H, DK, DK), lambda i: (nb - 1 - i, 0, 0, 0)), rev_row(HK)],
        out_specs=(rev_row(4 * HK), vec(HK), vec(DK)),
        out_shape=(jax.ShapeDtypeStruct((T, 4 * HK), BF16), jax.ShapeDtypeStruct((1, HK), F32),
                   jax.ShapeDtypeStruct((1, DK), F32)),
        scratch_shapes=[pltpu.VMEM((H, DK, DK), F32), scr, scr, scr, scr, scr, scr, scr],
        compiler_params=_cparams("arbitrary"))(proj, lb, o_norm, o, states, dy)


def _hgrn_layer_fwd(a, o_norm, lb, wbuf, pk, slot):
    D = a.shape[1]
    in_by_n, _ = _col_sharded(pk, "hgrn_w_in", slot, D)
    out_by_k, _, _ = _row_sharded(pk, "hgrn_w_o", slot, D // N_CHIPS)
    proj = _mm(a, wbuf, n=4 * D, b_map=in_by_n, tk=D, tn=D, name="hgrn_in")
    y, o, states = _hgrn_fwd(proj, lb, o_norm)
    m = _mm(y, wbuf, n=D, b_map=out_by_k, tm=2048, tk=D // N_CHIPS, tn=D, name="hgrn_out")
    return m, (a, proj, y, o, states)


def _hgrn_layer_bwd(dm, saved, o_norm, lb, wbuf, gbuf, pk, slot):
    a, proj, y, o, states = saved
    D = a.shape[1]
    in_by_n, in_by_k = _col_sharded(pk, "hgrn_w_in", slot, D)
    _, out_by_n, out_by_m = _row_sharded(pk, "hgrn_w_o", slot, D // N_CHIPS)
    dy = _mm(dm, wbuf, tb=True, n=y.shape[1], b_map=out_by_n, tm=2048, tn=D // N_CHIPS, tk=D,
             name="hgrn_out_dx")
    gbuf = _mm(y, dm, ta=True, into=gbuf, o_map=out_by_m, tm=D // N_CHIPS, tn=D, tk=2048, name="hgrn_out_dw")
    dproj, dlb, don = _hgrn_bwd(proj, lb, o_norm, o, states, dy)
    gbuf = _mm(a, dproj, ta=True, into=gbuf, o_map=in_by_n, tm=D, tn=D, name="hgrn_in_dw")
    da = _mm(dproj, wbuf, tb=True, n=D, b_map=in_by_k, tn=D, tk=D, name="hgrn_in_dx")
    return da, gbuf, dict(o_norm=don, lb=dlb)


def _lower_bounds(lb_logits):
    p = jax.nn.softmax(lb_logits.astype(F32), axis=0)
    return jnp.cumsum(p, axis=0) - p[0]


def _rope_tables(positions):
    inv_freq = jnp.power(ROPE_BASE, -jnp.arange(0, MLA_ROPE, 2, dtype=F32) / MLA_ROPE)
    ang = positions.astype(F32)[:, None] * inv_freq
    cos, sin = jnp.cos(ang), jnp.sin(ang)
    zero = jnp.zeros((positions.shape[0], 128 - MLA_ROPE), F32)
    return (jnp.concatenate([cos, cos, zero], axis=-1), jnp.concatenate([-sin, sin, zero], axis=-1))


def _pad_mla_weights(w_in, w_uq):
    w_in_p = jnp.pad(w_in, ((0, 0), (0, 0), (0, 128 - MLA_ROPE)))
    n, ql, _ = w_uq.shape
    w_uq_p = jnp.pad(w_uq.reshape(n, ql, MLA_HEADS, MLA_NOPE + MLA_ROPE),
                     ((0, 0), (0, 0), (0, 0), (0, MLA_QK_PAD - MLA_NOPE - MLA_ROPE)))
    return w_in_p, w_uq_p.reshape(n, ql, MLA_HEADS * MLA_QK_PAD)


def _local_step(x, positions, target, small, fetch, gbufs, emit, emit_mlp):
    T, D = x.shape
    lbounds, lb_vjp = jax.vjp(_lower_bounds, small["hgrn_lb_logits"])
    cc, ss = _rope_tables(positions)
    fetched = {0: fetch(0, None)}
    gains = fetched[0]["gains"]
    tick = [jnp.zeros((), F32)]

    def g(layer, i):
        return gains[layer, i][None, :] + tick[0]

    def mla_weights(layer):
        f = fetched[layer]
        w_in_p, w_uq_p = _pad_mla_weights(f["w_in"][None], f["w_uq"][None])
        slot = layer // 2
        return dict(w_in=w_in_p[0], w_uq=w_uq_p[0], w_ukv=f["w_ukv"],
                    q_norm=small["mla_q_norm"][slot][None, :], kv_norm=small["mla_kv_norm"][slot][None, :])

    saved = []
    h = x
    a = _prenorm_fwd(x, g(0, 0))
    dy = sq = None
    for layer in range(DEPTH):
        slot = layer // 2
        if layer not in fetched:
            fetched[layer] = fetch(layer, a)
        wbuf, pk = fetched[layer]["wbuf"], fetched[layer]["pk"]
        if layer % 2 == 0:
            m, mix_saved = _mla_fwd(a, mla_weights(layer), cc, ss, wbuf, pk, slot)
        else:
            m, mix_saved = _hgrn_layer_fwd(a, small["hgrn_o_norm"][slot][None, :], lbounds[layer][None, :],
                                           wbuf, pk, slot)
        h1, a2 = _resnorm_fwd(h, m, g(layer, 1), g(layer, 2), name="resnorm_fwd_mix")
        u, mlp_saved = _mlp_fwd(a2, wbuf, pk, layer)
        if layer + 1 < DEPTH:
            h2, a = _resnorm_fwd(h1, u, g(layer, 3), g(layer + 1, 0), name="resnorm_fwd_mlp")
        else:
            h2 = None
            dy, sq = _resnorm_loss(h1, u, g(layer, 3), target)
        saved.append((h, m, h1, u, mix_saved, mlp_saved))
        h = h2

    n_mla, n_hgrn = (DEPTH + 1) // 2, DEPTH // 2
    dgains = [[None] * 4 for _ in range(DEPTH)]
    gw = {k: [None] * n_mla for k in ("mla_w_in", "mla_w_uq", "mla_w_ukv", "mla_q_norm", "mla_kv_norm")}
    gw["hgrn_o_norm"] = [None] * n_hgrn
    dlb = [jnp.zeros((1, lbounds.shape[1]), F32) for _ in range(DEPTH)]
    dh = dy
    da_next = None
    for layer in reversed(range(DEPTH)):
        h0, m, h1, u, mix_saved, mlp_saved = saved[layer]
        slot = layer // 2
        wbuf, pk, gbuf = fetched[layer]["wbuf"], fetched[layer]["pk"], gbufs[layer]
        if da_next is None:
            du, dgains[layer][3] = _resnorm_bwd(u, g(layer, 3), dh, name="resnorm_bwd_last")
            t = dh
        else:
            h2 = saved[layer + 1][0]
            t, du, dgains[layer][3], dgains[layer + 1][0] = _resnorm_bwd(
                u, g(layer, 3), dh, h2, da_next, g(layer + 1, 0), name="resnorm_bwd_mlp")
        da2, gbuf = _mlp_bwd(du, mlp_saved, wbuf, gbuf, pk, layer)
        if layer == 0:
            gbuf = emit_mlp(layer, gbuf)
        t, dm, dgains[layer][1], dgains[layer][2] = _resnorm_bwd(
            m, g(layer, 1), t, h1, da2, g(layer, 2), name="resnorm_bwd_mix")
        if layer % 2 == 0:
            da_next, gbuf, mg = _mla_bwd(dm, mix_saved, mla_weights(layer), cc, ss, wbuf, gbuf, pk, slot)
            ql = mg["q_norm"].shape[-1]
            kvl = mg["kv_norm"].shape[-1]
            gw["mla_w_in"][slot] = mg["w_in"][:, :ql + kvl + MLA_ROPE]
            gw["mla_w_uq"][slot] = mg["w_uq"].reshape(ql, MLA_HEADS, MLA_QK_PAD)[
                :, :, :MLA_NOPE + MLA_ROPE].reshape(ql, MLA_HEADS * (MLA_NOPE + MLA_ROPE))
            gw["mla_w_ukv"][slot] = mg["w_ukv"]
            gw["mla_q_norm"][slot] = mg["q_norm"][0]
            gw["mla_kv_norm"][slot] = mg["kv_norm"][0]
        else:
            da_next, gbuf, hg = _hgrn_layer_bwd(dm, mix_saved, small["hgrn_o_norm"][slot][None, :],
                                                lbounds[layer][None, :], wbuf, gbuf, pk, slot)
            gw["hgrn_o_norm"][slot] = hg["o_norm"][0]
            dlb[layer] = hg["lb"]
        dh = t
        if layer > 0:
            mine = ({k: gw[k][slot] for k in ("mla_w_in", "mla_w_uq", "mla_w_ukv")} if layer % 2 == 0 else {})
            tick[0] = emit(layer, gbuf, mine)
        else:
            gbuf0 = gbuf
    grad_x, dgains[0][0] = _prenorm_bwd(x, g(0, 0), dh, da_next)

    last = {k: gw[k][0] for k in ("mla_w_in", "mla_w_uq", "mla_w_ukv")}
    last.update({k: jnp.stack(gw[k]) for k in ("mla_q_norm", "mla_kv_norm", "hgrn_o_norm")})
    last["norm_gains"] = jnp.stack([jnp.concatenate(row, axis=0) for row in dgains])
    (last["hgrn_lb_logits"],) = lb_vjp(jnp.concatenate(dlb, axis=0))
    emit(0, gbuf0, last)
    return sq, grad_x


def _size(shape):
    n = 1
    for d in shape:
        n *= d
    return n


def _piece_rows(shape):
    return -(-_size(shape) // PACK_W)


def _packed_misc_rows(shapes):
    return sum(_piece_rows(s) for s in shapes)


def _cast_into(src, buf, row, name):
    rows, W = src.shape
    tr = min(256, rows)
    assert rows % tr == 0 and row % tr == 0

    def body(s_ref, b_ref, o_ref):
        o_ref[...] = s_ref[...].astype(BF16)

    return pl.pallas_call(
        body, name=name, grid=(rows // tr,),
        in_specs=[pl.BlockSpec((tr, W), lambda i: (i, 0)), pl.BlockSpec(memory_space=pl.ANY)],
        out_specs=pl.BlockSpec((tr, W), lambda i: (row // tr + i, 0)),
        out_shape=jax.ShapeDtypeStruct(buf.shape, buf.dtype), input_output_aliases={1: 0},
        compiler_params=_cparams("parallel"))(src, buf)


def _pack_blocks(pieces, rows, dtype):
    blocks, used = [], 0
    for p in pieces:
        flat = p.astype(dtype).reshape(-1)
        r = _piece_rows(p.shape)
        if r * PACK_W != flat.shape[0]:
            flat = jnp.pad(flat, (0, r * PACK_W - flat.shape[0]))
        blocks.append(flat.reshape(r, PACK_W))
        used += r
    if rows > used:
        blocks.append(jnp.zeros((rows - used, PACK_W), dtype))
    return blocks


def _unpack(buf, shapes):
    out, off = [], 0
    for shp in shapes:
        r = _piece_rows(shp)
        piece = buf[off:off + r]
        if r * PACK_W != _size(shp):
            piece = piece.reshape(-1)[:_size(shp)]
        out.append(piece.reshape(shp))
        off += r
    return out


def _mesh_place():
    x, y, c = lax.axis_index("x"), lax.axis_index("y"), lax.axis_index("c")
    chips = [(1 - x, y), (x, 1 - y), (1 - x, 1 - y)]
    return x, y, c, chips


_HBM = pl.BlockSpec(memory_space=pltpu.HBM)


def _all_gather(wp):
    R, W = wp.shape
    rh = R // 2
    rq = rh // 2
    assert rq % 16 == 0

    def body(w_ref, out_ref, send_sems, recv_sems):
        x, y, c, _ = _mesh_place()
        me, jx, jy, jd = 2 * x + y, 2 * (1 - x) + y, 2 * x + (1 - y), 2 * (1 - x) + (1 - y)
        to_x, to_y, sibling = (1 - x, y, c), (x, 1 - y, c), (x, y, 1 - c)

        def rows(core, quarter):
            return pl.ds(pl.multiple_of(core * rh + quarter * rq, 16), rq)

        def slot(j, core, quarter):
            return out_ref.at[j, rows(core, quarter)]

        def copy(k, src, dst, to):
            return pltpu.make_async_remote_copy(src_ref=src, dst_ref=dst, send_sem=send_sems.at[k],
                                                recv_sem=recv_sems.at[k], device_id=to, device_id_type=MESH)

        sends = [copy(0, w_ref.at[rows(c, 0)], slot(me, c, 0), to_x),
                 copy(2, w_ref.at[rows(c, 1)], slot(me, c, 1), to_y),
                 copy(1, w_ref.at[rows(c, 1)], slot(me, c, 1), to_x),
                 copy(3, w_ref.at[rows(c, 0)], slot(me, c, 0), to_y)]
        for cp in sends:
            cp.start()
        arrivals = [(0, slot(jx, c, 0), 4, to_y, 6), (2, slot(jy, c, 1), 5, to_x, 7),
                    (1, slot(jx, c, 1), None, None, 8), (3, slot(jy, c, 0), None, None, 9),
                    (4, slot(jd, c, 0), None, None, 10), (5, slot(jd, c, 1), None, None, 11)]
        for k, landed, k_on, to_on, k_sib in arrivals:
            copy(k, landed, landed, sibling).wait_recv()
            if k_on is not None:
                cp = copy(k_on, landed, landed, to_on)
                cp.start()
                sends.append(cp)
            cp = copy(k_sib, landed, landed, sibling)
            cp.start()
            sends.append(cp)
        for k_sib, j, quarter in ((6, jx, 0), (7, jy, 1), (8, jx, 1), (9, jy, 0), (10, jd, 0), (11, jd, 1)):
            landed = slot(j, 1 - c, quarter)
            copy(k_sib, landed, landed, sibling).wait_recv()
        for cp in sends:
            cp.wait_send()

    out = pl.pallas_call(
        body, name="weights_all_gather", in_specs=[_HBM], out_specs=_HBM,
        out_shape=jax.ShapeDtypeStruct((N_CHIPS, R, W), wp.dtype),
        scratch_shapes=[pltpu.SemaphoreType.DMA((12,)), pltpu.SemaphoreType.DMA((12,))],
    )(wp)
    me = 2 * lax.axis_index("x") + lax.axis_index("y")
    return lax.dynamic_update_slice(out, wp[None], (me, 0, 0))


def _exchange_halves(g):
    n, _, rh, W = g.shape

    def body(g_ref, out_ref, send_sems, recv_sems):
        x, y, c, _ = _mesh_place()
        sibling = (x, y, 1 - c)
        copies = [pltpu.make_async_remote_copy(
            src_ref=g_ref.at[j, 1 - c], dst_ref=out_ref.at[j], send_sem=send_sems.at[j],
            recv_sem=recv_sems.at[j], device_id=sibling, device_id_type=MESH) for j in range(n)]
        for cp in copies:
            cp.start()
        for cp in copies:
            cp.wait()

    return pl.pallas_call(
        body, name="grads_to_sibling", in_specs=[_HBM], out_specs=_HBM,
        out_shape=jax.ShapeDtypeStruct((n, rh, W), g.dtype),
        scratch_shapes=[pltpu.SemaphoreType.DMA((n,)), pltpu.SemaphoreType.DMA((n,))],
    )(g)


def _scatter_to_owners(p):
    n, rh, W = p.shape
    rq = rh // 2
    assert rq % 16 == 0

    def body(p_ref, out_ref, stage_ref, send_sems, recv_sems):
        x, y, c, _ = _mesh_place()
        me, jx, jy, jd = 2 * x + y, 2 * (1 - x) + y, 2 * x + (1 - y), 2 * (1 - x) + (1 - y)
        to_x, to_y = (1 - x, y, c), (x, 1 - y, c)

        def quarter(ref, j, q):
            return ref.at[j, pl.ds(q * rq, rq)]

        def copy(k, src, dst, to):
            return pltpu.make_async_remote_copy(src_ref=src, dst_ref=dst, send_sem=send_sems.at[k],
                                                recv_sem=recv_sems.at[k], device_id=to, device_id_type=MESH)

        sends = [copy(2, quarter(p_ref, jd, 0), stage_ref.at[0], to_x),
                 copy(3, quarter(p_ref, jd, 1), stage_ref.at[1], to_y),
                 copy(0, p_ref.at[jx], out_ref.at[me], to_x),
                 copy(1, p_ref.at[jy], out_ref.at[me], to_y)]
        for cp in sends:
            cp.start()
        copy(2, stage_ref.at[0], stage_ref.at[0], to_x).wait_recv()
        relay = copy(4, stage_ref.at[0], quarter(out_ref, jx, 0), to_y)
        relay.start()
        sends.append(relay)
        copy(3, stage_ref.at[1], stage_ref.at[1], to_y).wait_recv()
        relay = copy(5, stage_ref.at[1], quarter(out_ref, jy, 1), to_x)
        relay.start()
        sends.append(relay)
        copy(0, out_ref.at[jx], out_ref.at[jx], to_x).wait_recv()
        copy(1, out_ref.at[jy], out_ref.at[jy], to_y).wait_recv()
        copy(4, quarter(out_ref, jd, 0), quarter(out_ref, jd, 0), to_y).wait_recv()
        copy(5, quarter(out_ref, jd, 1), quarter(out_ref, jd, 1), to_x).wait_recv()
        for cp in sends:
            cp.wait_send()

    out, _ = pl.pallas_call(
        body, name="grads_to_owner", in_specs=[_HBM], out_specs=(_HBM, _HBM),
        out_shape=(jax.ShapeDtypeStruct((n, rh, W), p.dtype), jax.ShapeDtypeStruct((2, rq, W), p.dtype)),
        scratch_shapes=[pltpu.SemaphoreType.DMA((6,)), pltpu.SemaphoreType.DMA((6,))],
    )(p)
    me = 2 * lax.axis_index("x") + lax.axis_index("y")
    mine = lax.dynamic_index_in_dim(p, me, axis=0, keepdims=True)
    return lax.dynamic_update_slice(out, mine, (me, 0, 0))


def _share_reduced(q, name="grads_share_reduced"):
    rh, W = q.shape

    def body(q_ref, out_ref, send_sem, recv_sem):
        x, y, c, _ = _mesh_place()
        cp = pltpu.make_async_remote_copy(src_ref=q_ref, dst_ref=out_ref.at[c], send_sem=send_sem,
                                          recv_sem=recv_sem, device_id=(x, y, 1 - c), device_id_type=MESH)
        cp.start()
        cp.wait()

    out = pl.pallas_call(
        body, name=name, in_specs=[_HBM], out_specs=_HBM,
        out_shape=jax.ShapeDtypeStruct((2, rh, W), q.dtype),
        scratch_shapes=[pltpu.SemaphoreType.DMA, pltpu.SemaphoreType.DMA],
    )(q)
    return lax.dynamic_update_slice(out, q[None], (lax.axis_index("c"), 0, 0))


def _add_sibling(g, recv, c_arr):
    n, _, rh, W = g.shape
    tr = PACK_TILE

    def body(c_ref, g_ref, r_ref, o_ref):
        o_ref[...] = (g_ref[...].astype(F32) + r_ref[...].astype(F32)).astype(BF16)

    return pl.pallas_call(
        body, name="grads_add_sibling",
        grid_spec=pltpu.PrefetchScalarGridSpec(
            num_scalar_prefetch=1, grid=(n, rh // tr),
            in_specs=[pl.BlockSpec((None, None, tr, W), lambda j, i, c_ref: (j, c_ref[0], i, 0)),
                      pl.BlockSpec((None, tr, W), lambda j, i, c_ref: (j, i, 0))],
            out_specs=pl.BlockSpec((None, tr, W), lambda j, i, c_ref: (j, i, 0))),
        out_shape=jax.ShapeDtypeStruct((n, rh, W), BF16),
        compiler_params=_cparams("parallel", "parallel"))(c_arr, g, recv)


def _sum_chips(parts, name="grads_sum_chips", out_dtype=F32):
    n, rh, W = parts.shape
    tr = PACK_TILE

    def body(p_ref, o_ref):
        acc = p_ref[0].astype(F32)
        for j in range(1, n):
            acc = acc + p_ref[j].astype(F32)
        o_ref[...] = acc.astype(out_dtype)

    return pl.pallas_call(
        body, name=name, grid=(rh // tr,),
        in_specs=[pl.BlockSpec((n, tr, W), lambda i: (0, i, 0))],
        out_specs=pl.BlockSpec((tr, W), lambda i: (i, 0)),
        out_shape=jax.ShapeDtypeStruct((rh, W), out_dtype),
        compiler_params=_cparams("parallel"))(parts)


_SEM = pl.BlockSpec(memory_space=pltpu.SEMAPHORE)
_ASYNC = pltpu.CompilerParams(has_side_effects=pltpu.SideEffectType.DATAFLOW_SIDE_EFFECTING)


def _hbm(a):
    return pltpu.with_memory_space_constraint(a, pltpu.HBM)


def _gather_copies(w_ref, land_ref, send_sems, recv_sems):
    x, y, c, chips = _mesh_place()
    me = 2 * x + y
    rh = w_ref.shape[0] // 2
    rows = pl.ds(pl.multiple_of(c * rh, 16), rh)
    return [pltpu.make_async_remote_copy(
        src_ref=w_ref.at[rows], dst_ref=land_ref.at[me, rows], send_sem=send_sems.at[r],
        recv_sem=recv_sems.at[r], device_id=(px, py, c), device_id_type=MESH)
        for r, (px, py) in enumerate(chips)]


def _scatter_copies(g_ref, land_ref, send_sems, recv_sems, row0):
    x, y, c, chips = _mesh_place()
    me = 2 * x + y
    rows = pl.ds(row0, land_ref.shape[1])
    return [pltpu.make_async_remote_copy(
        src_ref=g_ref.at[2 * px + py, rows], dst_ref=land_ref.at[me], send_sem=send_sems.at[r],
        recv_sem=recv_sems.at[r], device_id=(px, py, c), device_id_type=MESH)
        for r, (px, py) in enumerate(chips)]


def _halves_to_sibling(land, name):
    n, R, W = land.shape
    rh = R // 2

    def body(l_ref, o_ref, send_sems, recv_sems):
        x, y, c, chips = _mesh_place()
        rows = pl.ds(pl.multiple_of(c * rh, 16), rh)
        copies = [pltpu.make_async_remote_copy(
            src_ref=o_ref.at[2 * px + py, rows], dst_ref=o_ref.at[2 * px + py, rows], send_sem=send_sems.at[r],
            recv_sem=recv_sems.at[r], device_id=(x, y, 1 - c), device_id_type=MESH)
            for r, (px, py) in enumerate(chips)]
        for cp in copies:
            cp.start()
        for cp in copies:
            cp.wait()

    return pl.pallas_call(
        body, name=name, in_specs=[_HBM], out_specs=_HBM, out_shape=jax.ShapeDtypeStruct(land.shape, land.dtype),
        scratch_shapes=[pltpu.SemaphoreType.DMA((3,)), pltpu.SemaphoreType.DMA((3,))],
        input_output_aliases={0: 0})(land)


def _gather_start(wp, name):
    R, W = wp.shape

    def body(w_ref, land_ref, send_sems, recv_sems, w_thru, land_thru, token):
        for cp in _gather_copies(w_ref, land_ref, send_sems, recv_sems):
            cp.start()
        token[...] = jnp.zeros_like(token)

    return pl.pallas_call(
        body, name=name,
        out_shape=(pltpu.SemaphoreType.DMA((3,)), pltpu.SemaphoreType.DMA((3,)), pltpu.HBM(wp.shape, wp.dtype),
                   pltpu.HBM((N_CHIPS, R, W), wp.dtype), jax.ShapeDtypeStruct((8, 128), F32)),
        in_specs=(_HBM, _HBM),
        out_specs=(_SEM, _SEM, _HBM, _HBM, pl.BlockSpec(memory_space=pltpu.VMEM)),
        input_output_aliases={0: 2, 1: 3}, compiler_params=_ASYNC,
    )(_hbm(wp), _hbm(lax.empty((N_CHIPS, R, W), wp.dtype)))


def _gather_wait(send_sems, recv_sems, w_thru, land_thru, after, name):
    R, W = w_thru.shape
    rh = R // 2

    def body(w_ref, land_ref, send_sems, recv_sems, after_ref, w_dead, got_ref):
        x, y, c, _ = _mesh_place()
        half = land_ref.at[0, pl.ds(0, rh)]
        for k in range(3):
            cp = pltpu.make_async_remote_copy(src_ref=half, dst_ref=half, send_sem=send_sems.at[k],
                                              recv_sem=recv_sems.at[k], device_id=(x, y, 1 - c),
                                              device_id_type=MESH)
            cp.wait_send()
            cp.wait_recv()

    return pl.pallas_call(
        body, name=name,
        out_shape=(pltpu.HBM(w_thru.shape, w_thru.dtype), pltpu.HBM(land_thru.shape, land_thru.dtype)),
        in_specs=(_HBM, _HBM, _SEM, _SEM, pl.BlockSpec(memory_space=pl.ANY)), out_specs=(_HBM, _HBM),
        input_output_aliases={0: 0, 1: 1}, compiler_params=_ASYNC,
    )(w_thru, land_thru, send_sems, recv_sems, after)


def _scatter_start(g, row0, nrows, name):
    n, R, W = g.shape
    land_shape = (n, nrows, W)

    def body(g_ref, land_ref, send_sems, recv_sems, g_thru, land_thru, token):
        for cp in _scatter_copies(g_ref, land_ref, send_sems, recv_sems, row0):
            cp.start()
        token[...] = jnp.zeros_like(token)

    return pl.pallas_call(
        body, name=name,
        out_shape=(pltpu.SemaphoreType.DMA((3,)), pltpu.SemaphoreType.DMA((3,)), pltpu.HBM(g.shape, g.dtype),
                   pltpu.HBM(land_shape, g.dtype), jax.ShapeDtypeStruct((8, 128), F32)),
        in_specs=(_HBM, _HBM),
        out_specs=(_SEM, _SEM, _HBM, _HBM, pl.BlockSpec(memory_space=pltpu.VMEM)),
        input_output_aliases={0: 2, 1: 3}, compiler_params=_ASYNC,
    )(_hbm(g), _hbm(lax.empty(land_shape, g.dtype)))


def _scatter_wait(send_sems, recv_sems, g_thru, land_thru, after, name):
    def body(g_ref, land_ref, send_sems, recv_sems, after_ref, g_out, got_ref):
        x, y, c, _ = _mesh_place()
        for k in range(3):
            cp = pltpu.make_async_remote_copy(src_ref=land_ref.at[0], dst_ref=land_ref.at[0], send_sem=send_sems.at[k],
                                              recv_sem=recv_sems.at[k], device_id=(x, y, 1 - c),
                                              device_id_type=MESH)
            cp.wait_send()
            cp.wait_recv()

    return pl.pallas_call(
        body, name=name,
        out_shape=(pltpu.HBM(g_thru.shape, g_thru.dtype), pltpu.HBM(land_thru.shape, land_thru.dtype)),
        in_specs=(_HBM, _HBM, _SEM, _SEM, pl.BlockSpec(memory_space=pl.ANY)), out_specs=(_HBM, _HBM),
        input_output_aliases={0: 0, 1: 1}, compiler_params=_ASYNC,
    )(g_thru, land_thru, send_sems, recv_sems, after)


def _adamw(w, g, m, v, name):
    shape = w.shape
    cols = shape[-1]
    w2, g2, m2, v2 = (t.reshape(-1, cols) for t in (w, g, m, v))
    rows = w2.shape[0]
    tr = rows
    for cand in (512, 256, 128, 64, 32, 16, 8):
        if rows > cand and rows % cand == 0:
            tr = cand
            break
    c1 = 1.0 / (1.0 - ADAM_B1 ** ADAM_STEP)
    c2 = 1.0 / (1.0 - ADAM_B2 ** ADAM_STEP)

    def body(w_ref, g_ref, m_ref, v_ref, d_ref, nm_ref, nv_ref):
        gv = g_ref[...]
        nm = ADAM_B1 * m_ref[...] + (1.0 - ADAM_B1) * gv
        nv = ADAM_B2 * v_ref[...] + (1.0 - ADAM_B2) * (gv * gv)
        nm_ref[...] = nm
        nv_ref[...] = nv
        d_ref[...] = -ADAM_LR * ((nm * c1) / (jnp.sqrt(nv * c2) + ADAM_EPS) + ADAM_WD * w_ref[...])

    blk = pl.BlockSpec((tr, cols), lambda i: (i, 0))
    sds = jax.ShapeDtypeStruct((rows, cols), F32)
    d, nm, nv = pl.pallas_call(body, name=name, grid=(rows // tr,), in_specs=[blk] * 4,
                               out_specs=(blk, blk, blk), out_shape=(sds, sds, sds),
                               compiler_params=_cparams("parallel"))(w2, g2, m2, v2)
    return d.reshape(shape), nm.reshape(shape), nv.reshape(shape)


def kernel(x, positions, norm_gains, mla_w_in, mla_q_norm, mla_kv_norm, mla_w_uq, mla_w_ukv, mla_w_o, hgrn_w_in, hgrn_lb_logits, hgrn_o_norm, hgrn_w_o, mlp_w1, mlp_w2, loss_target, m_norm_gains, m_mla_w_in, m_mla_q_norm, m_mla_kv_norm, m_mla_w_uq, m_mla_w_ukv, m_mla_w_o, m_hgrn_w_in, m_hgrn_lb_logits, m_hgrn_o_norm, m_hgrn_w_o, m_mlp_w1, m_mlp_w2, v_norm_gains, v_mla_w_in, v_mla_q_norm, v_mla_kv_norm, v_mla_w_uq, v_mla_w_ukv, v_mla_w_o, v_hgrn_w_in, v_hgrn_lb_logits, v_hgrn_o_norm, v_hgrn_w_o, v_mlp_w1, v_mlp_w2):
    w = dict(norm_gains=norm_gains, mla_w_in=mla_w_in, mla_q_norm=mla_q_norm, mla_kv_norm=mla_kv_norm,
             mla_w_uq=mla_w_uq, mla_w_ukv=mla_w_ukv, mla_w_o=mla_w_o, hgrn_w_in=hgrn_w_in,
             hgrn_lb_logits=hgrn_lb_logits, hgrn_o_norm=hgrn_o_norm, hgrn_w_o=hgrn_w_o,
             mlp_w1=mlp_w1, mlp_w2=mlp_w2)
    mom_m = dict(norm_gains=m_norm_gains, mla_w_in=m_mla_w_in, mla_q_norm=m_mla_q_norm,
                 mla_kv_norm=m_mla_kv_norm, mla_w_uq=m_mla_w_uq, mla_w_ukv=m_mla_w_ukv,
                 mla_w_o=m_mla_w_o, hgrn_w_in=m_hgrn_w_in, hgrn_lb_logits=m_hgrn_lb_logits,
                 hgrn_o_norm=m_hgrn_o_norm, hgrn_w_o=m_hgrn_w_o, mlp_w1=m_mlp_w1, mlp_w2=m_mlp_w2)
    mom_v = dict(norm_gains=v_norm_gains, mla_w_in=v_mla_w_in, mla_q_norm=v_mla_q_norm,
                 mla_kv_norm=v_mla_kv_norm, mla_w_uq=v_mla_w_uq, mla_w_ukv=v_mla_w_ukv,
                 mla_w_o=v_mla_w_o, hgrn_w_in=v_hgrn_w_in, hgrn_lb_logits=v_hgrn_lb_logits,
                 hgrn_o_norm=v_hgrn_o_norm, hgrn_w_o=v_hgrn_w_o, mlp_w1=v_mlp_w1, mlp_w2=v_mlp_w2)
    c = lax.axis_index("c")

    axis_of = dict(SHARDED)
    me = 2 * lax.axis_index("x") + lax.axis_index("y")
    gain_bits = lax.bitcast_convert_type(norm_gains, jnp.uint32)
    gain_hi = lax.bitcast_convert_type((gain_bits >> 16).astype(jnp.uint16), BF16)
    gain_lo = lax.bitcast_convert_type((gain_bits & 0xFFFF).astype(jnp.uint16), BF16)

    layers = []
    for l in range(DEPTH):
        s = l // 2
        if l % 2 == 0:
            big = [("mlp_w1", l), ("mlp_w2", l), ("mla_w_o", s)]
            tail = [("mla_w_in", s), ("mla_w_uq", s), ("mla_w_ukv", s)]
        else:
            big = [("hgrn_w_in", s), ("mlp_w1", l), ("mlp_w2", l), ("hgrn_w_o", s)]
            tail = []
        w_tail = [w[n][i] for n, i in tail] + ([gain_hi, gain_lo] if l == 0 else [])
        g_tail = tail + ([("norm_gains", None)] + [(n, None) for n in REPLICATED] if l == 0 else [])
        g_shapes = [w[n].shape if i is None else w[n][i].shape for n, i in g_tail]
        tail_rows = max(_packed_misc_rows([t.shape for t in w_tail]), _packed_misc_rows(g_shapes))
        pk = _Packed([(n, w[n].shape[1]) for n, _ in big], tail_rows)
        wpack = jnp.zeros((pk.rows, PACK_W), BF16)
        for n, i in big:
            assert w[n].shape[2] == PACK_W
            wpack = _cast_into(w[n][i], wpack, pk.off[n], name="pack_%s_%d" % (n, l))
        if w_tail:
            wpack = lax.dynamic_update_slice(
                wpack, jnp.concatenate(_pack_blocks(w_tail, 0, BF16), axis=0), (pk.misc, 0))
        layers.append(dict(pk=pk, big=big, tail=tail, w_tail=w_tail, g_tail=g_tail, g_shapes=g_shapes,
                           gather=_gather_start(wpack, name="gather_start_%d" % l)))

    def fetch(l, after):
        lay = layers[l]
        pk = lay["pk"]
        send_sems, recv_sems, w_thru, land_thru, _ = lay["gather"]
        if after is None:
            after = sum(layers[k]["gather"][4] for k in range(1, DEPTH))
        w_back, land = _gather_wait(send_sems, recv_sems, w_thru, land_thru, after, name="gather_wait_%d" % l)
        land = _halves_to_sibling(land, name="gather_halves_%d" % l)
        land = lax.dynamic_update_slice(land, w_back[None], (me, 0, 0))
        out = dict(wbuf=land.reshape(N_CHIPS * pk.rows, PACK_W), pk=pk)
        if lay["w_tail"]:
            rows = _packed_misc_rows([t.shape for t in lay["w_tail"]])
            per_chip = [_unpack(land[j, pk.misc:pk.misc + rows], [t.shape for t in lay["w_tail"]])
                        for j in range(N_CHIPS)]
            for i, (n, _) in enumerate(lay["tail"]):
                out[n[4:]] = jnp.concatenate([per_chip[j][i] for j in range(N_CHIPS)], axis=axis_of[n] - 1)
            if l == 0:
                got_hi, got_lo = (lax.bitcast_convert_type(
                    jnp.concatenate([per_chip[j][i] for j in range(N_CHIPS)], axis=2),
                    jnp.uint16).astype(jnp.uint32) for i in (-2, -1))
                out["gains"] = lax.bitcast_convert_type((got_hi << 16) | got_lo, F32)
        return out

    def emit(l, gbuf, grads):
        lay = layers[l]
        pk = lay["pk"]
        if lay["g_tail"]:
            for j in range(N_CHIPS):
                pieces = []
                for n, i in lay["g_tail"]:
                    if n not in axis_of:
                        pieces.append(grads[n])
                    else:
                        pieces.append(jnp.split(grads[n], N_CHIPS, axis=axis_of[n] - (0 if i is None else 1))[j])
                block = jnp.concatenate(_pack_blocks(pieces, 0, BF16), axis=0)
                gbuf = lax.dynamic_update_slice(gbuf, block, (j * pk.rows + pk.misc, 0))
        row0 = lay.get("early_rows", 0)
        lay["scatter"] = _scatter_start(gbuf.reshape(N_CHIPS, pk.rows, PACK_W), row0, pk.rows - row0,
                                        name="scatter_start_%d" % l)
        return lay["scatter"][4][0, 0]

    def emit_mlp(l, gbuf):
        lay = layers[l]
        pk = lay["pk"]
        assert pk.off["mlp_w1"] == 0 and pk.off["mlp_w2"] == w["mlp_w1"].shape[1]
        lay["early_rows"] = w["mlp_w1"].shape[1] + w["mlp_w2"].shape[1]
        lay["scatter_early"] = _scatter_start(gbuf.reshape(N_CHIPS, pk.rows, PACK_W), 0, lay["early_rows"],
                                              name="scatter_start_%d_mlp" % l)
        return lay["scatter_early"][2].reshape(N_CHIPS * pk.rows, PACK_W)

    small = dict(mla_q_norm=mla_q_norm, mla_kv_norm=mla_kv_norm, hgrn_lb_logits=hgrn_lb_logits,
                 hgrn_o_norm=hgrn_o_norm)
    gbufs = [jnp.zeros((N_CHIPS * lay["pk"].rows, PACK_W), BF16) for lay in layers]
    sq, grad_x = _local_step(x[0], positions[0], loss_target[0], small, fetch, gbufs, emit, emit_mlp)
    d_model = x.shape[-1]
    loss = lax.psum(0.5 * jnp.sum(sq) / d_model, ("x", "y", "c"))

    per_name = {}
    behind = grad_x
    for l, lay in reversed(list(enumerate(layers))):
        pk = lay["pk"]
        send_sems, recv_sems, g_thru, land_thru, _ = lay["scatter"]
        row0 = lay.get("early_rows", 0)
        early = None
        if row0:
            e_send, e_recv, _, e_land, _ = lay["scatter_early"]
            g_thru, land = _scatter_wait(e_send, e_recv, g_thru, e_land, behind, name="scatter_wait_%d_mlp" % l)
            own = lax.dynamic_slice_in_dim(g_thru, me, 1, axis=0)
            land = lax.dynamic_update_slice(land, own[:, :row0], (me, 0, 0))
            early = behind = _sum_chips(land, name="grads_sum_chips_%d_mlp" % l, out_dtype=BF16)
        g_back, land = _scatter_wait(send_sems, recv_sems, g_thru, land_thru, behind, name="scatter_wait_%d" % l)
        own = lax.dynamic_slice_in_dim(g_back, me, 1, axis=0)
        land = lax.dynamic_update_slice(land, own[:, row0:], (me, 0, 0))
        mine = _sum_chips(land, name="grads_sum_chips_%d" % l, out_dtype=BF16)
        if early is not None:
            mine = jnp.concatenate([early, mine], axis=0)
        red = behind = _sum_chips(_share_reduced(mine, name="grads_share_%d" % l), name="grads_sum_cores_%d" % l)
        for n, i in lay["big"]:
            per_name.setdefault(n, {})[i] = red[pk.off[n]:pk.off[n] + w[n].shape[1]]
        for (n, i), piece in zip(lay["g_tail"], _unpack(red[pk.misc:pk.misc + pk.misc_rows], lay["g_shapes"])):
            per_name.setdefault(n, {})[i] = piece
    g_out = {n: (parts[None] if None in parts else jnp.stack([parts[i] for i in sorted(parts)]))
             for n, parts in per_name.items()}

    deltas, new_m, new_v = {}, {}, {}
    for name in WEIGHTS:
        deltas[name], new_m[name], new_v[name] = _adamw(w[name], g_out[name], mom_m[name], mom_v[name],
                                                        name="adamw_" + name)
    return (loss, grad_x[None], *[g_out[n] for n in WEIGHTS], *[deltas[n] for n in WEIGHTS],
            *[new_m[n] for n in WEIGHTS], *[new_v[n] for n in WEIGHTS])
```

```python
import functools

import jax
import jax.numpy as jnp
from jax import lax
from jax.experimental import pallas as pl
from jax.experimental.pallas import tpu as pltpu

F32 = jnp.float32
BF16 = jnp.bfloat16
MESH = pl.DeviceIdType.MESH

DEPTH = 4
MLA_HEADS = 8
MLA_NOPE = 128
MLA_ROPE = 64
MLA_V = 128
MLA_QK_PAD = 256
MLA_HEADS_PER_STEP = 2
MLA_SCALE = float(MLA_NOPE + MLA_ROPE) ** -0.5
ROPE_BASE = 10000.0
HGRN_HEADS = 8
HGRN_CHUNK = 32
HGRN_BLOCK = 128
EPS = 1e-6

ADAM_LR = 0.001
ADAM_B1 = 0.9
ADAM_B2 = 0.999
ADAM_EPS = 1e-08
ADAM_WD = 0.01
ADAM_STEP = 10

N_CHIPS = 4
PACK_W = 1024
PACK_ALIGN = 1024
PACK_TILE = 512
V7X_VMEM_LIMIT = 56 * 1024 * 1024

SHARDED = (("norm_gains", 2), ("mla_w_in", 1), ("mla_w_uq", 2), ("mla_w_ukv", 2), ("mla_w_o", 1),
           ("hgrn_w_in", 2), ("hgrn_w_o", 1), ("mlp_w1", 2), ("mlp_w2", 1))
REPLICATED = ("mla_q_norm", "mla_kv_norm", "hgrn_lb_logits", "hgrn_o_norm")
WEIGHTS = ("norm_gains", "mla_w_in", "mla_q_norm", "mla_kv_norm", "mla_w_uq", "mla_w_ukv", "mla_w_o",
           "hgrn_w_in", "hgrn_lb_logits", "hgrn_o_norm", "hgrn_w_o", "mlp_w1", "mlp_w2")


def _cparams(*semantics):
    return pltpu.CompilerParams(dimension_semantics=semantics, vmem_limit_bytes=V7X_VMEM_LIMIT)


def _sigmoid(x):
    return 1.0 / (1.0 + jnp.exp(-x))


def _mm(a, b, *, ta=False, tb=False, out_dtype=F32, tm=2048, tn=1024, tk=1024, epi=None, extra=None,
        name="mm", n=None, b_map=None, into=None, o_map=None):
    if ta:
        K, M = a.shape
    else:
        M, K = a.shape
    if b_map is not None:
        N = n
    elif tb:
        N, Kb = b.shape
    else:
        Kb, N = b.shape
    assert b_map is not None or K == Kb, (a.shape, b.shape, ta, tb)
    tm, tn = min(tm, M), min(tn, N)
    tk = K if (K <= 1024 and b_map is None) else min(tk, K)
    assert M % tm == 0 and N % tn == 0 and K % tk == 0, (M, N, K, tm, tn, tk)
    nk = K // tk
    a_spec = (pl.BlockSpec((tk, tm), lambda i, j, k: (k, i)) if ta
              else pl.BlockSpec((tm, tk), lambda i, j, k: (i, k)))
    if b_map is None:
        b_map = (lambda i, j, k: (j, k)) if tb else (lambda i, j, k: (k, j))
    b_spec = pl.BlockSpec((tn, tk) if tb else (tk, tn), b_map)
    o_spec = pl.BlockSpec((tm, tn), lambda i, j, k: (i, j))
    dims = (((0 if ta else 1,), (1 if tb else 0,)), ((), ()))
    in_specs = [a_spec, b_spec]
    operands = [a, b]
    aliases = {}
    if epi == "mul2r":
        in_specs.append(o_spec)
        operands.append(extra)
    if into is not None:
        assert epi is None
        in_specs.append(pl.BlockSpec(memory_space=pl.ANY))
        operands.append(into)
        aliases = {2: 0}
        out_dtype = into.dtype
        out_shape = jax.ShapeDtypeStruct(into.shape, into.dtype)
        out_specs = pl.BlockSpec((tm, tn), o_map)
    elif epi == "relu2":
        out_shape = (jax.ShapeDtypeStruct((M, N), BF16), jax.ShapeDtypeStruct((M, N), BF16))
        out_specs = (o_spec, o_spec)
    elif epi == "mul2r":
        out_shape = jax.ShapeDtypeStruct((M, N), BF16)
        out_specs = o_spec
    else:
        out_shape = jax.ShapeDtypeStruct((M, N), out_dtype)
        out_specs = o_spec
    n_in = len(operands)

    def body(*refs):
        a_ref, b_ref = refs[0], refs[1]
        outs = refs[n_in:n_in + (2 if epi == "relu2" else 1)]
        k = pl.program_id(2)

        def finish(acc):
            if epi == "relu2":
                r = jnp.maximum(acc, 0.0)
                outs[0][...] = (r * r).astype(BF16)
                outs[1][...] = r.astype(BF16)
            elif epi == "mul2r":
                outs[0][...] = (acc * (2.0 * refs[2][...].astype(F32))).astype(BF16)
            else:
                outs[0][...] = acc.astype(out_dtype)

        part = lax.dot_general(a_ref[...], b_ref[...], dims, preferred_element_type=F32)
        if nk == 1:
            finish(part)
            return
        acc_ref = refs[-1]

        @pl.when(k == 0)
        def _():
            acc_ref[...] = part

        @pl.when((k > 0) & (k < nk - 1))
        def _():
            acc_ref[...] += part

        @pl.when(k == nk - 1)
        def _():
            finish(acc_ref[...] + part)

    return pl.pallas_call(
        body, name=name, grid=(M // tm, N // tn, nk), in_specs=in_specs, out_specs=out_specs,
        out_shape=out_shape, scratch_shapes=[pltpu.VMEM((tm, tn), F32)] if nk > 1 else [],
        input_output_aliases=aliases,
        compiler_params=_cparams("parallel", "parallel", "arbitrary"))(*operands)


def _rms_rstd(x):
    return lax.rsqrt(jnp.mean(x * x, axis=-1, keepdims=True) + EPS)


def _rms_bwd_tile(x, g, dy):
    r = _rms_rstd(x)
    xh = x * r
    u = dy * g
    dx = r * (u - xh * jnp.mean(u * xh, axis=-1, keepdims=True))
    dg = jnp.sum(dy * xh, axis=0, keepdims=True)
    return dx, dg


def _row_tile(T):
    return min(256, T)


def _prenorm_fwd(x, g, name="prenorm_fwd"):
    T, D = x.shape
    tm = _row_tile(T)

    def body(x_ref, g_ref, a_ref):
        xv = x_ref[...]
        a_ref[...] = (xv * _rms_rstd(xv) * g_ref[...]).astype(BF16)

    row = pl.BlockSpec((tm, D), lambda i: (i, 0))
    vec = pl.BlockSpec((1, D), lambda i: (0, 0))
    return pl.pallas_call(body, name=name, grid=(T // tm,), in_specs=[row, vec], out_specs=row,
                          out_shape=jax.ShapeDtypeStruct((T, D), BF16),
                          compiler_params=_cparams("parallel"))(x, g)


def _resnorm_fwd(h, z, g_post, g_pre, name="resnorm_fwd"):
    T, D = h.shape
    tm = _row_tile(T)

    def body(h_ref, z_ref, gp_ref, gn_ref, hn_ref, a_ref):
        zv = z_ref[...]
        hn = h_ref[...] + zv * _rms_rstd(zv) * gp_ref[...]
        hn_ref[...] = hn
        a_ref[...] = (hn * _rms_rstd(hn) * gn_ref[...]).astype(BF16)

    row = pl.BlockSpec((tm, D), lambda i: (i, 0))
    vec = pl.BlockSpec((1, D), lambda i: (0, 0))
    return pl.pallas_call(body, name=name, grid=(T // tm,), in_specs=[row, row, vec, vec],
                          out_specs=(row, row),
                          out_shape=(jax.ShapeDtypeStruct((T, D), F32), jax.ShapeDtypeStruct((T, D), BF16)),
                          compiler_params=_cparams("parallel"))(h, z, g_post, g_pre)


def _resnorm_loss(h, z, g_post, target, name="resnorm_loss"):
    T, D = h.shape
    tm = _row_tile(T)

    def body(h_ref, z_ref, gp_ref, t_ref, dy_ref, sq_ref):
        zv = z_ref[...]
        err = h_ref[...] + zv * _rms_rstd(zv) * gp_ref[...] - t_ref[...]
        dy_ref[...] = err * (1.0 / D)

        @pl.when(pl.program_id(0) == 0)
        def _():
            sq_ref[...] = jnp.zeros_like(sq_ref)

        sq_ref[...] += jnp.sum(err * err, axis=0, keepdims=True)

    row = pl.BlockSpec((tm, D), lambda i: (i, 0))
    vec = pl.BlockSpec((1, D), lambda i: (0, 0))
    return pl.pallas_call(body, name=name, grid=(T // tm,), in_specs=[row, row, vec, row],
                          out_specs=(row, vec),
                          out_shape=(jax.ShapeDtypeStruct((T, D), F32), jax.ShapeDtypeStruct((1, D), F32)),
                          compiler_params=_cparams("arbitrary"))(h, z, g_post, target)


def _resnorm_bwd(z, g_post, dh, h_new=None, da=None, g_pre=None, name="resnorm_bwd"):
    T, D = z.shape
    tm = _row_tile(T)
    has_next = h_new is not None
    row = pl.BlockSpec((tm, D), lambda i: (i, 0))
    vec = pl.BlockSpec((1, D), lambda i: (0, 0))

    if has_next:
        def body(z_ref, gp_ref, dh_ref, hn_ref, da_ref, gn_ref, t_ref, dz_ref, dgp_ref, dgn_ref):
            first = pl.program_id(0) == 0

            @pl.when(first)
            def _():
                dgp_ref[...] = jnp.zeros_like(dgp_ref)
                dgn_ref[...] = jnp.zeros_like(dgn_ref)

            dpre, dgn = _rms_bwd_tile(hn_ref[...], gn_ref[...], da_ref[...])
            t = dh_ref[...] + dpre
            t_ref[...] = t
            dz, dgp = _rms_bwd_tile(z_ref[...], gp_ref[...], t)
            dz_ref[...] = dz.astype(BF16)
            dgp_ref[...] += dgp
            dgn_ref[...] += dgn

        return pl.pallas_call(
            body, name=name, grid=(T // tm,), in_specs=[row, vec, row, row, row, vec],
            out_specs=(row, row, vec, vec),
            out_shape=(jax.ShapeDtypeStruct((T, D), F32), jax.ShapeDtypeStruct((T, D), BF16),
                       jax.ShapeDtypeStruct((1, D), F32), jax.ShapeDtypeStruct((1, D), F32)),
            compiler_params=_cparams("arbitrary"))(z, g_post, dh, h_new, da, g_pre)

    def body_last(z_ref, gp_ref, dh_ref, dz_ref, dgp_ref):
        @pl.when(pl.program_id(0) == 0)
        def _():
            dgp_ref[...] = jnp.zeros_like(dgp_ref)

        dz, dgp = _rms_bwd_tile(z_ref[...], gp_ref[...], dh_ref[...])
        dz_ref[...] = dz.astype(BF16)
        dgp_ref[...] += dgp

    return pl.pallas_call(
        body_last, name=name, grid=(T // tm,), in_specs=[row, vec, row], out_specs=(row, vec),
        out_shape=(jax.ShapeDtypeStruct((T, D), BF16), jax.ShapeDtypeStruct((1, D), F32)),
        compiler_params=_cparams("arbitrary"))(z, g_post, dh)


def _prenorm_bwd(x, g, dh, da, name="prenorm_bwd"):
    T, D = x.shape
    tm = _row_tile(T)

    def body(x_ref, g_ref, dh_ref, da_ref, dx_ref, dg_ref):
        @pl.when(pl.program_id(0) == 0)
        def _():
            dg_ref[...] = jnp.zeros_like(dg_ref)

        dpre, dg = _rms_bwd_tile(x_ref[...], g_ref[...], da_ref[...])
        dx_ref[...] = dh_ref[...] + dpre
        dg_ref[...] += dg

    row = pl.BlockSpec((tm, D), lambda i: (i, 0))
    vec = pl.BlockSpec((1, D), lambda i: (0, 0))
    return pl.pallas_call(
        body, name=name, grid=(T // tm,), in_specs=[row, vec, row, row], out_specs=(row, vec),
        out_shape=(jax.ShapeDtypeStruct((T, D), F32), jax.ShapeDtypeStruct((1, D), F32)),
        compiler_params=_cparams("arbitrary"))(x, g, dh, da)


class _Packed:
    def __init__(self, big, misc_rows):
        self.big = tuple(big)
        self.off = {}
        r = 0
        for name, rows in big:
            self.off[name] = r
            r += rows
        self.misc, self.misc_rows = r, misc_rows
        self.rows = -(-(r + misc_rows) // PACK_ALIGN) * PACK_ALIGN

    def block(self, name, layer, unit):
        r = self.off[name]
        assert r % unit == 0 and self.rows % unit == 0
        return r // unit, self.rows // unit


def _col_sharded(pk, name, layer, unit):
    base, stride = pk.block(name, layer, unit)
    return (lambda i, j, k: (j * stride + base, 0)), (lambda i, j, k: (k * stride + base, 0))


def _row_sharded(pk, name, layer, unit):
    base, stride = pk.block(name, layer, unit)
    return ((lambda i, j, k: (k * stride + base, 0)), (lambda i, j, k: (j * stride + base, 0)),
            (lambda i, j, k: (i * stride + base, 0)))


def _mlp_fwd(a, wbuf, pk, layer):
    D = a.shape[1]
    by_n, _ = _col_sharded(pk, "mlp_w1", layer, D)
    by_k, _, _ = _row_sharded(pk, "mlp_w2", layer, D)
    act, r = _mm(a, wbuf, n=4 * D, b_map=by_n, tk=D, tn=D, epi="relu2", name="mlp_up")
    u = _mm(act, wbuf, n=D, b_map=by_k, tk=D, tn=D, name="mlp_down")
    return u, (a, act, r)


def _mlp_bwd(du, saved, wbuf, gbuf, pk, layer):
    a, act, r = saved
    D = a.shape[1]
    w1_by_n, w1_by_k = _col_sharded(pk, "mlp_w1", layer, D)
    _, w2_by_n, w2_by_m = _row_sharded(pk, "mlp_w2", layer, D)
    dz1 = _mm(du, wbuf, tb=True, n=4 * D, b_map=w2_by_n, tn=D, tk=D, epi="mul2r", extra=r, name="mlp_down_dx")
    gbuf = _mm(act, du, ta=True, into=gbuf, o_map=w2_by_m, tm=D, tn=D, name="mlp_down_dw")
    gbuf = _mm(a, dz1, ta=True, into=gbuf, o_map=w1_by_n, tm=D, tn=D, name="mlp_up_dw")
    da = _mm(dz1, wbuf, tb=True, n=D, b_map=w1_by_k, tn=D, tk=D, name="mlp_up_dx")
    return da, gbuf


def _rope_swap(t):
    n = t.shape[-1]
    lane = lax.broadcasted_iota(jnp.int32, t.shape, t.ndim - 1)
    half = MLA_ROPE // 2
    first = (lane & (MLA_ROPE - 1)) < half
    return jnp.where(first, pltpu.roll(t, n - half, t.ndim - 1), pltpu.roll(t, half, t.ndim - 1))


def _mla_mid_fwd(proj, q_norm, kv_norm, w_uq, w_ukv, cc, ss):
    T, PW = proj.shape
    QL, KVL = q_norm.shape[-1], kv_norm.shape[-1]
    H = MLA_HEADS
    assert PW == QL + KVL + 128
    tm = _row_tile(T)

    def body(p_ref, qn_ref, kn_ref, wq_ref, wkv_ref, cc_ref, ss_ref,
             cq_ref, ckv_ref, q_ref, k_ref, v_ref):
        cq = p_ref[:, 0:QL]
        ckv = p_ref[:, QL:QL + KVL]
        kr = p_ref[:, QL + KVL:QL + KVL + 128]
        c, s = cc_ref[...], ss_ref[...]
        cqn = (cq * _rms_rstd(cq) * qn_ref[...]).astype(BF16)
        ckvn = (ckv * _rms_rstd(ckv) * kn_ref[...]).astype(BF16)
        cq_ref[...] = cqn
        ckv_ref[...] = ckvn
        q = jnp.dot(cqn, wq_ref[...], preferred_element_type=F32)
        kv = jnp.dot(ckvn, wkv_ref[...], preferred_element_type=F32)
        krf = (kr * c + _rope_swap(kr) * s).astype(BF16)
        for h in range(H):
            o = h * MLA_QK_PAD
            q_ref[:, o:o + MLA_NOPE] = (q[:, o:o + MLA_NOPE] * MLA_SCALE).astype(BF16)
            qr = q[:, o + MLA_NOPE:o + MLA_QK_PAD]
            q_ref[:, o + MLA_NOPE:o + MLA_QK_PAD] = ((qr * c + _rope_swap(qr) * s) * MLA_SCALE).astype(BF16)
            k_ref[:, o:o + MLA_NOPE] = kv[:, o:o + MLA_NOPE].astype(BF16)
            k_ref[:, o + MLA_NOPE:o + MLA_QK_PAD] = krf
            v_ref[:, h * MLA_V:(h + 1) * MLA_V] = kv[:, o + MLA_NOPE:o + MLA_QK_PAD].astype(BF16)

    def row(w):
        return pl.BlockSpec((tm, w), lambda i: (i, 0))

    def full(shape):
        return pl.BlockSpec(shape, lambda i: (0, 0))

    return pl.pallas_call(
        body, name="mla_mid_fwd", grid=(T // tm,),
        in_specs=[row(PW), full((1, QL)), full((1, KVL)), full(w_uq.shape), full(w_ukv.shape),
                  row(128), row(128)],
        out_specs=(row(QL), row(KVL), row(H * MLA_QK_PAD), row(H * MLA_QK_PAD), row(H * MLA_V)),
        out_shape=(jax.ShapeDtypeStruct((T, QL), BF16), jax.ShapeDtypeStruct((T, KVL), BF16),
                   jax.ShapeDtypeStruct((T, H * MLA_QK_PAD), BF16),
                   jax.ShapeDtypeStruct((T, H * MLA_QK_PAD), BF16),
                   jax.ShapeDtypeStruct((T, H * MLA_V), BF16)),
        compiler_params=_cparams("parallel"))(proj, q_norm, kv_norm, w_uq, w_ukv, cc, ss)


def _mla_mid_bwd(proj, q_norm, kv_norm, w_uq, w_ukv, cc, ss, dq, dk, dv):
    T, PW = proj.shape
    QL, KVL = q_norm.shape[-1], kv_norm.shape[-1]
    H = MLA_HEADS
    tm = _row_tile(T)
    nt = (((1,), (1,)), ((), ()))

    def body(p_ref, qn_ref, kn_ref, wq_ref, wkv_ref, cc_ref, ss_ref, dq_ref, dk_ref, dv_ref,
             dqp_ref, dkv_ref, dp_ref, dqn_ref, dkn_ref):
        @pl.when(pl.program_id(0) == 0)
        def _():
            dqn_ref[...] = jnp.zeros_like(dqn_ref)
            dkn_ref[...] = jnp.zeros_like(dkn_ref)

        c, s = cc_ref[...], ss_ref[...]
        dkr = jnp.zeros((tm, 128), F32)
        for h in range(H):
            o = h * MLA_QK_PAD
            dqp_ref[:, o:o + MLA_NOPE] = (dq_ref[:, o:o + MLA_NOPE] * MLA_SCALE).astype(BF16)
            dqr = dq_ref[:, o + MLA_NOPE:o + MLA_QK_PAD] * MLA_SCALE
            dqp_ref[:, o + MLA_NOPE:o + MLA_QK_PAD] = (dqr * c + _rope_swap(dqr * s)).astype(BF16)
            dkv_ref[:, o:o + MLA_NOPE] = dk_ref[:, o:o + MLA_NOPE].astype(BF16)
            dkv_ref[:, o + MLA_NOPE:o + MLA_QK_PAD] = dv_ref[:, h * MLA_V:(h + 1) * MLA_V].astype(BF16)
            dkr = dkr + dk_ref[:, o + MLA_NOPE:o + MLA_QK_PAD]
        dcqn = lax.dot_general(dqp_ref[...], wq_ref[...], nt, preferred_element_type=F32)
        dckvn = lax.dot_general(dkv_ref[...], wkv_ref[...], nt, preferred_element_type=F32)
        dcq, dqn = _rms_bwd_tile(p_ref[:, 0:QL], qn_ref[...], dcqn)
        dckv, dkn = _rms_bwd_tile(p_ref[:, QL:QL + KVL], kn_ref[...], dckvn)
        dp_ref[:, 0:QL] = dcq.astype(BF16)
        dp_ref[:, QL:QL + KVL] = dckv.astype(BF16)
        dp_ref[:, QL + KVL:QL + KVL + 128] = (dkr * c + _rope_swap(dkr * s)).astype(BF16)
        dqn_ref[...] += dqn
        dkn_ref[...] += dkn

    def row(w):
        return pl.BlockSpec((tm, w), lambda i: (i, 0))

    def full(shape):
        return pl.BlockSpec(shape, lambda i: (0, 0))

    return pl.pallas_call(
        body, name="mla_mid_bwd", grid=(T // tm,),
        in_specs=[row(PW), full((1, QL)), full((1, KVL)), full(w_uq.shape), full(w_ukv.shape),
                  row(128), row(128), row(H * MLA_QK_PAD), row(H * MLA_QK_PAD), row(H * MLA_V)],
        out_specs=(row(H * MLA_QK_PAD), row(H * MLA_QK_PAD), row(PW), full((1, QL)), full((1, KVL))),
        out_shape=(jax.ShapeDtypeStruct((T, H * MLA_QK_PAD), BF16),
                   jax.ShapeDtypeStruct((T, H * MLA_QK_PAD), BF16),
                   jax.ShapeDtypeStruct((T, PW), BF16),
                   jax.ShapeDtypeStruct((1, QL), F32), jax.ShapeDtypeStruct((1, KVL), F32)),
        compiler_params=_cparams("arbitrary"))(proj, q_norm, kv_norm, w_uq, w_ukv, cc, ss, dq, dk, dv)


def _attn_tile(T):
    return min(1024, T)


def _attn_pairs(n, by_key):
    if by_key:
        pairs = [(qi, ki) for ki in range(n) for qi in range(ki, n)]
    else:
        pairs = [(qi, ki) for qi in range(n) for ki in range(qi + 1)]
    return (jnp.asarray([p[0] for p in pairs], jnp.int32), jnp.asarray([p[1] for p in pairs], jnp.int32))


def _scores(q, k, diagonal):
    s = lax.dot_general(q, k, (((1,), (1,)), ((), ())), preferred_element_type=F32)
    if diagonal:
        rows = lax.broadcasted_iota(jnp.int32, s.shape, 0)
        cols = lax.broadcasted_iota(jnp.int32, s.shape, 1)
        s = jnp.where(rows >= cols, s, -jnp.inf)
    return s


def _attn_fwd(q, k, v):
    T = q.shape[0]
    H, DQ, DV = MLA_HEADS, MLA_QK_PAD, MLA_V
    tq = _attn_tile(T)
    nq = T // tq
    scale = float(MLA_NOPE + MLA_ROPE) ** -0.5
    G = MLA_HEADS_PER_STEP
    qi_tab, ki_tab = _attn_pairs(nq, by_key=False)

    def body(qi_ref, ki_ref, q_ref, k_ref, v_ref, o_ref, lse_ref, *scratch):
        m_refs, l_refs, acc_refs = scratch[0:G], scratch[G:2 * G], scratch[2 * G:3 * G]
        p = pl.program_id(1)
        qi, ki = qi_ref[p], ki_ref[p]

        @pl.when(ki == 0)
        def _():
            for g in range(G):
                m_refs[g][...] = jnp.full_like(m_refs[g], -jnp.inf)
                l_refs[g][...] = jnp.zeros_like(l_refs[g])
                acc_refs[g][...] = jnp.zeros_like(acc_refs[g])

        def update(diagonal):
            for g in range(G):
                qs, vs = slice(g * DQ, (g + 1) * DQ), slice(g * DV, (g + 1) * DV)
                st = _scores(k_ref[:, qs], q_ref[:, qs], False)
                if diagonal:
                    key = lax.broadcasted_iota(jnp.int32, st.shape, 0)
                    qry = lax.broadcasted_iota(jnp.int32, st.shape, 1)
                    st = jnp.where(qry >= key, st, -jnp.inf)
                m_prev = m_refs[g][...]
                m_new = jnp.maximum(m_prev, jnp.max(st, axis=0, keepdims=True))
                alpha = jnp.exp(m_prev - m_new)
                pt = jnp.exp(st - m_new)
                l_refs[g][...] = alpha * l_refs[g][...] + jnp.sum(pt, axis=0, keepdims=True)
                acc_refs[g][...] = alpha * acc_refs[g][...] + lax.dot_general(
                    v_ref[:, vs], pt.astype(BF16), (((0,), (0,)), ((), ())), preferred_element_type=F32)
                m_refs[g][...] = m_new

        @pl.when(ki < qi)
        def _():
            update(False)

        @pl.when(ki == qi)
        def _():
            update(True)
            for g in range(G):
                vs = slice(g * DV, (g + 1) * DV)
                o_ref[:, vs] = jnp.transpose(acc_refs[g][...] / l_refs[g][...]).astype(BF16)
                lse_ref[g] = m_refs[g][...] + jnp.log(l_refs[g][...])

    return pl.pallas_call(
        body, name="attn_fwd",
        grid_spec=pltpu.PrefetchScalarGridSpec(
            num_scalar_prefetch=2, grid=(H // G, int(qi_tab.shape[0])),
            in_specs=[pl.BlockSpec((tq, G * DQ), lambda h, p, qt, kt: (qt[p], h)),
                      pl.BlockSpec((tq, G * DQ), lambda h, p, qt, kt: (kt[p], h)),
                      pl.BlockSpec((tq, G * DV), lambda h, p, qt, kt: (kt[p], h))],
            out_specs=(pl.BlockSpec((tq, G * DV), lambda h, p, qt, kt: (qt[p], h)),
                       pl.BlockSpec((G, 1, tq), lambda h, p, qt, kt: (h, 0, qt[p]))),
            scratch_shapes=([pltpu.VMEM((1, tq), F32)] * (2 * G) + [pltpu.VMEM((DV, tq), F32)] * G)),
        out_shape=(jax.ShapeDtypeStruct((T, H * DV), BF16), jax.ShapeDtypeStruct((H, 1, T), F32)),
        compiler_params=_cparams("parallel", "arbitrary"))(qi_tab, ki_tab, q, k, v)


def _attn_bwd(q, k, v, o, do, lse):
    T = q.shape[0]
    H, DQ, DV = MLA_HEADS, MLA_QK_PAD, MLA_V
    tq = _attn_tile(T)
    nq = T // tq
    scale = float(MLA_NOPE + MLA_ROPE) ** -0.5
    tn = (((0,), (0,)), ((), ()))
    nt = (((1,), (1,)), ((), ()))
    G = MLA_HEADS_PER_STEP
    qi_tab, ki_tab = _attn_pairs(nq, by_key=True)

    def body(qi_ref, ki_ref, q_ref, k_ref, v_ref, o_ref, do_ref, lse_ref, dq_ref, dk_ref, dv_ref,
             dk_acc, dv_acc):
        p = pl.program_id(1)
        qi, ki = qi_ref[p], ki_ref[p]

        @pl.when(p == 0)
        def _():
            dq_ref[...] = jnp.zeros_like(dq_ref)

        @pl.when(qi == ki)
        def _():
            dk_acc[...] = jnp.zeros_like(dk_acc)
            dv_acc[...] = jnp.zeros_like(dv_acc)

        def step(diagonal):
            rows = pl.ds(pl.multiple_of(qi * tq, tq), tq)
            for g in range(G):
                qs, vs = slice(g * DQ, (g + 1) * DQ), slice(g * DV, (g + 1) * DV)
                dof = do_ref[:, vs]
                delta = jnp.sum(jnp.transpose(dof.astype(F32) * o_ref[:, vs].astype(F32)), axis=0,
                                keepdims=True)
                st = _scores(k_ref[:, qs], q_ref[:, qs], False)
                if diagonal:
                    key = lax.broadcasted_iota(jnp.int32, st.shape, 0)
                    qry = lax.broadcasted_iota(jnp.int32, st.shape, 1)
                    st = jnp.where(qry >= key, st, -jnp.inf)
                pt = jnp.exp(st - lse_ref[g])
                dpt = lax.dot_general(v_ref[:, vs], dof, nt, preferred_element_type=F32)
                dst = (pt * (dpt - delta)).astype(BF16)
                dv_acc[:, vs] += jnp.dot(pt.astype(BF16), dof, preferred_element_type=F32)
                dk_acc[:, qs] += jnp.dot(dst, q_ref[:, qs], preferred_element_type=F32)
                dq_ref[rows, qs] += lax.dot_general(dst, k_ref[:, qs], tn, preferred_element_type=F32)

        @pl.when(qi == ki)
        def _():
            step(True)

        @pl.when(qi > ki)
        def _():
            step(False)

        @pl.when(qi == nq - 1)
        def _():
            dk_ref[...] = dk_acc[...]
            dv_ref[...] = dv_acc[...]

    qspec = pl.BlockSpec((tq, G * DQ), lambda h, p, qt, kt: (qt[p], h))
    ospec = pl.BlockSpec((tq, G * DV), lambda h, p, qt, kt: (qt[p], h))
    kspec = pl.BlockSpec((tq, G * DQ), lambda h, p, qt, kt: (kt[p], h))
    vspec = pl.BlockSpec((tq, G * DV), lambda h, p, qt, kt: (kt[p], h))
    return pl.pallas_call(
        body, name="attn_bwd",
        grid_spec=pltpu.PrefetchScalarGridSpec(
            num_scalar_prefetch=2, grid=(H // G, int(qi_tab.shape[0])),
            in_specs=[qspec, kspec, vspec, ospec, ospec,
                      pl.BlockSpec((G, 1, tq), lambda h, p, qt, kt: (h, 0, qt[p]))],
            out_specs=(pl.BlockSpec((T, G * DQ), lambda h, p, qt, kt: (0, h)), kspec, vspec),
            scratch_shapes=[pltpu.VMEM((tq, G * DQ), F32), pltpu.VMEM((tq, G * DV), F32)]),
        out_shape=(jax.ShapeDtypeStruct((T, H * DQ), F32), jax.ShapeDtypeStruct((T, H * DQ), F32),
                   jax.ShapeDtypeStruct((T, H * DV), F32)),
        compiler_params=_cparams("parallel", "arbitrary"))(qi_tab, ki_tab, q, k, v, o, do, lse)


def _mla_fwd(a, w, cc, ss, wbuf, pk, slot):
    D = a.shape[1]
    by_k, _, _ = _row_sharded(pk, "mla_w_o", slot, D // N_CHIPS)
    proj = _mm(a, w["w_in"], name="mla_in")
    cqn, ckvn, q, k, v = _mla_mid_fwd(proj, w["q_norm"], w["kv_norm"], w["w_uq"], w["w_ukv"], cc, ss)
    o, lse = _attn_fwd(q, k, v)
    m = _mm(o, wbuf, n=D, b_map=by_k, tm=2048, tk=D // N_CHIPS, tn=D, name="mla_out")
    return m, (a, proj, cqn, ckvn, q, k, v, o, lse)


def _mla_bwd(dm, saved, w, cc, ss, wbuf, gbuf, pk, slot):
    a, proj, cqn, ckvn, q, k, v, o, lse = saved
    D = a.shape[1]
    _, by_n, by_m = _row_sharded(pk, "mla_w_o", slot, D // N_CHIPS)
    do = _mm(dm, wbuf, tb=True, n=o.shape[1], b_map=by_n, tm=2048, tn=D // N_CHIPS, tk=D, out_dtype=BF16,
             name="mla_out_dx")
    gbuf = _mm(o, dm, ta=True, into=gbuf, o_map=by_m, tm=D // N_CHIPS, tn=D, tk=2048, name="mla_out_dw")
    dq, dk, dv = _attn_bwd(q, k, v, o, do, lse)
    dqp, dkv, dproj, dqn, dkn = _mla_mid_bwd(proj, w["q_norm"], w["kv_norm"], w["w_uq"], w["w_ukv"],
                                             cc, ss, dq, dk, dv)
    dw_uq = _mm(cqn, dqp, ta=True, out_dtype=BF16, name="mla_uq_dw")
    dw_ukv = _mm(ckvn, dkv, ta=True, out_dtype=BF16, name="mla_ukv_dw")
    dw_in = _mm(a, dproj, ta=True, out_dtype=BF16, name="mla_in_dw")
    da = _mm(dproj, w["w_in"], tb=True, name="mla_in_dx")
    return da, gbuf, dict(w_in=dw_in, w_uq=dw_uq, w_ukv=dw_ukv, q_norm=dqn, kv_norm=dkn)


def _split_dot(mat, x, parts):
    acc = None
    rem = x
    for _ in range(parts):
        piece = rem.astype(BF16)
        term = jnp.dot(mat, piece, preferred_element_type=F32)
        acc = term if acc is None else acc + term
        rem = rem - piece.astype(F32)
    return acc


def _chunk_sums(cum, rel, rest, logf):
    tb = logf.shape[0]
    stacked = _split_dot(jnp.concatenate([cum, rel, rest], axis=0).astype(BF16), logf, 3)
    return stacked[0:tb], stacked[tb:2 * tb], stacked[2 * tb:3 * tb]


def _chunk_mats(tb):
    C = HGRN_CHUNK
    assert C & (C - 1) == 0
    r = lax.broadcasted_iota(jnp.int32, (tb, tb), 0)
    s = lax.broadcasted_iota(jnp.int32, (tb, tb), 1)
    start = r & ~(C - 1)
    same = start == (s & ~(C - 1))
    ref = start + C // 2
    last = start + C - 1
    one, zero = jnp.float32(1.0), jnp.float32(0.0)
    cum = jnp.where(same & (s <= r), one, zero)
    rel = cum - jnp.where(same & (s <= ref), one, zero)
    rest = jnp.where(same & (s > r) & (s <= last), one, zero)
    rev = jnp.where(same & (s >= r), one, zero)
    ones = jnp.where(same, one, zero)
    causal = same & (s <= r)
    return cum, rel, rest, rev, ones, causal


def _hgrn_gates(p_ref, lb, HK):
    qx = p_ref[:, 0:HK]
    fx = p_ref[:, HK:2 * HK]
    sf = _sigmoid(fx)
    f = lb + (1.0 - lb) * sf
    sq = _sigmoid(qx)
    return qx, sq, qx * sq, sf, f, 1.0 - f, jnp.log(f)


def _hgrn_fwd(proj, lb, o_norm):
    T = proj.shape[0]
    H, C = HGRN_HEADS, HGRN_CHUNK
    HK = proj.shape[1] // 4
    DK = HK // H
    tb = min(HGRN_BLOCK, T)
    ncb = tb // C
    nt = (((1,), (1,)), ((), ()))
    tn = (((0,), (0,)), ((), ()))

    def body(p_ref, lb_ref, on_ref, y_ref, o_ref, st_ref, state, oacc):
        @pl.when(pl.program_id(0) == 0)
        def _():
            state[...] = jnp.zeros_like(state)

        cum, rel, rest, _, _, causal = _chunk_mats(tb)
        _, _, q, _, f, k, logf = _hgrn_gates(p_ref, lb_ref[...], HK)
        b, brel, brest = _chunk_sums(cum, rel, rest, logf)
        eb = jnp.exp(b)
        q_rel = (q * jnp.exp(brel)).astype(BF16)
        k_rel = (k * jnp.exp(-brel)).astype(BF16)
        q_dec = (q * eb).astype(BF16)
        k_dec = (k * jnp.exp(brest)).astype(BF16)
        v = p_ref[:, 2 * HK:3 * HK].astype(BF16)
        for h in range(H):
            hs = slice(h * DK, (h + 1) * DK)
            a = lax.dot_general(q_rel[:, hs], k_rel[:, hs], nt, preferred_element_type=F32)
            a = jnp.where(causal, a, 0.0).astype(BF16)
            oacc[:, hs] = jnp.dot(a, v[:, hs], preferred_element_type=F32)
            for j in range(ncb):
                rs = slice(j * C, (j + 1) * C)
                st = state[h]
                st_ref[j, h] = st
                oacc[rs, hs] += lax.dot_general(q_dec[rs, hs], st.astype(BF16), nt,
                                                preferred_element_type=F32)
                dec = jnp.exp(jnp.sum(logf[rs, hs], axis=0, keepdims=True))
                state[h] = dec * st + lax.dot_general(v[rs, hs], k_dec[rs, hs], tn,
                                                      preferred_element_type=F32)
        o = oacc[...]
        o_ref[...] = o
        gx = p_ref[:, 3 * HK:4 * HK]
        gate = gx * _sigmoid(gx)
        for h in range(H):
            hs = slice(h * DK, (h + 1) * DK)
            oh = o[:, hs]
            y_ref[:, hs] = (oh * _rms_rstd(oh) * on_ref[...] * gate[:, hs]).astype(BF16)

    return pl.pallas_call(
        body, name="hgrn_fwd", grid=(T // tb,),
        in_specs=[pl.BlockSpec((tb, 4 * HK), lambda i: (i, 0)),
                  pl.BlockSpec((1, HK), lambda i: (0, 0)),
                  pl.BlockSpec((1, DK), lambda i: (0, 0))],
        out_specs=(pl.BlockSpec((tb, HK), lambda i: (i, 0)),
                   pl.BlockSpec((tb, HK), lambda i: (i, 0)),
                   pl.BlockSpec((ncb, H, DK, DK), lambda i: (i, 0, 0, 0))),
        out_shape=(jax.ShapeDtypeStruct((T, HK), BF16), jax.ShapeDtypeStruct((T, HK), F32),
                   jax.ShapeDtypeStruct((T // C, H, DK, DK), F32)),
        scratch_shapes=[pltpu.VMEM((H, DK, DK), F32), pltpu.VMEM((tb, HK), F32)],
        compiler_params=_cparams("arbitrary"))(proj, lb, o_norm)


def _hgrn_bwd(proj, lb, o_norm, o, states, dy):
    T = proj.shape[0]
    H, C = HGRN_HEADS, HGRN_CHUNK
    HK = proj.shape[1] // 4
    DK = HK // H
    tb = min(HGRN_BLOCK, T)
    ncb = tb // C
    nb = T // tb
    nt = (((1,), (1,)), ((), ()))
    tn = (((0,), (0,)), ((), ()))

    def body(p_ref, lb_ref, on_ref, o_ref, st_ref, dy_ref, dp_ref, dlb_ref, don_ref,
             dstate, dqr_s, dkr_s, dqd_s, dkd_s, dv_s, do_s, e_s):
        @pl.when(pl.program_id(0) == 0)
        def _():
            dstate[...] = jnp.zeros_like(dstate)
            dlb_ref[...] = jnp.zeros_like(dlb_ref)
            don_ref[...] = jnp.zeros_like(don_ref)

        cum, rel, rest, rev, ones, causal = _chunk_mats(tb)
        lb = lb_ref[...]
        qx, sq, q, sf, f, k, logf = _hgrn_gates(p_ref, lb, HK)
        b, brel, brest = _chunk_sums(cum, rel, rest, logf)
        eb = jnp.exp(b)
        erel = jnp.exp(brel)
        enrel = jnp.exp(-brel)
        erest = jnp.exp(brest)
        q_rel_f, k_rel_f, q_dec_f, k_dec_f = q * erel, k * enrel, q * eb, k * erest
        q_rel, k_rel = q_rel_f.astype(BF16), k_rel_f.astype(BF16)
        q_dec, k_dec = q_dec_f.astype(BF16), k_dec_f.astype(BF16)
        v = p_ref[:, 2 * HK:3 * HK].astype(BF16)

        gx = p_ref[:, 3 * HK:4 * HK]
        sg = _sigmoid(gx)
        gate = gx * sg
        dy = dy_ref[...]
        ov = o_ref[...]
        on = on_ref[...]
        don = jnp.zeros((1, DK), F32)
        for h in range(H):
            hs = slice(h * DK, (h + 1) * DK)
            oh = ov[:, hs]
            r = _rms_rstd(oh)
            xh = oh * r
            d_on = dy[:, hs] * gate[:, hs]
            don = don + jnp.sum(d_on * xh, axis=0, keepdims=True)
            u = d_on * on
            do_s[:, hs] = r * (u - xh * jnp.mean(u * xh, axis=-1, keepdims=True))
            dp_ref[:, 3 * HK + h * DK:3 * HK + (h + 1) * DK] = (
                dy[:, hs] * xh * on * (sg[:, hs] * (1.0 + gx[:, hs] * (1.0 - sg[:, hs])))).astype(BF16)
        don_ref[...] += don

        for h in range(H):
            hs = slice(h * DK, (h + 1) * DK)
            doh = do_s[:, hs].astype(BF16)
            a = lax.dot_general(q_rel[:, hs], k_rel[:, hs], nt, preferred_element_type=F32)
            a = jnp.where(causal, a, 0.0).astype(BF16)
            da = lax.dot_general(doh, v[:, hs], nt, preferred_element_type=F32)
            da = jnp.where(causal, da, 0.0).astype(BF16)
            dv_s[:, hs] = lax.dot_general(a, doh, tn, preferred_element_type=F32)
            dqr_s[:, hs] = jnp.dot(da, k_rel[:, hs], preferred_element_type=F32)
            dkr_s[:, hs] = lax.dot_general(da, q_rel[:, hs], tn, preferred_element_type=F32)
            for j in reversed(range(ncb)):
                rs = slice(j * C, (j + 1) * C)
                dst = dstate[h]
                dstb = dst.astype(BF16)
                st = st_ref[j, h]
                dkd_s[rs, hs] = jnp.dot(v[rs, hs], dstb, preferred_element_type=F32)
                dv_s[rs, hs] += lax.dot_general(k_dec[rs, hs], dstb, nt, preferred_element_type=F32)
                dec = jnp.exp(jnp.sum(logf[rs, hs], axis=0, keepdims=True))
                e_s[rs, hs] = jnp.broadcast_to(jnp.sum(dst * st, axis=0, keepdims=True) * dec, (C, DK))
                dqd_s[rs, hs] = jnp.dot(doh[rs], st.astype(BF16), preferred_element_type=F32)
                dstate[h] = dec * dst + lax.dot_general(doh[rs], q_dec[rs, hs], tn,
                                                        preferred_element_type=F32)

        dqr, dkr, dqd, dkd = dqr_s[...], dkr_s[...], dqd_s[...], dkd_s[...]
        kdk = dkd * k_dec_f
        db = dqr * q_rel_f - dkr * k_rel_f + dqd * q_dec_f - kdk
        dlogf = _split_dot(rev.astype(BF16), db, 2) + _split_dot(ones.astype(BF16), kdk, 2) + e_s[...]
        dk = dkr * enrel + dkd * erest
        df = dlogf / f - dk
        dlb_ref[...] += jnp.sum(df * (1.0 - sf), axis=0, keepdims=True)
        dq = dqr * erel + dqd * eb
        dp_ref[:, 0:HK] = (dq * (sq * (1.0 + qx * (1.0 - sq)))).astype(BF16)
        dp_ref[:, HK:2 * HK] = (df * (1.0 - lb) * sf * (1.0 - sf)).astype(BF16)
        dp_ref[:, 2 * HK:3 * HK] = dv_s[...].astype(BF16)

    rev_row = lambda w: pl.BlockSpec((tb, w), lambda i: (nb - 1 - i, 0))
    vec = lambda w: pl.BlockSpec((1, w), lambda i: (0, 0))
    scr = pltpu.VMEM((tb, HK), F32)
    return pl.pallas_call(
        body, name="hgrn_bwd", grid=(nb,),
        in_specs=[rev_row(4 * HK), vec(HK), vec(DK), rev_row(HK),
                  pl.BlockSpec((ncb, H, DK, DK), lambda i: (nb - 1 - i, 0, 0, 0)), rev_row(HK)],
        out_specs=(rev_row(4 * HK), vec(HK), vec(DK)),
        out_shape=(jax.ShapeDtypeStruct((T, 4 * HK), BF16), jax.ShapeDtypeStruct((1, HK), F32),
                   jax.ShapeDtypeStruct((1, DK), F32)),
        scratch_shapes=[pltpu.VMEM((H, DK, DK), F32), scr, scr, scr, scr, scr, scr, scr],
        compiler_params=_cparams("arbitrary"))(proj, lb, o_norm, o, states, dy)


def _hgrn_layer_fwd(a, o_norm, lb, wbuf, pk, slot):
    D = a.shape[1]
    in_by_n, _ = _col_sharded(pk, "hgrn_w_in", slot, D)
    out_by_k, _, _ = _row_sharded(pk, "hgrn_w_o", slot, D // N_CHIPS)
    proj = _mm(a, wbuf, n=4 * D, b_map=in_by_n, tk=D, tn=D, name="hgrn_in")
    y, o, states = _hgrn_fwd(proj, lb, o_norm)
    m = _mm(y, wbuf, n=D, b_map=out_by_k, tm=2048, tk=D // N_CHIPS, tn=D, name="hgrn_out")
    return m, (a, proj, y, o, states)


def _hgrn_layer_bwd(dm, saved, o_norm, lb, wbuf, gbuf, pk, slot):
    a, proj, y, o, states = saved
    D = a.shape[1]
    in_by_n, in_by_k = _col_sharded(pk, "hgrn_w_in", slot, D)
    _, out_by_n, out_by_m = _row_sharded(pk, "hgrn_w_o", slot, D // N_CHIPS)
    dy = _mm(dm, wbuf, tb=True, n=y.shape[1], b_map=out_by_n, tm=2048, tn=D // N_CHIPS, tk=D,
             name="hgrn_out_dx")
    gbuf = _mm(y, dm, ta=True, into=gbuf, o_map=out_by_m, tm=D // N_CHIPS, tn=D, tk=2048, name="hgrn_out_dw")
    dproj, dlb, don = _hgrn_bwd(proj, lb, o_norm, o, states, dy)
    gbuf = _mm(a, dproj, ta=True, into=gbuf, o_map=in_by_n, tm=D, tn=D, name="hgrn_in_dw")
    da = _mm(dproj, wbuf, tb=True, n=D, b_map=in_by_k, tn=D, tk=D, name="hgrn_in_dx")
    return da, gbuf, dict(o_norm=don, lb=dlb)


def _lower_bounds(lb_logits):
    p = jax.nn.softmax(lb_logits.astype(F32), axis=0)
    return jnp.cumsum(p, axis=0) - p[0]


def _rope_tables(positions):
    inv_freq = jnp.power(ROPE_BASE, -jnp.arange(0, MLA_ROPE, 2, dtype=F32) / MLA_ROPE)
    ang = positions.astype(F32)[:, None] * inv_freq
    cos, sin = jnp.cos(ang), jnp.sin(ang)
    zero = jnp.zeros((positions.shape[0], 128 - MLA_ROPE), F32)
    return (jnp.concatenate([cos, cos, zero], axis=-1), jnp.concatenate([-sin, sin, zero], axis=-1))


def _pad_mla_weights(w_in, w_uq):
    w_in_p = jnp.pad(w_in, ((0, 0), (0, 0), (0, 128 - MLA_ROPE)))
    n, ql, _ = w_uq.shape
    w_uq_p = jnp.pad(w_uq.reshape(n, ql, MLA_HEADS, MLA_NOPE + MLA_ROPE),
                     ((0, 0), (0, 0), (0, 0), (0, MLA_QK_PAD - MLA_NOPE - MLA_ROPE)))
    return w_in_p, w_uq_p.reshape(n, ql, MLA_HEADS * MLA_QK_PAD)


def _local_step(x, positions, target, small, fetch, gbufs, emit, emit_mlp):
    T, D = x.shape
    lbounds, lb_vjp = jax.vjp(_lower_bounds, small["hgrn_lb_logits"])
    cc, ss = _rope_tables(positions)
    fetched = {0: fetch(0, None)}
    gains = fetched[0]["gains"]
    tick = [jnp.zeros((), F32)]

    def g(layer, i):
        return gains[layer, i][None, :] + tick[0]

    def mla_weights(layer):
        f = fetched[layer]
        w_in_p, w_uq_p = _pad_mla_weights(f["w_in"][None], f["w_uq"][None])
        slot = layer // 2
        return dict(w_in=w_in_p[0], w_uq=w_uq_p[0], w_ukv=f["w_ukv"],
                    q_norm=small["mla_q_norm"][slot][None, :], kv_norm=small["mla_kv_norm"][slot][None, :])

    saved = []
    h = x
    a = _prenorm_fwd(x, g(0, 0))
    dy = sq = None
    for layer in range(DEPTH):
        slot = layer // 2
        if layer not in fetched:
            fetched[layer] = fetch(layer, a)
        wbuf, pk = fetched[layer]["wbuf"], fetched[layer]["pk"]
        if layer % 2 == 0:
            m, mix_saved = _mla_fwd(a, mla_weights(layer), cc, ss, wbuf, pk, slot)
        else:
            m, mix_saved = _hgrn_layer_fwd(a, small["hgrn_o_norm"][slot][None, :], lbounds[layer][None, :],
                                           wbuf, pk, slot)
        h1, a2 = _resnorm_fwd(h, m, g(layer, 1), g(layer, 2), name="resnorm_fwd_mix")
        u, mlp_saved = _mlp_fwd(a2, wbuf, pk, layer)
        if layer + 1 < DEPTH:
            h2, a = _resnorm_fwd(h1, u, g(layer, 3), g(layer + 1, 0), name="resnorm_fwd_mlp")
        else:
            h2 = None
            dy, sq = _resnorm_loss(h1, u, g(layer, 3), target)
        saved.append((h, m, h1, u, mix_saved, mlp_saved))
        h = h2

    n_mla, n_hgrn = (DEPTH + 1) // 2, DEPTH // 2
    dgains = [[None] * 4 for _ in range(DEPTH)]
    gw = {k: [None] * n_mla for k in ("mla_w_in", "mla_w_uq", "mla_w_ukv", "mla_q_norm", "mla_kv_norm")}
    gw["hgrn_o_norm"] = [None] * n_hgrn
    dlb = [jnp.zeros((1, lbounds.shape[1]), F32) for _ in range(DEPTH)]
    dh = dy
    da_next = None
    for layer in reversed(range(DEPTH)):
        h0, m, h1, u, mix_saved, mlp_saved = saved[layer]
        slot = layer // 2
        wbuf, pk, gbuf = fetched[layer]["wbuf"], fetched[layer]["pk"], gbufs[layer]
        if da_next is None:
            du, dgains[layer][3] = _resnorm_bwd(u, g(layer, 3), dh, name="resnorm_bwd_last")
            t = dh
        else:
            h2 = saved[layer + 1][0]
            t, du, dgains[layer][3], dgains[layer + 1][0] = _resnorm_bwd(
                u, g(layer, 3), dh, h2, da_next, g(layer + 1, 0), name="resnorm_bwd_mlp")
        da2, gbuf = _mlp_bwd(du, mlp_saved, wbuf, gbuf, pk, layer)
        if layer == 0:
            gbuf = emit_mlp(layer, gbuf)
        t, dm, dgains[layer][1], dgains[layer][2] = _resnorm_bwd(
            m, g(layer, 1), t, h1, da2, g(layer, 2), name="resnorm_bwd_mix")
        if layer % 2 == 0:
            da_next, gbuf, mg = _mla_bwd(dm, mix_saved, mla_weights(layer), cc, ss, wbuf, gbuf, pk, slot)
            ql = mg["q_norm"].shape[-1]
            kvl = mg["kv_norm"].shape[-1]
            gw["mla_w_in"][slot] = mg["w_in"][:, :ql + kvl + MLA_ROPE]
            gw["mla_w_uq"][slot] = mg["w_uq"].reshape(ql, MLA_HEADS, MLA_QK_PAD)[
                :, :, :MLA_NOPE + MLA_ROPE].reshape(ql, MLA_HEADS * (MLA_NOPE + MLA_ROPE))
            gw["mla_w_ukv"][slot] = mg["w_ukv"]
            gw["mla_q_norm"][slot] = mg["q_norm"][0]
            gw["mla_kv_norm"][slot] = mg["kv_norm"][0]
        else:
            da_next, gbuf, hg = _hgrn_layer_bwd(dm, mix_saved, small["hgrn_o_norm"][slot][None, :],
                                                lbounds[layer][None, :], wbuf, gbuf, pk, slot)
            gw["hgrn_o_norm"][slot] = hg["o_norm"][0]
            dlb[layer] = hg["lb"]
        dh = t
        if layer > 0:
            mine = ({k: gw[k][slot] for k in ("mla_w_in", "mla_w_uq", "mla_w_ukv")} if layer % 2 == 0 else {})
            tick[0] = emit(layer, gbuf, mine)
        else:
            gbuf0 = gbuf
    grad_x, dgains[0][0] = _prenorm_bwd(x, g(0, 0), dh, da_next)

    last = {k: gw[k][0] for k in ("mla_w_in", "mla_w_uq", "mla_w_ukv")}
    last.update({k: jnp.stack(gw[k]) for k in ("mla_q_norm", "mla_kv_norm", "hgrn_o_norm")})
    last["norm_gains"] = jnp.stack([jnp.concatenate(row, axis=0) for row in dgains])
    (last["hgrn_lb_logits"],) = lb_vjp(jnp.concatenate(dlb, axis=0))
    emit(0, gbuf0, last)
    return sq, grad_x


def _size(shape):
    n = 1
    for d in shape:
        n *= d
    return n


def _piece_rows(shape):
    return -(-_size(shape) // PACK_W)


def _packed_misc_rows(shapes):
    return sum(_piece_rows(s) for s in shapes)


def _cast_into(src, buf, row, name):
    rows, W = src.shape
    tr = min(256, rows)
    assert rows % tr == 0 and row % tr == 0

    def body(s_ref, b_ref, o_ref):
        o_ref[...] = s_ref[...].astype(BF16)

    return pl.pallas_call(
        body, name=name, grid=(rows // tr,),
        in_specs=[pl.BlockSpec((tr, W), lambda i: (i, 0)), pl.BlockSpec(memory_space=pl.ANY)],
        out_specs=pl.BlockSpec((tr, W), lambda i: (row // tr + i, 0)),
        out_shape=jax.ShapeDtypeStruct(buf.shape, buf.dtype), input_output_aliases={1: 0},
        compiler_params=_cparams("parallel"))(src, buf)


def _pack_blocks(pieces, rows, dtype):
    blocks, used = [], 0
    for p in pieces:
        flat = p.astype(dtype).reshape(-1)
        r = _piece_rows(p.shape)
        if r * PACK_W != flat.shape[0]:
            flat = jnp.pad(flat, (0, r * PACK_W - flat.shape[0]))
        blocks.append(flat.reshape(r, PACK_W))
        used += r
    if rows > used:
        blocks.append(jnp.zeros((rows - used, PACK_W), dtype))
    return blocks


def _unpack(buf, shapes):
    out, off = [], 0
    for shp in shapes:
        r = _piece_rows(shp)
        piece = buf[off:off + r]
        if r * PACK_W != _size(shp):
            piece = piece.reshape(-1)[:_size(shp)]
        out.append(piece.reshape(shp))
        off += r
    return out


def _mesh_place():
    x, y, c = lax.axis_index("x"), lax.axis_index("y"), lax.axis_index("c")
    chips = [(1 - x, y), (x, 1 - y), (1 - x, 1 - y)]
    return x, y, c, chips


_HBM = pl.BlockSpec(memory_space=pltpu.HBM)


def _all_gather(wp):
    R, W = wp.shape
    rh = R // 2
    rq = rh // 2
    assert rq % 16 == 0

    def body(w_ref, out_ref, send_sems, recv_sems):
        x, y, c, _ = _mesh_place()
        me, jx, jy, jd = 2 * x + y, 2 * (1 - x) + y, 2 * x + (1 - y), 2 * (1 - x) + (1 - y)
        to_x, to_y, sibling = (1 - x, y, c), (x, 1 - y, c), (x, y, 1 - c)

        def rows(core, quarter):
            return pl.ds(pl.multiple_of(core * rh + quarter * rq, 16), rq)

        def slot(j, core, quarter):
            return out_ref.at[j, rows(core, quarter)]

        def copy(k, src, dst, to):
            return pltpu.make_async_remote_copy(src_ref=src, dst_ref=dst, send_sem=send_sems.at[k],
                                                recv_sem=recv_sems.at[k], device_id=to, device_id_type=MESH)

        sends = [copy(0, w_ref.at[rows(c, 0)], slot(me, c, 0), to_x),
                 copy(2, w_ref.at[rows(c, 1)], slot(me, c, 1), to_y),
                 copy(1, w_ref.at[rows(c, 1)], slot(me, c, 1), to_x),
                 copy(3, w_ref.at[rows(c, 0)], slot(me, c, 0), to_y)]
        for cp in sends:
            cp.start()
        arrivals = [(0, slot(jx, c, 0), 4, to_y, 6), (2, slot(jy, c, 1), 5, to_x, 7),
                    (1, slot(jx, c, 1), None, None, 8), (3, slot(jy, c, 0), None, None, 9),
                    (4, slot(jd, c, 0), None, None, 10), (5, slot(jd, c, 1), None, None, 11)]
        for k, landed, k_on, to_on, k_sib in arrivals:
            copy(k, landed, landed, sibling).wait_recv()
            if k_on is not None:
                cp = copy(k_on, landed, landed, to_on)
                cp.start()
                sends.append(cp)
            cp = copy(k_sib, landed, landed, sibling)
            cp.start()
            sends.append(cp)
        for k_sib, j, quarter in ((6, jx, 0), (7, jy, 1), (8, jx, 1), (9, jy, 0), (10, jd, 0), (11, jd, 1)):
            landed = slot(j, 1 - c, quarter)
            copy(k_sib, landed, landed, sibling).wait_recv()
        for cp in sends:
            cp.wait_send()

    out = pl.pallas_call(
        body, name="weights_all_gather", in_specs=[_HBM], out_specs=_HBM,
        out_shape=jax.ShapeDtypeStruct((N_CHIPS, R, W), wp.dtype),
        scratch_shapes=[pltpu.SemaphoreType.DMA((12,)), pltpu.SemaphoreType.DMA((12,))],
    )(wp)
    me = 2 * lax.axis_index("x") + lax.axis_index("y")
    return lax.dynamic_update_slice(out, wp[None], (me, 0, 0))


def _exchange_halves(g):
    n, _, rh, W = g.shape

    def body(g_ref, out_ref, send_sems, recv_sems):
        x, y, c, _ = _mesh_place()
        sibling = (x, y, 1 - c)
        copies = [pltpu.make_async_remote_copy(
            src_ref=g_ref.at[j, 1 - c], dst_ref=out_ref.at[j], send_sem=send_sems.at[j],
            recv_sem=recv_sems.at[j], device_id=sibling, device_id_type=MESH) for j in range(n)]
        for cp in copies:
            cp.start()
        for cp in copies:
            cp.wait()

    return pl.pallas_call(
        body, name="grads_to_sibling", in_specs=[_HBM], out_specs=_HBM,
        out_shape=jax.ShapeDtypeStruct((n, rh, W), g.dtype),
        scratch_shapes=[pltpu.SemaphoreType.DMA((n,)), pltpu.SemaphoreType.DMA((n,))],
    )(g)


def _scatter_to_owners(p):
    n, rh, W = p.shape
    rq = rh // 2
    assert rq % 16 == 0

    def body(p_ref, out_ref, stage_ref, send_sems, recv_sems):
        x, y, c, _ = _mesh_place()
        me, jx, jy, jd = 2 * x + y, 2 * (1 - x) + y, 2 * x + (1 - y), 2 * (1 - x) + (1 - y)
        to_x, to_y = (1 - x, y, c), (x, 1 - y, c)

        def quarter(ref, j, q):
            return ref.at[j, pl.ds(q * rq, rq)]

        def copy(k, src, dst, to):
            return pltpu.make_async_remote_copy(src_ref=src, dst_ref=dst, send_sem=send_sems.at[k],
                                                recv_sem=recv_sems.at[k], device_id=to, device_id_type=MESH)

        sends = [copy(2, quarter(p_ref, jd, 0), stage_ref.at[0], to_x),
                 copy(3, quarter(p_ref, jd, 1), stage_ref.at[1], to_y),
                 copy(0, p_ref.at[jx], out_ref.at[me], to_x),
                 copy(1, p_ref.at[jy], out_ref.at[me], to_y)]
        for cp in sends:
            cp.start()
        copy(2, stage_ref.at[0], stage_ref.at[0], to_x).wait_recv()
        relay = copy(4, stage_ref.at[0], quarter(out_ref, jx, 0), to_y)
        relay.start()
        sends.append(relay)
        copy(3, stage_ref.at[1], stage_ref.at[1], to_y).wait_recv()
        relay = copy(5, stage_ref.at[1], quarter(out_ref, jy, 1), to_x)
        relay.start()
        sends.append(relay)
        copy(0, out_ref.at[jx], out_ref.at[jx], to_x).wait_recv()
        copy(1, out_ref.at[jy], out_ref.at[jy], to_y).wait_recv()
        copy(4, quarter(out_ref, jd, 0), quarter(out_ref, jd, 0), to_y).wait_recv()
        copy(5, quarter(out_ref, jd, 1), quarter(out_ref, jd, 1), to_x).wait_recv()
        for cp in sends:
            cp.wait_send()

    out, _ = pl.pallas_call(
        body, name="grads_to_owner", in_specs=[_HBM], out_specs=(_HBM, _HBM),
        out_shape=(jax.ShapeDtypeStruct((n, rh, W), p.dtype), jax.ShapeDtypeStruct((2, rq, W), p.dtype)),
        scratch_shapes=[pltpu.SemaphoreType.DMA((6,)), pltpu.SemaphoreType.DMA((6,))],
    )(p)
    me = 2 * lax.axis_index("x") + lax.axis_index("y")
    mine = lax.dynamic_index_in_dim(p, me, axis=0, keepdims=True)
    return lax.dynamic_update_slice(out, mine, (me, 0, 0))


def _share_reduced(q, name="grads_share_reduced"):
    rh, W = q.shape

    def body(q_ref, out_ref, send_sem, recv_sem):
        x, y, c, _ = _mesh_place()
        cp = pltpu.make_async_remote_copy(src_ref=q_ref, dst_ref=out_ref.at[c], send_sem=send_sem,
                                          recv_sem=recv_sem, device_id=(x, y, 1 - c), device_id_type=MESH)
        cp.start()
        cp.wait()

    out = pl.pallas_call(
        body, name=name, in_specs=[_HBM], out_specs=_HBM,
        out_shape=jax.ShapeDtypeStruct((2, rh, W), q.dtype),
        scratch_shapes=[pltpu.SemaphoreType.DMA, pltpu.SemaphoreType.DMA],
    )(q)
    return lax.dynamic_update_slice(out, q[None], (lax.axis_index("c"), 0, 0))


def _add_sibling(g, recv, c_arr):
    n, _, rh, W = g.shape
    tr = PACK_TILE

    def body(c_ref, g_ref, r_ref, o_ref):
        o_ref[...] = (g_ref[...].astype(F32) + r_ref[...].astype(F32)).astype(BF16)

    return pl.pallas_call(
        body, name="grads_add_sibling",
        grid_spec=pltpu.PrefetchScalarGridSpec(
            num_scalar_prefetch=1, grid=(n, rh // tr),
            in_specs=[pl.BlockSpec((None, None, tr, W), lambda j, i, c_ref: (j, c_ref[0], i, 0)),
                      pl.BlockSpec((None, tr, W), lambda j, i, c_ref: (j, i, 0))],
            out_specs=pl.BlockSpec((None, tr, W), lambda j, i, c_ref: (j, i, 0))),
        out_shape=jax.ShapeDtypeStruct((n, rh, W), BF16),
        compiler_params=_cparams("parallel", "parallel"))(c_arr, g, recv)


def _sum_chips(parts, name="grads_sum_chips", out_dtype=F32):
    n, rh, W = parts.shape
    tr = PACK_TILE

    def body(p_ref, o_ref):
        acc = p_ref[0].astype(F32)
        for j in range(1, n):
            acc = acc + p_ref[j].astype(F32)
        o_ref[...] = acc.astype(out_dtype)

    return pl.pallas_call(
        body, name=name, grid=(rh // tr,),
        in_specs=[pl.BlockSpec((n, tr, W), lambda i: (0, i, 0))],
        out_specs=pl.BlockSpec((tr, W), lambda i: (i, 0)),
        out_shape=jax.ShapeDtypeStruct((rh, W), out_dtype),
        compiler_params=_cparams("parallel"))(parts)


_SEM = pl.BlockSpec(memory_space=pltpu.SEMAPHORE)
_ASYNC = pltpu.CompilerParams(has_side_effects=pltpu.SideEffectType.DATAFLOW_SIDE_EFFECTING)


def _hbm(a):
    return pltpu.with_memory_space_constraint(a, pltpu.HBM)


def _gather_copies(w_ref, land_ref, send_sems, recv_sems):
    x, y, c, chips = _mesh_place()
    me = 2 * x + y
    rh = w_ref.shape[0] // 2
    rows = pl.ds(pl.multiple_of(c * rh, 16), rh)
    return [pltpu.make_async_remote_copy(
        src_ref=w_ref.at[rows], dst_ref=land_ref.at[me, rows], send_sem=send_sems.at[r],
        recv_sem=recv_sems.at[r], device_id=(px, py, c), device_id_type=MESH)
        for r, (px, py) in enumerate(chips)]


def _scatter_copies(g_ref, land_ref, send_sems, recv_sems, row0):
    x, y, c, chips = _mesh_place()
    me = 2 * x + y
    rows = pl.ds(row0, land_ref.shape[1])
    return [pltpu.make_async_remote_copy(
        src_ref=g_ref.at[2 * px + py, rows], dst_ref=land_ref.at[me], send_sem=send_sems.at[r],
        recv_sem=recv_sems.at[r], device_id=(px, py, c), device_id_type=MESH)
        for r, (px, py) in enumerate(chips)]


def _halves_to_sibling(land, name):
    n, R, W = land.shape
    rh = R // 2

    def body(l_ref, o_ref, send_sems, recv_sems):
        x, y, c, chips = _mesh_place()
        rows = pl.ds(pl.multiple_of(c * rh, 16), rh)
        copies = [pltpu.make_async_remote_copy(
            src_ref=o_ref.at[2 * px + py, rows], dst_ref=o_ref.at[2 * px + py, rows], send_sem=send_sems.at[r],
            recv_sem=recv_sems.at[r], device_id=(x, y, 1 - c), device_id_type=MESH)
            for r, (px, py) in enumerate(chips)]
        for cp in copies:
            cp.start()
        for cp in copies:
            cp.wait()

    return pl.pallas_call(
        body, name=name, in_specs=[_HBM], out_specs=_HBM, out_shape=jax.ShapeDtypeStruct(land.shape, land.dtype),
        scratch_shapes=[pltpu.SemaphoreType.DMA((3,)), pltpu.SemaphoreType.DMA((3,))],
        input_output_aliases={0: 0})(land)


def _gather_start(wp, name):
    R, W = wp.shape

    def body(w_ref, land_ref, send_sems, recv_sems, w_thru, land_thru, token):
        for cp in _gather_copies(w_ref, land_ref, send_sems, recv_sems):
            cp.start()
        token[...] = jnp.zeros_like(token)

    return pl.pallas_call(
        body, name=name,
        out_shape=(pltpu.SemaphoreType.DMA((3,)), pltpu.SemaphoreType.DMA((3,)), pltpu.HBM(wp.shape, wp.dtype),
                   pltpu.HBM((N_CHIPS, R, W), wp.dtype), jax.ShapeDtypeStruct((8, 128), F32)),
        in_specs=(_HBM, _HBM),
        out_specs=(_SEM, _SEM, _HBM, _HBM, pl.BlockSpec(memory_space=pltpu.VMEM)),
        input_output_aliases={0: 2, 1: 3}, compiler_params=_ASYNC,
    )(_hbm(wp), _hbm(lax.empty((N_CHIPS, R, W), wp.dtype)))


def _gather_wait(send_sems, recv_sems, w_thru, land_thru, after, name):
    R, W = w_thru.shape
    rh = R // 2

    def body(w_ref, land_ref, send_sems, recv_sems, after_ref, w_dead, got_ref):
        x, y, c, _ = _mesh_place()
        half = land_ref.at[0, pl.ds(0, rh)]
        for k in range(3):
            cp = pltpu.make_async_remote_copy(src_ref=half, dst_ref=half, send_sem=send_sems.at[k],
                                              recv_sem=recv_sems.at[k], device_id=(x, y, 1 - c),
                                              device_id_type=MESH)
            cp.wait_send()
            cp.wait_recv()

    return pl.pallas_call(
        body, name=name,
        out_shape=(pltpu.HBM(w_thru.shape, w_thru.dtype), pltpu.HBM(land_thru.shape, land_thru.dtype)),
        in_specs=(_HBM, _HBM, _SEM, _SEM, pl.BlockSpec(memory_space=pl.ANY)), out_specs=(_HBM, _HBM),
        input_output_aliases={0: 0, 1: 1}, compiler_params=_ASYNC,
    )(w_thru, land_thru, send_sems, recv_sems, after)


def _scatter_start(g, row0, nrows, name):
    n, R, W = g.shape
    land_shape = (n, nrows, W)

    def body(g_ref, land_ref, send_sems, recv_sems, g_thru, land_thru, token):
        for cp in _scatter_copies(g_ref, land_ref, send_sems, recv_sems, row0):
            cp.start()
        token[...] = jnp.zeros_like(token)

    return pl.pallas_call(
        body, name=name,
        out_shape=(pltpu.SemaphoreType.DMA((3,)), pltpu.SemaphoreType.DMA((3,)), pltpu.HBM(g.shape, g.dtype),
                   pltpu.HBM(land_shape, g.dtype), jax.ShapeDtypeStruct((8, 128), F32)),
        in_specs=(_HBM, _HBM),
        out_specs=(_SEM, _SEM, _HBM, _HBM, pl.BlockSpec(memory_space=pltpu.VMEM)),
        input_output_aliases={0: 2, 1: 3}, compiler_params=_ASYNC,
    )(_hbm(g), _hbm(lax.empty(land_shape, g.dtype)))


def _scatter_wait(send_sems, recv_sems, g_thru, land_thru, after, name):
    def body(g_ref, land_ref, send_sems, recv_sems, after_ref, g_out, got_ref):
        x, y, c, _ = _mesh_place()
        for k in range(3):
            cp = pltpu.make_async_remote_copy(src_ref=land_ref.at[0], dst_ref=land_ref.at[0], send_sem=send_sems.at[k],
                                              recv_sem=recv_sems.at[k], device_id=(x, y, 1 - c),
                                              device_id_type=MESH)
            cp.wait_send()
            cp.wait_recv()

    return pl.pallas_call(
        body, name=name,
        out_shape=(pltpu.HBM(g_thru.shape, g_thru.dtype), pltpu.HBM(land_thru.shape, land_thru.dtype)),
        in_specs=(_HBM, _HBM, _SEM, _SEM, pl.BlockSpec(memory_space=pl.ANY)), out_specs=(_HBM, _HBM),
        input_output_aliases={0: 0, 1: 1}, compiler_params=_ASYNC,
    )(g_thru, land_thru, send_sems, recv_sems, after)


def _adamw(w, g, m, v, name):
    shape = w.shape
    cols = shape[-1]
    w2, g2, m2, v2 = (t.reshape(-1, cols) for t in (w, g, m, v))
    rows = w2.shape[0]
    tr = rows
    for cand in (512, 256, 128, 64, 32, 16, 8):
        if rows > cand and rows % cand == 0:
            tr = cand
            break
    c1 = 1.0 / (1.0 - ADAM_B1 ** ADAM_STEP)
    c2 = 1.0 / (1.0 - ADAM_B2 ** ADAM_STEP)

    def body(w_ref, g_ref, m_ref, v_ref, d_ref, nm_ref, nv_ref):
        gv = g_ref[...]
        nm = ADAM_B1 * m_ref[...] + (1.0 - ADAM_B1) * gv
        nv = ADAM_B2 * v_ref[...] + (1.0 - ADAM_B2) * (gv * gv)
        nm_ref[...] = nm
        nv_ref[...] = nv
        d_ref[...] = -ADAM_LR * ((nm * c1) / (jnp.sqrt(nv * c2) + ADAM_EPS) + ADAM_WD * w_ref[...])

    blk = pl.BlockSpec((tr, cols), lambda i: (i, 0))
    sds = jax.ShapeDtypeStruct((rows, cols), F32)
    d, nm, nv = pl.pallas_call(body, name=name, grid=(rows // tr,), in_specs=[blk] * 4,
                               out_specs=(blk, blk, blk), out_shape=(sds, sds, sds),
                               compiler_params=_cparams("parallel"))(w2, g2, m2, v2)
    return d.reshape(shape), nm.reshape(shape), nv.reshape(shape)


def kernel(x, positions, norm_gains, mla_w_in, mla_q_norm, mla_kv_norm, mla_w_uq, mla_w_ukv, mla_w_o, hgrn_w_in, hgrn_lb_logits, hgrn_o_norm, hgrn_w_o, mlp_w1, mlp_w2, loss_target, m_norm_gains, m_mla_w_in, m_mla_q_norm, m_mla_kv_norm, m_mla_w_uq, m_mla_w_ukv, m_mla_w_o, m_hgrn_w_in, m_hgrn_lb_logits, m_hgrn_o_norm, m_hgrn_w_o, m_mlp_w1, m_mlp_w2, v_norm_gains, v_mla_w_in, v_mla_q_norm, v_mla_kv_norm, v_mla_w_uq, v_mla_w_ukv, v_mla_w_o, v_hgrn_w_in, v_hgrn_lb_logits, v_hgrn_o_norm, v_hgrn_w_o, v_mlp_w1, v_mlp_w2):
    w = dict(norm_gains=norm_gains, mla_w_in=mla_w_in, mla_q_norm=mla_q_norm, mla_kv_norm=mla_kv_norm,
             mla_w_uq=mla_w_uq, mla_w_ukv=mla_w_ukv, mla_w_o=mla_w_o, hgrn_w_in=hgrn_w_in,
             hgrn_lb_logits=hgrn_lb_logits, hgrn_o_norm=hgrn_o_norm, hgrn_w_o=hgrn_w_o,
             mlp_w1=mlp_w1, mlp_w2=mlp_w2)
    mom_m = dict(norm_gains=m_norm_gains, mla_w_in=m_mla_w_in, mla_q_norm=m_mla_q_norm,
                 mla_kv_norm=m_mla_kv_norm, mla_w_uq=m_mla_w_uq, mla_w_ukv=m_mla_w_ukv,
                 mla_w_o=m_mla_w_o, hgrn_w_in=m_hgrn_w_in, hgrn_lb_logits=m_hgrn_lb_logits,
                 hgrn_o_norm=m_hgrn_o_norm, hgrn_w_o=m_hgrn_w_o, mlp_w1=m_mlp_w1, mlp_w2=m_mlp_w2)
    mom_v = dict(norm_gains=v_norm_gains, mla_w_in=v_mla_w_in, mla_q_norm=v_mla_q_norm,
                 mla_kv_norm=v_mla_kv_norm, mla_w_uq=v_mla_w_uq, mla_w_ukv=v_mla_w_ukv,
                 mla_w_o=v_mla_w_o, hgrn_w_in=v_hgrn_w_in, hgrn_lb_logits=v_hgrn_lb_logits,
                 hgrn_o_norm=v_hgrn_o_norm, hgrn_w_o=v_hgrn_w_o, mlp_w1=v_mlp_w1, mlp_w2=v_mlp_w2)
    c = lax.axis_index("c")

    axis_of = dict(SHARDED)
    me = 2 * lax.axis_index("x") + lax.axis_index("y")
    gain_bits = lax.bitcast_convert_type(norm_gains, jnp.uint32)
    gain_hi = lax.bitcast_convert_type((gain_bits >> 16).astype(jnp.uint16), BF16)
    gain_lo = lax.bitcast_convert_type((gain_bits & 0xFFFF).astype(jnp.uint16), BF16)

    layers = []
    for l in range(DEPTH):
        s = l // 2
        if l % 2 == 0:
            big = [("mlp_w1", l), ("mlp_w2", l), ("mla_w_o", s)]
            tail = [("mla_w_in", s), ("mla_w_uq", s), ("mla_w_ukv", s)]
        else:
            big = [("hgrn_w_in", s), ("mlp_w1", l), ("mlp_w2", l), ("hgrn_w_o", s)]
            tail = []
        w_tail = [w[n][i] for n, i in tail] + ([gain_hi, gain_lo] if l == 0 else [])
        g_tail = tail + ([("norm_gains", None)] + [(n, None) for n in REPLICATED] if l == 0 else [])
        g_shapes = [w[n].shape if i is None else w[n][i].shape for n, i in g_tail]
        tail_rows = max(_packed_misc_rows([t.shape for t in w_tail]), _packed_misc_rows(g_shapes))
        pk = _Packed([(n, w[n].shape[1]) for n, _ in big], tail_rows)
        wpack = jnp.zeros((pk.rows, PACK_W), BF16)
        for n, i in big:
            assert w[n].shape[2] == PACK_W
            wpack = _cast_into(w[n][i], wpack, pk.off[n], name="pack_%s_%d" % (n, l))
        if w_tail:
            wpack = lax.dynamic_update_slice(
                wpack, jnp.concatenate(_pack_blocks(w_tail, 0, BF16), axis=0), (pk.misc, 0))
        layers.append(dict(pk=pk, big=big, tail=tail, w_tail=w_tail, g_tail=g_tail, g_shapes=g_shapes,
                           gather=_gather_start(wpack, name="gather_start_%d" % l)))

    def fetch(l, after):
        lay = layers[l]
        pk = lay["pk"]
        send_sems, recv_sems, w_thru, land_thru, _ = lay["gather"]
        if after is None:
            after = sum(layers[k]["gather"][4] for k in range(1, DEPTH))
        w_back, land = _gather_wait(send_sems, recv_sems, w_thru, land_thru, after, name="gather_wait_%d" % l)
        land = _halves_to_sibling(land, name="gather_halves_%d" % l)
        land = lax.dynamic_update_slice(land, w_back[None], (me, 0, 0))
        out = dict(wbuf=land.reshape(N_CHIPS * pk.rows, PACK_W), pk=pk)
        if lay["w_tail"]:
            rows = _packed_misc_rows([t.shape for t in lay["w_tail"]])
            per_chip = [_unpack(land[j, pk.misc:pk.misc + rows], [t.shape for t in lay["w_tail"]])
                        for j in range(N_CHIPS)]
            for i, (n, _) in enumerate(lay["tail"]):
                out[n[4:]] = jnp.concatenate([per_chip[j][i] for j in range(N_CHIPS)], axis=axis_of[n] - 1)
            if l == 0:
                got_hi, got_lo = (lax.bitcast_convert_type(
                    jnp.concatenate([per_chip[j][i] for j in range(N_CHIPS)], axis=2),
                    jnp.uint16).astype(jnp.uint32) for i in (-2, -1))
                out["gains"] = lax.bitcast_convert_type((got_hi << 16) | got_lo, F32)
        return out

    def emit(l, gbuf, grads):
        lay = layers[l]
        pk = lay["pk"]
        if lay["g_tail"]:
            for j in range(N_CHIPS):
                pieces = []
                for n, i in lay["g_tail"]:
                    if n not in axis_of:
                        pieces.append(grads[n])
                    else:
                        pieces.append(jnp.split(grads[n], N_CHIPS, axis=axis_of[n] - (0 if i is None else 1))[j])
                block = jnp.concatenate(_pack_blocks(pieces, 0, BF16), axis=0)
                gbuf = lax.dynamic_update_slice(gbuf, block, (j * pk.rows + pk.misc, 0))
        row0 = lay.get("early_rows", 0)
        lay["scatter"] = _scatter_start(gbuf.reshape(N_CHIPS, pk.rows, PACK_W), row0, pk.rows - row0,
                                        name="scatter_start_%d" % l)
        return lay["scatter"][4][0, 0]

    def emit_mlp(l, gbuf):
        lay = layers[l]
        pk = lay["pk"]
        assert pk.off["mlp_w1"] == 0 and pk.off["mlp_w2"] == w["mlp_w1"].shape[1]
        lay["early_rows"] = w["mlp_w1"].shape[1] + w["mlp_w2"].shape[1]
        lay["scatter_early"] = _scatter_start(gbuf.reshape(N_CHIPS, pk.rows, PACK_W), 0, lay["early_rows"],
                                              name="scatter_start_%d_mlp" % l)
        return lay["scatter_early"][2].reshape(N_CHIPS * pk.rows, PACK_W)

    small = dict(mla_q_norm=mla_q_norm, mla_kv_norm=mla_kv_norm, hgrn_lb_logits=hgrn_lb_logits,
                 hgrn_o_norm=hgrn_o_norm)
    gbufs = [jnp.zeros((N_CHIPS * lay["pk"].rows, PACK_W), BF16) for lay in layers]
    sq, grad_x = _local_step(x[0], positions[0], loss_target[0], small, fetch, gbufs, emit, emit_mlp)
    d_model = x.shape[-1]
    loss = lax.psum(0.5 * jnp.sum(sq) / d_model, ("x", "y", "c"))

    per_name = {}
    behind = grad_x
    for l, lay in reversed(list(enumerate(layers))):
        pk = lay["pk"]
        send_sems, recv_sems, g_thru, land_thru, _ = lay["scatter"]
        row0 = lay.get("early_rows", 0)
        early = None
        if row0:
            e_send, e_recv, _, e_land, _ = lay["scatter_early"]
            g_thru, land = _scatter_wait(e_send, e_recv, g_thru, e_land, behind, name="scatter_wait_%d_mlp" % l)
            own = lax.dynamic_slice_in_dim(g_thru, me, 1, axis=0)
            land = lax.dynamic_update_slice(land, own[:, :row0], (me, 0, 0))
            early = behind = _sum_chips(land, name="grads_sum_chips_%d_mlp" % l, out_dtype=BF16)
        g_back, land = _scatter_wait(send_sems, recv_sems, g_thru, land_thru, behind, name="scatter_wait_%d" % l)
        own = lax.dynamic_slice_in_dim(g_back, me, 1, axis=0)
        land = lax.dynamic_update_slice(land, own[:, row0:], (me, 0, 0))
        mine = _sum_chips(land, name="grads_sum_chips_%d" % l, out_dtype=BF16)
        if early is not None:
            mine = jnp.concatenate([early, mine], axis=0)
        red = behind = _sum_chips(_share_reduced(mine, name="grads_share_%d" % l), name="grads_sum_cores_%d" % l)
        for n, i in lay["big"]:
            per_name.setdefault(n, {})[i] = red[pk.off[n]:pk.off[n] + w[n].shape[1]]
        for (n, i), piece in zip(lay["g_tail"], _unpack(red[pk.misc:pk.misc + pk.misc_rows], lay["g_shapes"])):
            per_name.setdefault(n, {})[i] = piece
    g_out = {n: (parts[None] if None in parts else jnp.stack([parts[i] for i in sorted(parts)]))
             for n, parts in per_name.items()}

    deltas, new_m, new_v = {}, {}, {}
    for name in WEIGHTS:
        deltas[name], new_m[name], new_v[name] = _adamw(w[name], g_out[name], mom_m[name], mom_v[name],
                                                        name="adamw_" + name)
    return (loss, grad_x[None], *[g_out[n] for n in WEIGHTS], *[deltas[n] for n in WEIGHTS],
            *[new_m[n] for n in WEIGHTS], *[new_v[n] for n in WEIGHTS])
```

```python
import functools

import jax
import jax.numpy as jnp
from jax import lax
from jax.experimental import pallas as pl
from jax.experimental.pallas import tpu as pltpu

F32 = jnp.float32
BF16 = jnp.bfloat16
MESH = pl.DeviceIdType.MESH

DEPTH = 4
MLA_HEADS = 8
MLA_NOPE = 128
MLA_ROPE = 64
MLA_V = 128
MLA_QK_PAD = 256
MLA_HEADS_PER_STEP = 2
MLA_SCALE = float(MLA_NOPE + MLA_ROPE) ** -0.5
ROPE_BASE = 10000.0
HGRN_HEADS = 8
HGRN_CHUNK = 32
HGRN_BLOCK = 128
EPS = 1e-6

ADAM_LR = 0.001
ADAM_B1 = 0.9
ADAM_B2 = 0.999
ADAM_EPS = 1e-08
ADAM_WD = 0.01
ADAM_STEP = 10

N_CHIPS = 4
PACK_W = 1024
PACK_ALIGN = 1024
PACK_TILE = 512
V7X_VMEM_LIMIT = 56 * 1024 * 1024

SHARDED = (("norm_gains", 2), ("mla_w_in", 1), ("mla_w_uq", 2), ("mla_w_ukv", 2), ("mla_w_o", 1),
           ("hgrn_w_in", 2), ("hgrn_w_o", 1), ("mlp_w1", 2), ("mlp_w2", 1))
REPLICATED = ("mla_q_norm", "mla_kv_norm", "hgrn_lb_logits", "hgrn_o_norm")
WEIGHTS = ("norm_gains", "mla_w_in", "mla_q_norm", "mla_kv_norm", "mla_w_uq", "mla_w_ukv", "mla_w_o",
           "hgrn_w_in", "hgrn_lb_logits", "hgrn_o_norm", "hgrn_w_o", "mlp_w1", "mlp_w2")


def _cparams(*semantics):
    return pltpu.CompilerParams(dimension_semantics=semantics, vmem_limit_bytes=V7X_VMEM_LIMIT)


def _sigmoid(x):
    return 1.0 / (1.0 + jnp.exp(-x))


def _mm(a, b, *, ta=False, tb=False, out_dtype=F32, tm=2048, tn=1024, tk=1024, epi=None, extra=None,
        name="mm", n=None, b_map=None, into=None, o_map=None):
    if ta:
        K, M = a.shape
    else:
        M, K = a.shape
    if b_map is not None:
        N = n
    elif tb:
        N, Kb = b.shape
    else:
        Kb, N = b.shape
    assert b_map is not None or K == Kb, (a.shape, b.shape, ta, tb)
    tm, tn = min(tm, M), min(tn, N)
    tk = K if (K <= 1024 and b_map is None) else min(tk, K)
    assert M % tm == 0 and N % tn == 0 and K % tk == 0, (M, N, K, tm, tn, tk)
    nk = K // tk
    a_spec = (pl.BlockSpec((tk, tm), lambda i, j, k: (k, i)) if ta
              else pl.BlockSpec((tm, tk), lambda i, j, k: (i, k)))
    if b_map is None:
        b_map = (lambda i, j, k: (j, k)) if tb else (lambda i, j, k: (k, j))
    b_spec = pl.BlockSpec((tn, tk) if tb else (tk, tn), b_map)
    o_spec = pl.BlockSpec((tm, tn), lambda i, j, k: (i, j))
    dims = (((0 if ta else 1,), (1 if tb else 0,)), ((), ()))
    in_specs = [a_spec, b_spec]
    operands = [a, b]
    aliases = {}
    if epi == "mul2r":
        in_specs.append(o_spec)
        operands.append(extra)
    if into is not None:
        assert epi is None
        in_specs.append(pl.BlockSpec(memory_space=pl.ANY))
        operands.append(into)
        aliases = {2: 0}
        out_dtype = into.dtype
        out_shape = jax.ShapeDtypeStruct(into.shape, into.dtype)
        out_specs = pl.BlockSpec((tm, tn), o_map)
    elif epi == "relu2":
        out_shape = (jax.ShapeDtypeStruct((M, N), BF16), jax.ShapeDtypeStruct((M, N), BF16))
        out_specs = (o_spec, o_spec)
    elif epi == "mul2r":
        out_shape = jax.ShapeDtypeStruct((M, N), BF16)
        out_specs = o_spec
    else:
        out_shape = jax.ShapeDtypeStruct((M, N), out_dtype)
        out_specs = o_spec
    n_in = len(operands)

    def body(*refs):
        a_ref, b_ref = refs[0], refs[1]
        outs = refs[n_in:n_in + (2 if epi == "relu2" else 1)]
        k = pl.program_id(2)

        def finish(acc):
            if epi == "relu2":
                r = jnp.maximum(acc, 0.0)
                outs[0][...] = (r * r).astype(BF16)
                outs[1][...] = r.astype(BF16)
            elif epi == "mul2r":
                outs[0][...] = (acc * (2.0 * refs[2][...].astype(F32))).astype(BF16)
            else:
                outs[0][...] = acc.astype(out_dtype)

        part = lax.dot_general(a_ref[...], b_ref[...], dims, preferred_element_type=F32)
        if nk == 1:
            finish(part)
            return
        acc_ref = refs[-1]

        @pl.when(k == 0)
        def _():
            acc_ref[...] = part

        @pl.when((k > 0) & (k < nk - 1))
        def _():
            acc_ref[...] += part

        @pl.when(k == nk - 1)
        def _():
            finish(acc_ref[...] + part)

    return pl.pallas_call(
        body, name=name, grid=(M // tm, N // tn, nk), in_specs=in_specs, out_specs=out_specs,
        out_shape=out_shape, scratch_shapes=[pltpu.VMEM((tm, tn), F32)] if nk > 1 else [],
        input_output_aliases=aliases,
        compiler_params=_cparams("parallel", "parallel", "arbitrary"))(*operands)


def _rms_rstd(x):
    return lax.rsqrt(jnp.mean(x * x, axis=-1, keepdims=True) + EPS)


def _rms_bwd_tile(x, g, dy):
    r = _rms_rstd(x)
    xh = x * r
    u = dy * g
    dx = r * (u - xh * jnp.mean(u * xh, axis=-1, keepdims=True))
    dg = jnp.sum(dy * xh, axis=0, keepdims=True)
    return dx, dg


def _row_tile(T):
    return min(256, T)


def _prenorm_fwd(x, g, name="prenorm_fwd"):
    T, D = x.shape
    tm = _row_tile(T)

    def body(x_ref, g_ref, a_ref):
        xv = x_ref[...]
        a_ref[...] = (xv * _rms_rstd(xv) * g_ref[...]).astype(BF16)

    row = pl.BlockSpec((tm, D), lambda i: (i, 0))
    vec = pl.BlockSpec((1, D), lambda i: (0, 0))
    return pl.pallas_call(body, name=name, grid=(T // tm,), in_specs=[row, vec], out_specs=row,
                          out_shape=jax.ShapeDtypeStruct((T, D), BF16),
                          compiler_params=_cparams("parallel"))(x, g)


def _resnorm_fwd(h, z, g_post, g_pre, name="resnorm_fwd"):
    T, D = h.shape
    tm = _row_tile(T)

    def body(h_ref, z_ref, gp_ref, gn_ref, hn_ref, a_ref):
        zv = z_ref[...]
        hn = h_ref[...] + zv * _rms_rstd(zv) * gp_ref[...]
        hn_ref[...] = hn
        a_ref[...] = (hn * _rms_rstd(hn) * gn_ref[...]).astype(BF16)

    row = pl.BlockSpec((tm, D), lambda i: (i, 0))
    vec = pl.BlockSpec((1, D), lambda i: (0, 0))
    return pl.pallas_call(body, name=name, grid=(T // tm,), in_specs=[row, row, vec, vec],
                          out_specs=(row, row),
                          out_shape=(jax.ShapeDtypeStruct((T, D), F32), jax.ShapeDtypeStruct((T, D), BF16)),
                          compiler_params=_cparams("parallel"))(h, z, g_post, g_pre)


def _resnorm_loss(h, z, g_post, target, name="resnorm_loss"):
    T, D = h.shape
    tm = _row_tile(T)

    def body(h_ref, z_ref, gp_ref, t_ref, dy_ref, sq_ref):
        zv = z_ref[...]
        err = h_ref[...] + zv * _rms_rstd(zv) * gp_ref[...] - t_ref[...]
        dy_ref[...] = err * (1.0 / D)

        @pl.when(pl.program_id(0) == 0)
        def _():
            sq_ref[...] = jnp.zeros_like(sq_ref)

        sq_ref[...] += jnp.sum(err * err, axis=0, keepdims=True)

    row = pl.BlockSpec((tm, D), lambda i: (i, 0))
    vec = pl.BlockSpec((1, D), lambda i: (0, 0))
    return pl.pallas_call(body, name=name, grid=(T // tm,), in_specs=[row, row, vec, row],
                          out_specs=(row, vec),
                          out_shape=(jax.ShapeDtypeStruct((T, D), F32), jax.ShapeDtypeStruct((1, D), F32)),
                          compiler_params=_cparams("arbitrary"))(h, z, g_post, target)


def _resnorm_bwd(z, g_post, dh, h_new=None, da=None, g_pre=None, name="resnorm_bwd"):
    T, D = z.shape
    tm = _row_tile(T)
    has_next = h_new is not None
    row = pl.BlockSpec((tm, D), lambda i: (i, 0))
    vec = pl.BlockSpec((1, D), lambda i: (0, 0))

    if has_next:
        def body(z_ref, gp_ref, dh_ref, hn_ref, da_ref, gn_ref, t_ref, dz_ref, dgp_ref, dgn_ref):
            first = pl.program_id(0) == 0

            @pl.when(first)
            def _():
                dgp_ref[...] = jnp.zeros_like(dgp_ref)
                dgn_ref[...] = jnp.zeros_like(dgn_ref)

            dpre, dgn = _rms_bwd_tile(hn_ref[...], gn_ref[...], da_ref[...])
            t = dh_ref[...] + dpre
            t_ref[...] = t
            dz, dgp = _rms_bwd_tile(z_ref[...], gp_ref[...], t)
            dz_ref[...] = dz.astype(BF16)
            dgp_ref[...] += dgp
            dgn_ref[...] += dgn

        return pl.pallas_call(
            body, name=name, grid=(T // tm,), in_specs=[row, vec, row, row, row, vec],
            out_specs=(row, row, vec, vec),
            out_shape=(jax.ShapeDtypeStruct((T, D), F32), jax.ShapeDtypeStruct((T, D), BF16),
                       jax.ShapeDtypeStruct((1, D), F32), jax.ShapeDtypeStruct((1, D), F32)),
            compiler_params=_cparams("arbitrary"))(z, g_post, dh, h_new, da, g_pre)

    def body_last(z_ref, gp_ref, dh_ref, dz_ref, dgp_ref):
        @pl.when(pl.program_id(0) == 0)
        def _():
            dgp_ref[...] = jnp.zeros_like(dgp_ref)

        dz, dgp = _rms_bwd_tile(z_ref[...], gp_ref[...], dh_ref[...])
        dz_ref[...] = dz.astype(BF16)
        dgp_ref[...] += dgp

    return pl.pallas_call(
        body_last, name=name, grid=(T // tm,), in_specs=[row, vec, row], out_specs=(row, vec),
        out_shape=(jax.ShapeDtypeStruct((T, D), BF16), jax.ShapeDtypeStruct((1, D), F32)),
        compiler_params=_cparams("arbitrary"))(z, g_post, dh)


def _prenorm_bwd(x, g, dh, da, name="prenorm_bwd"):
    T, D = x.shape
    tm = _row_tile(T)

    def body(x_ref, g_ref, dh_ref, da_ref, dx_ref, dg_ref):
        @pl.when(pl.program_id(0) == 0)
        def _():
            dg_ref[...] = jnp.zeros_like(dg_ref)

        dpre, dg = _rms_bwd_tile(x_ref[...], g_ref[...], da_ref[...])
        dx_ref[...] = dh_ref[...] + dpre
        dg_ref[...] += dg

    row = pl.BlockSpec((tm, D), lambda i: (i, 0))
    vec = pl.BlockSpec((1, D), lambda i: (0, 0))
    return pl.pallas_call(
        body, name=name, grid=(T // tm,), in_specs=[row, vec, row, row], out_specs=(row, vec),
        out_shape=(jax.ShapeDtypeStruct((T, D), F32), jax.ShapeDtypeStruct((1, D), F32)),
        compiler_params=_cparams("arbitrary"))(x, g, dh, da)


class _Packed:
    def __init__(self, big, misc_rows):
        self.big = tuple(big)
        self.off = {}
        r = 0
        for name, rows in big:
            self.off[name] = r
            r += rows
        self.misc, self.misc_rows = r, misc_rows
        self.rows = -(-(r + misc_rows) // PACK_ALIGN) * PACK_ALIGN

    def block(self, name, layer, unit):
        r = self.off[name]
        assert r % unit == 0 and self.rows % unit == 0
        return r // unit, self.rows // unit


def _col_sharded(pk, name, layer, unit):
    base, stride = pk.block(name, layer, unit)
    return (lambda i, j, k: (j * stride + base, 0)), (lambda i, j, k: (k * stride + base, 0))


def _row_sharded(pk, name, layer, unit):
    base, stride = pk.block(name, layer, unit)
    return ((lambda i, j, k: (k * stride + base, 0)), (lambda i, j, k: (j * stride + base, 0)),
            (lambda i, j, k: (i * stride + base, 0)))


def _mlp_fwd(a, wbuf, pk, layer):
    D = a.shape[1]
    by_n, _ = _col_sharded(pk, "mlp_w1", layer, D)
    by_k, _, _ = _row_sharded(pk, "mlp_w2", layer, D)
    act, r = _mm(a, wbuf, n=4 * D, b_map=by_n, tk=D, tn=D, epi="relu2", name="mlp_up")
    u = _mm(act, wbuf, n=D, b_map=by_k, tk=D, tn=D, name="mlp_down")
    return u, (a, act, r)


def _mlp_bwd(du, saved, wbuf, gbuf, pk, layer):
    a, act, r = saved
    D = a.shape[1]
    w1_by_n, w1_by_k = _col_sharded(pk, "mlp_w1", layer, D)
    _, w2_by_n, w2_by_m = _row_sharded(pk, "mlp_w2", layer, D)
    dz1 = _mm(du, wbuf, tb=True, n=4 * D, b_map=w2_by_n, tn=D, tk=D, epi="mul2r", extra=r, name="mlp_down_dx")
    gbuf = _mm(act, du, ta=True, into=gbuf, o_map=w2_by_m, tm=D, tn=D, name="mlp_down_dw")
    gbuf = _mm(a, dz1, ta=True, into=gbuf, o_map=w1_by_n, tm=D, tn=D, name="mlp_up_dw")
    da = _mm(dz1, wbuf, tb=True, n=D, b_map=w1_by_k, tn=D, tk=D, name="mlp_up_dx")
    return da, gbuf


def _rope_swap(t):
    n = t.shape[-1]
    lane = lax.broadcasted_iota(jnp.int32, t.shape, t.ndim - 1)
    half = MLA_ROPE // 2
    first = (lane & (MLA_ROPE - 1)) < half
    return jnp.where(first, pltpu.roll(t, n - half, t.ndim - 1), pltpu.roll(t, half, t.ndim - 1))


def _mla_mid_fwd(proj, q_norm, kv_norm, w_uq, w_ukv, cc, ss):
    T, PW = proj.shape
    QL, KVL = q_norm.shape[-1], kv_norm.shape[-1]
    H = MLA_HEADS
    assert PW == QL + KVL + 128
    tm = _row_tile(T)

    def body(p_ref, qn_ref, kn_ref, wq_ref, wkv_ref, cc_ref, ss_ref,
             cq_ref, ckv_ref, q_ref, k_ref, v_ref):
        cq = p_ref[:, 0:QL]
        ckv = p_ref[:, QL:QL + KVL]
        kr = p_ref[:, QL + KVL:QL + KVL + 128]
        c, s = cc_ref[...], ss_ref[...]
        cqn = (cq * _rms_rstd(cq) * qn_ref[...]).astype(BF16)
        ckvn = (ckv * _rms_rstd(ckv) * kn_ref[...]).astype(BF16)
        cq_ref[...] = cqn
        ckv_ref[...] = ckvn
        q = jnp.dot(cqn, wq_ref[...], preferred_element_type=F32)
        kv = jnp.dot(ckvn, wkv_ref[...], preferred_element_type=F32)
        krf = (kr * c + _rope_swap(kr) * s).astype(BF16)
        for h in range(H):
            o = h * MLA_QK_PAD
            q_ref[:, o:o + MLA_NOPE] = (q[:, o:o + MLA_NOPE] * MLA_SCALE).astype(BF16)
            qr = q[:, o + MLA_NOPE:o + MLA_QK_PAD]
            q_ref[:, o + MLA_NOPE:o + MLA_QK_PAD] = ((qr * c + _rope_swap(qr) * s) * MLA_SCALE).astype(BF16)
            k_ref[:, o:o + MLA_NOPE] = kv[:, o:o + MLA_NOPE].astype(BF16)
            k_ref[:, o + MLA_NOPE:o + MLA_QK_PAD] = krf
            v_ref[:, h * MLA_V:(h + 1) * MLA_V] = kv[:, o + MLA_NOPE:o + MLA_QK_PAD].astype(BF16)

    def row(w):
        return pl.BlockSpec((tm, w), lambda i: (i, 0))

    def full(shape):
        return pl.BlockSpec(shape, lambda i: (0, 0))

    return pl.pallas_call(
        body, name="mla_mid_fwd", grid=(T // tm,),
        in_specs=[row(PW), full((1, QL)), full((1, KVL)), full(w_uq.shape), full(w_ukv.shape),
                  row(128), row(128)],
        out_specs=(row(QL), row(KVL), row(H * MLA_QK_PAD), row(H * MLA_QK_PAD), row(H * MLA_V)),
        out_shape=(jax.ShapeDtypeStruct((T, QL), BF16), jax.ShapeDtypeStruct((T, KVL), BF16),
                   jax.ShapeDtypeStruct((T, H * MLA_QK_PAD), BF16),
                   jax.ShapeDtypeStruct((T, H * MLA_QK_PAD), BF16),
                   jax.ShapeDtypeStruct((T, H * MLA_V), BF16)),
        compiler_params=_cparams("parallel"))(proj, q_norm, kv_norm, w_uq, w_ukv, cc, ss)


def _mla_mid_bwd(proj, q_norm, kv_norm, w_uq, w_ukv, cc, ss, dq, dk, dv):
    T, PW = proj.shape
    QL, KVL = q_norm.shape[-1], kv_norm.shape[-1]
    H = MLA_HEADS
    tm = _row_tile(T)
    nt = (((1,), (1,)), ((), ()))

    def body(p_ref, qn_ref, kn_ref, wq_ref, wkv_ref, cc_ref, ss_ref, dq_ref, dk_ref, dv_ref,
             dqp_ref, dkv_ref, dp_ref, dqn_ref, dkn_ref):
        @pl.when(pl.program_id(0) == 0)
        def _():
            dqn_ref[...] = jnp.zeros_like(dqn_ref)
            dkn_ref[...] = jnp.zeros_like(dkn_ref)

        c, s = cc_ref[...], ss_ref[...]
        dkr = jnp.zeros((tm, 128), F32)
        for h in range(H):
            o = h * MLA_QK_PAD
            dqp_ref[:, o:o + MLA_NOPE] = (dq_ref[:, o:o + MLA_NOPE] * MLA_SCALE).astype(BF16)
            dqr = dq_ref[:, o + MLA_NOPE:o + MLA_QK_PAD] * MLA_SCALE
            dqp_ref[:, o + MLA_NOPE:o + MLA_QK_PAD] = (dqr * c + _rope_swap(dqr * s)).astype(BF16)
            dkv_ref[:, o:o + MLA_NOPE] = dk_ref[:, o:o + MLA_NOPE].astype(BF16)
            dkv_ref[:, o + MLA_NOPE:o + MLA_QK_PAD] = dv_ref[:, h * MLA_V:(h + 1) * MLA_V].astype(BF16)
            dkr = dkr + dk_ref[:, o + MLA_NOPE:o + MLA_QK_PAD]
        dcqn = lax.dot_general(dqp_ref[...], wq_ref[...], nt, preferred_element_type=F32)
        dckvn = lax.dot_general(dkv_ref[...], wkv_ref[...], nt, preferred_element_type=F32)
        dcq, dqn = _rms_bwd_tile(p_ref[:, 0:QL], qn_ref[...], dcqn)
        dckv, dkn = _rms_bwd_tile(p_ref[:, QL:QL + KVL], kn_ref[...], dckvn)
        dp_ref[:, 0:QL] = dcq.astype(BF16)
        dp_ref[:, QL:QL + KVL] = dckv.astype(BF16)
        dp_ref[:, QL + KVL:QL + KVL + 128] = (dkr * c + _rope_swap(dkr * s)).astype(BF16)
        dqn_ref[...] += dqn
        dkn_ref[...] += dkn

    def row(w):
        return pl.BlockSpec((tm, w), lambda i: (i, 0))

    def full(shape):
        return pl.BlockSpec(shape, lambda i: (0, 0))

    return pl.pallas_call(
        body, name="mla_mid_bwd", grid=(T // tm,),
        in_specs=[row(PW), full((1, QL)), full((1, KVL)), full(w_uq.shape), full(w_ukv.shape),
                  row(128), row(128), row(H * MLA_QK_PAD), row(H * MLA_QK_PAD), row(H * MLA_V)],
        out_specs=(row(H * MLA_QK_PAD), row(H * MLA_QK_PAD), row(PW), full((1, QL)), full((1, KVL))),
        out_shape=(jax.ShapeDtypeStruct((T, H * MLA_QK_PAD), BF16),
                   jax.ShapeDtypeStruct((T, H * MLA_QK_PAD), BF16),
                   jax.ShapeDtypeStruct((T, PW), BF16),
                   jax.ShapeDtypeStruct((1, QL), F32), jax.ShapeDtypeStruct((1, KVL), F32)),
        compiler_params=_cparams("arbitrary"))(proj, q_norm, kv_norm, w_uq, w_ukv, cc, ss, dq, dk, dv)


def _attn_tile(T):
    return min(1024, T)


def _attn_pairs(n, by_key):
    if by_key:
        pairs = [(qi, ki) for ki in range(n) for qi in range(ki, n)]
    else:
        pairs = [(qi, ki) for qi in range(n) for ki in range(qi + 1)]
    return (jnp.asarray([p[0] for p in pairs], jnp.int32), jnp.asarray([p[1] for p in pairs], jnp.int32))


def _scores(q, k, diagonal):
    s = lax.dot_general(q, k, (((1,), (1,)), ((), ())), preferred_element_type=F32)
    if diagonal:
        rows = lax.broadcasted_iota(jnp.int32, s.shape, 0)
        cols = lax.broadcasted_iota(jnp.int32, s.shape, 1)
        s = jnp.where(rows >= cols, s, -jnp.inf)
    return s


def _attn_fwd(q, k, v):
    T = q.shape[0]
    H, DQ, DV = MLA_HEADS, MLA_QK_PAD, MLA_V
    tq = _attn_tile(T)
    nq = T // tq
    scale = float(MLA_NOPE + MLA_ROPE) ** -0.5
    G = MLA_HEADS_PER_STEP
    qi_tab, ki_tab = _attn_pairs(nq, by_key=False)

    def body(qi_ref, ki_ref, q_ref, k_ref, v_ref, o_ref, lse_ref, *scratch):
        m_refs, l_refs, acc_refs = scratch[0:G], scratch[G:2 * G], scratch[2 * G:3 * G]
        p = pl.program_id(1)
        qi, ki = qi_ref[p], ki_ref[p]

        @pl.when(ki == 0)
        def _():
            for g in range(G):
                m_refs[g][...] = jnp.full_like(m_refs[g], -jnp.inf)
                l_refs[g][...] = jnp.zeros_like(l_refs[g])
                acc_refs[g][...] = jnp.zeros_like(acc_refs[g])

        def update(ks, qr, masked):
            for g in range(G):
                qs, vs = slice(g * DQ, (g + 1) * DQ), slice(g * DV, (g + 1) * DV)
                st = _scores(k_ref[ks, qs], q_ref[qr, qs], False)
                if masked:
                    key = ks.start + lax.broadcasted_iota(jnp.int32, st.shape, 0)
                    qry = qr.start + lax.broadcasted_iota(jnp.int32, st.shape, 1)
                    st = jnp.where(qry >= key, st, -jnp.inf)
                m_prev = m_refs[g][:, qr]
                m_new = jnp.maximum(m_prev, jnp.max(st, axis=0, keepdims=True))
                alpha = jnp.exp(m_prev - m_new)
                pt = jnp.exp(st - m_new)
                l_refs[g][:, qr] = alpha * l_refs[g][:, qr] + jnp.sum(pt, axis=0, keepdims=True)
                acc_refs[g][:, qr] = alpha * acc_refs[g][:, qr] + lax.dot_general(
                    v_ref[ks, vs], pt.astype(BF16), (((0,), (0,)), ((), ())), preferred_element_type=F32)
                m_refs[g][:, qr] = m_new

        whole, half = slice(0, tq), tq // 2

        @pl.when(ki < qi)
        def _():
            update(whole, whole, False)

        @pl.when(ki == qi)
        def _():
            update(slice(0, half), whole, True)
            update(slice(half, tq), slice(half, tq), True)
            for g in range(G):
                vs = slice(g * DV, (g + 1) * DV)
                o_ref[:, vs] = jnp.transpose(acc_refs[g][...] / l_refs[g][...]).astype(BF16)
                lse_ref[g] = m_refs[g][...] + jnp.log(l_refs[g][...])

    return pl.pallas_call(
        body, name="attn_fwd",
        grid_spec=pltpu.PrefetchScalarGridSpec(
            num_scalar_prefetch=2, grid=(H // G, int(qi_tab.shape[0])),
            in_specs=[pl.BlockSpec((tq, G * DQ), lambda h, p, qt, kt: (qt[p], h)),
                      pl.BlockSpec((tq, G * DQ), lambda h, p, qt, kt: (kt[p], h)),
                      pl.BlockSpec((tq, G * DV), lambda h, p, qt, kt: (kt[p], h))],
            out_specs=(pl.BlockSpec((tq, G * DV), lambda h, p, qt, kt: (qt[p], h)),
                       pl.BlockSpec((G, 1, tq), lambda h, p, qt, kt: (h, 0, qt[p]))),
            scratch_shapes=([pltpu.VMEM((1, tq), F32)] * (2 * G) + [pltpu.VMEM((DV, tq), F32)] * G)),
        out_shape=(jax.ShapeDtypeStruct((T, H * DV), BF16), jax.ShapeDtypeStruct((H, 1, T), F32)),
        compiler_params=_cparams("parallel", "arbitrary"))(qi_tab, ki_tab, q, k, v)


def _attn_bwd(q, k, v, o, do, lse):
    T = q.shape[0]
    H, DQ, DV = MLA_HEADS, MLA_QK_PAD, MLA_V
    tq = _attn_tile(T)
    nq = T // tq
    scale = float(MLA_NOPE + MLA_ROPE) ** -0.5
    tn = (((0,), (0,)), ((), ()))
    nt = (((1,), (1,)), ((), ()))
    G = MLA_HEADS_PER_STEP
    qi_tab, ki_tab = _attn_pairs(nq, by_key=True)

    def body(qi_ref, ki_ref, q_ref, k_ref, v_ref, o_ref, do_ref, lse_ref, dq_ref, dk_ref, dv_ref,
             dk_acc, dv_acc):
        p = pl.program_id(1)
        qi, ki = qi_ref[p], ki_ref[p]

        @pl.when(p == 0)
        def _():
            dq_ref[...] = jnp.zeros_like(dq_ref)

        @pl.when(qi == ki)
        def _():
            dk_acc[...] = jnp.zeros_like(dk_acc)
            dv_acc[...] = jnp.zeros_like(dv_acc)

        def step(ks, qr, masked):
            rows = pl.ds(pl.multiple_of(qi * tq + qr.start, qr.stop - qr.start), qr.stop - qr.start)
            for g in range(G):
                qs, vs = slice(g * DQ, (g + 1) * DQ), slice(g * DV, (g + 1) * DV)
                dof = do_ref[qr, vs]
                delta = jnp.sum(jnp.transpose(dof.astype(F32) * o_ref[qr, vs].astype(F32)), axis=0,
                                keepdims=True)
                st = _scores(k_ref[ks, qs], q_ref[qr, qs], False)
                if masked:
                    key = ks.start + lax.broadcasted_iota(jnp.int32, st.shape, 0)
                    qry = qr.start + lax.broadcasted_iota(jnp.int32, st.shape, 1)
                    st = jnp.where(qry >= key, st, -jnp.inf)
                pt = jnp.exp(st - lse_ref[g][:, qr])
                dpt = lax.dot_general(v_ref[ks, vs], dof, nt, preferred_element_type=F32)
                dst = (pt * (dpt - delta)).astype(BF16)
                dv_acc[ks, vs] += jnp.dot(pt.astype(BF16), dof, preferred_element_type=F32)
                dk_acc[ks, qs] += jnp.dot(dst, q_ref[qr, qs], preferred_element_type=F32)
                dq_ref[rows, qs] += lax.dot_general(dst, k_ref[ks, qs], tn, preferred_element_type=F32)

        whole, half = slice(0, tq), tq // 2

        @pl.when(qi == ki)
        def _():
            step(slice(0, half), whole, True)
            step(slice(half, tq), slice(half, tq), True)

        @pl.when(qi > ki)
        def _():
            step(whole, whole, False)

        @pl.when(qi == nq - 1)
        def _():
            dk_ref[...] = dk_acc[...]
            dv_ref[...] = dv_acc[...]

    qspec = pl.BlockSpec((tq, G * DQ), lambda h, p, qt, kt: (qt[p], h))
    ospec = pl.BlockSpec((tq, G * DV), lambda h, p, qt, kt: (qt[p], h))
    kspec = pl.BlockSpec((tq, G * DQ), lambda h, p, qt, kt: (kt[p], h))
    vspec = pl.BlockSpec((tq, G * DV), lambda h, p, qt, kt: (kt[p], h))
    return pl.pallas_call(
        body, name="attn_bwd",
        grid_spec=pltpu.PrefetchScalarGridSpec(
            num_scalar_prefetch=2, grid=(H // G, int(qi_tab.shape[0])),
            in_specs=[qspec, kspec, vspec, ospec, ospec,
                      pl.BlockSpec((G, 1, tq), lambda h, p, qt, kt: (h, 0, qt[p]))],
            out_specs=(pl.BlockSpec((T, G * DQ), lambda h, p, qt, kt: (0, h)), kspec, vspec),
            scratch_shapes=[pltpu.VMEM((tq, G * DQ), F32), pltpu.VMEM((tq, G * DV), F32)]),
        out_shape=(jax.ShapeDtypeStruct((T, H * DQ), F32), jax.ShapeDtypeStruct((T, H * DQ), F32),
                   jax.ShapeDtypeStruct((T, H * DV), F32)),
        compiler_params=_cparams("parallel", "arbitrary"))(qi_tab, ki_tab, q, k, v, o, do, lse)


def _mla_fwd(a, w, cc, ss, wbuf, pk, slot):
    D = a.shape[1]
    by_k, _, _ = _row_sharded(pk, "mla_w_o", slot, D // N_CHIPS)
    proj = _mm(a, w["w_in"], name="mla_in")
    cqn, ckvn, q, k, v = _mla_mid_fwd(proj, w["q_norm"], w["kv_norm"], w["w_uq"], w["w_ukv"], cc, ss)
    o, lse = _attn_fwd(q, k, v)
    m = _mm(o, wbuf, n=D, b_map=by_k, tm=2048, tk=D // N_CHIPS, tn=D, name="mla_out")
    return m, (a, proj, cqn, ckvn, q, k, v, o, lse)


def _mla_bwd(dm, saved, w, cc, ss, wbuf, gbuf, pk, slot):
    a, proj, cqn, ckvn, q, k, v, o, lse = saved
    D = a.shape[1]
    _, by_n, by_m = _row_sharded(pk, "mla_w_o", slot, D // N_CHIPS)
    do = _mm(dm, wbuf, tb=True, n=o.shape[1], b_map=by_n, tm=2048, tn=D // N_CHIPS, tk=D, out_dtype=BF16,
             name="mla_out_dx")
    gbuf = _mm(o, dm, ta=True, into=gbuf, o_map=by_m, tm=D // N_CHIPS, tn=D, tk=2048, name="mla_out_dw")
    dq, dk, dv = _attn_bwd(q, k, v, o, do, lse)
    dqp, dkv, dproj, dqn, dkn = _mla_mid_bwd(proj, w["q_norm"], w["kv_norm"], w["w_uq"], w["w_ukv"],
                                             cc, ss, dq, dk, dv)
    dw_uq = _mm(cqn, dqp, ta=True, out_dtype=BF16, name="mla_uq_dw")
    dw_ukv = _mm(ckvn, dkv, ta=True, out_dtype=BF16, name="mla_ukv_dw")
    dw_in = _mm(a, dproj, ta=True, out_dtype=BF16, name="mla_in_dw")
    da = _mm(dproj, w["w_in"], tb=True, name="mla_in_dx")
    return da, gbuf, dict(w_in=dw_in, w_uq=dw_uq, w_ukv=dw_ukv, q_norm=dqn, kv_norm=dkn)


def _split_dot(mat, x, parts):
    acc = None
    rem = x
    for _ in range(parts):
        piece = rem.astype(BF16)
        term = jnp.dot(mat, piece, preferred_element_type=F32)
        acc = term if acc is None else acc + term
        rem = rem - piece.astype(F32)
    return acc


def _chunk_sums(cum, rel, rest, logf):
    return tuple(_split_dot(m.astype(BF16), logf, 3) for m in (cum, rel, rest))


def _chunk_mats(tb):
    C = HGRN_CHUNK
    assert C & (C - 1) == 0
    r = lax.broadcasted_iota(jnp.int32, (tb, tb), 0)
    s = lax.broadcasted_iota(jnp.int32, (tb, tb), 1)
    start = r & ~(C - 1)
    same = start == (s & ~(C - 1))
    ref = start + C // 2
    last = start + C - 1
    one, zero = jnp.float32(1.0), jnp.float32(0.0)
    cum = jnp.where(same & (s <= r), one, zero)
    rel = cum - jnp.where(same & (s <= ref), one, zero)
    rest = jnp.where(same & (s > r) & (s <= last), one, zero)
    rev = jnp.where(same & (s >= r), one, zero)
    ones = jnp.where(same, one, zero)
    causal = same & (s <= r)
    return cum, rel, rest, rev, ones, causal


def _hgrn_gates(p_ref, lb, HK):
    qx = p_ref[:, 0:HK]
    fx = p_ref[:, HK:2 * HK]
    sf = _sigmoid(fx)
    f = lb + (1.0 - lb) * sf
    sq = _sigmoid(qx)
    return qx, sq, qx * sq, sf, f, 1.0 - f, jnp.log(f)


def _hgrn_fwd(proj, lb, o_norm):
    T = proj.shape[0]
    H, C = HGRN_HEADS, HGRN_CHUNK
    HK = proj.shape[1] // 4
    DK = HK // H
    tb = min(HGRN_BLOCK, T)
    ncb = tb // C
    nt = (((1,), (1,)), ((), ()))
    tn = (((0,), (0,)), ((), ()))

    def body(p_ref, lb_ref, on_ref, y_ref, o_ref, st_ref, state, oacc):
        @pl.when(pl.program_id(0) == 0)
        def _():
            state[...] = jnp.zeros_like(state)

        cum, rel, rest, _, _, causal = _chunk_mats(tb)
        _, _, q, _, f, k, logf = _hgrn_gates(p_ref, lb_ref[...], HK)
        b, brel, brest = _chunk_sums(cum, rel, rest, logf)
        eb = jnp.exp(b)
        q_rel = (q * jnp.exp(brel)).astype(BF16)
        k_rel = (k * jnp.exp(-brel)).astype(BF16)
        q_dec = (q * eb).astype(BF16)
        k_dec = (k * jnp.exp(brest)).astype(BF16)
        v = p_ref[:, 2 * HK:3 * HK].astype(BF16)
        for h in range(H):
            hs = slice(h * DK, (h + 1) * DK)
            a = lax.dot_general(q_rel[:, hs], k_rel[:, hs], nt, preferred_element_type=F32)
            a = jnp.where(causal, a, 0.0).astype(BF16)
            oacc[:, hs] = jnp.dot(a, v[:, hs], preferred_element_type=F32)
            for j in range(ncb):
                rs = slice(j * C, (j + 1) * C)
                st = state[h]
                st_ref[j, h] = st
                oacc[rs, hs] += lax.dot_general(q_dec[rs, hs], st.astype(BF16), nt,
                                                preferred_element_type=F32)
                dec = jnp.exp(jnp.sum(logf[rs, hs], axis=0, keepdims=True))
                state[h] = dec * st + lax.dot_general(v[rs, hs], k_dec[rs, hs], tn,
                                                      preferred_element_type=F32)
        o = oacc[...]
        o_ref[...] = o
        gx = p_ref[:, 3 * HK:4 * HK]
        gate = gx * _sigmoid(gx)
        for h in range(H):
            hs = slice(h * DK, (h + 1) * DK)
            oh = o[:, hs]
            y_ref[:, hs] = (oh * _rms_rstd(oh) * on_ref[...] * gate[:, hs]).astype(BF16)

    return pl.pallas_call(
        body, name="hgrn_fwd", grid=(T // tb,),
        in_specs=[pl.BlockSpec((tb, 4 * HK), lambda i: (i, 0)),
                  pl.BlockSpec((1, HK), lambda i: (0, 0)),
                  pl.BlockSpec((1, DK), lambda i: (0, 0))],
        out_specs=(pl.BlockSpec((tb, HK), lambda i: (i, 0)),
                   pl.BlockSpec((tb, HK), lambda i: (i, 0)),
                   pl.BlockSpec((ncb, H, DK, DK), lambda i: (i, 0, 0, 0))),
        out_shape=(jax.ShapeDtypeStruct((T, HK), BF16), jax.ShapeDtypeStruct((T, HK), F32),
                   jax.ShapeDtypeStruct((T // C, H, DK, DK), F32)),
        scratch_shapes=[pltpu.VMEM((H, DK, DK), F32), pltpu.VMEM((tb, HK), F32)],
        compiler_params=_cparams("arbitrary"))(proj, lb, o_norm)


def _hgrn_bwd(proj, lb, o_norm, o, states, dy):
    T = proj.shape[0]
    H, C = HGRN_HEADS, HGRN_CHUNK
    HK = proj.shape[1] // 4
    DK = HK // H
    tb = min(HGRN_BLOCK, T)
    ncb = tb // C
    nb = T // tb
    nt = (((1,), (1,)), ((), ()))
    tn = (((0,), (0,)), ((), ()))

    def body(p_ref, lb_ref, on_ref, o_ref, st_ref, dy_ref, dp_ref, dlb_ref, don_ref,
             dstate, dqr_s, dkr_s, dqd_s, dkd_s, dv_s, do_s, e_s):
        @pl.when(pl.program_id(0) == 0)
        def _():
            dstate[...] = jnp.zeros_like(dstate)
            dlb_ref[...] = jnp.zeros_like(dlb_ref)
            don_ref[...] = jnp.zeros_like(don_ref)

        cum, rel, rest, rev, ones, causal = _chunk_mats(tb)
        lb = lb_ref[...]
        qx, sq, q, sf, f, k, logf = _hgrn_gates(p_ref, lb, HK)
        b, brel, brest = _chunk_sums(cum, rel, rest, logf)
        eb = jnp.exp(b)
        erel = jnp.exp(brel)
        enrel = jnp.exp(-brel)
        erest = jnp.exp(brest)
        q_rel_f, k_rel_f, q_dec_f, k_dec_f = q * erel, k * enrel, q * eb, k * erest
        q_rel, k_rel = q_rel_f.astype(BF16), k_rel_f.astype(BF16)
        q_dec, k_dec = q_dec_f.astype(BF16), k_dec_f.astype(BF16)
        v = p_ref[:, 2 * HK:3 * HK].astype(BF16)

        gx = p_ref[:, 3 * HK:4 * HK]
        sg = _sigmoid(gx)
        gate = gx * sg
        dy = dy_ref[...]
        ov = o_ref[...]
        on = on_ref[...]
        don = jnp.zeros((1, DK), F32)
        for h in range(H):
            hs = slice(h * DK, (h + 1) * DK)
            oh = ov[:, hs]
            r = _rms_rstd(oh)
            xh = oh * r
            d_on = dy[:, hs] * gate[:, hs]
            don = don + jnp.sum(d_on * xh, axis=0, keepdims=True)
            u = d_on * on
            do_s[:, hs] = r * (u - xh * jnp.mean(u * xh, axis=-1, keepdims=True))
            dp_ref[:, 3 * HK + h * DK:3 * HK + (h + 1) * DK] = (
                dy[:, hs] * xh * on * (sg[:, hs] * (1.0 + gx[:, hs] * (1.0 - sg[:, hs])))).astype(BF16)
        don_ref[...] += don

        for h in range(H):
            hs = slice(h * DK, (h + 1) * DK)
            doh = do_s[:, hs].astype(BF16)
            a = lax.dot_general(q_rel[:, hs], k_rel[:, hs], nt, preferred_element_type=F32)
            a = jnp.where(causal, a, 0.0).astype(BF16)
            da = lax.dot_general(doh, v[:, hs], nt, preferred_element_type=F32)
            da = jnp.where(causal, da, 0.0).astype(BF16)
            dv_s[:, hs] = lax.dot_general(a, doh, tn, preferred_element_type=F32)
            dqr_s[:, hs] = jnp.dot(da, k_rel[:, hs], preferred_element_type=F32)
            dkr_s[:, hs] = lax.dot_general(da, q_rel[:, hs], tn, preferred_element_type=F32)
            for j in reversed(range(ncb)):
                rs = slice(j * C, (j + 1) * C)
                dst = dstate[h]
                dstb = dst.astype(BF16)
                st = st_ref[j, h]
                dkd_s[rs, hs] = jnp.dot(v[rs, hs], dstb, preferred_element_type=F32)
                dv_s[rs, hs] += lax.dot_general(k_dec[rs, hs], dstb, nt, preferred_element_type=F32)
                dec = jnp.exp(jnp.sum(logf[rs, hs], axis=0, keepdims=True))
                e_s[rs, hs] = jnp.broadcast_to(jnp.sum(dst * st, axis=0, keepdims=True) * dec, (C, DK))
                dqd_s[rs, hs] = jnp.dot(doh[rs], st.astype(BF16), preferred_element_type=F32)
                dstate[h] = dec * dst + lax.dot_general(doh[rs], q_dec[rs, hs], tn,
                                                        preferred_element_type=F32)

        dqr, dkr, dqd, dkd = dqr_s[...], dkr_s[...], dqd_s[...], dkd_s[...]
        kdk = dkd * k_dec_f
        db = dqr * q_rel_f - dkr * k_rel_f + dqd * q_dec_f - kdk
        dlogf = _split_dot(rev.astype(BF16), db, 2) + _split_dot(ones.astype(BF16), kdk, 2) + e_s[...]
        dk = dkr * enrel + dkd * erest
        df = dlogf / f - dk
        dlb_ref[...] += jnp.sum(df * (1.0 - sf), axis=0, keepdims=True)
        dq = dqr * erel + dqd * eb
        dp_ref[:, 0:HK] = (dq * (sq * (1.0 + qx * (1.0 - sq)))).astype(BF16)
        dp_ref[:, HK:2 * HK] = (df * (1.0 - lb) * sf * (1.0 - sf)).astype(BF16)
        dp_ref[:, 2 * HK:3 * HK] = dv_s[...].astype(BF16)

    rev_row = lambda w: pl.BlockSpec((tb, w), lambda i: (nb - 1 - i, 0))
    vec = lambda w: pl.BlockSpec((1, w), lambda i: (0, 0))
    scr = pltpu.VMEM((tb, HK), F32)
    return pl.pallas_call(
        body, name="hgrn_bwd", grid=(nb,),
        in_specs=[rev_row(4 * HK), vec(HK), vec(DK), rev_row(HK),
                  pl.BlockSpec((ncb, H, DK, DK), lambda i: (nb - 1 - i, 0, 0, 0)), rev_row(HK)],
        out_specs=(rev_row(4 * HK), vec(HK), vec(DK)),
        out_shape=(jax.ShapeDtypeStruct((T, 4 * HK), BF16), jax.ShapeDtypeStruct((1, HK), F32),
                   jax.ShapeDtypeStruct((1, DK), F32)),
        scratch_shapes=[pltpu.VMEM((H, DK, DK), F32), scr, scr, scr, scr, scr, scr, scr],
        compiler_params=_cparams("arbitrary"))(proj, lb, o_norm, o, states, dy)


def _hgrn_layer_fwd(a, o_norm, lb, wbuf, pk, slot):
    D = a.shape[1]
    in_by_n, _ = _col_sharded(pk, "hgrn_w_in", slot, D)
    out_by_k, _, _ = _row_sharded(pk, "hgrn_w_o", slot, D // N_CHIPS)
    proj = _mm(a, wbuf, n=4 * D, b_map=in_by_n, tk=D, tn=D, name="hgrn_in")
    y, o, states = _hgrn_fwd(proj, lb, o_norm)
    m = _mm(y, wbuf, n=D, b_map=out_by_k, tm=2048, tk=D // N_CHIPS, tn=D, name="hgrn_out")
    return m, (a, proj, y, o, states)


def _hgrn_layer_bwd(dm, saved, o_norm, lb, wbuf, gbuf, pk, slot):
    a, proj, y, o, states = saved
    D = a.shape[1]
    in_by_n, in_by_k = _col_sharded(pk, "hgrn_w_in", slot, D)
    _, out_by_n, out_by_m = _row_sharded(pk, "hgrn_w_o", slot, D // N_CHIPS)
    dy = _mm(dm, wbuf, tb=True, n=y.shape[1], b_map=out_by_n, tm=2048, tn=D // N_CHIPS, tk=D,
             name="hgrn_out_dx")
    gbuf = _mm(y, dm, ta=True, into=gbuf, o_map=out_by_m, tm=D // N_CHIPS, tn=D, tk=2048, name="hgrn_out_dw")
    dproj, dlb, don = _hgrn_bwd(proj, lb, o_norm, o, states, dy)
    gbuf = _mm(a, dproj, ta=True, into=gbuf, o_map=in_by_n, tm=D, tn=D, name="hgrn_in_dw")
    da = _mm(dproj, wbuf, tb=True, n=D, b_map=in_by_k, tn=D, tk=D, name="hgrn_in_dx")
    return da, gbuf, dict(o_norm=don, lb=dlb)


def _lower_bounds(lb_logits):
    p = jax.nn.softmax(lb_logits.astype(F32), axis=0)
    return jnp.cumsum(p, axis=0) - p[0]


def _rope_tables(positions):
    inv_freq = jnp.power(ROPE_BASE, -jnp.arange(0, MLA_ROPE, 2, dtype=F32) / MLA_ROPE)
    ang = positions.astype(F32)[:, None] * inv_freq
    cos, sin = jnp.cos(ang), jnp.sin(ang)
    zero = jnp.zeros((positions.shape[0], 128 - MLA_ROPE), F32)
    return (jnp.concatenate([cos, cos, zero], axis=-1), jnp.concatenate([-sin, sin, zero], axis=-1))


def _pad_mla_weights(w_in, w_uq):
    w_in_p = jnp.pad(w_in, ((0, 0), (0, 0), (0, 128 - MLA_ROPE)))
    n, ql, _ = w_uq.shape
    w_uq_p = jnp.pad(w_uq.reshape(n, ql, MLA_HEADS, MLA_NOPE + MLA_ROPE),
                     ((0, 0), (0, 0), (0, 0), (0, MLA_QK_PAD - MLA_NOPE - MLA_ROPE)))
    return w_in_p, w_uq_p.reshape(n, ql, MLA_HEADS * MLA_QK_PAD)


def _local_step(x, positions, target, small, fetch, gbufs, emit, emit_mlp):
    T, D = x.shape
    lbounds, lb_vjp = jax.vjp(_lower_bounds, small["hgrn_lb_logits"])
    cc, ss = _rope_tables(positions)
    fetched = {0: fetch(0, None)}
    gains = fetched[0]["gains"]
    tick = [jnp.zeros((), F32)]

    def g(layer, i):
        return gains[layer, i][None, :] + tick[0]

    def mla_weights(layer):
        f = fetched[layer]
        w_in_p, w_uq_p = _pad_mla_weights(f["w_in"][None], f["w_uq"][None])
        slot = layer // 2
        return dict(w_in=w_in_p[0], w_uq=w_uq_p[0], w_ukv=f["w_ukv"],
                    q_norm=small["mla_q_norm"][slot][None, :], kv_norm=small["mla_kv_norm"][slot][None, :])

    saved = []
    h = x
    a = _prenorm_fwd(x, g(0, 0))
    dy = sq = None
    for layer in range(DEPTH):
        slot = layer // 2
        if layer not in fetched:
            fetched[layer] = fetch(layer, a)
        wbuf, pk = fetched[layer]["wbuf"], fetched[layer]["pk"]
        if layer % 2 == 0:
            m, mix_saved = _mla_fwd(a, mla_weights(layer), cc, ss, wbuf, pk, slot)
        else:
            m, mix_saved = _hgrn_layer_fwd(a, small["hgrn_o_norm"][slot][None, :], lbounds[layer][None, :],
                                           wbuf, pk, slot)
        h1, a2 = _resnorm_fwd(h, m, g(layer, 1), g(layer, 2), name="resnorm_fwd_mix")
        u, mlp_saved = _mlp_fwd(a2, wbuf, pk, layer)
        if layer + 1 < DEPTH:
            h2, a = _resnorm_fwd(h1, u, g(layer, 3), g(layer + 1, 0), name="resnorm_fwd_mlp")
        else:
            h2 = None
            dy, sq = _resnorm_loss(h1, u, g(layer, 3), target)
        saved.append((h, m, h1, u, mix_saved, mlp_saved))
        h = h2

    n_mla, n_hgrn = (DEPTH + 1) // 2, DEPTH // 2
    dgains = [[None] * 4 for _ in range(DEPTH)]
    gw = {k: [None] * n_mla for k in ("mla_w_in", "mla_w_uq", "mla_w_ukv", "mla_q_norm", "mla_kv_norm")}
    gw["hgrn_o_norm"] = [None] * n_hgrn
    dlb = [jnp.zeros((1, lbounds.shape[1]), F32) for _ in range(DEPTH)]
    dh = dy
    da_next = None
    for layer in reversed(range(DEPTH)):
        h0, m, h1, u, mix_saved, mlp_saved = saved[layer]
        slot = layer // 2
        wbuf, pk, gbuf = fetched[layer]["wbuf"], fetched[layer]["pk"], gbufs[layer]
        if da_next is None:
            du, dgains[layer][3] = _resnorm_bwd(u, g(layer, 3), dh, name="resnorm_bwd_last")
            t = dh
        else:
            h2 = saved[layer + 1][0]
            t, du, dgains[layer][3], dgains[layer + 1][0] = _resnorm_bwd(
                u, g(layer, 3), dh, h2, da_next, g(layer + 1, 0), name="resnorm_bwd_mlp")
        da2, gbuf = _mlp_bwd(du, mlp_saved, wbuf, gbuf, pk, layer)
        if layer == 0:
            gbuf = emit_mlp(layer, gbuf)
        t, dm, dgains[layer][1], dgains[layer][2] = _resnorm_bwd(
            m, g(layer, 1), t, h1, da2, g(layer, 2), name="resnorm_bwd_mix")
        if layer % 2 == 0:
            da_next, gbuf, mg = _mla_bwd(dm, mix_saved, mla_weights(layer), cc, ss, wbuf, gbuf, pk, slot)
            ql = mg["q_norm"].shape[-1]
            kvl = mg["kv_norm"].shape[-1]
            gw["mla_w_in"][slot] = mg["w_in"][:, :ql + kvl + MLA_ROPE]
            gw["mla_w_uq"][slot] = mg["w_uq"].reshape(ql, MLA_HEADS, MLA_QK_PAD)[
                :, :, :MLA_NOPE + MLA_ROPE].reshape(ql, MLA_HEADS * (MLA_NOPE + MLA_ROPE))
            gw["mla_w_ukv"][slot] = mg["w_ukv"]
            gw["mla_q_norm"][slot] = mg["q_norm"][0]
            gw["mla_kv_norm"][slot] = mg["kv_norm"][0]
        else:
            da_next, gbuf, hg = _hgrn_layer_bwd(dm, mix_saved, small["hgrn_o_norm"][slot][None, :],
                                                lbounds[layer][None, :], wbuf, gbuf, pk, slot)
            gw["hgrn_o_norm"][slot] = hg["o_norm"][0]
            dlb[layer] = hg["lb"]
        dh = t
        if layer > 0:
            mine = ({k: gw[k][slot] for k in ("mla_w_in", "mla_w_uq", "mla_w_ukv")} if layer % 2 == 0 else {})
            tick[0] = emit(layer, gbuf, mine)
        else:
            gbuf0 = gbuf
    grad_x, dgains[0][0] = _prenorm_bwd(x, g(0, 0), dh, da_next)

    last = {k: gw[k][0] for k in ("mla_w_in", "mla_w_uq", "mla_w_ukv")}
    last.update({k: jnp.stack(gw[k]) for k in ("mla_q_norm", "mla_kv_norm", "hgrn_o_norm")})
    last["norm_gains"] = jnp.stack([jnp.concatenate(row, axis=0) for row in dgains])
    (last["hgrn_lb_logits"],) = lb_vjp(jnp.concatenate(dlb, axis=0))
    emit(0, gbuf0, last)
    return sq, grad_x


def _size(shape):
    n = 1
    for d in shape:
        n *= d
    return n


def _piece_rows(shape):
    return -(-_size(shape) // PACK_W)


def _packed_misc_rows(shapes):
    return sum(_piece_rows(s) for s in shapes)


def _cast_into(src, buf, row, name):
    rows, W = src.shape
    tr = min(256, rows)
    assert rows % tr == 0 and row % tr == 0

    def body(s_ref, b_ref, o_ref):
        o_ref[...] = s_ref[...].astype(BF16)

    return pl.pallas_call(
        body, name=name, grid=(rows // tr,),
        in_specs=[pl.BlockSpec((tr, W), lambda i: (i, 0)), pl.BlockSpec(memory_space=pl.ANY)],
        out_specs=pl.BlockSpec((tr, W), lambda i: (row // tr + i, 0)),
        out_shape=jax.ShapeDtypeStruct(buf.shape, buf.dtype), input_output_aliases={1: 0},
        compiler_params=_cparams("parallel"))(src, buf)


def _pack_blocks(pieces, rows, dtype):
    blocks, used = [], 0
    for p in pieces:
        flat = p.astype(dtype).reshape(-1)
        r = _piece_rows(p.shape)
        if r * PACK_W != flat.shape[0]:
            flat = jnp.pad(flat, (0, r * PACK_W - flat.shape[0]))
        blocks.append(flat.reshape(r, PACK_W))
        used += r
    if rows > used:
        blocks.append(jnp.zeros((rows - used, PACK_W), dtype))
    return blocks


def _unpack(buf, shapes):
    out, off = [], 0
    for shp in shapes:
        r = _piece_rows(shp)
        piece = buf[off:off + r]
        if r * PACK_W != _size(shp):
            piece = piece.reshape(-1)[:_size(shp)]
        out.append(piece.reshape(shp))
        off += r
    return out


def _mesh_place():
    x, y, c = lax.axis_index("x"), lax.axis_index("y"), lax.axis_index("c")
    chips = [(1 - x, y), (x, 1 - y), (1 - x, 1 - y)]
    return x, y, c, chips


_HBM = pl.BlockSpec(memory_space=pltpu.HBM)


def _all_gather(wp):
    R, W = wp.shape
    rh = R // 2
    rq = rh // 2
    assert rq % 16 == 0

    def body(w_ref, out_ref, send_sems, recv_sems):
        x, y, c, _ = _mesh_place()
        me, jx, jy, jd = 2 * x + y, 2 * (1 - x) + y, 2 * x + (1 - y), 2 * (1 - x) + (1 - y)
        to_x, to_y, sibling = (1 - x, y, c), (x, 1 - y, c), (x, y, 1 - c)

        def rows(core, quarter):
            return pl.ds(pl.multiple_of(core * rh + quarter * rq, 16), rq)

        def slot(j, core, quarter):
            return out_ref.at[j, rows(core, quarter)]

        def copy(k, src, dst, to):
            return pltpu.make_async_remote_copy(src_ref=src, dst_ref=dst, send_sem=send_sems.at[k],
                                                recv_sem=recv_sems.at[k], device_id=to, device_id_type=MESH)

        sends = [copy(0, w_ref.at[rows(c, 0)], slot(me, c, 0), to_x),
                 copy(2, w_ref.at[rows(c, 1)], slot(me, c, 1), to_y),
                 copy(1, w_ref.at[rows(c, 1)], slot(me, c, 1), to_x),
                 copy(3, w_ref.at[rows(c, 0)], slot(me, c, 0), to_y)]
        for cp in sends:
            cp.start()
        arrivals = [(0, slot(jx, c, 0), 4, to_y, 6), (2, slot(jy, c, 1), 5, to_x, 7),
                    (1, slot(jx, c, 1), None, None, 8), (3, slot(jy, c, 0), None, None, 9),
                    (4, slot(jd, c, 0), None, None, 10), (5, slot(jd, c, 1), None, None, 11)]
        for k, landed, k_on, to_on, k_sib in arrivals:
            copy(k, landed, landed, sibling).wait_recv()
            if k_on is not None:
                cp = copy(k_on, landed, landed, to_on)
                cp.start()
                sends.append(cp)
            cp = copy(k_sib, landed, landed, sibling)
            cp.start()
            sends.append(cp)
        for k_sib, j, quarter in ((6, jx, 0), (7, jy, 1), (8, jx, 1), (9, jy, 0), (10, jd, 0), (11, jd, 1)):
            landed = slot(j, 1 - c, quarter)
            copy(k_sib, landed, landed, sibling).wait_recv()
        for cp in sends:
            cp.wait_send()

    out = pl.pallas_call(
        body, name="weights_all_gather", in_specs=[_HBM], out_specs=_HBM,
        out_shape=jax.ShapeDtypeStruct((N_CHIPS, R, W), wp.dtype),
        scratch_shapes=[pltpu.SemaphoreType.DMA((12,)), pltpu.SemaphoreType.DMA((12,))],
    )(wp)
    me = 2 * lax.axis_index("x") + lax.axis_index("y")
    return lax.dynamic_update_slice(out, wp[None], (me, 0, 0))


def _exchange_halves(g):
    n, _, rh, W = g.shape

    def body(g_ref, out_ref, send_sems, recv_sems):
        x, y, c, _ = _mesh_place()
        sibling = (x, y, 1 - c)
        copies = [pltpu.make_async_remote_copy(
            src_ref=g_ref.at[j, 1 - c], dst_ref=out_ref.at[j], send_sem=send_sems.at[j],
            recv_sem=recv_sems.at[j], device_id=sibling, device_id_type=MESH) for j in range(n)]
        for cp in copies:
            cp.start()
        for cp in copies:
            cp.wait()

    return pl.pallas_call(
        body, name="grads_to_sibling", in_specs=[_HBM], out_specs=_HBM,
        out_shape=jax.ShapeDtypeStruct((n, rh, W), g.dtype),
        scratch_shapes=[pltpu.SemaphoreType.DMA((n,)), pltpu.SemaphoreType.DMA((n,))],
    )(g)


def _scatter_to_owners(p):
    n, rh, W = p.shape
    rq = rh // 2
    assert rq % 16 == 0

    def body(p_ref, out_ref, stage_ref, send_sems, recv_sems):
        x, y, c, _ = _mesh_place()
        me, jx, jy, jd = 2 * x + y, 2 * (1 - x) + y, 2 * x + (1 - y), 2 * (1 - x) + (1 - y)
        to_x, to_y = (1 - x, y, c), (x, 1 - y, c)

        def quarter(ref, j, q):
            return ref.at[j, pl.ds(q * rq, rq)]

        def copy(k, src, dst, to):
            return pltpu.make_async_remote_copy(src_ref=src, dst_ref=dst, send_sem=send_sems.at[k],
                                                recv_sem=recv_sems.at[k], device_id=to, device_id_type=MESH)

        sends = [copy(2, quarter(p_ref, jd, 0), stage_ref.at[0], to_x),
                 copy(3, quarter(p_ref, jd, 1), stage_ref.at[1], to_y),
                 copy(0, p_ref.at[jx], out_ref.at[me], to_x),
                 copy(1, p_ref.at[jy], out_ref.at[me], to_y)]
        for cp in sends:
            cp.start()
        copy(2, stage_ref.at[0], stage_ref.at[0], to_x).wait_recv()
        relay = copy(4, stage_ref.at[0], quarter(out_ref, jx, 0), to_y)
        relay.start()
        sends.append(relay)
        copy(3, stage_ref.at[1], stage_ref.at[1], to_y).wait_recv()
        relay = copy(5, stage_ref.at[1], quarter(out_ref, jy, 1), to_x)
        relay.start()
        sends.append(relay)
        copy(0, out_ref.at[jx], out_ref.at[jx], to_x).wait_recv()
        copy(1, out_ref.at[jy], out_ref.at[jy], to_y).wait_recv()
        copy(4, quarter(out_ref, jd, 0), quarter(out_ref, jd, 0), to_y).wait_recv()
        copy(5, quarter(out_ref, jd, 1), quarter(out_ref, jd, 1), to_x).wait_recv()
        for cp in sends:
            cp.wait_send()

    out, _ = pl.pallas_call(
        body, name="grads_to_owner", in_specs=[_HBM], out_specs=(_HBM, _HBM),
        out_shape=(jax.ShapeDtypeStruct((n, rh, W), p.dtype), jax.ShapeDtypeStruct((2, rq, W), p.dtype)),
        scratch_shapes=[pltpu.SemaphoreType.DMA((6,)), pltpu.SemaphoreType.DMA((6,))],
    )(p)
    me = 2 * lax.axis_index("x") + lax.axis_index("y")
    mine = lax.dynamic_index_in_dim(p, me, axis=0, keepdims=True)
    return lax.dynamic_update_slice(out, mine, (me, 0, 0))


def _share_reduced(q, name="grads_share_reduced"):
    rh, W = q.shape

    def body(q_ref, out_ref, send_sem, recv_sem):
        x, y, c, _ = _mesh_place()
        cp = pltpu.make_async_remote_copy(src_ref=q_ref, dst_ref=out_ref.at[c], send_sem=send_sem,
                                          recv_sem=recv_sem, device_id=(x, y, 1 - c), device_id_type=MESH)
        cp.start()
        cp.wait()

    out = pl.pallas_call(
        body, name=name, in_specs=[_HBM], out_specs=_HBM,
        out_shape=jax.ShapeDtypeStruct((2, rh, W), q.dtype),
        scratch_shapes=[pltpu.SemaphoreType.DMA, pltpu.SemaphoreType.DMA],
    )(q)
    return out


def _add_sibling(g, recv, c_arr):
    n, _, rh, W = g.shape
    tr = PACK_TILE

    def body(c_ref, g_ref, r_ref, o_ref):
        o_ref[...] = (g_ref[...].astype(F32) + r_ref[...].astype(F32)).astype(BF16)

    return pl.pallas_call(
        body, name="grads_add_sibling",
        grid_spec=pltpu.PrefetchScalarGridSpec(
            num_scalar_prefetch=1, grid=(n, rh // tr),
            in_specs=[pl.BlockSpec((None, None, tr, W), lambda j, i, c_ref: (j, c_ref[0], i, 0)),
                      pl.BlockSpec((None, tr, W), lambda j, i, c_ref: (j, i, 0))],
            out_specs=pl.BlockSpec((None, tr, W), lambda j, i, c_ref: (j, i, 0))),
        out_shape=jax.ShapeDtypeStruct((n, rh, W), BF16),
        compiler_params=_cparams("parallel", "parallel"))(c_arr, g, recv)


def _sum_chips(parts, own, own_row0, which, name, out_dtype=F32):
    n, rh, W = parts.shape
    tr = PACK_TILE
    assert own_row0 % tr == 0
    if own.ndim == 3:
        own_spec = pl.BlockSpec((None, tr, W), lambda i, w_ref: (w_ref[0], own_row0 // tr + i, 0))
    else:
        own_spec = pl.BlockSpec((tr, W), lambda i, w_ref: (own_row0 // tr + i, 0))

    def body(w_ref, p_ref, own_ref, o_ref):
        mine = own_ref[...].astype(F32)
        acc = None
        for j in range(n):
            term = jnp.where(w_ref[0] == j, mine, p_ref[j].astype(F32))
            acc = term if acc is None else acc + term
        o_ref[...] = acc.astype(out_dtype)

    return pl.pallas_call(
        body, name=name,
        grid_spec=pltpu.PrefetchScalarGridSpec(
            num_scalar_prefetch=1, grid=(rh // tr,),
            in_specs=[pl.BlockSpec((n, tr, W), lambda i, w_ref: (0, i, 0)), own_spec],
            out_specs=pl.BlockSpec((tr, W), lambda i, w_ref: (i, 0))),
        out_shape=jax.ShapeDtypeStruct((rh, W), out_dtype),
        compiler_params=_cparams("parallel"))(jnp.reshape(which, (1,)).astype(jnp.int32), parts, own)


_SEM = pl.BlockSpec(memory_space=pltpu.SEMAPHORE)
_ASYNC = pltpu.CompilerParams(has_side_effects=pltpu.SideEffectType.DATAFLOW_SIDE_EFFECTING)


def _hbm(a):
    return pltpu.with_memory_space_constraint(a, pltpu.HBM)


def _gather_copies(w_ref, land_ref, send_sems, recv_sems):
    x, y, c, chips = _mesh_place()
    me = 2 * x + y
    rh = w_ref.shape[0] // 2
    rows = pl.ds(pl.multiple_of(c * rh, 16), rh)
    return [pltpu.make_async_remote_copy(
        src_ref=w_ref.at[rows], dst_ref=land_ref.at[me, rows], send_sem=send_sems.at[r],
        recv_sem=recv_sems.at[r], device_id=(px, py, c), device_id_type=MESH)
        for r, (px, py) in enumerate(chips)]


def _scatter_copies(g_ref, land_ref, send_sems, recv_sems, row0):
    x, y, c, chips = _mesh_place()
    me = 2 * x + y
    rows = pl.ds(row0, land_ref.shape[1])
    return [pltpu.make_async_remote_copy(
        src_ref=g_ref.at[2 * px + py, rows], dst_ref=land_ref.at[me], send_sem=send_sems.at[r],
        recv_sem=recv_sems.at[r], device_id=(px, py, c), device_id_type=MESH)
        for r, (px, py) in enumerate(chips)]


def _halves_to_sibling(land, name):
    n, R, W = land.shape
    rh = R // 2

    def body(l_ref, o_ref, send_sems, recv_sems):
        x, y, c, chips = _mesh_place()
        rows = pl.ds(pl.multiple_of(c * rh, 16), rh)
        copies = [pltpu.make_async_remote_copy(
            src_ref=o_ref.at[2 * px + py, rows], dst_ref=o_ref.at[2 * px + py, rows], send_sem=send_sems.at[r],
            recv_sem=recv_sems.at[r], device_id=(x, y, 1 - c), device_id_type=MESH)
            for r, (px, py) in enumerate(chips)]
        for cp in copies:
            cp.start()
        for cp in copies:
            cp.wait()

    return pl.pallas_call(
        body, name=name, in_specs=[_HBM], out_specs=_HBM, out_shape=jax.ShapeDtypeStruct(land.shape, land.dtype),
        scratch_shapes=[pltpu.SemaphoreType.DMA((3,)), pltpu.SemaphoreType.DMA((3,))],
        input_output_aliases={0: 0})(land)


def _gather_start(wp, name):
    R, W = wp.shape

    def body(w_ref, land_ref, send_sems, recv_sems, w_thru, land_thru, token):
        for cp in _gather_copies(w_ref, land_ref, send_sems, recv_sems):
            cp.start()
        token[...] = jnp.zeros_like(token)

    return pl.pallas_call(
        body, name=name,
        out_shape=(pltpu.SemaphoreType.DMA((3,)), pltpu.SemaphoreType.DMA((3,)), pltpu.HBM(wp.shape, wp.dtype),
                   pltpu.HBM((N_CHIPS, R, W), wp.dtype), jax.ShapeDtypeStruct((8, 128), F32)),
        in_specs=(_HBM, _HBM),
        out_specs=(_SEM, _SEM, _HBM, _HBM, pl.BlockSpec(memory_space=pltpu.VMEM)),
        input_output_aliases={0: 2, 1: 3}, compiler_params=_ASYNC,
    )(_hbm(wp), _hbm(lax.empty((N_CHIPS, R, W), wp.dtype)))


def _gather_wait(send_sems, recv_sems, w_thru, land_thru, after, name):
    R, W = w_thru.shape
    rh = R // 2

    def body(w_ref, land_ref, send_sems, recv_sems, after_ref, w_dead, got_ref):
        x, y, c, _ = _mesh_place()
        half = land_ref.at[0, pl.ds(0, rh)]
        for k in range(3):
            cp = pltpu.make_async_remote_copy(src_ref=half, dst_ref=half, send_sem=send_sems.at[k],
                                              recv_sem=recv_sems.at[k], device_id=(x, y, 1 - c),
                                              device_id_type=MESH)
            cp.wait_send()
            cp.wait_recv()

    return pl.pallas_call(
        body, name=name,
        out_shape=(pltpu.HBM(w_thru.shape, w_thru.dtype), pltpu.HBM(land_thru.shape, land_thru.dtype)),
        in_specs=(_HBM, _HBM, _SEM, _SEM, pl.BlockSpec(memory_space=pl.ANY)), out_specs=(_HBM, _HBM),
        input_output_aliases={0: 0, 1: 1}, compiler_params=_ASYNC,
    )(w_thru, land_thru, send_sems, recv_sems, after)


def _scatter_start(g, row0, nrows, name):
    n, R, W = g.shape
    land_shape = (n, nrows, W)

    def body(g_ref, land_ref, send_sems, recv_sems, g_thru, land_thru, token):
        for cp in _scatter_copies(g_ref, land_ref, send_sems, recv_sems, row0):
            cp.start()
        token[...] = jnp.zeros_like(token)

    return pl.pallas_call(
        body, name=name,
        out_shape=(pltpu.SemaphoreType.DMA((3,)), pltpu.SemaphoreType.DMA((3,)), pltpu.HBM(g.shape, g.dtype),
                   pltpu.HBM(land_shape, g.dtype), jax.ShapeDtypeStruct((8, 128), F32)),
        in_specs=(_HBM, _HBM),
        out_specs=(_SEM, _SEM, _HBM, _HBM, pl.BlockSpec(memory_space=pltpu.VMEM)),
        input_output_aliases={0: 2, 1: 3}, compiler_params=_ASYNC,
    )(_hbm(g), _hbm(lax.empty(land_shape, g.dtype)))


def _scatter_wait(send_sems, recv_sems, g_thru, land_thru, after, name):
    def body(g_ref, land_ref, send_sems, recv_sems, after_ref, g_out, got_ref):
        x, y, c, _ = _mesh_place()
        for k in range(3):
            cp = pltpu.make_async_remote_copy(src_ref=land_ref.at[0], dst_ref=land_ref.at[0], send_sem=send_sems.at[k],
                                              recv_sem=recv_sems.at[k], device_id=(x, y, 1 - c),
                                              device_id_type=MESH)
            cp.wait_send()
            cp.wait_recv()

    return pl.pallas_call(
        body, name=name,
        out_shape=(pltpu.HBM(g_thru.shape, g_thru.dtype), pltpu.HBM(land_thru.shape, land_thru.dtype)),
        in_specs=(_HBM, _HBM, _SEM, _SEM, pl.BlockSpec(memory_space=pl.ANY)), out_specs=(_HBM, _HBM),
        input_output_aliases={0: 0, 1: 1}, compiler_params=_ASYNC,
    )(g_thru, land_thru, send_sems, recv_sems, after)


def _adamw(w, g, m, v, name):
    shape = w.shape
    cols = shape[-1]
    w2, g2, m2, v2 = (t.reshape(-1, cols) for t in (w, g, m, v))
    rows = w2.shape[0]
    tr = rows
    for cand in (512, 256, 128, 64, 32, 16, 8):
        if rows > cand and rows % cand == 0:
            tr = cand
            break
    c1 = 1.0 / (1.0 - ADAM_B1 ** ADAM_STEP)
    c2 = 1.0 / (1.0 - ADAM_B2 ** ADAM_STEP)

    def body(w_ref, g_ref, m_ref, v_ref, d_ref, nm_ref, nv_ref):
        gv = g_ref[...]
        nm = ADAM_B1 * m_ref[...] + (1.0 - ADAM_B1) * gv
        nv = ADAM_B2 * v_ref[...] + (1.0 - ADAM_B2) * (gv * gv)
        nm_ref[...] = nm
        nv_ref[...] = nv
        d_ref[...] = -ADAM_LR * ((nm * c1) / (jnp.sqrt(nv * c2) + ADAM_EPS) + ADAM_WD * w_ref[...])

    blk = pl.BlockSpec((tr, cols), lambda i: (i, 0))
    sds = jax.ShapeDtypeStruct((rows, cols), F32)
    d, nm, nv = pl.pallas_call(body, name=name, grid=(rows // tr,), in_specs=[blk] * 4,
                               out_specs=(blk, blk, blk), out_shape=(sds, sds, sds),
                               compiler_params=_cparams("parallel"))(w2, g2, m2, v2)
    return d.reshape(shape), nm.reshape(shape), nv.reshape(shape)


def kernel(x, positions, norm_gains, mla_w_in, mla_q_norm, mla_kv_norm, mla_w_uq, mla_w_ukv, mla_w_o, hgrn_w_in, hgrn_lb_logits, hgrn_o_norm, hgrn_w_o, mlp_w1, mlp_w2, loss_target, m_norm_gains, m_mla_w_in, m_mla_q_norm, m_mla_kv_norm, m_mla_w_uq, m_mla_w_ukv, m_mla_w_o, m_hgrn_w_in, m_hgrn_lb_logits, m_hgrn_o_norm, m_hgrn_w_o, m_mlp_w1, m_mlp_w2, v_norm_gains, v_mla_w_in, v_mla_q_norm, v_mla_kv_norm, v_mla_w_uq, v_mla_w_ukv, v_mla_w_o, v_hgrn_w_in, v_hgrn_lb_logits, v_hgrn_o_norm, v_hgrn_w_o, v_mlp_w1, v_mlp_w2):
    w = dict(norm_gains=norm_gains, mla_w_in=mla_w_in, mla_q_norm=mla_q_norm, mla_kv_norm=mla_kv_norm,
             mla_w_uq=mla_w_uq, mla_w_ukv=mla_w_ukv, mla_w_o=mla_w_o, hgrn_w_in=hgrn_w_in,
             hgrn_lb_logits=hgrn_lb_logits, hgrn_o_norm=hgrn_o_norm, hgrn_w_o=hgrn_w_o,
             mlp_w1=mlp_w1, mlp_w2=mlp_w2)
    mom_m = dict(norm_gains=m_norm_gains, mla_w_in=m_mla_w_in, mla_q_norm=m_mla_q_norm,
                 mla_kv_norm=m_mla_kv_norm, mla_w_uq=m_mla_w_uq, mla_w_ukv=m_mla_w_ukv,
                 mla_w_o=m_mla_w_o, hgrn_w_in=m_hgrn_w_in, hgrn_lb_logits=m_hgrn_lb_logits,
                 hgrn_o_norm=m_hgrn_o_norm, hgrn_w_o=m_hgrn_w_o, mlp_w1=m_mlp_w1, mlp_w2=m_mlp_w2)
    mom_v = dict(norm_gains=v_norm_gains, mla_w_in=v_mla_w_in, mla_q_norm=v_mla_q_norm,
                 mla_kv_norm=v_mla_kv_norm, mla_w_uq=v_mla_w_uq, mla_w_ukv=v_mla_w_ukv,
                 mla_w_o=v_mla_w_o, hgrn_w_in=v_hgrn_w_in, hgrn_lb_logits=v_hgrn_lb_logits,
                 hgrn_o_norm=v_hgrn_o_norm, hgrn_w_o=v_hgrn_w_o, mlp_w1=v_mlp_w1, mlp_w2=v_mlp_w2)
    c = lax.axis_index("c")

    axis_of = dict(SHARDED)
    me = 2 * lax.axis_index("x") + lax.axis_index("y")
    gain_bits = lax.bitcast_convert_type(norm_gains, jnp.uint32)
    gain_hi = lax.bitcast_convert_type((gain_bits >> 16).astype(jnp.uint16), BF16)
    gain_lo = lax.bitcast_convert_type((gain_bits & 0xFFFF).astype(jnp.uint16), BF16)

    layers = []
    for l in range(DEPTH):
        s = l // 2
        if l % 2 == 0:
            big = [("mlp_w1", l), ("mlp_w2", l), ("mla_w_o", s)]
            tail = [("mla_w_in", s), ("mla_w_uq", s), ("mla_w_ukv", s)]
        else:
            big = [("hgrn_w_in", s), ("mlp_w1", l), ("mlp_w2", l), ("hgrn_w_o", s)]
            tail = []
        w_tail = [w[n][i] for n, i in tail] + ([gain_hi, gain_lo] if l == 0 else [])
        g_tail = tail + ([("norm_gains", None)] + [(n, None) for n in REPLICATED] if l == 0 else [])
        g_shapes = [w[n].shape if i is None else w[n][i].shape for n, i in g_tail]
        tail_rows = max(_packed_misc_rows([t.shape for t in w_tail]), _packed_misc_rows(g_shapes))
        pk = _Packed([(n, w[n].shape[1]) for n, _ in big], tail_rows)
        wpack = jnp.zeros((pk.rows, PACK_W), BF16)
        for n, i in big:
            assert w[n].shape[2] == PACK_W
            wpack = _cast_into(w[n][i], wpack, pk.off[n], name="pack_%s_%d" % (n, l))
        if w_tail:
            wpack = lax.dynamic_update_slice(
                wpack, jnp.concatenate(_pack_blocks(w_tail, 0, BF16), axis=0), (pk.misc, 0))
        layers.append(dict(pk=pk, big=big, tail=tail, w_tail=w_tail, g_tail=g_tail, g_shapes=g_shapes,
                           gather=_gather_start(wpack, name="gather_start_%d" % l)))

    def fetch(l, after):
        lay = layers[l]
        pk = lay["pk"]
        send_sems, recv_sems, w_thru, land_thru, _ = lay["gather"]
        if after is None:
            after = sum(layers[k]["gather"][4] for k in range(1, DEPTH))
        w_back, land = _gather_wait(send_sems, recv_sems, w_thru, land_thru, after, name="gather_wait_%d" % l)
        land = _halves_to_sibling(land, name="gather_halves_%d" % l)
        land = lax.dynamic_update_slice(land, w_back[None], (me, 0, 0))
        out = dict(wbuf=land.reshape(N_CHIPS * pk.rows, PACK_W), pk=pk)
        if lay["w_tail"]:
            rows = _packed_misc_rows([t.shape for t in lay["w_tail"]])
            per_chip = [_unpack(land[j, pk.misc:pk.misc + rows], [t.shape for t in lay["w_tail"]])
                        for j in range(N_CHIPS)]
            for i, (n, _) in enumerate(lay["tail"]):
                out[n[4:]] = jnp.concatenate([per_chip[j][i] for j in range(N_CHIPS)], axis=axis_of[n] - 1)
            if l == 0:
                got_hi, got_lo = (lax.bitcast_convert_type(
                    jnp.concatenate([per_chip[j][i] for j in range(N_CHIPS)], axis=2),
                    jnp.uint16).astype(jnp.uint32) for i in (-2, -1))
                out["gains"] = lax.bitcast_convert_type((got_hi << 16) | got_lo, F32)
        return out

    def emit(l, gbuf, grads):
        lay = layers[l]
        pk = lay["pk"]
        if lay["g_tail"]:
            for j in range(N_CHIPS):
                pieces = []
                for n, i in lay["g_tail"]:
                    if n not in axis_of:
                        pieces.append(grads[n])
                    else:
                        pieces.append(jnp.split(grads[n], N_CHIPS, axis=axis_of[n] - (0 if i is None else 1))[j])
                block = jnp.concatenate(_pack_blocks(pieces, 0, BF16), axis=0)
                gbuf = lax.dynamic_update_slice(gbuf, block, (j * pk.rows + pk.misc, 0))
        row0 = lay.get("early_rows", 0)
        lay["scatter"] = _scatter_start(gbuf.reshape(N_CHIPS, pk.rows, PACK_W), row0, pk.rows - row0,
                                        name="scatter_start_%d" % l)
        return lay["scatter"][4][0, 0]

    def emit_mlp(l, gbuf):
        lay = layers[l]
        pk = lay["pk"]
        assert pk.off["mlp_w1"] == 0 and pk.off["mlp_w2"] == w["mlp_w1"].shape[1]
        lay["early_rows"] = w["mlp_w1"].shape[1] + w["mlp_w2"].shape[1]
        lay["scatter_early"] = _scatter_start(gbuf.reshape(N_CHIPS, pk.rows, PACK_W), 0, lay["early_rows"],
                                              name="scatter_start_%d_mlp" % l)
        return lay["scatter_early"][2].reshape(N_CHIPS * pk.rows, PACK_W)

    small = dict(mla_q_norm=mla_q_norm, mla_kv_norm=mla_kv_norm, hgrn_lb_logits=hgrn_lb_logits,
                 hgrn_o_norm=hgrn_o_norm)
    gbufs = [jnp.zeros((N_CHIPS * lay["pk"].rows, PACK_W), BF16) for lay in layers]
    sq, grad_x = _local_step(x[0], positions[0], loss_target[0], small, fetch, gbufs, emit, emit_mlp)
    d_model = x.shape[-1]
    loss = lax.psum(0.5 * jnp.sum(sq) / d_model, ("x", "y", "c"))

    per_name = {}
    behind = grad_x
    for l, lay in reversed(list(enumerate(layers))):
        pk = lay["pk"]
        send_sems, recv_sems, g_thru, land_thru, _ = lay["scatter"]
        row0 = lay.get("early_rows", 0)
        early = None
        if row0:
            e_send, e_recv, _, e_land, _ = lay["scatter_early"]
            g_thru, land = _scatter_wait(e_send, e_recv, g_thru, e_land, behind, name="scatter_wait_%d_mlp" % l)
            early = behind = _sum_chips(land, g_thru, 0, me, name="grads_sum_chips_%d_mlp" % l, out_dtype=BF16)
        g_back, land = _scatter_wait(send_sems, recv_sems, g_thru, land_thru, behind, name="scatter_wait_%d" % l)
        mine = _sum_chips(land, g_back, row0, me, name="grads_sum_chips_%d" % l, out_dtype=BF16)
        if early is not None:
            mine = jnp.concatenate([early, mine], axis=0)
        red = behind = _sum_chips(_share_reduced(mine, name="grads_share_%d" % l), mine, 0, c,
                                  name="grads_sum_cores_%d" % l)
        for n, i in lay["big"]:
            per_name.setdefault(n, {})[i] = red[pk.off[n]:pk.off[n] + w[n].shape[1]]
        for (n, i), piece in zip(lay["g_tail"], _unpack(red[pk.misc:pk.misc + pk.misc_rows], lay["g_shapes"])):
            per_name.setdefault(n, {})[i] = piece
    g_out = {n: (parts[None] if None in parts else jnp.stack([parts[i] for i in sorted(parts)]))
             for n, parts in per_name.items()}

    deltas, new_m, new_v = {}, {}, {}
    for name in WEIGHTS:
        deltas[name], new_m[name], new_v[name] = _adamw(w[name], g_out[name], mom_m[name], mom_v[name],
                                                        name="adamw_" + name)
    return (loss, grad_x[None], *[g_out[n] for n in WEIGHTS], *[deltas[n] for n in WEIGHTS],
            *[new_m[n] for n in WEIGHTS], *[new_v[n] for n in WEIGHTS])
```

```python
import jax
import jax.numpy as jnp
from jax import lax
from jax.experimental import pallas as pl
from jax.experimental.pallas import tpu as pltpu

F32 = jnp.float32
BF16 = jnp.bfloat16
MESH = pl.DeviceIdType.MESH

DEPTH = 4
MLA_HEADS = 8
MLA_NOPE = 128
MLA_ROPE = 64
MLA_V = 128
MLA_QK_PAD = 256
MLA_HEADS_PER_STEP = 2
MLA_SCALE = float(MLA_NOPE + MLA_ROPE) ** -0.5
ROPE_BASE = 10000.0
HGRN_HEADS = 8
HGRN_CHUNK = 32
HGRN_BLOCK = 128
EPS = 1e-6

ADAM_LR = 0.001
ADAM_B1 = 0.9
ADAM_B2 = 0.999
ADAM_EPS = 1e-08
ADAM_WD = 0.01
ADAM_STEP = 10

N_CHIPS = 4
PACK_W = 1024
PACK_ALIGN = 1024
PACK_TILE = 512
V7X_VMEM_LIMIT = 56 * 1024 * 1024

SHARDED = (("norm_gains", 2), ("mla_w_in", 1), ("mla_w_uq", 2), ("mla_w_ukv", 2), ("mla_w_o", 1),
           ("hgrn_w_in", 2), ("hgrn_w_o", 1), ("mlp_w1", 2), ("mlp_w2", 1))
REPLICATED = ("mla_q_norm", "mla_kv_norm", "hgrn_lb_logits", "hgrn_o_norm")
WEIGHTS = ("norm_gains", "mla_w_in", "mla_q_norm", "mla_kv_norm", "mla_w_uq", "mla_w_ukv", "mla_w_o",
           "hgrn_w_in", "hgrn_lb_logits", "hgrn_o_norm", "hgrn_w_o", "mlp_w1", "mlp_w2")


def _cparams(*semantics):
    return pltpu.CompilerParams(dimension_semantics=semantics, vmem_limit_bytes=V7X_VMEM_LIMIT)


def _sigmoid(x):
    return 0.5 * jnp.tanh(0.5 * x) + 0.5


def _mm(a, b, *, ta=False, tb=False, out_dtype=F32, tm=2048, tn=1024, tk=1024, epi=None, extra=None,
        name="mm", n=None, b_map=None, into=None, o_map=None):
    if ta:
        K, M = a.shape
    else:
        M, K = a.shape
    if b_map is not None:
        N = n
    elif tb:
        N, Kb = b.shape
    else:
        Kb, N = b.shape
    assert b_map is not None or K == Kb, (a.shape, b.shape, ta, tb)
    tm, tn = min(tm, M), min(tn, N)
    tk = K if (K <= 1024 and b_map is None) else min(tk, K)
    assert M % tm == 0 and N % tn == 0 and K % tk == 0, (M, N, K, tm, tn, tk)
    nk = K // tk
    a_spec = (pl.BlockSpec((tk, tm), lambda i, j, k: (k, i)) if ta
              else pl.BlockSpec((tm, tk), lambda i, j, k: (i, k)))
    if b_map is None:
        b_map = (lambda i, j, k: (j, k)) if tb else (lambda i, j, k: (k, j))
    b_spec = pl.BlockSpec((tn, tk) if tb else (tk, tn), b_map)
    o_spec = pl.BlockSpec((tm, tn), lambda i, j, k: (i, j))
    dims = (((0 if ta else 1,), (1 if tb else 0,)), ((), ()))
    in_specs = [a_spec, b_spec]
    operands = [a, b]
    aliases = {}
    if epi == "mul2r":
        in_specs.append(o_spec)
        operands.append(extra)
    if into is not None:
        assert epi is None
        in_specs.append(pl.BlockSpec(memory_space=pl.ANY))
        operands.append(into)
        aliases = {2: 0}
        out_dtype = into.dtype
        out_shape = jax.ShapeDtypeStruct(into.shape, into.dtype)
        out_specs = pl.BlockSpec((tm, tn), o_map)
    elif epi == "relu2":
        out_shape = (jax.ShapeDtypeStruct((M, N), BF16), jax.ShapeDtypeStruct((M, N), BF16))
        out_specs = (o_spec, o_spec)
    elif epi == "mul2r":
        out_shape = jax.ShapeDtypeStruct((M, N), BF16)
        out_specs = o_spec
    else:
        out_shape = jax.ShapeDtypeStruct((M, N), out_dtype)
        out_specs = o_spec
    n_in = len(operands)

    def body(*refs):
        a_ref, b_ref = refs[0], refs[1]
        outs = refs[n_in:n_in + (2 if epi == "relu2" else 1)]
        k = pl.program_id(2)

        def finish(acc):
            if epi == "relu2":
                r = jnp.maximum(acc, 0.0)
                outs[0][...] = (r * r).astype(BF16)
                outs[1][...] = r.astype(BF16)
            elif epi == "mul2r":
                outs[0][...] = (acc * (2.0 * refs[2][...].astype(F32))).astype(BF16)
            else:
                outs[0][...] = acc.astype(out_dtype)

        part = lax.dot_general(a_ref[...], b_ref[...], dims, preferred_element_type=F32)
        if nk == 1:
            finish(part)
            return
        acc_ref = refs[-1]

        @pl.when(k == 0)
        def _():
            acc_ref[...] = part

        @pl.when((k > 0) & (k < nk - 1))
        def _():
            acc_ref[...] += part

        @pl.when(k == nk - 1)
        def _():
            finish(acc_ref[...] + part)

    return pl.pallas_call(
        body, name=name, grid=(M // tm, N // tn, nk), in_specs=in_specs, out_specs=out_specs,
        out_shape=out_shape, scratch_shapes=[pltpu.VMEM((tm, tn), F32)] if nk > 1 else [],
        input_output_aliases=aliases,
        compiler_params=_cparams("parallel", "parallel", "arbitrary"))(*operands)


def _rms_rstd(x):
    return lax.rsqrt(jnp.mean(x * x, axis=-1, keepdims=True) + EPS)


def _rms_bwd_tile(x, g, dy):
    r = _rms_rstd(x)
    xh = x * r
    u = dy * g
    dx = r * (u - xh * jnp.mean(u * xh, axis=-1, keepdims=True))
    dg = jnp.sum(dy * xh, axis=0, keepdims=True)
    return dx, dg


def _row_tile(T):
    return min(256, T)


def _prenorm_fwd(x, g, name="prenorm_fwd"):
    T, D = x.shape
    tm = _row_tile(T)

    def body(x_ref, g_ref, a_ref):
        xv = x_ref[...]
        a_ref[...] = (xv * _rms_rstd(xv) * g_ref[...]).astype(BF16)

    row = pl.BlockSpec((tm, D), lambda i: (i, 0))
    vec = pl.BlockSpec((1, D), lambda i: (0, 0))
    return pl.pallas_call(body, name=name, grid=(T // tm,), in_specs=[row, vec], out_specs=row,
                          out_shape=jax.ShapeDtypeStruct((T, D), BF16),
                          compiler_params=_cparams("parallel"))(x, g)


def _resnorm_fwd(h, z, g_post, g_pre, name="resnorm_fwd"):
    T, D = h.shape
    tm = _row_tile(T)

    def body(h_ref, z_ref, gp_ref, gn_ref, hn_ref, a_ref):
        zv = z_ref[...]
        hn = h_ref[...] + zv * _rms_rstd(zv) * gp_ref[...]
        hn_ref[...] = hn
        a_ref[...] = (hn * _rms_rstd(hn) * gn_ref[...]).astype(BF16)

    row = pl.BlockSpec((tm, D), lambda i: (i, 0))
    vec = pl.BlockSpec((1, D), lambda i: (0, 0))
    return pl.pallas_call(body, name=name, grid=(T // tm,), in_specs=[row, row, vec, vec],
                          out_specs=(row, row),
                          out_shape=(jax.ShapeDtypeStruct((T, D), F32), jax.ShapeDtypeStruct((T, D), BF16)),
                          compiler_params=_cparams("parallel"))(h, z, g_post, g_pre)


def _resnorm_loss(h, z, g_post, target, name="resnorm_loss"):
    T, D = h.shape
    tm = _row_tile(T)

    def body(h_ref, z_ref, gp_ref, t_ref, dy_ref, sq_ref):
        zv = z_ref[...]
        err = h_ref[...] + zv * _rms_rstd(zv) * gp_ref[...] - t_ref[...]
        dy_ref[...] = err * (1.0 / D)

        @pl.when(pl.program_id(0) == 0)
        def _():
            sq_ref[...] = jnp.zeros_like(sq_ref)

        sq_ref[...] += jnp.sum(err * err, axis=0, keepdims=True)

    row = pl.BlockSpec((tm, D), lambda i: (i, 0))
    vec = pl.BlockSpec((1, D), lambda i: (0, 0))
    return pl.pallas_call(body, name=name, grid=(T // tm,), in_specs=[row, row, vec, row],
                          out_specs=(row, vec),
                          out_shape=(jax.ShapeDtypeStruct((T, D), F32), jax.ShapeDtypeStruct((1, D), F32)),
                          compiler_params=_cparams("arbitrary"))(h, z, g_post, target)


def _resnorm_bwd(z, g_post, dh, h_new=None, da=None, g_pre=None, name="resnorm_bwd"):
    T, D = z.shape
    tm = _row_tile(T)
    has_next = h_new is not None
    row = pl.BlockSpec((tm, D), lambda i: (i, 0))
    vec = pl.BlockSpec((1, D), lambda i: (0, 0))

    if has_next:
        def body(z_ref, gp_ref, dh_ref, hn_ref, da_ref, gn_ref, t_ref, dz_ref, dgp_ref, dgn_ref):
            first = pl.program_id(0) == 0

            @pl.when(first)
            def _():
                dgp_ref[...] = jnp.zeros_like(dgp_ref)
                dgn_ref[...] = jnp.zeros_like(dgn_ref)

            dpre, dgn = _rms_bwd_tile(hn_ref[...], gn_ref[...], da_ref[...])
            t = dh_ref[...] + dpre
            t_ref[...] = t
            dz, dgp = _rms_bwd_tile(z_ref[...], gp_ref[...], t)
            dz_ref[...] = dz.astype(BF16)
            dgp_ref[...] += dgp
            dgn_ref[...] += dgn

        return pl.pallas_call(
            body, name=name, grid=(T // tm,), in_specs=[row, vec, row, row, row, vec],
            out_specs=(row, row, vec, vec),
            out_shape=(jax.ShapeDtypeStruct((T, D), F32), jax.ShapeDtypeStruct((T, D), BF16),
                       jax.ShapeDtypeStruct((1, D), F32), jax.ShapeDtypeStruct((1, D), F32)),
            compiler_params=_cparams("arbitrary"))(z, g_post, dh, h_new, da, g_pre)

    def body_last(z_ref, gp_ref, dh_ref, dz_ref, dgp_ref):
        @pl.when(pl.program_id(0) == 0)
        def _():
            dgp_ref[...] = jnp.zeros_like(dgp_ref)

        dz, dgp = _rms_bwd_tile(z_ref[...], gp_ref[...], dh_ref[...])
        dz_ref[...] = dz.astype(BF16)
        dgp_ref[...] += dgp

    return pl.pallas_call(
        body_last, name=name, grid=(T // tm,), in_specs=[row, vec, row], out_specs=(row, vec),
        out_shape=(jax.ShapeDtypeStruct((T, D), BF16), jax.ShapeDtypeStruct((1, D), F32)),
        compiler_params=_cparams("arbitrary"))(z, g_post, dh)


def _prenorm_bwd(x, g, dh, da, name="prenorm_bwd"):
    T, D = x.shape
    tm = _row_tile(T)

    def body(x_ref, g_ref, dh_ref, da_ref, dx_ref, dg_ref):
        @pl.when(pl.program_id(0) == 0)
        def _():
            dg_ref[...] = jnp.zeros_like(dg_ref)

        dpre, dg = _rms_bwd_tile(x_ref[...], g_ref[...], da_ref[...])
        dx_ref[...] = dh_ref[...] + dpre
        dg_ref[...] += dg

    row = pl.BlockSpec((tm, D), lambda i: (i, 0))
    vec = pl.BlockSpec((1, D), lambda i: (0, 0))
    return pl.pallas_call(
        body, name=name, grid=(T // tm,), in_specs=[row, vec, row, row], out_specs=(row, vec),
        out_shape=(jax.ShapeDtypeStruct((T, D), F32), jax.ShapeDtypeStruct((1, D), F32)),
        compiler_params=_cparams("arbitrary"))(x, g, dh, da)


class _Packed:
    def __init__(self, big, misc_rows):
        self.big = tuple(big)
        self.off = {}
        r = 0
        for name, rows in big:
            self.off[name] = r
            r += rows
        self.misc, self.misc_rows = r, misc_rows
        self.rows = -(-(r + misc_rows) // PACK_ALIGN) * PACK_ALIGN

    def block(self, name, layer, unit):
        r = self.off[name]
        assert r % unit == 0 and self.rows % unit == 0
        return r // unit, self.rows // unit


def _col_sharded(pk, name, layer, unit):
    base, stride = pk.block(name, layer, unit)
    return (lambda i, j, k: (j * stride + base, 0)), (lambda i, j, k: (k * stride + base, 0))


def _row_sharded(pk, name, layer, unit):
    base, stride = pk.block(name, layer, unit)
    return ((lambda i, j, k: (k * stride + base, 0)), (lambda i, j, k: (j * stride + base, 0)),
            (lambda i, j, k: (i * stride + base, 0)))


def _mlp_fwd(a, wbuf, pk, layer):
    D = a.shape[1]
    by_n, _ = _col_sharded(pk, "mlp_w1", layer, D)
    by_k, _, _ = _row_sharded(pk, "mlp_w2", layer, D)
    act, r = _mm(a, wbuf, n=4 * D, b_map=by_n, tk=D, tn=D, epi="relu2", name="mlp_up")
    u = _mm(act, wbuf, n=D, b_map=by_k, tk=D, tn=D, name="mlp_down")
    return u, (a, act, r)


def _mlp_bwd(du, saved, wbuf, gbuf, pk, layer):
    a, act, r = saved
    D = a.shape[1]
    w1_by_n, w1_by_k = _col_sharded(pk, "mlp_w1", layer, D)
    _, w2_by_n, w2_by_m = _row_sharded(pk, "mlp_w2", layer, D)
    dz1 = _mm(du, wbuf, tb=True, n=4 * D, b_map=w2_by_n, tn=D, tk=D, epi="mul2r", extra=r, name="mlp_down_dx")
    gbuf = _mm(act, du, ta=True, into=gbuf, o_map=w2_by_m, tm=D, tn=D, name="mlp_down_dw")
    gbuf = _mm(a, dz1, ta=True, into=gbuf, o_map=w1_by_n, tm=D, tn=D, name="mlp_up_dw")
    da = _mm(dz1, wbuf, tb=True, n=D, b_map=w1_by_k, tn=D, tk=D, name="mlp_up_dx")
    return da, gbuf


def _rope_swap(t):
    n = t.shape[-1]
    lane = lax.broadcasted_iota(jnp.int32, t.shape, t.ndim - 1)
    half = MLA_ROPE // 2
    first = (lane & (MLA_ROPE - 1)) < half
    return jnp.where(first, pltpu.roll(t, n - half, t.ndim - 1), pltpu.roll(t, half, t.ndim - 1))


def _mla_mid_fwd(proj, q_norm, kv_norm, w_uq, w_ukv, cc, ss):
    T, PW = proj.shape
    QL, KVL = q_norm.shape[-1], kv_norm.shape[-1]
    H = MLA_HEADS
    assert PW == QL + KVL + 128
    tm = _row_tile(T)

    def body(p_ref, qn_ref, kn_ref, wq_ref, wkv_ref, cc_ref, ss_ref,
             cq_ref, ckv_ref, q_ref, k_ref, v_ref):
        cq = p_ref[:, 0:QL]
        ckv = p_ref[:, QL:QL + KVL]
        kr = p_ref[:, QL + KVL:QL + KVL + 128]
        c, s = cc_ref[...], ss_ref[...]
        cqn = (cq * _rms_rstd(cq) * qn_ref[...]).astype(BF16)
        ckvn = (ckv * _rms_rstd(ckv) * kn_ref[...]).astype(BF16)
        cq_ref[...] = cqn
        ckv_ref[...] = ckvn
        q = jnp.dot(cqn, wq_ref[...], preferred_element_type=F32)
        kv = jnp.dot(ckvn, wkv_ref[...], preferred_element_type=F32)
        krf = (kr * c + _rope_swap(kr) * s).astype(BF16)
        for h in range(H):
            o = h * MLA_QK_PAD
            q_ref[:, o:o + MLA_NOPE] = (q[:, o:o + MLA_NOPE] * MLA_SCALE).astype(BF16)
            qr = q[:, o + MLA_NOPE:o + MLA_QK_PAD]
            q_ref[:, o + MLA_NOPE:o + MLA_QK_PAD] = ((qr * c + _rope_swap(qr) * s) * MLA_SCALE).astype(BF16)
            k_ref[:, o:o + MLA_NOPE] = kv[:, o:o + MLA_NOPE].astype(BF16)
            k_ref[:, o + MLA_NOPE:o + MLA_QK_PAD] = krf
            v_ref[:, h * MLA_V:(h + 1) * MLA_V] = kv[:, o + MLA_NOPE:o + MLA_QK_PAD].astype(BF16)

    def row(w):
        return pl.BlockSpec((tm, w), lambda i: (i, 0))

    def full(shape):
        return pl.BlockSpec(shape, lambda i: (0, 0))

    return pl.pallas_call(
        body, name="mla_mid_fwd", grid=(T // tm,),
        in_specs=[row(PW), full((1, QL)), full((1, KVL)), full(w_uq.shape), full(w_ukv.shape),
                  row(128), row(128)],
        out_specs=(row(QL), row(KVL), row(H * MLA_QK_PAD), row(H * MLA_QK_PAD), row(H * MLA_V)),
        out_shape=(jax.ShapeDtypeStruct((T, QL), BF16), jax.ShapeDtypeStruct((T, KVL), BF16),
                   jax.ShapeDtypeStruct((T, H * MLA_QK_PAD), BF16),
                   jax.ShapeDtypeStruct((T, H * MLA_QK_PAD), BF16),
                   jax.ShapeDtypeStruct((T, H * MLA_V), BF16)),
        compiler_params=_cparams("parallel"))(proj, q_norm, kv_norm, w_uq, w_ukv, cc, ss)


def _mla_mid_bwd(proj, q_norm, kv_norm, w_uq, w_ukv, cc, ss, dq, dk, dv):
    T, PW = proj.shape
    QL, KVL = q_norm.shape[-1], kv_norm.shape[-1]
    H = MLA_HEADS
    tm = _row_tile(T)
    nt = (((1,), (1,)), ((), ()))

    def body(p_ref, qn_ref, kn_ref, wq_ref, wkv_ref, cc_ref, ss_ref, dq_ref, dk_ref, dv_ref,
             dqp_ref, dkv_ref, dp_ref, dqn_ref, dkn_ref):
        @pl.when(pl.program_id(0) == 0)
        def _():
            dqn_ref[...] = jnp.zeros_like(dqn_ref)
            dkn_ref[...] = jnp.zeros_like(dkn_ref)

        c, s = cc_ref[...], ss_ref[...]
        dkr = jnp.zeros((tm, 128), F32)
        for h in range(H):
            o = h * MLA_QK_PAD
            dqp_ref[:, o:o + MLA_NOPE] = (dq_ref[:, o:o + MLA_NOPE] * MLA_SCALE).astype(BF16)
            dqr = dq_ref[:, o + MLA_NOPE:o + MLA_QK_PAD] * MLA_SCALE
            dqp_ref[:, o + MLA_NOPE:o + MLA_QK_PAD] = (dqr * c + _rope_swap(dqr * s)).astype(BF16)
            dkv_ref[:, o:o + MLA_NOPE] = dk_ref[:, o:o + MLA_NOPE].astype(BF16)
            dkv_ref[:, o + MLA_NOPE:o + MLA_QK_PAD] = dv_ref[:, h * MLA_V:(h + 1) * MLA_V].astype(BF16)
            dkr = dkr + dk_ref[:, o + MLA_NOPE:o + MLA_QK_PAD]
        dcqn = lax.dot_general(dqp_ref[...], wq_ref[...], nt, preferred_element_type=F32)
        dckvn = lax.dot_general(dkv_ref[...], wkv_ref[...], nt, preferred_element_type=F32)
        dcq, dqn = _rms_bwd_tile(p_ref[:, 0:QL], qn_ref[...], dcqn)
        dckv, dkn = _rms_bwd_tile(p_ref[:, QL:QL + KVL], kn_ref[...], dckvn)
        dp_ref[:, 0:QL] = dcq.astype(BF16)
        dp_ref[:, QL:QL + KVL] = dckv.astype(BF16)
        dp_ref[:, QL + KVL:QL + KVL + 128] = (dkr * c + _rope_swap(dkr * s)).astype(BF16)
        dqn_ref[...] += dqn
        dkn_ref[...] += dkn

    def row(w):
        return pl.BlockSpec((tm, w), lambda i: (i, 0))

    def full(shape):
        return pl.BlockSpec(shape, lambda i: (0, 0))

    return pl.pallas_call(
        body, name="mla_mid_bwd", grid=(T // tm,),
        in_specs=[row(PW), full((1, QL)), full((1, KVL)), full(w_uq.shape), full(w_ukv.shape),
                  row(128), row(128), row(H * MLA_QK_PAD), row(H * MLA_QK_PAD), row(H * MLA_V)],
        out_specs=(row(H * MLA_QK_PAD), row(H * MLA_QK_PAD), row(PW), full((1, QL)), full((1, KVL))),
        out_shape=(jax.ShapeDtypeStruct((T, H * MLA_QK_PAD), BF16),
                   jax.ShapeDtypeStruct((T, H * MLA_QK_PAD), BF16),
                   jax.ShapeDtypeStruct((T, PW), BF16),
                   jax.ShapeDtypeStruct((1, QL), F32), jax.ShapeDtypeStruct((1, KVL), F32)),
        compiler_params=_cparams("arbitrary"))(proj, q_norm, kv_norm, w_uq, w_ukv, cc, ss, dq, dk, dv)


def _attn_tile(T):
    return min(1024, T)


def _attn_pairs(n, by_key):
    if by_key:
        pairs = [(qi, ki) for ki in range(n) for qi in range(ki, n)]
    else:
        pairs = [(qi, ki) for qi in range(n) for ki in range(qi + 1)]
    return (jnp.asarray([p[0] for p in pairs], jnp.int32), jnp.asarray([p[1] for p in pairs], jnp.int32))


def _scores(q, k, diagonal):
    s = lax.dot_general(q, k, (((1,), (1,)), ((), ())), preferred_element_type=F32)
    if diagonal:
        rows = lax.broadcasted_iota(jnp.int32, s.shape, 0)
        cols = lax.broadcasted_iota(jnp.int32, s.shape, 1)
        s = jnp.where(rows >= cols, s, -jnp.inf)
    return s


def _attn_fwd(q, k, v):
    T = q.shape[0]
    H, DQ, DV = MLA_HEADS, MLA_QK_PAD, MLA_V
    tq = _attn_tile(T)
    nq = T // tq
    G = MLA_HEADS_PER_STEP
    qi_tab, ki_tab = _attn_pairs(nq, by_key=False)

    def body(qi_ref, ki_ref, q_ref, k_ref, v_ref, o_ref, lse_ref, *scratch):
        m_refs, l_refs, acc_refs = scratch[0:G], scratch[G:2 * G], scratch[2 * G:3 * G]
        p = pl.program_id(1)
        qi, ki = qi_ref[p], ki_ref[p]

        @pl.when(ki == 0)
        def _():
            for g in range(G):
                m_refs[g][...] = jnp.full_like(m_refs[g], -jnp.inf)
                l_refs[g][...] = jnp.zeros_like(l_refs[g])
                acc_refs[g][...] = jnp.zeros_like(acc_refs[g])

        def update(ks, qr, masked):
            for g in range(G):
                qs, vs = slice(g * DQ, (g + 1) * DQ), slice(g * DV, (g + 1) * DV)
                st = _scores(k_ref[ks, qs], q_ref[qr, qs], False)
                if masked:
                    key = ks.start + lax.broadcasted_iota(jnp.int32, st.shape, 0)
                    qry = qr.start + lax.broadcasted_iota(jnp.int32, st.shape, 1)
                    st = jnp.where(qry >= key, st, -jnp.inf)
                m_prev = m_refs[g][:, qr]
                m_new = jnp.maximum(m_prev, jnp.max(st, axis=0, keepdims=True))
                alpha = jnp.exp(m_prev - m_new)
                pt = jnp.exp(st - m_new)
                l_refs[g][:, qr] = alpha * l_refs[g][:, qr] + jnp.sum(pt, axis=0, keepdims=True)
                acc_refs[g][:, qr] = alpha * acc_refs[g][:, qr] + lax.dot_general(
                    v_ref[ks, vs], pt.astype(BF16), (((0,), (0,)), ((), ())), preferred_element_type=F32)
                m_refs[g][:, qr] = m_new

        whole, half = slice(0, tq), tq // 2

        @pl.when(ki < qi)
        def _():
            update(whole, whole, False)

        @pl.when(ki == qi)
        def _():
            update(slice(0, half), whole, True)
            update(slice(half, tq), slice(half, tq), True)
            for g in range(G):
                vs = slice(g * DV, (g + 1) * DV)
                o_ref[:, vs] = jnp.transpose(acc_refs[g][...] / l_refs[g][...]).astype(BF16)
                lse_ref[g] = m_refs[g][...] + jnp.log(l_refs[g][...])

    return pl.pallas_call(
        body, name="attn_fwd",
        grid_spec=pltpu.PrefetchScalarGridSpec(
            num_scalar_prefetch=2, grid=(H // G, int(qi_tab.shape[0])),
            in_specs=[pl.BlockSpec((tq, G * DQ), lambda h, p, qt, kt: (qt[p], h)),
                      pl.BlockSpec((tq, G * DQ), lambda h, p, qt, kt: (kt[p], h)),
                      pl.BlockSpec((tq, G * DV), lambda h, p, qt, kt: (kt[p], h))],
            out_specs=(pl.BlockSpec((tq, G * DV), lambda h, p, qt, kt: (qt[p], h)),
                       pl.BlockSpec((G, 1, tq), lambda h, p, qt, kt: (h, 0, qt[p]))),
            scratch_shapes=([pltpu.VMEM((1, tq), F32)] * (2 * G) + [pltpu.VMEM((DV, tq), F32)] * G)),
        out_shape=(jax.ShapeDtypeStruct((T, H * DV), BF16), jax.ShapeDtypeStruct((H, 1, T), F32)),
        compiler_params=_cparams("parallel", "arbitrary"))(qi_tab, ki_tab, q, k, v)


def _attn_bwd(q, k, v, o, do, lse):
    T = q.shape[0]
    H, DQ, DV = MLA_HEADS, MLA_QK_PAD, MLA_V
    tq = _attn_tile(T)
    nq = T // tq
    tn = (((0,), (0,)), ((), ()))
    nt = (((1,), (1,)), ((), ()))
    G = MLA_HEADS_PER_STEP
    qi_tab, ki_tab = _attn_pairs(nq, by_key=True)

    def body(qi_ref, ki_ref, q_ref, k_ref, v_ref, o_ref, do_ref, lse_ref, dq_ref, dk_ref, dv_ref,
             dk_acc, dv_acc):
        p = pl.program_id(1)
        qi, ki = qi_ref[p], ki_ref[p]

        @pl.when(p == 0)
        def _():
            dq_ref[...] = jnp.zeros_like(dq_ref)

        @pl.when(qi == ki)
        def _():
            dk_acc[...] = jnp.zeros_like(dk_acc)
            dv_acc[...] = jnp.zeros_like(dv_acc)

        def step(ks, qr, masked):
            rows = pl.ds(pl.multiple_of(qi * tq + qr.start, qr.stop - qr.start), qr.stop - qr.start)
            for g in range(G):
                qs, vs = slice(g * DQ, (g + 1) * DQ), slice(g * DV, (g + 1) * DV)
                dof = do_ref[qr, vs]
                delta = jnp.sum(jnp.transpose(dof.astype(F32) * o_ref[qr, vs].astype(F32)), axis=0,
                                keepdims=True)
                st = _scores(k_ref[ks, qs], q_ref[qr, qs], False)
                if masked:
                    key = ks.start + lax.broadcasted_iota(jnp.int32, st.shape, 0)
                    qry = qr.start + lax.broadcasted_iota(jnp.int32, st.shape, 1)
                    st = jnp.where(qry >= key, st, -jnp.inf)
                pt = jnp.exp(st - lse_ref[g][:, qr])
                dpt = lax.dot_general(v_ref[ks, vs], dof, nt, preferred_element_type=F32)
                dst = (pt * (dpt - delta)).astype(BF16)
                dv_acc[ks, vs] += jnp.dot(pt.astype(BF16), dof, preferred_element_type=F32)
                dk_acc[ks, qs] += jnp.dot(dst, q_ref[qr, qs], preferred_element_type=F32)
                dq_ref[rows, qs] += lax.dot_general(dst, k_ref[ks, qs], tn, preferred_element_type=F32)

        whole, half = slice(0, tq), tq // 2

        @pl.when(qi == ki)
        def _():
            step(slice(0, half), whole, True)
            step(slice(half, tq), slice(half, tq), True)

        @pl.when(qi > ki)
        def _():
            step(whole, whole, False)

        @pl.when(qi == nq - 1)
        def _():
            dk_ref[...] = dk_acc[...]
            dv_ref[...] = dv_acc[...]

    qspec = pl.BlockSpec((tq, G * DQ), lambda h, p, qt, kt: (qt[p], h))
    ospec = pl.BlockSpec((tq, G * DV), lambda h, p, qt, kt: (qt[p], h))
    kspec = pl.BlockSpec((tq, G * DQ), lambda h, p, qt, kt: (kt[p], h))
    vspec = pl.BlockSpec((tq, G * DV), lambda h, p, qt, kt: (kt[p], h))
    return pl.pallas_call(
        body, name="attn_bwd",
        grid_spec=pltpu.PrefetchScalarGridSpec(
            num_scalar_prefetch=2, grid=(H // G, int(qi_tab.shape[0])),
            in_specs=[qspec, kspec, vspec, ospec, ospec,
                      pl.BlockSpec((G, 1, tq), lambda h, p, qt, kt: (h, 0, qt[p]))],
            out_specs=(pl.BlockSpec((T, G * DQ), lambda h, p, qt, kt: (0, h)), kspec, vspec),
            scratch_shapes=[pltpu.VMEM((tq, G * DQ), F32), pltpu.VMEM((tq, G * DV), F32)]),
        out_shape=(jax.ShapeDtypeStruct((T, H * DQ), F32), jax.ShapeDtypeStruct((T, H * DQ), F32),
                   jax.ShapeDtypeStruct((T, H * DV), F32)),
        compiler_params=_cparams("parallel", "arbitrary"))(qi_tab, ki_tab, q, k, v, o, do, lse)


def _mla_fwd(a, w, cc, ss, wbuf, pk, slot):
    D = a.shape[1]
    by_k, _, _ = _row_sharded(pk, "mla_w_o", slot, D // N_CHIPS)
    proj = _mm(a, w["w_in"], name="mla_in")
    cqn, ckvn, q, k, v = _mla_mid_fwd(proj, w["q_norm"], w["kv_norm"], w["w_uq"], w["w_ukv"], cc, ss)
    o, lse = _attn_fwd(q, k, v)
    m = _mm(o, wbuf, n=D, b_map=by_k, tm=2048, tk=D // N_CHIPS, tn=D, name="mla_out")
    return m, (a, proj, cqn, ckvn, q, k, v, o, lse)


def _mla_bwd(dm, saved, w, cc, ss, wbuf, gbuf, pk, slot):
    a, proj, cqn, ckvn, q, k, v, o, lse = saved
    D = a.shape[1]
    _, by_n, by_m = _row_sharded(pk, "mla_w_o", slot, D // N_CHIPS)
    do = _mm(dm, wbuf, tb=True, n=o.shape[1], b_map=by_n, tm=2048, tn=D // N_CHIPS, tk=D, out_dtype=BF16,
             name="mla_out_dx")
    gbuf = _mm(o, dm, ta=True, into=gbuf, o_map=by_m, tm=D // N_CHIPS, tn=D, tk=2048, name="mla_out_dw")
    dq, dk, dv = _attn_bwd(q, k, v, o, do, lse)
    dqp, dkv, dproj, dqn, dkn = _mla_mid_bwd(proj, w["q_norm"], w["kv_norm"], w["w_uq"], w["w_ukv"],
                                             cc, ss, dq, dk, dv)
    dw_uq = _mm(cqn, dqp, ta=True, out_dtype=BF16, name="mla_uq_dw")
    dw_ukv = _mm(ckvn, dkv, ta=True, out_dtype=BF16, name="mla_ukv_dw")
    dw_in = _mm(a, dproj, ta=True, out_dtype=BF16, name="mla_in_dw")
    da = _mm(dproj, w["w_in"], tb=True, name="mla_in_dx")
    return da, gbuf, dict(w_in=dw_in, w_uq=dw_uq, w_ukv=dw_ukv, q_norm=dqn, kv_norm=dkn)


def _split_dot(mat, x, parts):
    acc = None
    rem = x
    for _ in range(parts):
        piece = rem.astype(BF16)
        term = jnp.dot(mat, piece, preferred_element_type=F32)
        acc = term if acc is None else acc + term
        rem = rem - piece.astype(F32)
    return acc


def _chunk_sums(cum, rel, rest, logf):
    return tuple(_split_dot(m.astype(BF16), logf, 3) for m in (cum, rel, rest))


def _chunk_mats(tb):
    C = HGRN_CHUNK
    assert C & (C - 1) == 0
    r = lax.broadcasted_iota(jnp.int32, (tb, tb), 0)
    s = lax.broadcasted_iota(jnp.int32, (tb, tb), 1)
    start = r & ~(C - 1)
    same = start == (s & ~(C - 1))
    ref = start + C // 2
    last = start + C - 1
    one, zero = jnp.float32(1.0), jnp.float32(0.0)
    cum = jnp.where(same & (s <= r), one, zero)
    rel = cum - jnp.where(same & (s <= ref), one, zero)
    rest = jnp.where(same & (s > r) & (s <= last), one, zero)
    rev = jnp.where(same & (s >= r), one, zero)
    ones = jnp.where(same, one, zero)
    causal = same & (s <= r)
    return cum, rel, rest, rev, ones, causal


def _hgrn_gates(p_ref, lb, HK):
    qx = p_ref[:, 0:HK]
    fx = p_ref[:, HK:2 * HK]
    sf = _sigmoid(fx)
    f = lb + (1.0 - lb) * sf
    sq = _sigmoid(qx)
    return qx, sq, qx * sq, sf, f, 1.0 - f, jnp.log(f)


def _hgrn_fwd(proj, lb, o_norm):
    T = proj.shape[0]
    H, C = HGRN_HEADS, HGRN_CHUNK
    HK = proj.shape[1] // 4
    DK = HK // H
    tb = min(HGRN_BLOCK, T)
    ncb = tb // C
    nt = (((1,), (1,)), ((), ()))
    tn = (((0,), (0,)), ((), ()))

    def body(p_ref, lb_ref, on_ref, y_ref, o_ref, st_ref, state, oacc):
        @pl.when(pl.program_id(0) == 0)
        def _():
            state[...] = jnp.zeros_like(state)

        cum, rel, rest, _, _, causal = _chunk_mats(tb)
        _, _, q, _, f, k, logf = _hgrn_gates(p_ref, lb_ref[...], HK)
        b, brel, brest = _chunk_sums(cum, rel, rest, logf)
        eb = jnp.exp(b)
        q_rel = (q * jnp.exp(brel)).astype(BF16)
        k_rel = (k * jnp.exp(-brel)).astype(BF16)
        q_dec = (q * eb).astype(BF16)
        k_dec = (k * jnp.exp(brest)).astype(BF16)
        v = p_ref[:, 2 * HK:3 * HK].astype(BF16)
        for h in range(H):
            hs = slice(h * DK, (h + 1) * DK)
            a = lax.dot_general(q_rel[:, hs], k_rel[:, hs], nt, preferred_element_type=F32)
            a = jnp.where(causal, a, 0.0).astype(BF16)
            oacc[:, hs] = jnp.dot(a, v[:, hs], preferred_element_type=F32)
            for j in range(ncb):
                rs = slice(j * C, (j + 1) * C)
                st = state[h]
                st_ref[j, h] = st
                oacc[rs, hs] += lax.dot_general(q_dec[rs, hs], st.astype(BF16), nt,
                                                preferred_element_type=F32)
                dec = jnp.exp(jnp.sum(logf[rs, hs], axis=0, keepdims=True))
                state[h] = dec * st + lax.dot_general(v[rs, hs], k_dec[rs, hs], tn,
                                                      preferred_element_type=F32)
        o = oacc[...]
        o_ref[...] = o
        gx = p_ref[:, 3 * HK:4 * HK]
        gate = gx * _sigmoid(gx)
        for h in range(H):
            hs = slice(h * DK, (h + 1) * DK)
            oh = o[:, hs]
            y_ref[:, hs] = (oh * _rms_rstd(oh) * on_ref[...] * gate[:, hs]).astype(BF16)

    return pl.pallas_call(
        body, name="hgrn_fwd", grid=(T // tb,),
        in_specs=[pl.BlockSpec((tb, 4 * HK), lambda i: (i, 0)),
                  pl.BlockSpec((1, HK), lambda i: (0, 0)),
                  pl.BlockSpec((1, DK), lambda i: (0, 0))],
        out_specs=(pl.BlockSpec((tb, HK), lambda i: (i, 0)),
                   pl.BlockSpec((tb, HK), lambda i: (i, 0)),
                   pl.BlockSpec((ncb, H, DK, DK), lambda i: (i, 0, 0, 0))),
        out_shape=(jax.ShapeDtypeStruct((T, HK), BF16), jax.ShapeDtypeStruct((T, HK), F32),
                   jax.ShapeDtypeStruct((T // C, H, DK, DK), F32)),
        scratch_shapes=[pltpu.VMEM((H, DK, DK), F32), pltpu.VMEM((tb, HK), F32)],
        compiler_params=_cparams("arbitrary"))(proj, lb, o_norm)


def _hgrn_bwd(proj, lb, o_norm, o, states, dy):
    T = proj.shape[0]
    H, C = HGRN_HEADS, HGRN_CHUNK
    HK = proj.shape[1] // 4
    DK = HK // H
    tb = min(HGRN_BLOCK, T)
    ncb = tb // C
    nb = T // tb
    nt = (((1,), (1,)), ((), ()))
    tn = (((0,), (0,)), ((), ()))

    def body(p_ref, lb_ref, on_ref, o_ref, st_ref, dy_ref, dp_ref, dlb_ref, don_ref,
             dstate, dqr_s, dkr_s, dqd_s, dkd_s, dv_s, do_s, e_s):
        @pl.when(pl.program_id(0) == 0)
        def _():
            dstate[...] = jnp.zeros_like(dstate)
            dlb_ref[...] = jnp.zeros_like(dlb_ref)
            don_ref[...] = jnp.zeros_like(don_ref)

        cum, rel, rest, rev, ones, causal = _chunk_mats(tb)
        lb = lb_ref[...]
        qx, sq, q, sf, f, k, logf = _hgrn_gates(p_ref, lb, HK)
        b, brel, brest = _chunk_sums(cum, rel, rest, logf)
        eb = jnp.exp(b)
        erel = jnp.exp(brel)
        enrel = jnp.exp(-brel)
        erest = jnp.exp(brest)
        q_rel_f, k_rel_f, q_dec_f, k_dec_f = q * erel, k * enrel, q * eb, k * erest
        q_rel, k_rel = q_rel_f.astype(BF16), k_rel_f.astype(BF16)
        q_dec, k_dec = q_dec_f.astype(BF16), k_dec_f.astype(BF16)
        v = p_ref[:, 2 * HK:3 * HK].astype(BF16)

        gx = p_ref[:, 3 * HK:4 * HK]
        sg = _sigmoid(gx)
        gate = gx * sg
        dy = dy_ref[...]
        ov = o_ref[...]
        on = on_ref[...]
        don = jnp.zeros((1, DK), F32)
        for h in range(H):
            hs = slice(h * DK, (h + 1) * DK)
            oh = ov[:, hs]
            r = _rms_rstd(oh)
            xh = oh * r
            d_on = dy[:, hs] * gate[:, hs]
            don = don + jnp.sum(d_on * xh, axis=0, keepdims=True)
            u = d_on * on
            do_s[:, hs] = r * (u - xh * jnp.mean(u * xh, axis=-1, keepdims=True))
            dp_ref[:, 3 * HK + h * DK:3 * HK + (h + 1) * DK] = (
                dy[:, hs] * xh * on * (sg[:, hs] * (1.0 + gx[:, hs] * (1.0 - sg[:, hs])))).astype(BF16)
        don_ref[...] += don

        for h in range(H):
            hs = slice(h * DK, (h + 1) * DK)
            doh = do_s[:, hs].astype(BF16)
            a = lax.dot_general(q_rel[:, hs], k_rel[:, hs], nt, preferred_element_type=F32)
            a = jnp.where(causal, a, 0.0).astype(BF16)
            da = lax.dot_general(doh, v[:, hs], nt, preferred_element_type=F32)
            da = jnp.where(causal, da, 0.0).astype(BF16)
            dv_s[:, hs] = lax.dot_general(a, doh, tn, preferred_element_type=F32)
            dqr_s[:, hs] = jnp.dot(da, k_rel[:, hs], preferred_element_type=F32)
            dkr_s[:, hs] = lax.dot_general(da, q_rel[:, hs], tn, preferred_element_type=F32)
            for j in reversed(range(ncb)):
                rs = slice(j * C, (j + 1) * C)
                dst = dstate[h]
                dstb = dst.astype(BF16)
                st = st_ref[j, h]
                dkd_s[rs, hs] = jnp.dot(v[rs, hs], dstb, preferred_element_type=F32)
                dv_s[rs, hs] += lax.dot_general(k_dec[rs, hs], dstb, nt, preferred_element_type=F32)
                dec = jnp.exp(jnp.sum(logf[rs, hs], axis=0, keepdims=True))
                e_s[rs, hs] = jnp.broadcast_to(jnp.sum(dst * st, axis=0, keepdims=True) * dec, (C, DK))
                dqd_s[rs, hs] = jnp.dot(doh[rs], st.astype(BF16), preferred_element_type=F32)
                dstate[h] = dec * dst + lax.dot_general(doh[rs], q_dec[rs, hs], tn,
                                                        preferred_element_type=F32)

        dqr, dkr, dqd, dkd = dqr_s[...], dkr_s[...], dqd_s[...], dkd_s[...]
        kdk = dkd * k_dec_f
        db = dqr * q_rel_f - dkr * k_rel_f + dqd * q_dec_f - kdk
        dlogf = _split_dot(rev.astype(BF16), db, 2) + _split_dot(ones.astype(BF16), kdk, 2) + e_s[...]
        dk = dkr * enrel + dkd * erest
        df = dlogf / f - dk
        dlb_ref[...] += jnp.sum(df * (1.0 - sf), axis=0, keepdims=True)
        dq = dqr * erel + dqd * eb
        dp_ref[:, 0:HK] = (dq * (sq * (1.0 + qx * (1.0 - sq)))).astype(BF16)
        dp_ref[:, HK:2 * HK] = (df * (1.0 - lb) * sf * (1.0 - sf)).astype(BF16)
        dp_ref[:, 2 * HK:3 * HK] = dv_s[...].astype(BF16)

    rev_row = lambda w: pl.BlockSpec((tb, w), lambda i: (nb - 1 - i, 0))
    vec = lambda w: pl.BlockSpec((1, w), lambda i: (0, 0))
    scr = pltpu.VMEM((tb, HK), F32)
    return pl.pallas_call(
        body, name="hgrn_bwd", grid=(nb,),
        in_specs=[rev_row(4 * HK), vec(HK), vec(DK), rev_row(HK),
                  pl.BlockSpec((ncb, H, DK, DK), lambda i: (nb - 1 - i, 0, 0, 0)), rev_row(HK)],
        out_specs=(rev_row(4 * HK), vec(HK), vec(DK)),
        out_shape=(jax.ShapeDtypeStruct((T, 4 * HK), BF16), jax.ShapeDtypeStruct((1, HK), F32),
                   jax.ShapeDtypeStruct((1, DK), F32)),
        scratch_shapes=[pltpu.VMEM((H, DK, DK), F32), scr, scr, scr, scr, scr, scr, scr],
        compiler_params=_cparams("arbitrary"))(proj, lb, o_norm, o, states, dy)


def _hgrn_layer_fwd(a, o_norm, lb, wbuf, pk, slot):
    D = a.shape[1]
    in_by_n, _ = _col_sharded(pk, "hgrn_w_in", slot, D)
    out_by_k, _, _ = _row_sharded(pk, "hgrn_w_o", slot, D // N_CHIPS)
    proj = _mm(a, wbuf, n=4 * D, b_map=in_by_n, tk=D, tn=D, name="hgrn_in")
    y, o, states = _hgrn_fwd(proj, lb, o_norm)
    m = _mm(y, wbuf, n=D, b_map=out_by_k, tm=2048, tk=D // N_CHIPS, tn=D, name="hgrn_out")
    return m, (a, proj, y, o, states)


def _hgrn_layer_bwd(dm, saved, o_norm, lb, wbuf, gbuf, pk, slot):
    a, proj, y, o, states = saved
    D = a.shape[1]
    in_by_n, in_by_k = _col_sharded(pk, "hgrn_w_in", slot, D)
    _, out_by_n, out_by_m = _row_sharded(pk, "hgrn_w_o", slot, D // N_CHIPS)
    dy = _mm(dm, wbuf, tb=True, n=y.shape[1], b_map=out_by_n, tm=2048, tn=D // N_CHIPS, tk=D,
             name="hgrn_out_dx")
    gbuf = _mm(y, dm, ta=True, into=gbuf, o_map=out_by_m, tm=D // N_CHIPS, tn=D, tk=2048, name="hgrn_out_dw")
    dproj, dlb, don = _hgrn_bwd(proj, lb, o_norm, o, states, dy)
    gbuf = _mm(a, dproj, ta=True, into=gbuf, o_map=in_by_n, tm=D, tn=D, name="hgrn_in_dw")
    da = _mm(dproj, wbuf, tb=True, n=D, b_map=in_by_k, tn=D, tk=D, name="hgrn_in_dx")
    return da, gbuf, dict(o_norm=don, lb=dlb)


def _lower_bounds(lb_logits):
    p = jax.nn.softmax(lb_logits.astype(F32), axis=0)
    return jnp.cumsum(p, axis=0) - p[0]


def _rope_tables(positions):
    inv_freq = jnp.power(ROPE_BASE, -jnp.arange(0, MLA_ROPE, 2, dtype=F32) / MLA_ROPE)
    ang = positions.astype(F32)[:, None] * inv_freq
    cos, sin = jnp.cos(ang), jnp.sin(ang)
    zero = jnp.zeros((positions.shape[0], 128 - MLA_ROPE), F32)
    return (jnp.concatenate([cos, cos, zero], axis=-1), jnp.concatenate([-sin, sin, zero], axis=-1))


def _pad_mla_weights(w_in, w_uq):
    w_in_p = jnp.pad(w_in, ((0, 0), (0, 0), (0, 128 - MLA_ROPE)))
    n, ql, _ = w_uq.shape
    w_uq_p = jnp.pad(w_uq.reshape(n, ql, MLA_HEADS, MLA_NOPE + MLA_ROPE),
                     ((0, 0), (0, 0), (0, 0), (0, MLA_QK_PAD - MLA_NOPE - MLA_ROPE)))
    return w_in_p, w_uq_p.reshape(n, ql, MLA_HEADS * MLA_QK_PAD)


def _local_step(x, positions, target, small, fetch, gbufs, emit, emit_mlp):
    T, D = x.shape
    lbounds, lb_vjp = jax.vjp(_lower_bounds, small["hgrn_lb_logits"])
    cc, ss = _rope_tables(positions)
    fetched = {0: fetch(0, None)}
    gains = fetched[0]["gains"]
    tick = [jnp.zeros((), F32)]

    def g(layer, i):
        return gains[layer, i][None, :] + tick[0]

    def mla_weights(layer):
        f = fetched[layer]
        w_in_p, w_uq_p = _pad_mla_weights(f["w_in"][None], f["w_uq"][None])
        slot = layer // 2
        return dict(w_in=w_in_p[0], w_uq=w_uq_p[0], w_ukv=f["w_ukv"],
                    q_norm=small["mla_q_norm"][slot][None, :], kv_norm=small["mla_kv_norm"][slot][None, :])

    saved = []
    h = x
    a = _prenorm_fwd(x, g(0, 0))
    dy = sq = None
    for layer in range(DEPTH):
        slot = layer // 2
        if layer not in fetched:
            fetched[layer] = fetch(layer, a)
        wbuf, pk = fetched[layer]["wbuf"], fetched[layer]["pk"]
        if layer % 2 == 0:
            m, mix_saved = _mla_fwd(a, mla_weights(layer), cc, ss, wbuf, pk, slot)
        else:
            m, mix_saved = _hgrn_layer_fwd(a, small["hgrn_o_norm"][slot][None, :], lbounds[layer][None, :],
                                           wbuf, pk, slot)
        h1, a2 = _resnorm_fwd(h, m, g(layer, 1), g(layer, 2), name="resnorm_fwd_mix")
        u, mlp_saved = _mlp_fwd(a2, wbuf, pk, layer)
        if layer + 1 < DEPTH:
            h2, a = _resnorm_fwd(h1, u, g(layer, 3), g(layer + 1, 0), name="resnorm_fwd_mlp")
        else:
            h2 = None
            dy, sq = _resnorm_loss(h1, u, g(layer, 3), target)
        saved.append((h, m, h1, u, mix_saved, mlp_saved))
        h = h2

    n_mla, n_hgrn = (DEPTH + 1) // 2, DEPTH // 2
    dgains = [[None] * 4 for _ in range(DEPTH)]
    gw = {k: [None] * n_mla for k in ("mla_w_in", "mla_w_uq", "mla_w_ukv", "mla_q_norm", "mla_kv_norm")}
    gw["hgrn_o_norm"] = [None] * n_hgrn
    dlb = [jnp.zeros((1, lbounds.shape[1]), F32) for _ in range(DEPTH)]
    dh = dy
    da_next = None
    for layer in reversed(range(DEPTH)):
        h0, m, h1, u, mix_saved, mlp_saved = saved[layer]
        slot = layer // 2
        wbuf, pk, gbuf = fetched[layer]["wbuf"], fetched[layer]["pk"], gbufs[layer]
        if da_next is None:
            du, dgains[layer][3] = _resnorm_bwd(u, g(layer, 3), dh, name="resnorm_bwd_last")
            t = dh
        else:
            h2 = saved[layer + 1][0]
            t, du, dgains[layer][3], dgains[layer + 1][0] = _resnorm_bwd(
                u, g(layer, 3), dh, h2, da_next, g(layer + 1, 0), name="resnorm_bwd_mlp")
        da2, gbuf = _mlp_bwd(du, mlp_saved, wbuf, gbuf, pk, layer)
        if layer == 0:
            gbuf = emit_mlp(layer, gbuf)
        t, dm, dgains[layer][1], dgains[layer][2] = _resnorm_bwd(
            m, g(layer, 1), t, h1, da2, g(layer, 2), name="resnorm_bwd_mix")
        if layer % 2 == 0:
            da_next, gbuf, mg = _mla_bwd(dm, mix_saved, mla_weights(layer), cc, ss, wbuf, gbuf, pk, slot)
            ql = mg["q_norm"].shape[-1]
            kvl = mg["kv_norm"].shape[-1]
            gw["mla_w_in"][slot] = mg["w_in"][:, :ql + kvl + MLA_ROPE]
            gw["mla_w_uq"][slot] = mg["w_uq"].reshape(ql, MLA_HEADS, MLA_QK_PAD)[
                :, :, :MLA_NOPE + MLA_ROPE].reshape(ql, MLA_HEADS * (MLA_NOPE + MLA_ROPE))
            gw["mla_w_ukv"][slot] = mg["w_ukv"]
            gw["mla_q_norm"][slot] = mg["q_norm"][0]
            gw["mla_kv_norm"][slot] = mg["kv_norm"][0]
        else:
            da_next, gbuf, hg = _hgrn_layer_bwd(dm, mix_saved, small["hgrn_o_norm"][slot][None, :],
                                                lbounds[layer][None, :], wbuf, gbuf, pk, slot)
            gw["hgrn_o_norm"][slot] = hg["o_norm"][0]
            dlb[layer] = hg["lb"]
        dh = t
        if layer > 0:
            mine = ({k: gw[k][slot] for k in ("mla_w_in", "mla_w_uq", "mla_w_ukv")} if layer % 2 == 0 else {})
            tick[0] = emit(layer, gbuf, mine)
        else:
            gbuf0 = gbuf
    grad_x, dgains[0][0] = _prenorm_bwd(x, g(0, 0), dh, da_next)

    last = {k: gw[k][0] for k in ("mla_w_in", "mla_w_uq", "mla_w_ukv")}
    last.update({k: jnp.stack(gw[k]) for k in ("mla_q_norm", "mla_kv_norm", "hgrn_o_norm")})
    last["norm_gains"] = jnp.stack([jnp.concatenate(row, axis=0) for row in dgains])
    (last["hgrn_lb_logits"],) = lb_vjp(jnp.concatenate(dlb, axis=0))
    emit(0, gbuf0, last)
    return sq, grad_x


def _size(shape):
    n = 1
    for d in shape:
        n *= d
    return n


def _piece_rows(shape):
    return -(-_size(shape) // PACK_W)


def _packed_misc_rows(shapes):
    return sum(_piece_rows(s) for s in shapes)


def _cast_into(src, buf, row, name):
    rows, W = src.shape
    tr = min(256, rows)
    assert rows % tr == 0 and row % tr == 0

    def body(s_ref, b_ref, o_ref):
        o_ref[...] = s_ref[...].astype(BF16)

    return pl.pallas_call(
        body, name=name, grid=(rows // tr,),
        in_specs=[pl.BlockSpec((tr, W), lambda i: (i, 0)), pl.BlockSpec(memory_space=pl.ANY)],
        out_specs=pl.BlockSpec((tr, W), lambda i: (row // tr + i, 0)),
        out_shape=jax.ShapeDtypeStruct(buf.shape, buf.dtype), input_output_aliases={1: 0},
        compiler_params=_cparams("parallel"))(src, buf)


def _pack_blocks(pieces, rows, dtype):
    blocks, used = [], 0
    for p in pieces:
        flat = p.astype(dtype).reshape(-1)
        r = _piece_rows(p.shape)
        if r * PACK_W != flat.shape[0]:
            flat = jnp.pad(flat, (0, r * PACK_W - flat.shape[0]))
        blocks.append(flat.reshape(r, PACK_W))
        used += r
    if rows > used:
        blocks.append(jnp.zeros((rows - used, PACK_W), dtype))
    return blocks


def _unpack(buf, shapes):
    out, off = [], 0
    for shp in shapes:
        r = _piece_rows(shp)
        piece = buf[off:off + r]
        if r * PACK_W != _size(shp):
            piece = piece.reshape(-1)[:_size(shp)]
        out.append(piece.reshape(shp))
        off += r
    return out


def _mesh_place():
    x, y, c = lax.axis_index("x"), lax.axis_index("y"), lax.axis_index("c")
    chips = [(1 - x, y), (x, 1 - y), (1 - x, 1 - y)]
    return x, y, c, chips


_HBM = pl.BlockSpec(memory_space=pltpu.HBM)


def _share_reduced(q, name="grads_share_reduced"):
    rh, W = q.shape

    def body(q_ref, out_ref, send_sem, recv_sem):
        x, y, c, _ = _mesh_place()
        cp = pltpu.make_async_remote_copy(src_ref=q_ref, dst_ref=out_ref.at[c], send_sem=send_sem,
                                          recv_sem=recv_sem, device_id=(x, y, 1 - c), device_id_type=MESH)
        cp.start()
        cp.wait()

    out = pl.pallas_call(
        body, name=name, in_specs=[_HBM], out_specs=_HBM,
        out_shape=jax.ShapeDtypeStruct((2, rh, W), q.dtype),
        scratch_shapes=[pltpu.SemaphoreType.DMA, pltpu.SemaphoreType.DMA],
    )(q)
    return out


def _sum_chips(parts, own, own_row0, which, name, out_dtype=F32):
    n, rh, W = parts.shape
    tr = PACK_TILE
    assert own_row0 % tr == 0
    if own.ndim == 3:
        own_spec = pl.BlockSpec((None, tr, W), lambda i, w_ref: (w_ref[0], own_row0 // tr + i, 0))
    else:
        own_spec = pl.BlockSpec((tr, W), lambda i, w_ref: (own_row0 // tr + i, 0))

    def body(w_ref, p_ref, own_ref, o_ref):
        mine = own_ref[...].astype(F32)
        acc = None
        for j in range(n):
            term = jnp.where(w_ref[0] == j, mine, p_ref[j].astype(F32))
            acc = term if acc is None else acc + term
        o_ref[...] = acc.astype(out_dtype)

    return pl.pallas_call(
        body, name=name,
        grid_spec=pltpu.PrefetchScalarGridSpec(
            num_scalar_prefetch=1, grid=(rh // tr,),
            in_specs=[pl.BlockSpec((n, tr, W), lambda i, w_ref: (0, i, 0)), own_spec],
            out_specs=pl.BlockSpec((tr, W), lambda i, w_ref: (i, 0))),
        out_shape=jax.ShapeDtypeStruct((rh, W), out_dtype),
        compiler_params=_cparams("parallel"))(jnp.reshape(which, (1,)).astype(jnp.int32), parts, own)


_SEM = pl.BlockSpec(memory_space=pltpu.SEMAPHORE)
_ASYNC = pltpu.CompilerParams(has_side_effects=pltpu.SideEffectType.DATAFLOW_SIDE_EFFECTING)


def _hbm(a):
    return pltpu.with_memory_space_constraint(a, pltpu.HBM)


def _gather_copies(w_ref, land_ref, send_sems, recv_sems, both_cores):
    x, y, c, chips = _mesh_place()
    me = 2 * x + y
    rh = w_ref.shape[0] // 2
    rows = pl.ds(pl.multiple_of(c * rh, 16), rh)
    if both_cores:
        return [pltpu.make_async_remote_copy(
            src_ref=w_ref.at[rows], dst_ref=land_ref.at[me, rows], send_sem=send_sems.at[2 * r + core],
            recv_sem=recv_sems.at[2 * r + c], device_id=(px, py, core), device_id_type=MESH)
            for r, (px, py) in enumerate(chips) for core in range(2)]
    return [pltpu.make_async_remote_copy(
        src_ref=w_ref.at[rows], dst_ref=land_ref.at[me, rows], send_sem=send_sems.at[r],
        recv_sem=recv_sems.at[r], device_id=(px, py, c), device_id_type=MESH)
        for r, (px, py) in enumerate(chips)]


def _scatter_copies(g_ref, land_ref, send_sems, recv_sems, row0):
    x, y, c, chips = _mesh_place()
    me = 2 * x + y
    rows = pl.ds(row0, land_ref.shape[1])
    return [pltpu.make_async_remote_copy(
        src_ref=g_ref.at[2 * px + py, rows], dst_ref=land_ref.at[me], send_sem=send_sems.at[r],
        recv_sem=recv_sems.at[r], device_id=(px, py, c), device_id_type=MESH)
        for r, (px, py) in enumerate(chips)]


def _halves_to_sibling(land, name):
    n, R, W = land.shape
    rh = R // 2

    def body(l_ref, o_ref, send_sems, recv_sems):
        x, y, c, chips = _mesh_place()
        rows = pl.ds(pl.multiple_of(c * rh, 16), rh)
        copies = [pltpu.make_async_remote_copy(
            src_ref=o_ref.at[2 * px + py, rows], dst_ref=o_ref.at[2 * px + py, rows], send_sem=send_sems.at[r],
            recv_sem=recv_sems.at[r], device_id=(x, y, 1 - c), device_id_type=MESH)
            for r, (px, py) in enumerate(chips)]
        for cp in copies:
            cp.start()
        for cp in copies:
            cp.wait()

    return pl.pallas_call(
        body, name=name, in_specs=[_HBM], out_specs=_HBM, out_shape=jax.ShapeDtypeStruct(land.shape, land.dtype),
        scratch_shapes=[pltpu.SemaphoreType.DMA((3,)), pltpu.SemaphoreType.DMA((3,))],
        input_output_aliases={0: 0})(land)


def _gather_start(wp, both_cores, name):
    R, W = wp.shape
    n = 6 if both_cores else 3

    def body(w_ref, land_ref, send_sems, recv_sems, w_thru, land_thru, token):
        for cp in _gather_copies(w_ref, land_ref, send_sems, recv_sems, both_cores):
            cp.start()
        token[...] = jnp.zeros_like(token)

    return pl.pallas_call(
        body, name=name,
        out_shape=(pltpu.SemaphoreType.DMA((n,)), pltpu.SemaphoreType.DMA((n,)), pltpu.HBM(wp.shape, wp.dtype),
                   pltpu.HBM((N_CHIPS, R, W), wp.dtype), jax.ShapeDtypeStruct((8, 128), F32)),
        in_specs=(_HBM, _HBM),
        out_specs=(_SEM, _SEM, _HBM, _HBM, pl.BlockSpec(memory_space=pltpu.VMEM)),
        input_output_aliases={0: 2, 1: 3}, compiler_params=_ASYNC,
    )(_hbm(wp), _hbm(lax.empty((N_CHIPS, R, W), wp.dtype)))


def _gather_wait(send_sems, recv_sems, w_thru, land_thru, after, both_cores, name):
    R, W = w_thru.shape
    rh = R // 2

    def body(w_ref, land_ref, send_sems, recv_sems, after_ref, w_dead, got_ref):
        x, y, c, _ = _mesh_place()
        half = land_ref.at[0, pl.ds(0, rh)]
        for k in range(6 if both_cores else 3):
            cp = pltpu.make_async_remote_copy(src_ref=half, dst_ref=half, send_sem=send_sems.at[k],
                                              recv_sem=recv_sems.at[k], device_id=(x, y, 1 - c),
                                              device_id_type=MESH)
            cp.wait_send()
            cp.wait_recv()

    return pl.pallas_call(
        body, name=name,
        out_shape=(pltpu.HBM(w_thru.shape, w_thru.dtype), pltpu.HBM(land_thru.shape, land_thru.dtype)),
        in_specs=(_HBM, _HBM, _SEM, _SEM, pl.BlockSpec(memory_space=pl.ANY)), out_specs=(_HBM, _HBM),
        input_output_aliases={0: 0, 1: 1}, compiler_params=_ASYNC,
    )(w_thru, land_thru, send_sems, recv_sems, after)


def _scatter_start(g, row0, nrows, name):
    n, R, W = g.shape
    land_shape = (n, nrows, W)

    def body(g_ref, land_ref, send_sems, recv_sems, g_thru, land_thru, token):
        for cp in _scatter_copies(g_ref, land_ref, send_sems, recv_sems, row0):
            cp.start()
        token[...] = jnp.zeros_like(token)

    return pl.pallas_call(
        body, name=name,
        out_shape=(pltpu.SemaphoreType.DMA((3,)), pltpu.SemaphoreType.DMA((3,)), pltpu.HBM(g.shape, g.dtype),
                   pltpu.HBM(land_shape, g.dtype), jax.ShapeDtypeStruct((8, 128), F32)),
        in_specs=(_HBM, _HBM),
        out_specs=(_SEM, _SEM, _HBM, _HBM, pl.BlockSpec(memory_space=pltpu.VMEM)),
        input_output_aliases={0: 2, 1: 3}, compiler_params=_ASYNC,
    )(_hbm(g), _hbm(lax.empty(land_shape, g.dtype)))


def _scatter_wait(send_sems, recv_sems, g_thru, land_thru, after, name):
    def body(g_ref, land_ref, send_sems, recv_sems, after_ref, g_out, got_ref):
        x, y, c, _ = _mesh_place()
        for k in range(3):
            cp = pltpu.make_async_remote_copy(src_ref=land_ref.at[0], dst_ref=land_ref.at[0], send_sem=send_sems.at[k],
                                              recv_sem=recv_sems.at[k], device_id=(x, y, 1 - c),
                                              device_id_type=MESH)
            cp.wait_send()
            cp.wait_recv()

    return pl.pallas_call(
        body, name=name,
        out_shape=(pltpu.HBM(g_thru.shape, g_thru.dtype), pltpu.HBM(land_thru.shape, land_thru.dtype)),
        in_specs=(_HBM, _HBM, _SEM, _SEM, pl.BlockSpec(memory_space=pl.ANY)), out_specs=(_HBM, _HBM),
        input_output_aliases={0: 0, 1: 1}, compiler_params=_ASYNC,
    )(g_thru, land_thru, send_sems, recv_sems, after)


def _adamw(w, g, m, v, name):
    shape = w.shape
    cols = shape[-1]
    w2, g2, m2, v2 = (t.reshape(-1, cols) for t in (w, g, m, v))
    rows = w2.shape[0]
    tr = rows
    for cand in (512, 256, 128, 64, 32, 16, 8):
        if rows > cand and rows % cand == 0:
            tr = cand
            break
    c1 = 1.0 / (1.0 - ADAM_B1 ** ADAM_STEP)
    c2 = 1.0 / (1.0 - ADAM_B2 ** ADAM_STEP)

    def body(w_ref, g_ref, m_ref, v_ref, d_ref, nm_ref, nv_ref):
        gv = g_ref[...]
        nm = ADAM_B1 * m_ref[...] + (1.0 - ADAM_B1) * gv
        nv = ADAM_B2 * v_ref[...] + (1.0 - ADAM_B2) * (gv * gv)
        nm_ref[...] = nm
        nv_ref[...] = nv
        d_ref[...] = -ADAM_LR * ((nm * c1) / (jnp.sqrt(nv * c2) + ADAM_EPS) + ADAM_WD * w_ref[...])

    blk = pl.BlockSpec((tr, cols), lambda i: (i, 0))
    sds = jax.ShapeDtypeStruct((rows, cols), F32)
    d, nm, nv = pl.pallas_call(body, name=name, grid=(rows // tr,), in_specs=[blk] * 4,
                               out_specs=(blk, blk, blk), out_shape=(sds, sds, sds),
                               compiler_params=_cparams("parallel"))(w2, g2, m2, v2)
    return d.reshape(shape), nm.reshape(shape), nv.reshape(shape)


def kernel(x, positions, norm_gains, mla_w_in, mla_q_norm, mla_kv_norm, mla_w_uq, mla_w_ukv, mla_w_o, hgrn_w_in, hgrn_lb_logits, hgrn_o_norm, hgrn_w_o, mlp_w1, mlp_w2, loss_target, m_norm_gains, m_mla_w_in, m_mla_q_norm, m_mla_kv_norm, m_mla_w_uq, m_mla_w_ukv, m_mla_w_o, m_hgrn_w_in, m_hgrn_lb_logits, m_hgrn_o_norm, m_hgrn_w_o, m_mlp_w1, m_mlp_w2, v_norm_gains, v_mla_w_in, v_mla_q_norm, v_mla_kv_norm, v_mla_w_uq, v_mla_w_ukv, v_mla_w_o, v_hgrn_w_in, v_hgrn_lb_logits, v_hgrn_o_norm, v_hgrn_w_o, v_mlp_w1, v_mlp_w2):
    w = dict(norm_gains=norm_gains, mla_w_in=mla_w_in, mla_q_norm=mla_q_norm, mla_kv_norm=mla_kv_norm,
             mla_w_uq=mla_w_uq, mla_w_ukv=mla_w_ukv, mla_w_o=mla_w_o, hgrn_w_in=hgrn_w_in,
             hgrn_lb_logits=hgrn_lb_logits, hgrn_o_norm=hgrn_o_norm, hgrn_w_o=hgrn_w_o,
             mlp_w1=mlp_w1, mlp_w2=mlp_w2)
    mom_m = dict(norm_gains=m_norm_gains, mla_w_in=m_mla_w_in, mla_q_norm=m_mla_q_norm,
                 mla_kv_norm=m_mla_kv_norm, mla_w_uq=m_mla_w_uq, mla_w_ukv=m_mla_w_ukv,
                 mla_w_o=m_mla_w_o, hgrn_w_in=m_hgrn_w_in, hgrn_lb_logits=m_hgrn_lb_logits,
                 hgrn_o_norm=m_hgrn_o_norm, hgrn_w_o=m_hgrn_w_o, mlp_w1=m_mlp_w1, mlp_w2=m_mlp_w2)
    mom_v = dict(norm_gains=v_norm_gains, mla_w_in=v_mla_w_in, mla_q_norm=v_mla_q_norm,
                 mla_kv_norm=v_mla_kv_norm, mla_w_uq=v_mla_w_uq, mla_w_ukv=v_mla_w_ukv,
                 mla_w_o=v_mla_w_o, hgrn_w_in=v_hgrn_w_in, hgrn_lb_logits=v_hgrn_lb_logits,
                 hgrn_o_norm=v_hgrn_o_norm, hgrn_w_o=v_hgrn_w_o, mlp_w1=v_mlp_w1, mlp_w2=v_mlp_w2)
    c = lax.axis_index("c")

    axis_of = dict(SHARDED)
    me = 2 * lax.axis_index("x") + lax.axis_index("y")
    gain_bits = lax.bitcast_convert_type(norm_gains, jnp.uint32)
    gain_hi = lax.bitcast_convert_type((gain_bits >> 16).astype(jnp.uint16), BF16)
    gain_lo = lax.bitcast_convert_type((gain_bits & 0xFFFF).astype(jnp.uint16), BF16)

    layers = []
    for l in range(DEPTH):
        s = l // 2
        if l % 2 == 0:
            big = [("mlp_w1", l), ("mlp_w2", l), ("mla_w_o", s)]
            tail = [("mla_w_in", s), ("mla_w_uq", s), ("mla_w_ukv", s)]
        else:
            big = [("hgrn_w_in", s), ("mlp_w1", l), ("mlp_w2", l), ("hgrn_w_o", s)]
            tail = []
        w_tail = [w[n][i] for n, i in tail] + ([gain_hi, gain_lo] if l == 0 else [])
        g_tail = tail + ([("norm_gains", None)] + [(n, None) for n in REPLICATED] if l == 0 else [])
        g_shapes = [w[n].shape if i is None else w[n][i].shape for n, i in g_tail]
        tail_rows = max(_packed_misc_rows([t.shape for t in w_tail]), _packed_misc_rows(g_shapes))
        pk = _Packed([(n, w[n].shape[1]) for n, _ in big], tail_rows)
        wpack = jnp.zeros((pk.rows, PACK_W), BF16)
        for n, i in big:
            assert w[n].shape[2] == PACK_W
            wpack = _cast_into(w[n][i], wpack, pk.off[n], name="pack_%s_%d" % (n, l))
        if w_tail:
            wpack = lax.dynamic_update_slice(
                wpack, jnp.concatenate(_pack_blocks(w_tail, 0, BF16), axis=0), (pk.misc, 0))
        layers.append(dict(pk=pk, big=big, tail=tail, w_tail=w_tail, g_tail=g_tail, g_shapes=g_shapes,
                           both=l >= 2, gather=_gather_start(wpack, l >= 2, name="gather_start_%d" % l)))

    def fetch(l, after):
        lay = layers[l]
        pk = lay["pk"]
        send_sems, recv_sems, w_thru, land_thru, _ = lay["gather"]
        if after is None:
            after = sum(layers[k]["gather"][4] for k in range(1, DEPTH))
        w_back, land = _gather_wait(send_sems, recv_sems, w_thru, land_thru, after, lay["both"],
                                    name="gather_wait_%d" % l)
        if not lay["both"]:
            land = _halves_to_sibling(land, name="gather_halves_%d" % l)
        land = lax.dynamic_update_slice(land, w_back[None], (me, 0, 0))
        out = dict(wbuf=land.reshape(N_CHIPS * pk.rows, PACK_W), pk=pk)
        if lay["w_tail"]:
            rows = _packed_misc_rows([t.shape for t in lay["w_tail"]])
            per_chip = [_unpack(land[j, pk.misc:pk.misc + rows], [t.shape for t in lay["w_tail"]])
                        for j in range(N_CHIPS)]
            for i, (n, _) in enumerate(lay["tail"]):
                out[n[4:]] = jnp.concatenate([per_chip[j][i] for j in range(N_CHIPS)], axis=axis_of[n] - 1)
            if l == 0:
                got_hi, got_lo = (lax.bitcast_convert_type(
                    jnp.concatenate([per_chip[j][i] for j in range(N_CHIPS)], axis=2),
                    jnp.uint16).astype(jnp.uint32) for i in (-2, -1))
                out["gains"] = lax.bitcast_convert_type((got_hi << 16) | got_lo, F32)
        return out

    def emit(l, gbuf, grads):
        lay = layers[l]
        pk = lay["pk"]
        if lay["g_tail"]:
            for j in range(N_CHIPS):
                pieces = []
                for n, i in lay["g_tail"]:
                    if n not in axis_of:
                        pieces.append(grads[n])
                    else:
                        pieces.append(jnp.split(grads[n], N_CHIPS, axis=axis_of[n] - (0 if i is None else 1))[j])
                block = jnp.concatenate(_pack_blocks(pieces, 0, BF16), axis=0)
                gbuf = lax.dynamic_update_slice(gbuf, block, (j * pk.rows + pk.misc, 0))
        row0 = lay.get("early_rows", 0)
        lay["scatter"] = _scatter_start(gbuf.reshape(N_CHIPS, pk.rows, PACK_W), row0, pk.rows - row0,
                                        name="scatter_start_%d" % l)
        return lay["scatter"][4][0, 0]

    def emit_mlp(l, gbuf):
        lay = layers[l]
        pk = lay["pk"]
        assert pk.off["mlp_w1"] == 0 and pk.off["mlp_w2"] == w["mlp_w1"].shape[1]
        lay["early_rows"] = w["mlp_w1"].shape[1] + w["mlp_w2"].shape[1]
        lay["scatter_early"] = _scatter_start(gbuf.reshape(N_CHIPS, pk.rows, PACK_W), 0, lay["early_rows"],
                                              name="scatter_start_%d_mlp" % l)
        return lay["scatter_early"][2].reshape(N_CHIPS * pk.rows, PACK_W)

    small = dict(mla_q_norm=mla_q_norm, mla_kv_norm=mla_kv_norm, hgrn_lb_logits=hgrn_lb_logits,
                 hgrn_o_norm=hgrn_o_norm)
    gbufs = [lax.empty((N_CHIPS * lay["pk"].rows, PACK_W), BF16) for lay in layers]
    sq, grad_x = _local_step(x[0], positions[0], loss_target[0], small, fetch, gbufs, emit, emit_mlp)
    d_model = x.shape[-1]
    loss = lax.psum(0.5 * jnp.sum(sq) / d_model, ("x", "y", "c"))

    per_name = {}
    behind = grad_x
    for l, lay in reversed(list(enumerate(layers))):
        pk = lay["pk"]
        send_sems, recv_sems, g_thru, land_thru, _ = lay["scatter"]
        row0 = lay.get("early_rows", 0)
        early = None
        if row0:
            e_send, e_recv, _, e_land, _ = lay["scatter_early"]
            g_thru, land = _scatter_wait(e_send, e_recv, g_thru, e_land, behind, name="scatter_wait_%d_mlp" % l)
            early = behind = _sum_chips(land, g_thru, 0, me, name="grads_sum_chips_%d_mlp" % l, out_dtype=BF16)
        g_back, land = _scatter_wait(send_sems, recv_sems, g_thru, land_thru, behind, name="scatter_wait_%d" % l)
        mine = _sum_chips(land, g_back, row0, me, name="grads_sum_chips_%d" % l, out_dtype=BF16)
        if early is not None:
            mine = jnp.concatenate([early, mine], axis=0)
        red = behind = _sum_chips(_share_reduced(mine, name="grads_share_%d" % l), mine, 0, c,
                                  name="grads_sum_cores_%d" % l)
        for n, i in lay["big"]:
            per_name.setdefault(n, {})[i] = red[pk.off[n]:pk.off[n] + w[n].shape[1]]
        for (n, i), piece in zip(lay["g_tail"], _unpack(red[pk.misc:pk.misc + pk.misc_rows], lay["g_shapes"])):
            per_name.setdefault(n, {})[i] = piece
    g_out = {n: (parts[None] if None in parts else jnp.stack([parts[i] for i in sorted(parts)]))
             for n, parts in per_name.items()}

    deltas, new_m, new_v = {}, {}, {}
    for name in WEIGHTS:
        deltas[name], new_m[name], new_v[name] = _adamw(w[name], g_out[name], mom_m[name], mom_v[name],
                                                        name="adamw_" + name)
    return (loss, grad_x[None], *[g_out[n] for n in WEIGHTS], *[deltas[n] for n in WEIGHTS],
            *[new_m[n] for n in WEIGHTS], *[new_v[n] for n in WEIGHTS])
```

```python
import jax
import jax.numpy as jnp
from jax import lax
from jax.experimental import pallas as pl
from jax.experimental.pallas import tpu as pltpu

F32 = jnp.float32
BF16 = jnp.bfloat16
MESH = pl.DeviceIdType.MESH

DEPTH = 4
MLA_HEADS = 8
MLA_NOPE = 128
MLA_ROPE = 64
MLA_V = 128
MLA_QK_PAD = 256
MLA_HEADS_PER_STEP = 2
MLA_SCALE = float(MLA_NOPE + MLA_ROPE) ** -0.5
ROPE_BASE = 10000.0
HGRN_HEADS = 8
HGRN_CHUNK = 32
HGRN_BLOCK = 128
EPS = 1e-6

ADAM_LR = 0.001
ADAM_B1 = 0.9
ADAM_B2 = 0.999
ADAM_EPS = 1e-08
ADAM_WD = 0.01
ADAM_STEP = 10

N_CHIPS = 4
PACK_W = 1024
PACK_ALIGN = 1024
PACK_TILE = 512
V7X_VMEM_LIMIT = 56 * 1024 * 1024

SHARDED = (("norm_gains", 2), ("mla_w_in", 1), ("mla_w_uq", 2), ("mla_w_ukv", 2), ("mla_w_o", 1),
           ("hgrn_w_in", 2), ("hgrn_w_o", 1), ("mlp_w1", 2), ("mlp_w2", 1))
REPLICATED = ("mla_q_norm", "mla_kv_norm", "hgrn_lb_logits", "hgrn_o_norm")
WEIGHTS = ("norm_gains", "mla_w_in", "mla_q_norm", "mla_kv_norm", "mla_w_uq", "mla_w_ukv", "mla_w_o",
           "hgrn_w_in", "hgrn_lb_logits", "hgrn_o_norm", "hgrn_w_o", "mlp_w1", "mlp_w2")


def _cparams(*semantics):
    return pltpu.CompilerParams(dimension_semantics=semantics, vmem_limit_bytes=V7X_VMEM_LIMIT)


def _sigmoid(x):
    return 0.5 * jnp.tanh(0.5 * x) + 0.5


def _mm(a, b, *, ta=False, tb=False, out_dtype=F32, tm=2048, tn=1024, tk=1024, epi=None, extra=None,
        name="mm", n=None, b_map=None, into=None, o_map=None):
    if ta:
        K, M = a.shape
    else:
        M, K = a.shape
    if b_map is not None:
        N = n
    elif tb:
        N, Kb = b.shape
    else:
        Kb, N = b.shape
    assert b_map is not None or K == Kb, (a.shape, b.shape, ta, tb)
    tm, tn = min(tm, M), min(tn, N)
    tk = K if (K <= 1024 and b_map is None) else min(tk, K)
    assert M % tm == 0 and N % tn == 0 and K % tk == 0, (M, N, K, tm, tn, tk)
    nk = K // tk
    a_spec = (pl.BlockSpec((tk, tm), lambda i, j, k: (k, i)) if ta
              else pl.BlockSpec((tm, tk), lambda i, j, k: (i, k)))
    if b_map is None:
        b_map = (lambda i, j, k: (j, k)) if tb else (lambda i, j, k: (k, j))
    b_spec = pl.BlockSpec((tn, tk) if tb else (tk, tn), b_map)
    o_spec = pl.BlockSpec((tm, tn), lambda i, j, k: (i, j))
    dims = (((0 if ta else 1,), (1 if tb else 0,)), ((), ()))
    in_specs = [a_spec, b_spec]
    operands = [a, b]
    aliases = {}
    if epi == "mul2r":
        in_specs.append(o_spec)
        operands.append(extra)
    if epi == "resnorm":
        assert tn == N
        vec = pl.BlockSpec((1, N), lambda i, j, k: (0, 0))
        in_specs += [o_spec, vec, vec]
        operands += list(extra)
    if into is not None:
        assert epi is None
        in_specs.append(pl.BlockSpec(memory_space=pl.ANY))
        operands.append(into)
        aliases = {2: 0}
        out_dtype = into.dtype
        out_shape = jax.ShapeDtypeStruct(into.shape, into.dtype)
        out_specs = pl.BlockSpec((tm, tn), o_map)
    elif epi == "relu2":
        out_shape = (jax.ShapeDtypeStruct((M, N), BF16), jax.ShapeDtypeStruct((M, N), BF16))
        out_specs = (o_spec, o_spec)
    elif epi == "mul2r":
        out_shape = jax.ShapeDtypeStruct((M, N), BF16)
        out_specs = o_spec
    elif epi == "resnorm":
        out_shape = (jax.ShapeDtypeStruct((M, N), F32), jax.ShapeDtypeStruct((M, N), F32),
                     jax.ShapeDtypeStruct((M, N), BF16))
        out_specs = (o_spec, o_spec, o_spec)
    else:
        out_shape = jax.ShapeDtypeStruct((M, N), out_dtype)
        out_specs = o_spec
    n_in = len(operands)
    n_out = {"relu2": 2, "resnorm": 3}.get(epi, 1)

    def body(*refs):
        a_ref, b_ref = refs[0], refs[1]
        outs = refs[n_in:n_in + n_out]
        k = pl.program_id(2)

        def finish(acc):
            if epi == "relu2":
                r = jnp.maximum(acc, 0.0)
                outs[0][...] = (r * r).astype(BF16)
                outs[1][...] = r.astype(BF16)
            elif epi == "mul2r":
                outs[0][...] = (acc * (2.0 * refs[2][...].astype(F32))).astype(BF16)
            elif epi == "resnorm":
                h_ref, gp_ref, gn_ref = refs[2], refs[3], refs[4]
                hn = h_ref[...] + acc * _rms_rstd(acc) * gp_ref[...]
                outs[0][...] = acc
                outs[1][...] = hn
                outs[2][...] = (hn * _rms_rstd(hn) * gn_ref[...]).astype(BF16)
            else:
                outs[0][...] = acc.astype(out_dtype)

        part = lax.dot_general(a_ref[...], b_ref[...], dims, preferred_element_type=F32)
        if nk == 1:
            finish(part)
            return
        acc_ref = refs[-1]

        @pl.when(k == 0)
        def _():
            acc_ref[...] = part

        @pl.when((k > 0) & (k < nk - 1))
        def _():
            acc_ref[...] += part

        @pl.when(k == nk - 1)
        def _():
            finish(acc_ref[...] + part)

    return pl.pallas_call(
        body, name=name, grid=(M // tm, N // tn, nk), in_specs=in_specs, out_specs=out_specs,
        out_shape=out_shape, scratch_shapes=[pltpu.VMEM((tm, tn), F32)] if nk > 1 else [],
        input_output_aliases=aliases,
        compiler_params=_cparams("parallel", "parallel", "arbitrary"))(*operands)


def _rms_rstd(x):
    return lax.rsqrt(jnp.mean(x * x, axis=-1, keepdims=True) + EPS)


def _rms_bwd_tile(x, g, dy):
    r = _rms_rstd(x)
    xh = x * r
    u = dy * g
    dx = r * (u - xh * jnp.mean(u * xh, axis=-1, keepdims=True))
    dg = jnp.sum(dy * xh, axis=0, keepdims=True)
    return dx, dg


def _row_tile(T):
    return min(256, T)


def _prenorm_fwd(x, g, name="prenorm_fwd"):
    T, D = x.shape
    tm = _row_tile(T)

    def body(x_ref, g_ref, a_ref):
        xv = x_ref[...]
        a_ref[...] = (xv * _rms_rstd(xv) * g_ref[...]).astype(BF16)

    row = pl.BlockSpec((tm, D), lambda i: (i, 0))
    vec = pl.BlockSpec((1, D), lambda i: (0, 0))
    return pl.pallas_call(body, name=name, grid=(T // tm,), in_specs=[row, vec], out_specs=row,
                          out_shape=jax.ShapeDtypeStruct((T, D), BF16),
                          compiler_params=_cparams("parallel"))(x, g)


def _resnorm_loss(h, z, g_post, target, name="resnorm_loss"):
    T, D = h.shape
    tm = _row_tile(T)

    def body(h_ref, z_ref, gp_ref, t_ref, dy_ref, sq_ref):
        zv = z_ref[...]
        err = h_ref[...] + zv * _rms_rstd(zv) * gp_ref[...] - t_ref[...]
        dy_ref[...] = err * (1.0 / D)

        @pl.when(pl.program_id(0) == 0)
        def _():
            sq_ref[...] = jnp.zeros_like(sq_ref)

        sq_ref[...] += jnp.sum(err * err, axis=0, keepdims=True)

    row = pl.BlockSpec((tm, D), lambda i: (i, 0))
    vec = pl.BlockSpec((1, D), lambda i: (0, 0))
    return pl.pallas_call(body, name=name, grid=(T // tm,), in_specs=[row, row, vec, row],
                          out_specs=(row, vec),
                          out_shape=(jax.ShapeDtypeStruct((T, D), F32), jax.ShapeDtypeStruct((1, D), F32)),
                          compiler_params=_cparams("arbitrary"))(h, z, g_post, target)


def _resnorm_bwd(z, g_post, dh, h_new=None, da=None, g_pre=None, name="resnorm_bwd"):
    T, D = z.shape
    tm = _row_tile(T)
    has_next = h_new is not None
    row = pl.BlockSpec((tm, D), lambda i: (i, 0))
    vec = pl.BlockSpec((1, D), lambda i: (0, 0))

    if has_next:
        def body(z_ref, gp_ref, dh_ref, hn_ref, da_ref, gn_ref, t_ref, dz_ref, dgp_ref, dgn_ref):
            first = pl.program_id(0) == 0

            @pl.when(first)
            def _():
                dgp_ref[...] = jnp.zeros_like(dgp_ref)
                dgn_ref[...] = jnp.zeros_like(dgn_ref)

            dpre, dgn = _rms_bwd_tile(hn_ref[...], gn_ref[...], da_ref[...])
            t = dh_ref[...] + dpre
            t_ref[...] = t
            dz, dgp = _rms_bwd_tile(z_ref[...], gp_ref[...], t)
            dz_ref[...] = dz.astype(BF16)
            dgp_ref[...] += dgp
            dgn_ref[...] += dgn

        return pl.pallas_call(
            body, name=name, grid=(T // tm,), in_specs=[row, vec, row, row, row, vec],
            out_specs=(row, row, vec, vec),
            out_shape=(jax.ShapeDtypeStruct((T, D), F32), jax.ShapeDtypeStruct((T, D), BF16),
                       jax.ShapeDtypeStruct((1, D), F32), jax.ShapeDtypeStruct((1, D), F32)),
            compiler_params=_cparams("arbitrary"))(z, g_post, dh, h_new, da, g_pre)

    def body_last(z_ref, gp_ref, dh_ref, dz_ref, dgp_ref):
        @pl.when(pl.program_id(0) == 0)
        def _():
            dgp_ref[...] = jnp.zeros_like(dgp_ref)

        dz, dgp = _rms_bwd_tile(z_ref[...], gp_ref[...], dh_ref[...])
        dz_ref[...] = dz.astype(BF16)
        dgp_ref[...] += dgp

    return pl.pallas_call(
        body_last, name=name, grid=(T // tm,), in_specs=[row, vec, row], out_specs=(row, vec),
        out_shape=(jax.ShapeDtypeStruct((T, D), BF16), jax.ShapeDtypeStruct((1, D), F32)),
        compiler_params=_cparams("arbitrary"))(z, g_post, dh)


def _prenorm_bwd(x, g, dh, da, name="prenorm_bwd"):
    T, D = x.shape
    tm = _row_tile(T)

    def body(x_ref, g_ref, dh_ref, da_ref, dx_ref, dg_ref):
        @pl.when(pl.program_id(0) == 0)
        def _():
            dg_ref[...] = jnp.zeros_like(dg_ref)

        dpre, dg = _rms_bwd_tile(x_ref[...], g_ref[...], da_ref[...])
        dx_ref[...] = dh_ref[...] + dpre
        dg_ref[...] += dg

    row = pl.BlockSpec((tm, D), lambda i: (i, 0))
    vec = pl.BlockSpec((1, D), lambda i: (0, 0))
    return pl.pallas_call(
        body, name=name, grid=(T // tm,), in_specs=[row, vec, row, row], out_specs=(row, vec),
        out_shape=(jax.ShapeDtypeStruct((T, D), F32), jax.ShapeDtypeStruct((1, D), F32)),
        compiler_params=_cparams("arbitrary"))(x, g, dh, da)


class _Packed:
    def __init__(self, big, misc_rows):
        self.big = tuple(big)
        self.off = {}
        r = 0
        for name, rows in big:
            self.off[name] = r
            r += rows
        self.misc, self.misc_rows = r, misc_rows
        self.rows = -(-(r + misc_rows) // PACK_ALIGN) * PACK_ALIGN

    def block(self, name, layer, unit):
        r = self.off[name]
        assert r % unit == 0 and self.rows % unit == 0
        return r // unit, self.rows // unit


def _col_sharded(pk, name, layer, unit):
    base, stride = pk.block(name, layer, unit)
    return (lambda i, j, k: (j * stride + base, 0)), (lambda i, j, k: (k * stride + base, 0))


def _row_sharded(pk, name, layer, unit):
    base, stride = pk.block(name, layer, unit)
    return ((lambda i, j, k: (k * stride + base, 0)), (lambda i, j, k: (j * stride + base, 0)),
            (lambda i, j, k: (i * stride + base, 0)))


def _mlp_fwd(a, wbuf, pk, layer, res):
    D = a.shape[1]
    by_n, _ = _col_sharded(pk, "mlp_w1", layer, D)
    by_k, _, _ = _row_sharded(pk, "mlp_w2", layer, D)
    act, r = _mm(a, wbuf, n=4 * D, b_map=by_n, tk=D, tn=D, epi="relu2", name="mlp_up")
    if res is None:
        return _mm(act, wbuf, n=D, b_map=by_k, tk=D, tn=D, name="mlp_down"), (a, act, r), None, None
    u, h_new, a_next = _mm(act, wbuf, n=D, b_map=by_k, tm=1024, tk=D, tn=D, epi="resnorm", extra=res,
                           name="mlp_down_res")
    return u, (a, act, r), h_new, a_next


def _mlp_bwd(du, saved, wbuf, gbuf, pk, layer):
    a, act, r = saved
    D = a.shape[1]
    w1_by_n, w1_by_k = _col_sharded(pk, "mlp_w1", layer, D)
    _, w2_by_n, w2_by_m = _row_sharded(pk, "mlp_w2", layer, D)
    dz1 = _mm(du, wbuf, tb=True, n=4 * D, b_map=w2_by_n, tn=D, tk=D, epi="mul2r", extra=r, name="mlp_down_dx")
    gbuf = _mm(act, du, ta=True, into=gbuf, o_map=w2_by_m, tm=D, tn=D, name="mlp_down_dw")
    gbuf = _mm(a, dz1, ta=True, into=gbuf, o_map=w1_by_n, tm=D, tn=D, name="mlp_up_dw")
    da = _mm(dz1, wbuf, tb=True, n=D, b_map=w1_by_k, tn=D, tk=D, name="mlp_up_dx")
    return da, gbuf


def _rope_swap(t):
    n = t.shape[-1]
    lane = lax.broadcasted_iota(jnp.int32, t.shape, t.ndim - 1)
    half = MLA_ROPE // 2
    first = (lane & (MLA_ROPE - 1)) < half
    return jnp.where(first, pltpu.roll(t, n - half, t.ndim - 1), pltpu.roll(t, half, t.ndim - 1))


def _mla_mid_fwd(proj, q_norm, kv_norm, w_uq, w_ukv, cc, ss):
    T, PW = proj.shape
    QL, KVL = q_norm.shape[-1], kv_norm.shape[-1]
    H = MLA_HEADS
    assert PW == QL + KVL + 128
    tm = _row_tile(T)

    def body(p_ref, qn_ref, kn_ref, wq_ref, wkv_ref, cc_ref, ss_ref,
             cq_ref, ckv_ref, q_ref, k_ref, v_ref):
        cq = p_ref[:, 0:QL]
        ckv = p_ref[:, QL:QL + KVL]
        kr = p_ref[:, QL + KVL:QL + KVL + 128]
        c, s = cc_ref[...], ss_ref[...]
        cqn = (cq * _rms_rstd(cq) * qn_ref[...]).astype(BF16)
        ckvn = (ckv * _rms_rstd(ckv) * kn_ref[...]).astype(BF16)
        cq_ref[...] = cqn
        ckv_ref[...] = ckvn
        q = jnp.dot(cqn, wq_ref[...], preferred_element_type=F32)
        kv = jnp.dot(ckvn, wkv_ref[...], preferred_element_type=F32)
        krf = (kr * c + _rope_swap(kr) * s).astype(BF16)
        for h in range(H):
            o = h * MLA_QK_PAD
            q_ref[:, o:o + MLA_NOPE] = (q[:, o:o + MLA_NOPE] * MLA_SCALE).astype(BF16)
            qr = q[:, o + MLA_NOPE:o + MLA_QK_PAD]
            q_ref[:, o + MLA_NOPE:o + MLA_QK_PAD] = ((qr * c + _rope_swap(qr) * s) * MLA_SCALE).astype(BF16)
            k_ref[:, o:o + MLA_NOPE] = kv[:, o:o + MLA_NOPE].astype(BF16)
            k_ref[:, o + MLA_NOPE:o + MLA_QK_PAD] = krf
            v_ref[:, h * MLA_V:(h + 1) * MLA_V] = kv[:, o + MLA_NOPE:o + MLA_QK_PAD].astype(BF16)

    def row(w):
        return pl.BlockSpec((tm, w), lambda i: (i, 0))

    def full(shape):
        return pl.BlockSpec(shape, lambda i: (0, 0))

    return pl.pallas_call(
        body, name="mla_mid_fwd", grid=(T // tm,),
        in_specs=[row(PW), full((1, QL)), full((1, KVL)), full(w_uq.shape), full(w_ukv.shape),
                  row(128), row(128)],
        out_specs=(row(QL), row(KVL), row(H * MLA_QK_PAD), row(H * MLA_QK_PAD), row(H * MLA_V)),
        out_shape=(jax.ShapeDtypeStruct((T, QL), BF16), jax.ShapeDtypeStruct((T, KVL), BF16),
                   jax.ShapeDtypeStruct((T, H * MLA_QK_PAD), BF16),
                   jax.ShapeDtypeStruct((T, H * MLA_QK_PAD), BF16),
                   jax.ShapeDtypeStruct((T, H * MLA_V), BF16)),
        compiler_params=_cparams("parallel"))(proj, q_norm, kv_norm, w_uq, w_ukv, cc, ss)


def _mla_mid_bwd(proj, q_norm, kv_norm, w_uq, w_ukv, cc, ss, dq, dk, dv):
    T, PW = proj.shape
    QL, KVL = q_norm.shape[-1], kv_norm.shape[-1]
    H = MLA_HEADS
    tm = _row_tile(T)
    nt = (((1,), (1,)), ((), ()))

    def body(p_ref, qn_ref, kn_ref, wq_ref, wkv_ref, cc_ref, ss_ref, dq_ref, dk_ref, dv_ref,
             dqp_ref, dkv_ref, dp_ref, dqn_ref, dkn_ref):
        @pl.when(pl.program_id(0) == 0)
        def _():
            dqn_ref[...] = jnp.zeros_like(dqn_ref)
            dkn_ref[...] = jnp.zeros_like(dkn_ref)

        c, s = cc_ref[...], ss_ref[...]
        dkr = jnp.zeros((tm, 128), F32)
        for h in range(H):
            o = h * MLA_QK_PAD
            dqp_ref[:, o:o + MLA_NOPE] = (dq_ref[:, o:o + MLA_NOPE] * MLA_SCALE).astype(BF16)
            dqr = dq_ref[:, o + MLA_NOPE:o + MLA_QK_PAD] * MLA_SCALE
            dqp_ref[:, o + MLA_NOPE:o + MLA_QK_PAD] = (dqr * c + _rope_swap(dqr * s)).astype(BF16)
            dkv_ref[:, o:o + MLA_NOPE] = dk_ref[:, o:o + MLA_NOPE].astype(BF16)
            dkv_ref[:, o + MLA_NOPE:o + MLA_QK_PAD] = dv_ref[:, h * MLA_V:(h + 1) * MLA_V].astype(BF16)
            dkr = dkr + dk_ref[:, o + MLA_NOPE:o + MLA_QK_PAD]
        dcqn = lax.dot_general(dqp_ref[...], wq_ref[...], nt, preferred_element_type=F32)
        dckvn = lax.dot_general(dkv_ref[...], wkv_ref[...], nt, preferred_element_type=F32)
        dcq, dqn = _rms_bwd_tile(p_ref[:, 0:QL], qn_ref[...], dcqn)
        dckv, dkn = _rms_bwd_tile(p_ref[:, QL:QL + KVL], kn_ref[...], dckvn)
        dp_ref[:, 0:QL] = dcq.astype(BF16)
        dp_ref[:, QL:QL + KVL] = dckv.astype(BF16)
        dp_ref[:, QL + KVL:QL + KVL + 128] = (dkr * c + _rope_swap(dkr * s)).astype(BF16)
        dqn_ref[...] += dqn
        dkn_ref[...] += dkn

    def row(w):
        return pl.BlockSpec((tm, w), lambda i: (i, 0))

    def full(shape):
        return pl.BlockSpec(shape, lambda i: (0, 0))

    return pl.pallas_call(
        body, name="mla_mid_bwd", grid=(T // tm,),
        in_specs=[row(PW), full((1, QL)), full((1, KVL)), full(w_uq.shape), full(w_ukv.shape),
                  row(128), row(128), row(H * MLA_QK_PAD), row(H * MLA_QK_PAD), row(H * MLA_V)],
        out_specs=(row(H * MLA_QK_PAD), row(H * MLA_QK_PAD), row(PW), full((1, QL)), full((1, KVL))),
        out_shape=(jax.ShapeDtypeStruct((T, H * MLA_QK_PAD), BF16),
                   jax.ShapeDtypeStruct((T, H * MLA_QK_PAD), BF16),
                   jax.ShapeDtypeStruct((T, PW), BF16),
                   jax.ShapeDtypeStruct((1, QL), F32), jax.ShapeDtypeStruct((1, KVL), F32)),
        compiler_params=_cparams("arbitrary"))(proj, q_norm, kv_norm, w_uq, w_ukv, cc, ss, dq, dk, dv)


def _attn_tile(T):
    return min(1024, T)


def _attn_pairs(n, by_key):
    if by_key:
        pairs = [(qi, ki) for ki in range(n) for qi in range(ki, n)]
    else:
        pairs = [(qi, ki) for qi in range(n) for ki in range(qi + 1)]
    return (jnp.asarray([p[0] for p in pairs], jnp.int32), jnp.asarray([p[1] for p in pairs], jnp.int32))


def _scores(q, k, diagonal):
    s = lax.dot_general(q, k, (((1,), (1,)), ((), ())), preferred_element_type=F32)
    if diagonal:
        rows = lax.broadcasted_iota(jnp.int32, s.shape, 0)
        cols = lax.broadcasted_iota(jnp.int32, s.shape, 1)
        s = jnp.where(rows >= cols, s, -jnp.inf)
    return s


def _attn_fwd(q, k, v):
    T = q.shape[0]
    H, DQ, DV = MLA_HEADS, MLA_QK_PAD, MLA_V
    tq = _attn_tile(T)
    nq = T // tq
    G = MLA_HEADS_PER_STEP
    qi_tab, ki_tab = _attn_pairs(nq, by_key=False)

    def body(qi_ref, ki_ref, q_ref, k_ref, v_ref, o_ref, lse_ref, *scratch):
        m_refs, l_refs, acc_refs = scratch[0:G], scratch[G:2 * G], scratch[2 * G:3 * G]
        p = pl.program_id(1)
        qi, ki = qi_ref[p], ki_ref[p]

        @pl.when(ki == 0)
        def _():
            for g in range(G):
                m_refs[g][...] = jnp.full_like(m_refs[g], -jnp.inf)
                l_refs[g][...] = jnp.zeros_like(l_refs[g])
                acc_refs[g][...] = jnp.zeros_like(acc_refs[g])

        def update(ks, qr, masked):
            for g in range(G):
                qs, vs = slice(g * DQ, (g + 1) * DQ), slice(g * DV, (g + 1) * DV)
                st = _scores(k_ref[ks, qs], q_ref[qr, qs], False)
                if masked:
                    key = ks.start + lax.broadcasted_iota(jnp.int32, st.shape, 0)
                    qry = qr.start + lax.broadcasted_iota(jnp.int32, st.shape, 1)
                    st = jnp.where(qry >= key, st, -jnp.inf)
                m_prev = m_refs[g][:, qr]
                m_new = jnp.maximum(m_prev, jnp.max(st, axis=0, keepdims=True))
                alpha = jnp.exp(m_prev - m_new)
                pt = jnp.exp(st - m_new)
                l_refs[g][:, qr] = alpha * l_refs[g][:, qr] + jnp.sum(pt, axis=0, keepdims=True)
                acc_refs[g][:, qr] = alpha * acc_refs[g][:, qr] + lax.dot_general(
                    v_ref[ks, vs], pt.astype(BF16), (((0,), (0,)), ((), ())), preferred_element_type=F32)
                m_refs[g][:, qr] = m_new

        whole, half = slice(0, tq), tq // 2

        @pl.when(ki < qi)
        def _():
            update(whole, whole, False)

        @pl.when(ki == qi)
        def _():
            update(slice(0, half), whole, True)
            update(slice(half, tq), slice(half, tq), True)
            for g in range(G):
                vs = slice(g * DV, (g + 1) * DV)
                o_ref[:, vs] = jnp.transpose(acc_refs[g][...] / l_refs[g][...]).astype(BF16)
                lse_ref[g] = m_refs[g][...] + jnp.log(l_refs[g][...])

    return pl.pallas_call(
        body, name="attn_fwd",
        grid_spec=pltpu.PrefetchScalarGridSpec(
            num_scalar_prefetch=2, grid=(H // G, int(qi_tab.shape[0])),
            in_specs=[pl.BlockSpec((tq, G * DQ), lambda h, p, qt, kt: (qt[p], h)),
                      pl.BlockSpec((tq, G * DQ), lambda h, p, qt, kt: (kt[p], h)),
                      pl.BlockSpec((tq, G * DV), lambda h, p, qt, kt: (kt[p], h))],
            out_specs=(pl.BlockSpec((tq, G * DV), lambda h, p, qt, kt: (qt[p], h)),
                       pl.BlockSpec((G, 1, tq), lambda h, p, qt, kt: (h, 0, qt[p]))),
            scratch_shapes=([pltpu.VMEM((1, tq), F32)] * (2 * G) + [pltpu.VMEM((DV, tq), F32)] * G)),
        out_shape=(jax.ShapeDtypeStruct((T, H * DV), BF16), jax.ShapeDtypeStruct((H, 1, T), F32)),
        compiler_params=_cparams("parallel", "arbitrary"))(qi_tab, ki_tab, q, k, v)


def _attn_bwd(q, k, v, o, do, lse):
    T = q.shape[0]
    H, DQ, DV = MLA_HEADS, MLA_QK_PAD, MLA_V
    tq = _attn_tile(T)
    nq = T // tq
    tn = (((0,), (0,)), ((), ()))
    nt = (((1,), (1,)), ((), ()))
    G = MLA_HEADS_PER_STEP
    qi_tab, ki_tab = _attn_pairs(nq, by_key=True)

    def body(qi_ref, ki_ref, q_ref, k_ref, v_ref, o_ref, do_ref, lse_ref, dq_ref, dk_ref, dv_ref,
             dk_acc, dv_acc):
        p = pl.program_id(1)
        qi, ki = qi_ref[p], ki_ref[p]

        @pl.when(p == 0)
        def _():
            dq_ref[...] = jnp.zeros_like(dq_ref)

        @pl.when(qi == ki)
        def _():
            dk_acc[...] = jnp.zeros_like(dk_acc)
            dv_acc[...] = jnp.zeros_like(dv_acc)

        def step(ks, qr, masked):
            rows = pl.ds(pl.multiple_of(qi * tq + qr.start, qr.stop - qr.start), qr.stop - qr.start)
            for g in range(G):
                qs, vs = slice(g * DQ, (g + 1) * DQ), slice(g * DV, (g + 1) * DV)
                dof = do_ref[qr, vs]
                delta = jnp.sum(jnp.transpose(dof.astype(F32) * o_ref[qr, vs].astype(F32)), axis=0,
                                keepdims=True)
                st = _scores(k_ref[ks, qs], q_ref[qr, qs], False)
                if masked:
                    key = ks.start + lax.broadcasted_iota(jnp.int32, st.shape, 0)
                    qry = qr.start + lax.broadcasted_iota(jnp.int32, st.shape, 1)
                    st = jnp.where(qry >= key, st, -jnp.inf)
                pt = jnp.exp(st - lse_ref[g][:, qr])
                dpt = lax.dot_general(v_ref[ks, vs], dof, nt, preferred_element_type=F32)
                dst = (pt * (dpt - delta)).astype(BF16)
                dv_acc[ks, vs] += jnp.dot(pt.astype(BF16), dof, preferred_element_type=F32)
                dk_acc[ks, qs] += jnp.dot(dst, q_ref[qr, qs], preferred_element_type=F32)
                dq_ref[rows, qs] += lax.dot_general(dst, k_ref[ks, qs], tn, preferred_element_type=F32)

        whole, half = slice(0, tq), tq // 2

        @pl.when(qi == ki)
        def _():
            step(slice(0, half), whole, True)
            step(slice(half, tq), slice(half, tq), True)

        @pl.when(qi > ki)
        def _():
            step(whole, whole, False)

        @pl.when(qi == nq - 1)
        def _():
            dk_ref[...] = dk_acc[...]
            dv_ref[...] = dv_acc[...]

    qspec = pl.BlockSpec((tq, G * DQ), lambda h, p, qt, kt: (qt[p], h))
    ospec = pl.BlockSpec((tq, G * DV), lambda h, p, qt, kt: (qt[p], h))
    kspec = pl.BlockSpec((tq, G * DQ), lambda h, p, qt, kt: (kt[p], h))
    vspec = pl.BlockSpec((tq, G * DV), lambda h, p, qt, kt: (kt[p], h))
    return pl.pallas_call(
        body, name="attn_bwd",
        grid_spec=pltpu.PrefetchScalarGridSpec(
            num_scalar_prefetch=2, grid=(H // G, int(qi_tab.shape[0])),
            in_specs=[qspec, kspec, vspec, ospec, ospec,
                      pl.BlockSpec((G, 1, tq), lambda h, p, qt, kt: (h, 0, qt[p]))],
            out_specs=(pl.BlockSpec((T, G * DQ), lambda h, p, qt, kt: (0, h)), kspec, vspec),
            scratch_shapes=[pltpu.VMEM((tq, G * DQ), F32), pltpu.VMEM((tq, G * DV), F32)]),
        out_shape=(jax.ShapeDtypeStruct((T, H * DQ), F32), jax.ShapeDtypeStruct((T, H * DQ), F32),
                   jax.ShapeDtypeStruct((T, H * DV), F32)),
        compiler_params=_cparams("parallel", "arbitrary"))(qi_tab, ki_tab, q, k, v, o, do, lse)


def _mla_fwd(a, w, cc, ss, wbuf, pk, slot, res):
    D = a.shape[1]
    by_k, _, _ = _row_sharded(pk, "mla_w_o", slot, D // N_CHIPS)
    proj = _mm(a, w["w_in"], name="mla_in")
    cqn, ckvn, q, k, v = _mla_mid_fwd(proj, w["q_norm"], w["kv_norm"], w["w_uq"], w["w_ukv"], cc, ss)
    o, lse = _attn_fwd(q, k, v)
    m, h_new, a_next = _mm(o, wbuf, n=D, b_map=by_k, tm=1024, tk=D // N_CHIPS, tn=D, epi="resnorm", extra=res,
                           name="mla_out_res")
    return m, (a, proj, cqn, ckvn, q, k, v, o, lse), h_new, a_next


def _mla_bwd(dm, saved, w, cc, ss, wbuf, gbuf, pk, slot):
    a, proj, cqn, ckvn, q, k, v, o, lse = saved
    D = a.shape[1]
    _, by_n, by_m = _row_sharded(pk, "mla_w_o", slot, D // N_CHIPS)
    do = _mm(dm, wbuf, tb=True, n=o.shape[1], b_map=by_n, tm=2048, tn=D // N_CHIPS, tk=D, out_dtype=BF16,
             name="mla_out_dx")
    gbuf = _mm(o, dm, ta=True, into=gbuf, o_map=by_m, tm=D // N_CHIPS, tn=D, tk=2048, name="mla_out_dw")
    dq, dk, dv = _attn_bwd(q, k, v, o, do, lse)
    dqp, dkv, dproj, dqn, dkn = _mla_mid_bwd(proj, w["q_norm"], w["kv_norm"], w["w_uq"], w["w_ukv"],
                                             cc, ss, dq, dk, dv)
    dw_uq = _mm(cqn, dqp, ta=True, out_dtype=BF16, name="mla_uq_dw")
    dw_ukv = _mm(ckvn, dkv, ta=True, out_dtype=BF16, name="mla_ukv_dw")
    dw_in = _mm(a, dproj, ta=True, out_dtype=BF16, name="mla_in_dw")
    da = _mm(dproj, w["w_in"], tb=True, name="mla_in_dx")
    return da, gbuf, dict(w_in=dw_in, w_uq=dw_uq, w_ukv=dw_ukv, q_norm=dqn, kv_norm=dkn)


def _split_dot(mat, x, parts):
    acc = None
    rem = x
    for _ in range(parts):
        piece = rem.astype(BF16)
        term = jnp.dot(mat, piece, preferred_element_type=F32)
        acc = term if acc is None else acc + term
        rem = rem - piece.astype(F32)
    return acc


def _chunk_sums(cum, rel, rest, logf):
    return tuple(_split_dot(m.astype(BF16), logf, 3) for m in (cum, rel, rest))


def _chunk_mats(tb):
    C = HGRN_CHUNK
    assert C & (C - 1) == 0
    r = lax.broadcasted_iota(jnp.int32, (tb, tb), 0)
    s = lax.broadcasted_iota(jnp.int32, (tb, tb), 1)
    start = r & ~(C - 1)
    same = start == (s & ~(C - 1))
    ref = start + C // 2
    last = start + C - 1
    one, zero = jnp.float32(1.0), jnp.float32(0.0)
    cum = jnp.where(same & (s <= r), one, zero)
    rel = cum - jnp.where(same & (s <= ref), one, zero)
    rest = jnp.where(same & (s > r) & (s <= last), one, zero)
    rev = jnp.where(same & (s >= r), one, zero)
    ones = jnp.where(same, one, zero)
    causal = same & (s <= r)
    return cum, rel, rest, rev, ones, causal


def _hgrn_gates(p_ref, lb, HK):
    qx = p_ref[:, 0:HK]
    fx = p_ref[:, HK:2 * HK]
    sf = _sigmoid(fx)
    f = lb + (1.0 - lb) * sf
    sq = _sigmoid(qx)
    return qx, sq, qx * sq, sf, f, 1.0 - f, jnp.log(f)


def _hgrn_fwd(proj, lb, o_norm):
    T = proj.shape[0]
    H, C = HGRN_HEADS, HGRN_CHUNK
    HK = proj.shape[1] // 4
    DK = HK // H
    tb = min(HGRN_BLOCK, T)
    ncb = tb // C
    nt = (((1,), (1,)), ((), ()))
    tn = (((0,), (0,)), ((), ()))

    def body(p_ref, lb_ref, on_ref, y_ref, o_ref, st_ref, state, oacc):
        @pl.when(pl.program_id(0) == 0)
        def _():
            state[...] = jnp.zeros_like(state)

        cum, rel, rest, _, _, causal = _chunk_mats(tb)
        _, _, q, _, f, k, logf = _hgrn_gates(p_ref, lb_ref[...], HK)
        b, brel, brest = _chunk_sums(cum, rel, rest, logf)
        eb = jnp.exp(b)
        q_rel = (q * jnp.exp(brel)).astype(BF16)
        k_rel = (k * jnp.exp(-brel)).astype(BF16)
        q_dec = (q * eb).astype(BF16)
        k_dec = (k * jnp.exp(brest)).astype(BF16)
        v = p_ref[:, 2 * HK:3 * HK].astype(BF16)
        for h in range(H):
            hs = slice(h * DK, (h + 1) * DK)
            a = lax.dot_general(q_rel[:, hs], k_rel[:, hs], nt, preferred_element_type=F32)
            a = jnp.where(causal, a, 0.0).astype(BF16)
            oacc[:, hs] = jnp.dot(a, v[:, hs], preferred_element_type=F32)
            for j in range(ncb):
                rs = slice(j * C, (j + 1) * C)
                st = state[h]
                st_ref[j, h] = st
                oacc[rs, hs] += lax.dot_general(q_dec[rs, hs], st.astype(BF16), nt,
                                                preferred_element_type=F32)
                dec = jnp.exp(jnp.sum(logf[rs, hs], axis=0, keepdims=True))
                state[h] = dec * st + lax.dot_general(v[rs, hs], k_dec[rs, hs], tn,
                                                      preferred_element_type=F32)
        o = oacc[...]
        o_ref[...] = o
        gx = p_ref[:, 3 * HK:4 * HK]
        gate = gx * _sigmoid(gx)
        for h in range(H):
            hs = slice(h * DK, (h + 1) * DK)
            oh = o[:, hs]
            y_ref[:, hs] = (oh * _rms_rstd(oh) * on_ref[...] * gate[:, hs]).astype(BF16)

    return pl.pallas_call(
        body, name="hgrn_fwd", grid=(T // tb,),
        in_specs=[pl.BlockSpec((tb, 4 * HK), lambda i: (i, 0)),
                  pl.BlockSpec((1, HK), lambda i: (0, 0)),
                  pl.BlockSpec((1, DK), lambda i: (0, 0))],
        out_specs=(pl.BlockSpec((tb, HK), lambda i: (i, 0)),
                   pl.BlockSpec((tb, HK), lambda i: (i, 0)),
                   pl.BlockSpec((ncb, H, DK, DK), lambda i: (i, 0, 0, 0))),
        out_shape=(jax.ShapeDtypeStruct((T, HK), BF16), jax.ShapeDtypeStruct((T, HK), F32),
                   jax.ShapeDtypeStruct((T // C, H, DK, DK), F32)),
        scratch_shapes=[pltpu.VMEM((H, DK, DK), F32), pltpu.VMEM((tb, HK), F32)],
        compiler_params=_cparams("arbitrary"))(proj, lb, o_norm)


def _hgrn_bwd(proj, lb, o_norm, o, states, dy):
    T = proj.shape[0]
    H, C = HGRN_HEADS, HGRN_CHUNK
    HK = proj.shape[1] // 4
    DK = HK // H
    tb = min(HGRN_BLOCK, T)
    ncb = tb // C
    nb = T // tb
    nt = (((1,), (1,)), ((), ()))
    tn = (((0,), (0,)), ((), ()))

    def body(p_ref, lb_ref, on_ref, o_ref, st_ref, dy_ref, dp_ref, dlb_ref, don_ref,
             dstate, dqr_s, dkr_s, dqd_s, dkd_s, dv_s, do_s, e_s):
        @pl.when(pl.program_id(0) == 0)
        def _():
            dstate[...] = jnp.zeros_like(dstate)
            dlb_ref[...] = jnp.zeros_like(dlb_ref)
            don_ref[...] = jnp.zeros_like(don_ref)

        cum, rel, rest, rev, ones, causal = _chunk_mats(tb)
        lb = lb_ref[...]
        qx, sq, q, sf, f, k, logf = _hgrn_gates(p_ref, lb, HK)
        b, brel, brest = _chunk_sums(cum, rel, rest, logf)
        eb = jnp.exp(b)
        erel = jnp.exp(brel)
        enrel = jnp.exp(-brel)
        erest = jnp.exp(brest)
        q_rel_f, k_rel_f, q_dec_f, k_dec_f = q * erel, k * enrel, q * eb, k * erest
        q_rel, k_rel = q_rel_f.astype(BF16), k_rel_f.astype(BF16)
        q_dec, k_dec = q_dec_f.astype(BF16), k_dec_f.astype(BF16)
        v = p_ref[:, 2 * HK:3 * HK].astype(BF16)

        gx = p_ref[:, 3 * HK:4 * HK]
        sg = _sigmoid(gx)
        gate = gx * sg
        dy = dy_ref[...]
        ov = o_ref[...]
        on = on_ref[...]
        don = jnp.zeros((1, DK), F32)
        for h in range(H):
            hs = slice(h * DK, (h + 1) * DK)
            oh = ov[:, hs]
            r = _rms_rstd(oh)
            xh = oh * r
            d_on = dy[:, hs] * gate[:, hs]
            don = don + jnp.sum(d_on * xh, axis=0, keepdims=True)
            u = d_on * on
            do_s[:, hs] = r * (u - xh * jnp.mean(u * xh, axis=-1, keepdims=True))
            dp_ref[:, 3 * HK + h * DK:3 * HK + (h + 1) * DK] = (
                dy[:, hs] * xh * on * (sg[:, hs] * (1.0 + gx[:, hs] * (1.0 - sg[:, hs])))).astype(BF16)
        don_ref[...] += don

        for h in range(H):
            hs = slice(h * DK, (h + 1) * DK)
            doh = do_s[:, hs].astype(BF16)
            a = lax.dot_general(q_rel[:, hs], k_rel[:, hs], nt, preferred_element_type=F32)
            a = jnp.where(causal, a, 0.0).astype(BF16)
            da = lax.dot_general(doh, v[:, hs], nt, preferred_element_type=F32)
            da = jnp.where(causal, da, 0.0).astype(BF16)
            dv_s[:, hs] = lax.dot_general(a, doh, tn, preferred_element_type=F32)
            dqr_s[:, hs] = jnp.dot(da, k_rel[:, hs], preferred_element_type=F32)
            dkr_s[:, hs] = lax.dot_general(da, q_rel[:, hs], tn, preferred_element_type=F32)
            for j in reversed(range(ncb)):
                rs = slice(j * C, (j + 1) * C)
                dst = dstate[h]
                dstb = dst.astype(BF16)
                st = st_ref[j, h]
                dkd_s[rs, hs] = jnp.dot(v[rs, hs], dstb, preferred_element_type=F32)
                dv_s[rs, hs] += lax.dot_general(k_dec[rs, hs], dstb, nt, preferred_element_type=F32)
                dec = jnp.exp(jnp.sum(logf[rs, hs], axis=0, keepdims=True))
                e_s[rs, hs] = jnp.broadcast_to(jnp.sum(dst * st, axis=0, keepdims=True) * dec, (C, DK))
                dqd_s[rs, hs] = jnp.dot(doh[rs], st.astype(BF16), preferred_element_type=F32)
                dstate[h] = dec * dst + lax.dot_general(doh[rs], q_dec[rs, hs], tn,
                                                        preferred_element_type=F32)

        dqr, dkr, dqd, dkd = dqr_s[...], dkr_s[...], dqd_s[...], dkd_s[...]
        kdk = dkd * k_dec_f
        db = dqr * q_rel_f - dkr * k_rel_f + dqd * q_dec_f - kdk
        dlogf = _split_dot(rev.astype(BF16), db, 2) + _split_dot(ones.astype(BF16), kdk, 2) + e_s[...]
        dk = dkr * enrel + dkd * erest
        df = dlogf / f - dk
        dlb_ref[...] += jnp.sum(df * (1.0 - sf), axis=0, keepdims=True)
        dq = dqr * erel + dqd * eb
        dp_ref[:, 0:HK] = (dq * (sq * (1.0 + qx * (1.0 - sq)))).astype(BF16)
        dp_ref[:, HK:2 * HK] = (df * (1.0 - lb) * sf * (1.0 - sf)).astype(BF16)
        dp_ref[:, 2 * HK:3 * HK] = dv_s[...].astype(BF16)

    rev_row = lambda w: pl.BlockSpec((tb, w), lambda i: (nb - 1 - i, 0))
    vec = lambda w: pl.BlockSpec((1, w), lambda i: (0, 0))
    scr = pltpu.VMEM((tb, HK), F32)
    return pl.pallas_call(
        body, name="hgrn_bwd", grid=(nb,),
        in_specs=[rev_row(4 * HK), vec(HK), vec(DK), rev_row(HK),
                  pl.BlockSpec((ncb, H, DK, DK), lambda i: (nb - 1 - i, 0, 0, 0)), rev_row(HK)],
        out_specs=(rev_row(4 * HK), vec(HK), vec(DK)),
        out_shape=(jax.ShapeDtypeStruct((T, 4 * HK), BF16), jax.ShapeDtypeStruct((1, HK), F32),
                   jax.ShapeDtypeStruct((1, DK), F32)),
        scratch_shapes=[pltpu.VMEM((H, DK, DK), F32), scr, scr, scr, scr, scr, scr, scr],
        compiler_params=_cparams("arbitrary"))(proj, lb, o_norm, o, states, dy)


def _hgrn_layer_fwd(a, o_norm, lb, wbuf, pk, slot, res):
    D = a.shape[1]
    in_by_n, _ = _col_sharded(pk, "hgrn_w_in", slot, D)
    out_by_k, _, _ = _row_sharded(pk, "hgrn_w_o", slot, D // N_CHIPS)
    proj = _mm(a, wbuf, n=4 * D, b_map=in_by_n, tk=D, tn=D, name="hgrn_in")
    y, o, states = _hgrn_fwd(proj, lb, o_norm)
    m, h_new, a_next = _mm(y, wbuf, n=D, b_map=out_by_k, tm=1024, tk=D // N_CHIPS, tn=D, epi="resnorm",
                           extra=res, name="hgrn_out_res")
    return m, (a, proj, y, o, states), h_new, a_next


def _hgrn_layer_bwd(dm, saved, o_norm, lb, wbuf, gbuf, pk, slot):
    a, proj, y, o, states = saved
    D = a.shape[1]
    in_by_n, in_by_k = _col_sharded(pk, "hgrn_w_in", slot, D)
    _, out_by_n, out_by_m = _row_sharded(pk, "hgrn_w_o", slot, D // N_CHIPS)
    dy = _mm(dm, wbuf, tb=True, n=y.shape[1], b_map=out_by_n, tm=2048, tn=D // N_CHIPS, tk=D,
             name="hgrn_out_dx")
    gbuf = _mm(y, dm, ta=True, into=gbuf, o_map=out_by_m, tm=D // N_CHIPS, tn=D, tk=2048, name="hgrn_out_dw")
    dproj, dlb, don = _hgrn_bwd(proj, lb, o_norm, o, states, dy)
    gbuf = _mm(a, dproj, ta=True, into=gbuf, o_map=in_by_n, tm=D, tn=D, name="hgrn_in_dw")
    da = _mm(dproj, wbuf, tb=True, n=D, b_map=in_by_k, tn=D, tk=D, name="hgrn_in_dx")
    return da, gbuf, dict(o_norm=don, lb=dlb)


def _lower_bounds(lb_logits):
    p = jax.nn.softmax(lb_logits.astype(F32), axis=0)
    return jnp.cumsum(p, axis=0) - p[0]


def _rope_tables(positions):
    inv_freq = jnp.power(ROPE_BASE, -jnp.arange(0, MLA_ROPE, 2, dtype=F32) / MLA_ROPE)
    ang = positions.astype(F32)[:, None] * inv_freq
    cos, sin = jnp.cos(ang), jnp.sin(ang)
    zero = jnp.zeros((positions.shape[0], 128 - MLA_ROPE), F32)
    return (jnp.concatenate([cos, cos, zero], axis=-1), jnp.concatenate([-sin, sin, zero], axis=-1))


def _pad_mla_weights(w_in, w_uq):
    w_in_p = jnp.pad(w_in, ((0, 0), (0, 0), (0, 128 - MLA_ROPE)))
    n, ql, _ = w_uq.shape
    w_uq_p = jnp.pad(w_uq.reshape(n, ql, MLA_HEADS, MLA_NOPE + MLA_ROPE),
                     ((0, 0), (0, 0), (0, 0), (0, MLA_QK_PAD - MLA_NOPE - MLA_ROPE)))
    return w_in_p, w_uq_p.reshape(n, ql, MLA_HEADS * MLA_QK_PAD)


def _local_step(x, positions, target, small, fetch, gbufs, emit, emit_mlp):
    T, D = x.shape
    lbounds, lb_vjp = jax.vjp(_lower_bounds, small["hgrn_lb_logits"])
    cc, ss = _rope_tables(positions)
    fetched = {0: fetch(0, None)}
    gains = fetched[0]["gains"]
    tick = [jnp.zeros((), F32)]

    def g(layer, i):
        return gains[layer, i][None, :] + tick[0]

    def mla_weights(layer):
        f = fetched[layer]
        w_in_p, w_uq_p = _pad_mla_weights(f["w_in"][None], f["w_uq"][None])
        slot = layer // 2
        return dict(w_in=w_in_p[0], w_uq=w_uq_p[0], w_ukv=f["w_ukv"],
                    q_norm=small["mla_q_norm"][slot][None, :], kv_norm=small["mla_kv_norm"][slot][None, :])

    saved = []
    h = x
    a = _prenorm_fwd(x, g(0, 0))
    dy = sq = None
    for layer in range(DEPTH):
        slot = layer // 2
        if layer not in fetched:
            fetched[layer] = fetch(layer, a)
        wbuf, pk = fetched[layer]["wbuf"], fetched[layer]["pk"]
        res = (h, g(layer, 1), g(layer, 2))
        if layer % 2 == 0:
            m, mix_saved, h1, a2 = _mla_fwd(a, mla_weights(layer), cc, ss, wbuf, pk, slot, res)
        else:
            m, mix_saved, h1, a2 = _hgrn_layer_fwd(a, small["hgrn_o_norm"][slot][None, :],
                                                   lbounds[layer][None, :], wbuf, pk, slot, res)
        if layer + 1 < DEPTH:
            u, mlp_saved, h2, a = _mlp_fwd(a2, wbuf, pk, layer, (h1, g(layer, 3), g(layer + 1, 0)))
        else:
            u, mlp_saved, h2, _ = _mlp_fwd(a2, wbuf, pk, layer, None)
            dy, sq = _resnorm_loss(h1, u, g(layer, 3), target)
        saved.append((h, m, h1, u, mix_saved, mlp_saved))
        h = h2

    n_mla, n_hgrn = (DEPTH + 1) // 2, DEPTH // 2
    dgains = [[None] * 4 for _ in range(DEPTH)]
    gw = {k: [None] * n_mla for k in ("mla_w_in", "mla_w_uq", "mla_w_ukv", "mla_q_norm", "mla_kv_norm")}
    gw["hgrn_o_norm"] = [None] * n_hgrn
    dlb = [jnp.zeros((1, lbounds.shape[1]), F32) for _ in range(DEPTH)]
    dh = dy
    da_next = None
    for layer in reversed(range(DEPTH)):
        h0, m, h1, u, mix_saved, mlp_saved = saved[layer]
        slot = layer // 2
        wbuf, pk, gbuf = fetched[layer]["wbuf"], fetched[layer]["pk"], gbufs[layer]
        if da_next is None:
            du, dgains[layer][3] = _resnorm_bwd(u, g(layer, 3), dh, name="resnorm_bwd_last")
            t = dh
        else:
            h2 = saved[layer + 1][0]
            t, du, dgains[layer][3], dgains[layer + 1][0] = _resnorm_bwd(
                u, g(layer, 3), dh, h2, da_next, g(layer + 1, 0), name="resnorm_bwd_mlp")
        da2, gbuf = _mlp_bwd(du, mlp_saved, wbuf, gbuf, pk, layer)
        if layer == 0:
            gbuf = emit_mlp(layer, gbuf)
        t, dm, dgains[layer][1], dgains[layer][2] = _resnorm_bwd(
            m, g(layer, 1), t, h1, da2, g(layer, 2), name="resnorm_bwd_mix")
        if layer % 2 == 0:
            da_next, gbuf, mg = _mla_bwd(dm, mix_saved, mla_weights(layer), cc, ss, wbuf, gbuf, pk, slot)
            ql = mg["q_norm"].shape[-1]
            kvl = mg["kv_norm"].shape[-1]
            gw["mla_w_in"][slot] = mg["w_in"][:, :ql + kvl + MLA_ROPE]
            gw["mla_w_uq"][slot] = mg["w_uq"].reshape(ql, MLA_HEADS, MLA_QK_PAD)[
                :, :, :MLA_NOPE + MLA_ROPE].reshape(ql, MLA_HEADS * (MLA_NOPE + MLA_ROPE))
            gw["mla_w_ukv"][slot] = mg["w_ukv"]
            gw["mla_q_norm"][slot] = mg["q_norm"][0]
            gw["mla_kv_norm"][slot] = mg["kv_norm"][0]
        else:
            da_next, gbuf, hg = _hgrn_layer_bwd(dm, mix_saved, small["hgrn_o_norm"][slot][None, :],
                                                lbounds[layer][None, :], wbuf, gbuf, pk, slot)
            gw["hgrn_o_norm"][slot] = hg["o_norm"][0]
            dlb[layer] = hg["lb"]
        dh = t
        if layer > 0:
            mine = ({k: gw[k][slot] for k in ("mla_w_in", "mla_w_uq", "mla_w_ukv")} if layer % 2 == 0 else {})
            tick[0] = emit(layer, gbuf, mine)
        else:
            gbuf0 = gbuf
    grad_x, dgains[0][0] = _prenorm_bwd(x, g(0, 0), dh, da_next)

    last = {k: gw[k][0] for k in ("mla_w_in", "mla_w_uq", "mla_w_ukv")}
    last.update({k: jnp.stack(gw[k]) for k in ("mla_q_norm", "mla_kv_norm", "hgrn_o_norm")})
    last["norm_gains"] = jnp.stack([jnp.concatenate(row, axis=0) for row in dgains])
    (last["hgrn_lb_logits"],) = lb_vjp(jnp.concatenate(dlb, axis=0))
    emit(0, gbuf0, last)
    return sq, grad_x


def _size(shape):
    n = 1
    for d in shape:
        n *= d
    return n


def _piece_rows(shape):
    return -(-_size(shape) // PACK_W)


def _packed_misc_rows(shapes):
    return sum(_piece_rows(s) for s in shapes)


def _cast_into(src, buf, row, name):
    rows, W = src.shape
    tr = min(256, rows)
    assert rows % tr == 0 and row % tr == 0

    def body(s_ref, b_ref, o_ref):
        o_ref[...] = s_ref[...].astype(BF16)

    return pl.pallas_call(
        body, name=name, grid=(rows // tr,),
        in_specs=[pl.BlockSpec((tr, W), lambda i: (i, 0)), pl.BlockSpec(memory_space=pl.ANY)],
        out_specs=pl.BlockSpec((tr, W), lambda i: (row // tr + i, 0)),
        out_shape=jax.ShapeDtypeStruct(buf.shape, buf.dtype), input_output_aliases={1: 0},
        compiler_params=_cparams("parallel"))(src, buf)


def _pack_blocks(pieces, rows, dtype):
    blocks, used = [], 0
    for p in pieces:
        flat = p.astype(dtype).reshape(-1)
        r = _piece_rows(p.shape)
        if r * PACK_W != flat.shape[0]:
            flat = jnp.pad(flat, (0, r * PACK_W - flat.shape[0]))
        blocks.append(flat.reshape(r, PACK_W))
        used += r
    if rows > used:
        blocks.append(jnp.zeros((rows - used, PACK_W), dtype))
    return blocks


def _unpack(buf, shapes):
    out, off = [], 0
    for shp in shapes:
        r = _piece_rows(shp)
        piece = buf[off:off + r]
        if r * PACK_W != _size(shp):
            piece = piece.reshape(-1)[:_size(shp)]
        out.append(piece.reshape(shp))
        off += r
    return out


def _mesh_place():
    x, y, c = lax.axis_index("x"), lax.axis_index("y"), lax.axis_index("c")
    chips = [(1 - x, y), (x, 1 - y), (1 - x, 1 - y)]
    return x, y, c, chips


_HBM = pl.BlockSpec(memory_space=pltpu.HBM)


def _share_reduced(q, name="grads_share_reduced"):
    rh, W = q.shape

    def body(q_ref, out_ref, send_sem, recv_sem):
        x, y, c, _ = _mesh_place()
        cp = pltpu.make_async_remote_copy(src_ref=q_ref, dst_ref=out_ref.at[c], send_sem=send_sem,
                                          recv_sem=recv_sem, device_id=(x, y, 1 - c), device_id_type=MESH)
        cp.start()
        cp.wait()

    out = pl.pallas_call(
        body, name=name, in_specs=[_HBM], out_specs=_HBM,
        out_shape=jax.ShapeDtypeStruct((2, rh, W), q.dtype),
        scratch_shapes=[pltpu.SemaphoreType.DMA, pltpu.SemaphoreType.DMA],
    )(q)
    return out


def _sum_chips(parts, own, own_row0, which, name, out_dtype=F32):
    n, rh, W = parts.shape
    tr = PACK_TILE
    assert own_row0 % tr == 0
    if own.ndim == 3:
        own_spec = pl.BlockSpec((None, tr, W), lambda i, w_ref: (w_ref[0], own_row0 // tr + i, 0))
    else:
        own_spec = pl.BlockSpec((tr, W), lambda i, w_ref: (own_row0 // tr + i, 0))

    def body(w_ref, p_ref, own_ref, o_ref):
        mine = own_ref[...].astype(F32)
        acc = None
        for j in range(n):
            term = jnp.where(w_ref[0] == j, mine, p_ref[j].astype(F32))
            acc = term if acc is None else acc + term
        o_ref[...] = acc.astype(out_dtype)

    return pl.pallas_call(
        body, name=name,
        grid_spec=pltpu.PrefetchScalarGridSpec(
            num_scalar_prefetch=1, grid=(rh // tr,),
            in_specs=[pl.BlockSpec((n, tr, W), lambda i, w_ref: (0, i, 0)), own_spec],
            out_specs=pl.BlockSpec((tr, W), lambda i, w_ref: (i, 0))),
        out_shape=jax.ShapeDtypeStruct((rh, W), out_dtype),
        compiler_params=_cparams("parallel"))(jnp.reshape(which, (1,)).astype(jnp.int32), parts, own)


_SEM = pl.BlockSpec(memory_space=pltpu.SEMAPHORE)
_ASYNC = pltpu.CompilerParams(has_side_effects=pltpu.SideEffectType.DATAFLOW_SIDE_EFFECTING)


def _hbm(a):
    return pltpu.with_memory_space_constraint(a, pltpu.HBM)


def _gather_copies(w_ref, land_ref, send_sems, recv_sems):
    x, y, c, chips = _mesh_place()
    me = 2 * x + y
    rh = w_ref.shape[0] // 2
    rows = pl.ds(pl.multiple_of(c * rh, 16), rh)
    return [pltpu.make_async_remote_copy(
        src_ref=w_ref.at[rows], dst_ref=land_ref.at[me, rows], send_sem=send_sems.at[r],
        recv_sem=recv_sems.at[r], device_id=(px, py, c), device_id_type=MESH)
        for r, (px, py) in enumerate(chips)]


def _scatter_copies(g_ref, land_ref, send_sems, recv_sems, row0):
    x, y, c, chips = _mesh_place()
    me = 2 * x + y
    rows = pl.ds(row0, land_ref.shape[1])
    return [pltpu.make_async_remote_copy(
        src_ref=g_ref.at[2 * px + py, rows], dst_ref=land_ref.at[me], send_sem=send_sems.at[r],
        recv_sem=recv_sems.at[r], device_id=(px, py, c), device_id_type=MESH)
        for r, (px, py) in enumerate(chips)]


def _halves_to_sibling(land, name):
    n, R, W = land.shape
    rh = R // 2

    def body(l_ref, o_ref, send_sems, recv_sems):
        x, y, c, chips = _mesh_place()
        rows = pl.ds(pl.multiple_of(c * rh, 16), rh)
        copies = [pltpu.make_async_remote_copy(
            src_ref=o_ref.at[2 * px + py, rows], dst_ref=o_ref.at[2 * px + py, rows], send_sem=send_sems.at[r],
            recv_sem=recv_sems.at[r], device_id=(x, y, 1 - c), device_id_type=MESH)
            for r, (px, py) in enumerate(chips)]
        for cp in copies:
            cp.start()
        for cp in copies:
            cp.wait()

    return pl.pallas_call(
        body, name=name, in_specs=[_HBM], out_specs=_HBM, out_shape=jax.ShapeDtypeStruct(land.shape, land.dtype),
        scratch_shapes=[pltpu.SemaphoreType.DMA((3,)), pltpu.SemaphoreType.DMA((3,))],
        input_output_aliases={0: 0})(land)


def _gather_start(wp, name):
    R, W = wp.shape

    def body(w_ref, land_ref, send_sems, recv_sems, w_thru, land_thru, token):
        for cp in _gather_copies(w_ref, land_ref, send_sems, recv_sems):
            cp.start()
        token[...] = jnp.zeros_like(token)

    return pl.pallas_call(
        body, name=name,
        out_shape=(pltpu.SemaphoreType.DMA((3,)), pltpu.SemaphoreType.DMA((3,)), pltpu.HBM(wp.shape, wp.dtype),
                   pltpu.HBM((N_CHIPS, R, W), wp.dtype), jax.ShapeDtypeStruct((8, 128), F32)),
        in_specs=(_HBM, _HBM),
        out_specs=(_SEM, _SEM, _HBM, _HBM, pl.BlockSpec(memory_space=pltpu.VMEM)),
        input_output_aliases={0: 2, 1: 3}, compiler_params=_ASYNC,
    )(_hbm(wp), _hbm(lax.empty((N_CHIPS, R, W), wp.dtype)))


def _gather_wait(send_sems, recv_sems, w_thru, land_thru, after, name):
    R, W = w_thru.shape
    rh = R // 2

    def body(w_ref, land_ref, send_sems, recv_sems, after_ref, w_dead, got_ref):
        x, y, c, _ = _mesh_place()
        half = land_ref.at[0, pl.ds(0, rh)]
        for k in range(3):
            cp = pltpu.make_async_remote_copy(src_ref=half, dst_ref=half, send_sem=send_sems.at[k],
                                              recv_sem=recv_sems.at[k], device_id=(x, y, 1 - c),
                                              device_id_type=MESH)
            cp.wait_send()
            cp.wait_recv()

    return pl.pallas_call(
        body, name=name,
        out_shape=(pltpu.HBM(w_thru.shape, w_thru.dtype), pltpu.HBM(land_thru.shape, land_thru.dtype)),
        in_specs=(_HBM, _HBM, _SEM, _SEM, pl.BlockSpec(memory_space=pl.ANY)), out_specs=(_HBM, _HBM),
        input_output_aliases={0: 0, 1: 1}, compiler_params=_ASYNC,
    )(w_thru, land_thru, send_sems, recv_sems, after)


def _scatter_start(g, row0, nrows, name):
    n, R, W = g.shape
    land_shape = (n, nrows, W)

    def body(g_ref, land_ref, send_sems, recv_sems, g_thru, land_thru, token):
        for cp in _scatter_copies(g_ref, land_ref, send_sems, recv_sems, row0):
            cp.start()
        token[...] = jnp.zeros_like(token)

    return pl.pallas_call(
        body, name=name,
        out_shape=(pltpu.SemaphoreType.DMA((3,)), pltpu.SemaphoreType.DMA((3,)), pltpu.HBM(g.shape, g.dtype),
                   pltpu.HBM(land_shape, g.dtype), jax.ShapeDtypeStruct((8, 128), F32)),
        in_specs=(_HBM, _HBM),
        out_specs=(_SEM, _SEM, _HBM, _HBM, pl.BlockSpec(memory_space=pltpu.VMEM)),
        input_output_aliases={0: 2, 1: 3}, compiler_params=_ASYNC,
    )(_hbm(g), _hbm(lax.empty(land_shape, g.dtype)))


def _scatter_wait(send_sems, recv_sems, g_thru, land_thru, after, name):
    def body(g_ref, land_ref, send_sems, recv_sems, after_ref, g_out, got_ref):
        x, y, c, _ = _mesh_place()
        for k in range(3):
            cp = pltpu.make_async_remote_copy(src_ref=land_ref.at[0], dst_ref=land_ref.at[0], send_sem=send_sems.at[k],
                                              recv_sem=recv_sems.at[k], device_id=(x, y, 1 - c),
                                              device_id_type=MESH)
            cp.wait_send()
            cp.wait_recv()

    return pl.pallas_call(
        body, name=name,
        out_shape=(pltpu.HBM(g_thru.shape, g_thru.dtype), pltpu.HBM(land_thru.shape, land_thru.dtype)),
        in_specs=(_HBM, _HBM, _SEM, _SEM, pl.BlockSpec(memory_space=pl.ANY)), out_specs=(_HBM, _HBM),
        input_output_aliases={0: 0, 1: 1}, compiler_params=_ASYNC,
    )(g_thru, land_thru, send_sems, recv_sems, after)


def _adamw(w, g, m, v, name):
    shape = w.shape
    cols = shape[-1]
    w2, g2, m2, v2 = (t.reshape(-1, cols) for t in (w, g, m, v))
    rows = w2.shape[0]
    tr = rows
    for cand in (512, 256, 128, 64, 32, 16, 8):
        if rows > cand and rows % cand == 0:
            tr = cand
            break
    c1 = 1.0 / (1.0 - ADAM_B1 ** ADAM_STEP)
    c2 = 1.0 / (1.0 - ADAM_B2 ** ADAM_STEP)

    def body(w_ref, g_ref, m_ref, v_ref, d_ref, nm_ref, nv_ref):
        gv = g_ref[...]
        nm = ADAM_B1 * m_ref[...] + (1.0 - ADAM_B1) * gv
        nv = ADAM_B2 * v_ref[...] + (1.0 - ADAM_B2) * (gv * gv)
        nm_ref[...] = nm
        nv_ref[...] = nv
        d_ref[...] = -ADAM_LR * ((nm * c1) / (jnp.sqrt(nv * c2) + ADAM_EPS) + ADAM_WD * w_ref[...])

    blk = pl.BlockSpec((tr, cols), lambda i: (i, 0))
    sds = jax.ShapeDtypeStruct((rows, cols), F32)
    d, nm, nv = pl.pallas_call(body, name=name, grid=(rows // tr,), in_specs=[blk] * 4,
                               out_specs=(blk, blk, blk), out_shape=(sds, sds, sds),
                               compiler_params=_cparams("parallel"))(w2, g2, m2, v2)
    return d.reshape(shape), nm.reshape(shape), nv.reshape(shape)


def kernel(x, positions, norm_gains, mla_w_in, mla_q_norm, mla_kv_norm, mla_w_uq, mla_w_ukv, mla_w_o, hgrn_w_in, hgrn_lb_logits, hgrn_o_norm, hgrn_w_o, mlp_w1, mlp_w2, loss_target, m_norm_gains, m_mla_w_in, m_mla_q_norm, m_mla_kv_norm, m_mla_w_uq, m_mla_w_ukv, m_mla_w_o, m_hgrn_w_in, m_hgrn_lb_logits, m_hgrn_o_norm, m_hgrn_w_o, m_mlp_w1, m_mlp_w2, v_norm_gains, v_mla_w_in, v_mla_q_norm, v_mla_kv_norm, v_mla_w_uq, v_mla_w_ukv, v_mla_w_o, v_hgrn_w_in, v_hgrn_lb_logits, v_hgrn_o_norm, v_hgrn_w_o, v_mlp_w1, v_mlp_w2):
    w = dict(norm_gains=norm_gains, mla_w_in=mla_w_in, mla_q_norm=mla_q_norm, mla_kv_norm=mla_kv_norm,
             mla_w_uq=mla_w_uq, mla_w_ukv=mla_w_ukv, mla_w_o=mla_w_o, hgrn_w_in=hgrn_w_in,
             hgrn_lb_logits=hgrn_lb_logits, hgrn_o_norm=hgrn_o_norm, hgrn_w_o=hgrn_w_o,
             mlp_w1=mlp_w1, mlp_w2=mlp_w2)
    mom_m = dict(norm_gains=m_norm_gains, mla_w_in=m_mla_w_in, mla_q_norm=m_mla_q_norm,
                 mla_kv_norm=m_mla_kv_norm, mla_w_uq=m_mla_w_uq, mla_w_ukv=m_mla_w_ukv,
                 mla_w_o=m_mla_w_o, hgrn_w_in=m_hgrn_w_in, hgrn_lb_logits=m_hgrn_lb_logits,
                 hgrn_o_norm=m_hgrn_o_norm, hgrn_w_o=m_hgrn_w_o, mlp_w1=m_mlp_w1, mlp_w2=m_mlp_w2)
    mom_v = dict(norm_gains=v_norm_gains, mla_w_in=v_mla_w_in, mla_q_norm=v_mla_q_norm,
                 mla_kv_norm=v_mla_kv_norm, mla_w_uq=v_mla_w_uq, mla_w_ukv=v_mla_w_ukv,
                 mla_w_o=v_mla_w_o, hgrn_w_in=v_hgrn_w_in, hgrn_lb_logits=v_hgrn_lb_logits,
                 hgrn_o_norm=v_hgrn_o_norm, hgrn_w_o=v_hgrn_w_o, mlp_w1=v_mlp_w1, mlp_w2=v_mlp_w2)
    c = lax.axis_index("c")

    axis_of = dict(SHARDED)
    me = 2 * lax.axis_index("x") + lax.axis_index("y")
    gain_bits = lax.bitcast_convert_type(norm_gains, jnp.uint32)
    gain_hi = lax.bitcast_convert_type((gain_bits >> 16).astype(jnp.uint16), BF16)
    gain_lo = lax.bitcast_convert_type((gain_bits & 0xFFFF).astype(jnp.uint16), BF16)

    layers = []
    for l in range(DEPTH):
        s = l // 2
        if l % 2 == 0:
            big = [("mlp_w1", l), ("mlp_w2", l), ("mla_w_o", s)]
            tail = [("mla_w_in", s), ("mla_w_uq", s), ("mla_w_ukv", s)]
        else:
            big = [("hgrn_w_in", s), ("mlp_w1", l), ("mlp_w2", l), ("hgrn_w_o", s)]
            tail = []
        w_tail = [w[n][i] for n, i in tail] + ([gain_hi, gain_lo] if l == 0 else [])
        g_tail = tail + ([("norm_gains", None)] + [(n, None) for n in REPLICATED] if l == 0 else [])
        g_shapes = [w[n].shape if i is None else w[n][i].shape for n, i in g_tail]
        tail_rows = max(_packed_misc_rows([t.shape for t in w_tail]), _packed_misc_rows(g_shapes))
        pk = _Packed([(n, w[n].shape[1]) for n, _ in big], tail_rows)
        wpack = jnp.zeros((pk.rows, PACK_W), BF16)
        for n, i in big:
            assert w[n].shape[2] == PACK_W
            wpack = _cast_into(w[n][i], wpack, pk.off[n], name="pack_%s_%d" % (n, l))
        if w_tail:
            wpack = lax.dynamic_update_slice(
                wpack, jnp.concatenate(_pack_blocks(w_tail, 0, BF16), axis=0), (pk.misc, 0))
        layers.append(dict(pk=pk, big=big, tail=tail, w_tail=w_tail, g_tail=g_tail, g_shapes=g_shapes,
                           gather=_gather_start(wpack, name="gather_start_%d" % l)))

    def fetch(l, after):
        lay = layers[l]
        pk = lay["pk"]
        send_sems, recv_sems, w_thru, land_thru, _ = lay["gather"]
        if after is None:
            after = sum(layers[k]["gather"][4] for k in range(1, DEPTH))
        w_back, land = _gather_wait(send_sems, recv_sems, w_thru, land_thru, after, name="gather_wait_%d" % l)
        land = _halves_to_sibling(land, name="gather_halves_%d" % l)
        land = lax.dynamic_update_slice(land, w_back[None], (me, 0, 0))
        out = dict(wbuf=land.reshape(N_CHIPS * pk.rows, PACK_W), pk=pk)
        if lay["w_tail"]:
            rows = _packed_misc_rows([t.shape for t in lay["w_tail"]])
            per_chip = [_unpack(land[j, pk.misc:pk.misc + rows], [t.shape for t in lay["w_tail"]])
                        for j in range(N_CHIPS)]
            for i, (n, _) in enumerate(lay["tail"]):
                out[n[4:]] = jnp.concatenate([per_chip[j][i] for j in range(N_CHIPS)], axis=axis_of[n] - 1)
            if l == 0:
                got_hi, got_lo = (lax.bitcast_convert_type(
                    jnp.concatenate([per_chip[j][i] for j in range(N_CHIPS)], axis=2),
                    jnp.uint16).astype(jnp.uint32) for i in (-2, -1))
                out["gains"] = lax.bitcast_convert_type((got_hi << 16) | got_lo, F32)
        return out

    def emit(l, gbuf, grads):
        lay = layers[l]
        pk = lay["pk"]
        if lay["g_tail"]:
            for j in range(N_CHIPS):
                pieces = []
                for n, i in lay["g_tail"]:
                    if n not in axis_of:
                        pieces.append(grads[n])
                    else:
                        pieces.append(jnp.split(grads[n], N_CHIPS, axis=axis_of[n] - (0 if i is None else 1))[j])
                block = jnp.concatenate(_pack_blocks(pieces, 0, BF16), axis=0)
                gbuf = lax.dynamic_update_slice(gbuf, block, (j * pk.rows + pk.misc, 0))
        row0 = lay.get("early_rows", 0)
        lay["scatter"] = _scatter_start(gbuf.reshape(N_CHIPS, pk.rows, PACK_W), row0, pk.rows - row0,
                                        name="scatter_start_%d" % l)
        return lay["scatter"][4][0, 0]

    def emit_mlp(l, gbuf):
        lay = layers[l]
        pk = lay["pk"]
        assert pk.off["mlp_w1"] == 0 and pk.off["mlp_w2"] == w["mlp_w1"].shape[1]
        lay["early_rows"] = w["mlp_w1"].shape[1] + w["mlp_w2"].shape[1]
        lay["scatter_early"] = _scatter_start(gbuf.reshape(N_CHIPS, pk.rows, PACK_W), 0, lay["early_rows"],
                                              name="scatter_start_%d_mlp" % l)
        return lay["scatter_early"][2].reshape(N_CHIPS * pk.rows, PACK_W)

    small = dict(mla_q_norm=mla_q_norm, mla_kv_norm=mla_kv_norm, hgrn_lb_logits=hgrn_lb_logits,
                 hgrn_o_norm=hgrn_o_norm)
    gbufs = [lax.empty((N_CHIPS * lay["pk"].rows, PACK_W), BF16) for lay in layers]
    sq, grad_x = _local_step(x[0], positions[0], loss_target[0], small, fetch, gbufs, emit, emit_mlp)
    d_model = x.shape[-1]
    loss = lax.psum(0.5 * jnp.sum(sq) / d_model, ("x", "y", "c"))

    per_name = {}
    behind = grad_x
    for l, lay in reversed(list(enumerate(layers))):
        pk = lay["pk"]
        send_sems, recv_sems, g_thru, land_thru, _ = lay["scatter"]
        row0 = lay.get("early_rows", 0)
        early = None
        if row0:
            e_send, e_recv, _, e_land, _ = lay["scatter_early"]
            g_thru, land = _scatter_wait(e_send, e_recv, g_thru, e_land, behind, name="scatter_wait_%d_mlp" % l)
            early = behind = _sum_chips(land, g_thru, 0, me, name="grads_sum_chips_%d_mlp" % l, out_dtype=BF16)
        g_back, land = _scatter_wait(send_sems, recv_sems, g_thru, land_thru, behind, name="scatter_wait_%d" % l)
        mine = _sum_chips(land, g_back, row0, me, name="grads_sum_chips_%d" % l, out_dtype=BF16)
        if early is not None:
            mine = jnp.concatenate([early, mine], axis=0)
        red = behind = _sum_chips(_share_reduced(mine, name="grads_share_%d" % l), mine, 0, c,
                                  name="grads_sum_cores_%d" % l)
        for n, i in lay["big"]:
            per_name.setdefault(n, {})[i] = red[pk.off[n]:pk.off[n] + w[n].shape[1]]
        for (n, i), piece in zip(lay["g_tail"], _unpack(red[pk.misc:pk.misc + pk.misc_rows], lay["g_shapes"])):
            per_name.setdefault(n, {})[i] = piece
    g_out = {n: (parts[None] if None in parts else jnp.stack([parts[i] for i in sorted(parts)]))
             for n, parts in per_name.items()}

    deltas, new_m, new_v = {}, {}, {}
    for name in WEIGHTS:
        deltas[name], new_m[name], new_v[name] = _adamw(w[name], g_out[name], mom_m[name], mom_v[name],
                                                        name="adamw_" + name)
    return (loss, grad_x[None], *[g_out[n] for n in WEIGHTS], *[deltas[n] for n in WEIGHTS],
            *[new_m[n] for n in WEIGHTS], *[new_v[n] for n in WEIGHTS])
```

```python
import jax
import jax.numpy as jnp
from jax import lax
from jax.experimental import pallas as pl
from jax.experimental.pallas import tpu as pltpu

F32 = jnp.float32
BF16 = jnp.bfloat16
MESH = pl.DeviceIdType.MESH

DEPTH = 4
MLA_HEADS = 8
MLA_NOPE = 128
MLA_ROPE = 64
MLA_V = 128
MLA_QK_PAD = 256
MLA_HEADS_PER_STEP = 2
MLA_SCALE = float(MLA_NOPE + MLA_ROPE) ** -0.5
ROPE_BASE = 10000.0
HGRN_HEADS = 8
HGRN_CHUNK = 32
HGRN_BLOCK = 128
EPS = 1e-6

ADAM_LR = 0.001
ADAM_B1 = 0.9
ADAM_B2 = 0.999
ADAM_EPS = 1e-08
ADAM_WD = 0.01
ADAM_STEP = 10

N_CHIPS = 4
PACK_W = 1024
PACK_ALIGN = 1024
PACK_TILE = 512
V7X_VMEM_LIMIT = 56 * 1024 * 1024

SHARDED = (("norm_gains", 2), ("mla_w_in", 1), ("mla_w_uq", 2), ("mla_w_ukv", 2), ("mla_w_o", 1),
           ("hgrn_w_in", 2), ("hgrn_w_o", 1), ("mlp_w1", 2), ("mlp_w2", 1))
REPLICATED = ("mla_q_norm", "mla_kv_norm", "hgrn_lb_logits", "hgrn_o_norm")
WEIGHTS = ("norm_gains", "mla_w_in", "mla_q_norm", "mla_kv_norm", "mla_w_uq", "mla_w_ukv", "mla_w_o",
           "hgrn_w_in", "hgrn_lb_logits", "hgrn_o_norm", "hgrn_w_o", "mlp_w1", "mlp_w2")


def _cparams(*semantics):
    return pltpu.CompilerParams(dimension_semantics=semantics, vmem_limit_bytes=V7X_VMEM_LIMIT)


def _sigmoid(x):
    return 0.5 * jnp.tanh(0.5 * x) + 0.5


def _mm(a, b, *, ta=False, tb=False, out_dtype=F32, tm=2048, tn=1024, tk=1024, epi=None, extra=None,
        name="mm", n=None, b_map=None, into=None, o_map=None):
    if ta:
        K, M = a.shape
    else:
        M, K = a.shape
    if b_map is not None:
        N = n
    elif tb:
        N, Kb = b.shape
    else:
        Kb, N = b.shape
    assert b_map is not None or K == Kb, (a.shape, b.shape, ta, tb)
    tm, tn = min(tm, M), min(tn, N)
    tk = K if (K <= 1024 and b_map is None) else min(tk, K)
    assert M % tm == 0 and N % tn == 0 and K % tk == 0, (M, N, K, tm, tn, tk)
    nk = K // tk
    a_spec = (pl.BlockSpec((tk, tm), lambda i, j, k: (k, i)) if ta
              else pl.BlockSpec((tm, tk), lambda i, j, k: (i, k)))
    if b_map is None:
        b_map = (lambda i, j, k: (j, k)) if tb else (lambda i, j, k: (k, j))
    b_spec = pl.BlockSpec((tn, tk) if tb else (tk, tn), b_map)
    o_spec = pl.BlockSpec((tm, tn), lambda i, j, k: (i, j))
    dims = (((0 if ta else 1,), (1 if tb else 0,)), ((), ()))
    in_specs = [a_spec, b_spec]
    operands = [a, b]
    aliases = {}
    if epi == "mul2r":
        in_specs.append(o_spec)
        operands.append(extra)
    if epi == "resnorm":
        assert tn == N
        vec = pl.BlockSpec((1, N), lambda i, j, k: (0, 0))
        in_specs += [o_spec, vec, vec]
        operands += list(extra)
    if into is not None:
        assert epi is None
        in_specs.append(pl.BlockSpec(memory_space=pl.ANY))
        operands.append(into)
        aliases = {2: 0}
        out_dtype = into.dtype
        out_shape = jax.ShapeDtypeStruct(into.shape, into.dtype)
        out_specs = pl.BlockSpec((tm, tn), o_map)
    elif epi == "relu2":
        out_shape = (jax.ShapeDtypeStruct((M, N), BF16), jax.ShapeDtypeStruct((M, N), BF16))
        out_specs = (o_spec, o_spec)
    elif epi == "mul2r":
        out_shape = jax.ShapeDtypeStruct((M, N), BF16)
        out_specs = o_spec
    elif epi == "resnorm":
        out_shape = (jax.ShapeDtypeStruct((M, N), F32), jax.ShapeDtypeStruct((M, N), F32),
                     jax.ShapeDtypeStruct((M, N), BF16))
        out_specs = (o_spec, o_spec, o_spec)
    else:
        out_shape = jax.ShapeDtypeStruct((M, N), out_dtype)
        out_specs = o_spec
    n_in = len(operands)
    n_out = {"relu2": 2, "resnorm": 3}.get(epi, 1)

    def body(*refs):
        a_ref, b_ref = refs[0], refs[1]
        outs = refs[n_in:n_in + n_out]
        k = pl.program_id(2)

        def finish(acc):
            if epi == "relu2":
                r = jnp.maximum(acc, 0.0)
                outs[0][...] = (r * r).astype(BF16)
                outs[1][...] = r.astype(BF16)
            elif epi == "mul2r":
                outs[0][...] = (acc * (2.0 * refs[2][...].astype(F32))).astype(BF16)
            elif epi == "resnorm":
                h_ref, gp_ref, gn_ref = refs[2], refs[3], refs[4]
                hn = h_ref[...] + acc * _rms_rstd(acc) * gp_ref[...]
                outs[0][...] = acc
                outs[1][...] = hn
                outs[2][...] = (hn * _rms_rstd(hn) * gn_ref[...]).astype(BF16)
            else:
                outs[0][...] = acc.astype(out_dtype)

        part = lax.dot_general(a_ref[...], b_ref[...], dims, preferred_element_type=F32)
        if nk == 1:
            finish(part)
            return
        acc_ref = refs[-1]

        @pl.when(k == 0)
        def _():
            acc_ref[...] = part

        @pl.when((k > 0) & (k < nk - 1))
        def _():
            acc_ref[...] += part

        @pl.when(k == nk - 1)
        def _():
            finish(acc_ref[...] + part)

    return pl.pallas_call(
        body, name=name, grid=(M // tm, N // tn, nk), in_specs=in_specs, out_specs=out_specs,
        out_shape=out_shape, scratch_shapes=[pltpu.VMEM((tm, tn), F32)] if nk > 1 else [],
        input_output_aliases=aliases,
        compiler_params=_cparams("parallel", "parallel", "arbitrary"))(*operands)


def _rms_rstd(x):
    return lax.rsqrt(jnp.mean(x * x, axis=-1, keepdims=True) + EPS)


def _rms_bwd_tile(x, g, dy):
    r = _rms_rstd(x)
    xh = x * r
    u = dy * g
    dx = r * (u - xh * jnp.mean(u * xh, axis=-1, keepdims=True))
    dg = jnp.sum(dy * xh, axis=0, keepdims=True)
    return dx, dg


def _row_tile(T):
    return min(512, T)


def _mid_tile(T):
    return min(256, T)


def _prenorm_fwd(x, g, name="prenorm_fwd"):
    T, D = x.shape
    tm = _row_tile(T)

    def body(x_ref, g_ref, a_ref):
        xv = x_ref[...]
        a_ref[...] = (xv * _rms_rstd(xv) * g_ref[...]).astype(BF16)

    row = pl.BlockSpec((tm, D), lambda i: (i, 0))
    vec = pl.BlockSpec((1, D), lambda i: (0, 0))
    return pl.pallas_call(body, name=name, grid=(T // tm,), in_specs=[row, vec], out_specs=row,
                          out_shape=jax.ShapeDtypeStruct((T, D), BF16),
                          compiler_params=_cparams("parallel"))(x, g)


def _resnorm_loss(h, z, g_post, target, name="resnorm_loss"):
    T, D = h.shape
    tm = _row_tile(T)

    def body(h_ref, z_ref, gp_ref, t_ref, dy_ref, sq_ref):
        zv = z_ref[...]
        err = h_ref[...] + zv * _rms_rstd(zv) * gp_ref[...] - t_ref[...]
        dy_ref[...] = err * (1.0 / D)

        @pl.when(pl.program_id(0) == 0)
        def _():
            sq_ref[...] = jnp.zeros_like(sq_ref)

        sq_ref[...] += jnp.sum(err * err, axis=0, keepdims=True)

    row = pl.BlockSpec((tm, D), lambda i: (i, 0))
    vec = pl.BlockSpec((1, D), lambda i: (0, 0))
    return pl.pallas_call(body, name=name, grid=(T // tm,), in_specs=[row, row, vec, row],
                          out_specs=(row, vec),
                          out_shape=(jax.ShapeDtypeStruct((T, D), F32), jax.ShapeDtypeStruct((1, D), F32)),
                          compiler_params=_cparams("arbitrary"))(h, z, g_post, target)


def _resnorm_bwd(z, g_post, dh, h_new=None, da=None, g_pre=None, name="resnorm_bwd"):
    T, D = z.shape
    tm = _row_tile(T)
    has_next = h_new is not None
    row = pl.BlockSpec((tm, D), lambda i: (i, 0))
    vec = pl.BlockSpec((1, D), lambda i: (0, 0))

    if has_next:
        def body(z_ref, gp_ref, dh_ref, hn_ref, da_ref, gn_ref, t_ref, dz_ref, dgp_ref, dgn_ref):
            first = pl.program_id(0) == 0

            @pl.when(first)
            def _():
                dgp_ref[...] = jnp.zeros_like(dgp_ref)
                dgn_ref[...] = jnp.zeros_like(dgn_ref)

            dpre, dgn = _rms_bwd_tile(hn_ref[...], gn_ref[...], da_ref[...])
            t = dh_ref[...] + dpre
            t_ref[...] = t
            dz, dgp = _rms_bwd_tile(z_ref[...], gp_ref[...], t)
            dz_ref[...] = dz.astype(BF16)
            dgp_ref[...] += dgp
            dgn_ref[...] += dgn

        return pl.pallas_call(
            body, name=name, grid=(T // tm,), in_specs=[row, vec, row, row, row, vec],
            out_specs=(row, row, vec, vec),
            out_shape=(jax.ShapeDtypeStruct((T, D), F32), jax.ShapeDtypeStruct((T, D), BF16),
                       jax.ShapeDtypeStruct((1, D), F32), jax.ShapeDtypeStruct((1, D), F32)),
            compiler_params=_cparams("arbitrary"))(z, g_post, dh, h_new, da, g_pre)

    def body_last(z_ref, gp_ref, dh_ref, dz_ref, dgp_ref):
        @pl.when(pl.program_id(0) == 0)
        def _():
            dgp_ref[...] = jnp.zeros_like(dgp_ref)

        dz, dgp = _rms_bwd_tile(z_ref[...], gp_ref[...], dh_ref[...])
        dz_ref[...] = dz.astype(BF16)
        dgp_ref[...] += dgp

    return pl.pallas_call(
        body_last, name=name, grid=(T // tm,), in_specs=[row, vec, row], out_specs=(row, vec),
        out_shape=(jax.ShapeDtypeStruct((T, D), BF16), jax.ShapeDtypeStruct((1, D), F32)),
        compiler_params=_cparams("arbitrary"))(z, g_post, dh)


def _prenorm_bwd(x, g, dh, da, name="prenorm_bwd"):
    T, D = x.shape
    tm = _row_tile(T)

    def body(x_ref, g_ref, dh_ref, da_ref, dx_ref, dg_ref):
        @pl.when(pl.program_id(0) == 0)
        def _():
            dg_ref[...] = jnp.zeros_like(dg_ref)

        dpre, dg = _rms_bwd_tile(x_ref[...], g_ref[...], da_ref[...])
        dx_ref[...] = dh_ref[...] + dpre
        dg_ref[...] += dg

    row = pl.BlockSpec((tm, D), lambda i: (i, 0))
    vec = pl.BlockSpec((1, D), lambda i: (0, 0))
    return pl.pallas_call(
        body, name=name, grid=(T // tm,), in_specs=[row, vec, row, row], out_specs=(row, vec),
        out_shape=(jax.ShapeDtypeStruct((T, D), F32), jax.ShapeDtypeStruct((1, D), F32)),
        compiler_params=_cparams("arbitrary"))(x, g, dh, da)


class _Packed:
    def __init__(self, big, misc_rows):
        self.big = tuple(big)
        self.off = {}
        r = 0
        for name, rows in big:
            self.off[name] = r
            r += rows
        self.misc, self.misc_rows = r, misc_rows
        self.rows = -(-(r + misc_rows) // PACK_ALIGN) * PACK_ALIGN

    def block(self, name, layer, unit):
        r = self.off[name]
        assert r % unit == 0 and self.rows % unit == 0
        return r // unit, self.rows // unit


def _col_sharded(pk, name, layer, unit):
    base, stride = pk.block(name, layer, unit)
    return (lambda i, j, k: (j * stride + base, 0)), (lambda i, j, k: (k * stride + base, 0))


def _row_sharded(pk, name, layer, unit):
    base, stride = pk.block(name, layer, unit)
    return ((lambda i, j, k: (k * stride + base, 0)), (lambda i, j, k: (j * stride + base, 0)),
            (lambda i, j, k: (i * stride + base, 0)))


def _mlp_fwd(a, wbuf, pk, layer, res):
    D = a.shape[1]
    by_n, _ = _col_sharded(pk, "mlp_w1", layer, D)
    by_k, _, _ = _row_sharded(pk, "mlp_w2", layer, D)
    act, r = _mm(a, wbuf, n=4 * D, b_map=by_n, tk=D, tn=D, epi="relu2", name="mlp_up")
    if res is None:
        return _mm(act, wbuf, n=D, b_map=by_k, tk=D, tn=D, name="mlp_down"), (a, act, r), None, None
    u, h_new, a_next = _mm(act, wbuf, n=D, b_map=by_k, tm=1024, tk=D, tn=D, epi="resnorm", extra=res,
                           name="mlp_down_res")
    return u, (a, act, r), h_new, a_next


def _mlp_bwd(du, saved, wbuf, gbuf, pk, layer):
    a, act, r = saved
    D = a.shape[1]
    w1_by_n, w1_by_k = _col_sharded(pk, "mlp_w1", layer, D)
    _, w2_by_n, w2_by_m = _row_sharded(pk, "mlp_w2", layer, D)
    dz1 = _mm(du, wbuf, tb=True, n=4 * D, b_map=w2_by_n, tn=D, tk=D, epi="mul2r", extra=r, name="mlp_down_dx")
    gbuf = _mm(act, du, ta=True, into=gbuf, o_map=w2_by_m, tm=D, tn=D, name="mlp_down_dw")
    gbuf = _mm(a, dz1, ta=True, into=gbuf, o_map=w1_by_n, tm=D, tn=D, name="mlp_up_dw")
    da = _mm(dz1, wbuf, tb=True, n=D, b_map=w1_by_k, tn=D, tk=D, name="mlp_up_dx")
    return da, gbuf


def _rope_swap(t):
    n = t.shape[-1]
    lane = lax.broadcasted_iota(jnp.int32, t.shape, t.ndim - 1)
    half = MLA_ROPE // 2
    first = (lane & (MLA_ROPE - 1)) < half
    return jnp.where(first, pltpu.roll(t, n - half, t.ndim - 1), pltpu.roll(t, half, t.ndim - 1))


def _mla_mid_fwd(proj, q_norm, kv_norm, w_uq, w_ukv, cc, ss):
    T, PW = proj.shape
    QL, KVL = q_norm.shape[-1], kv_norm.shape[-1]
    H = MLA_HEADS
    assert PW == QL + KVL + 128
    tm = _mid_tile(T)

    def body(p_ref, qn_ref, kn_ref, wq_ref, wkv_ref, cc_ref, ss_ref,
             cq_ref, ckv_ref, q_ref, k_ref, v_ref):
        cq = p_ref[:, 0:QL]
        ckv = p_ref[:, QL:QL + KVL]
        kr = p_ref[:, QL + KVL:QL + KVL + 128]
        c, s = cc_ref[...], ss_ref[...]
        cqn = (cq * _rms_rstd(cq) * qn_ref[...]).astype(BF16)
        ckvn = (ckv * _rms_rstd(ckv) * kn_ref[...]).astype(BF16)
        cq_ref[...] = cqn
        ckv_ref[...] = ckvn
        q = jnp.dot(cqn, wq_ref[...], preferred_element_type=F32)
        kv = jnp.dot(ckvn, wkv_ref[...], preferred_element_type=F32)
        krf = (kr * c + _rope_swap(kr) * s).astype(BF16)
        for h in range(H):
            o = h * MLA_QK_PAD
            q_ref[:, o:o + MLA_NOPE] = (q[:, o:o + MLA_NOPE] * MLA_SCALE).astype(BF16)
            qr = q[:, o + MLA_NOPE:o + MLA_QK_PAD]
            q_ref[:, o + MLA_NOPE:o + MLA_QK_PAD] = ((qr * c + _rope_swap(qr) * s) * MLA_SCALE).astype(BF16)
            k_ref[:, o:o + MLA_NOPE] = kv[:, o:o + MLA_NOPE].astype(BF16)
            k_ref[:, o + MLA_NOPE:o + MLA_QK_PAD] = krf
            v_ref[:, h * MLA_V:(h + 1) * MLA_V] = kv[:, o + MLA_NOPE:o + MLA_QK_PAD].astype(BF16)

    def row(w):
        return pl.BlockSpec((tm, w), lambda i: (i, 0))

    def full(shape):
        return pl.BlockSpec(shape, lambda i: (0, 0))

    return pl.pallas_call(
        body, name="mla_mid_fwd", grid=(T // tm,),
        in_specs=[row(PW), full((1, QL)), full((1, KVL)), full(w_uq.shape), full(w_ukv.shape),
                  row(128), row(128)],
        out_specs=(row(QL), row(KVL), row(H * MLA_QK_PAD), row(H * MLA_QK_PAD), row(H * MLA_V)),
        out_shape=(jax.ShapeDtypeStruct((T, QL), BF16), jax.ShapeDtypeStruct((T, KVL), BF16),
                   jax.ShapeDtypeStruct((T, H * MLA_QK_PAD), BF16),
                   jax.ShapeDtypeStruct((T, H * MLA_QK_PAD), BF16),
                   jax.ShapeDtypeStruct((T, H * MLA_V), BF16)),
        compiler_params=_cparams("parallel"))(proj, q_norm, kv_norm, w_uq, w_ukv, cc, ss)


def _mla_mid_bwd(proj, q_norm, kv_norm, w_uq, w_ukv, cc, ss, dq, dk, dv):
    T, PW = proj.shape
    QL, KVL = q_norm.shape[-1], kv_norm.shape[-1]
    H = MLA_HEADS
    tm = _mid_tile(T)
    nt = (((1,), (1,)), ((), ()))

    def body(p_ref, qn_ref, kn_ref, wq_ref, wkv_ref, cc_ref, ss_ref, dq_ref, dk_ref, dv_ref,
             dqp_ref, dkv_ref, dp_ref, dqn_ref, dkn_ref):
        @pl.when(pl.program_id(0) == 0)
        def _():
            dqn_ref[...] = jnp.zeros_like(dqn_ref)
            dkn_ref[...] = jnp.zeros_like(dkn_ref)

        c, s = cc_ref[...], ss_ref[...]
        dkr = jnp.zeros((tm, 128), F32)
        for h in range(H):
            o = h * MLA_QK_PAD
            dqp_ref[:, o:o + MLA_NOPE] = (dq_ref[:, o:o + MLA_NOPE] * MLA_SCALE).astype(BF16)
            dqr = dq_ref[:, o + MLA_NOPE:o + MLA_QK_PAD] * MLA_SCALE
            dqp_ref[:, o + MLA_NOPE:o + MLA_QK_PAD] = (dqr * c + _rope_swap(dqr * s)).astype(BF16)
            dkv_ref[:, o:o + MLA_NOPE] = dk_ref[:, o:o + MLA_NOPE].astype(BF16)
            dkv_ref[:, o + MLA_NOPE:o + MLA_QK_PAD] = dv_ref[:, h * MLA_V:(h + 1) * MLA_V].astype(BF16)
            dkr = dkr + dk_ref[:, o + MLA_NOPE:o + MLA_QK_PAD]
        dcqn = lax.dot_general(dqp_ref[...], wq_ref[...], nt, preferred_element_type=F32)
        dckvn = lax.dot_general(dkv_ref[...], wkv_ref[...], nt, preferred_element_type=F32)
        dcq, dqn = _rms_bwd_tile(p_ref[:, 0:QL], qn_ref[...], dcqn)
        dckv, dkn = _rms_bwd_tile(p_ref[:, QL:QL + KVL], kn_ref[...], dckvn)
        dp_ref[:, 0:QL] = dcq.astype(BF16)
        dp_ref[:, QL:QL + KVL] = dckv.astype(BF16)
        dp_ref[:, QL + KVL:QL + KVL + 128] = (dkr * c + _rope_swap(dkr * s)).astype(BF16)
        dqn_ref[...] += dqn
        dkn_ref[...] += dkn

    def row(w):
        return pl.BlockSpec((tm, w), lambda i: (i, 0))

    def full(shape):
        return pl.BlockSpec(shape, lambda i: (0, 0))

    return pl.pallas_call(
        body, name="mla_mid_bwd", grid=(T // tm,),
        in_specs=[row(PW), full((1, QL)), full((1, KVL)), full(w_uq.shape), full(w_ukv.shape),
                  row(128), row(128), row(H * MLA_QK_PAD), row(H * MLA_QK_PAD), row(H * MLA_V)],
        out_specs=(row(H * MLA_QK_PAD), row(H * MLA_QK_PAD), row(PW), full((1, QL)), full((1, KVL))),
        out_shape=(jax.ShapeDtypeStruct((T, H * MLA_QK_PAD), BF16),
                   jax.ShapeDtypeStruct((T, H * MLA_QK_PAD), BF16),
                   jax.ShapeDtypeStruct((T, PW), BF16),
                   jax.ShapeDtypeStruct((1, QL), F32), jax.ShapeDtypeStruct((1, KVL), F32)),
        compiler_params=_cparams("arbitrary"))(proj, q_norm, kv_norm, w_uq, w_ukv, cc, ss, dq, dk, dv)


def _attn_tile(T):
    return min(1024, T)


def _attn_pairs(n, by_key):
    if by_key:
        pairs = [(qi, ki) for ki in range(n) for qi in range(ki, n)]
    else:
        pairs = [(qi, ki) for qi in range(n) for ki in range(qi + 1)]
    return (jnp.asarray([p[0] for p in pairs], jnp.int32), jnp.asarray([p[1] for p in pairs], jnp.int32))


def _scores(q, k, diagonal):
    s = lax.dot_general(q, k, (((1,), (1,)), ((), ())), preferred_element_type=F32)
    if diagonal:
        rows = lax.broadcasted_iota(jnp.int32, s.shape, 0)
        cols = lax.broadcasted_iota(jnp.int32, s.shape, 1)
        s = jnp.where(rows >= cols, s, -jnp.inf)
    return s


def _attn_fwd(q, k, v):
    T = q.shape[0]
    H, DQ, DV = MLA_HEADS, MLA_QK_PAD, MLA_V
    tq = _attn_tile(T)
    nq = T // tq
    G = MLA_HEADS_PER_STEP
    qi_tab, ki_tab = _attn_pairs(nq, by_key=False)

    def body(qi_ref, ki_ref, q_ref, k_ref, v_ref, o_ref, lse_ref, *scratch):
        m_refs, l_refs, acc_refs = scratch[0:G], scratch[G:2 * G], scratch[2 * G:3 * G]
        p = pl.program_id(1)
        qi, ki = qi_ref[p], ki_ref[p]

        @pl.when(ki == 0)
        def _():
            for g in range(G):
                m_refs[g][...] = jnp.full_like(m_refs[g], -jnp.inf)
                l_refs[g][...] = jnp.zeros_like(l_refs[g])
                acc_refs[g][...] = jnp.zeros_like(acc_refs[g])

        def update(ks, qr, masked):
            for g in range(G):
                qs, vs = slice(g * DQ, (g + 1) * DQ), slice(g * DV, (g + 1) * DV)
                st = _scores(k_ref[ks, qs], q_ref[qr, qs], False)
                if masked:
                    key = ks.start + lax.broadcasted_iota(jnp.int32, st.shape, 0)
                    qry = qr.start + lax.broadcasted_iota(jnp.int32, st.shape, 1)
                    st = jnp.where(qry >= key, st, -jnp.inf)
                m_prev = m_refs[g][:, qr]
                m_new = jnp.maximum(m_prev, jnp.max(st, axis=0, keepdims=True))
                alpha = jnp.exp(m_prev - m_new)
                pt = jnp.exp(st - m_new)
                l_refs[g][:, qr] = alpha * l_refs[g][:, qr] + jnp.sum(pt, axis=0, keepdims=True)
                acc_refs[g][:, qr] = alpha * acc_refs[g][:, qr] + lax.dot_general(
                    v_ref[ks, vs], pt.astype(BF16), (((0,), (0,)), ((), ())), preferred_element_type=F32)
                m_refs[g][:, qr] = m_new

        whole, half = slice(0, tq), tq // 2

        @pl.when(ki < qi)
        def _():
            update(whole, whole, False)

        @pl.when(ki == qi)
        def _():
            update(slice(0, half), whole, True)
            update(slice(half, tq), slice(half, tq), True)
            for g in range(G):
                vs = slice(g * DV, (g + 1) * DV)
                o_ref[:, vs] = jnp.transpose(acc_refs[g][...] / l_refs[g][...]).astype(BF16)
                lse_ref[g] = m_refs[g][...] + jnp.log(l_refs[g][...])

    return pl.pallas_call(
        body, name="attn_fwd",
        grid_spec=pltpu.PrefetchScalarGridSpec(
            num_scalar_prefetch=2, grid=(H // G, int(qi_tab.shape[0])),
            in_specs=[pl.BlockSpec((tq, G * DQ), lambda h, p, qt, kt: (qt[p], h)),
                      pl.BlockSpec((tq, G * DQ), lambda h, p, qt, kt: (kt[p], h)),
                      pl.BlockSpec((tq, G * DV), lambda h, p, qt, kt: (kt[p], h))],
            out_specs=(pl.BlockSpec((tq, G * DV), lambda h, p, qt, kt: (qt[p], h)),
                       pl.BlockSpec((G, 1, tq), lambda h, p, qt, kt: (h, 0, qt[p]))),
            scratch_shapes=([pltpu.VMEM((1, tq), F32)] * (2 * G) + [pltpu.VMEM((DV, tq), F32)] * G)),
        out_shape=(jax.ShapeDtypeStruct((T, H * DV), BF16), jax.ShapeDtypeStruct((H, 1, T), F32)),
        compiler_params=_cparams("parallel", "arbitrary"))(qi_tab, ki_tab, q, k, v)


def _attn_bwd(q, k, v, o, do, lse):
    T = q.shape[0]
    H, DQ, DV = MLA_HEADS, MLA_QK_PAD, MLA_V
    tq = _attn_tile(T)
    nq = T // tq
    tn = (((0,), (0,)), ((), ()))
    nt = (((1,), (1,)), ((), ()))
    G = MLA_HEADS_PER_STEP
    qi_tab, ki_tab = _attn_pairs(nq, by_key=True)

    def body(qi_ref, ki_ref, q_ref, k_ref, v_ref, o_ref, do_ref, lse_ref, dq_ref, dk_ref, dv_ref,
             dk_acc, dv_acc):
        p = pl.program_id(1)
        qi, ki = qi_ref[p], ki_ref[p]

        @pl.when(p == 0)
        def _():
            dq_ref[...] = jnp.zeros_like(dq_ref)

        @pl.when(qi == ki)
        def _():
            dk_acc[...] = jnp.zeros_like(dk_acc)
            dv_acc[...] = jnp.zeros_like(dv_acc)

        def step(ks, qr, masked):
            rows = pl.ds(pl.multiple_of(qi * tq + qr.start, qr.stop - qr.start), qr.stop - qr.start)
            for g in range(G):
                qs, vs = slice(g * DQ, (g + 1) * DQ), slice(g * DV, (g + 1) * DV)
                dof = do_ref[qr, vs]
                delta = jnp.sum(jnp.transpose(dof.astype(F32) * o_ref[qr, vs].astype(F32)), axis=0,
                                keepdims=True)
                st = _scores(k_ref[ks, qs], q_ref[qr, qs], False)
                if masked:
                    key = ks.start + lax.broadcasted_iota(jnp.int32, st.shape, 0)
                    qry = qr.start + lax.broadcasted_iota(jnp.int32, st.shape, 1)
                    st = jnp.where(qry >= key, st, -jnp.inf)
                pt = jnp.exp(st - lse_ref[g][:, qr])
                dpt = lax.dot_general(v_ref[ks, vs], dof, nt, preferred_element_type=F32)
                dst = (pt * (dpt - delta)).astype(BF16)
                dv_acc[ks, vs] += jnp.dot(pt.astype(BF16), dof, preferred_element_type=F32)
                dk_acc[ks, qs] += jnp.dot(dst, q_ref[qr, qs], preferred_element_type=F32)
                dq_ref[rows, qs] += lax.dot_general(dst, k_ref[ks, qs], tn, preferred_element_type=F32)

        whole, half = slice(0, tq), tq // 2

        @pl.when(qi == ki)
        def _():
            step(slice(0, half), whole, True)
            step(slice(half, tq), slice(half, tq), True)

        @pl.when(qi > ki)
        def _():
            step(whole, whole, False)

        @pl.when(qi == nq - 1)
        def _():
            dk_ref[...] = dk_acc[...]
            dv_ref[...] = dv_acc[...]

    qspec = pl.BlockSpec((tq, G * DQ), lambda h, p, qt, kt: (qt[p], h))
    ospec = pl.BlockSpec((tq, G * DV), lambda h, p, qt, kt: (qt[p], h))
    kspec = pl.BlockSpec((tq, G * DQ), lambda h, p, qt, kt: (kt[p], h))
    vspec = pl.BlockSpec((tq, G * DV), lambda h, p, qt, kt: (kt[p], h))
    return pl.pallas_call(
        body, name="attn_bwd",
        grid_spec=pltpu.PrefetchScalarGridSpec(
            num_scalar_prefetch=2, grid=(H // G, int(qi_tab.shape[0])),
            in_specs=[qspec, kspec, vspec, ospec, ospec,
                      pl.BlockSpec((G, 1, tq), lambda h, p, qt, kt: (h, 0, qt[p]))],
            out_specs=(pl.BlockSpec((T, G * DQ), lambda h, p, qt, kt: (0, h)), kspec, vspec),
            scratch_shapes=[pltpu.VMEM((tq, G * DQ), F32), pltpu.VMEM((tq, G * DV), F32)]),
        out_shape=(jax.ShapeDtypeStruct((T, H * DQ), F32), jax.ShapeDtypeStruct((T, H * DQ), F32),
                   jax.ShapeDtypeStruct((T, H * DV), F32)),
        compiler_params=_cparams("parallel", "arbitrary"))(qi_tab, ki_tab, q, k, v, o, do, lse)


def _mla_fwd(a, w, cc, ss, wbuf, pk, slot, res):
    D = a.shape[1]
    by_k, _, _ = _row_sharded(pk, "mla_w_o", slot, D // N_CHIPS)
    proj = _mm(a, w["w_in"], name="mla_in")
    cqn, ckvn, q, k, v = _mla_mid_fwd(proj, w["q_norm"], w["kv_norm"], w["w_uq"], w["w_ukv"], cc, ss)
    o, lse = _attn_fwd(q, k, v)
    m, h_new, a_next = _mm(o, wbuf, n=D, b_map=by_k, tm=1024, tk=D // N_CHIPS, tn=D, epi="resnorm", extra=res,
                           name="mla_out_res")
    return m, (a, proj, cqn, ckvn, q, k, v, o, lse), h_new, a_next


def _mla_bwd(dm, saved, w, cc, ss, wbuf, gbuf, pk, slot):
    a, proj, cqn, ckvn, q, k, v, o, lse = saved
    D = a.shape[1]
    _, by_n, by_m = _row_sharded(pk, "mla_w_o", slot, D // N_CHIPS)
    do = _mm(dm, wbuf, tb=True, n=o.shape[1], b_map=by_n, tm=2048, tn=D // N_CHIPS, tk=D, out_dtype=BF16,
             name="mla_out_dx")
    gbuf = _mm(o, dm, ta=True, into=gbuf, o_map=by_m, tm=D // N_CHIPS, tn=D, tk=2048, name="mla_out_dw")
    dq, dk, dv = _attn_bwd(q, k, v, o, do, lse)
    dqp, dkv, dproj, dqn, dkn = _mla_mid_bwd(proj, w["q_norm"], w["kv_norm"], w["w_uq"], w["w_ukv"],
                                             cc, ss, dq, dk, dv)
    dw_uq = _mm(cqn, dqp, ta=True, out_dtype=BF16, name="mla_uq_dw")
    dw_ukv = _mm(ckvn, dkv, ta=True, out_dtype=BF16, name="mla_ukv_dw")
    dw_in = _mm(a, dproj, ta=True, out_dtype=BF16, name="mla_in_dw")
    da = _mm(dproj, w["w_in"], tb=True, name="mla_in_dx")
    return da, gbuf, dict(w_in=dw_in, w_uq=dw_uq, w_ukv=dw_ukv, q_norm=dqn, kv_norm=dkn)


def _split_dot(mat, x, parts):
    acc = None
    rem = x
    for _ in range(parts):
        piece = rem.astype(BF16)
        term = jnp.dot(mat, piece, preferred_element_type=F32)
        acc = term if acc is None else acc + term
        rem = rem - piece.astype(F32)
    return acc


def _chunk_sums(cum, rel, rest, logf):
    return tuple(_split_dot(m.astype(BF16), logf, 3) for m in (cum, rel, rest))


def _chunk_mats(tb):
    C = HGRN_CHUNK
    assert C & (C - 1) == 0
    r = lax.broadcasted_iota(jnp.int32, (tb, tb), 0)
    s = lax.broadcasted_iota(jnp.int32, (tb, tb), 1)
    start = r & ~(C - 1)
    same = start == (s & ~(C - 1))
    ref = start + C // 2
    last = start + C - 1
    one, zero = jnp.float32(1.0), jnp.float32(0.0)
    cum = jnp.where(same & (s <= r), one, zero)
    rel = cum - jnp.where(same & (s <= ref), one, zero)
    rest = jnp.where(same & (s > r) & (s <= last), one, zero)
    rev = jnp.where(same & (s >= r), one, zero)
    ones = jnp.where(same, one, zero)
    causal = same & (s <= r)
    return cum, rel, rest, rev, ones, causal


def _hgrn_gates(p_ref, lb, HK):
    qx = p_ref[:, 0:HK]
    fx = p_ref[:, HK:2 * HK]
    sf = _sigmoid(fx)
    f = lb + (1.0 - lb) * sf
    sq = _sigmoid(qx)
    return qx, sq, qx * sq, sf, f, 1.0 - f, jnp.log(f)


def _hgrn_fwd(proj, lb, o_norm):
    T = proj.shape[0]
    H, C = HGRN_HEADS, HGRN_CHUNK
    HK = proj.shape[1] // 4
    DK = HK // H
    tb = min(HGRN_BLOCK, T)
    ncb = tb // C
    nt = (((1,), (1,)), ((), ()))
    tn = (((0,), (0,)), ((), ()))

    def body(p_ref, lb_ref, on_ref, y_ref, o_ref, st_ref, state, oacc):
        @pl.when(pl.program_id(0) == 0)
        def _():
            state[...] = jnp.zeros_like(state)

        cum, rel, rest, _, _, causal = _chunk_mats(tb)
        _, _, q, _, f, k, logf = _hgrn_gates(p_ref, lb_ref[...], HK)
        b, brel, brest = _chunk_sums(cum, rel, rest, logf)
        eb = jnp.exp(b)
        q_rel = (q * jnp.exp(brel)).astype(BF16)
        k_rel = (k * jnp.exp(-brel)).astype(BF16)
        q_dec = (q * eb).astype(BF16)
        k_dec = (k * jnp.exp(brest)).astype(BF16)
        v = p_ref[:, 2 * HK:3 * HK].astype(BF16)
        for h in range(H):
            hs = slice(h * DK, (h + 1) * DK)
            a = lax.dot_general(q_rel[:, hs], k_rel[:, hs], nt, preferred_element_type=F32)
            a = jnp.where(causal, a, 0.0).astype(BF16)
            oacc[:, hs] = jnp.dot(a, v[:, hs], preferred_element_type=F32)
            for j in range(ncb):
                rs = slice(j * C, (j + 1) * C)
                st = state[h]
                st_ref[j, h] = st
                oacc[rs, hs] += lax.dot_general(q_dec[rs, hs], st.astype(BF16), nt,
                                                preferred_element_type=F32)
                dec = jnp.exp(jnp.sum(logf[rs, hs], axis=0, keepdims=True))
                state[h] = dec * st + lax.dot_general(v[rs, hs], k_dec[rs, hs], tn,
                                                      preferred_element_type=F32)
        o = oacc[...]
        o_ref[...] = o
        gx = p_ref[:, 3 * HK:4 * HK]
        gate = gx * _sigmoid(gx)
        for h in range(H):
            hs = slice(h * DK, (h + 1) * DK)
            oh = o[:, hs]
            y_ref[:, hs] = (oh * _rms_rstd(oh) * on_ref[...] * gate[:, hs]).astype(BF16)

    return pl.pallas_call(
        body, name="hgrn_fwd", grid=(T // tb,),
        in_specs=[pl.BlockSpec((tb, 4 * HK), lambda i: (i, 0)),
                  pl.BlockSpec((1, HK), lambda i: (0, 0)),
                  pl.BlockSpec((1, DK), lambda i: (0, 0))],
        out_specs=(pl.BlockSpec((tb, HK), lambda i: (i, 0)),
                   pl.BlockSpec((tb, HK), lambda i: (i, 0)),
                   pl.BlockSpec((ncb, H, DK, DK), lambda i: (i, 0, 0, 0))),
        out_shape=(jax.ShapeDtypeStruct((T, HK), BF16), jax.ShapeDtypeStruct((T, HK), F32),
                   jax.ShapeDtypeStruct((T // C, H, DK, DK), F32)),
        scratch_shapes=[pltpu.VMEM((H, DK, DK), F32), pltpu.VMEM((tb, HK), F32)],
        compiler_params=_cparams("arbitrary"))(proj, lb, o_norm)


def _hgrn_bwd(proj, lb, o_norm, o, states, dy):
    T = proj.shape[0]
    H, C = HGRN_HEADS, HGRN_CHUNK
    HK = proj.shape[1] // 4
    DK = HK // H
    tb = min(HGRN_BLOCK, T)
    ncb = tb // C
    nb = T // tb
    nt = (((1,), (1,)), ((), ()))
    tn = (((0,), (0,)), ((), ()))

    def body(p_ref, lb_ref, on_ref, o_ref, st_ref, dy_ref, dp_ref, dlb_ref, don_ref,
             dstate, dqr_s, dkr_s, dqd_s, dkd_s, dv_s, do_s, e_s):
        @pl.when(pl.program_id(0) == 0)
        def _():
            dstate[...] = jnp.zeros_like(dstate)
            dlb_ref[...] = jnp.zeros_like(dlb_ref)
            don_ref[...] = jnp.zeros_like(don_ref)

        cum, rel, rest, rev, ones, causal = _chunk_mats(tb)
        lb = lb_ref[...]
        qx, sq, q, sf, f, k, logf = _hgrn_gates(p_ref, lb, HK)
        b, brel, brest = _chunk_sums(cum, rel, rest, logf)
        eb = jnp.exp(b)
        erel = jnp.exp(brel)
        enrel = jnp.exp(-brel)
        erest = jnp.exp(brest)
        q_rel_f, k_rel_f, q_dec_f, k_dec_f = q * erel, k * enrel, q * eb, k * erest
        q_rel, k_rel = q_rel_f.astype(BF16), k_rel_f.astype(BF16)
        q_dec, k_dec = q_dec_f.astype(BF16), k_dec_f.astype(BF16)
        v = p_ref[:, 2 * HK:3 * HK].astype(BF16)

        gx = p_ref[:, 3 * HK:4 * HK]
        sg = _sigmoid(gx)
        gate = gx * sg
        dy = dy_ref[...]
        ov = o_ref[...]
        on = on_ref[...]
        don = jnp.zeros((1, DK), F32)
        for h in range(H):
            hs = slice(h * DK, (h + 1) * DK)
            oh = ov[:, hs]
            r = _rms_rstd(oh)
            xh = oh * r
            d_on = dy[:, hs] * gate[:, hs]
            don = don + jnp.sum(d_on * xh, axis=0, keepdims=True)
            u = d_on * on
            do_s[:, hs] = r * (u - xh * jnp.mean(u * xh, axis=-1, keepdims=True))
            dp_ref[:, 3 * HK + h * DK:3 * HK + (h + 1) * DK] = (
                dy[:, hs] * xh * on * (sg[:, hs] * (1.0 + gx[:, hs] * (1.0 - sg[:, hs])))).astype(BF16)
        don_ref[...] += don

        for h in range(H):
            hs = slice(h * DK, (h + 1) * DK)
            doh = do_s[:, hs].astype(BF16)
            a = lax.dot_general(q_rel[:, hs], k_rel[:, hs], nt, preferred_element_type=F32)
            a = jnp.where(causal, a, 0.0).astype(BF16)
            da = lax.dot_general(doh, v[:, hs], nt, preferred_element_type=F32)
            da = jnp.where(causal, da, 0.0).astype(BF16)
            dv_s[:, hs] = lax.dot_general(a, doh, tn, preferred_element_type=F32)
            dqr_s[:, hs] = jnp.dot(da, k_rel[:, hs], preferred_element_type=F32)
            dkr_s[:, hs] = lax.dot_general(da, q_rel[:, hs], tn, preferred_element_type=F32)
            for j in reversed(range(ncb)):
                rs = slice(j * C, (j + 1) * C)
                dst = dstate[h]
                dstb = dst.astype(BF16)
                st = st_ref[j, h]
                dkd_s[rs, hs] = jnp.dot(v[rs, hs], dstb, preferred_element_type=F32)
                dv_s[rs, hs] += lax.dot_general(k_dec[rs, hs], dstb, nt, preferred_element_type=F32)
                dec = jnp.exp(jnp.sum(logf[rs, hs], axis=0, keepdims=True))
                e_s[rs, hs] = jnp.broadcast_to(jnp.sum(dst * st, axis=0, keepdims=True) * dec, (C, DK))
                dqd_s[rs, hs] = jnp.dot(doh[rs], st.astype(BF16), preferred_element_type=F32)
                dstate[h] = dec * dst + lax.dot_general(doh[rs], q_dec[rs, hs], tn,
                                                        preferred_element_type=F32)

        dqr, dkr, dqd, dkd = dqr_s[...], dkr_s[...], dqd_s[...], dkd_s[...]
        kdk = dkd * k_dec_f
        db = dqr * q_rel_f - dkr * k_rel_f + dqd * q_dec_f - kdk
        dlogf = _split_dot(rev.astype(BF16), db, 2) + _split_dot(ones.astype(BF16), kdk, 2) + e_s[...]
        dk = dkr * enrel + dkd * erest
        df = dlogf / f - dk
        dlb_ref[...] += jnp.sum(df * (1.0 - sf), axis=0, keepdims=True)
        dq = dqr * erel + dqd * eb
        dp_ref[:, 0:HK] = (dq * (sq * (1.0 + qx * (1.0 - sq)))).astype(BF16)
        dp_ref[:, HK:2 * HK] = (df * (1.0 - lb) * sf * (1.0 - sf)).astype(BF16)
        dp_ref[:, 2 * HK:3 * HK] = dv_s[...].astype(BF16)

    rev_row = lambda w: pl.BlockSpec((tb, w), lambda i: (nb - 1 - i, 0))
    vec = lambda w: pl.BlockSpec((1, w), lambda i: (0, 0))
    scr = pltpu.VMEM((tb, HK), F32)
    return pl.pallas_call(
        body, name="hgrn_bwd", grid=(nb,),
        in_specs=[rev_row(4 * HK), vec(HK), vec(DK), rev_row(HK),
                  pl.BlockSpec((ncb, H, DK, DK), lambda i: (nb - 1 - i, 0, 0, 0)), rev_row(HK)],
        out_specs=(rev_row(4 * HK), vec(HK), vec(DK)),
        out_shape=(jax.ShapeDtypeStruct((T, 4 * HK), BF16), jax.ShapeDtypeStruct((1, HK), F32),
                   jax.ShapeDtypeStruct((1, DK), F32)),
        scratch_shapes=[pltpu.VMEM((H, DK, DK), F32), scr, scr, scr, scr, scr, scr, scr],
        compiler_params=_cparams("arbitrary"))(proj, lb, o_norm, o, states, dy)


def _hgrn_layer_fwd(a, o_norm, lb, wbuf, pk, slot, res):
    D = a.shape[1]
    in_by_n, _ = _col_sharded(pk, "hgrn_w_in", slot, D)
    out_by_k, _, _ = _row_sharded(pk, "hgrn_w_o", slot, D // N_CHIPS)
    proj = _mm(a, wbuf, n=4 * D, b_map=in_by_n, tk=D, tn=D, name="hgrn_in")
    y, o, states = _hgrn_fwd(proj, lb, o_norm)
    m, h_new, a_next = _mm(y, wbuf, n=D, b_map=out_by_k, tm=1024, tk=D // N_CHIPS, tn=D, epi="resnorm",
                           extra=res, name="hgrn_out_res")
    return m, (a, proj, y, o, states), h_new, a_next


def _hgrn_layer_bwd(dm, saved, o_norm, lb, wbuf, gbuf, pk, slot):
    a, proj, y, o, states = saved
    D = a.shape[1]
    in_by_n, in_by_k = _col_sharded(pk, "hgrn_w_in", slot, D)
    _, out_by_n, out_by_m = _row_sharded(pk, "hgrn_w_o", slot, D // N_CHIPS)
    dy = _mm(dm, wbuf, tb=True, n=y.shape[1], b_map=out_by_n, tm=2048, tn=D // N_CHIPS, tk=D,
             name="hgrn_out_dx")
    gbuf = _mm(y, dm, ta=True, into=gbuf, o_map=out_by_m, tm=D // N_CHIPS, tn=D, tk=2048, name="hgrn_out_dw")
    dproj, dlb, don = _hgrn_bwd(proj, lb, o_norm, o, states, dy)
    gbuf = _mm(a, dproj, ta=True, into=gbuf, o_map=in_by_n, tm=D, tn=D, name="hgrn_in_dw")
    da = _mm(dproj, wbuf, tb=True, n=D, b_map=in_by_k, tn=D, tk=D, name="hgrn_in_dx")
    return da, gbuf, dict(o_norm=don, lb=dlb)


def _lower_bounds(lb_logits):
    p = jax.nn.softmax(lb_logits.astype(F32), axis=0)
    return jnp.cumsum(p, axis=0) - p[0]


def _rope_tables(positions):
    inv_freq = jnp.power(ROPE_BASE, -jnp.arange(0, MLA_ROPE, 2, dtype=F32) / MLA_ROPE)
    ang = positions.astype(F32)[:, None] * inv_freq
    cos, sin = jnp.cos(ang), jnp.sin(ang)
    zero = jnp.zeros((positions.shape[0], 128 - MLA_ROPE), F32)
    return (jnp.concatenate([cos, cos, zero], axis=-1), jnp.concatenate([-sin, sin, zero], axis=-1))


def _pad_mla_weights(w_in, w_uq):
    w_in_p = jnp.pad(w_in, ((0, 0), (0, 0), (0, 128 - MLA_ROPE)))
    n, ql, _ = w_uq.shape
    w_uq_p = jnp.pad(w_uq.reshape(n, ql, MLA_HEADS, MLA_NOPE + MLA_ROPE),
                     ((0, 0), (0, 0), (0, 0), (0, MLA_QK_PAD - MLA_NOPE - MLA_ROPE)))
    return w_in_p, w_uq_p.reshape(n, ql, MLA_HEADS * MLA_QK_PAD)


def _local_step(x, positions, target, small, fetch, gbufs, emit, emit_mlp):
    T, D = x.shape
    lbounds, lb_vjp = jax.vjp(_lower_bounds, small["hgrn_lb_logits"])
    cc, ss = _rope_tables(positions)
    fetched = {0: fetch(0, None)}
    gains = fetched[0]["gains"]
    tick = [jnp.zeros((), F32)]

    def g(layer, i):
        return gains[layer, i][None, :] + tick[0]

    def mla_weights(layer):
        f = fetched[layer]
        w_in_p, w_uq_p = _pad_mla_weights(f["w_in"][None], f["w_uq"][None])
        slot = layer // 2
        return dict(w_in=w_in_p[0], w_uq=w_uq_p[0], w_ukv=f["w_ukv"],
                    q_norm=small["mla_q_norm"][slot][None, :], kv_norm=small["mla_kv_norm"][slot][None, :])

    saved = []
    h = x
    a = _prenorm_fwd(x, g(0, 0))
    dy = sq = None
    for layer in range(DEPTH):
        slot = layer // 2
        if layer not in fetched:
            fetched[layer] = fetch(layer, a)
        wbuf, pk = fetched[layer]["wbuf"], fetched[layer]["pk"]
        res = (h, g(layer, 1), g(layer, 2))
        if layer % 2 == 0:
            m, mix_saved, h1, a2 = _mla_fwd(a, mla_weights(layer), cc, ss, wbuf, pk, slot, res)
        else:
            m, mix_saved, h1, a2 = _hgrn_layer_fwd(a, small["hgrn_o_norm"][slot][None, :],
                                                   lbounds[layer][None, :], wbuf, pk, slot, res)
        if layer + 1 < DEPTH:
            u, mlp_saved, h2, a = _mlp_fwd(a2, wbuf, pk, layer, (h1, g(layer, 3), g(layer + 1, 0)))
        else:
            u, mlp_saved, h2, _ = _mlp_fwd(a2, wbuf, pk, layer, None)
            dy, sq = _resnorm_loss(h1, u, g(layer, 3), target)
        saved.append((h, m, h1, u, mix_saved, mlp_saved))
        h = h2

    n_mla, n_hgrn = (DEPTH + 1) // 2, DEPTH // 2
    dgains = [[None] * 4 for _ in range(DEPTH)]
    gw = {k: [None] * n_mla for k in ("mla_w_in", "mla_w_uq", "mla_w_ukv", "mla_q_norm", "mla_kv_norm")}
    gw["hgrn_o_norm"] = [None] * n_hgrn
    dlb = [jnp.zeros((1, lbounds.shape[1]), F32) for _ in range(DEPTH)]
    dh = dy
    da_next = None
    for layer in reversed(range(DEPTH)):
        h0, m, h1, u, mix_saved, mlp_saved = saved[layer]
        slot = layer // 2
        wbuf, pk, gbuf = fetched[layer]["wbuf"], fetched[layer]["pk"], gbufs[layer]
        if da_next is None:
            du, dgains[layer][3] = _resnorm_bwd(u, g(layer, 3), dh, name="resnorm_bwd_last")
            t = dh
        else:
            h2 = saved[layer + 1][0]
            t, du, dgains[layer][3], dgains[layer + 1][0] = _resnorm_bwd(
                u, g(layer, 3), dh, h2, da_next, g(layer + 1, 0), name="resnorm_bwd_mlp")
        da2, gbuf = _mlp_bwd(du, mlp_saved, wbuf, gbuf, pk, layer)
        if layer == 0:
            gbuf = emit_mlp(layer, gbuf)
        t, dm, dgains[layer][1], dgains[layer][2] = _resnorm_bwd(
            m, g(layer, 1), t, h1, da2, g(layer, 2), name="resnorm_bwd_mix")
        if layer % 2 == 0:
            da_next, gbuf, mg = _mla_bwd(dm, mix_saved, mla_weights(layer), cc, ss, wbuf, gbuf, pk, slot)
            ql = mg["q_norm"].shape[-1]
            kvl = mg["kv_norm"].shape[-1]
            gw["mla_w_in"][slot] = mg["w_in"][:, :ql + kvl + MLA_ROPE]
            gw["mla_w_uq"][slot] = mg["w_uq"].reshape(ql, MLA_HEADS, MLA_QK_PAD)[
                :, :, :MLA_NOPE + MLA_ROPE].reshape(ql, MLA_HEADS * (MLA_NOPE + MLA_ROPE))
            gw["mla_w_ukv"][slot] = mg["w_ukv"]
            gw["mla_q_norm"][slot] = mg["q_norm"][0]
            gw["mla_kv_norm"][slot] = mg["kv_norm"][0]
        else:
            da_next, gbuf, hg = _hgrn_layer_bwd(dm, mix_saved, small["hgrn_o_norm"][slot][None, :],
                                                lbounds[layer][None, :], wbuf, gbuf, pk, slot)
            gw["hgrn_o_norm"][slot] = hg["o_norm"][0]
            dlb[layer] = hg["lb"]
        dh = t
        if layer > 0:
            mine = ({k: gw[k][slot] for k in ("mla_w_in", "mla_w_uq", "mla_w_ukv")} if layer % 2 == 0 else {})
            tick[0] = emit(layer, gbuf, mine)
        else:
            gbuf0 = gbuf
    grad_x, dgains[0][0] = _prenorm_bwd(x, g(0, 0), dh, da_next)

    last = {k: gw[k][0] for k in ("mla_w_in", "mla_w_uq", "mla_w_ukv")}
    last.update({k: jnp.stack(gw[k]) for k in ("mla_q_norm", "mla_kv_norm", "hgrn_o_norm")})
    last["norm_gains"] = jnp.stack([jnp.concatenate(row, axis=0) for row in dgains])
    (last["hgrn_lb_logits"],) = lb_vjp(jnp.concatenate(dlb, axis=0))
    emit(0, gbuf0, last)
    return sq, grad_x


def _size(shape):
    n = 1
    for d in shape:
        n *= d
    return n


def _piece_rows(shape):
    return -(-_size(shape) // PACK_W)


def _packed_misc_rows(shapes):
    return sum(_piece_rows(s) for s in shapes)


def _cast_into(src, buf, row, name):
    rows, W = src.shape
    tr = min(256, rows)
    assert rows % tr == 0 and row % tr == 0

    def body(s_ref, b_ref, o_ref):
        o_ref[...] = s_ref[...].astype(BF16)

    return pl.pallas_call(
        body, name=name, grid=(rows // tr,),
        in_specs=[pl.BlockSpec((tr, W), lambda i: (i, 0)), pl.BlockSpec(memory_space=pl.ANY)],
        out_specs=pl.BlockSpec((tr, W), lambda i: (row // tr + i, 0)),
        out_shape=jax.ShapeDtypeStruct(buf.shape, buf.dtype), input_output_aliases={1: 0},
        compiler_params=_cparams("parallel"))(src, buf)


def _pack_blocks(pieces, rows, dtype):
    blocks, used = [], 0
    for p in pieces:
        flat = p.astype(dtype).reshape(-1)
        r = _piece_rows(p.shape)
        if r * PACK_W != flat.shape[0]:
            flat = jnp.pad(flat, (0, r * PACK_W - flat.shape[0]))
        blocks.append(flat.reshape(r, PACK_W))
        used += r
    if rows > used:
        blocks.append(jnp.zeros((rows - used, PACK_W), dtype))
    return blocks


def _unpack(buf, shapes):
    out, off = [], 0
    for shp in shapes:
        r = _piece_rows(shp)
        piece = buf[off:off + r]
        if r * PACK_W != _size(shp):
            piece = piece.reshape(-1)[:_size(shp)]
        out.append(piece.reshape(shp))
        off += r
    return out


def _mesh_place():
    x, y, c = lax.axis_index("x"), lax.axis_index("y"), lax.axis_index("c")
    chips = [(1 - x, y), (x, 1 - y), (1 - x, 1 - y)]
    return x, y, c, chips


_HBM = pl.BlockSpec(memory_space=pltpu.HBM)


def _share_reduced(q, name="grads_share_reduced"):
    rh, W = q.shape

    def body(q_ref, out_ref, send_sem, recv_sem):
        x, y, c, _ = _mesh_place()
        cp = pltpu.make_async_remote_copy(src_ref=q_ref, dst_ref=out_ref.at[c], send_sem=send_sem,
                                          recv_sem=recv_sem, device_id=(x, y, 1 - c), device_id_type=MESH)
        cp.start()
        cp.wait()

    out = pl.pallas_call(
        body, name=name, in_specs=[_HBM], out_specs=_HBM,
        out_shape=jax.ShapeDtypeStruct((2, rh, W), q.dtype),
        scratch_shapes=[pltpu.SemaphoreType.DMA, pltpu.SemaphoreType.DMA],
    )(q)
    return out


def _sum_chips(parts, own, own_row0, which, name, out_dtype=F32):
    n, rh, W = parts.shape
    tr = PACK_TILE
    assert own_row0 % tr == 0
    if own.ndim == 3:
        own_spec = pl.BlockSpec((None, tr, W), lambda i, w_ref: (w_ref[0], own_row0 // tr + i, 0))
    else:
        own_spec = pl.BlockSpec((tr, W), lambda i, w_ref: (own_row0 // tr + i, 0))

    def body(w_ref, p_ref, own_ref, o_ref):
        mine = own_ref[...].astype(F32)
        acc = None
        for j in range(n):
            term = jnp.where(w_ref[0] == j, mine, p_ref[j].astype(F32))
            acc = term if acc is None else acc + term
        o_ref[...] = acc.astype(out_dtype)

    return pl.pallas_call(
        body, name=name,
        grid_spec=pltpu.PrefetchScalarGridSpec(
            num_scalar_prefetch=1, grid=(rh // tr,),
            in_specs=[pl.BlockSpec((n, tr, W), lambda i, w_ref: (0, i, 0)), own_spec],
            out_specs=pl.BlockSpec((tr, W), lambda i, w_ref: (i, 0))),
        out_shape=jax.ShapeDtypeStruct((rh, W), out_dtype),
        compiler_params=_cparams("parallel"))(jnp.reshape(which, (1,)).astype(jnp.int32), parts, own)


_SEM = pl.BlockSpec(memory_space=pltpu.SEMAPHORE)
_ASYNC = pltpu.CompilerParams(has_side_effects=pltpu.SideEffectType.DATAFLOW_SIDE_EFFECTING)


def _hbm(a):
    return pltpu.with_memory_space_constraint(a, pltpu.HBM)


def _gather_copies(w_ref, land_ref, send_sems, recv_sems):
    x, y, c, chips = _mesh_place()
    me = 2 * x + y
    rh = w_ref.shape[0] // 2
    rows = pl.ds(pl.multiple_of(c * rh, 16), rh)
    return [pltpu.make_async_remote_copy(
        src_ref=w_ref.at[rows], dst_ref=land_ref.at[me, rows], send_sem=send_sems.at[r],
        recv_sem=recv_sems.at[r], device_id=(px, py, c), device_id_type=MESH)
        for r, (px, py) in enumerate(chips)]


def _scatter_copies(g_ref, land_ref, send_sems, recv_sems, row0):
    x, y, c, chips = _mesh_place()
    me = 2 * x + y
    rows = pl.ds(row0, land_ref.shape[1])
    return [pltpu.make_async_remote_copy(
        src_ref=g_ref.at[2 * px + py, rows], dst_ref=land_ref.at[me], send_sem=send_sems.at[r],
        recv_sem=recv_sems.at[r], device_id=(px, py, c), device_id_type=MESH)
        for r, (px, py) in enumerate(chips)]


def _halves_to_sibling(land, name):
    n, R, W = land.shape
    rh = R // 2

    def body(l_ref, o_ref, send_sems, recv_sems):
        x, y, c, chips = _mesh_place()
        rows = pl.ds(pl.multiple_of(c * rh, 16), rh)
        copies = [pltpu.make_async_remote_copy(
            src_ref=o_ref.at[2 * px + py, rows], dst_ref=o_ref.at[2 * px + py, rows], send_sem=send_sems.at[r],
            recv_sem=recv_sems.at[r], device_id=(x, y, 1 - c), device_id_type=MESH)
            for r, (px, py) in enumerate(chips)]
        for cp in copies:
            cp.start()
        for cp in copies:
            cp.wait()

    return pl.pallas_call(
        body, name=name, in_specs=[_HBM], out_specs=_HBM, out_shape=jax.ShapeDtypeStruct(land.shape, land.dtype),
        scratch_shapes=[pltpu.SemaphoreType.DMA((3,)), pltpu.SemaphoreType.DMA((3,))],
        input_output_aliases={0: 0})(land)


def _gather_start(wp, name):
    R, W = wp.shape

    def body(w_ref, land_ref, send_sems, recv_sems, w_thru, land_thru, token):
        for cp in _gather_copies(w_ref, land_ref, send_sems, recv_sems):
            cp.start()
        token[...] = jnp.zeros_like(token)

    return pl.pallas_call(
        body, name=name,
        out_shape=(pltpu.SemaphoreType.DMA((3,)), pltpu.SemaphoreType.DMA((3,)), pltpu.HBM(wp.shape, wp.dtype),
                   pltpu.HBM((N_CHIPS, R, W), wp.dtype), jax.ShapeDtypeStruct((8, 128), F32)),
        in_specs=(_HBM, _HBM),
        out_specs=(_SEM, _SEM, _HBM, _HBM, pl.BlockSpec(memory_space=pltpu.VMEM)),
        input_output_aliases={0: 2, 1: 3}, compiler_params=_ASYNC,
    )(_hbm(wp), _hbm(lax.empty((N_CHIPS, R, W), wp.dtype)))


def _gather_wait(send_sems, recv_sems, w_thru, land_thru, after, name):
    R, W = w_thru.shape
    rh = R // 2

    def body(w_ref, land_ref, send_sems, recv_sems, after_ref, w_dead, got_ref):
        x, y, c, _ = _mesh_place()
        half = land_ref.at[0, pl.ds(0, rh)]
        for k in range(3):
            cp = pltpu.make_async_remote_copy(src_ref=half, dst_ref=half, send_sem=send_sems.at[k],
                                              recv_sem=recv_sems.at[k], device_id=(x, y, 1 - c),
                                              device_id_type=MESH)
            cp.wait_send()
            cp.wait_recv()

    return pl.pallas_call(
        body, name=name,
        out_shape=(pltpu.HBM(w_thru.shape, w_thru.dtype), pltpu.HBM(land_thru.shape, land_thru.dtype)),
        in_specs=(_HBM, _HBM, _SEM, _SEM, pl.BlockSpec(memory_space=pl.ANY)), out_specs=(_HBM, _HBM),
        input_output_aliases={0: 0, 1: 1}, compiler_params=_ASYNC,
    )(w_thru, land_thru, send_sems, recv_sems, after)


def _scatter_start(g, row0, nrows, name):
    n, R, W = g.shape
    land_shape = (n, nrows, W)

    def body(g_ref, land_ref, send_sems, recv_sems, g_thru, land_thru, token):
        for cp in _scatter_copies(g_ref, land_ref, send_sems, recv_sems, row0):
            cp.start()
        token[...] = jnp.zeros_like(token)

    return pl.pallas_call(
        body, name=name,
        out_shape=(pltpu.SemaphoreType.DMA((3,)), pltpu.SemaphoreType.DMA((3,)), pltpu.HBM(g.shape, g.dtype),
                   pltpu.HBM(land_shape, g.dtype), jax.ShapeDtypeStruct((8, 128), F32)),
        in_specs=(_HBM, _HBM),
        out_specs=(_SEM, _SEM, _HBM, _HBM, pl.BlockSpec(memory_space=pltpu.VMEM)),
        input_output_aliases={0: 2, 1: 3}, compiler_params=_ASYNC,
    )(_hbm(g), _hbm(lax.empty(land_shape, g.dtype)))


def _scatter_wait(send_sems, recv_sems, g_thru, land_thru, after, name):
    def body(g_ref, land_ref, send_sems, recv_sems, after_ref, g_out, got_ref):
        x, y, c, _ = _mesh_place()
        for k in range(3):
            cp = pltpu.make_async_remote_copy(src_ref=land_ref.at[0], dst_ref=land_ref.at[0], send_sem=send_sems.at[k],
                                              recv_sem=recv_sems.at[k], device_id=(x, y, 1 - c),
                                              device_id_type=MESH)
            cp.wait_send()
            cp.wait_recv()

    return pl.pallas_call(
        body, name=name,
        out_shape=(pltpu.HBM(g_thru.shape, g_thru.dtype), pltpu.HBM(land_thru.shape, land_thru.dtype)),
        in_specs=(_HBM, _HBM, _SEM, _SEM, pl.BlockSpec(memory_space=pl.ANY)), out_specs=(_HBM, _HBM),
        input_output_aliases={0: 0, 1: 1}, compiler_params=_ASYNC,
    )(g_thru, land_thru, send_sems, recv_sems, after)


def _adamw(w, g, m, v, name):
    shape = w.shape
    cols = shape[-1]
    w2, g2, m2, v2 = (t.reshape(-1, cols) for t in (w, g, m, v))
    rows = w2.shape[0]
    tr = rows
    for cand in (512, 256, 128, 64, 32, 16, 8):
        if rows > cand and rows % cand == 0:
            tr = cand
            break
    c1 = 1.0 / (1.0 - ADAM_B1 ** ADAM_STEP)
    c2 = 1.0 / (1.0 - ADAM_B2 ** ADAM_STEP)

    def body(w_ref, g_ref, m_ref, v_ref, d_ref, nm_ref, nv_ref):
        gv = g_ref[...]
        nm = ADAM_B1 * m_ref[...] + (1.0 - ADAM_B1) * gv
        nv = ADAM_B2 * v_ref[...] + (1.0 - ADAM_B2) * (gv * gv)
        nm_ref[...] = nm
        nv_ref[...] = nv
        d_ref[...] = -ADAM_LR * ((nm * c1) / (jnp.sqrt(nv * c2) + ADAM_EPS) + ADAM_WD * w_ref[...])

    blk = pl.BlockSpec((tr, cols), lambda i: (i, 0))
    sds = jax.ShapeDtypeStruct((rows, cols), F32)
    d, nm, nv = pl.pallas_call(body, name=name, grid=(rows // tr,), in_specs=[blk] * 4,
                               out_specs=(blk, blk, blk), out_shape=(sds, sds, sds),
                               compiler_params=_cparams("parallel"))(w2, g2, m2, v2)
    return d.reshape(shape), nm.reshape(shape), nv.reshape(shape)


def kernel(x, positions, norm_gains, mla_w_in, mla_q_norm, mla_kv_norm, mla_w_uq, mla_w_ukv, mla_w_o, hgrn_w_in, hgrn_lb_logits, hgrn_o_norm, hgrn_w_o, mlp_w1, mlp_w2, loss_target, m_norm_gains, m_mla_w_in, m_mla_q_norm, m_mla_kv_norm, m_mla_w_uq, m_mla_w_ukv, m_mla_w_o, m_hgrn_w_in, m_hgrn_lb_logits, m_hgrn_o_norm, m_hgrn_w_o, m_mlp_w1, m_mlp_w2, v_norm_gains, v_mla_w_in, v_mla_q_norm, v_mla_kv_norm, v_mla_w_uq, v_mla_w_ukv, v_mla_w_o, v_hgrn_w_in, v_hgrn_lb_logits, v_hgrn_o_norm, v_hgrn_w_o, v_mlp_w1, v_mlp_w2):
    w = dict(norm_gains=norm_gains, mla_w_in=mla_w_in, mla_q_norm=mla_q_norm, mla_kv_norm=mla_kv_norm,
             mla_w_uq=mla_w_uq, mla_w_ukv=mla_w_ukv, mla_w_o=mla_w_o, hgrn_w_in=hgrn_w_in,
             hgrn_lb_logits=hgrn_lb_logits, hgrn_o_norm=hgrn_o_norm, hgrn_w_o=hgrn_w_o,
             mlp_w1=mlp_w1, mlp_w2=mlp_w2)
    mom_m = dict(norm_gains=m_norm_gains, mla_w_in=m_mla_w_in, mla_q_norm=m_mla_q_norm,
                 mla_kv_norm=m_mla_kv_norm, mla_w_uq=m_mla_w_uq, mla_w_ukv=m_mla_w_ukv,
                 mla_w_o=m_mla_w_o, hgrn_w_in=m_hgrn_w_in, hgrn_lb_logits=m_hgrn_lb_logits,
                 hgrn_o_norm=m_hgrn_o_norm, hgrn_w_o=m_hgrn_w_o, mlp_w1=m_mlp_w1, mlp_w2=m_mlp_w2)
    mom_v = dict(norm_gains=v_norm_gains, mla_w_in=v_mla_w_in, mla_q_norm=v_mla_q_norm,
                 mla_kv_norm=v_mla_kv_norm, mla_w_uq=v_mla_w_uq, mla_w_ukv=v_mla_w_ukv,
                 mla_w_o=v_mla_w_o, hgrn_w_in=v_hgrn_w_in, hgrn_lb_logits=v_hgrn_lb_logits,
                 hgrn_o_norm=v_hgrn_o_norm, hgrn_w_o=v_hgrn_w_o, mlp_w1=v_mlp_w1, mlp_w2=v_mlp_w2)
    c = lax.axis_index("c")

    axis_of = dict(SHARDED)
    me = 2 * lax.axis_index("x") + lax.axis_index("y")
    gain_bits = lax.bitcast_convert_type(norm_gains, jnp.uint32)
    gain_hi = lax.bitcast_convert_type((gain_bits >> 16).astype(jnp.uint16), BF16)
    gain_lo = lax.bitcast_convert_type((gain_bits & 0xFFFF).astype(jnp.uint16), BF16)

    layers = []
    for l in range(DEPTH):
        s = l // 2
        if l % 2 == 0:
            big = [("mlp_w1", l), ("mlp_w2", l), ("mla_w_o", s)]
            tail = [("mla_w_in", s), ("mla_w_uq", s), ("mla_w_ukv", s)]
        else:
            big = [("hgrn_w_in", s), ("mlp_w1", l), ("mlp_w2", l), ("hgrn_w_o", s)]
            tail = []
        w_tail = [w[n][i] for n, i in tail] + ([gain_hi, gain_lo] if l == 0 else [])
        g_tail = tail + ([("norm_gains", None)] + [(n, None) for n in REPLICATED] if l == 0 else [])
        g_shapes = [w[n].shape if i is None else w[n][i].shape for n, i in g_tail]
        tail_rows = max(_packed_misc_rows([t.shape for t in w_tail]), _packed_misc_rows(g_shapes))
        pk = _Packed([(n, w[n].shape[1]) for n, _ in big], tail_rows)
        wpack = jnp.zeros((pk.rows, PACK_W), BF16)
        for n, i in big:
            assert w[n].shape[2] == PACK_W
            wpack = _cast_into(w[n][i], wpack, pk.off[n], name="pack_%s_%d" % (n, l))
        if w_tail:
            wpack = lax.dynamic_update_slice(
                wpack, jnp.concatenate(_pack_blocks(w_tail, 0, BF16), axis=0), (pk.misc, 0))
        layers.append(dict(pk=pk, big=big, tail=tail, w_tail=w_tail, g_tail=g_tail, g_shapes=g_shapes,
                           gather=_gather_start(wpack, name="gather_start_%d" % l)))

    def fetch(l, after):
        lay = layers[l]
        pk = lay["pk"]
        send_sems, recv_sems, w_thru, land_thru, _ = lay["gather"]
        if after is None:
            after = sum(layers[k]["gather"][4] for k in range(1, DEPTH))
        w_back, land = _gather_wait(send_sems, recv_sems, w_thru, land_thru, after, name="gather_wait_%d" % l)
        land = _halves_to_sibling(land, name="gather_halves_%d" % l)
        land = lax.dynamic_update_slice(land, w_back[None], (me, 0, 0))
        out = dict(wbuf=land.reshape(N_CHIPS * pk.rows, PACK_W), pk=pk)
        if lay["w_tail"]:
            rows = _packed_misc_rows([t.shape for t in lay["w_tail"]])
            per_chip = [_unpack(land[j, pk.misc:pk.misc + rows], [t.shape for t in lay["w_tail"]])
                        for j in range(N_CHIPS)]
            for i, (n, _) in enumerate(lay["tail"]):
                out[n[4:]] = jnp.concatenate([per_chip[j][i] for j in range(N_CHIPS)], axis=axis_of[n] - 1)
            if l == 0:
                got_hi, got_lo = (lax.bitcast_convert_type(
                    jnp.concatenate([per_chip[j][i] for j in range(N_CHIPS)], axis=2),
                    jnp.uint16).astype(jnp.uint32) for i in (-2, -1))
                out["gains"] = lax.bitcast_convert_type((got_hi << 16) | got_lo, F32)
        return out

    def emit(l, gbuf, grads):
        lay = layers[l]
        pk = lay["pk"]
        if lay["g_tail"]:
            for j in range(N_CHIPS):
                pieces = []
                for n, i in lay["g_tail"]:
                    if n not in axis_of:
                        pieces.append(grads[n])
                    else:
                        pieces.append(jnp.split(grads[n], N_CHIPS, axis=axis_of[n] - (0 if i is None else 1))[j])
                block = jnp.concatenate(_pack_blocks(pieces, 0, BF16), axis=0)
                gbuf = lax.dynamic_update_slice(gbuf, block, (j * pk.rows + pk.misc, 0))
        row0 = lay.get("early_rows", 0)
        lay["scatter"] = _scatter_start(gbuf.reshape(N_CHIPS, pk.rows, PACK_W), row0, pk.rows - row0,
                                        name="scatter_start_%d" % l)
        return lay["scatter"][4][0, 0]

    def emit_mlp(l, gbuf):
        lay = layers[l]
        pk = lay["pk"]
        assert pk.off["mlp_w1"] == 0 and pk.off["mlp_w2"] == w["mlp_w1"].shape[1]
        lay["early_rows"] = w["mlp_w1"].shape[1] + w["mlp_w2"].shape[1]
        lay["scatter_early"] = _scatter_start(gbuf.reshape(N_CHIPS, pk.rows, PACK_W), 0, lay["early_rows"],
                                              name="scatter_start_%d_mlp" % l)
        return lay["scatter_early"][2].reshape(N_CHIPS * pk.rows, PACK_W)

    small = dict(mla_q_norm=mla_q_norm, mla_kv_norm=mla_kv_norm, hgrn_lb_logits=hgrn_lb_logits,
                 hgrn_o_norm=hgrn_o_norm)
    gbufs = [lax.empty((N_CHIPS * lay["pk"].rows, PACK_W), BF16) for lay in layers]
    sq, grad_x = _local_step(x[0], positions[0], loss_target[0], small, fetch, gbufs, emit, emit_mlp)
    d_model = x.shape[-1]
    loss = lax.psum(0.5 * jnp.sum(sq) / d_model, ("x", "y", "c"))

    per_name = {}
    behind = grad_x
    for l, lay in reversed(list(enumerate(layers))):
        pk = lay["pk"]
        send_sems, recv_sems, g_thru, land_thru, _ = lay["scatter"]
        row0 = lay.get("early_rows", 0)
        early = None
        if row0:
            e_send, e_recv, _, e_land, _ = lay["scatter_early"]
            g_thru, land = _scatter_wait(e_send, e_recv, g_thru, e_land, behind, name="scatter_wait_%d_mlp" % l)
            early = behind = _sum_chips(land, g_thru, 0, me, name="grads_sum_chips_%d_mlp" % l, out_dtype=BF16)
        g_back, land = _scatter_wait(send_sems, recv_sems, g_thru, land_thru, behind, name="scatter_wait_%d" % l)
        mine = _sum_chips(land, g_back, row0, me, name="grads_sum_chips_%d" % l, out_dtype=BF16)
        if early is not None:
            mine = jnp.concatenate([early, mine], axis=0)
        red = behind = _sum_chips(_share_reduced(mine, name="grads_share_%d" % l), mine, 0, c,
                                  name="grads_sum_cores_%d" % l)
        for n, i in lay["big"]:
            per_name.setdefault(n, {})[i] = red[pk.off[n]:pk.off[n] + w[n].shape[1]]
        for (n, i), piece in zip(lay["g_tail"], _unpack(red[pk.misc:pk.misc + pk.misc_rows], lay["g_shapes"])):
            per_name.setdefault(n, {})[i] = piece
    g_out = {n: (parts[None] if None in parts else jnp.stack([parts[i] for i in sorted(parts)]))
             for n, parts in per_name.items()}

    deltas, new_m, new_v = {}, {}, {}
    for name in WEIGHTS:
        deltas[name], new_m[name], new_v[name] = _adamw(w[name], g_out[name], mom_m[name], mom_v[name],
                                                        name="adamw_" + name)
    return (loss, grad_x[None], *[g_out[n] for n in WEIGHTS], *[deltas[n] for n in WEIGHTS],
            *[new_m[n] for n in WEIGHTS], *[new_v[n] for n in WEIGHTS])
```

```python
import jax
import jax.numpy as jnp
from jax import lax
from jax.experimental import pallas as pl
from jax.experimental.pallas import tpu as pltpu

F32 = jnp.float32
BF16 = jnp.bfloat16
MESH = pl.DeviceIdType.MESH

DEPTH = 4
MLA_HEADS = 8
MLA_NOPE = 128
MLA_ROPE = 64
MLA_V = 128
MLA_QK_PAD = 256
MLA_HEADS_PER_STEP = 2
MLA_SCALE = float(MLA_NOPE + MLA_ROPE) ** -0.5
ROPE_BASE = 10000.0
HGRN_HEADS = 8
HGRN_CHUNK = 32
HGRN_BLOCK = 128
EPS = 1e-6

ADAM_LR = 0.001
ADAM_B1 = 0.9
ADAM_B2 = 0.999
ADAM_EPS = 1e-08
ADAM_WD = 0.01
ADAM_STEP = 10

N_CHIPS = 4
PACK_W = 1024
PACK_ALIGN = 1024
PACK_TILE = 512
V7X_VMEM_LIMIT = 56 * 1024 * 1024

SHARDED = (("norm_gains", 2), ("mla_w_in", 1), ("mla_w_uq", 2), ("mla_w_ukv", 2), ("mla_w_o", 1),
           ("hgrn_w_in", 2), ("hgrn_w_o", 1), ("mlp_w1", 2), ("mlp_w2", 1))
REPLICATED = ("mla_q_norm", "mla_kv_norm", "hgrn_lb_logits", "hgrn_o_norm")
WEIGHTS = ("norm_gains", "mla_w_in", "mla_q_norm", "mla_kv_norm", "mla_w_uq", "mla_w_ukv", "mla_w_o",
           "hgrn_w_in", "hgrn_lb_logits", "hgrn_o_norm", "hgrn_w_o", "mlp_w1", "mlp_w2")


def _cparams(*semantics):
    return pltpu.CompilerParams(dimension_semantics=semantics, vmem_limit_bytes=V7X_VMEM_LIMIT)


def _sigmoid(x):
    return 0.5 * jnp.tanh(0.5 * x) + 0.5


def _mm(a, b, *, ta=False, tb=False, out_dtype=F32, tm=2048, tn=1024, tk=1024, epi=None, extra=None,
        name="mm", n=None, b_map=None, into=None, o_map=None):
    if ta:
        K, M = a.shape
    else:
        M, K = a.shape
    if b_map is not None:
        N = n
    elif tb:
        N, Kb = b.shape
    else:
        Kb, N = b.shape
    assert b_map is not None or K == Kb, (a.shape, b.shape, ta, tb)
    tm, tn = min(tm, M), min(tn, N)
    if ta and b_map is None:
        tk = max(tk, 2048)
    tk = K if (K <= 1024 and b_map is None) else min(tk, K)
    assert M % tm == 0 and N % tn == 0 and K % tk == 0, (M, N, K, tm, tn, tk)
    nk = K // tk
    a_spec = (pl.BlockSpec((tk, tm), lambda i, j, k: (k, i)) if ta
              else pl.BlockSpec((tm, tk), lambda i, j, k: (i, k)))
    if b_map is None:
        b_map = (lambda i, j, k: (j, k)) if tb else (lambda i, j, k: (k, j))
    b_spec = pl.BlockSpec((tn, tk) if tb else (tk, tn), b_map)
    o_spec = pl.BlockSpec((tm, tn), lambda i, j, k: (i, j))
    dims = (((0 if ta else 1,), (1 if tb else 0,)), ((), ()))
    in_specs = [a_spec, b_spec]
    operands = [a, b]
    aliases = {}
    if epi == "mul2r":
        in_specs.append(o_spec)
        operands.append(extra)
    if epi == "resnorm":
        assert tn == N
        vec = pl.BlockSpec((1, N), lambda i, j, k: (0, 0))
        in_specs += [o_spec, vec, vec]
        operands += list(extra)
    if into is not None:
        assert epi is None
        in_specs.append(pl.BlockSpec(memory_space=pl.ANY))
        operands.append(into)
        aliases = {2: 0}
        out_dtype = into.dtype
        out_shape = jax.ShapeDtypeStruct(into.shape, into.dtype)
        out_specs = pl.BlockSpec((tm, tn), o_map)
    elif epi == "relu2":
        out_shape = (jax.ShapeDtypeStruct((M, N), BF16), jax.ShapeDtypeStruct((M, N), BF16))
        out_specs = (o_spec, o_spec)
    elif epi == "mul2r":
        out_shape = jax.ShapeDtypeStruct((M, N), BF16)
        out_specs = o_spec
    elif epi == "resnorm":
        out_shape = (jax.ShapeDtypeStruct((M, N), F32), jax.ShapeDtypeStruct((M, N), F32),
                     jax.ShapeDtypeStruct((M, N), BF16))
        out_specs = (o_spec, o_spec, o_spec)
    else:
        out_shape = jax.ShapeDtypeStruct((M, N), out_dtype)
        out_specs = o_spec
    n_in = len(operands)
    n_out = {"relu2": 2, "resnorm": 3}.get(epi, 1)

    def body(*refs):
        a_ref, b_ref = refs[0], refs[1]
        outs = refs[n_in:n_in + n_out]
        k = pl.program_id(2)

        def finish(acc):
            if epi == "relu2":
                r = jnp.maximum(acc, 0.0)
                outs[0][...] = (r * r).astype(BF16)
                outs[1][...] = r.astype(BF16)
            elif epi == "mul2r":
                outs[0][...] = (acc * (2.0 * refs[2][...].astype(F32))).astype(BF16)
            elif epi == "resnorm":
                h_ref, gp_ref, gn_ref = refs[2], refs[3], refs[4]
                hn = h_ref[...] + acc * _rms_rstd(acc) * gp_ref[...]
                outs[0][...] = acc
                outs[1][...] = hn
                outs[2][...] = (hn * _rms_rstd(hn) * gn_ref[...]).astype(BF16)
            else:
                outs[0][...] = acc.astype(out_dtype)

        part = lax.dot_general(a_ref[...], b_ref[...], dims, preferred_element_type=F32)
        if nk == 1:
            finish(part)
            return
        acc_ref = refs[-1]

        @pl.when(k == 0)
        def _():
            acc_ref[...] = part

        @pl.when((k > 0) & (k < nk - 1))
        def _():
            acc_ref[...] += part

        @pl.when(k == nk - 1)
        def _():
            finish(acc_ref[...] + part)

    return pl.pallas_call(
        body, name=name, grid=(M // tm, N // tn, nk), in_specs=in_specs, out_specs=out_specs,
        out_shape=out_shape, scratch_shapes=[pltpu.VMEM((tm, tn), F32)] if nk > 1 else [],
        input_output_aliases=aliases,
        compiler_params=_cparams("parallel", "parallel", "arbitrary"))(*operands)


def _rms_rstd(x):
    return lax.rsqrt(jnp.mean(x * x, axis=-1, keepdims=True) + EPS)


def _rms_bwd_tile(x, g, dy):
    r = _rms_rstd(x)
    xh = x * r
    u = dy * g
    dx = r * (u - xh * jnp.mean(u * xh, axis=-1, keepdims=True))
    dg = jnp.sum(dy * xh, axis=0, keepdims=True)
    return dx, dg


def _row_tile(T):
    return min(512, T)


def _mid_tile(T):
    return min(256, T)


def _prenorm_fwd(x, g, name="prenorm_fwd"):
    T, D = x.shape
    tm = _row_tile(T)

    def body(x_ref, g_ref, a_ref):
        xv = x_ref[...]
        a_ref[...] = (xv * _rms_rstd(xv) * g_ref[...]).astype(BF16)

    row = pl.BlockSpec((tm, D), lambda i: (i, 0))
    vec = pl.BlockSpec((1, D), lambda i: (0, 0))
    return pl.pallas_call(body, name=name, grid=(T // tm,), in_specs=[row, vec], out_specs=row,
                          out_shape=jax.ShapeDtypeStruct((T, D), BF16),
                          compiler_params=_cparams("parallel"))(x, g)


def _resnorm_loss(h, z, g_post, target, name="resnorm_loss"):
    T, D = h.shape
    tm = _row_tile(T)

    def body(h_ref, z_ref, gp_ref, t_ref, dy_ref, sq_ref):
        zv = z_ref[...]
        err = h_ref[...] + zv * _rms_rstd(zv) * gp_ref[...] - t_ref[...]
        dy_ref[...] = err * (1.0 / D)

        @pl.when(pl.program_id(0) == 0)
        def _():
            sq_ref[...] = jnp.zeros_like(sq_ref)

        sq_ref[...] += jnp.sum(err * err, axis=0, keepdims=True)

    row = pl.BlockSpec((tm, D), lambda i: (i, 0))
    vec = pl.BlockSpec((1, D), lambda i: (0, 0))
    return pl.pallas_call(body, name=name, grid=(T // tm,), in_specs=[row, row, vec, row],
                          out_specs=(row, vec),
                          out_shape=(jax.ShapeDtypeStruct((T, D), F32), jax.ShapeDtypeStruct((1, D), F32)),
                          compiler_params=_cparams("arbitrary"))(h, z, g_post, target)


def _resnorm_bwd(z, g_post, dh, h_new=None, da=None, g_pre=None, name="resnorm_bwd"):
    T, D = z.shape
    tm = _row_tile(T)
    has_next = h_new is not None
    row = pl.BlockSpec((tm, D), lambda i: (i, 0))
    vec = pl.BlockSpec((1, D), lambda i: (0, 0))

    if has_next:
        def body(z_ref, gp_ref, dh_ref, hn_ref, da_ref, gn_ref, t_ref, dz_ref, dgp_ref, dgn_ref):
            first = pl.program_id(0) == 0

            @pl.when(first)
            def _():
                dgp_ref[...] = jnp.zeros_like(dgp_ref)
                dgn_ref[...] = jnp.zeros_like(dgn_ref)

            dpre, dgn = _rms_bwd_tile(hn_ref[...], gn_ref[...], da_ref[...])
            t = dh_ref[...] + dpre
            t_ref[...] = t
            dz, dgp = _rms_bwd_tile(z_ref[...], gp_ref[...], t)
            dz_ref[...] = dz.astype(BF16)
            dgp_ref[...] += dgp
            dgn_ref[...] += dgn

        return pl.pallas_call(
            body, name=name, grid=(T // tm,), in_specs=[row, vec, row, row, row, vec],
            out_specs=(row, row, vec, vec),
            out_shape=(jax.ShapeDtypeStruct((T, D), F32), jax.ShapeDtypeStruct((T, D), BF16),
                       jax.ShapeDtypeStruct((1, D), F32), jax.ShapeDtypeStruct((1, D), F32)),
            compiler_params=_cparams("arbitrary"))(z, g_post, dh, h_new, da, g_pre)

    def body_last(z_ref, gp_ref, dh_ref, dz_ref, dgp_ref):
        @pl.when(pl.program_id(0) == 0)
        def _():
            dgp_ref[...] = jnp.zeros_like(dgp_ref)

        dz, dgp = _rms_bwd_tile(z_ref[...], gp_ref[...], dh_ref[...])
        dz_ref[...] = dz.astype(BF16)
        dgp_ref[...] += dgp

    return pl.pallas_call(
        body_last, name=name, grid=(T // tm,), in_specs=[row, vec, row], out_specs=(row, vec),
        out_shape=(jax.ShapeDtypeStruct((T, D), BF16), jax.ShapeDtypeStruct((1, D), F32)),
        compiler_params=_cparams("arbitrary"))(z, g_post, dh)


def _prenorm_bwd(x, g, dh, da, name="prenorm_bwd"):
    T, D = x.shape
    tm = _row_tile(T)

    def body(x_ref, g_ref, dh_ref, da_ref, dx_ref, dg_ref):
        @pl.when(pl.program_id(0) == 0)
        def _():
            dg_ref[...] = jnp.zeros_like(dg_ref)

        dpre, dg = _rms_bwd_tile(x_ref[...], g_ref[...], da_ref[...])
        dx_ref[...] = dh_ref[...] + dpre
        dg_ref[...] += dg

    row = pl.BlockSpec((tm, D), lambda i: (i, 0))
    vec = pl.BlockSpec((1, D), lambda i: (0, 0))
    return pl.pallas_call(
        body, name=name, grid=(T // tm,), in_specs=[row, vec, row, row], out_specs=(row, vec),
        out_shape=(jax.ShapeDtypeStruct((T, D), F32), jax.ShapeDtypeStruct((1, D), F32)),
        compiler_params=_cparams("arbitrary"))(x, g, dh, da)


class _Packed:
    def __init__(self, big, misc_rows):
        self.big = tuple(big)
        self.off = {}
        r = 0
        for name, rows in big:
            self.off[name] = r
            r += rows
        self.misc, self.misc_rows = r, misc_rows
        self.rows = -(-(r + misc_rows) // PACK_ALIGN) * PACK_ALIGN

    def block(self, name, layer, unit):
        r = self.off[name]
        assert r % unit == 0 and self.rows % unit == 0
        return r // unit, self.rows // unit


def _col_sharded(pk, name, layer, unit):
    base, stride = pk.block(name, layer, unit)
    return (lambda i, j, k: (j * stride + base, 0)), (lambda i, j, k: (k * stride + base, 0))


def _row_sharded(pk, name, layer, unit):
    base, stride = pk.block(name, layer, unit)
    return ((lambda i, j, k: (k * stride + base, 0)), (lambda i, j, k: (j * stride + base, 0)),
            (lambda i, j, k: (i * stride + base, 0)))


def _mlp_fwd(a, wbuf, pk, layer, res):
    D = a.shape[1]
    by_n, _ = _col_sharded(pk, "mlp_w1", layer, D)
    by_k, _, _ = _row_sharded(pk, "mlp_w2", layer, D)
    act, r = _mm(a, wbuf, n=4 * D, b_map=by_n, tk=D, tn=D, epi="relu2", name="mlp_up")
    if res is None:
        return _mm(act, wbuf, n=D, b_map=by_k, tk=D, tn=D, name="mlp_down"), (a, act, r), None, None
    u, h_new, a_next = _mm(act, wbuf, n=D, b_map=by_k, tm=1024, tk=D, tn=D, epi="resnorm", extra=res,
                           name="mlp_down_res")
    return u, (a, act, r), h_new, a_next


def _mlp_bwd(du, saved, wbuf, gbuf, pk, layer):
    a, act, r = saved
    D = a.shape[1]
    w1_by_n, w1_by_k = _col_sharded(pk, "mlp_w1", layer, D)
    _, w2_by_n, w2_by_m = _row_sharded(pk, "mlp_w2", layer, D)
    dz1 = _mm(du, wbuf, tb=True, n=4 * D, b_map=w2_by_n, tn=D, tk=D, epi="mul2r", extra=r, name="mlp_down_dx")
    gbuf = _mm(act, du, ta=True, into=gbuf, o_map=w2_by_m, tm=D, tn=D, name="mlp_down_dw")
    gbuf = _mm(a, dz1, ta=True, into=gbuf, o_map=w1_by_n, tm=D, tn=D, name="mlp_up_dw")
    da = _mm(dz1, wbuf, tb=True, n=D, b_map=w1_by_k, tn=D, tk=D, name="mlp_up_dx")
    return da, gbuf


def _rope_swap(t):
    n = t.shape[-1]
    lane = lax.broadcasted_iota(jnp.int32, t.shape, t.ndim - 1)
    half = MLA_ROPE // 2
    first = (lane & (MLA_ROPE - 1)) < half
    return jnp.where(first, pltpu.roll(t, n - half, t.ndim - 1), pltpu.roll(t, half, t.ndim - 1))


def _mla_mid_fwd(proj, q_norm, kv_norm, w_uq, w_ukv, cc, ss):
    T, PW = proj.shape
    QL, KVL = q_norm.shape[-1], kv_norm.shape[-1]
    H = MLA_HEADS
    assert PW == QL + KVL + 128
    tm = _mid_tile(T)

    def body(p_ref, qn_ref, kn_ref, wq_ref, wkv_ref, cc_ref, ss_ref,
             cq_ref, ckv_ref, q_ref, k_ref, v_ref):
        cq = p_ref[:, 0:QL]
        ckv = p_ref[:, QL:QL + KVL]
        kr = p_ref[:, QL + KVL:QL + KVL + 128]
        c, s = cc_ref[...], ss_ref[...]
        cqn = (cq * _rms_rstd(cq) * qn_ref[...]).astype(BF16)
        ckvn = (ckv * _rms_rstd(ckv) * kn_ref[...]).astype(BF16)
        cq_ref[...] = cqn
        ckv_ref[...] = ckvn
        q = jnp.dot(cqn, wq_ref[...], preferred_element_type=F32)
        kv = jnp.dot(ckvn, wkv_ref[...], preferred_element_type=F32)
        krf = (kr * c + _rope_swap(kr) * s).astype(BF16)
        for h in range(H):
            o = h * MLA_QK_PAD
            q_ref[:, o:o + MLA_NOPE] = (q[:, o:o + MLA_NOPE] * MLA_SCALE).astype(BF16)
            qr = q[:, o + MLA_NOPE:o + MLA_QK_PAD]
            q_ref[:, o + MLA_NOPE:o + MLA_QK_PAD] = ((qr * c + _rope_swap(qr) * s) * MLA_SCALE).astype(BF16)
            k_ref[:, o:o + MLA_NOPE] = kv[:, o:o + MLA_NOPE].astype(BF16)
            k_ref[:, o + MLA_NOPE:o + MLA_QK_PAD] = krf
            v_ref[:, h * MLA_V:(h + 1) * MLA_V] = kv[:, o + MLA_NOPE:o + MLA_QK_PAD].astype(BF16)

    def row(w):
        return pl.BlockSpec((tm, w), lambda i: (i, 0))

    def full(shape):
        return pl.BlockSpec(shape, lambda i: (0, 0))

    return pl.pallas_call(
        body, name="mla_mid_fwd", grid=(T // tm,),
        in_specs=[row(PW), full((1, QL)), full((1, KVL)), full(w_uq.shape), full(w_ukv.shape),
                  row(128), row(128)],
        out_specs=(row(QL), row(KVL), row(H * MLA_QK_PAD), row(H * MLA_QK_PAD), row(H * MLA_V)),
        out_shape=(jax.ShapeDtypeStruct((T, QL), BF16), jax.ShapeDtypeStruct((T, KVL), BF16),
                   jax.ShapeDtypeStruct((T, H * MLA_QK_PAD), BF16),
                   jax.ShapeDtypeStruct((T, H * MLA_QK_PAD), BF16),
                   jax.ShapeDtypeStruct((T, H * MLA_V), BF16)),
        compiler_params=_cparams("parallel"))(proj, q_norm, kv_norm, w_uq, w_ukv, cc, ss)


def _mla_mid_bwd(proj, q_norm, kv_norm, w_uq, w_ukv, cc, ss, dq, dk, dv):
    T, PW = proj.shape
    QL, KVL = q_norm.shape[-1], kv_norm.shape[-1]
    H = MLA_HEADS
    tm = _mid_tile(T)
    nt = (((1,), (1,)), ((), ()))

    def body(p_ref, qn_ref, kn_ref, wq_ref, wkv_ref, cc_ref, ss_ref, dq_ref, dk_ref, dv_ref,
             dqp_ref, dkv_ref, dp_ref, dqn_ref, dkn_ref):
        @pl.when(pl.program_id(0) == 0)
        def _():
            dqn_ref[...] = jnp.zeros_like(dqn_ref)
            dkn_ref[...] = jnp.zeros_like(dkn_ref)

        c, s = cc_ref[...], ss_ref[...]
        dkr = jnp.zeros((tm, 128), F32)
        for h in range(H):
            o = h * MLA_QK_PAD
            dqp_ref[:, o:o + MLA_NOPE] = (dq_ref[:, o:o + MLA_NOPE] * MLA_SCALE).astype(BF16)
            dqr = dq_ref[:, o + MLA_NOPE:o + MLA_QK_PAD] * MLA_SCALE
            dqp_ref[:, o + MLA_NOPE:o + MLA_QK_PAD] = (dqr * c + _rope_swap(dqr * s)).astype(BF16)
            dkv_ref[:, o:o + MLA_NOPE] = dk_ref[:, o:o + MLA_NOPE].astype(BF16)
            dkv_ref[:, o + MLA_NOPE:o + MLA_QK_PAD] = dv_ref[:, h * MLA_V:(h + 1) * MLA_V].astype(BF16)
            dkr = dkr + dk_ref[:, o + MLA_NOPE:o + MLA_QK_PAD]
        dcqn = lax.dot_general(dqp_ref[...], wq_ref[...], nt, preferred_element_type=F32)
        dckvn = lax.dot_general(dkv_ref[...], wkv_ref[...], nt, preferred_element_type=F32)
        dcq, dqn = _rms_bwd_tile(p_ref[:, 0:QL], qn_ref[...], dcqn)
        dckv, dkn = _rms_bwd_tile(p_ref[:, QL:QL + KVL], kn_ref[...], dckvn)
        dp_ref[:, 0:QL] = dcq.astype(BF16)
        dp_ref[:, QL:QL + KVL] = dckv.astype(BF16)
        dp_ref[:, QL + KVL:QL + KVL + 128] = (dkr * c + _rope_swap(dkr * s)).astype(BF16)
        dqn_ref[...] += dqn
        dkn_ref[...] += dkn

    def row(w):
        return pl.BlockSpec((tm, w), lambda i: (i, 0))

    def full(shape):
        return pl.BlockSpec(shape, lambda i: (0, 0))

    return pl.pallas_call(
        body, name="mla_mid_bwd", grid=(T // tm,),
        in_specs=[row(PW), full((1, QL)), full((1, KVL)), full(w_uq.shape), full(w_ukv.shape),
                  row(128), row(128), row(H * MLA_QK_PAD), row(H * MLA_QK_PAD), row(H * MLA_V)],
        out_specs=(row(H * MLA_QK_PAD), row(H * MLA_QK_PAD), row(PW), full((1, QL)), full((1, KVL))),
        out_shape=(jax.ShapeDtypeStruct((T, H * MLA_QK_PAD), BF16),
                   jax.ShapeDtypeStruct((T, H * MLA_QK_PAD), BF16),
                   jax.ShapeDtypeStruct((T, PW), BF16),
                   jax.ShapeDtypeStruct((1, QL), F32), jax.ShapeDtypeStruct((1, KVL), F32)),
        compiler_params=_cparams("arbitrary"))(proj, q_norm, kv_norm, w_uq, w_ukv, cc, ss, dq, dk, dv)


def _attn_tile(T):
    return min(1024, T)


def _attn_pairs(n, by_key):
    if by_key:
        pairs = [(qi, ki) for ki in range(n) for qi in range(ki, n)]
    else:
        pairs = [(qi, ki) for qi in range(n) for ki in range(qi + 1)]
    return (jnp.asarray([p[0] for p in pairs], jnp.int32), jnp.asarray([p[1] for p in pairs], jnp.int32))


def _scores(q, k, diagonal):
    s = lax.dot_general(q, k, (((1,), (1,)), ((), ())), preferred_element_type=F32)
    if diagonal:
        rows = lax.broadcasted_iota(jnp.int32, s.shape, 0)
        cols = lax.broadcasted_iota(jnp.int32, s.shape, 1)
        s = jnp.where(rows >= cols, s, -jnp.inf)
    return s


def _attn_fwd(q, k, v):
    T = q.shape[0]
    H, DQ, DV = MLA_HEADS, MLA_QK_PAD, MLA_V
    tq = _attn_tile(T)
    nq = T // tq
    G = MLA_HEADS_PER_STEP
    qi_tab, ki_tab = _attn_pairs(nq, by_key=False)

    def body(qi_ref, ki_ref, q_ref, k_ref, v_ref, o_ref, lse_ref, *scratch):
        m_refs, l_refs, acc_refs = scratch[0:G], scratch[G:2 * G], scratch[2 * G:3 * G]
        p = pl.program_id(1)
        qi, ki = qi_ref[p], ki_ref[p]

        @pl.when(ki == 0)
        def _():
            for g in range(G):
                m_refs[g][...] = jnp.full_like(m_refs[g], -jnp.inf)
                l_refs[g][...] = jnp.zeros_like(l_refs[g])
                acc_refs[g][...] = jnp.zeros_like(acc_refs[g])

        def update(ks, qr, masked):
            for g in range(G):
                qs, vs = slice(g * DQ, (g + 1) * DQ), slice(g * DV, (g + 1) * DV)
                st = _scores(k_ref[ks, qs], q_ref[qr, qs], False)
                if masked:
                    key = ks.start + lax.broadcasted_iota(jnp.int32, st.shape, 0)
                    qry = qr.start + lax.broadcasted_iota(jnp.int32, st.shape, 1)
                    st = jnp.where(qry >= key, st, -jnp.inf)
                m_prev = m_refs[g][:, qr]
                m_new = jnp.maximum(m_prev, jnp.max(st, axis=0, keepdims=True))
                alpha = jnp.exp(m_prev - m_new)
                pt = jnp.exp(st - m_new)
                l_refs[g][:, qr] = alpha * l_refs[g][:, qr] + jnp.sum(pt, axis=0, keepdims=True)
                acc_refs[g][:, qr] = alpha * acc_refs[g][:, qr] + lax.dot_general(
                    v_ref[ks, vs], pt.astype(BF16), (((0,), (0,)), ((), ())), preferred_element_type=F32)
                m_refs[g][:, qr] = m_new

        whole, half = slice(0, tq), tq // 2

        @pl.when(ki < qi)
        def _():
            update(whole, whole, False)

        @pl.when(ki == qi)
        def _():
            update(slice(0, half), whole, True)
            update(slice(half, tq), slice(half, tq), True)
            for g in range(G):
                vs = slice(g * DV, (g + 1) * DV)
                o_ref[:, vs] = jnp.transpose(acc_refs[g][...] / l_refs[g][...]).astype(BF16)
                lse_ref[g] = m_refs[g][...] + jnp.log(l_refs[g][...])

    return pl.pallas_call(
        body, name="attn_fwd",
        grid_spec=pltpu.PrefetchScalarGridSpec(
            num_scalar_prefetch=2, grid=(H // G, int(qi_tab.shape[0])),
            in_specs=[pl.BlockSpec((tq, G * DQ), lambda h, p, qt, kt: (qt[p], h)),
                      pl.BlockSpec((tq, G * DQ), lambda h, p, qt, kt: (kt[p], h)),
                      pl.BlockSpec((tq, G * DV), lambda h, p, qt, kt: (kt[p], h))],
            out_specs=(pl.BlockSpec((tq, G * DV), lambda h, p, qt, kt: (qt[p], h)),
                       pl.BlockSpec((G, 1, tq), lambda h, p, qt, kt: (h, 0, qt[p]))),
            scratch_shapes=([pltpu.VMEM((1, tq), F32)] * (2 * G) + [pltpu.VMEM((DV, tq), F32)] * G)),
        out_shape=(jax.ShapeDtypeStruct((T, H * DV), BF16), jax.ShapeDtypeStruct((H, 1, T), F32)),
        compiler_params=_cparams("parallel", "arbitrary"))(qi_tab, ki_tab, q, k, v)


def _attn_bwd(q, k, v, o, do, lse):
    T = q.shape[0]
    H, DQ, DV = MLA_HEADS, MLA_QK_PAD, MLA_V
    tq = _attn_tile(T)
    nq = T // tq
    tn = (((0,), (0,)), ((), ()))
    nt = (((1,), (1,)), ((), ()))
    G = MLA_HEADS_PER_STEP
    qi_tab, ki_tab = _attn_pairs(nq, by_key=True)

    def body(qi_ref, ki_ref, q_ref, k_ref, v_ref, o_ref, do_ref, lse_ref, dq_ref, dk_ref, dv_ref,
             dk_acc, dv_acc):
        p = pl.program_id(1)
        qi, ki = qi_ref[p], ki_ref[p]

        @pl.when(p == 0)
        def _():
            dq_ref[...] = jnp.zeros_like(dq_ref)

        @pl.when(qi == ki)
        def _():
            dk_acc[...] = jnp.zeros_like(dk_acc)
            dv_acc[...] = jnp.zeros_like(dv_acc)

        def step(ks, qr, masked):
            rows = pl.ds(pl.multiple_of(qi * tq + qr.start, qr.stop - qr.start), qr.stop - qr.start)
            for g in range(G):
                qs, vs = slice(g * DQ, (g + 1) * DQ), slice(g * DV, (g + 1) * DV)
                dof = do_ref[qr, vs]
                delta = jnp.sum(jnp.transpose(dof.astype(F32) * o_ref[qr, vs].astype(F32)), axis=0,
                                keepdims=True)
                st = _scores(k_ref[ks, qs], q_ref[qr, qs], False)
                if masked:
                    key = ks.start + lax.broadcasted_iota(jnp.int32, st.shape, 0)
                    qry = qr.start + lax.broadcasted_iota(jnp.int32, st.shape, 1)
                    st = jnp.where(qry >= key, st, -jnp.inf)
                pt = jnp.exp(st - lse_ref[g][:, qr])
                dpt = lax.dot_general(v_ref[ks, vs], dof, nt, preferred_element_type=F32)
                dst = (pt * (dpt - delta)).astype(BF16)
                dv_acc[ks, vs] += jnp.dot(pt.astype(BF16), dof, preferred_element_type=F32)
                dk_acc[ks, qs] += jnp.dot(dst, q_ref[qr, qs], preferred_element_type=F32)
                dq_ref[rows, qs] += lax.dot_general(dst, k_ref[ks, qs], tn, preferred_element_type=F32)

        whole, half = slice(0, tq), tq // 2

        @pl.when(qi == ki)
        def _():
            step(slice(0, half), whole, True)
            step(slice(half, tq), slice(half, tq), True)

        @pl.when(qi > ki)
        def _():
            step(whole, whole, False)

        @pl.when(qi == nq - 1)
        def _():
            dk_ref[...] = dk_acc[...]
            dv_ref[...] = dv_acc[...]

    qspec = pl.BlockSpec((tq, G * DQ), lambda h, p, qt, kt: (qt[p], h))
    ospec = pl.BlockSpec((tq, G * DV), lambda h, p, qt, kt: (qt[p], h))
    kspec = pl.BlockSpec((tq, G * DQ), lambda h, p, qt, kt: (kt[p], h))
    vspec = pl.BlockSpec((tq, G * DV), lambda h, p, qt, kt: (kt[p], h))
    return pl.pallas_call(
        body, name="attn_bwd",
        grid_spec=pltpu.PrefetchScalarGridSpec(
            num_scalar_prefetch=2, grid=(H // G, int(qi_tab.shape[0])),
            in_specs=[qspec, kspec, vspec, ospec, ospec,
                      pl.BlockSpec((G, 1, tq), lambda h, p, qt, kt: (h, 0, qt[p]))],
            out_specs=(pl.BlockSpec((T, G * DQ), lambda h, p, qt, kt: (0, h)), kspec, vspec),
            scratch_shapes=[pltpu.VMEM((tq, G * DQ), F32), pltpu.VMEM((tq, G * DV), F32)]),
        out_shape=(jax.ShapeDtypeStruct((T, H * DQ), F32), jax.ShapeDtypeStruct((T, H * DQ), F32),
                   jax.ShapeDtypeStruct((T, H * DV), F32)),
        compiler_params=_cparams("parallel", "arbitrary"))(qi_tab, ki_tab, q, k, v, o, do, lse)


def _mla_fwd(a, w, cc, ss, wbuf, pk, slot, res):
    D = a.shape[1]
    by_k, _, _ = _row_sharded(pk, "mla_w_o", slot, D // N_CHIPS)
    proj = _mm(a, w["w_in"], name="mla_in")
    cqn, ckvn, q, k, v = _mla_mid_fwd(proj, w["q_norm"], w["kv_norm"], w["w_uq"], w["w_ukv"], cc, ss)
    o, lse = _attn_fwd(q, k, v)
    m, h_new, a_next = _mm(o, wbuf, n=D, b_map=by_k, tm=1024, tk=D // N_CHIPS, tn=D, epi="resnorm", extra=res,
                           name="mla_out_res")
    return m, (a, proj, cqn, ckvn, q, k, v, o, lse), h_new, a_next


def _mla_bwd(dm, saved, w, cc, ss, wbuf, gbuf, pk, slot):
    a, proj, cqn, ckvn, q, k, v, o, lse = saved
    D = a.shape[1]
    _, by_n, by_m = _row_sharded(pk, "mla_w_o", slot, D // N_CHIPS)
    do = _mm(dm, wbuf, tb=True, n=o.shape[1], b_map=by_n, tm=2048, tn=D // N_CHIPS, tk=D, out_dtype=BF16,
             name="mla_out_dx")
    gbuf = _mm(o, dm, ta=True, into=gbuf, o_map=by_m, tm=D // N_CHIPS, tn=D, tk=2048, name="mla_out_dw")
    dq, dk, dv = _attn_bwd(q, k, v, o, do, lse)
    dqp, dkv, dproj, dqn, dkn = _mla_mid_bwd(proj, w["q_norm"], w["kv_norm"], w["w_uq"], w["w_ukv"],
                                             cc, ss, dq, dk, dv)
    dw_uq = _mm(cqn, dqp, ta=True, out_dtype=BF16, name="mla_uq_dw")
    dw_ukv = _mm(ckvn, dkv, ta=True, out_dtype=BF16, name="mla_ukv_dw")
    dw_in = _mm(a, dproj, ta=True, out_dtype=BF16, name="mla_in_dw")
    da = _mm(dproj, w["w_in"], tb=True, name="mla_in_dx")
    return da, gbuf, dict(w_in=dw_in, w_uq=dw_uq, w_ukv=dw_ukv, q_norm=dqn, kv_norm=dkn)


def _split_dot(mat, x, parts):
    acc = None
    rem = x
    for _ in range(parts):
        piece = rem.astype(BF16)
        term = jnp.dot(mat, piece, preferred_element_type=F32)
        acc = term if acc is None else acc + term
        rem = rem - piece.astype(F32)
    return acc


def _chunk_sums(cum, rel, rest, logf):
    return tuple(_split_dot(m.astype(BF16), logf, 3) for m in (cum, rel, rest))


def _chunk_mats(tb):
    C = HGRN_CHUNK
    assert C & (C - 1) == 0
    r = lax.broadcasted_iota(jnp.int32, (tb, tb), 0)
    s = lax.broadcasted_iota(jnp.int32, (tb, tb), 1)
    start = r & ~(C - 1)
    same = start == (s & ~(C - 1))
    ref = start + C // 2
    last = start + C - 1
    one, zero = jnp.float32(1.0), jnp.float32(0.0)
    cum = jnp.where(same & (s <= r), one, zero)
    rel = cum - jnp.where(same & (s <= ref), one, zero)
    rest = jnp.where(same & (s > r) & (s <= last), one, zero)
    rev = jnp.where(same & (s >= r), one, zero)
    ones = jnp.where(same, one, zero)
    causal = same & (s <= r)
    return cum, rel, rest, rev, ones, causal


def _hgrn_gates(p_ref, lb, HK):
    qx = p_ref[:, 0:HK]
    fx = p_ref[:, HK:2 * HK]
    sf = _sigmoid(fx)
    f = lb + (1.0 - lb) * sf
    sq = _sigmoid(qx)
    return qx, sq, qx * sq, sf, f, 1.0 - f, jnp.log(f)


def _hgrn_fwd(proj, lb, o_norm):
    T = proj.shape[0]
    H, C = HGRN_HEADS, HGRN_CHUNK
    HK = proj.shape[1] // 4
    DK = HK // H
    tb = min(HGRN_BLOCK, T)
    ncb = tb // C
    nt = (((1,), (1,)), ((), ()))
    tn = (((0,), (0,)), ((), ()))

    def body(p_ref, lb_ref, on_ref, y_ref, o_ref, st_ref, state, oacc):
        @pl.when(pl.program_id(0) == 0)
        def _():
            state[...] = jnp.zeros_like(state)

        cum, rel, rest, _, _, causal = _chunk_mats(tb)
        _, _, q, _, f, k, logf = _hgrn_gates(p_ref, lb_ref[...], HK)
        b, brel, brest = _chunk_sums(cum, rel, rest, logf)
        eb = jnp.exp(b)
        q_rel = (q * jnp.exp(brel)).astype(BF16)
        k_rel = (k * jnp.exp(-brel)).astype(BF16)
        q_dec = (q * eb).astype(BF16)
        k_dec = (k * jnp.exp(brest)).astype(BF16)
        v = p_ref[:, 2 * HK:3 * HK].astype(BF16)
        for h in range(H):
            hs = slice(h * DK, (h + 1) * DK)
            a = lax.dot_general(q_rel[:, hs], k_rel[:, hs], nt, preferred_element_type=F32)
            a = jnp.where(causal, a, 0.0).astype(BF16)
            oacc[:, hs] = jnp.dot(a, v[:, hs], preferred_element_type=F32)
            for j in range(ncb):
                rs = slice(j * C, (j + 1) * C)
                st = state[h]
                st_ref[j, h] = st
                oacc[rs, hs] += lax.dot_general(q_dec[rs, hs], st.astype(BF16), nt,
                                                preferred_element_type=F32)
                dec = jnp.exp(jnp.sum(logf[rs, hs], axis=0, keepdims=True))
                state[h] = dec * st + lax.dot_general(v[rs, hs], k_dec[rs, hs], tn,
                                                      preferred_element_type=F32)
        o = oacc[...]
        o_ref[...] = o
        gx = p_ref[:, 3 * HK:4 * HK]
        gate = gx * _sigmoid(gx)
        for h in range(H):
            hs = slice(h * DK, (h + 1) * DK)
            oh = o[:, hs]
            y_ref[:, hs] = (oh * _rms_rstd(oh) * on_ref[...] * gate[:, hs]).astype(BF16)

    return pl.pallas_call(
        body, name="hgrn_fwd", grid=(T // tb,),
        in_specs=[pl.BlockSpec((tb, 4 * HK), lambda i: (i, 0)),
                  pl.BlockSpec((1, HK), lambda i: (0, 0)),
                  pl.BlockSpec((1, DK), lambda i: (0, 0))],
        out_specs=(pl.BlockSpec((tb, HK), lambda i: (i, 0)),
                   pl.BlockSpec((tb, HK), lambda i: (i, 0)),
                   pl.BlockSpec((ncb, H, DK, DK), lambda i: (i, 0, 0, 0))),
        out_shape=(jax.ShapeDtypeStruct((T, HK), BF16), jax.ShapeDtypeStruct((T, HK), F32),
                   jax.ShapeDtypeStruct((T // C, H, DK, DK), F32)),
        scratch_shapes=[pltpu.VMEM((H, DK, DK), F32), pltpu.VMEM((tb, HK), F32)],
        compiler_params=_cparams("arbitrary"))(proj, lb, o_norm)


def _hgrn_bwd(proj, lb, o_norm, o, states, dy):
    T = proj.shape[0]
    H, C = HGRN_HEADS, HGRN_CHUNK
    HK = proj.shape[1] // 4
    DK = HK // H
    tb = min(HGRN_BLOCK, T)
    ncb = tb // C
    nb = T // tb
    nt = (((1,), (1,)), ((), ()))
    tn = (((0,), (0,)), ((), ()))

    def body(p_ref, lb_ref, on_ref, o_ref, st_ref, dy_ref, dp_ref, dlb_ref, don_ref,
             dstate, dqr_s, dkr_s, dqd_s, dkd_s, dv_s, do_s, e_s):
        @pl.when(pl.program_id(0) == 0)
        def _():
            dstate[...] = jnp.zeros_like(dstate)
            dlb_ref[...] = jnp.zeros_like(dlb_ref)
            don_ref[...] = jnp.zeros_like(don_ref)

        cum, rel, rest, rev, ones, causal = _chunk_mats(tb)
        lb = lb_ref[...]
        qx, sq, q, sf, f, k, logf = _hgrn_gates(p_ref, lb, HK)
        b, brel, brest = _chunk_sums(cum, rel, rest, logf)
        eb = jnp.exp(b)
        erel = jnp.exp(brel)
        enrel = jnp.exp(-brel)
        erest = jnp.exp(brest)
        q_rel_f, k_rel_f, q_dec_f, k_dec_f = q * erel, k * enrel, q * eb, k * erest
        q_rel, k_rel = q_rel_f.astype(BF16), k_rel_f.astype(BF16)
        q_dec, k_dec = q_dec_f.astype(BF16), k_dec_f.astype(BF16)
        v = p_ref[:, 2 * HK:3 * HK].astype(BF16)

        gx = p_ref[:, 3 * HK:4 * HK]
        sg = _sigmoid(gx)
        gate = gx * sg
        dy = dy_ref[...]
        ov = o_ref[...]
        on = on_ref[...]
        don = jnp.zeros((1, DK), F32)
        for h in range(H):
            hs = slice(h * DK, (h + 1) * DK)
            oh = ov[:, hs]
            r = _rms_rstd(oh)
            xh = oh * r
            d_on = dy[:, hs] * gate[:, hs]
            don = don + jnp.sum(d_on * xh, axis=0, keepdims=True)
            u = d_on * on
            do_s[:, hs] = r * (u - xh * jnp.mean(u * xh, axis=-1, keepdims=True))
            dp_ref[:, 3 * HK + h * DK:3 * HK + (h + 1) * DK] = (
                dy[:, hs] * xh * on * (sg[:, hs] * (1.0 + gx[:, hs] * (1.0 - sg[:, hs])))).astype(BF16)
        don_ref[...] += don

        for h in range(H):
            hs = slice(h * DK, (h + 1) * DK)
            doh = do_s[:, hs].astype(BF16)
            a = lax.dot_general(q_rel[:, hs], k_rel[:, hs], nt, preferred_element_type=F32)
            a = jnp.where(causal, a, 0.0).astype(BF16)
            da = lax.dot_general(doh, v[:, hs], nt, preferred_element_type=F32)
            da = jnp.where(causal, da, 0.0).astype(BF16)
            dv_s[:, hs] = lax.dot_general(a, doh, tn, preferred_element_type=F32)
            dqr_s[:, hs] = jnp.dot(da, k_rel[:, hs], preferred_element_type=F32)
            dkr_s[:, hs] = lax.dot_general(da, q_rel[:, hs], tn, preferred_element_type=F32)
            for j in reversed(range(ncb)):
                rs = slice(j * C, (j + 1) * C)
                dst = dstate[h]
                dstb = dst.astype(BF16)
                st = st_ref[j, h]
                dkd_s[rs, hs] = jnp.dot(v[rs, hs], dstb, preferred_element_type=F32)
                dv_s[rs, hs] += lax.dot_general(k_dec[rs, hs], dstb, nt, preferred_element_type=F32)
                dec = jnp.exp(jnp.sum(logf[rs, hs], axis=0, keepdims=True))
                e_s[rs, hs] = jnp.broadcast_to(jnp.sum(dst * st, axis=0, keepdims=True) * dec, (C, DK))
                dqd_s[rs, hs] = jnp.dot(doh[rs], st.astype(BF16), preferred_element_type=F32)
                dstate[h] = dec * dst + lax.dot_general(doh[rs], q_dec[rs, hs], tn,
                                                        preferred_element_type=F32)

        dqr, dkr, dqd, dkd = dqr_s[...], dkr_s[...], dqd_s[...], dkd_s[...]
        kdk = dkd * k_dec_f
        db = dqr * q_rel_f - dkr * k_rel_f + dqd * q_dec_f - kdk
        dlogf = _split_dot(rev.astype(BF16), db, 2) + _split_dot(ones.astype(BF16), kdk, 2) + e_s[...]
        dk = dkr * enrel + dkd * erest
        df = dlogf / f - dk
        dlb_ref[...] += jnp.sum(df * (1.0 - sf), axis=0, keepdims=True)
        dq = dqr * erel + dqd * eb
        dp_ref[:, 0:HK] = (dq * (sq * (1.0 + qx * (1.0 - sq)))).astype(BF16)
        dp_ref[:, HK:2 * HK] = (df * (1.0 - lb) * sf * (1.0 - sf)).astype(BF16)
        dp_ref[:, 2 * HK:3 * HK] = dv_s[...].astype(BF16)

    rev_row = lambda w: pl.BlockSpec((tb, w), lambda i: (nb - 1 - i, 0))
    vec = lambda w: pl.BlockSpec((1, w), lambda i: (0, 0))
    scr = pltpu.VMEM((tb, HK), F32)
    return pl.pallas_call(
        body, name="hgrn_bwd", grid=(nb,),
        in_specs=[rev_row(4 * HK), vec(HK), vec(DK), rev_row(HK),
                  pl.BlockSpec((ncb, H, DK, DK), lambda i: (nb - 1 - i, 0, 0, 0)), rev_row(HK)],
        out_specs=(rev_row(4 * HK), vec(HK), vec(DK)),
        out_shape=(jax.ShapeDtypeStruct((T, 4 * HK), BF16), jax.ShapeDtypeStruct((1, HK), F32),
                   jax.ShapeDtypeStruct((1, DK), F32)),
        scratch_shapes=[pltpu.VMEM((H, DK, DK), F32), scr, scr, scr, scr, scr, scr, scr],
        compiler_params=_cparams("arbitrary"))(proj, lb, o_norm, o, states, dy)


def _hgrn_layer_fwd(a, o_norm, lb, wbuf, pk, slot, res):
    D = a.shape[1]
    in_by_n, _ = _col_sharded(pk, "hgrn_w_in", slot, D)
    out_by_k, _, _ = _row_sharded(pk, "hgrn_w_o", slot, D // N_CHIPS)
    proj = _mm(a, wbuf, n=4 * D, b_map=in_by_n, tk=D, tn=D, name="hgrn_in")
    y, o, states = _hgrn_fwd(proj, lb, o_norm)
    m, h_new, a_next = _mm(y, wbuf, n=D, b_map=out_by_k, tm=1024, tk=D // N_CHIPS, tn=D, epi="resnorm",
                           extra=res, name="hgrn_out_res")
    return m, (a, proj, y, o, states), h_new, a_next


def _hgrn_layer_bwd(dm, saved, o_norm, lb, wbuf, gbuf, pk, slot):
    a, proj, y, o, states = saved
    D = a.shape[1]
    in_by_n, in_by_k = _col_sharded(pk, "hgrn_w_in", slot, D)
    _, out_by_n, out_by_m = _row_sharded(pk, "hgrn_w_o", slot, D // N_CHIPS)
    dy = _mm(dm, wbuf, tb=True, n=y.shape[1], b_map=out_by_n, tm=2048, tn=D // N_CHIPS, tk=D,
             name="hgrn_out_dx")
    gbuf = _mm(y, dm, ta=True, into=gbuf, o_map=out_by_m, tm=D // N_CHIPS, tn=D, tk=2048, name="hgrn_out_dw")
    dproj, dlb, don = _hgrn_bwd(proj, lb, o_norm, o, states, dy)
    gbuf = _mm(a, dproj, ta=True, into=gbuf, o_map=in_by_n, tm=D, tn=D, name="hgrn_in_dw")
    da = _mm(dproj, wbuf, tb=True, n=D, b_map=in_by_k, tn=D, tk=D, name="hgrn_in_dx")
    return da, gbuf, dict(o_norm=don, lb=dlb)


def _lower_bounds(lb_logits):
    p = jax.nn.softmax(lb_logits.astype(F32), axis=0)
    return jnp.cumsum(p, axis=0) - p[0]


def _rope_tables(positions):
    inv_freq = jnp.power(ROPE_BASE, -jnp.arange(0, MLA_ROPE, 2, dtype=F32) / MLA_ROPE)
    ang = positions.astype(F32)[:, None] * inv_freq
    cos, sin = jnp.cos(ang), jnp.sin(ang)
    zero = jnp.zeros((positions.shape[0], 128 - MLA_ROPE), F32)
    return (jnp.concatenate([cos, cos, zero], axis=-1), jnp.concatenate([-sin, sin, zero], axis=-1))


def _pad_mla_weights(w_in, w_uq):
    w_in_p = jnp.pad(w_in, ((0, 0), (0, 0), (0, 128 - MLA_ROPE)))
    n, ql, _ = w_uq.shape
    w_uq_p = jnp.pad(w_uq.reshape(n, ql, MLA_HEADS, MLA_NOPE + MLA_ROPE),
                     ((0, 0), (0, 0), (0, 0), (0, MLA_QK_PAD - MLA_NOPE - MLA_ROPE)))
    return w_in_p, w_uq_p.reshape(n, ql, MLA_HEADS * MLA_QK_PAD)


def _local_step(x, positions, target, small, fetch, gbufs, emit, emit_mlp):
    T, D = x.shape
    lbounds, lb_vjp = jax.vjp(_lower_bounds, small["hgrn_lb_logits"])
    cc, ss = _rope_tables(positions)
    fetched = {0: fetch(0, None)}
    gains = fetched[0]["gains"]
    tick = [jnp.zeros((), F32)]

    def g(layer, i):
        return gains[layer, i][None, :] + tick[0]

    def mla_weights(layer):
        f = fetched[layer]
        w_in_p, w_uq_p = _pad_mla_weights(f["w_in"][None], f["w_uq"][None])
        slot = layer // 2
        return dict(w_in=w_in_p[0], w_uq=w_uq_p[0], w_ukv=f["w_ukv"],
                    q_norm=small["mla_q_norm"][slot][None, :], kv_norm=small["mla_kv_norm"][slot][None, :])

    saved = []
    h = x
    a = _prenorm_fwd(x, g(0, 0))
    dy = sq = None
    for layer in range(DEPTH):
        slot = layer // 2
        if layer not in fetched:
            fetched[layer] = fetch(layer, a)
        wbuf, pk = fetched[layer]["wbuf"], fetched[layer]["pk"]
        res = (h, g(layer, 1), g(layer, 2))
        if layer % 2 == 0:
            m, mix_saved, h1, a2 = _mla_fwd(a, mla_weights(layer), cc, ss, wbuf, pk, slot, res)
        else:
            m, mix_saved, h1, a2 = _hgrn_layer_fwd(a, small["hgrn_o_norm"][slot][None, :],
                                                   lbounds[layer][None, :], wbuf, pk, slot, res)
        if layer + 1 < DEPTH:
            u, mlp_saved, h2, a = _mlp_fwd(a2, wbuf, pk, layer, (h1, g(layer, 3), g(layer + 1, 0)))
        else:
            u, mlp_saved, h2, _ = _mlp_fwd(a2, wbuf, pk, layer, None)
            dy, sq = _resnorm_loss(h1, u, g(layer, 3), target)
        saved.append((h, m, h1, u, mix_saved, mlp_saved))
        h = h2

    n_mla, n_hgrn = (DEPTH + 1) // 2, DEPTH // 2
    dgains = [[None] * 4 for _ in range(DEPTH)]
    gw = {k: [None] * n_mla for k in ("mla_w_in", "mla_w_uq", "mla_w_ukv", "mla_q_norm", "mla_kv_norm")}
    gw["hgrn_o_norm"] = [None] * n_hgrn
    dlb = [jnp.zeros((1, lbounds.shape[1]), F32) for _ in range(DEPTH)]
    dh = dy
    da_next = None
    for layer in reversed(range(DEPTH)):
        h0, m, h1, u, mix_saved, mlp_saved = saved[layer]
        slot = layer // 2
        wbuf, pk, gbuf = fetched[layer]["wbuf"], fetched[layer]["pk"], gbufs[layer]
        if da_next is None:
            du, dgains[layer][3] = _resnorm_bwd(u, g(layer, 3), dh, name="resnorm_bwd_last")
            t = dh
        else:
            h2 = saved[layer + 1][0]
            t, du, dgains[layer][3], dgains[layer + 1][0] = _resnorm_bwd(
                u, g(layer, 3), dh, h2, da_next, g(layer + 1, 0), name="resnorm_bwd_mlp")
        da2, gbuf = _mlp_bwd(du, mlp_saved, wbuf, gbuf, pk, layer)
        if layer == 0:
            gbuf = emit_mlp(layer, gbuf)
        t, dm, dgains[layer][1], dgains[layer][2] = _resnorm_bwd(
            m, g(layer, 1), t, h1, da2, g(layer, 2), name="resnorm_bwd_mix")
        if layer % 2 == 0:
            da_next, gbuf, mg = _mla_bwd(dm, mix_saved, mla_weights(layer), cc, ss, wbuf, gbuf, pk, slot)
            ql = mg["q_norm"].shape[-1]
            kvl = mg["kv_norm"].shape[-1]
            gw["mla_w_in"][slot] = mg["w_in"][:, :ql + kvl + MLA_ROPE]
            gw["mla_w_uq"][slot] = mg["w_uq"].reshape(ql, MLA_HEADS, MLA_QK_PAD)[
                :, :, :MLA_NOPE + MLA_ROPE].reshape(ql, MLA_HEADS * (MLA_NOPE + MLA_ROPE))
            gw["mla_w_ukv"][slot] = mg["w_ukv"]
            gw["mla_q_norm"][slot] = mg["q_norm"][0]
            gw["mla_kv_norm"][slot] = mg["kv_norm"][0]
        else:
            da_next, gbuf, hg = _hgrn_layer_bwd(dm, mix_saved, small["hgrn_o_norm"][slot][None, :],
                                                lbounds[layer][None, :], wbuf, gbuf, pk, slot)
            gw["hgrn_o_norm"][slot] = hg["o_norm"][0]
            dlb[layer] = hg["lb"]
        dh = t
        if layer > 0:
            mine = ({k: gw[k][slot] for k in ("mla_w_in", "mla_w_uq", "mla_w_ukv")} if layer % 2 == 0 else {})
            tick[0] = emit(layer, gbuf, mine)
        else:
            gbuf0 = gbuf
    grad_x, dgains[0][0] = _prenorm_bwd(x, g(0, 0), dh, da_next)

    last = {k: gw[k][0] for k in ("mla_w_in", "mla_w_uq", "mla_w_ukv")}
    last.update({k: jnp.stack(gw[k]) for k in ("mla_q_norm", "mla_kv_norm", "hgrn_o_norm")})
    last["norm_gains"] = jnp.stack([jnp.concatenate(row, axis=0) for row in dgains])
    (last["hgrn_lb_logits"],) = lb_vjp(jnp.concatenate(dlb, axis=0))
    emit(0, gbuf0, last)
    return sq, grad_x


def _size(shape):
    n = 1
    for d in shape:
        n *= d
    return n


def _piece_rows(shape):
    return -(-_size(shape) // PACK_W)


def _packed_misc_rows(shapes):
    return sum(_piece_rows(s) for s in shapes)


def _cast_into(src, buf, row, name):
    rows, W = src.shape
    tr = min(256, rows)
    assert rows % tr == 0 and row % tr == 0

    def body(s_ref, b_ref, o_ref):
        o_ref[...] = s_ref[...].astype(BF16)

    return pl.pallas_call(
        body, name=name, grid=(rows // tr,),
        in_specs=[pl.BlockSpec((tr, W), lambda i: (i, 0)), pl.BlockSpec(memory_space=pl.ANY)],
        out_specs=pl.BlockSpec((tr, W), lambda i: (row // tr + i, 0)),
        out_shape=jax.ShapeDtypeStruct(buf.shape, buf.dtype), input_output_aliases={1: 0},
        compiler_params=_cparams("parallel"))(src, buf)


def _pack_blocks(pieces, rows, dtype):
    blocks, used = [], 0
    for p in pieces:
        flat = p.astype(dtype).reshape(-1)
        r = _piece_rows(p.shape)
        if r * PACK_W != flat.shape[0]:
            flat = jnp.pad(flat, (0, r * PACK_W - flat.shape[0]))
        blocks.append(flat.reshape(r, PACK_W))
        used += r
    if rows > used:
        blocks.append(jnp.zeros((rows - used, PACK_W), dtype))
    return blocks


def _unpack(buf, shapes):
    out, off = [], 0
    for shp in shapes:
        r = _piece_rows(shp)
        piece = buf[off:off + r]
        if r * PACK_W != _size(shp):
            piece = piece.reshape(-1)[:_size(shp)]
        out.append(piece.reshape(shp))
        off += r
    return out


def _mesh_place():
    x, y, c = lax.axis_index("x"), lax.axis_index("y"), lax.axis_index("c")
    chips = [(1 - x, y), (x, 1 - y), (1 - x, 1 - y)]
    return x, y, c, chips


_HBM = pl.BlockSpec(memory_space=pltpu.HBM)


def _share_reduced(q, name="grads_share_reduced"):
    rh, W = q.shape

    def body(q_ref, out_ref, send_sem, recv_sem):
        x, y, c, _ = _mesh_place()
        cp = pltpu.make_async_remote_copy(src_ref=q_ref, dst_ref=out_ref.at[c], send_sem=send_sem,
                                          recv_sem=recv_sem, device_id=(x, y, 1 - c), device_id_type=MESH)
        cp.start()
        cp.wait()

    out = pl.pallas_call(
        body, name=name, in_specs=[_HBM], out_specs=_HBM,
        out_shape=jax.ShapeDtypeStruct((2, rh, W), q.dtype),
        scratch_shapes=[pltpu.SemaphoreType.DMA, pltpu.SemaphoreType.DMA],
    )(q)
    return out


def _sum_chips(parts, own, own_row0, which, name, out_dtype=F32):
    n, rh, W = parts.shape
    tr = PACK_TILE
    assert own_row0 % tr == 0
    if own.ndim == 3:
        own_spec = pl.BlockSpec((None, tr, W), lambda i, w_ref: (w_ref[0], own_row0 // tr + i, 0))
    else:
        own_spec = pl.BlockSpec((tr, W), lambda i, w_ref: (own_row0 // tr + i, 0))

    def body(w_ref, p_ref, own_ref, o_ref):
        mine = own_ref[...].astype(F32)
        acc = None
        for j in range(n):
            term = jnp.where(w_ref[0] == j, mine, p_ref[j].astype(F32))
            acc = term if acc is None else acc + term
        o_ref[...] = acc.astype(out_dtype)

    return pl.pallas_call(
        body, name=name,
        grid_spec=pltpu.PrefetchScalarGridSpec(
            num_scalar_prefetch=1, grid=(rh // tr,),
            in_specs=[pl.BlockSpec((n, tr, W), lambda i, w_ref: (0, i, 0)), own_spec],
            out_specs=pl.BlockSpec((tr, W), lambda i, w_ref: (i, 0))),
        out_shape=jax.ShapeDtypeStruct((rh, W), out_dtype),
        compiler_params=_cparams("parallel"))(jnp.reshape(which, (1,)).astype(jnp.int32), parts, own)


_SEM = pl.BlockSpec(memory_space=pltpu.SEMAPHORE)
_ASYNC = pltpu.CompilerParams(has_side_effects=pltpu.SideEffectType.DATAFLOW_SIDE_EFFECTING)


def _hbm(a):
    return pltpu.with_memory_space_constraint(a, pltpu.HBM)


def _gather_copies(w_ref, land_ref, send_sems, recv_sems):
    x, y, c, chips = _mesh_place()
    me = 2 * x + y
    rh = w_ref.shape[0] // 2
    rows = pl.ds(pl.multiple_of(c * rh, 16), rh)
    return [pltpu.make_async_remote_copy(
        src_ref=w_ref.at[rows], dst_ref=land_ref.at[me, rows], send_sem=send_sems.at[r],
        recv_sem=recv_sems.at[r], device_id=(px, py, c), device_id_type=MESH)
        for r, (px, py) in enumerate(chips)]


def _scatter_copies(g_ref, land_ref, send_sems, recv_sems, row0):
    x, y, c, chips = _mesh_place()
    me = 2 * x + y
    rows = pl.ds(row0, land_ref.shape[1])
    return [pltpu.make_async_remote_copy(
        src_ref=g_ref.at[2 * px + py, rows], dst_ref=land_ref.at[me], send_sem=send_sems.at[r],
        recv_sem=recv_sems.at[r], device_id=(px, py, c), device_id_type=MESH)
        for r, (px, py) in enumerate(chips)]


def _halves_to_sibling(land, name):
    n, R, W = land.shape
    rh = R // 2

    def body(l_ref, o_ref, send_sems, recv_sems):
        x, y, c, chips = _mesh_place()
        rows = pl.ds(pl.multiple_of(c * rh, 16), rh)
        copies = [pltpu.make_async_remote_copy(
            src_ref=o_ref.at[2 * px + py, rows], dst_ref=o_ref.at[2 * px + py, rows], send_sem=send_sems.at[r],
            recv_sem=recv_sems.at[r], device_id=(x, y, 1 - c), device_id_type=MESH)
            for r, (px, py) in enumerate(chips)]
        for cp in copies:
            cp.start()
        for cp in copies:
            cp.wait()

    return pl.pallas_call(
        body, name=name, in_specs=[_HBM], out_specs=_HBM, out_shape=jax.ShapeDtypeStruct(land.shape, land.dtype),
        scratch_shapes=[pltpu.SemaphoreType.DMA((3,)), pltpu.SemaphoreType.DMA((3,))],
        input_output_aliases={0: 0})(land)


def _gather_start(wp, name):
    R, W = wp.shape

    def body(w_ref, land_ref, send_sems, recv_sems, w_thru, land_thru, token):
        for cp in _gather_copies(w_ref, land_ref, send_sems, recv_sems):
            cp.start()
        token[...] = jnp.zeros_like(token)

    return pl.pallas_call(
        body, name=name,
        out_shape=(pltpu.SemaphoreType.DMA((3,)), pltpu.SemaphoreType.DMA((3,)), pltpu.HBM(wp.shape, wp.dtype),
                   pltpu.HBM((N_CHIPS, R, W), wp.dtype), jax.ShapeDtypeStruct((8, 128), F32)),
        in_specs=(_HBM, _HBM),
        out_specs=(_SEM, _SEM, _HBM, _HBM, pl.BlockSpec(memory_space=pltpu.VMEM)),
        input_output_aliases={0: 2, 1: 3}, compiler_params=_ASYNC,
    )(_hbm(wp), _hbm(lax.empty((N_CHIPS, R, W), wp.dtype)))


def _gather_wait(send_sems, recv_sems, w_thru, land_thru, after, name):
    R, W = w_thru.shape
    rh = R // 2

    def body(w_ref, land_ref, send_sems, recv_sems, after_ref, w_dead, got_ref):
        x, y, c, _ = _mesh_place()
        half = land_ref.at[0, pl.ds(0, rh)]
        for k in range(3):
            cp = pltpu.make_async_remote_copy(src_ref=half, dst_ref=half, send_sem=send_sems.at[k],
                                              recv_sem=recv_sems.at[k], device_id=(x, y, 1 - c),
                                              device_id_type=MESH)
            cp.wait_send()
            cp.wait_recv()

    return pl.pallas_call(
        body, name=name,
        out_shape=(pltpu.HBM(w_thru.shape, w_thru.dtype), pltpu.HBM(land_thru.shape, land_thru.dtype)),
        in_specs=(_HBM, _HBM, _SEM, _SEM, pl.BlockSpec(memory_space=pl.ANY)), out_specs=(_HBM, _HBM),
        input_output_aliases={0: 0, 1: 1}, compiler_params=_ASYNC,
    )(w_thru, land_thru, send_sems, recv_sems, after)


def _scatter_start(g, row0, nrows, name):
    n, R, W = g.shape
    land_shape = (n, nrows, W)

    def body(g_ref, land_ref, send_sems, recv_sems, g_thru, land_thru, token):
        for cp in _scatter_copies(g_ref, land_ref, send_sems, recv_sems, row0):
            cp.start()
        token[...] = jnp.zeros_like(token)

    return pl.pallas_call(
        body, name=name,
        out_shape=(pltpu.SemaphoreType.DMA((3,)), pltpu.SemaphoreType.DMA((3,)), pltpu.HBM(g.shape, g.dtype),
                   pltpu.HBM(land_shape, g.dtype), jax.ShapeDtypeStruct((8, 128), F32)),
        in_specs=(_HBM, _HBM),
        out_specs=(_SEM, _SEM, _HBM, _HBM, pl.BlockSpec(memory_space=pltpu.VMEM)),
        input_output_aliases={0: 2, 1: 3}, compiler_params=_ASYNC,
    )(_hbm(g), _hbm(lax.empty(land_shape, g.dtype)))


def _scatter_wait(send_sems, recv_sems, g_thru, land_thru, after, name):
    def body(g_ref, land_ref, send_sems, recv_sems, after_ref, g_out, got_ref):
        x, y, c, _ = _mesh_place()
        for k in range(3):
            cp = pltpu.make_async_remote_copy(src_ref=land_ref.at[0], dst_ref=land_ref.at[0], send_sem=send_sems.at[k],
                                              recv_sem=recv_sems.at[k], device_id=(x, y, 1 - c),
                                              device_id_type=MESH)
            cp.wait_send()
            cp.wait_recv()

    return pl.pallas_call(
        body, name=name,
        out_shape=(pltpu.HBM(g_thru.shape, g_thru.dtype), pltpu.HBM(land_thru.shape, land_thru.dtype)),
        in_specs=(_HBM, _HBM, _SEM, _SEM, pl.BlockSpec(memory_space=pl.ANY)), out_specs=(_HBM, _HBM),
        input_output_aliases={0: 0, 1: 1}, compiler_params=_ASYNC,
    )(g_thru, land_thru, send_sems, recv_sems, after)


def _adamw(w, g, m, v, name):
    shape = w.shape
    cols = shape[-1]
    w2, g2, m2, v2 = (t.reshape(-1, cols) for t in (w, g, m, v))
    rows = w2.shape[0]
    tr = rows
    for cand in (512, 256, 128, 64, 32, 16, 8):
        if rows > cand and rows % cand == 0:
            tr = cand
            break
    c1 = 1.0 / (1.0 - ADAM_B1 ** ADAM_STEP)
    c2 = 1.0 / (1.0 - ADAM_B2 ** ADAM_STEP)

    def body(w_ref, g_ref, m_ref, v_ref, d_ref, nm_ref, nv_ref):
        gv = g_ref[...]
        nm = ADAM_B1 * m_ref[...] + (1.0 - ADAM_B1) * gv
        nv = ADAM_B2 * v_ref[...] + (1.0 - ADAM_B2) * (gv * gv)
        nm_ref[...] = nm
        nv_ref[...] = nv
        d_ref[...] = -ADAM_LR * ((nm * c1) / (jnp.sqrt(nv * c2) + ADAM_EPS) + ADAM_WD * w_ref[...])

    blk = pl.BlockSpec((tr, cols), lambda i: (i, 0))
    sds = jax.ShapeDtypeStruct((rows, cols), F32)
    d, nm, nv = pl.pallas_call(body, name=name, grid=(rows // tr,), in_specs=[blk] * 4,
                               out_specs=(blk, blk, blk), out_shape=(sds, sds, sds),
                               compiler_params=_cparams("parallel"))(w2, g2, m2, v2)
    return d.reshape(shape), nm.reshape(shape), nv.reshape(shape)


def kernel(x, positions, norm_gains, mla_w_in, mla_q_norm, mla_kv_norm, mla_w_uq, mla_w_ukv, mla_w_o, hgrn_w_in, hgrn_lb_logits, hgrn_o_norm, hgrn_w_o, mlp_w1, mlp_w2, loss_target, m_norm_gains, m_mla_w_in, m_mla_q_norm, m_mla_kv_norm, m_mla_w_uq, m_mla_w_ukv, m_mla_w_o, m_hgrn_w_in, m_hgrn_lb_logits, m_hgrn_o_norm, m_hgrn_w_o, m_mlp_w1, m_mlp_w2, v_norm_gains, v_mla_w_in, v_mla_q_norm, v_mla_kv_norm, v_mla_w_uq, v_mla_w_ukv, v_mla_w_o, v_hgrn_w_in, v_hgrn_lb_logits, v_hgrn_o_norm, v_hgrn_w_o, v_mlp_w1, v_mlp_w2):
    w = dict(norm_gains=norm_gains, mla_w_in=mla_w_in, mla_q_norm=mla_q_norm, mla_kv_norm=mla_kv_norm,
             mla_w_uq=mla_w_uq, mla_w_ukv=mla_w_ukv, mla_w_o=mla_w_o, hgrn_w_in=hgrn_w_in,
             hgrn_lb_logits=hgrn_lb_logits, hgrn_o_norm=hgrn_o_norm, hgrn_w_o=hgrn_w_o,
             mlp_w1=mlp_w1, mlp_w2=mlp_w2)
    mom_m = dict(norm_gains=m_norm_gains, mla_w_in=m_mla_w_in, mla_q_norm=m_mla_q_norm,
                 mla_kv_norm=m_mla_kv_norm, mla_w_uq=m_mla_w_uq, mla_w_ukv=m_mla_w_ukv,
                 mla_w_o=m_mla_w_o, hgrn_w_in=m_hgrn_w_in, hgrn_lb_logits=m_hgrn_lb_logits,
                 hgrn_o_norm=m_hgrn_o_norm, hgrn_w_o=m_hgrn_w_o, mlp_w1=m_mlp_w1, mlp_w2=m_mlp_w2)
    mom_v = dict(norm_gains=v_norm_gains, mla_w_in=v_mla_w_in, mla_q_norm=v_mla_q_norm,
                 mla_kv_norm=v_mla_kv_norm, mla_w_uq=v_mla_w_uq, mla_w_ukv=v_mla_w_ukv,
                 mla_w_o=v_mla_w_o, hgrn_w_in=v_hgrn_w_in, hgrn_lb_logits=v_hgrn_lb_logits,
                 hgrn_o_norm=v_hgrn_o_norm, hgrn_w_o=v_hgrn_w_o, mlp_w1=v_mlp_w1, mlp_w2=v_mlp_w2)
    c = lax.axis_index("c")

    axis_of = dict(SHARDED)
    me = 2 * lax.axis_index("x") + lax.axis_index("y")
    gain_bits = lax.bitcast_convert_type(norm_gains, jnp.uint32)
    gain_hi = lax.bitcast_convert_type((gain_bits >> 16).astype(jnp.uint16), BF16)
    gain_lo = lax.bitcast_convert_type((gain_bits & 0xFFFF).astype(jnp.uint16), BF16)

    layers = []
    for l in range(DEPTH):
        s = l // 2
        if l % 2 == 0:
            big = [("mlp_w1", l), ("mlp_w2", l), ("mla_w_o", s)]
            tail = [("mla_w_in", s), ("mla_w_uq", s), ("mla_w_ukv", s)]
        else:
            big = [("hgrn_w_in", s), ("mlp_w1", l), ("mlp_w2", l), ("hgrn_w_o", s)]
            tail = []
        w_tail = [w[n][i] for n, i in tail] + ([gain_hi, gain_lo] if l == 0 else [])
        g_tail = tail + ([("norm_gains", None)] + [(n, None) for n in REPLICATED] if l == 0 else [])
        g_shapes = [w[n].shape if i is None else w[n][i].shape for n, i in g_tail]
        tail_rows = max(_packed_misc_rows([t.shape for t in w_tail]), _packed_misc_rows(g_shapes))
        pk = _Packed([(n, w[n].shape[1]) for n, _ in big], tail_rows)
        wpack = jnp.zeros((pk.rows, PACK_W), BF16)
        for n, i in big:
            assert w[n].shape[2] == PACK_W
            wpack = _cast_into(w[n][i], wpack, pk.off[n], name="pack_%s_%d" % (n, l))
        if w_tail:
            wpack = lax.dynamic_update_slice(
                wpack, jnp.concatenate(_pack_blocks(w_tail, 0, BF16), axis=0), (pk.misc, 0))
        layers.append(dict(pk=pk, big=big, tail=tail, w_tail=w_tail, g_tail=g_tail, g_shapes=g_shapes,
                           gather=_gather_start(wpack, name="gather_start_%d" % l)))

    def fetch(l, after):
        lay = layers[l]
        pk = lay["pk"]
        send_sems, recv_sems, w_thru, land_thru, _ = lay["gather"]
        if after is None:
            after = sum(layers[k]["gather"][4] for k in range(1, DEPTH))
        w_back, land = _gather_wait(send_sems, recv_sems, w_thru, land_thru, after, name="gather_wait_%d" % l)
        land = _halves_to_sibling(land, name="gather_halves_%d" % l)
        land = lax.dynamic_update_slice(land, w_back[None], (me, 0, 0))
        out = dict(wbuf=land.reshape(N_CHIPS * pk.rows, PACK_W), pk=pk)
        if lay["w_tail"]:
            rows = _packed_misc_rows([t.shape for t in lay["w_tail"]])
            per_chip = [_unpack(land[j, pk.misc:pk.misc + rows], [t.shape for t in lay["w_tail"]])
                        for j in range(N_CHIPS)]
            for i, (n, _) in enumerate(lay["tail"]):
                out[n[4:]] = jnp.concatenate([per_chip[j][i] for j in range(N_CHIPS)], axis=axis_of[n] - 1)
            if l == 0:
                got_hi, got_lo = (lax.bitcast_convert_type(
                    jnp.concatenate([per_chip[j][i] for j in range(N_CHIPS)], axis=2),
                    jnp.uint16).astype(jnp.uint32) for i in (-2, -1))
                out["gains"] = lax.bitcast_convert_type((got_hi << 16) | got_lo, F32)
        return out

    def emit(l, gbuf, grads):
        lay = layers[l]
        pk = lay["pk"]
        if lay["g_tail"]:
            for j in range(N_CHIPS):
                pieces = []
                for n, i in lay["g_tail"]:
                    if n not in axis_of:
                        pieces.append(grads[n])
                    else:
                        pieces.append(jnp.split(grads[n], N_CHIPS, axis=axis_of[n] - (0 if i is None else 1))[j])
                block = jnp.concatenate(_pack_blocks(pieces, 0, BF16), axis=0)
                gbuf = lax.dynamic_update_slice(gbuf, block, (j * pk.rows + pk.misc, 0))
        row0 = lay.get("early_rows", 0)
        lay["scatter"] = _scatter_start(gbuf.reshape(N_CHIPS, pk.rows, PACK_W), row0, pk.rows - row0,
                                        name="scatter_start_%d" % l)
        return lay["scatter"][4][0, 0]

    def emit_mlp(l, gbuf):
        lay = layers[l]
        pk = lay["pk"]
        assert pk.off["mlp_w1"] == 0 and pk.off["mlp_w2"] == w["mlp_w1"].shape[1]
        lay["early_rows"] = w["mlp_w1"].shape[1] + w["mlp_w2"].shape[1]
        lay["scatter_early"] = _scatter_start(gbuf.reshape(N_CHIPS, pk.rows, PACK_W), 0, lay["early_rows"],
                                              name="scatter_start_%d_mlp" % l)
        return lay["scatter_early"][2].reshape(N_CHIPS * pk.rows, PACK_W)

    small = dict(mla_q_norm=mla_q_norm, mla_kv_norm=mla_kv_norm, hgrn_lb_logits=hgrn_lb_logits,
                 hgrn_o_norm=hgrn_o_norm)
    gbufs = [lax.empty((N_CHIPS * lay["pk"].rows, PACK_W), BF16) for lay in layers]
    sq, grad_x = _local_step(x[0], positions[0], loss_target[0], small, fetch, gbufs, emit, emit_mlp)
    d_model = x.shape[-1]
    loss = lax.psum(0.5 * jnp.sum(sq) / d_model, ("x", "y", "c"))

    per_name = {}
    behind = grad_x
    for l, lay in reversed(list(enumerate(layers))):
        pk = lay["pk"]
        send_sems, recv_sems, g_thru, land_thru, _ = lay["scatter"]
        row0 = lay.get("early_rows", 0)
        early = None
        if row0:
            e_send, e_recv, _, e_land, _ = lay["scatter_early"]
            g_thru, land = _scatter_wait(e_send, e_recv, g_thru, e_land, behind, name="scatter_wait_%d_mlp" % l)
            early = behind = _sum_chips(land, g_thru, 0, me, name="grads_sum_chips_%d_mlp" % l, out_dtype=BF16)
        g_back, land = _scatter_wait(send_sems, recv_sems, g_thru, land_thru, behind, name="scatter_wait_%d" % l)
        mine = _sum_chips(land, g_back, row0, me, name="grads_sum_chips_%d" % l, out_dtype=BF16)
        if early is not None:
            mine = jnp.concatenate([early, mine], axis=0)
        red = behind = _sum_chips(_share_reduced(mine, name="grads_share_%d" % l), mine, 0, c,
                                  name="grads_sum_cores_%d" % l)
        for n, i in lay["big"]:
            per_name.setdefault(n, {})[i] = red[pk.off[n]:pk.off[n] + w[n].shape[1]]
        for (n, i), piece in zip(lay["g_tail"], _unpack(red[pk.misc:pk.misc + pk.misc_rows], lay["g_shapes"])):
            per_name.setdefault(n, {})[i] = piece
    g_out = {n: (parts[None] if None in parts else jnp.stack([parts[i] for i in sorted(parts)]))
             for n, parts in per_name.items()}

    deltas, new_m, new_v = {}, {}, {}
    for name in WEIGHTS:
        deltas[name], new_m[name], new_v[name] = _adamw(w[name], g_out[name], mom_m[name], mom_v[name],
                                                        name="adamw_" + name)
    return (loss, grad_x[None], *[g_out[n] for n in WEIGHTS], *[deltas[n] for n in WEIGHTS],
            *[new_m[n] for n in WEIGHTS], *[new_v[n] for n in WEIGHTS])
```

```python
import jax
import jax.numpy as jnp
from jax import lax
from jax.experimental import pallas as pl
from jax.experimental.pallas import tpu as pltpu

F32 = jnp.float32
BF16 = jnp.bfloat16
MESH = pl.DeviceIdType.MESH

DEPTH = 4
MLA_HEADS = 8
MLA_NOPE = 128
MLA_ROPE = 64
MLA_V = 128
MLA_QK_PAD = 256
MLA_HEADS_PER_STEP = 2
MLA_SCALE = float(MLA_NOPE + MLA_ROPE) ** -0.5
ROPE_BASE = 10000.0
HGRN_HEADS = 8
HGRN_CHUNK = 32
HGRN_BLOCK = 128
EPS = 1e-6

ADAM_LR = 0.001
ADAM_B1 = 0.9
ADAM_B2 = 0.999
ADAM_EPS = 1e-08
ADAM_WD = 0.01
ADAM_STEP = 10

N_CHIPS = 4
PACK_W = 1024
PACK_ALIGN = 1024
PACK_TILE = 512
V7X_VMEM_LIMIT = 56 * 1024 * 1024

SHARDED = (("norm_gains", 2), ("mla_w_in", 1), ("mla_w_uq", 2), ("mla_w_ukv", 2), ("mla_w_o", 1),
           ("hgrn_w_in", 2), ("hgrn_w_o", 1), ("mlp_w1", 2), ("mlp_w2", 1))
REPLICATED = ("mla_q_norm", "mla_kv_norm", "hgrn_lb_logits", "hgrn_o_norm")
WEIGHTS = ("norm_gains", "mla_w_in", "mla_q_norm", "mla_kv_norm", "mla_w_uq", "mla_w_ukv", "mla_w_o",
           "hgrn_w_in", "hgrn_lb_logits", "hgrn_o_norm", "hgrn_w_o", "mlp_w1", "mlp_w2")


def _cparams(*semantics):
    return pltpu.CompilerParams(dimension_semantics=semantics, vmem_limit_bytes=V7X_VMEM_LIMIT)


def _sigmoid(x):
    return 0.5 * jnp.tanh(0.5 * x) + 0.5


def _mm(a, b, *, ta=False, tb=False, out_dtype=F32, tm=2048, tn=1024, tk=1024, epi=None, extra=None,
        name="mm", n=None, b_map=None, into=None, o_map=None):
    if ta:
        K, M = a.shape
    else:
        M, K = a.shape
    if b_map is not None:
        N = n
    elif tb:
        N, Kb = b.shape
    else:
        Kb, N = b.shape
    assert b_map is not None or K == Kb, (a.shape, b.shape, ta, tb)
    tm, tn = min(tm, M), min(tn, N)
    if ta and b_map is None:
        tk = max(tk, 4096)
    tk = K if (K <= 1024 and b_map is None) else min(tk, K)
    assert M % tm == 0 and N % tn == 0 and K % tk == 0, (M, N, K, tm, tn, tk)
    nk = K // tk
    a_spec = (pl.BlockSpec((tk, tm), lambda i, j, k: (k, i)) if ta
              else pl.BlockSpec((tm, tk), lambda i, j, k: (i, k)))
    if b_map is None:
        b_map = (lambda i, j, k: (j, k)) if tb else (lambda i, j, k: (k, j))
    b_spec = pl.BlockSpec((tn, tk) if tb else (tk, tn), b_map)
    o_spec = pl.BlockSpec((tm, tn), lambda i, j, k: (i, j))
    dims = (((0 if ta else 1,), (1 if tb else 0,)), ((), ()))
    in_specs = [a_spec, b_spec]
    operands = [a, b]
    aliases = {}
    if epi == "mul2r":
        in_specs.append(o_spec)
        operands.append(extra)
    if epi == "resnorm":
        assert tn == N
        vec = pl.BlockSpec((1, N), lambda i, j, k: (0, 0))
        in_specs += [o_spec, vec, vec]
        operands += list(extra)
    if into is not None:
        assert epi is None
        in_specs.append(pl.BlockSpec(memory_space=pl.ANY))
        operands.append(into)
        aliases = {2: 0}
        out_dtype = into.dtype
        out_shape = jax.ShapeDtypeStruct(into.shape, into.dtype)
        out_specs = pl.BlockSpec((tm, tn), o_map)
    elif epi == "relu2":
        out_shape = (jax.ShapeDtypeStruct((M, N), BF16), jax.ShapeDtypeStruct((M, N), BF16))
        out_specs = (o_spec, o_spec)
    elif epi == "mul2r":
        out_shape = jax.ShapeDtypeStruct((M, N), BF16)
        out_specs = o_spec
    elif epi == "resnorm":
        out_shape = (jax.ShapeDtypeStruct((M, N), F32), jax.ShapeDtypeStruct((M, N), F32),
                     jax.ShapeDtypeStruct((M, N), BF16))
        out_specs = (o_spec, o_spec, o_spec)
    else:
        out_shape = jax.ShapeDtypeStruct((M, N), out_dtype)
        out_specs = o_spec
    n_in = len(operands)
    n_out = {"relu2": 2, "resnorm": 3}.get(epi, 1)

    def body(*refs):
        a_ref, b_ref = refs[0], refs[1]
        outs = refs[n_in:n_in + n_out]
        k = pl.program_id(2)

        def finish(acc):
            if epi == "relu2":
                r = jnp.maximum(acc, 0.0)
                outs[0][...] = (r * r).astype(BF16)
                outs[1][...] = r.astype(BF16)
            elif epi == "mul2r":
                outs[0][...] = (acc * (2.0 * refs[2][...].astype(F32))).astype(BF16)
            elif epi == "resnorm":
                h_ref, gp_ref, gn_ref = refs[2], refs[3], refs[4]
                hn = h_ref[...] + acc * _rms_rstd(acc) * gp_ref[...]
                outs[0][...] = acc
                outs[1][...] = hn
                outs[2][...] = (hn * _rms_rstd(hn) * gn_ref[...]).astype(BF16)
            else:
                outs[0][...] = acc.astype(out_dtype)

        part = lax.dot_general(a_ref[...], b_ref[...], dims, preferred_element_type=F32)
        if nk == 1:
            finish(part)
            return
        acc_ref = refs[-1]

        @pl.when(k == 0)
        def _():
            acc_ref[...] = part

        @pl.when((k > 0) & (k < nk - 1))
        def _():
            acc_ref[...] += part

        @pl.when(k == nk - 1)
        def _():
            finish(acc_ref[...] + part)

    return pl.pallas_call(
        body, name=name, grid=(M // tm, N // tn, nk), in_specs=in_specs, out_specs=out_specs,
        out_shape=out_shape, scratch_shapes=[pltpu.VMEM((tm, tn), F32)] if nk > 1 else [],
        input_output_aliases=aliases,
        compiler_params=_cparams("parallel", "parallel", "arbitrary"))(*operands)


def _rms_rstd(x):
    return lax.rsqrt(jnp.mean(x * x, axis=-1, keepdims=True) + EPS)


def _rms_bwd_tile(x, g, dy):
    r = _rms_rstd(x)
    xh = x * r
    u = dy * g
    dx = r * (u - xh * jnp.mean(u * xh, axis=-1, keepdims=True))
    dg = jnp.sum(dy * xh, axis=0, keepdims=True)
    return dx, dg


def _row_tile(T):
    return min(512, T)


def _mid_tile(T):
    return min(256, T)


def _prenorm_fwd(x, g, name="prenorm_fwd"):
    T, D = x.shape
    tm = _row_tile(T)

    def body(x_ref, g_ref, a_ref):
        xv = x_ref[...]
        a_ref[...] = (xv * _rms_rstd(xv) * g_ref[...]).astype(BF16)

    row = pl.BlockSpec((tm, D), lambda i: (i, 0))
    vec = pl.BlockSpec((1, D), lambda i: (0, 0))
    return pl.pallas_call(body, name=name, grid=(T // tm,), in_specs=[row, vec], out_specs=row,
                          out_shape=jax.ShapeDtypeStruct((T, D), BF16),
                          compiler_params=_cparams("parallel"))(x, g)


def _resnorm_loss(h, z, g_post, target, name="resnorm_loss"):
    T, D = h.shape
    tm = _row_tile(T)

    def body(h_ref, z_ref, gp_ref, t_ref, dy_ref, sq_ref):
        zv = z_ref[...]
        err = h_ref[...] + zv * _rms_rstd(zv) * gp_ref[...] - t_ref[...]
        dy_ref[...] = err * (1.0 / D)

        @pl.when(pl.program_id(0) == 0)
        def _():
            sq_ref[...] = jnp.zeros_like(sq_ref)

        sq_ref[...] += jnp.sum(err * err, axis=0, keepdims=True)

    row = pl.BlockSpec((tm, D), lambda i: (i, 0))
    vec = pl.BlockSpec((1, D), lambda i: (0, 0))
    return pl.pallas_call(body, name=name, grid=(T // tm,), in_specs=[row, row, vec, row],
                          out_specs=(row, vec),
                          out_shape=(jax.ShapeDtypeStruct((T, D), F32), jax.ShapeDtypeStruct((1, D), F32)),
                          compiler_params=_cparams("arbitrary"))(h, z, g_post, target)


def _resnorm_bwd(z, g_post, dh, h_new=None, da=None, g_pre=None, name="resnorm_bwd"):
    T, D = z.shape
    tm = _row_tile(T)
    has_next = h_new is not None
    row = pl.BlockSpec((tm, D), lambda i: (i, 0))
    vec = pl.BlockSpec((1, D), lambda i: (0, 0))

    if has_next:
        def body(z_ref, gp_ref, dh_ref, hn_ref, da_ref, gn_ref, t_ref, dz_ref, dgp_ref, dgn_ref):
            first = pl.program_id(0) == 0

            @pl.when(first)
            def _():
                dgp_ref[...] = jnp.zeros_like(dgp_ref)
                dgn_ref[...] = jnp.zeros_like(dgn_ref)

            dpre, dgn = _rms_bwd_tile(hn_ref[...], gn_ref[...], da_ref[...])
            t = dh_ref[...] + dpre
            t_ref[...] = t
            dz, dgp = _rms_bwd_tile(z_ref[...], gp_ref[...], t)
            dz_ref[...] = dz.astype(BF16)
            dgp_ref[...] += dgp
            dgn_ref[...] += dgn

        return pl.pallas_call(
            body, name=name, grid=(T // tm,), in_specs=[row, vec, row, row, row, vec],
            out_specs=(row, row, vec, vec),
            out_shape=(jax.ShapeDtypeStruct((T, D), F32), jax.ShapeDtypeStruct((T, D), BF16),
                       jax.ShapeDtypeStruct((1, D), F32), jax.ShapeDtypeStruct((1, D), F32)),
            compiler_params=_cparams("arbitrary"))(z, g_post, dh, h_new, da, g_pre)

    def body_last(z_ref, gp_ref, dh_ref, dz_ref, dgp_ref):
        @pl.when(pl.program_id(0) == 0)
        def _():
            dgp_ref[...] = jnp.zeros_like(dgp_ref)

        dz, dgp = _rms_bwd_tile(z_ref[...], gp_ref[...], dh_ref[...])
        dz_ref[...] = dz.astype(BF16)
        dgp_ref[...] += dgp

    return pl.pallas_call(
        body_last, name=name, grid=(T // tm,), in_specs=[row, vec, row], out_specs=(row, vec),
        out_shape=(jax.ShapeDtypeStruct((T, D), BF16), jax.ShapeDtypeStruct((1, D), F32)),
        compiler_params=_cparams("arbitrary"))(z, g_post, dh)


def _prenorm_bwd(x, g, dh, da, name="prenorm_bwd"):
    T, D = x.shape
    tm = _row_tile(T)

    def body(x_ref, g_ref, dh_ref, da_ref, dx_ref, dg_ref):
        @pl.when(pl.program_id(0) == 0)
        def _():
            dg_ref[...] = jnp.zeros_like(dg_ref)

        dpre, dg = _rms_bwd_tile(x_ref[...], g_ref[...], da_ref[...])
        dx_ref[...] = dh_ref[...] + dpre
        dg_ref[...] += dg

    row = pl.BlockSpec((tm, D), lambda i: (i, 0))
    vec = pl.BlockSpec((1, D), lambda i: (0, 0))
    return pl.pallas_call(
        body, name=name, grid=(T // tm,), in_specs=[row, vec, row, row], out_specs=(row, vec),
        out_shape=(jax.ShapeDtypeStruct((T, D), F32), jax.ShapeDtypeStruct((1, D), F32)),
        compiler_params=_cparams("arbitrary"))(x, g, dh, da)


class _Packed:
    def __init__(self, big, misc_rows):
        self.big = tuple(big)
        self.off = {}
        r = 0
        for name, rows in big:
            self.off[name] = r
            r += rows
        self.misc, self.misc_rows = r, misc_rows
        self.rows = -(-(r + misc_rows) // PACK_ALIGN) * PACK_ALIGN

    def block(self, name, layer, unit):
        r = self.off[name]
        assert r % unit == 0 and self.rows % unit == 0
        return r // unit, self.rows // unit


def _col_sharded(pk, name, layer, unit):
    base, stride = pk.block(name, layer, unit)
    return (lambda i, j, k: (j * stride + base, 0)), (lambda i, j, k: (k * stride + base, 0))


def _row_sharded(pk, name, layer, unit):
    base, stride = pk.block(name, layer, unit)
    return ((lambda i, j, k: (k * stride + base, 0)), (lambda i, j, k: (j * stride + base, 0)),
            (lambda i, j, k: (i * stride + base, 0)))


def _mlp_fwd(a, wbuf, pk, layer, res):
    D = a.shape[1]
    by_n, _ = _col_sharded(pk, "mlp_w1", layer, D)
    by_k, _, _ = _row_sharded(pk, "mlp_w2", layer, D)
    act, r = _mm(a, wbuf, n=4 * D, b_map=by_n, tk=D, tn=D, epi="relu2", name="mlp_up")
    if res is None:
        return _mm(act, wbuf, n=D, b_map=by_k, tk=D, tn=D, name="mlp_down"), (a, act, r), None, None
    u, h_new, a_next = _mm(act, wbuf, n=D, b_map=by_k, tm=1024, tk=D, tn=D, epi="resnorm", extra=res,
                           name="mlp_down_res")
    return u, (a, act, r), h_new, a_next


def _mlp_bwd(du, saved, wbuf, gbuf, pk, layer):
    a, act, r = saved
    D = a.shape[1]
    w1_by_n, w1_by_k = _col_sharded(pk, "mlp_w1", layer, D)
    _, w2_by_n, w2_by_m = _row_sharded(pk, "mlp_w2", layer, D)
    dz1 = _mm(du, wbuf, tb=True, n=4 * D, b_map=w2_by_n, tn=D, tk=D, epi="mul2r", extra=r, name="mlp_down_dx")
    gbuf = _mm(act, du, ta=True, into=gbuf, o_map=w2_by_m, tm=D, tn=D, name="mlp_down_dw")
    gbuf = _mm(a, dz1, ta=True, into=gbuf, o_map=w1_by_n, tm=D, tn=D, name="mlp_up_dw")
    da = _mm(dz1, wbuf, tb=True, n=D, b_map=w1_by_k, tn=D, tk=D, name="mlp_up_dx")
    return da, gbuf


def _rope_swap(t):
    n = t.shape[-1]
    lane = lax.broadcasted_iota(jnp.int32, t.shape, t.ndim - 1)
    half = MLA_ROPE // 2
    first = (lane & (MLA_ROPE - 1)) < half
    return jnp.where(first, pltpu.roll(t, n - half, t.ndim - 1), pltpu.roll(t, half, t.ndim - 1))


def _mla_mid_fwd(proj, q_norm, kv_norm, w_uq, w_ukv, cc, ss):
    T, PW = proj.shape
    QL, KVL = q_norm.shape[-1], kv_norm.shape[-1]
    H = MLA_HEADS
    assert PW == QL + KVL + 128
    tm = _mid_tile(T)

    def body(p_ref, qn_ref, kn_ref, wq_ref, wkv_ref, cc_ref, ss_ref,
             cq_ref, ckv_ref, q_ref, k_ref, v_ref):
        cq = p_ref[:, 0:QL]
        ckv = p_ref[:, QL:QL + KVL]
        kr = p_ref[:, QL + KVL:QL + KVL + 128]
        c, s = cc_ref[...], ss_ref[...]
        cqn = (cq * _rms_rstd(cq) * qn_ref[...]).astype(BF16)
        ckvn = (ckv * _rms_rstd(ckv) * kn_ref[...]).astype(BF16)
        cq_ref[...] = cqn
        ckv_ref[...] = ckvn
        q = jnp.dot(cqn, wq_ref[...], preferred_element_type=F32)
        kv = jnp.dot(ckvn, wkv_ref[...], preferred_element_type=F32)
        krf = (kr * c + _rope_swap(kr) * s).astype(BF16)
        for h in range(H):
            o = h * MLA_QK_PAD
            q_ref[:, o:o + MLA_NOPE] = (q[:, o:o + MLA_NOPE] * MLA_SCALE).astype(BF16)
            qr = q[:, o + MLA_NOPE:o + MLA_QK_PAD]
            q_ref[:, o + MLA_NOPE:o + MLA_QK_PAD] = ((qr * c + _rope_swap(qr) * s) * MLA_SCALE).astype(BF16)
            k_ref[:, o:o + MLA_NOPE] = kv[:, o:o + MLA_NOPE].astype(BF16)
            k_ref[:, o + MLA_NOPE:o + MLA_QK_PAD] = krf
            v_ref[:, h * MLA_V:(h + 1) * MLA_V] = kv[:, o + MLA_NOPE:o + MLA_QK_PAD].astype(BF16)

    def row(w):
        return pl.BlockSpec((tm, w), lambda i: (i, 0))

    def full(shape):
        return pl.BlockSpec(shape, lambda i: (0, 0))

    return pl.pallas_call(
        body, name="mla_mid_fwd", grid=(T // tm,),
        in_specs=[row(PW), full((1, QL)), full((1, KVL)), full(w_uq.shape), full(w_ukv.shape),
                  row(128), row(128)],
        out_specs=(row(QL), row(KVL), row(H * MLA_QK_PAD), row(H * MLA_QK_PAD), row(H * MLA_V)),
        out_shape=(jax.ShapeDtypeStruct((T, QL), BF16), jax.ShapeDtypeStruct((T, KVL), BF16),
                   jax.ShapeDtypeStruct((T, H * MLA_QK_PAD), BF16),
                   jax.ShapeDtypeStruct((T, H * MLA_QK_PAD), BF16),
                   jax.ShapeDtypeStruct((T, H * MLA_V), BF16)),
        compiler_params=_cparams("parallel"))(proj, q_norm, kv_norm, w_uq, w_ukv, cc, ss)


def _mla_mid_bwd(proj, q_norm, kv_norm, w_uq, w_ukv, cc, ss, dq, dk, dv):
    T, PW = proj.shape
    QL, KVL = q_norm.shape[-1], kv_norm.shape[-1]
    H = MLA_HEADS
    tm = _mid_tile(T)
    nt = (((1,), (1,)), ((), ()))

    def body(p_ref, qn_ref, kn_ref, wq_ref, wkv_ref, cc_ref, ss_ref, dq_ref, dk_ref, dv_ref,
             dqp_ref, dkv_ref, dp_ref, dqn_ref, dkn_ref):
        @pl.when(pl.program_id(0) == 0)
        def _():
            dqn_ref[...] = jnp.zeros_like(dqn_ref)
            dkn_ref[...] = jnp.zeros_like(dkn_ref)

        c, s = cc_ref[...], ss_ref[...]
        dkr = jnp.zeros((tm, 128), F32)
        for h in range(H):
            o = h * MLA_QK_PAD
            dqp_ref[:, o:o + MLA_NOPE] = (dq_ref[:, o:o + MLA_NOPE] * MLA_SCALE).astype(BF16)
            dqr = dq_ref[:, o + MLA_NOPE:o + MLA_QK_PAD] * MLA_SCALE
            dqp_ref[:, o + MLA_NOPE:o + MLA_QK_PAD] = (dqr * c + _rope_swap(dqr * s)).astype(BF16)
            dkv_ref[:, o:o + MLA_NOPE] = dk_ref[:, o:o + MLA_NOPE].astype(BF16)
            dkv_ref[:, o + MLA_NOPE:o + MLA_QK_PAD] = dv_ref[:, h * MLA_V:(h + 1) * MLA_V].astype(BF16)
            dkr = dkr + dk_ref[:, o + MLA_NOPE:o + MLA_QK_PAD]
        dcqn = lax.dot_general(dqp_ref[...], wq_ref[...], nt, preferred_element_type=F32)
        dckvn = lax.dot_general(dkv_ref[...], wkv_ref[...], nt, preferred_element_type=F32)
        dcq, dqn = _rms_bwd_tile(p_ref[:, 0:QL], qn_ref[...], dcqn)
        dckv, dkn = _rms_bwd_tile(p_ref[:, QL:QL + KVL], kn_ref[...], dckvn)
        dp_ref[:, 0:QL] = dcq.astype(BF16)
        dp_ref[:, QL:QL + KVL] = dckv.astype(BF16)
        dp_ref[:, QL + KVL:QL + KVL + 128] = (dkr * c + _rope_swap(dkr * s)).astype(BF16)
        dqn_ref[...] += dqn
        dkn_ref[...] += dkn

    def row(w):
        return pl.BlockSpec((tm, w), lambda i: (i, 0))

    def full(shape):
        return pl.BlockSpec(shape, lambda i: (0, 0))

    return pl.pallas_call(
        body, name="mla_mid_bwd", grid=(T // tm,),
        in_specs=[row(PW), full((1, QL)), full((1, KVL)), full(w_uq.shape), full(w_ukv.shape),
                  row(128), row(128), row(H * MLA_QK_PAD), row(H * MLA_QK_PAD), row(H * MLA_V)],
        out_specs=(row(H * MLA_QK_PAD), row(H * MLA_QK_PAD), row(PW), full((1, QL)), full((1, KVL))),
        out_shape=(jax.ShapeDtypeStruct((T, H * MLA_QK_PAD), BF16),
                   jax.ShapeDtypeStruct((T, H * MLA_QK_PAD), BF16),
                   jax.ShapeDtypeStruct((T, PW), BF16),
                   jax.ShapeDtypeStruct((1, QL), F32), jax.ShapeDtypeStruct((1, KVL), F32)),
        compiler_params=_cparams("arbitrary"))(proj, q_norm, kv_norm, w_uq, w_ukv, cc, ss, dq, dk, dv)


def _attn_tile(T):
    return min(1024, T)


def _attn_pairs(n, by_key):
    if by_key:
        pairs = [(qi, ki) for ki in range(n) for qi in range(ki, n)]
    else:
        pairs = [(qi, ki) for qi in range(n) for ki in range(qi + 1)]
    return (jnp.asarray([p[0] for p in pairs], jnp.int32), jnp.asarray([p[1] for p in pairs], jnp.int32))


def _scores(q, k, diagonal):
    s = lax.dot_general(q, k, (((1,), (1,)), ((), ())), preferred_element_type=F32)
    if diagonal:
        rows = lax.broadcasted_iota(jnp.int32, s.shape, 0)
        cols = lax.broadcasted_iota(jnp.int32, s.shape, 1)
        s = jnp.where(rows >= cols, s, -jnp.inf)
    return s


def _attn_fwd(q, k, v):
    T = q.shape[0]
    H, DQ, DV = MLA_HEADS, MLA_QK_PAD, MLA_V
    tq = _attn_tile(T)
    nq = T // tq
    G = MLA_HEADS_PER_STEP
    qi_tab, ki_tab = _attn_pairs(nq, by_key=False)

    def body(qi_ref, ki_ref, q_ref, k_ref, v_ref, o_ref, lse_ref, *scratch):
        m_refs, l_refs, acc_refs = scratch[0:G], scratch[G:2 * G], scratch[2 * G:3 * G]
        p = pl.program_id(1)
        qi, ki = qi_ref[p], ki_ref[p]

        @pl.when(ki == 0)
        def _():
            for g in range(G):
                m_refs[g][...] = jnp.full_like(m_refs[g], -jnp.inf)
                l_refs[g][...] = jnp.zeros_like(l_refs[g])
                acc_refs[g][...] = jnp.zeros_like(acc_refs[g])

        def update(ks, qr, masked):
            for g in range(G):
                qs, vs = slice(g * DQ, (g + 1) * DQ), slice(g * DV, (g + 1) * DV)
                st = _scores(k_ref[ks, qs], q_ref[qr, qs], False)
                if masked:
                    key = ks.start + lax.broadcasted_iota(jnp.int32, st.shape, 0)
                    qry = qr.start + lax.broadcasted_iota(jnp.int32, st.shape, 1)
                    st = jnp.where(qry >= key, st, -jnp.inf)
                m_prev = m_refs[g][:, qr]
                m_new = jnp.maximum(m_prev, jnp.max(st, axis=0, keepdims=True))
                alpha = jnp.exp(m_prev - m_new)
                pt = jnp.exp(st - m_new)
                l_refs[g][:, qr] = alpha * l_refs[g][:, qr] + jnp.sum(pt, axis=0, keepdims=True)
                acc_refs[g][:, qr] = alpha * acc_refs[g][:, qr] + lax.dot_general(
                    v_ref[ks, vs], pt.astype(BF16), (((0,), (0,)), ((), ())), preferred_element_type=F32)
                m_refs[g][:, qr] = m_new

        whole, half = slice(0, tq), tq // 2

        @pl.when(ki < qi)
        def _():
            update(whole, whole, False)

        @pl.when(ki == qi)
        def _():
            update(slice(0, half), whole, True)
            update(slice(half, tq), slice(half, tq), True)
            for g in range(G):
                vs = slice(g * DV, (g + 1) * DV)
                o_ref[:, vs] = jnp.transpose(acc_refs[g][...] / l_refs[g][...]).astype(BF16)
                lse_ref[g] = m_refs[g][...] + jnp.log(l_refs[g][...])

    return pl.pallas_call(
        body, name="attn_fwd",
        grid_spec=pltpu.PrefetchScalarGridSpec(
            num_scalar_prefetch=2, grid=(H // G, int(qi_tab.shape[0])),
            in_specs=[pl.BlockSpec((tq, G * DQ), lambda h, p, qt, kt: (qt[p], h)),
                      pl.BlockSpec((tq, G * DQ), lambda h, p, qt, kt: (kt[p], h)),
                      pl.BlockSpec((tq, G * DV), lambda h, p, qt, kt: (kt[p], h))],
            out_specs=(pl.BlockSpec((tq, G * DV), lambda h, p, qt, kt: (qt[p], h)),
                       pl.BlockSpec((G, 1, tq), lambda h, p, qt, kt: (h, 0, qt[p]))),
            scratch_shapes=([pltpu.VMEM((1, tq), F32)] * (2 * G) + [pltpu.VMEM((DV, tq), F32)] * G)),
        out_shape=(jax.ShapeDtypeStruct((T, H * DV), BF16), jax.ShapeDtypeStruct((H, 1, T), F32)),
        compiler_params=_cparams("parallel", "arbitrary"))(qi_tab, ki_tab, q, k, v)


def _attn_bwd(q, k, v, o, do, lse):
    T = q.shape[0]
    H, DQ, DV = MLA_HEADS, MLA_QK_PAD, MLA_V
    tq = _attn_tile(T)
    nq = T // tq
    tn = (((0,), (0,)), ((), ()))
    nt = (((1,), (1,)), ((), ()))
    G = MLA_HEADS_PER_STEP
    qi_tab, ki_tab = _attn_pairs(nq, by_key=True)

    def body(qi_ref, ki_ref, q_ref, k_ref, v_ref, o_ref, do_ref, lse_ref, dq_ref, dk_ref, dv_ref,
             dk_acc, dv_acc):
        p = pl.program_id(1)
        qi, ki = qi_ref[p], ki_ref[p]

        @pl.when(p == 0)
        def _():
            dq_ref[...] = jnp.zeros_like(dq_ref)

        @pl.when(qi == ki)
        def _():
            dk_acc[...] = jnp.zeros_like(dk_acc)
            dv_acc[...] = jnp.zeros_like(dv_acc)

        def step(ks, qr, masked):
            rows = pl.ds(pl.multiple_of(qi * tq + qr.start, qr.stop - qr.start), qr.stop - qr.start)
            for g in range(G):
                qs, vs = slice(g * DQ, (g + 1) * DQ), slice(g * DV, (g + 1) * DV)
                dof = do_ref[qr, vs]
                delta = jnp.sum(jnp.transpose(dof.astype(F32) * o_ref[qr, vs].astype(F32)), axis=0,
                                keepdims=True)
                st = _scores(k_ref[ks, qs], q_ref[qr, qs], False)
                if masked:
                    key = ks.start + lax.broadcasted_iota(jnp.int32, st.shape, 0)
                    qry = qr.start + lax.broadcasted_iota(jnp.int32, st.shape, 1)
                    st = jnp.where(qry >= key, st, -jnp.inf)
                pt = jnp.exp(st - lse_ref[g][:, qr])
                dpt = lax.dot_general(v_ref[ks, vs], dof, nt, preferred_element_type=F32)
                dst = (pt * (dpt - delta)).astype(BF16)
                dv_acc[ks, vs] += jnp.dot(pt.astype(BF16), dof, preferred_element_type=F32)
                dk_acc[ks, qs] += jnp.dot(dst, q_ref[qr, qs], preferred_element_type=F32)
                dq_ref[rows, qs] += lax.dot_general(dst, k_ref[ks, qs], tn, preferred_element_type=F32)

        whole, half = slice(0, tq), tq // 2

        @pl.when(qi == ki)
        def _():
            step(slice(0, half), whole, True)
            step(slice(half, tq), slice(half, tq), True)

        @pl.when(qi > ki)
        def _():
            step(whole, whole, False)

        @pl.when(qi == nq - 1)
        def _():
            dk_ref[...] = dk_acc[...]
            dv_ref[...] = dv_acc[...]

    qspec = pl.BlockSpec((tq, G * DQ), lambda h, p, qt, kt: (qt[p], h))
    ospec = pl.BlockSpec((tq, G * DV), lambda h, p, qt, kt: (qt[p], h))
    kspec = pl.BlockSpec((tq, G * DQ), lambda h, p, qt, kt: (kt[p], h))
    vspec = pl.BlockSpec((tq, G * DV), lambda h, p, qt, kt: (kt[p], h))
    return pl.pallas_call(
        body, name="attn_bwd",
        grid_spec=pltpu.PrefetchScalarGridSpec(
            num_scalar_prefetch=2, grid=(H // G, int(qi_tab.shape[0])),
            in_specs=[qspec, kspec, vspec, ospec, ospec,
                      pl.BlockSpec((G, 1, tq), lambda h, p, qt, kt: (h, 0, qt[p]))],
            out_specs=(pl.BlockSpec((T, G * DQ), lambda h, p, qt, kt: (0, h)), kspec, vspec),
            scratch_shapes=[pltpu.VMEM((tq, G * DQ), F32), pltpu.VMEM((tq, G * DV), F32)]),
        out_shape=(jax.ShapeDtypeStruct((T, H * DQ), F32), jax.ShapeDtypeStruct((T, H * DQ), F32),
                   jax.ShapeDtypeStruct((T, H * DV), F32)),
        compiler_params=_cparams("parallel", "arbitrary"))(qi_tab, ki_tab, q, k, v, o, do, lse)


def _mla_fwd(a, w, cc, ss, wbuf, pk, slot, res):
    D = a.shape[1]
    by_k, _, _ = _row_sharded(pk, "mla_w_o", slot, D // N_CHIPS)
    proj = _mm(a, w["w_in"], name="mla_in")
    cqn, ckvn, q, k, v = _mla_mid_fwd(proj, w["q_norm"], w["kv_norm"], w["w_uq"], w["w_ukv"], cc, ss)
    o, lse = _attn_fwd(q, k, v)
    m, h_new, a_next = _mm(o, wbuf, n=D, b_map=by_k, tm=1024, tk=D // N_CHIPS, tn=D, epi="resnorm", extra=res,
                           name="mla_out_res")
    return m, (a, proj, cqn, ckvn, q, k, v, o, lse), h_new, a_next


def _mla_bwd(dm, saved, w, cc, ss, wbuf, gbuf, pk, slot):
    a, proj, cqn, ckvn, q, k, v, o, lse = saved
    D = a.shape[1]
    _, by_n, by_m = _row_sharded(pk, "mla_w_o", slot, D // N_CHIPS)
    do = _mm(dm, wbuf, tb=True, n=o.shape[1], b_map=by_n, tm=2048, tn=D // N_CHIPS, tk=D, out_dtype=BF16,
             name="mla_out_dx")
    gbuf = _mm(o, dm, ta=True, into=gbuf, o_map=by_m, tm=D // N_CHIPS, tn=D, tk=2048, name="mla_out_dw")
    dq, dk, dv = _attn_bwd(q, k, v, o, do, lse)
    dqp, dkv, dproj, dqn, dkn = _mla_mid_bwd(proj, w["q_norm"], w["kv_norm"], w["w_uq"], w["w_ukv"],
                                             cc, ss, dq, dk, dv)
    dw_uq = _mm(cqn, dqp, ta=True, out_dtype=BF16, name="mla_uq_dw")
    dw_ukv = _mm(ckvn, dkv, ta=True, out_dtype=BF16, name="mla_ukv_dw")
    dw_in = _mm(a, dproj, ta=True, out_dtype=BF16, name="mla_in_dw")
    da = _mm(dproj, w["w_in"], tb=True, name="mla_in_dx")
    return da, gbuf, dict(w_in=dw_in, w_uq=dw_uq, w_ukv=dw_ukv, q_norm=dqn, kv_norm=dkn)


def _split_dot(mat, x, parts):
    acc = None
    rem = x
    for _ in range(parts):
        piece = rem.astype(BF16)
        term = jnp.dot(mat, piece, preferred_element_type=F32)
        acc = term if acc is None else acc + term
        rem = rem - piece.astype(F32)
    return acc


def _chunk_sums(cum, rel, rest, logf):
    return tuple(_split_dot(m.astype(BF16), logf, 3) for m in (cum, rel, rest))


def _chunk_mats(tb):
    C = HGRN_CHUNK
    assert C & (C - 1) == 0
    r = lax.broadcasted_iota(jnp.int32, (tb, tb), 0)
    s = lax.broadcasted_iota(jnp.int32, (tb, tb), 1)
    start = r & ~(C - 1)
    same = start == (s & ~(C - 1))
    ref = start + C // 2
    last = start + C - 1
    one, zero = jnp.float32(1.0), jnp.float32(0.0)
    cum = jnp.where(same & (s <= r), one, zero)
    rel = cum - jnp.where(same & (s <= ref), one, zero)
    rest = jnp.where(same & (s > r) & (s <= last), one, zero)
    rev = jnp.where(same & (s >= r), one, zero)
    ones = jnp.where(same, one, zero)
    causal = same & (s <= r)
    return cum, rel, rest, rev, ones, causal


def _hgrn_gates(p_ref, lb, HK):
    qx = p_ref[:, 0:HK]
    fx = p_ref[:, HK:2 * HK]
    sf = _sigmoid(fx)
    f = lb + (1.0 - lb) * sf
    sq = _sigmoid(qx)
    return qx, sq, qx * sq, sf, f, 1.0 - f, jnp.log(f)


def _hgrn_fwd(proj, lb, o_norm):
    T = proj.shape[0]
    H, C = HGRN_HEADS, HGRN_CHUNK
    HK = proj.shape[1] // 4
    DK = HK // H
    tb = min(HGRN_BLOCK, T)
    ncb = tb // C
    nt = (((1,), (1,)), ((), ()))
    tn = (((0,), (0,)), ((), ()))

    def body(p_ref, lb_ref, on_ref, y_ref, o_ref, st_ref, state, oacc):
        @pl.when(pl.program_id(0) == 0)
        def _():
            state[...] = jnp.zeros_like(state)

        cum, rel, rest, _, _, causal = _chunk_mats(tb)
        _, _, q, _, f, k, logf = _hgrn_gates(p_ref, lb_ref[...], HK)
        b, brel, brest = _chunk_sums(cum, rel, rest, logf)
        eb = jnp.exp(b)
        q_rel = (q * jnp.exp(brel)).astype(BF16)
        k_rel = (k * jnp.exp(-brel)).astype(BF16)
        q_dec = (q * eb).astype(BF16)
        k_dec = (k * jnp.exp(brest)).astype(BF16)
        v = p_ref[:, 2 * HK:3 * HK].astype(BF16)
        for h in range(H):
            hs = slice(h * DK, (h + 1) * DK)
            a = lax.dot_general(q_rel[:, hs], k_rel[:, hs], nt, preferred_element_type=F32)
            a = jnp.where(causal, a, 0.0).astype(BF16)
            oacc[:, hs] = jnp.dot(a, v[:, hs], preferred_element_type=F32)
            for j in range(ncb):
                rs = slice(j * C, (j + 1) * C)
                st = state[h]
                st_ref[j, h] = st
                oacc[rs, hs] += lax.dot_general(q_dec[rs, hs], st.astype(BF16), nt,
                                                preferred_element_type=F32)
                dec = jnp.exp(jnp.sum(logf[rs, hs], axis=0, keepdims=True))
                state[h] = dec * st + lax.dot_general(v[rs, hs], k_dec[rs, hs], tn,
                                                      preferred_element_type=F32)
        o = oacc[...]
        o_ref[...] = o
        gx = p_ref[:, 3 * HK:4 * HK]
        gate = gx * _sigmoid(gx)
        for h in range(H):
            hs = slice(h * DK, (h + 1) * DK)
            oh = o[:, hs]
            y_ref[:, hs] = (oh * _rms_rstd(oh) * on_ref[...] * gate[:, hs]).astype(BF16)

    return pl.pallas_call(
        body, name="hgrn_fwd", grid=(T // tb,),
        in_specs=[pl.BlockSpec((tb, 4 * HK), lambda i: (i, 0)),
                  pl.BlockSpec((1, HK), lambda i: (0, 0)),
                  pl.BlockSpec((1, DK), lambda i: (0, 0))],
        out_specs=(pl.BlockSpec((tb, HK), lambda i: (i, 0)),
                   pl.BlockSpec((tb, HK), lambda i: (i, 0)),
                   pl.BlockSpec((ncb, H, DK, DK), lambda i: (i, 0, 0, 0))),
        out_shape=(jax.ShapeDtypeStruct((T, HK), BF16), jax.ShapeDtypeStruct((T, HK), F32),
                   jax.ShapeDtypeStruct((T // C, H, DK, DK), F32)),
        scratch_shapes=[pltpu.VMEM((H, DK, DK), F32), pltpu.VMEM((tb, HK), F32)],
        compiler_params=_cparams("arbitrary"))(proj, lb, o_norm)


def _hgrn_bwd(proj, lb, o_norm, o, states, dy):
    T = proj.shape[0]
    H, C = HGRN_HEADS, HGRN_CHUNK
    HK = proj.shape[1] // 4
    DK = HK // H
    tb = min(HGRN_BLOCK, T)
    ncb = tb // C
    nb = T // tb
    nt = (((1,), (1,)), ((), ()))
    tn = (((0,), (0,)), ((), ()))

    def body(p_ref, lb_ref, on_ref, o_ref, st_ref, dy_ref, dp_ref, dlb_ref, don_ref,
             dstate, dqr_s, dkr_s, dqd_s, dkd_s, dv_s, do_s, e_s):
        @pl.when(pl.program_id(0) == 0)
        def _():
            dstate[...] = jnp.zeros_like(dstate)
            dlb_ref[...] = jnp.zeros_like(dlb_ref)
            don_ref[...] = jnp.zeros_like(don_ref)

        cum, rel, rest, rev, ones, causal = _chunk_mats(tb)
        lb = lb_ref[...]
        qx, sq, q, sf, f, k, logf = _hgrn_gates(p_ref, lb, HK)
        b, brel, brest = _chunk_sums(cum, rel, rest, logf)
        eb = jnp.exp(b)
        erel = jnp.exp(brel)
        enrel = jnp.exp(-brel)
        erest = jnp.exp(brest)
        q_rel_f, k_rel_f, q_dec_f, k_dec_f = q * erel, k * enrel, q * eb, k * erest
        q_rel, k_rel = q_rel_f.astype(BF16), k_rel_f.astype(BF16)
        q_dec, k_dec = q_dec_f.astype(BF16), k_dec_f.astype(BF16)
        v = p_ref[:, 2 * HK:3 * HK].astype(BF16)

        gx = p_ref[:, 3 * HK:4 * HK]
        sg = _sigmoid(gx)
        gate = gx * sg
        dy = dy_ref[...]
        ov = o_ref[...]
        on = on_ref[...]
        don = jnp.zeros((1, DK), F32)
        for h in range(H):
            hs = slice(h * DK, (h + 1) * DK)
            oh = ov[:, hs]
            r = _rms_rstd(oh)
            xh = oh * r
            d_on = dy[:, hs] * gate[:, hs]
            don = don + jnp.sum(d_on * xh, axis=0, keepdims=True)
            u = d_on * on
            do_s[:, hs] = r * (u - xh * jnp.mean(u * xh, axis=-1, keepdims=True))
            dp_ref[:, 3 * HK + h * DK:3 * HK + (h + 1) * DK] = (
                dy[:, hs] * xh * on * (sg[:, hs] * (1.0 + gx[:, hs] * (1.0 - sg[:, hs])))).astype(BF16)
        don_ref[...] += don

        for h in range(H):
            hs = slice(h * DK, (h + 1) * DK)
            doh = do_s[:, hs].astype(BF16)
            a = lax.dot_general(q_rel[:, hs], k_rel[:, hs], nt, preferred_element_type=F32)
            a = jnp.where(causal, a, 0.0).astype(BF16)
            da = lax.dot_general(doh, v[:, hs], nt, preferred_element_type=F32)
            da = jnp.where(causal, da, 0.0).astype(BF16)
            dv_s[:, hs] = lax.dot_general(a, doh, tn, preferred_element_type=F32)
            dqr_s[:, hs] = jnp.dot(da, k_rel[:, hs], preferred_element_type=F32)
            dkr_s[:, hs] = lax.dot_general(da, q_rel[:, hs], tn, preferred_element_type=F32)
            for j in reversed(range(ncb)):
                rs = slice(j * C, (j + 1) * C)
                dst = dstate[h]
                dstb = dst.astype(BF16)
                st = st_ref[j, h]
                dkd_s[rs, hs] = jnp.dot(v[rs, hs], dstb, preferred_element_type=F32)
                dv_s[rs, hs] += lax.dot_general(k_dec[rs, hs], dstb, nt, preferred_element_type=F32)
                dec = jnp.exp(jnp.sum(logf[rs, hs], axis=0, keepdims=True))
                e_s[rs, hs] = jnp.broadcast_to(jnp.sum(dst * st, axis=0, keepdims=True) * dec, (C, DK))
                dqd_s[rs, hs] = jnp.dot(doh[rs], st.astype(BF16), preferred_element_type=F32)
                dstate[h] = dec * dst + lax.dot_general(doh[rs], q_dec[rs, hs], tn,
                                                        preferred_element_type=F32)

        dqr, dkr, dqd, dkd = dqr_s[...], dkr_s[...], dqd_s[...], dkd_s[...]
        kdk = dkd * k_dec_f
        db = dqr * q_rel_f - dkr * k_rel_f + dqd * q_dec_f - kdk
        dlogf = _split_dot(rev.astype(BF16), db, 2) + _split_dot(ones.astype(BF16), kdk, 2) + e_s[...]
        dk = dkr * enrel + dkd * erest
        df = dlogf / f - dk
        dlb_ref[...] += jnp.sum(df * (1.0 - sf), axis=0, keepdims=True)
        dq = dqr * erel + dqd * eb
        dp_ref[:, 0:HK] = (dq * (sq * (1.0 + qx * (1.0 - sq)))).astype(BF16)
        dp_ref[:, HK:2 * HK] = (df * (1.0 - lb) * sf * (1.0 - sf)).astype(BF16)
        dp_ref[:, 2 * HK:3 * HK] = dv_s[...].astype(BF16)

    rev_row = lambda w: pl.BlockSpec((tb, w), lambda i: (nb - 1 - i, 0))
    vec = lambda w: pl.BlockSpec((1, w), lambda i: (0, 0))
    scr = pltpu.VMEM((tb, HK), F32)
    return pl.pallas_call(
        body, name="hgrn_bwd", grid=(nb,),
        in_specs=[rev_row(4 * HK), vec(HK), vec(DK), rev_row(HK),
                  pl.BlockSpec((ncb, H, DK, DK), lambda i: (nb - 1 - i, 0, 0, 0)), rev_row(HK)],
        out_specs=(rev_row(4 * HK), vec(HK), vec(DK)),
        out_shape=(jax.ShapeDtypeStruct((T, 4 * HK), BF16), jax.ShapeDtypeStruct((1, HK), F32),
                   jax.ShapeDtypeStruct((1, DK), F32)),
        scratch_shapes=[pltpu.VMEM((H, DK, DK), F32), scr, scr, scr, scr, scr, scr, scr],
        compiler_params=_cparams("arbitrary"))(proj, lb, o_norm, o, states, dy)


def _hgrn_layer_fwd(a, o_norm, lb, wbuf, pk, slot, res):
    D = a.shape[1]
    in_by_n, _ = _col_sharded(pk, "hgrn_w_in", slot, D)
    out_by_k, _, _ = _row_sharded(pk, "hgrn_w_o", slot, D // N_CHIPS)
    proj = _mm(a, wbuf, n=4 * D, b_map=in_by_n, tk=D, tn=D, name="hgrn_in")
    y, o, states = _hgrn_fwd(proj, lb, o_norm)
    m, h_new, a_next = _mm(y, wbuf, n=D, b_map=out_by_k, tm=1024, tk=D // N_CHIPS, tn=D, epi="resnorm",
                           extra=res, name="hgrn_out_res")
    return m, (a, proj, y, o, states), h_new, a_next


def _hgrn_layer_bwd(dm, saved, o_norm, lb, wbuf, gbuf, pk, slot):
    a, proj, y, o, states = saved
    D = a.shape[1]
    in_by_n, in_by_k = _col_sharded(pk, "hgrn_w_in", slot, D)
    _, out_by_n, out_by_m = _row_sharded(pk, "hgrn_w_o", slot, D // N_CHIPS)
    dy = _mm(dm, wbuf, tb=True, n=y.shape[1], b_map=out_by_n, tm=2048, tn=D // N_CHIPS, tk=D,
             name="hgrn_out_dx")
    gbuf = _mm(y, dm, ta=True, into=gbuf, o_map=out_by_m, tm=D // N_CHIPS, tn=D, tk=2048, name="hgrn_out_dw")
    dproj, dlb, don = _hgrn_bwd(proj, lb, o_norm, o, states, dy)
    gbuf = _mm(a, dproj, ta=True, into=gbuf, o_map=in_by_n, tm=D, tn=D, name="hgrn_in_dw")
    da = _mm(dproj, wbuf, tb=True, n=D, b_map=in_by_k, tn=D, tk=D, name="hgrn_in_dx")
    return da, gbuf, dict(o_norm=don, lb=dlb)


def _lower_bounds(lb_logits):
    p = jax.nn.softmax(lb_logits.astype(F32), axis=0)
    return jnp.cumsum(p, axis=0) - p[0]


def _rope_tables(positions):
    inv_freq = jnp.power(ROPE_BASE, -jnp.arange(0, MLA_ROPE, 2, dtype=F32) / MLA_ROPE)
    ang = positions.astype(F32)[:, None] * inv_freq
    cos, sin = jnp.cos(ang), jnp.sin(ang)
    zero = jnp.zeros((positions.shape[0], 128 - MLA_ROPE), F32)
    return (jnp.concatenate([cos, cos, zero], axis=-1), jnp.concatenate([-sin, sin, zero], axis=-1))


def _pad_mla_weights(w_in, w_uq):
    w_in_p = jnp.pad(w_in, ((0, 0), (0, 0), (0, 128 - MLA_ROPE)))
    n, ql, _ = w_uq.shape
    w_uq_p = jnp.pad(w_uq.reshape(n, ql, MLA_HEADS, MLA_NOPE + MLA_ROPE),
                     ((0, 0), (0, 0), (0, 0), (0, MLA_QK_PAD - MLA_NOPE - MLA_ROPE)))
    return w_in_p, w_uq_p.reshape(n, ql, MLA_HEADS * MLA_QK_PAD)


def _local_step(x, positions, target, small, fetch, gbufs, emit, emit_mlp):
    T, D = x.shape
    lbounds, lb_vjp = jax.vjp(_lower_bounds, small["hgrn_lb_logits"])
    cc, ss = _rope_tables(positions)
    fetched = {0: fetch(0, None)}
    gains = fetched[0]["gains"]
    tick = [jnp.zeros((), F32)]

    def g(layer, i):
        return gains[layer, i][None, :] + tick[0]

    def mla_weights(layer):
        f = fetched[layer]
        w_in_p, w_uq_p = _pad_mla_weights(f["w_in"][None], f["w_uq"][None])
        slot = layer // 2
        return dict(w_in=w_in_p[0], w_uq=w_uq_p[0], w_ukv=f["w_ukv"],
                    q_norm=small["mla_q_norm"][slot][None, :], kv_norm=small["mla_kv_norm"][slot][None, :])

    saved = []
    h = x
    a = _prenorm_fwd(x, g(0, 0))
    dy = sq = None
    for layer in range(DEPTH):
        slot = layer // 2
        if layer not in fetched:
            fetched[layer] = fetch(layer, a)
        wbuf, pk = fetched[layer]["wbuf"], fetched[layer]["pk"]
        res = (h, g(layer, 1), g(layer, 2))
        if layer % 2 == 0:
            m, mix_saved, h1, a2 = _mla_fwd(a, mla_weights(layer), cc, ss, wbuf, pk, slot, res)
        else:
            m, mix_saved, h1, a2 = _hgrn_layer_fwd(a, small["hgrn_o_norm"][slot][None, :],
                                                   lbounds[layer][None, :], wbuf, pk, slot, res)
        if layer + 1 < DEPTH:
            u, mlp_saved, h2, a = _mlp_fwd(a2, wbuf, pk, layer, (h1, g(layer, 3), g(layer + 1, 0)))
        else:
            u, mlp_saved, h2, _ = _mlp_fwd(a2, wbuf, pk, layer, None)
            dy, sq = _resnorm_loss(h1, u, g(layer, 3), target)
        saved.append((h, m, h1, u, mix_saved, mlp_saved))
        h = h2

    n_mla, n_hgrn = (DEPTH + 1) // 2, DEPTH // 2
    dgains = [[None] * 4 for _ in range(DEPTH)]
    gw = {k: [None] * n_mla for k in ("mla_w_in", "mla_w_uq", "mla_w_ukv", "mla_q_norm", "mla_kv_norm")}
    gw["hgrn_o_norm"] = [None] * n_hgrn
    dlb = [jnp.zeros((1, lbounds.shape[1]), F32) for _ in range(DEPTH)]
    dh = dy
    da_next = None
    for layer in reversed(range(DEPTH)):
        h0, m, h1, u, mix_saved, mlp_saved = saved[layer]
        slot = layer // 2
        wbuf, pk, gbuf = fetched[layer]["wbuf"], fetched[layer]["pk"], gbufs[layer]
        if da_next is None:
            du, dgains[layer][3] = _resnorm_bwd(u, g(layer, 3), dh, name="resnorm_bwd_last")
            t = dh
        else:
            h2 = saved[layer + 1][0]
            t, du, dgains[layer][3], dgains[layer + 1][0] = _resnorm_bwd(
                u, g(layer, 3), dh, h2, da_next, g(layer + 1, 0), name="resnorm_bwd_mlp")
        da2, gbuf = _mlp_bwd(du, mlp_saved, wbuf, gbuf, pk, layer)
        if layer == 0:
            gbuf = emit_mlp(layer, gbuf)
        t, dm, dgains[layer][1], dgains[layer][2] = _resnorm_bwd(
            m, g(layer, 1), t, h1, da2, g(layer, 2), name="resnorm_bwd_mix")
        if layer % 2 == 0:
            da_next, gbuf, mg = _mla_bwd(dm, mix_saved, mla_weights(layer), cc, ss, wbuf, gbuf, pk, slot)
            ql = mg["q_norm"].shape[-1]
            kvl = mg["kv_norm"].shape[-1]
            gw["mla_w_in"][slot] = mg["w_in"][:, :ql + kvl + MLA_ROPE]
            gw["mla_w_uq"][slot] = mg["w_uq"].reshape(ql, MLA_HEADS, MLA_QK_PAD)[
                :, :, :MLA_NOPE + MLA_ROPE].reshape(ql, MLA_HEADS * (MLA_NOPE + MLA_ROPE))
            gw["mla_w_ukv"][slot] = mg["w_ukv"]
            gw["mla_q_norm"][slot] = mg["q_norm"][0]
            gw["mla_kv_norm"][slot] = mg["kv_norm"][0]
        else:
            da_next, gbuf, hg = _hgrn_layer_bwd(dm, mix_saved, small["hgrn_o_norm"][slot][None, :],
                                                lbounds[layer][None, :], wbuf, gbuf, pk, slot)
            gw["hgrn_o_norm"][slot] = hg["o_norm"][0]
            dlb[layer] = hg["lb"]
        dh = t
        if layer > 0:
            mine = ({k: gw[k][slot] for k in ("mla_w_in", "mla_w_uq", "mla_w_ukv")} if layer % 2 == 0 else {})
            tick[0] = emit(layer, gbuf, mine)
        else:
            gbuf0 = gbuf
    grad_x, dgains[0][0] = _prenorm_bwd(x, g(0, 0), dh, da_next)

    last = {k: gw[k][0] for k in ("mla_w_in", "mla_w_uq", "mla_w_ukv")}
    last.update({k: jnp.stack(gw[k]) for k in ("mla_q_norm", "mla_kv_norm", "hgrn_o_norm")})
    last["norm_gains"] = jnp.stack([jnp.concatenate(row, axis=0) for row in dgains])
    (last["hgrn_lb_logits"],) = lb_vjp(jnp.concatenate(dlb, axis=0))
    emit(0, gbuf0, last)
    return sq, grad_x


def _size(shape):
    n = 1
    for d in shape:
        n *= d
    return n


def _piece_rows(shape):
    return -(-_size(shape) // PACK_W)


def _packed_misc_rows(shapes):
    return sum(_piece_rows(s) for s in shapes)


def _cast_into(src, buf, row, name):
    rows, W = src.shape
    tr = min(256, rows)
    assert rows % tr == 0 and row % tr == 0

    def body(s_ref, b_ref, o_ref):
        o_ref[...] = s_ref[...].astype(BF16)

    return pl.pallas_call(
        body, name=name, grid=(rows // tr,),
        in_specs=[pl.BlockSpec((tr, W), lambda i: (i, 0)), pl.BlockSpec(memory_space=pl.ANY)],
        out_specs=pl.BlockSpec((tr, W), lambda i: (row // tr + i, 0)),
        out_shape=jax.ShapeDtypeStruct(buf.shape, buf.dtype), input_output_aliases={1: 0},
        compiler_params=_cparams("parallel"))(src, buf)


def _pack_blocks(pieces, rows, dtype):
    blocks, used = [], 0
    for p in pieces:
        flat = p.astype(dtype).reshape(-1)
        r = _piece_rows(p.shape)
        if r * PACK_W != flat.shape[0]:
            flat = jnp.pad(flat, (0, r * PACK_W - flat.shape[0]))
        blocks.append(flat.reshape(r, PACK_W))
        used += r
    if rows > used:
        blocks.append(jnp.zeros((rows - used, PACK_W), dtype))
    return blocks


def _unpack(buf, shapes):
    out, off = [], 0
    for shp in shapes:
        r = _piece_rows(shp)
        piece = buf[off:off + r]
        if r * PACK_W != _size(shp):
            piece = piece.reshape(-1)[:_size(shp)]
        out.append(piece.reshape(shp))
        off += r
    return out


def _mesh_place():
    x, y, c = lax.axis_index("x"), lax.axis_index("y"), lax.axis_index("c")
    chips = [(1 - x, y), (x, 1 - y), (1 - x, 1 - y)]
    return x, y, c, chips


_HBM = pl.BlockSpec(memory_space=pltpu.HBM)


def _share_reduced(q, name="grads_share_reduced"):
    rh, W = q.shape

    def body(q_ref, out_ref, send_sem, recv_sem):
        x, y, c, _ = _mesh_place()
        cp = pltpu.make_async_remote_copy(src_ref=q_ref, dst_ref=out_ref.at[c], send_sem=send_sem,
                                          recv_sem=recv_sem, device_id=(x, y, 1 - c), device_id_type=MESH)
        cp.start()
        cp.wait()

    out = pl.pallas_call(
        body, name=name, in_specs=[_HBM], out_specs=_HBM,
        out_shape=jax.ShapeDtypeStruct((2, rh, W), q.dtype),
        scratch_shapes=[pltpu.SemaphoreType.DMA, pltpu.SemaphoreType.DMA],
    )(q)
    return out


def _sum_chips(parts, own, own_row0, which, name, out_dtype=F32):
    n, rh, W = parts.shape
    tr = PACK_TILE
    assert own_row0 % tr == 0
    if own.ndim == 3:
        own_spec = pl.BlockSpec((None, tr, W), lambda i, w_ref: (w_ref[0], own_row0 // tr + i, 0))
    else:
        own_spec = pl.BlockSpec((tr, W), lambda i, w_ref: (own_row0 // tr + i, 0))

    def body(w_ref, p_ref, own_ref, o_ref):
        mine = own_ref[...].astype(F32)
        acc = None
        for j in range(n):
            term = jnp.where(w_ref[0] == j, mine, p_ref[j].astype(F32))
            acc = term if acc is None else acc + term
        o_ref[...] = acc.astype(out_dtype)

    return pl.pallas_call(
        body, name=name,
        grid_spec=pltpu.PrefetchScalarGridSpec(
            num_scalar_prefetch=1, grid=(rh // tr,),
            in_specs=[pl.BlockSpec((n, tr, W), lambda i, w_ref: (0, i, 0)), own_spec],
            out_specs=pl.BlockSpec((tr, W), lambda i, w_ref: (i, 0))),
        out_shape=jax.ShapeDtypeStruct((rh, W), out_dtype),
        compiler_params=_cparams("parallel"))(jnp.reshape(which, (1,)).astype(jnp.int32), parts, own)


_SEM = pl.BlockSpec(memory_space=pltpu.SEMAPHORE)
_ASYNC = pltpu.CompilerParams(has_side_effects=pltpu.SideEffectType.DATAFLOW_SIDE_EFFECTING)


def _hbm(a):
    return pltpu.with_memory_space_constraint(a, pltpu.HBM)


def _gather_copies(w_ref, land_ref, send_sems, recv_sems):
    x, y, c, chips = _mesh_place()
    me = 2 * x + y
    rh = w_ref.shape[0] // 2
    rows = pl.ds(pl.multiple_of(c * rh, 16), rh)
    return [pltpu.make_async_remote_copy(
        src_ref=w_ref.at[rows], dst_ref=land_ref.at[me, rows], send_sem=send_sems.at[r],
        recv_sem=recv_sems.at[r], device_id=(px, py, c), device_id_type=MESH)
        for r, (px, py) in enumerate(chips)]


def _scatter_copies(g_ref, land_ref, send_sems, recv_sems, row0):
    x, y, c, chips = _mesh_place()
    me = 2 * x + y
    rows = pl.ds(row0, land_ref.shape[1])
    return [pltpu.make_async_remote_copy(
        src_ref=g_ref.at[2 * px + py, rows], dst_ref=land_ref.at[me], send_sem=send_sems.at[r],
        recv_sem=recv_sems.at[r], device_id=(px, py, c), device_id_type=MESH)
        for r, (px, py) in enumerate(chips)]


def _halves_to_sibling(land, name):
    n, R, W = land.shape
    rh = R // 2

    def body(l_ref, o_ref, send_sems, recv_sems):
        x, y, c, chips = _mesh_place()
        rows = pl.ds(pl.multiple_of(c * rh, 16), rh)
        copies = [pltpu.make_async_remote_copy(
            src_ref=o_ref.at[2 * px + py, rows], dst_ref=o_ref.at[2 * px + py, rows], send_sem=send_sems.at[r],
            recv_sem=recv_sems.at[r], device_id=(x, y, 1 - c), device_id_type=MESH)
            for r, (px, py) in enumerate(chips)]
        for cp in copies:
            cp.start()
        for cp in copies:
            cp.wait()

    return pl.pallas_call(
        body, name=name, in_specs=[_HBM], out_specs=_HBM, out_shape=jax.ShapeDtypeStruct(land.shape, land.dtype),
        scratch_shapes=[pltpu.SemaphoreType.DMA((3,)), pltpu.SemaphoreType.DMA((3,))],
        input_output_aliases={0: 0})(land)


def _gather_start(wp, name):
    R, W = wp.shape

    def body(w_ref, land_ref, send_sems, recv_sems, w_thru, land_thru, token):
        for cp in _gather_copies(w_ref, land_ref, send_sems, recv_sems):
            cp.start()
        token[...] = jnp.zeros_like(token)

    return pl.pallas_call(
        body, name=name,
        out_shape=(pltpu.SemaphoreType.DMA((3,)), pltpu.SemaphoreType.DMA((3,)), pltpu.HBM(wp.shape, wp.dtype),
                   pltpu.HBM((N_CHIPS, R, W), wp.dtype), jax.ShapeDtypeStruct((8, 128), F32)),
        in_specs=(_HBM, _HBM),
        out_specs=(_SEM, _SEM, _HBM, _HBM, pl.BlockSpec(memory_space=pltpu.VMEM)),
        input_output_aliases={0: 2, 1: 3}, compiler_params=_ASYNC,
    )(_hbm(wp), _hbm(lax.empty((N_CHIPS, R, W), wp.dtype)))


def _gather_wait(send_sems, recv_sems, w_thru, land_thru, after, name):
    R, W = w_thru.shape
    rh = R // 2

    def body(w_ref, land_ref, send_sems, recv_sems, after_ref, w_dead, got_ref):
        x, y, c, _ = _mesh_place()
        half = land_ref.at[0, pl.ds(0, rh)]
        for k in range(3):
            cp = pltpu.make_async_remote_copy(src_ref=half, dst_ref=half, send_sem=send_sems.at[k],
                                              recv_sem=recv_sems.at[k], device_id=(x, y, 1 - c),
                                              device_id_type=MESH)
            cp.wait_send()
            cp.wait_recv()

    return pl.pallas_call(
        body, name=name,
        out_shape=(pltpu.HBM(w_thru.shape, w_thru.dtype), pltpu.HBM(land_thru.shape, land_thru.dtype)),
        in_specs=(_HBM, _HBM, _SEM, _SEM, pl.BlockSpec(memory_space=pl.ANY)), out_specs=(_HBM, _HBM),
        input_output_aliases={0: 0, 1: 1}, compiler_params=_ASYNC,
    )(w_thru, land_thru, send_sems, recv_sems, after)


def _scatter_start(g, row0, nrows, name):
    n, R, W = g.shape
    land_shape = (n, nrows, W)

    def body(g_ref, land_ref, send_sems, recv_sems, g_thru, land_thru, token):
        for cp in _scatter_copies(g_ref, land_ref, send_sems, recv_sems, row0):
            cp.start()
        token[...] = jnp.zeros_like(token)

    return pl.pallas_call(
        body, name=name,
        out_shape=(pltpu.SemaphoreType.DMA((3,)), pltpu.SemaphoreType.DMA((3,)), pltpu.HBM(g.shape, g.dtype),
                   pltpu.HBM(land_shape, g.dtype), jax.ShapeDtypeStruct((8, 128), F32)),
        in_specs=(_HBM, _HBM),
        out_specs=(_SEM, _SEM, _HBM, _HBM, pl.BlockSpec(memory_space=pltpu.VMEM)),
        input_output_aliases={0: 2, 1: 3}, compiler_params=_ASYNC,
    )(_hbm(g), _hbm(lax.empty(land_shape, g.dtype)))


def _scatter_wait(send_sems, recv_sems, g_thru, land_thru, after, name):
    def body(g_ref, land_ref, send_sems, recv_sems, after_ref, g_out, got_ref):
        x, y, c, _ = _mesh_place()
        for k in range(3):
            cp = pltpu.make_async_remote_copy(src_ref=land_ref.at[0], dst_ref=land_ref.at[0], send_sem=send_sems.at[k],
                                              recv_sem=recv_sems.at[k], device_id=(x, y, 1 - c),
                                              device_id_type=MESH)
            cp.wait_send()
            cp.wait_recv()

    return pl.pallas_call(
        body, name=name,
        out_shape=(pltpu.HBM(g_thru.shape, g_thru.dtype), pltpu.HBM(land_thru.shape, land_thru.dtype)),
        in_specs=(_HBM, _HBM, _SEM, _SEM, pl.BlockSpec(memory_space=pl.ANY)), out_specs=(_HBM, _HBM),
        input_output_aliases={0: 0, 1: 1}, compiler_params=_ASYNC,
    )(g_thru, land_thru, send_sems, recv_sems, after)


def _adamw(w, g, m, v, name):
    shape = w.shape
    cols = shape[-1]
    w2, g2, m2, v2 = (t.reshape(-1, cols) for t in (w, g, m, v))
    rows = w2.shape[0]
    tr = rows
    for cand in (512, 256, 128, 64, 32, 16, 8):
        if rows > cand and rows % cand == 0:
            tr = cand
            break
    c1 = 1.0 / (1.0 - ADAM_B1 ** ADAM_STEP)
    c2 = 1.0 / (1.0 - ADAM_B2 ** ADAM_STEP)

    def body(w_ref, g_ref, m_ref, v_ref, d_ref, nm_ref, nv_ref):
        gv = g_ref[...]
        nm = ADAM_B1 * m_ref[...] + (1.0 - ADAM_B1) * gv
        nv = ADAM_B2 * v_ref[...] + (1.0 - ADAM_B2) * (gv * gv)
        nm_ref[...] = nm
        nv_ref[...] = nv
        d_ref[...] = -ADAM_LR * ((nm * c1) / (jnp.sqrt(nv * c2) + ADAM_EPS) + ADAM_WD * w_ref[...])

    blk = pl.BlockSpec((tr, cols), lambda i: (i, 0))
    sds = jax.ShapeDtypeStruct((rows, cols), F32)
    d, nm, nv = pl.pallas_call(body, name=name, grid=(rows // tr,), in_specs=[blk] * 4,
                               out_specs=(blk, blk, blk), out_shape=(sds, sds, sds),
                               compiler_params=_cparams("parallel"))(w2, g2, m2, v2)
    return d.reshape(shape), nm.reshape(shape), nv.reshape(shape)


def kernel(x, positions, norm_gains, mla_w_in, mla_q_norm, mla_kv_norm, mla_w_uq, mla_w_ukv, mla_w_o, hgrn_w_in, hgrn_lb_logits, hgrn_o_norm, hgrn_w_o, mlp_w1, mlp_w2, loss_target, m_norm_gains, m_mla_w_in, m_mla_q_norm, m_mla_kv_norm, m_mla_w_uq, m_mla_w_ukv, m_mla_w_o, m_hgrn_w_in, m_hgrn_lb_logits, m_hgrn_o_norm, m_hgrn_w_o, m_mlp_w1, m_mlp_w2, v_norm_gains, v_mla_w_in, v_mla_q_norm, v_mla_kv_norm, v_mla_w_uq, v_mla_w_ukv, v_mla_w_o, v_hgrn_w_in, v_hgrn_lb_logits, v_hgrn_o_norm, v_hgrn_w_o, v_mlp_w1, v_mlp_w2):
    w = dict(norm_gains=norm_gains, mla_w_in=mla_w_in, mla_q_norm=mla_q_norm, mla_kv_norm=mla_kv_norm,
             mla_w_uq=mla_w_uq, mla_w_ukv=mla_w_ukv, mla_w_o=mla_w_o, hgrn_w_in=hgrn_w_in,
             hgrn_lb_logits=hgrn_lb_logits, hgrn_o_norm=hgrn_o_norm, hgrn_w_o=hgrn_w_o,
             mlp_w1=mlp_w1, mlp_w2=mlp_w2)
    mom_m = dict(norm_gains=m_norm_gains, mla_w_in=m_mla_w_in, mla_q_norm=m_mla_q_norm,
                 mla_kv_norm=m_mla_kv_norm, mla_w_uq=m_mla_w_uq, mla_w_ukv=m_mla_w_ukv,
                 mla_w_o=m_mla_w_o, hgrn_w_in=m_hgrn_w_in, hgrn_lb_logits=m_hgrn_lb_logits,
                 hgrn_o_norm=m_hgrn_o_norm, hgrn_w_o=m_hgrn_w_o, mlp_w1=m_mlp_w1, mlp_w2=m_mlp_w2)
    mom_v = dict(norm_gains=v_norm_gains, mla_w_in=v_mla_w_in, mla_q_norm=v_mla_q_norm,
                 mla_kv_norm=v_mla_kv_norm, mla_w_uq=v_mla_w_uq, mla_w_ukv=v_mla_w_ukv,
                 mla_w_o=v_mla_w_o, hgrn_w_in=v_hgrn_w_in, hgrn_lb_logits=v_hgrn_lb_logits,
                 hgrn_o_norm=v_hgrn_o_norm, hgrn_w_o=v_hgrn_w_o, mlp_w1=v_mlp_w1, mlp_w2=v_mlp_w2)
    c = lax.axis_index("c")

    axis_of = dict(SHARDED)
    me = 2 * lax.axis_index("x") + lax.axis_index("y")
    gain_bits = lax.bitcast_convert_type(norm_gains, jnp.uint32)
    gain_hi = lax.bitcast_convert_type((gain_bits >> 16).astype(jnp.uint16), BF16)
    gain_lo = lax.bitcast_convert_type((gain_bits & 0xFFFF).astype(jnp.uint16), BF16)

    layers = []
    for l in range(DEPTH):
        s = l // 2
        if l % 2 == 0:
            big = [("mlp_w1", l), ("mlp_w2", l), ("mla_w_o", s)]
            tail = [("mla_w_in", s), ("mla_w_uq", s), ("mla_w_ukv", s)]
        else:
            big = [("hgrn_w_in", s), ("mlp_w1", l), ("mlp_w2", l), ("hgrn_w_o", s)]
            tail = []
        w_tail = [w[n][i] for n, i in tail] + ([gain_hi, gain_lo] if l == 0 else [])
        g_tail = tail + ([("norm_gains", None)] + [(n, None) for n in REPLICATED] if l == 0 else [])
        g_shapes = [w[n].shape if i is None else w[n][i].shape for n, i in g_tail]
        tail_rows = max(_packed_misc_rows([t.shape for t in w_tail]), _packed_misc_rows(g_shapes))
        pk = _Packed([(n, w[n].shape[1]) for n, _ in big], tail_rows)
        wpack = jnp.zeros((pk.rows, PACK_W), BF16)
        for n, i in big:
            assert w[n].shape[2] == PACK_W
            wpack = _cast_into(w[n][i], wpack, pk.off[n], name="pack_%s_%d" % (n, l))
        if w_tail:
            wpack = lax.dynamic_update_slice(
                wpack, jnp.concatenate(_pack_blocks(w_tail, 0, BF16), axis=0), (pk.misc, 0))
        layers.append(dict(pk=pk, big=big, tail=tail, w_tail=w_tail, g_tail=g_tail, g_shapes=g_shapes,
                           gather=_gather_start(wpack, name="gather_start_%d" % l)))

    def fetch(l, after):
        lay = layers[l]
        pk = lay["pk"]
        send_sems, recv_sems, w_thru, land_thru, _ = lay["gather"]
        if after is None:
            after = sum(layers[k]["gather"][4] for k in range(1, DEPTH))
        w_back, land = _gather_wait(send_sems, recv_sems, w_thru, land_thru, after, name="gather_wait_%d" % l)
        land = _halves_to_sibling(land, name="gather_halves_%d" % l)
        land = lax.dynamic_update_slice(land, w_back[None], (me, 0, 0))
        out = dict(wbuf=land.reshape(N_CHIPS * pk.rows, PACK_W), pk=pk)
        if lay["w_tail"]:
            rows = _packed_misc_rows([t.shape for t in lay["w_tail"]])
            per_chip = [_unpack(land[j, pk.misc:pk.misc + rows], [t.shape for t in lay["w_tail"]])
                        for j in range(N_CHIPS)]
            for i, (n, _) in enumerate(lay["tail"]):
                out[n[4:]] = jnp.concatenate([per_chip[j][i] for j in range(N_CHIPS)], axis=axis_of[n] - 1)
            if l == 0:
                got_hi, got_lo = (lax.bitcast_convert_type(
                    jnp.concatenate([per_chip[j][i] for j in range(N_CHIPS)], axis=2),
                    jnp.uint16).astype(jnp.uint32) for i in (-2, -1))
                out["gains"] = lax.bitcast_convert_type((got_hi << 16) | got_lo, F32)
        return out

    def emit(l, gbuf, grads):
        lay = layers[l]
        pk = lay["pk"]
        if lay["g_tail"]:
            for j in range(N_CHIPS):
                pieces = []
                for n, i in lay["g_tail"]:
                    if n not in axis_of:
                        pieces.append(grads[n])
                    else:
                        pieces.append(jnp.split(grads[n], N_CHIPS, axis=axis_of[n] - (0 if i is None else 1))[j])
                block = jnp.concatenate(_pack_blocks(pieces, 0, BF16), axis=0)
                gbuf = lax.dynamic_update_slice(gbuf, block, (j * pk.rows + pk.misc, 0))
        row0 = lay.get("early_rows", 0)
        lay["scatter"] = _scatter_start(gbuf.reshape(N_CHIPS, pk.rows, PACK_W), row0, pk.rows - row0,
                                        name="scatter_start_%d" % l)
        return lay["scatter"][4][0, 0]

    def emit_mlp(l, gbuf):
        lay = layers[l]
        pk = lay["pk"]
        assert pk.off["mlp_w1"] == 0 and pk.off["mlp_w2"] == w["mlp_w1"].shape[1]
        lay["early_rows"] = w["mlp_w1"].shape[1] + w["mlp_w2"].shape[1]
        lay["scatter_early"] = _scatter_start(gbuf.reshape(N_CHIPS, pk.rows, PACK_W), 0, lay["early_rows"],
                                              name="scatter_start_%d_mlp" % l)
        return lay["scatter_early"][2].reshape(N_CHIPS * pk.rows, PACK_W)

    small = dict(mla_q_norm=mla_q_norm, mla_kv_norm=mla_kv_norm, hgrn_lb_logits=hgrn_lb_logits,
                 hgrn_o_norm=hgrn_o_norm)
    gbufs = [lax.empty((N_CHIPS * lay["pk"].rows, PACK_W), BF16) for lay in layers]
    sq, grad_x = _local_step(x[0], positions[0], loss_target[0], small, fetch, gbufs, emit, emit_mlp)
    d_model = x.shape[-1]
    loss = lax.psum(0.5 * jnp.sum(sq) / d_model, ("x", "y", "c"))

    per_name = {}
    behind = grad_x
    for l, lay in reversed(list(enumerate(layers))):
        pk = lay["pk"]
        send_sems, recv_sems, g_thru, land_thru, _ = lay["scatter"]
        row0 = lay.get("early_rows", 0)
        early = None
        if row0:
            e_send, e_recv, _, e_land, _ = lay["scatter_early"]
            g_thru, land = _scatter_wait(e_send, e_recv, g_thru, e_land, behind, name="scatter_wait_%d_mlp" % l)
            early = behind = _sum_chips(land, g_thru, 0, me, name="grads_sum_chips_%d_mlp" % l, out_dtype=BF16)
        g_back, land = _scatter_wait(send_sems, recv_sems, g_thru, land_thru, behind, name="scatter_wait_%d" % l)
        mine = _sum_chips(land, g_back, row0, me, name="grads_sum_chips_%d" % l, out_dtype=BF16)
        if early is not None:
            mine = jnp.concatenate([early, mine], axis=0)
        red = behind = _sum_chips(_share_reduced(mine, name="grads_share_%d" % l), mine, 0, c,
                                  name="grads_sum_cores_%d" % l)
        for n, i in lay["big"]:
            per_name.setdefault(n, {})[i] = red[pk.off[n]:pk.off[n] + w[n].shape[1]]
        for (n, i), piece in zip(lay["g_tail"], _unpack(red[pk.misc:pk.misc + pk.misc_rows], lay["g_shapes"])):
            per_name.setdefault(n, {})[i] = piece
    g_out = {n: (parts[None] if None in parts else jnp.stack([parts[i] for i in sorted(parts)]))
             for n, parts in per_name.items()}

    deltas, new_m, new_v = {}, {}, {}
    for name in WEIGHTS:
        deltas[name], new_m[name], new_v[name] = _adamw(w[name], g_out[name], mom_m[name], mom_v[name],
                                                        name="adamw_" + name)
    return (loss, grad_x[None], *[g_out[n] for n in WEIGHTS], *[deltas[n] for n in WEIGHTS],
            *[new_m[n] for n in WEIGHTS], *[new_v[n] for n in WEIGHTS])
```

```python
import jax
import jax.numpy as jnp
from jax import lax
from jax.experimental import pallas as pl
from jax.experimental.pallas import tpu as pltpu

F32 = jnp.float32
BF16 = jnp.bfloat16
MESH = pl.DeviceIdType.MESH

DEPTH = 4
MLA_HEADS = 8
MLA_NOPE = 128
MLA_ROPE = 64
MLA_V = 128
MLA_QK_PAD = 256
MLA_HEADS_PER_STEP = 2
MLA_SCALE = float(MLA_NOPE + MLA_ROPE) ** -0.5
ROPE_BASE = 10000.0
HGRN_HEADS = 8
HGRN_CHUNK = 32
HGRN_BLOCK = 128
EPS = 1e-6

ADAM_LR = 0.001
ADAM_B1 = 0.9
ADAM_B2 = 0.999
ADAM_EPS = 1e-08
ADAM_WD = 0.01
ADAM_STEP = 10

N_CHIPS = 4
PACK_W = 1024
PACK_ALIGN = 1024
PACK_TILE = 512
V7X_VMEM_LIMIT = 56 * 1024 * 1024

SHARDED = (("norm_gains", 2), ("mla_w_in", 1), ("mla_w_uq", 2), ("mla_w_ukv", 2), ("mla_w_o", 1),
           ("hgrn_w_in", 2), ("hgrn_w_o", 1), ("mlp_w1", 2), ("mlp_w2", 1))
REPLICATED = ("mla_q_norm", "mla_kv_norm", "hgrn_lb_logits", "hgrn_o_norm")
WEIGHTS = ("norm_gains", "mla_w_in", "mla_q_norm", "mla_kv_norm", "mla_w_uq", "mla_w_ukv", "mla_w_o",
           "hgrn_w_in", "hgrn_lb_logits", "hgrn_o_norm", "hgrn_w_o", "mlp_w1", "mlp_w2")


def _cparams(*semantics):
    return pltpu.CompilerParams(dimension_semantics=semantics, vmem_limit_bytes=V7X_VMEM_LIMIT)


def _sigmoid(x):
    return 0.5 * jnp.tanh(0.5 * x) + 0.5


def _mm(a, b, *, ta=False, tb=False, out_dtype=F32, tm=2048, tn=1024, tk=1024, epi=None, extra=None,
        name="mm", n=None, b_map=None, into=None, o_map=None):
    if ta:
        K, M = a.shape
    else:
        M, K = a.shape
    if b_map is not None:
        N = n
    elif tb:
        N, Kb = b.shape
    else:
        Kb, N = b.shape
    assert b_map is not None or K == Kb, (a.shape, b.shape, ta, tb)
    tm, tn = min(tm, M), min(tn, N)
    if ta and b_map is None:
        tk = max(tk, 4096)
    tk = K if (K <= 1024 and b_map is None) else min(tk, K)
    assert M % tm == 0 and N % tn == 0 and K % tk == 0, (M, N, K, tm, tn, tk)
    nk = K // tk
    a_spec = (pl.BlockSpec((tk, tm), lambda i, j, k: (k, i)) if ta
              else pl.BlockSpec((tm, tk), lambda i, j, k: (i, k)))
    if b_map is None:
        b_map = (lambda i, j, k: (j, k)) if tb else (lambda i, j, k: (k, j))
    b_spec = pl.BlockSpec((tn, tk) if tb else (tk, tn), b_map)
    o_spec = pl.BlockSpec((tm, tn), lambda i, j, k: (i, j))
    dims = (((0 if ta else 1,), (1 if tb else 0,)), ((), ()))
    in_specs = [a_spec, b_spec]
    operands = [a, b]
    aliases = {}
    if epi == "mul2r":
        in_specs.append(o_spec)
        operands.append(extra)
    if epi == "resnorm":
        assert tn == N
        vec = pl.BlockSpec((1, N), lambda i, j, k: (0, 0))
        in_specs += [o_spec, vec, vec]
        operands += list(extra)
    if into is not None:
        assert epi is None
        in_specs.append(pl.BlockSpec(memory_space=pl.ANY))
        operands.append(into)
        aliases = {2: 0}
        out_dtype = into.dtype
        out_shape = jax.ShapeDtypeStruct(into.shape, into.dtype)
        out_specs = pl.BlockSpec((tm, tn), o_map)
    elif epi == "relu2":
        out_shape = (jax.ShapeDtypeStruct((M, N), BF16), jax.ShapeDtypeStruct((M, N), BF16))
        out_specs = (o_spec, o_spec)
    elif epi == "mul2r":
        out_shape = jax.ShapeDtypeStruct((M, N), BF16)
        out_specs = o_spec
    elif epi == "resnorm":
        out_shape = (jax.ShapeDtypeStruct((M, N), F32), jax.ShapeDtypeStruct((M, N), F32),
                     jax.ShapeDtypeStruct((M, N), BF16))
        out_specs = (o_spec, o_spec, o_spec)
    else:
        out_shape = jax.ShapeDtypeStruct((M, N), out_dtype)
        out_specs = o_spec
    n_in = len(operands)
    n_out = {"relu2": 2, "resnorm": 3}.get(epi, 1)

    def body(*refs):
        a_ref, b_ref = refs[0], refs[1]
        outs = refs[n_in:n_in + n_out]
        k = pl.program_id(2)

        def finish(acc):
            if epi == "relu2":
                r = jnp.maximum(acc, 0.0)
                outs[0][...] = (r * r).astype(BF16)
                outs[1][...] = r.astype(BF16)
            elif epi == "mul2r":
                outs[0][...] = (acc * (2.0 * refs[2][...].astype(F32))).astype(BF16)
            elif epi == "resnorm":
                h_ref, gp_ref, gn_ref = refs[2], refs[3], refs[4]
                hn = h_ref[...] + acc * _rms_rstd(acc) * gp_ref[...]
                outs[0][...] = acc
                outs[1][...] = hn
                outs[2][...] = (hn * _rms_rstd(hn) * gn_ref[...]).astype(BF16)
            else:
                outs[0][...] = acc.astype(out_dtype)

        part = lax.dot_general(a_ref[...], b_ref[...], dims, preferred_element_type=F32)
        if nk == 1:
            finish(part)
            return
        acc_ref = refs[-1]

        @pl.when(k == 0)
        def _():
            acc_ref[...] = part

        @pl.when((k > 0) & (k < nk - 1))
        def _():
            acc_ref[...] += part

        @pl.when(k == nk - 1)
        def _():
            finish(acc_ref[...] + part)

    return pl.pallas_call(
        body, name=name, grid=(M // tm, N // tn, nk), in_specs=in_specs, out_specs=out_specs,
        out_shape=out_shape, scratch_shapes=[pltpu.VMEM((tm, tn), F32)] if nk > 1 else [],
        input_output_aliases=aliases,
        compiler_params=_cparams("parallel", "parallel", "arbitrary"))(*operands)


def _rms_rstd(x):
    return lax.rsqrt(jnp.mean(x * x, axis=-1, keepdims=True) + EPS)


def _rms_bwd_tile(x, g, dy):
    r = _rms_rstd(x)
    xh = x * r
    u = dy * g
    dx = r * (u - xh * jnp.mean(u * xh, axis=-1, keepdims=True))
    dg = jnp.sum(dy * xh, axis=0, keepdims=True)
    return dx, dg


def _row_tile(T):
    return min(512, T)


def _mid_tile(T):
    return min(256, T)


def _prenorm_fwd(x, g, name="prenorm_fwd"):
    T, D = x.shape
    tm = _row_tile(T)

    def body(x_ref, g_ref, a_ref):
        xv = x_ref[...]
        a_ref[...] = (xv * _rms_rstd(xv) * g_ref[...]).astype(BF16)

    row = pl.BlockSpec((tm, D), lambda i: (i, 0))
    vec = pl.BlockSpec((1, D), lambda i: (0, 0))
    return pl.pallas_call(body, name=name, grid=(T // tm,), in_specs=[row, vec], out_specs=row,
                          out_shape=jax.ShapeDtypeStruct((T, D), BF16),
                          compiler_params=_cparams("parallel"))(x, g)


def _resnorm_loss(h, z, g_post, target, name="resnorm_loss"):
    T, D = h.shape
    tm = _row_tile(T)

    def body(h_ref, z_ref, gp_ref, t_ref, dy_ref, sq_ref):
        zv = z_ref[...]
        err = h_ref[...] + zv * _rms_rstd(zv) * gp_ref[...] - t_ref[...]
        dy_ref[...] = err * (1.0 / D)

        @pl.when(pl.program_id(0) == 0)
        def _():
            sq_ref[...] = jnp.zeros_like(sq_ref)

        sq_ref[...] += jnp.sum(err * err, axis=0, keepdims=True)

    row = pl.BlockSpec((tm, D), lambda i: (i, 0))
    vec = pl.BlockSpec((1, D), lambda i: (0, 0))
    return pl.pallas_call(body, name=name, grid=(T // tm,), in_specs=[row, row, vec, row],
                          out_specs=(row, vec),
                          out_shape=(jax.ShapeDtypeStruct((T, D), F32), jax.ShapeDtypeStruct((1, D), F32)),
                          compiler_params=_cparams("arbitrary"))(h, z, g_post, target)


def _resnorm_bwd(z, g_post, dh, h_new=None, da=None, g_pre=None, name="resnorm_bwd"):
    T, D = z.shape
    tm = _row_tile(T)
    has_next = h_new is not None
    row = pl.BlockSpec((tm, D), lambda i: (i, 0))
    vec = pl.BlockSpec((1, D), lambda i: (0, 0))

    if has_next:
        def body(z_ref, gp_ref, dh_ref, hn_ref, da_ref, gn_ref, t_ref, dz_ref, dgp_ref, dgn_ref):
            first = pl.program_id(0) == 0

            @pl.when(first)
            def _():
                dgp_ref[...] = jnp.zeros_like(dgp_ref)
                dgn_ref[...] = jnp.zeros_like(dgn_ref)

            dpre, dgn = _rms_bwd_tile(hn_ref[...], gn_ref[...], da_ref[...])
            t = dh_ref[...] + dpre
            t_ref[...] = t
            dz, dgp = _rms_bwd_tile(z_ref[...], gp_ref[...], t)
            dz_ref[...] = dz.astype(BF16)
            dgp_ref[...] += dgp
            dgn_ref[...] += dgn

        return pl.pallas_call(
            body, name=name, grid=(T // tm,), in_specs=[row, vec, row, row, row, vec],
            out_specs=(row, row, vec, vec),
            out_shape=(jax.ShapeDtypeStruct((T, D), F32), jax.ShapeDtypeStruct((T, D), BF16),
                       jax.ShapeDtypeStruct((1, D), F32), jax.ShapeDtypeStruct((1, D), F32)),
            compiler_params=_cparams("arbitrary"))(z, g_post, dh, h_new, da, g_pre)

    def body_last(z_ref, gp_ref, dh_ref, dz_ref, dgp_ref):
        @pl.when(pl.program_id(0) == 0)
        def _():
            dgp_ref[...] = jnp.zeros_like(dgp_ref)

        dz, dgp = _rms_bwd_tile(z_ref[...], gp_ref[...], dh_ref[...])
        dz_ref[...] = dz.astype(BF16)
        dgp_ref[...] += dgp

    return pl.pallas_call(
        body_last, name=name, grid=(T // tm,), in_specs=[row, vec, row], out_specs=(row, vec),
        out_shape=(jax.ShapeDtypeStruct((T, D), BF16), jax.ShapeDtypeStruct((1, D), F32)),
        compiler_params=_cparams("arbitrary"))(z, g_post, dh)


def _prenorm_bwd(x, g, dh, da, name="prenorm_bwd"):
    T, D = x.shape
    tm = _row_tile(T)

    def body(x_ref, g_ref, dh_ref, da_ref, dx_ref, dg_ref):
        @pl.when(pl.program_id(0) == 0)
        def _():
            dg_ref[...] = jnp.zeros_like(dg_ref)

        dpre, dg = _rms_bwd_tile(x_ref[...], g_ref[...], da_ref[...])
        dx_ref[...] = dh_ref[...] + dpre
        dg_ref[...] += dg

    row = pl.BlockSpec((tm, D), lambda i: (i, 0))
    vec = pl.BlockSpec((1, D), lambda i: (0, 0))
    return pl.pallas_call(
        body, name=name, grid=(T // tm,), in_specs=[row, vec, row, row], out_specs=(row, vec),
        out_shape=(jax.ShapeDtypeStruct((T, D), F32), jax.ShapeDtypeStruct((1, D), F32)),
        compiler_params=_cparams("arbitrary"))(x, g, dh, da)


class _Packed:
    def __init__(self, big, misc_rows):
        self.big = tuple(big)
        self.off = {}
        r = 0
        for name, rows in big:
            self.off[name] = r
            r += rows
        self.misc, self.misc_rows = r, misc_rows
        self.rows = -(-(r + misc_rows) // PACK_ALIGN) * PACK_ALIGN

    def block(self, name, layer, unit):
        r = self.off[name]
        assert r % unit == 0 and self.rows % unit == 0
        return r // unit, self.rows // unit


def _col_sharded(pk, name, layer, unit):
    base, stride = pk.block(name, layer, unit)
    return (lambda i, j, k: (j * stride + base, 0)), (lambda i, j, k: (k * stride + base, 0))


def _row_sharded(pk, name, layer, unit):
    base, stride = pk.block(name, layer, unit)
    return ((lambda i, j, k: (k * stride + base, 0)), (lambda i, j, k: (j * stride + base, 0)),
            (lambda i, j, k: (i * stride + base, 0)))


def _mlp_fwd(a, wbuf, pk, layer, res):
    D = a.shape[1]
    by_n, _ = _col_sharded(pk, "mlp_w1", layer, D)
    by_k, _, _ = _row_sharded(pk, "mlp_w2", layer, D)
    act, r = _mm(a, wbuf, n=4 * D, b_map=by_n, tk=D, tn=D, epi="relu2", name="mlp_up")
    if res is None:
        return _mm(act, wbuf, n=D, b_map=by_k, tk=D, tn=D, name="mlp_down"), (a, act, r), None, None
    u, h_new, a_next = _mm(act, wbuf, n=D, b_map=by_k, tm=1024, tk=D, tn=D, epi="resnorm", extra=res,
                           name="mlp_down_res")
    return u, (a, act, r), h_new, a_next


def _mlp_bwd(du, saved, wbuf, gbuf, pk, layer):
    a, act, r = saved
    D = a.shape[1]
    w1_by_n, w1_by_k = _col_sharded(pk, "mlp_w1", layer, D)
    _, w2_by_n, w2_by_m = _row_sharded(pk, "mlp_w2", layer, D)
    dz1 = _mm(du, wbuf, tb=True, n=4 * D, b_map=w2_by_n, tn=D, tk=D, epi="mul2r", extra=r, name="mlp_down_dx")
    gbuf = _mm(act, du, ta=True, into=gbuf, o_map=w2_by_m, tm=D, tn=D, name="mlp_down_dw")
    gbuf = _mm(a, dz1, ta=True, into=gbuf, o_map=w1_by_n, tm=D, tn=D, name="mlp_up_dw")
    da = _mm(dz1, wbuf, tb=True, n=D, b_map=w1_by_k, tn=D, tk=D, name="mlp_up_dx")
    return da, gbuf


def _rope_swap(t):
    n = t.shape[-1]
    lane = lax.broadcasted_iota(jnp.int32, t.shape, t.ndim - 1)
    half = MLA_ROPE // 2
    first = (lane & (MLA_ROPE - 1)) < half
    return jnp.where(first, pltpu.roll(t, n - half, t.ndim - 1), pltpu.roll(t, half, t.ndim - 1))


def _mla_mid_fwd(proj, q_norm, kv_norm, w_uq, w_ukv, cc, ss):
    T, PW = proj.shape
    QL, KVL = q_norm.shape[-1], kv_norm.shape[-1]
    H = MLA_HEADS
    assert PW == QL + KVL + 128
    tm = _mid_tile(T)

    def body(p_ref, qn_ref, kn_ref, wq_ref, wkv_ref, cc_ref, ss_ref,
             cq_ref, ckv_ref, q_ref, k_ref, v_ref):
        cq = p_ref[:, 0:QL]
        ckv = p_ref[:, QL:QL + KVL]
        kr = p_ref[:, QL + KVL:QL + KVL + 128]
        c, s = cc_ref[...], ss_ref[...]
        cqn = (cq * _rms_rstd(cq) * qn_ref[...]).astype(BF16)
        ckvn = (ckv * _rms_rstd(ckv) * kn_ref[...]).astype(BF16)
        cq_ref[...] = cqn
        ckv_ref[...] = ckvn
        q = jnp.dot(cqn, wq_ref[...], preferred_element_type=F32)
        kv = jnp.dot(ckvn, wkv_ref[...], preferred_element_type=F32)
        krf = (kr * c + _rope_swap(kr) * s).astype(BF16)
        for h in range(H):
            o = h * MLA_QK_PAD
            q_ref[:, o:o + MLA_NOPE] = (q[:, o:o + MLA_NOPE] * MLA_SCALE).astype(BF16)
            qr = q[:, o + MLA_NOPE:o + MLA_QK_PAD]
            q_ref[:, o + MLA_NOPE:o + MLA_QK_PAD] = ((qr * c + _rope_swap(qr) * s) * MLA_SCALE).astype(BF16)
            k_ref[:, o:o + MLA_NOPE] = kv[:, o:o + MLA_NOPE].astype(BF16)
            k_ref[:, o + MLA_NOPE:o + MLA_QK_PAD] = krf
            v_ref[:, h * MLA_V:(h + 1) * MLA_V] = kv[:, o + MLA_NOPE:o + MLA_QK_PAD].astype(BF16)

    def row(w):
        return pl.BlockSpec((tm, w), lambda i: (i, 0))

    def full(shape):
        return pl.BlockSpec(shape, lambda i: (0, 0))

    return pl.pallas_call(
        body, name="mla_mid_fwd", grid=(T // tm,),
        in_specs=[row(PW), full((1, QL)), full((1, KVL)), full(w_uq.shape), full(w_ukv.shape),
                  row(128), row(128)],
        out_specs=(row(QL), row(KVL), row(H * MLA_QK_PAD), row(H * MLA_QK_PAD), row(H * MLA_V)),
        out_shape=(jax.ShapeDtypeStruct((T, QL), BF16), jax.ShapeDtypeStruct((T, KVL), BF16),
                   jax.ShapeDtypeStruct((T, H * MLA_QK_PAD), BF16),
                   jax.ShapeDtypeStruct((T, H * MLA_QK_PAD), BF16),
                   jax.ShapeDtypeStruct((T, H * MLA_V), BF16)),
        compiler_params=_cparams("parallel"))(proj, q_norm, kv_norm, w_uq, w_ukv, cc, ss)


def _mla_mid_bwd(proj, q_norm, kv_norm, w_uq, w_ukv, cc, ss, dq, dk, dv):
    T, PW = proj.shape
    QL, KVL = q_norm.shape[-1], kv_norm.shape[-1]
    H = MLA_HEADS
    tm = _mid_tile(T)
    nt = (((1,), (1,)), ((), ()))

    def body(p_ref, qn_ref, kn_ref, wq_ref, wkv_ref, cc_ref, ss_ref, dq_ref, dk_ref, dv_ref,
             dqp_ref, dkv_ref, dp_ref, dqn_ref, dkn_ref):
        @pl.when(pl.program_id(0) == 0)
        def _():
            dqn_ref[...] = jnp.zeros_like(dqn_ref)
            dkn_ref[...] = jnp.zeros_like(dkn_ref)

        c, s = cc_ref[...], ss_ref[...]
        dkr = jnp.zeros((tm, 128), F32)
        for h in range(H):
            o = h * MLA_QK_PAD
            dqp_ref[:, o:o + MLA_NOPE] = (dq_ref[:, o:o + MLA_NOPE] * MLA_SCALE).astype(BF16)
            dqr = dq_ref[:, o + MLA_NOPE:o + MLA_QK_PAD] * MLA_SCALE
            dqp_ref[:, o + MLA_NOPE:o + MLA_QK_PAD] = (dqr * c + _rope_swap(dqr * s)).astype(BF16)
            dkv_ref[:, o:o + MLA_NOPE] = dk_ref[:, o:o + MLA_NOPE].astype(BF16)
            dkv_ref[:, o + MLA_NOPE:o + MLA_QK_PAD] = dv_ref[:, h * MLA_V:(h + 1) * MLA_V].astype(BF16)
            dkr = dkr + dk_ref[:, o + MLA_NOPE:o + MLA_QK_PAD]
        dcqn = lax.dot_general(dqp_ref[...], wq_ref[...], nt, preferred_element_type=F32)
        dckvn = lax.dot_general(dkv_ref[...], wkv_ref[...], nt, preferred_element_type=F32)
        dcq, dqn = _rms_bwd_tile(p_ref[:, 0:QL], qn_ref[...], dcqn)
        dckv, dkn = _rms_bwd_tile(p_ref[:, QL:QL + KVL], kn_ref[...], dckvn)
        dp_ref[:, 0:QL] = dcq.astype(BF16)
        dp_ref[:, QL:QL + KVL] = dckv.astype(BF16)
        dp_ref[:, QL + KVL:QL + KVL + 128] = (dkr * c + _rope_swap(dkr * s)).astype(BF16)
        dqn_ref[...] += dqn
        dkn_ref[...] += dkn

    def row(w):
        return pl.BlockSpec((tm, w), lambda i: (i, 0))

    def full(shape):
        return pl.BlockSpec(shape, lambda i: (0, 0))

    return pl.pallas_call(
        body, name="mla_mid_bwd", grid=(T // tm,),
        in_specs=[row(PW), full((1, QL)), full((1, KVL)), full(w_uq.shape), full(w_ukv.shape),
                  row(128), row(128), row(H * MLA_QK_PAD), row(H * MLA_QK_PAD), row(H * MLA_V)],
        out_specs=(row(H * MLA_QK_PAD), row(H * MLA_QK_PAD), row(PW), full((1, QL)), full((1, KVL))),
        out_shape=(jax.ShapeDtypeStruct((T, H * MLA_QK_PAD), BF16),
                   jax.ShapeDtypeStruct((T, H * MLA_QK_PAD), BF16),
                   jax.ShapeDtypeStruct((T, PW), BF16),
                   jax.ShapeDtypeStruct((1, QL), F32), jax.ShapeDtypeStruct((1, KVL), F32)),
        compiler_params=_cparams("arbitrary"))(proj, q_norm, kv_norm, w_uq, w_ukv, cc, ss, dq, dk, dv)


def _attn_tile(T):
    return min(1024, T)


def _attn_pairs(n, by_key):
    if by_key:
        pairs = [(qi, ki) for ki in range(n) for qi in range(ki, n)]
    else:
        pairs = [(qi, ki) for qi in range(n) for ki in range(qi + 1)]
    return (jnp.asarray([p[0] for p in pairs], jnp.int32), jnp.asarray([p[1] for p in pairs], jnp.int32))


def _scores(q, k, diagonal):
    s = lax.dot_general(q, k, (((1,), (1,)), ((), ())), preferred_element_type=F32)
    if diagonal:
        rows = lax.broadcasted_iota(jnp.int32, s.shape, 0)
        cols = lax.broadcasted_iota(jnp.int32, s.shape, 1)
        s = jnp.where(rows >= cols, s, -jnp.inf)
    return s


def _attn_fwd(q, k, v):
    T = q.shape[0]
    H, DQ, DV = MLA_HEADS, MLA_QK_PAD, MLA_V
    tq = _attn_tile(T)
    nq = T // tq
    G = MLA_HEADS_PER_STEP
    qi_tab, ki_tab = _attn_pairs(nq, by_key=False)

    def body(qi_ref, ki_ref, q_ref, k_ref, v_ref, o_ref, lse_ref, *scratch):
        m_refs, l_refs, acc_refs = scratch[0:G], scratch[G:2 * G], scratch[2 * G:3 * G]
        p = pl.program_id(1)
        qi, ki = qi_ref[p], ki_ref[p]

        @pl.when(ki == 0)
        def _():
            for g in range(G):
                m_refs[g][...] = jnp.full_like(m_refs[g], -jnp.inf)
                l_refs[g][...] = jnp.zeros_like(l_refs[g])
                acc_refs[g][...] = jnp.zeros_like(acc_refs[g])

        def update(ks, qr, masked):
            for g in range(G):
                qs, vs = slice(g * DQ, (g + 1) * DQ), slice(g * DV, (g + 1) * DV)
                st = _scores(k_ref[ks, qs], q_ref[qr, qs], False)
                if masked:
                    key = ks.start + lax.broadcasted_iota(jnp.int32, st.shape, 0)
                    qry = qr.start + lax.broadcasted_iota(jnp.int32, st.shape, 1)
                    st = jnp.where(qry >= key, st, -jnp.inf)
                m_prev = m_refs[g][:, qr]
                m_new = jnp.maximum(m_prev, jnp.max(st, axis=0, keepdims=True))
                alpha = jnp.exp(m_prev - m_new)
                pt = jnp.exp(st - m_new)
                l_refs[g][:, qr] = alpha * l_refs[g][:, qr] + jnp.sum(pt, axis=0, keepdims=True)
                acc_refs[g][:, qr] = alpha * acc_refs[g][:, qr] + lax.dot_general(
                    v_ref[ks, vs], pt.astype(BF16), (((0,), (0,)), ((), ())), preferred_element_type=F32)
                m_refs[g][:, qr] = m_new

        whole, half = slice(0, tq), tq // 2

        @pl.when(ki < qi)
        def _():
            update(whole, whole, False)

        @pl.when(ki == qi)
        def _():
            update(slice(0, half), whole, True)
            update(slice(half, tq), slice(half, tq), True)
            for g in range(G):
                vs = slice(g * DV, (g + 1) * DV)
                o_ref[:, vs] = jnp.transpose(acc_refs[g][...] / l_refs[g][...]).astype(BF16)
                lse_ref[g] = m_refs[g][...] + jnp.log(l_refs[g][...])

    return pl.pallas_call(
        body, name="attn_fwd",
        grid_spec=pltpu.PrefetchScalarGridSpec(
            num_scalar_prefetch=2, grid=(H // G, int(qi_tab.shape[0])),
            in_specs=[pl.BlockSpec((tq, G * DQ), lambda h, p, qt, kt: (qt[p], h)),
                      pl.BlockSpec((tq, G * DQ), lambda h, p, qt, kt: (kt[p], h)),
                      pl.BlockSpec((tq, G * DV), lambda h, p, qt, kt: (kt[p], h))],
            out_specs=(pl.BlockSpec((tq, G * DV), lambda h, p, qt, kt: (qt[p], h)),
                       pl.BlockSpec((G, 1, tq), lambda h, p, qt, kt: (h, 0, qt[p]))),
            scratch_shapes=([pltpu.VMEM((1, tq), F32)] * (2 * G) + [pltpu.VMEM((DV, tq), F32)] * G)),
        out_shape=(jax.ShapeDtypeStruct((T, H * DV), BF16), jax.ShapeDtypeStruct((H, 1, T), F32)),
        compiler_params=_cparams("parallel", "arbitrary"))(qi_tab, ki_tab, q, k, v)


def _attn_bwd(q, k, v, o, do, lse):
    T = q.shape[0]
    H, DQ, DV = MLA_HEADS, MLA_QK_PAD, MLA_V
    tq = _attn_tile(T)
    nq = T // tq
    tn = (((0,), (0,)), ((), ()))
    nt = (((1,), (1,)), ((), ()))
    G = MLA_HEADS_PER_STEP
    qi_tab, ki_tab = _attn_pairs(nq, by_key=True)

    def body(qi_ref, ki_ref, q_ref, k_ref, v_ref, o_ref, do_ref, lse_ref, dq_ref, dk_ref, dv_ref,
             dk_acc, dv_acc):
        p = pl.program_id(1)
        qi, ki = qi_ref[p], ki_ref[p]

        @pl.when(p == 0)
        def _():
            dq_ref[...] = jnp.zeros_like(dq_ref)

        @pl.when(qi == ki)
        def _():
            dk_acc[...] = jnp.zeros_like(dk_acc)
            dv_acc[...] = jnp.zeros_like(dv_acc)

        def step(ks, qr, masked):
            rows = pl.ds(pl.multiple_of(qi * tq + qr.start, qr.stop - qr.start), qr.stop - qr.start)
            for g in range(G):
                qs, vs = slice(g * DQ, (g + 1) * DQ), slice(g * DV, (g + 1) * DV)
                dof = do_ref[qr, vs]
                delta = jnp.sum(jnp.transpose(dof.astype(F32) * o_ref[qr, vs].astype(F32)), axis=0,
                                keepdims=True)
                st = _scores(k_ref[ks, qs], q_ref[qr, qs], False)
                if masked:
                    key = ks.start + lax.broadcasted_iota(jnp.int32, st.shape, 0)
                    qry = qr.start + lax.broadcasted_iota(jnp.int32, st.shape, 1)
                    st = jnp.where(qry >= key, st, -jnp.inf)
                pt = jnp.exp(st - lse_ref[g][:, qr])
                dpt = lax.dot_general(v_ref[ks, vs], dof, nt, preferred_element_type=F32)
                dst = (pt * (dpt - delta)).astype(BF16)
                dv_acc[ks, vs] += jnp.dot(pt.astype(BF16), dof, preferred_element_type=F32)
                dk_acc[ks, qs] += jnp.dot(dst, q_ref[qr, qs], preferred_element_type=F32)
                dq_ref[rows, qs] += lax.dot_general(dst, k_ref[ks, qs], tn, preferred_element_type=F32)

        whole, half = slice(0, tq), tq // 2

        @pl.when(qi == ki)
        def _():
            step(slice(0, half), whole, True)
            step(slice(half, tq), slice(half, tq), True)

        @pl.when(qi > ki)
        def _():
            step(whole, whole, False)

        @pl.when(qi == nq - 1)
        def _():
            dk_ref[...] = dk_acc[...]
            dv_ref[...] = dv_acc[...]

    qspec = pl.BlockSpec((tq, G * DQ), lambda h, p, qt, kt: (qt[p], h))
    ospec = pl.BlockSpec((tq, G * DV), lambda h, p, qt, kt: (qt[p], h))
    kspec = pl.BlockSpec((tq, G * DQ), lambda h, p, qt, kt: (kt[p], h))
    vspec = pl.BlockSpec((tq, G * DV), lambda h, p, qt, kt: (kt[p], h))
    return pl.pallas_call(
        body, name="attn_bwd",
        grid_spec=pltpu.PrefetchScalarGridSpec(
            num_scalar_prefetch=2, grid=(H // G, int(qi_tab.shape[0])),
            in_specs=[qspec, kspec, vspec, ospec, ospec,
                      pl.BlockSpec((G, 1, tq), lambda h, p, qt, kt: (h, 0, qt[p]))],
            out_specs=(pl.BlockSpec((T, G * DQ), lambda h, p, qt, kt: (0, h)), kspec, vspec),
            scratch_shapes=[pltpu.VMEM((tq, G * DQ), F32), pltpu.VMEM((tq, G * DV), F32)]),
        out_shape=(jax.ShapeDtypeStruct((T, H * DQ), F32), jax.ShapeDtypeStruct((T, H * DQ), F32),
                   jax.ShapeDtypeStruct((T, H * DV), F32)),
        compiler_params=_cparams("parallel", "arbitrary"))(qi_tab, ki_tab, q, k, v, o, do, lse)


def _mla_fwd(a, w, cc, ss, wbuf, pk, slot, res):
    D = a.shape[1]
    by_k, _, _ = _row_sharded(pk, "mla_w_o", slot, D // N_CHIPS)
    proj = _mm(a, w["w_in"], name="mla_in")
    cqn, ckvn, q, k, v = _mla_mid_fwd(proj, w["q_norm"], w["kv_norm"], w["w_uq"], w["w_ukv"], cc, ss)
    o, lse = _attn_fwd(q, k, v)
    m, h_new, a_next = _mm(o, wbuf, n=D, b_map=by_k, tm=1024, tk=D // N_CHIPS, tn=D, epi="resnorm", extra=res,
                           name="mla_out_res")
    return m, (a, proj, cqn, ckvn, q, k, v, o, lse), h_new, a_next


def _mla_bwd(dm, saved, w, cc, ss, wbuf, gbuf, pk, slot):
    a, proj, cqn, ckvn, q, k, v, o, lse = saved
    D = a.shape[1]
    _, by_n, by_m = _row_sharded(pk, "mla_w_o", slot, D // N_CHIPS)
    do = _mm(dm, wbuf, tb=True, n=o.shape[1], b_map=by_n, tm=2048, tn=D // N_CHIPS, tk=D, out_dtype=BF16,
             name="mla_out_dx")
    gbuf = _mm(o, dm, ta=True, into=gbuf, o_map=by_m, tm=D // N_CHIPS, tn=D, tk=2048, name="mla_out_dw")
    dq, dk, dv = _attn_bwd(q, k, v, o, do, lse)
    dqp, dkv, dproj, dqn, dkn = _mla_mid_bwd(proj, w["q_norm"], w["kv_norm"], w["w_uq"], w["w_ukv"],
                                             cc, ss, dq, dk, dv)
    dw_uq = _mm(cqn, dqp, ta=True, out_dtype=BF16, name="mla_uq_dw")
    dw_ukv = _mm(ckvn, dkv, ta=True, out_dtype=BF16, name="mla_ukv_dw")
    dw_in = _mm(a, dproj, ta=True, out_dtype=BF16, name="mla_in_dw")
    da = _mm(dproj, w["w_in"], tb=True, name="mla_in_dx")
    return da, gbuf, dict(w_in=dw_in, w_uq=dw_uq, w_ukv=dw_ukv, q_norm=dqn, kv_norm=dkn)


def _split_dot(mat, x, parts):
    acc = None
    rem = x
    for _ in range(parts):
        piece = rem.astype(BF16)
        term = jnp.dot(mat, piece, preferred_element_type=F32)
        acc = term if acc is None else acc + term
        rem = rem - piece.astype(F32)
    return acc


def _chunk_sums(cum, rel, rest, logf):
    return tuple(_split_dot(m.astype(BF16), logf, 3) for m in (cum, rel, rest))


def _chunk_mats(tb):
    C = HGRN_CHUNK
    assert C & (C - 1) == 0
    r = lax.broadcasted_iota(jnp.int32, (tb, tb), 0)
    s = lax.broadcasted_iota(jnp.int32, (tb, tb), 1)
    start = r & ~(C - 1)
    same = start == (s & ~(C - 1))
    ref = start + C // 2
    last = start + C - 1
    one, zero = jnp.float32(1.0), jnp.float32(0.0)
    cum = jnp.where(same & (s <= r), one, zero)
    rel = cum - jnp.where(same & (s <= ref), one, zero)
    rest = jnp.where(same & (s > r) & (s <= last), one, zero)
    rev = jnp.where(same & (s >= r), one, zero)
    ones = jnp.where(same, one, zero)
    causal = same & (s <= r)
    return cum, rel, rest, rev, ones, causal


def _hgrn_gates(p_ref, lb, HK):
    qx = p_ref[:, 0:HK]
    fx = p_ref[:, HK:2 * HK]
    sf = _sigmoid(fx)
    f = lb + (1.0 - lb) * sf
    sq = _sigmoid(qx)
    return qx, sq, qx * sq, sf, f, 1.0 - f, jnp.log(f)


def _hgrn_fwd(proj, lb, o_norm):
    T = proj.shape[0]
    H, C = HGRN_HEADS, HGRN_CHUNK
    HK = proj.shape[1] // 4
    DK = HK // H
    tb = min(HGRN_BLOCK, T)
    ncb = tb // C
    nt = (((1,), (1,)), ((), ()))
    tn = (((0,), (0,)), ((), ()))

    def body(p_ref, lb_ref, on_ref, y_ref, o_ref, st_ref, state, oacc):
        @pl.when(pl.program_id(0) == 0)
        def _():
            state[...] = jnp.zeros_like(state)

        cum, rel, rest, _, _, causal = _chunk_mats(tb)
        _, _, q, _, f, k, logf = _hgrn_gates(p_ref, lb_ref[...], HK)
        b, brel, brest = _chunk_sums(cum, rel, rest, logf)
        eb = jnp.exp(b)
        q_rel = (q * jnp.exp(brel)).astype(BF16)
        k_rel = (k * jnp.exp(-brel)).astype(BF16)
        q_dec = (q * eb).astype(BF16)
        k_dec = (k * jnp.exp(brest)).astype(BF16)
        v = p_ref[:, 2 * HK:3 * HK].astype(BF16)
        for h in range(H):
            hs = slice(h * DK, (h + 1) * DK)
            a = lax.dot_general(q_rel[:, hs], k_rel[:, hs], nt, preferred_element_type=F32)
            a = jnp.where(causal, a, 0.0).astype(BF16)
            oacc[:, hs] = jnp.dot(a, v[:, hs], preferred_element_type=F32)
            for j in range(ncb):
                rs = slice(j * C, (j + 1) * C)
                st = state[h]
                st_ref[j, h] = st
                oacc[rs, hs] += lax.dot_general(q_dec[rs, hs], st.astype(BF16), nt,
                                                preferred_element_type=F32)
                dec = jnp.exp(jnp.sum(logf[rs, hs], axis=0, keepdims=True))
                state[h] = dec * st + lax.dot_general(v[rs, hs], k_dec[rs, hs], tn,
                                                      preferred_element_type=F32)
        o = oacc[...]
        o_ref[...] = o
        gx = p_ref[:, 3 * HK:4 * HK]
        gate = gx * _sigmoid(gx)
        for h in range(H):
            hs = slice(h * DK, (h + 1) * DK)
            oh = o[:, hs]
            y_ref[:, hs] = (oh * _rms_rstd(oh) * on_ref[...] * gate[:, hs]).astype(BF16)

    return pl.pallas_call(
        body, name="hgrn_fwd", grid=(T // tb,),
        in_specs=[pl.BlockSpec((tb, 4 * HK), lambda i: (i, 0)),
                  pl.BlockSpec((1, HK), lambda i: (0, 0)),
                  pl.BlockSpec((1, DK), lambda i: (0, 0))],
        out_specs=(pl.BlockSpec((tb, HK), lambda i: (i, 0)),
                   pl.BlockSpec((tb, HK), lambda i: (i, 0)),
                   pl.BlockSpec((ncb, H, DK, DK), lambda i: (i, 0, 0, 0))),
        out_shape=(jax.ShapeDtypeStruct((T, HK), BF16), jax.ShapeDtypeStruct((T, HK), F32),
                   jax.ShapeDtypeStruct((T // C, H, DK, DK), F32)),
        scratch_shapes=[pltpu.VMEM((H, DK, DK), F32), pltpu.VMEM((tb, HK), F32)],
        compiler_params=_cparams("arbitrary"))(proj, lb, o_norm)


def _hgrn_bwd(proj, lb, o_norm, o, states, dy):
    T = proj.shape[0]
    H, C = HGRN_HEADS, HGRN_CHUNK
    HK = proj.shape[1] // 4
    DK = HK // H
    tb = min(HGRN_BLOCK, T)
    ncb = tb // C
    nb = T // tb
    nt = (((1,), (1,)), ((), ()))
    tn = (((0,), (0,)), ((), ()))

    def body(p_ref, lb_ref, on_ref, o_ref, st_ref, dy_ref, dp_ref, dlb_ref, don_ref,
             dstate, dqr_s, dkr_s, dqd_s, dkd_s, dv_s, do_s, e_s):
        @pl.when(pl.program_id(0) == 0)
        def _():
            dstate[...] = jnp.zeros_like(dstate)
            dlb_ref[...] = jnp.zeros_like(dlb_ref)
            don_ref[...] = jnp.zeros_like(don_ref)

        cum, rel, rest, rev, ones, causal = _chunk_mats(tb)
        lb = lb_ref[...]
        qx, sq, q, sf, f, k, logf = _hgrn_gates(p_ref, lb, HK)
        b, brel, brest = _chunk_sums(cum, rel, rest, logf)
        eb = jnp.exp(b)
        erel = jnp.exp(brel)
        enrel = jnp.exp(-brel)
        erest = jnp.exp(brest)
        q_rel_f, k_rel_f, q_dec_f, k_dec_f = q * erel, k * enrel, q * eb, k * erest
        q_rel, k_rel = q_rel_f.astype(BF16), k_rel_f.astype(BF16)
        q_dec, k_dec = q_dec_f.astype(BF16), k_dec_f.astype(BF16)
        v = p_ref[:, 2 * HK:3 * HK].astype(BF16)

        gx = p_ref[:, 3 * HK:4 * HK]
        sg = _sigmoid(gx)
        gate = gx * sg
        dy = dy_ref[...]
        ov = o_ref[...]
        on = on_ref[...]
        don = jnp.zeros((1, DK), F32)
        for h in range(H):
            hs = slice(h * DK, (h + 1) * DK)
            oh = ov[:, hs]
            r = _rms_rstd(oh)
            xh = oh * r
            d_on = dy[:, hs] * gate[:, hs]
            don = don + jnp.sum(d_on * xh, axis=0, keepdims=True)
            u = d_on * on
            do_s[:, hs] = r * (u - xh * jnp.mean(u * xh, axis=-1, keepdims=True))
            dp_ref[:, 3 * HK + h * DK:3 * HK + (h + 1) * DK] = (
                dy[:, hs] * xh * on * (sg[:, hs] * (1.0 + gx[:, hs] * (1.0 - sg[:, hs])))).astype(BF16)
        don_ref[...] += don

        for h in range(H):
            hs = slice(h * DK, (h + 1) * DK)
            doh = do_s[:, hs].astype(BF16)
            a = lax.dot_general(q_rel[:, hs], k_rel[:, hs], nt, preferred_element_type=F32)
            a = jnp.where(causal, a, 0.0).astype(BF16)
            da = lax.dot_general(doh, v[:, hs], nt, preferred_element_type=F32)
            da = jnp.where(causal, da, 0.0).astype(BF16)
            dv_s[:, hs] = lax.dot_general(a, doh, tn, preferred_element_type=F32)
            dqr_s[:, hs] = jnp.dot(da, k_rel[:, hs], preferred_element_type=F32)
            dkr_s[:, hs] = lax.dot_general(da, q_rel[:, hs], tn, preferred_element_type=F32)
            for j in reversed(range(ncb)):
                rs = slice(j * C, (j + 1) * C)
                dst = dstate[h]
                dstb = dst.astype(BF16)
                st = st_ref[j, h]
                dkd_s[rs, hs] = jnp.dot(v[rs, hs], dstb, preferred_element_type=F32)
                dv_s[rs, hs] += lax.dot_general(k_dec[rs, hs], dstb, nt, preferred_element_type=F32)
                dec = jnp.exp(jnp.sum(logf[rs, hs], axis=0, keepdims=True))
                e_s[rs, hs] = jnp.broadcast_to(jnp.sum(dst * st, axis=0, keepdims=True) * dec, (C, DK))
                dqd_s[rs, hs] = jnp.dot(doh[rs], st.astype(BF16), preferred_element_type=F32)
                dstate[h] = dec * dst + lax.dot_general(doh[rs], q_dec[rs, hs], tn,
                                                        preferred_element_type=F32)

        dqr, dkr, dqd, dkd = dqr_s[...], dkr_s[...], dqd_s[...], dkd_s[...]
        kdk = dkd * k_dec_f
        db = dqr * q_rel_f - dkr * k_rel_f + dqd * q_dec_f - kdk
        dlogf = _split_dot(rev.astype(BF16), db, 2) + _split_dot(ones.astype(BF16), kdk, 2) + e_s[...]
        dk = dkr * enrel + dkd * erest
        df = dlogf / f - dk
        dlb_ref[...] += jnp.sum(df * (1.0 - sf), axis=0, keepdims=True)
        dq = dqr * erel + dqd * eb
        dp_ref[:, 0:HK] = (dq * (sq * (1.0 + qx * (1.0 - sq)))).astype(BF16)
        dp_ref[:, HK:2 * HK] = (df * (1.0 - lb) * sf * (1.0 - sf)).astype(BF16)
        dp_ref[:, 2 * HK:3 * HK] = dv_s[...].astype(BF16)

    rev_row = lambda w: pl.BlockSpec((tb, w), lambda i: (nb - 1 - i, 0))
    vec = lambda w: pl.BlockSpec((1, w), lambda i: (0, 0))
    scr = pltpu.VMEM((tb, HK), F32)
    return pl.pallas_call(
        body, name="hgrn_bwd", grid=(nb,),
        in_specs=[rev_row(4 * HK), vec(HK), vec(DK), rev_row(HK),
                  pl.BlockSpec((ncb, H, DK, DK), lambda i: (nb - 1 - i, 0, 0, 0)), rev_row(HK)],
        out_specs=(rev_row(4 * HK), vec(HK), vec(DK)),
        out_shape=(jax.ShapeDtypeStruct((T, 4 * HK), BF16), jax.ShapeDtypeStruct((1, HK), F32),
                   jax.ShapeDtypeStruct((1, DK), F32)),
        scratch_shapes=[pltpu.VMEM((H, DK, DK), F32), scr, scr, scr, scr, scr, scr, scr],
        compiler_params=_cparams("arbitrary"))(proj, lb, o_norm, o, states, dy)


def _hgrn_layer_fwd(a, o_norm, lb, wbuf, pk, slot, res):
    D = a.shape[1]
    in_by_n, _ = _col_sharded(pk, "hgrn_w_in", slot, D)
    out_by_k, _, _ = _row_sharded(pk, "hgrn_w_o", slot, D // N_CHIPS)
    proj = _mm(a, wbuf, n=4 * D, b_map=in_by_n, tk=D, tn=D, name="hgrn_in")
    y, o, states = _hgrn_fwd(proj, lb, o_norm)
    m, h_new, a_next = _mm(y, wbuf, n=D, b_map=out_by_k, tm=1024, tk=D // N_CHIPS, tn=D, epi="resnorm",
                           extra=res, name="hgrn_out_res")
    return m, (a, proj, y, o, states), h_new, a_next


def _hgrn_layer_bwd(dm, saved, o_norm, lb, wbuf, gbuf, pk, slot):
    a, proj, y, o, states = saved
    D = a.shape[1]
    in_by_n, in_by_k = _col_sharded(pk, "hgrn_w_in", slot, D)
    _, out_by_n, out_by_m = _row_sharded(pk, "hgrn_w_o", slot, D // N_CHIPS)
    dy = _mm(dm, wbuf, tb=True, n=y.shape[1], b_map=out_by_n, tm=2048, tn=D // N_CHIPS, tk=D,
             name="hgrn_out_dx")
    gbuf = _mm(y, dm, ta=True, into=gbuf, o_map=out_by_m, tm=D // N_CHIPS, tn=D, tk=2048, name="hgrn_out_dw")
    dproj, dlb, don = _hgrn_bwd(proj, lb, o_norm, o, states, dy)
    gbuf = _mm(a, dproj, ta=True, into=gbuf, o_map=in_by_n, tm=D, tn=D, name="hgrn_in_dw")
    da = _mm(dproj, wbuf, tb=True, n=D, b_map=in_by_k, tn=D, tk=D, name="hgrn_in_dx")
    return da, gbuf, dict(o_norm=don, lb=dlb)


def _lower_bounds(lb_logits):
    p = jax.nn.softmax(lb_logits.astype(F32), axis=0)
    return jnp.cumsum(p, axis=0) - p[0]


def _rope_tables(positions):
    inv_freq = jnp.power(ROPE_BASE, -jnp.arange(0, MLA_ROPE, 2, dtype=F32) / MLA_ROPE)
    ang = positions.astype(F32)[:, None] * inv_freq
    cos, sin = jnp.cos(ang), jnp.sin(ang)
    zero = jnp.zeros((positions.shape[0], 128 - MLA_ROPE), F32)
    return (jnp.concatenate([cos, cos, zero], axis=-1), jnp.concatenate([-sin, sin, zero], axis=-1))


def _pad_mla_weights(w_in, w_uq):
    w_in_p = jnp.pad(w_in, ((0, 0), (0, 0), (0, 128 - MLA_ROPE)))
    n, ql, _ = w_uq.shape
    w_uq_p = jnp.pad(w_uq.reshape(n, ql, MLA_HEADS, MLA_NOPE + MLA_ROPE),
                     ((0, 0), (0, 0), (0, 0), (0, MLA_QK_PAD - MLA_NOPE - MLA_ROPE)))
    return w_in_p, w_uq_p.reshape(n, ql, MLA_HEADS * MLA_QK_PAD)


def _local_step(x, positions, target, small, prefetch, fetch, gbufs, emit, emit_mlp):
    T, D = x.shape
    lbounds, lb_vjp = jax.vjp(_lower_bounds, small["hgrn_lb_logits"])
    cc, ss = _rope_tables(positions)
    fetched = {0: fetch(0, None)}
    gains = fetched[0]["gains"]
    tick = [jnp.zeros((), F32)]

    def g(layer, i):
        return gains[layer, i][None, :] + tick[0]

    def mla_weights(layer):
        f = fetched[layer]
        w_in_p, w_uq_p = _pad_mla_weights(f["w_in"][None], f["w_uq"][None])
        slot = layer // 2
        return dict(w_in=w_in_p[0], w_uq=w_uq_p[0], w_ukv=f["w_ukv"],
                    q_norm=small["mla_q_norm"][slot][None, :], kv_norm=small["mla_kv_norm"][slot][None, :])

    saved = []
    h = x
    a = _prenorm_fwd(x, g(0, 0))
    dy = sq = None
    for layer in range(DEPTH):
        slot = layer // 2
        if layer not in fetched:
            fetched[layer] = fetch(layer, a)
        wbuf, pk = fetched[layer]["wbuf"], fetched[layer]["pk"]
        res = (h, g(layer, 1), g(layer, 2))
        if layer % 2 == 0:
            m, mix_saved, h1, a2 = _mla_fwd(a, mla_weights(layer), cc, ss, wbuf, pk, slot, res)
        else:
            m, mix_saved, h1, a2 = _hgrn_layer_fwd(a, small["hgrn_o_norm"][slot][None, :],
                                                   lbounds[layer][None, :], wbuf, pk, slot, res)
        if layer + 1 < DEPTH:
            prefetch(layer + 1, a2)
            u, mlp_saved, h2, a = _mlp_fwd(a2, wbuf, pk, layer, (h1, g(layer, 3), g(layer + 1, 0)))
        else:
            u, mlp_saved, h2, _ = _mlp_fwd(a2, wbuf, pk, layer, None)
            dy, sq = _resnorm_loss(h1, u, g(layer, 3), target)
        saved.append((h, m, h1, u, mix_saved, mlp_saved))
        h = h2

    n_mla, n_hgrn = (DEPTH + 1) // 2, DEPTH // 2
    dgains = [[None] * 4 for _ in range(DEPTH)]
    gw = {k: [None] * n_mla for k in ("mla_w_in", "mla_w_uq", "mla_w_ukv", "mla_q_norm", "mla_kv_norm")}
    gw["hgrn_o_norm"] = [None] * n_hgrn
    dlb = [jnp.zeros((1, lbounds.shape[1]), F32) for _ in range(DEPTH)]
    dh = dy
    da_next = None
    for layer in reversed(range(DEPTH)):
        h0, m, h1, u, mix_saved, mlp_saved = saved[layer]
        slot = layer // 2
        wbuf, pk, gbuf = fetched[layer]["wbuf"], fetched[layer]["pk"], gbufs[layer]
        if da_next is None:
            du, dgains[layer][3] = _resnorm_bwd(u, g(layer, 3), dh, name="resnorm_bwd_last")
            t = dh
        else:
            h2 = saved[layer + 1][0]
            t, du, dgains[layer][3], dgains[layer + 1][0] = _resnorm_bwd(
                u, g(layer, 3), dh, h2, da_next, g(layer + 1, 0), name="resnorm_bwd_mlp")
        da2, gbuf = _mlp_bwd(du, mlp_saved, wbuf, gbuf, pk, layer)
        if layer == 0:
            gbuf = emit_mlp(layer, gbuf)
        t, dm, dgains[layer][1], dgains[layer][2] = _resnorm_bwd(
            m, g(layer, 1), t, h1, da2, g(layer, 2), name="resnorm_bwd_mix")
        if layer % 2 == 0:
            da_next, gbuf, mg = _mla_bwd(dm, mix_saved, mla_weights(layer), cc, ss, wbuf, gbuf, pk, slot)
            ql = mg["q_norm"].shape[-1]
            kvl = mg["kv_norm"].shape[-1]
            gw["mla_w_in"][slot] = mg["w_in"][:, :ql + kvl + MLA_ROPE]
            gw["mla_w_uq"][slot] = mg["w_uq"].reshape(ql, MLA_HEADS, MLA_QK_PAD)[
                :, :, :MLA_NOPE + MLA_ROPE].reshape(ql, MLA_HEADS * (MLA_NOPE + MLA_ROPE))
            gw["mla_w_ukv"][slot] = mg["w_ukv"]
            gw["mla_q_norm"][slot] = mg["q_norm"][0]
            gw["mla_kv_norm"][slot] = mg["kv_norm"][0]
        else:
            da_next, gbuf, hg = _hgrn_layer_bwd(dm, mix_saved, small["hgrn_o_norm"][slot][None, :],
                                                lbounds[layer][None, :], wbuf, gbuf, pk, slot)
            gw["hgrn_o_norm"][slot] = hg["o_norm"][0]
            dlb[layer] = hg["lb"]
        dh = t
        if layer > 0:
            mine = ({k: gw[k][slot] for k in ("mla_w_in", "mla_w_uq", "mla_w_ukv")} if layer % 2 == 0 else {})
            tick[0] = emit(layer, gbuf, mine)
        else:
            gbuf0 = gbuf
    grad_x, dgains[0][0] = _prenorm_bwd(x, g(0, 0), dh, da_next)

    last = {k: gw[k][0] for k in ("mla_w_in", "mla_w_uq", "mla_w_ukv")}
    last.update({k: jnp.stack(gw[k]) for k in ("mla_q_norm", "mla_kv_norm", "hgrn_o_norm")})
    last["norm_gains"] = jnp.stack([jnp.concatenate(row, axis=0) for row in dgains])
    (last["hgrn_lb_logits"],) = lb_vjp(jnp.concatenate(dlb, axis=0))
    emit(0, gbuf0, last)
    return sq, grad_x


def _size(shape):
    n = 1
    for d in shape:
        n *= d
    return n


def _piece_rows(shape):
    return -(-_size(shape) // PACK_W)


def _packed_misc_rows(shapes):
    return sum(_piece_rows(s) for s in shapes)


def _cast_into(src, buf, row, name):
    rows, W = src.shape
    tr = min(256, rows)
    assert rows % tr == 0 and row % tr == 0

    def body(s_ref, b_ref, o_ref):
        o_ref[...] = s_ref[...].astype(BF16)

    return pl.pallas_call(
        body, name=name, grid=(rows // tr,),
        in_specs=[pl.BlockSpec((tr, W), lambda i: (i, 0)), pl.BlockSpec(memory_space=pl.ANY)],
        out_specs=pl.BlockSpec((tr, W), lambda i: (row // tr + i, 0)),
        out_shape=jax.ShapeDtypeStruct(buf.shape, buf.dtype), input_output_aliases={1: 0},
        compiler_params=_cparams("parallel"))(src, buf)


def _pack_blocks(pieces, rows, dtype):
    blocks, used = [], 0
    for p in pieces:
        flat = p.astype(dtype).reshape(-1)
        r = _piece_rows(p.shape)
        if r * PACK_W != flat.shape[0]:
            flat = jnp.pad(flat, (0, r * PACK_W - flat.shape[0]))
        blocks.append(flat.reshape(r, PACK_W))
        used += r
    if rows > used:
        blocks.append(jnp.zeros((rows - used, PACK_W), dtype))
    return blocks


def _unpack(buf, shapes):
    out, off = [], 0
    for shp in shapes:
        r = _piece_rows(shp)
        piece = buf[off:off + r]
        if r * PACK_W != _size(shp):
            piece = piece.reshape(-1)[:_size(shp)]
        out.append(piece.reshape(shp))
        off += r
    return out


def _mesh_place():
    x, y, c = lax.axis_index("x"), lax.axis_index("y"), lax.axis_index("c")
    chips = [(1 - x, y), (x, 1 - y), (1 - x, 1 - y)]
    return x, y, c, chips


_HBM = pl.BlockSpec(memory_space=pltpu.HBM)


def _share_reduced(q, name="grads_share_reduced"):
    rh, W = q.shape

    def body(q_ref, out_ref, send_sem, recv_sem):
        x, y, c, _ = _mesh_place()
        cp = pltpu.make_async_remote_copy(src_ref=q_ref, dst_ref=out_ref.at[c], send_sem=send_sem,
                                          recv_sem=recv_sem, device_id=(x, y, 1 - c), device_id_type=MESH)
        cp.start()
        cp.wait()

    out = pl.pallas_call(
        body, name=name, in_specs=[_HBM], out_specs=_HBM,
        out_shape=jax.ShapeDtypeStruct((2, rh, W), q.dtype),
        scratch_shapes=[pltpu.SemaphoreType.DMA, pltpu.SemaphoreType.DMA],
    )(q)
    return out


def _sum_chips(parts, own, own_row0, which, name, out_dtype=F32):
    n, rh, W = parts.shape
    tr = PACK_TILE
    assert own_row0 % tr == 0
    if own.ndim == 3:
        own_spec = pl.BlockSpec((None, tr, W), lambda i, w_ref: (w_ref[0], own_row0 // tr + i, 0))
    else:
        own_spec = pl.BlockSpec((tr, W), lambda i, w_ref: (own_row0 // tr + i, 0))

    def body(w_ref, p_ref, own_ref, o_ref):
        mine = own_ref[...].astype(F32)
        acc = None
        for j in range(n):
            term = jnp.where(w_ref[0] == j, mine, p_ref[j].astype(F32))
            acc = term if acc is None else acc + term
        o_ref[...] = acc.astype(out_dtype)

    return pl.pallas_call(
        body, name=name,
        grid_spec=pltpu.PrefetchScalarGridSpec(
            num_scalar_prefetch=1, grid=(rh // tr,),
            in_specs=[pl.BlockSpec((n, tr, W), lambda i, w_ref: (0, i, 0)), own_spec],
            out_specs=pl.BlockSpec((tr, W), lambda i, w_ref: (i, 0))),
        out_shape=jax.ShapeDtypeStruct((rh, W), out_dtype),
        compiler_params=_cparams("parallel"))(jnp.reshape(which, (1,)).astype(jnp.int32), parts, own)


_SEM = pl.BlockSpec(memory_space=pltpu.SEMAPHORE)
_ASYNC = pltpu.CompilerParams(has_side_effects=pltpu.SideEffectType.DATAFLOW_SIDE_EFFECTING)


def _hbm(a):
    return pltpu.with_memory_space_constraint(a, pltpu.HBM)


def _gather_copies(w_ref, land_ref, send_sems, recv_sems):
    x, y, c, chips = _mesh_place()
    me = 2 * x + y
    rh = w_ref.shape[0] // 2
    rows = pl.ds(pl.multiple_of(c * rh, 16), rh)
    return [pltpu.make_async_remote_copy(
        src_ref=w_ref.at[rows], dst_ref=land_ref.at[me, rows], send_sem=send_sems.at[r],
        recv_sem=recv_sems.at[r], device_id=(px, py, c), device_id_type=MESH)
        for r, (px, py) in enumerate(chips)]


def _scatter_copies(g_ref, land_ref, send_sems, recv_sems, row0):
    x, y, c, chips = _mesh_place()
    me = 2 * x + y
    rows = pl.ds(row0, land_ref.shape[1])
    return [pltpu.make_async_remote_copy(
        src_ref=g_ref.at[2 * px + py, rows], dst_ref=land_ref.at[me], send_sem=send_sems.at[r],
        recv_sem=recv_sems.at[r], device_id=(px, py, c), device_id_type=MESH)
        for r, (px, py) in enumerate(chips)]


def _halves_copies(land_ref, send_sems, recv_sems):
    x, y, c, chips = _mesh_place()
    rh = land_ref.shape[1] // 2
    rows = pl.ds(pl.multiple_of(c * rh, 16), rh)
    return [pltpu.make_async_remote_copy(
        src_ref=land_ref.at[2 * px + py, rows], dst_ref=land_ref.at[2 * px + py, rows], send_sem=send_sems.at[r],
        recv_sem=recv_sems.at[r], device_id=(x, y, 1 - c), device_id_type=MESH)
        for r, (px, py) in enumerate(chips)]


def _halves_to_sibling(land, name):
    def body(l_ref, o_ref, send_sems, recv_sems):
        copies = _halves_copies(o_ref, send_sems, recv_sems)
        for cp in copies:
            cp.start()
        for cp in copies:
            cp.wait()

    return pl.pallas_call(
        body, name=name, in_specs=[_HBM], out_specs=_HBM, out_shape=jax.ShapeDtypeStruct(land.shape, land.dtype),
        scratch_shapes=[pltpu.SemaphoreType.DMA((3,)), pltpu.SemaphoreType.DMA((3,))],
        input_output_aliases={0: 0})(land)


def _halves_start(land, name):
    def body(l_ref, send_sems, recv_sems, land_thru, token):
        for cp in _halves_copies(l_ref, send_sems, recv_sems):
            cp.start()
        token[...] = jnp.zeros_like(token)

    return pl.pallas_call(
        body, name=name,
        out_shape=(pltpu.SemaphoreType.DMA((3,)), pltpu.SemaphoreType.DMA((3,)), pltpu.HBM(land.shape, land.dtype),
                   jax.ShapeDtypeStruct((8, 128), F32)),
        in_specs=(_HBM,), out_specs=(_SEM, _SEM, _HBM, pl.BlockSpec(memory_space=pltpu.VMEM)),
        input_output_aliases={0: 2}, compiler_params=_ASYNC,
    )(_hbm(land))


def _halves_wait(send_sems, recv_sems, land_thru, after, name):
    rh = land_thru.shape[1] // 2

    def body(land_ref, send_sems, recv_sems, after_ref, got_ref):
        x, y, c, _ = _mesh_place()
        half = land_ref.at[0, pl.ds(0, rh)]
        for k in range(3):
            cp = pltpu.make_async_remote_copy(src_ref=half, dst_ref=half, send_sem=send_sems.at[k],
                                              recv_sem=recv_sems.at[k], device_id=(x, y, 1 - c),
                                              device_id_type=MESH)
            cp.wait_send()
            cp.wait_recv()

    return pl.pallas_call(
        body, name=name, out_shape=pltpu.HBM(land_thru.shape, land_thru.dtype),
        in_specs=(_HBM, _SEM, _SEM, pl.BlockSpec(memory_space=pl.ANY)), out_specs=_HBM,
        input_output_aliases={0: 0}, compiler_params=_ASYNC,
    )(land_thru, send_sems, recv_sems, after)


def _gather_start(wp, name):
    R, W = wp.shape

    def body(w_ref, land_ref, send_sems, recv_sems, w_thru, land_thru, token):
        for cp in _gather_copies(w_ref, land_ref, send_sems, recv_sems):
            cp.start()
        token[...] = jnp.zeros_like(token)

    return pl.pallas_call(
        body, name=name,
        out_shape=(pltpu.SemaphoreType.DMA((3,)), pltpu.SemaphoreType.DMA((3,)), pltpu.HBM(wp.shape, wp.dtype),
                   pltpu.HBM((N_CHIPS, R, W), wp.dtype), jax.ShapeDtypeStruct((8, 128), F32)),
        in_specs=(_HBM, _HBM),
        out_specs=(_SEM, _SEM, _HBM, _HBM, pl.BlockSpec(memory_space=pltpu.VMEM)),
        input_output_aliases={0: 2, 1: 3}, compiler_params=_ASYNC,
    )(_hbm(wp), _hbm(lax.empty((N_CHIPS, R, W), wp.dtype)))


def _gather_wait(send_sems, recv_sems, w_thru, land_thru, after, name):
    R, W = w_thru.shape
    rh = R // 2

    def body(w_ref, land_ref, send_sems, recv_sems, after_ref, w_dead, got_ref):
        x, y, c, _ = _mesh_place()
        half = land_ref.at[0, pl.ds(0, rh)]
        for k in range(3):
            cp = pltpu.make_async_remote_copy(src_ref=half, dst_ref=half, send_sem=send_sems.at[k],
                                              recv_sem=recv_sems.at[k], device_id=(x, y, 1 - c),
                                              device_id_type=MESH)
            cp.wait_send()
            cp.wait_recv()

    return pl.pallas_call(
        body, name=name,
        out_shape=(pltpu.HBM(w_thru.shape, w_thru.dtype), pltpu.HBM(land_thru.shape, land_thru.dtype)),
        in_specs=(_HBM, _HBM, _SEM, _SEM, pl.BlockSpec(memory_space=pl.ANY)), out_specs=(_HBM, _HBM),
        input_output_aliases={0: 0, 1: 1}, compiler_params=_ASYNC,
    )(w_thru, land_thru, send_sems, recv_sems, after)


def _scatter_start(g, row0, nrows, name):
    n, R, W = g.shape
    land_shape = (n, nrows, W)

    def body(g_ref, land_ref, send_sems, recv_sems, g_thru, land_thru, token):
        for cp in _scatter_copies(g_ref, land_ref, send_sems, recv_sems, row0):
            cp.start()
        token[...] = jnp.zeros_like(token)

    return pl.pallas_call(
        body, name=name,
        out_shape=(pltpu.SemaphoreType.DMA((3,)), pltpu.SemaphoreType.DMA((3,)), pltpu.HBM(g.shape, g.dtype),
                   pltpu.HBM(land_shape, g.dtype), jax.ShapeDtypeStruct((8, 128), F32)),
        in_specs=(_HBM, _HBM),
        out_specs=(_SEM, _SEM, _HBM, _HBM, pl.BlockSpec(memory_space=pltpu.VMEM)),
        input_output_aliases={0: 2, 1: 3}, compiler_params=_ASYNC,
    )(_hbm(g), _hbm(lax.empty(land_shape, g.dtype)))


def _scatter_wait(send_sems, recv_sems, g_thru, land_thru, after, name):
    def body(g_ref, land_ref, send_sems, recv_sems, after_ref, g_out, got_ref):
        x, y, c, _ = _mesh_place()
        for k in range(3):
            cp = pltpu.make_async_remote_copy(src_ref=land_ref.at[0], dst_ref=land_ref.at[0], send_sem=send_sems.at[k],
                                              recv_sem=recv_sems.at[k], device_id=(x, y, 1 - c),
                                              device_id_type=MESH)
            cp.wait_send()
            cp.wait_recv()

    return pl.pallas_call(
        body, name=name,
        out_shape=(pltpu.HBM(g_thru.shape, g_thru.dtype), pltpu.HBM(land_thru.shape, land_thru.dtype)),
        in_specs=(_HBM, _HBM, _SEM, _SEM, pl.BlockSpec(memory_space=pl.ANY)), out_specs=(_HBM, _HBM),
        input_output_aliases={0: 0, 1: 1}, compiler_params=_ASYNC,
    )(g_thru, land_thru, send_sems, recv_sems, after)


def _adamw(w, g, m, v, name):
    shape = w.shape
    cols = shape[-1]
    w2, g2, m2, v2 = (t.reshape(-1, cols) for t in (w, g, m, v))
    rows = w2.shape[0]
    tr = rows
    for cand in (512, 256, 128, 64, 32, 16, 8):
        if rows > cand and rows % cand == 0:
            tr = cand
            break
    c1 = 1.0 / (1.0 - ADAM_B1 ** ADAM_STEP)
    c2 = 1.0 / (1.0 - ADAM_B2 ** ADAM_STEP)

    def body(w_ref, g_ref, m_ref, v_ref, d_ref, nm_ref, nv_ref):
        gv = g_ref[...]
        nm = ADAM_B1 * m_ref[...] + (1.0 - ADAM_B1) * gv
        nv = ADAM_B2 * v_ref[...] + (1.0 - ADAM_B2) * (gv * gv)
        nm_ref[...] = nm
        nv_ref[...] = nv
        d_ref[...] = -ADAM_LR * ((nm * c1) / (jnp.sqrt(nv * c2) + ADAM_EPS) + ADAM_WD * w_ref[...])

    blk = pl.BlockSpec((tr, cols), lambda i: (i, 0))
    sds = jax.ShapeDtypeStruct((rows, cols), F32)
    d, nm, nv = pl.pallas_call(body, name=name, grid=(rows // tr,), in_specs=[blk] * 4,
                               out_specs=(blk, blk, blk), out_shape=(sds, sds, sds),
                               compiler_params=_cparams("parallel"))(w2, g2, m2, v2)
    return d.reshape(shape), nm.reshape(shape), nv.reshape(shape)


def kernel(x, positions, norm_gains, mla_w_in, mla_q_norm, mla_kv_norm, mla_w_uq, mla_w_ukv, mla_w_o, hgrn_w_in, hgrn_lb_logits, hgrn_o_norm, hgrn_w_o, mlp_w1, mlp_w2, loss_target, m_norm_gains, m_mla_w_in, m_mla_q_norm, m_mla_kv_norm, m_mla_w_uq, m_mla_w_ukv, m_mla_w_o, m_hgrn_w_in, m_hgrn_lb_logits, m_hgrn_o_norm, m_hgrn_w_o, m_mlp_w1, m_mlp_w2, v_norm_gains, v_mla_w_in, v_mla_q_norm, v_mla_kv_norm, v_mla_w_uq, v_mla_w_ukv, v_mla_w_o, v_hgrn_w_in, v_hgrn_lb_logits, v_hgrn_o_norm, v_hgrn_w_o, v_mlp_w1, v_mlp_w2):
    w = dict(norm_gains=norm_gains, mla_w_in=mla_w_in, mla_q_norm=mla_q_norm, mla_kv_norm=mla_kv_norm,
             mla_w_uq=mla_w_uq, mla_w_ukv=mla_w_ukv, mla_w_o=mla_w_o, hgrn_w_in=hgrn_w_in,
             hgrn_lb_logits=hgrn_lb_logits, hgrn_o_norm=hgrn_o_norm, hgrn_w_o=hgrn_w_o,
             mlp_w1=mlp_w1, mlp_w2=mlp_w2)
    mom_m = dict(norm_gains=m_norm_gains, mla_w_in=m_mla_w_in, mla_q_norm=m_mla_q_norm,
                 mla_kv_norm=m_mla_kv_norm, mla_w_uq=m_mla_w_uq, mla_w_ukv=m_mla_w_ukv,
                 mla_w_o=m_mla_w_o, hgrn_w_in=m_hgrn_w_in, hgrn_lb_logits=m_hgrn_lb_logits,
                 hgrn_o_norm=m_hgrn_o_norm, hgrn_w_o=m_hgrn_w_o, mlp_w1=m_mlp_w1, mlp_w2=m_mlp_w2)
    mom_v = dict(norm_gains=v_norm_gains, mla_w_in=v_mla_w_in, mla_q_norm=v_mla_q_norm,
                 mla_kv_norm=v_mla_kv_norm, mla_w_uq=v_mla_w_uq, mla_w_ukv=v_mla_w_ukv,
                 mla_w_o=v_mla_w_o, hgrn_w_in=v_hgrn_w_in, hgrn_lb_logits=v_hgrn_lb_logits,
                 hgrn_o_norm=v_hgrn_o_norm, hgrn_w_o=v_hgrn_w_o, mlp_w1=v_mlp_w1, mlp_w2=v_mlp_w2)
    c = lax.axis_index("c")

    axis_of = dict(SHARDED)
    me = 2 * lax.axis_index("x") + lax.axis_index("y")
    gain_bits = lax.bitcast_convert_type(norm_gains, jnp.uint32)
    gain_hi = lax.bitcast_convert_type((gain_bits >> 16).astype(jnp.uint16), BF16)
    gain_lo = lax.bitcast_convert_type((gain_bits & 0xFFFF).astype(jnp.uint16), BF16)

    layers = []
    for l in range(DEPTH):
        s = l // 2
        if l % 2 == 0:
            big = [("mlp_w1", l), ("mlp_w2", l), ("mla_w_o", s)]
            tail = [("mla_w_in", s), ("mla_w_uq", s), ("mla_w_ukv", s)]
        else:
            big = [("hgrn_w_in", s), ("mlp_w1", l), ("mlp_w2", l), ("hgrn_w_o", s)]
            tail = []
        w_tail = [w[n][i] for n, i in tail] + ([gain_hi, gain_lo] if l == 0 else [])
        g_tail = tail + ([("norm_gains", None)] + [(n, None) for n in REPLICATED] if l == 0 else [])
        g_shapes = [w[n].shape if i is None else w[n][i].shape for n, i in g_tail]
        tail_rows = max(_packed_misc_rows([t.shape for t in w_tail]), _packed_misc_rows(g_shapes))
        pk = _Packed([(n, w[n].shape[1]) for n, _ in big], tail_rows)
        wpack = jnp.zeros((pk.rows, PACK_W), BF16)
        for n, i in big:
            assert w[n].shape[2] == PACK_W
            wpack = _cast_into(w[n][i], wpack, pk.off[n], name="pack_%s_%d" % (n, l))
        if w_tail:
            wpack = lax.dynamic_update_slice(
                wpack, jnp.concatenate(_pack_blocks(w_tail, 0, BF16), axis=0), (pk.misc, 0))
        layers.append(dict(pk=pk, big=big, tail=tail, w_tail=w_tail, g_tail=g_tail, g_shapes=g_shapes,
                           gather=_gather_start(wpack, name="gather_start_%d" % l)))

    def prefetch(l, after):
        lay = layers[l]
        send_sems, recv_sems, w_thru, land_thru, _ = lay["gather"]
        lay["w_back"], land = _gather_wait(send_sems, recv_sems, w_thru, land_thru, after,
                                           name="gather_wait_%d" % l)
        lay["halves"] = _halves_start(land, name="gather_halves_start_%d" % l)

    def fetch(l, after):
        lay = layers[l]
        pk = lay["pk"]
        if l == 0:
            send_sems, recv_sems, w_thru, land_thru, _ = lay["gather"]
            after = sum(layers[k]["gather"][4] for k in range(1, DEPTH))
            w_back, land = _gather_wait(send_sems, recv_sems, w_thru, land_thru, after, name="gather_wait_0")
            land = _halves_to_sibling(land, name="gather_halves_0")
        else:
            send_sems, recv_sems, land_thru, _ = lay["halves"]
            w_back = lay["w_back"]
            land = _halves_wait(send_sems, recv_sems, land_thru, after, name="gather_halves_wait_%d" % l)
        land = lax.dynamic_update_slice(land, w_back[None], (me, 0, 0))
        out = dict(wbuf=land.reshape(N_CHIPS * pk.rows, PACK_W), pk=pk)
        if lay["w_tail"]:
            rows = _packed_misc_rows([t.shape for t in lay["w_tail"]])
            per_chip = [_unpack(land[j, pk.misc:pk.misc + rows], [t.shape for t in lay["w_tail"]])
                        for j in range(N_CHIPS)]
            for i, (n, _) in enumerate(lay["tail"]):
                out[n[4:]] = jnp.concatenate([per_chip[j][i] for j in range(N_CHIPS)], axis=axis_of[n] - 1)
            if l == 0:
                got_hi, got_lo = (lax.bitcast_convert_type(
                    jnp.concatenate([per_chip[j][i] for j in range(N_CHIPS)], axis=2),
                    jnp.uint16).astype(jnp.uint32) for i in (-2, -1))
                out["gains"] = lax.bitcast_convert_type((got_hi << 16) | got_lo, F32)
        return out

    def emit(l, gbuf, grads):
        lay = layers[l]
        pk = lay["pk"]
        if lay["g_tail"]:
            for j in range(N_CHIPS):
                pieces = []
                for n, i in lay["g_tail"]:
                    if n not in axis_of:
                        pieces.append(grads[n])
                    else:
                        pieces.append(jnp.split(grads[n], N_CHIPS, axis=axis_of[n] - (0 if i is None else 1))[j])
                block = jnp.concatenate(_pack_blocks(pieces, 0, BF16), axis=0)
                gbuf = lax.dynamic_update_slice(gbuf, block, (j * pk.rows + pk.misc, 0))
        row0 = lay.get("early_rows", 0)
        lay["scatter"] = _scatter_start(gbuf.reshape(N_CHIPS, pk.rows, PACK_W), row0, pk.rows - row0,
                                        name="scatter_start_%d" % l)
        return lay["scatter"][4][0, 0]

    def emit_mlp(l, gbuf):
        lay = layers[l]
        pk = lay["pk"]
        assert pk.off["mlp_w1"] == 0 and pk.off["mlp_w2"] == w["mlp_w1"].shape[1]
        lay["early_rows"] = w["mlp_w1"].shape[1] + w["mlp_w2"].shape[1]
        lay["scatter_early"] = _scatter_start(gbuf.reshape(N_CHIPS, pk.rows, PACK_W), 0, lay["early_rows"],
                                              name="scatter_start_%d_mlp" % l)
        return lay["scatter_early"][2].reshape(N_CHIPS * pk.rows, PACK_W)

    small = dict(mla_q_norm=mla_q_norm, mla_kv_norm=mla_kv_norm, hgrn_lb_logits=hgrn_lb_logits,
                 hgrn_o_norm=hgrn_o_norm)
    gbufs = [lax.empty((N_CHIPS * lay["pk"].rows, PACK_W), BF16) for lay in layers]
    sq, grad_x = _local_step(x[0], positions[0], loss_target[0], small, prefetch, fetch, gbufs, emit, emit_mlp)
    d_model = x.shape[-1]
    loss = lax.psum(0.5 * jnp.sum(sq) / d_model, ("x", "y", "c"))

    per_name = {}
    behind = grad_x
    for l, lay in reversed(list(enumerate(layers))):
        pk = lay["pk"]
        send_sems, recv_sems, g_thru, land_thru, _ = lay["scatter"]
        row0 = lay.get("early_rows", 0)
        early = None
        if row0:
            e_send, e_recv, _, e_land, _ = lay["scatter_early"]
            g_thru, land = _scatter_wait(e_send, e_recv, g_thru, e_land, behind, name="scatter_wait_%d_mlp" % l)
            early = behind = _sum_chips(land, g_thru, 0, me, name="grads_sum_chips_%d_mlp" % l, out_dtype=BF16)
        g_back, land = _scatter_wait(send_sems, recv_sems, g_thru, land_thru, behind, name="scatter_wait_%d" % l)
        mine = _sum_chips(land, g_back, row0, me, name="grads_sum_chips_%d" % l, out_dtype=BF16)
        if early is not None:
            mine = jnp.concatenate([early, mine], axis=0)
        red = behind = _sum_chips(_share_reduced(mine, name="grads_share_%d" % l), mine, 0, c,
                                  name="grads_sum_cores_%d" % l)
        for n, i in lay["big"]:
            per_name.setdefault(n, {})[i] = red[pk.off[n]:pk.off[n] + w[n].shape[1]]
        for (n, i), piece in zip(lay["g_tail"], _unpack(red[pk.misc:pk.misc + pk.misc_rows], lay["g_shapes"])):
            per_name.setdefault(n, {})[i] = piece
    g_out = {n: (parts[None] if None in parts else jnp.stack([parts[i] for i in sorted(parts)]))
             for n, parts in per_name.items()}

    deltas, new_m, new_v = {}, {}, {}
    for name in WEIGHTS:
        deltas[name], new_m[name], new_v[name] = _adamw(w[name], g_out[name], mom_m[name], mom_v[name],
                                                        name="adamw_" + name)
    return (loss, grad_x[None], *[g_out[n] for n in WEIGHTS], *[deltas[n] for n in WEIGHTS],
            *[new_m[n] for n in WEIGHTS], *[new_v[n] for n in WEIGHTS])
```

```python
import jax
import jax.numpy as jnp
from jax import lax
from jax.experimental import pallas as pl
from jax.experimental.pallas import tpu as pltpu

F32 = jnp.float32
BF16 = jnp.bfloat16
MESH = pl.DeviceIdType.MESH

DEPTH = 4
MLA_HEADS = 8
MLA_NOPE = 128
MLA_ROPE = 64
MLA_V = 128
MLA_QK_PAD = 256
MLA_HEADS_PER_STEP = 2
MLA_SCALE = float(MLA_NOPE + MLA_ROPE) ** -0.5
ROPE_BASE = 10000.0
HGRN_HEADS = 8
HGRN_CHUNK = 32
HGRN_BLOCK = 128
EPS = 1e-6

ADAM_LR = 0.001
ADAM_B1 = 0.9
ADAM_B2 = 0.999
ADAM_EPS = 1e-08
ADAM_WD = 0.01
ADAM_STEP = 10

N_CHIPS = 4
PACK_W = 1024
PACK_ALIGN = 1024
PACK_TILE = 512
V7X_VMEM_LIMIT = 56 * 1024 * 1024

SHARDED = (("norm_gains", 2), ("mla_w_in", 1), ("mla_w_uq", 2), ("mla_w_ukv", 2), ("mla_w_o", 1),
           ("hgrn_w_in", 2), ("hgrn_w_o", 1), ("mlp_w1", 2), ("mlp_w2", 1))
REPLICATED = ("mla_q_norm", "mla_kv_norm", "hgrn_lb_logits", "hgrn_o_norm")
WEIGHTS = ("norm_gains", "mla_w_in", "mla_q_norm", "mla_kv_norm", "mla_w_uq", "mla_w_ukv", "mla_w_o",
           "hgrn_w_in", "hgrn_lb_logits", "hgrn_o_norm", "hgrn_w_o", "mlp_w1", "mlp_w2")


def _cparams(*semantics):
    return pltpu.CompilerParams(dimension_semantics=semantics, vmem_limit_bytes=V7X_VMEM_LIMIT)


def _sigmoid(x):
    return 0.5 * jnp.tanh(0.5 * x) + 0.5


def _mm(a, b, *, ta=False, tb=False, out_dtype=F32, tm=2048, tn=1024, tk=1024, epi=None, extra=None,
        name="mm", n=None, b_map=None, into=None, o_map=None):
    if ta:
        K, M = a.shape
    else:
        M, K = a.shape
    if b_map is not None:
        N = n
    elif tb:
        N, Kb = b.shape
    else:
        Kb, N = b.shape
    assert b_map is not None or K == Kb, (a.shape, b.shape, ta, tb)
    tm, tn = min(tm, M), min(tn, N)
    if ta and b_map is None:
        tk = max(tk, 4096)
    tk = K if (K <= 1024 and b_map is None) else min(tk, K)
    assert M % tm == 0 and N % tn == 0 and K % tk == 0, (M, N, K, tm, tn, tk)
    nk = K // tk
    a_spec = (pl.BlockSpec((tk, tm), lambda i, j, k: (k, i)) if ta
              else pl.BlockSpec((tm, tk), lambda i, j, k: (i, k)))
    if b_map is None:
        b_map = (lambda i, j, k: (j, k)) if tb else (lambda i, j, k: (k, j))
    b_spec = pl.BlockSpec((tn, tk) if tb else (tk, tn), b_map)
    o_spec = pl.BlockSpec((tm, tn), lambda i, j, k: (i, j))
    dims = (((0 if ta else 1,), (1 if tb else 0,)), ((), ()))
    in_specs = [a_spec, b_spec]
    operands = [a, b]
    aliases = {}
    if epi == "mul2r":
        in_specs.append(o_spec)
        operands.append(extra)
    if epi == "resnorm":
        assert tn == N
        vec = pl.BlockSpec((1, N), lambda i, j, k: (0, 0))
        in_specs += [o_spec, vec, vec]
        operands += list(extra)
    if into is not None:
        assert epi is None
        in_specs.append(pl.BlockSpec(memory_space=pl.ANY))
        operands.append(into)
        aliases = {2: 0}
        out_dtype = into.dtype
        out_shape = jax.ShapeDtypeStruct(into.shape, into.dtype)
        out_specs = pl.BlockSpec((tm, tn), o_map)
    elif epi == "relu2":
        out_shape = (jax.ShapeDtypeStruct((M, N), BF16), jax.ShapeDtypeStruct((M, N), BF16))
        out_specs = (o_spec, o_spec)
    elif epi == "mul2r":
        out_shape = jax.ShapeDtypeStruct((M, N), BF16)
        out_specs = o_spec
    elif epi == "resnorm":
        out_shape = (jax.ShapeDtypeStruct((M, N), F32), jax.ShapeDtypeStruct((M, N), F32),
                     jax.ShapeDtypeStruct((M, N), BF16))
        out_specs = (o_spec, o_spec, o_spec)
    else:
        out_shape = jax.ShapeDtypeStruct((M, N), out_dtype)
        out_specs = o_spec
    n_in = len(operands)
    n_out = {"relu2": 2, "resnorm": 3}.get(epi, 1)

    def body(*refs):
        a_ref, b_ref = refs[0], refs[1]
        outs = refs[n_in:n_in + n_out]
        k = pl.program_id(2)

        def finish(acc):
            if epi == "relu2":
                r = jnp.maximum(acc, 0.0)
                outs[0][...] = (r * r).astype(BF16)
                outs[1][...] = r.astype(BF16)
            elif epi == "mul2r":
                outs[0][...] = (acc * (2.0 * refs[2][...].astype(F32))).astype(BF16)
            elif epi == "resnorm":
                h_ref, gp_ref, gn_ref = refs[2], refs[3], refs[4]
                hn = h_ref[...] + acc * _rms_rstd(acc) * gp_ref[...]
                outs[0][...] = acc
                outs[1][...] = hn
                outs[2][...] = (hn * _rms_rstd(hn) * gn_ref[...]).astype(BF16)
            else:
                outs[0][...] = acc.astype(out_dtype)

        part = lax.dot_general(a_ref[...], b_ref[...], dims, preferred_element_type=F32)
        if nk == 1:
            finish(part)
            return
        acc_ref = refs[-1]

        @pl.when(k == 0)
        def _():
            acc_ref[...] = part

        @pl.when((k > 0) & (k < nk - 1))
        def _():
            acc_ref[...] += part

        @pl.when(k == nk - 1)
        def _():
            finish(acc_ref[...] + part)

    return pl.pallas_call(
        body, name=name, grid=(M // tm, N // tn, nk), in_specs=in_specs, out_specs=out_specs,
        out_shape=out_shape, scratch_shapes=[pltpu.VMEM((tm, tn), F32)] if nk > 1 else [],
        input_output_aliases=aliases,
        compiler_params=_cparams("parallel", "parallel", "arbitrary"))(*operands)


def _rms_rstd(x):
    return lax.rsqrt(jnp.mean(x * x, axis=-1, keepdims=True) + EPS)


def _rms_bwd_tile(x, g, dy):
    r = _rms_rstd(x)
    xh = x * r
    u = dy * g
    dx = r * (u - xh * jnp.mean(u * xh, axis=-1, keepdims=True))
    dg = jnp.sum(dy * xh, axis=0, keepdims=True)
    return dx, dg


def _row_tile(T):
    return min(512, T)


def _mid_tile(T):
    return min(256, T)


def _prenorm_fwd(x, g, name="prenorm_fwd"):
    T, D = x.shape
    tm = _row_tile(T)

    def body(x_ref, g_ref, a_ref):
        xv = x_ref[...]
        a_ref[...] = (xv * _rms_rstd(xv) * g_ref[...]).astype(BF16)

    row = pl.BlockSpec((tm, D), lambda i: (i, 0))
    vec = pl.BlockSpec((1, D), lambda i: (0, 0))
    return pl.pallas_call(body, name=name, grid=(T // tm,), in_specs=[row, vec], out_specs=row,
                          out_shape=jax.ShapeDtypeStruct((T, D), BF16),
                          compiler_params=_cparams("parallel"))(x, g)


def _resnorm_loss(h, z, g_post, target, name="resnorm_loss"):
    T, D = h.shape
    tm = _row_tile(T)

    def body(h_ref, z_ref, gp_ref, t_ref, dy_ref, sq_ref):
        zv = z_ref[...]
        err = h_ref[...] + zv * _rms_rstd(zv) * gp_ref[...] - t_ref[...]
        dy_ref[...] = err * (1.0 / D)

        @pl.when(pl.program_id(0) == 0)
        def _():
            sq_ref[...] = jnp.zeros_like(sq_ref)

        sq_ref[...] += jnp.sum(err * err, axis=0, keepdims=True)

    row = pl.BlockSpec((tm, D), lambda i: (i, 0))
    vec = pl.BlockSpec((1, D), lambda i: (0, 0))
    return pl.pallas_call(body, name=name, grid=(T // tm,), in_specs=[row, row, vec, row],
                          out_specs=(row, vec),
                          out_shape=(jax.ShapeDtypeStruct((T, D), F32), jax.ShapeDtypeStruct((1, D), F32)),
                          compiler_params=_cparams("arbitrary"))(h, z, g_post, target)


def _resnorm_bwd(z, g_post, dh, h_new=None, da=None, g_pre=None, name="resnorm_bwd"):
    T, D = z.shape
    tm = _row_tile(T)
    has_next = h_new is not None
    row = pl.BlockSpec((tm, D), lambda i: (i, 0))
    vec = pl.BlockSpec((1, D), lambda i: (0, 0))

    if has_next:
        def body(z_ref, gp_ref, dh_ref, hn_ref, da_ref, gn_ref, t_ref, dz_ref, dgp_ref, dgn_ref):
            first = pl.program_id(0) == 0

            @pl.when(first)
            def _():
                dgp_ref[...] = jnp.zeros_like(dgp_ref)
                dgn_ref[...] = jnp.zeros_like(dgn_ref)

            dpre, dgn = _rms_bwd_tile(hn_ref[...], gn_ref[...], da_ref[...])
            t = dh_ref[...] + dpre
            t_ref[...] = t
            dz, dgp = _rms_bwd_tile(z_ref[...], gp_ref[...], t)
            dz_ref[...] = dz.astype(BF16)
            dgp_ref[...] += dgp
            dgn_ref[...] += dgn

        return pl.pallas_call(
            body, name=name, grid=(T // tm,), in_specs=[row, vec, row, row, row, vec],
            out_specs=(row, row, vec, vec),
            out_shape=(jax.ShapeDtypeStruct((T, D), F32), jax.ShapeDtypeStruct((T, D), BF16),
                       jax.ShapeDtypeStruct((1, D), F32), jax.ShapeDtypeStruct((1, D), F32)),
            compiler_params=_cparams("arbitrary"))(z, g_post, dh, h_new, da, g_pre)

    def body_last(z_ref, gp_ref, dh_ref, dz_ref, dgp_ref):
        @pl.when(pl.program_id(0) == 0)
        def _():
            dgp_ref[...] = jnp.zeros_like(dgp_ref)

        dz, dgp = _rms_bwd_tile(z_ref[...], gp_ref[...], dh_ref[...])
        dz_ref[...] = dz.astype(BF16)
        dgp_ref[...] += dgp

    return pl.pallas_call(
        body_last, name=name, grid=(T // tm,), in_specs=[row, vec, row], out_specs=(row, vec),
        out_shape=(jax.ShapeDtypeStruct((T, D), BF16), jax.ShapeDtypeStruct((1, D), F32)),
        compiler_params=_cparams("arbitrary"))(z, g_post, dh)


def _prenorm_bwd(x, g, dh, da, name="prenorm_bwd"):
    T, D = x.shape
    tm = _row_tile(T)

    def body(x_ref, g_ref, dh_ref, da_ref, dx_ref, dg_ref):
        @pl.when(pl.program_id(0) == 0)
        def _():
            dg_ref[...] = jnp.zeros_like(dg_ref)

        dpre, dg = _rms_bwd_tile(x_ref[...], g_ref[...], da_ref[...])
        dx_ref[...] = dh_ref[...] + dpre
        dg_ref[...] += dg

    row = pl.BlockSpec((tm, D), lambda i: (i, 0))
    vec = pl.BlockSpec((1, D), lambda i: (0, 0))
    return pl.pallas_call(
        body, name=name, grid=(T // tm,), in_specs=[row, vec, row, row], out_specs=(row, vec),
        out_shape=(jax.ShapeDtypeStruct((T, D), F32), jax.ShapeDtypeStruct((1, D), F32)),
        compiler_params=_cparams("arbitrary"))(x, g, dh, da)


class _Packed:
    def __init__(self, big, misc_rows):
        self.big = tuple(big)
        self.off = {}
        r = 0
        for name, rows in big:
            self.off[name] = r
            r += rows
        self.misc, self.misc_rows = r, misc_rows
        self.rows = -(-(r + misc_rows) // PACK_ALIGN) * PACK_ALIGN

    def block(self, name, layer, unit):
        r = self.off[name]
        assert r % unit == 0 and self.rows % unit == 0
        return r // unit, self.rows // unit


def _col_sharded(pk, name, layer, unit):
    base, stride = pk.block(name, layer, unit)
    return (lambda i, j, k: (j * stride + base, 0)), (lambda i, j, k: (k * stride + base, 0))


def _row_sharded(pk, name, layer, unit):
    base, stride = pk.block(name, layer, unit)
    return ((lambda i, j, k: (k * stride + base, 0)), (lambda i, j, k: (j * stride + base, 0)),
            (lambda i, j, k: (i * stride + base, 0)))


def _mlp_fwd(a, wbuf, pk, layer, res):
    D = a.shape[1]
    by_n, _ = _col_sharded(pk, "mlp_w1", layer, D)
    by_k, _, _ = _row_sharded(pk, "mlp_w2", layer, D)
    act, r = _mm(a, wbuf, n=4 * D, b_map=by_n, tk=D, tn=D, epi="relu2", name="mlp_up")
    if res is None:
        return _mm(act, wbuf, n=D, b_map=by_k, tk=D, tn=D, name="mlp_down"), (a, act, r), None, None
    u, h_new, a_next = _mm(act, wbuf, n=D, b_map=by_k, tm=1024, tk=D, tn=D, epi="resnorm", extra=res,
                           name="mlp_down_res")
    return u, (a, act, r), h_new, a_next


def _mlp_bwd(du, saved, wbuf, gbuf, pk, layer):
    a, act, r = saved
    D = a.shape[1]
    w1_by_n, w1_by_k = _col_sharded(pk, "mlp_w1", layer, D)
    _, w2_by_n, w2_by_m = _row_sharded(pk, "mlp_w2", layer, D)
    dz1 = _mm(du, wbuf, tb=True, n=4 * D, b_map=w2_by_n, tn=D, tk=D, epi="mul2r", extra=r, name="mlp_down_dx")
    gbuf = _mm(act, du, ta=True, into=gbuf, o_map=w2_by_m, tm=D, tn=D, name="mlp_down_dw")
    gbuf = _mm(a, dz1, ta=True, into=gbuf, o_map=w1_by_n, tm=D, tn=D, name="mlp_up_dw")
    da = _mm(dz1, wbuf, tb=True, n=D, b_map=w1_by_k, tn=D, tk=D, name="mlp_up_dx")
    return da, gbuf


def _rope_swap(t):
    n = t.shape[-1]
    lane = lax.broadcasted_iota(jnp.int32, t.shape, t.ndim - 1)
    half = MLA_ROPE // 2
    first = (lane & (MLA_ROPE - 1)) < half
    return jnp.where(first, pltpu.roll(t, n - half, t.ndim - 1), pltpu.roll(t, half, t.ndim - 1))


def _mla_mid_fwd(proj, q_norm, kv_norm, w_uq, w_ukv, cc, ss):
    T, PW = proj.shape
    QL, KVL = q_norm.shape[-1], kv_norm.shape[-1]
    H = MLA_HEADS
    assert PW == QL + KVL + 128
    tm = _mid_tile(T)

    def body(p_ref, qn_ref, kn_ref, wq_ref, wkv_ref, cc_ref, ss_ref,
             cq_ref, ckv_ref, q_ref, k_ref, v_ref):
        cq = p_ref[:, 0:QL]
        ckv = p_ref[:, QL:QL + KVL]
        kr = p_ref[:, QL + KVL:QL + KVL + 128]
        c, s = cc_ref[...], ss_ref[...]
        cqn = (cq * _rms_rstd(cq) * qn_ref[...]).astype(BF16)
        ckvn = (ckv * _rms_rstd(ckv) * kn_ref[...]).astype(BF16)
        cq_ref[...] = cqn
        ckv_ref[...] = ckvn
        q = jnp.dot(cqn, wq_ref[...], preferred_element_type=F32)
        kv = jnp.dot(ckvn, wkv_ref[...], preferred_element_type=F32)
        krf = (kr * c + _rope_swap(kr) * s).astype(BF16)
        for h in range(H):
            o = h * MLA_QK_PAD
            q_ref[:, o:o + MLA_NOPE] = (q[:, o:o + MLA_NOPE] * MLA_SCALE).astype(BF16)
            qr = q[:, o + MLA_NOPE:o + MLA_QK_PAD]
            q_ref[:, o + MLA_NOPE:o + MLA_QK_PAD] = ((qr * c + _rope_swap(qr) * s) * MLA_SCALE).astype(BF16)
            k_ref[:, o:o + MLA_NOPE] = kv[:, o:o + MLA_NOPE].astype(BF16)
            k_ref[:, o + MLA_NOPE:o + MLA_QK_PAD] = krf
            v_ref[:, h * MLA_V:(h + 1) * MLA_V] = kv[:, o + MLA_NOPE:o + MLA_QK_PAD].astype(BF16)

    def row(w):
        return pl.BlockSpec((tm, w), lambda i: (i, 0))

    def full(shape):
        return pl.BlockSpec(shape, lambda i: (0, 0))

    return pl.pallas_call(
        body, name="mla_mid_fwd", grid=(T // tm,),
        in_specs=[row(PW), full((1, QL)), full((1, KVL)), full(w_uq.shape), full(w_ukv.shape),
                  row(128), row(128)],
        out_specs=(row(QL), row(KVL), row(H * MLA_QK_PAD), row(H * MLA_QK_PAD), row(H * MLA_V)),
        out_shape=(jax.ShapeDtypeStruct((T, QL), BF16), jax.ShapeDtypeStruct((T, KVL), BF16),
                   jax.ShapeDtypeStruct((T, H * MLA_QK_PAD), BF16),
                   jax.ShapeDtypeStruct((T, H * MLA_QK_PAD), BF16),
                   jax.ShapeDtypeStruct((T, H * MLA_V), BF16)),
        compiler_params=_cparams("parallel"))(proj, q_norm, kv_norm, w_uq, w_ukv, cc, ss)


def _mla_mid_bwd(proj, q_norm, kv_norm, w_uq, w_ukv, cc, ss, dq, dk, dv):
    T, PW = proj.shape
    QL, KVL = q_norm.shape[-1], kv_norm.shape[-1]
    H = MLA_HEADS
    tm = _mid_tile(T)
    nt = (((1,), (1,)), ((), ()))

    def body(p_ref, qn_ref, kn_ref, wq_ref, wkv_ref, cc_ref, ss_ref, dq_ref, dk_ref, dv_ref,
             dqp_ref, dkv_ref, dp_ref, dqn_ref, dkn_ref):
        @pl.when(pl.program_id(0) == 0)
        def _():
            dqn_ref[...] = jnp.zeros_like(dqn_ref)
            dkn_ref[...] = jnp.zeros_like(dkn_ref)

        c, s = cc_ref[...], ss_ref[...]
        dkr = jnp.zeros((tm, 128), F32)
        for h in range(H):
            o = h * MLA_QK_PAD
            dqp_ref[:, o:o + MLA_NOPE] = (dq_ref[:, o:o + MLA_NOPE] * MLA_SCALE).astype(BF16)
            dqr = dq_ref[:, o + MLA_NOPE:o + MLA_QK_PAD] * MLA_SCALE
            dqp_ref[:, o + MLA_NOPE:o + MLA_QK_PAD] = (dqr * c + _rope_swap(dqr * s)).astype(BF16)
            dkv_ref[:, o:o + MLA_NOPE] = dk_ref[:, o:o + MLA_NOPE].astype(BF16)
            dkv_ref[:, o + MLA_NOPE:o + MLA_QK_PAD] = dv_ref[:, h * MLA_V:(h + 1) * MLA_V].astype(BF16)
            dkr = dkr + dk_ref[:, o + MLA_NOPE:o + MLA_QK_PAD]
        dcqn = lax.dot_general(dqp_ref[...], wq_ref[...], nt, preferred_element_type=F32)
        dckvn = lax.dot_general(dkv_ref[...], wkv_ref[...], nt, preferred_element_type=F32)
        dcq, dqn = _rms_bwd_tile(p_ref[:, 0:QL], qn_ref[...], dcqn)
        dckv, dkn = _rms_bwd_tile(p_ref[:, QL:QL + KVL], kn_ref[...], dckvn)
        dp_ref[:, 0:QL] = dcq.astype(BF16)
        dp_ref[:, QL:QL + KVL] = dckv.astype(BF16)
        dp_ref[:, QL + KVL:QL + KVL + 128] = (dkr * c + _rope_swap(dkr * s)).astype(BF16)
        dqn_ref[...] += dqn
        dkn_ref[...] += dkn

    def row(w):
        return pl.BlockSpec((tm, w), lambda i: (i, 0))

    def full(shape):
        return pl.BlockSpec(shape, lambda i: (0, 0))

    return pl.pallas_call(
        body, name="mla_mid_bwd", grid=(T // tm,),
        in_specs=[row(PW), full((1, QL)), full((1, KVL)), full(w_uq.shape), full(w_ukv.shape),
                  row(128), row(128), row(H * MLA_QK_PAD), row(H * MLA_QK_PAD), row(H * MLA_V)],
        out_specs=(row(H * MLA_QK_PAD), row(H * MLA_QK_PAD), row(PW), full((1, QL)), full((1, KVL))),
        out_shape=(jax.ShapeDtypeStruct((T, H * MLA_QK_PAD), BF16),
                   jax.ShapeDtypeStruct((T, H * MLA_QK_PAD), BF16),
                   jax.ShapeDtypeStruct((T, PW), BF16),
                   jax.ShapeDtypeStruct((1, QL), F32), jax.ShapeDtypeStruct((1, KVL), F32)),
        compiler_params=_cparams("arbitrary"))(proj, q_norm, kv_norm, w_uq, w_ukv, cc, ss, dq, dk, dv)


def _attn_tile(T):
    return min(1024, T)


def _attn_pairs(n, by_key):
    if by_key:
        pairs = [(qi, ki) for ki in range(n) for qi in range(ki, n)]
    else:
        pairs = [(qi, ki) for qi in range(n) for ki in range(qi + 1)]
    return (jnp.asarray([p[0] for p in pairs], jnp.int32), jnp.asarray([p[1] for p in pairs], jnp.int32))


def _scores(q, k, diagonal):
    s = lax.dot_general(q, k, (((1,), (1,)), ((), ())), preferred_element_type=F32)
    if diagonal:
        rows = lax.broadcasted_iota(jnp.int32, s.shape, 0)
        cols = lax.broadcasted_iota(jnp.int32, s.shape, 1)
        s = jnp.where(rows >= cols, s, -jnp.inf)
    return s


def _attn_fwd(q, k, v):
    T = q.shape[0]
    H, DQ, DV = MLA_HEADS, MLA_QK_PAD, MLA_V
    tq = _attn_tile(T)
    nq = T // tq
    G = MLA_HEADS_PER_STEP
    qi_tab, ki_tab = _attn_pairs(nq, by_key=False)

    def body(qi_ref, ki_ref, q_ref, k_ref, v_ref, o_ref, lse_ref, *scratch):
        m_refs, l_refs, acc_refs = scratch[0:G], scratch[G:2 * G], scratch[2 * G:3 * G]
        p = pl.program_id(1)
        qi, ki = qi_ref[p], ki_ref[p]

        @pl.when(ki == 0)
        def _():
            for g in range(G):
                m_refs[g][...] = jnp.full_like(m_refs[g], -jnp.inf)
                l_refs[g][...] = jnp.zeros_like(l_refs[g])
                acc_refs[g][...] = jnp.zeros_like(acc_refs[g])

        def update(ks, qr, masked):
            for g in range(G):
                qs, vs = slice(g * DQ, (g + 1) * DQ), slice(g * DV, (g + 1) * DV)
                st = _scores(k_ref[ks, qs], q_ref[qr, qs], False)
                if masked:
                    key = ks.start + lax.broadcasted_iota(jnp.int32, st.shape, 0)
                    qry = qr.start + lax.broadcasted_iota(jnp.int32, st.shape, 1)
                    st = jnp.where(qry >= key, st, -jnp.inf)
                m_prev = m_refs[g][:, qr]
                m_new = jnp.maximum(m_prev, jnp.max(st, axis=0, keepdims=True))
                alpha = jnp.exp(m_prev - m_new)
                pt = jnp.exp(st - m_new)
                l_refs[g][:, qr] = alpha * l_refs[g][:, qr] + jnp.sum(pt, axis=0, keepdims=True)
                acc_refs[g][:, qr] = alpha * acc_refs[g][:, qr] + lax.dot_general(
                    v_ref[ks, vs], pt.astype(BF16), (((0,), (0,)), ((), ())), preferred_element_type=F32)
                m_refs[g][:, qr] = m_new

        whole, half = slice(0, tq), tq // 2

        @pl.when(ki < qi)
        def _():
            update(whole, whole, False)

        @pl.when(ki == qi)
        def _():
            update(slice(0, half), whole, True)
            update(slice(half, tq), slice(half, tq), True)
            for g in range(G):
                vs = slice(g * DV, (g + 1) * DV)
                o_ref[:, vs] = jnp.transpose(acc_refs[g][...] / l_refs[g][...]).astype(BF16)
                lse_ref[g] = m_refs[g][...] + jnp.log(l_refs[g][...])

    return pl.pallas_call(
        body, name="attn_fwd",
        grid_spec=pltpu.PrefetchScalarGridSpec(
            num_scalar_prefetch=2, grid=(H // G, int(qi_tab.shape[0])),
            in_specs=[pl.BlockSpec((tq, G * DQ), lambda h, p, qt, kt: (qt[p], h)),
                      pl.BlockSpec((tq, G * DQ), lambda h, p, qt, kt: (kt[p], h)),
                      pl.BlockSpec((tq, G * DV), lambda h, p, qt, kt: (kt[p], h))],
            out_specs=(pl.BlockSpec((tq, G * DV), lambda h, p, qt, kt: (qt[p], h)),
                       pl.BlockSpec((G, 1, tq), lambda h, p, qt, kt: (h, 0, qt[p]))),
            scratch_shapes=([pltpu.VMEM((1, tq), F32)] * (2 * G) + [pltpu.VMEM((DV, tq), F32)] * G)),
        out_shape=(jax.ShapeDtypeStruct((T, H * DV), BF16), jax.ShapeDtypeStruct((H, 1, T), F32)),
        compiler_params=_cparams("parallel", "arbitrary"))(qi_tab, ki_tab, q, k, v)


def _attn_bwd(q, k, v, o, do, lse):
    T = q.shape[0]
    H, DQ, DV = MLA_HEADS, MLA_QK_PAD, MLA_V
    tq = _attn_tile(T)
    nq = T // tq
    tn = (((0,), (0,)), ((), ()))
    nt = (((1,), (1,)), ((), ()))
    G = MLA_HEADS_PER_STEP
    qi_tab, ki_tab = _attn_pairs(nq, by_key=True)

    def body(qi_ref, ki_ref, q_ref, k_ref, v_ref, o_ref, do_ref, lse_ref, dq_ref, dk_ref, dv_ref,
             dk_acc, dv_acc):
        p = pl.program_id(1)
        qi, ki = qi_ref[p], ki_ref[p]

        @pl.when(p == 0)
        def _():
            dq_ref[...] = jnp.zeros_like(dq_ref)

        @pl.when(qi == ki)
        def _():
            dk_acc[...] = jnp.zeros_like(dk_acc)
            dv_acc[...] = jnp.zeros_like(dv_acc)

        def step(ks, qr, masked):
            rows = pl.ds(pl.multiple_of(qi * tq + qr.start, qr.stop - qr.start), qr.stop - qr.start)
            for g in range(G):
                qs, vs = slice(g * DQ, (g + 1) * DQ), slice(g * DV, (g + 1) * DV)
                dof = do_ref[qr, vs]
                delta = jnp.sum(jnp.transpose(dof.astype(F32) * o_ref[qr, vs].astype(F32)), axis=0,
                                keepdims=True)
                st = _scores(k_ref[ks, qs], q_ref[qr, qs], False)
                if masked:
                    key = ks.start + lax.broadcasted_iota(jnp.int32, st.shape, 0)
                    qry = qr.start + lax.broadcasted_iota(jnp.int32, st.shape, 1)
                    st = jnp.where(qry >= key, st, -jnp.inf)
                pt = jnp.exp(st - lse_ref[g][:, qr])
                dpt = lax.dot_general(v_ref[ks, vs], dof, nt, preferred_element_type=F32)
                dst = (pt * (dpt - delta)).astype(BF16)
                dv_acc[ks, vs] += jnp.dot(pt.astype(BF16), dof, preferred_element_type=F32)
                dk_acc[ks, qs] += jnp.dot(dst, q_ref[qr, qs], preferred_element_type=F32)
                dq_ref[rows, qs] += lax.dot_general(dst, k_ref[ks, qs], tn, preferred_element_type=F32)

        whole, half = slice(0, tq), tq // 2

        @pl.when(qi == ki)
        def _():
            step(slice(0, half), whole, True)
            step(slice(half, tq), slice(half, tq), True)

        @pl.when(qi > ki)
        def _():
            step(whole, whole, False)

        @pl.when(qi == nq - 1)
        def _():
            dk_ref[...] = dk_acc[...]
            dv_ref[...] = dv_acc[...]

    qspec = pl.BlockSpec((tq, G * DQ), lambda h, p, qt, kt: (qt[p], h))
    ospec = pl.BlockSpec((tq, G * DV), lambda h, p, qt, kt: (qt[p], h))
    kspec = pl.BlockSpec((tq, G * DQ), lambda h, p, qt, kt: (kt[p], h))
    vspec = pl.BlockSpec((tq, G * DV), lambda h, p, qt, kt: (kt[p], h))
    return pl.pallas_call(
        body, name="attn_bwd",
        grid_spec=pltpu.PrefetchScalarGridSpec(
            num_scalar_prefetch=2, grid=(H // G, int(qi_tab.shape[0])),
            in_specs=[qspec, kspec, vspec, ospec, ospec,
                      pl.BlockSpec((G, 1, tq), lambda h, p, qt, kt: (h, 0, qt[p]))],
            out_specs=(pl.BlockSpec((T, G * DQ), lambda h, p, qt, kt: (0, h)), kspec, vspec),
            scratch_shapes=[pltpu.VMEM((tq, G * DQ), F32), pltpu.VMEM((tq, G * DV), F32)]),
        out_shape=(jax.ShapeDtypeStruct((T, H * DQ), F32), jax.ShapeDtypeStruct((T, H * DQ), F32),
                   jax.ShapeDtypeStruct((T, H * DV), F32)),
        compiler_params=_cparams("parallel", "arbitrary"))(qi_tab, ki_tab, q, k, v, o, do, lse)


def _mla_fwd(a, w, cc, ss, wbuf, pk, slot, res):
    D = a.shape[1]
    by_k, _, _ = _row_sharded(pk, "mla_w_o", slot, D // N_CHIPS)
    proj = _mm(a, w["w_in"], name="mla_in")
    cqn, ckvn, q, k, v = _mla_mid_fwd(proj, w["q_norm"], w["kv_norm"], w["w_uq"], w["w_ukv"], cc, ss)
    o, lse = _attn_fwd(q, k, v)
    m, h_new, a_next = _mm(o, wbuf, n=D, b_map=by_k, tm=1024, tk=D // N_CHIPS, tn=D, epi="resnorm", extra=res,
                           name="mla_out_res")
    return m, (a, proj, cqn, ckvn, q, k, v, o, lse), h_new, a_next


def _mla_bwd(dm, saved, w, cc, ss, wbuf, gbuf, pk, slot):
    a, proj, cqn, ckvn, q, k, v, o, lse = saved
    D = a.shape[1]
    _, by_n, by_m = _row_sharded(pk, "mla_w_o", slot, D // N_CHIPS)
    do = _mm(dm, wbuf, tb=True, n=o.shape[1], b_map=by_n, tm=2048, tn=D // N_CHIPS, tk=D, out_dtype=BF16,
             name="mla_out_dx")
    gbuf = _mm(o, dm, ta=True, into=gbuf, o_map=by_m, tm=D // N_CHIPS, tn=D, tk=2048, name="mla_out_dw")
    dq, dk, dv = _attn_bwd(q, k, v, o, do, lse)
    dqp, dkv, dproj, dqn, dkn = _mla_mid_bwd(proj, w["q_norm"], w["kv_norm"], w["w_uq"], w["w_ukv"],
                                             cc, ss, dq, dk, dv)
    dw_uq = _mm(cqn, dqp, ta=True, out_dtype=BF16, name="mla_uq_dw")
    dw_ukv = _mm(ckvn, dkv, ta=True, out_dtype=BF16, name="mla_ukv_dw")
    dw_in = _mm(a, dproj, ta=True, out_dtype=BF16, name="mla_in_dw")
    da = _mm(dproj, w["w_in"], tb=True, name="mla_in_dx")
    return da, gbuf, dict(w_in=dw_in, w_uq=dw_uq, w_ukv=dw_ukv, q_norm=dqn, kv_norm=dkn)


def _split_dot(mat, x, parts):
    acc = None
    rem = x
    for _ in range(parts):
        piece = rem.astype(BF16)
        term = jnp.dot(mat, piece, preferred_element_type=F32)
        acc = term if acc is None else acc + term
        rem = rem - piece.astype(F32)
    return acc


def _chunk_sums(cum, rel, rest, logf):
    return tuple(_split_dot(m.astype(BF16), logf, 3) for m in (cum, rel, rest))


def _chunk_mats(tb):
    C = HGRN_CHUNK
    assert C & (C - 1) == 0
    r = lax.broadcasted_iota(jnp.int32, (tb, tb), 0)
    s = lax.broadcasted_iota(jnp.int32, (tb, tb), 1)
    start = r & ~(C - 1)
    same = start == (s & ~(C - 1))
    ref = start + C // 2
    last = start + C - 1
    one, zero = jnp.float32(1.0), jnp.float32(0.0)
    cum = jnp.where(same & (s <= r), one, zero)
    rel = cum - jnp.where(same & (s <= ref), one, zero)
    rest = jnp.where(same & (s > r) & (s <= last), one, zero)
    rev = jnp.where(same & (s >= r), one, zero)
    ones = jnp.where(same, one, zero)
    causal = same & (s <= r)
    return cum, rel, rest, rev, ones, causal


def _hgrn_gates(p_ref, lb, HK):
    qx = p_ref[:, 0:HK]
    fx = p_ref[:, HK:2 * HK]
    sf = _sigmoid(fx)
    f = lb + (1.0 - lb) * sf
    sq = _sigmoid(qx)
    return qx, sq, qx * sq, sf, f, 1.0 - f, jnp.log(f)


def _hgrn_fwd(proj, lb, o_norm):
    T = proj.shape[0]
    H, C = HGRN_HEADS, HGRN_CHUNK
    HK = proj.shape[1] // 4
    DK = HK // H
    tb = min(HGRN_BLOCK, T)
    ncb = tb // C
    nt = (((1,), (1,)), ((), ()))
    tn = (((0,), (0,)), ((), ()))

    def body(p_ref, lb_ref, on_ref, y_ref, o_ref, st_ref, state, oacc):
        @pl.when(pl.program_id(0) == 0)
        def _():
            state[...] = jnp.zeros_like(state)

        cum, rel, rest, _, _, causal = _chunk_mats(tb)
        _, _, q, _, f, k, logf = _hgrn_gates(p_ref, lb_ref[...], HK)
        b, brel, brest = _chunk_sums(cum, rel, rest, logf)
        eb = jnp.exp(b)
        q_rel = (q * jnp.exp(brel)).astype(BF16)
        k_rel = (k * jnp.exp(-brel)).astype(BF16)
        q_dec = (q * eb).astype(BF16)
        k_dec = (k * jnp.exp(brest)).astype(BF16)
        v = p_ref[:, 2 * HK:3 * HK].astype(BF16)
        for h in range(H):
            hs = slice(h * DK, (h + 1) * DK)
            a = lax.dot_general(q_rel[:, hs], k_rel[:, hs], nt, preferred_element_type=F32)
            a = jnp.where(causal, a, 0.0).astype(BF16)
            oacc[:, hs] = jnp.dot(a, v[:, hs], preferred_element_type=F32)
            for j in range(ncb):
                rs = slice(j * C, (j + 1) * C)
                st = state[h]
                st_ref[j, h] = st
                oacc[rs, hs] += lax.dot_general(q_dec[rs, hs], st.astype(BF16), nt,
                                                preferred_element_type=F32)
                dec = jnp.exp(jnp.sum(logf[rs, hs], axis=0, keepdims=True))
                state[h] = dec * st + lax.dot_general(v[rs, hs], k_dec[rs, hs], tn,
                                                      preferred_element_type=F32)
        o = oacc[...]
        o_ref[...] = o
        gx = p_ref[:, 3 * HK:4 * HK]
        gate = gx * _sigmoid(gx)
        for h in range(H):
            hs = slice(h * DK, (h + 1) * DK)
            oh = o[:, hs]
            y_ref[:, hs] = (oh * _rms_rstd(oh) * on_ref[...] * gate[:, hs]).astype(BF16)

    return pl.pallas_call(
        body, name="hgrn_fwd", grid=(T // tb,),
        in_specs=[pl.BlockSpec((tb, 4 * HK), lambda i: (i, 0)),
                  pl.BlockSpec((1, HK), lambda i: (0, 0)),
                  pl.BlockSpec((1, DK), lambda i: (0, 0))],
        out_specs=(pl.BlockSpec((tb, HK), lambda i: (i, 0)),
                   pl.BlockSpec((tb, HK), lambda i: (i, 0)),
                   pl.BlockSpec((ncb, H, DK, DK), lambda i: (i, 0, 0, 0))),
        out_shape=(jax.ShapeDtypeStruct((T, HK), BF16), jax.ShapeDtypeStruct((T, HK), F32),
                   jax.ShapeDtypeStruct((T // C, H, DK, DK), F32)),
        scratch_shapes=[pltpu.VMEM((H, DK, DK), F32), pltpu.VMEM((tb, HK), F32)],
        compiler_params=_cparams("arbitrary"))(proj, lb, o_norm)


def _hgrn_bwd(proj, lb, o_norm, o, states, dy):
    T = proj.shape[0]
    H, C = HGRN_HEADS, HGRN_CHUNK
    HK = proj.shape[1] // 4
    DK = HK // H
    tb = min(HGRN_BLOCK, T)
    ncb = tb // C
    nb = T // tb
    nt = (((1,), (1,)), ((), ()))
    tn = (((0,), (0,)), ((), ()))

    def body(p_ref, lb_ref, on_ref, o_ref, st_ref, dy_ref, dp_ref, dlb_ref, don_ref,
             dstate, dqr_s, dkr_s, dqd_s, dkd_s, dv_s, do_s, e_s):
        @pl.when(pl.program_id(0) == 0)
        def _():
            dstate[...] = jnp.zeros_like(dstate)
            dlb_ref[...] = jnp.zeros_like(dlb_ref)
            don_ref[...] = jnp.zeros_like(don_ref)

        cum, rel, rest, rev, ones, causal = _chunk_mats(tb)
        lb = lb_ref[...]
        qx, sq, q, sf, f, k, logf = _hgrn_gates(p_ref, lb, HK)
        b, brel, brest = _chunk_sums(cum, rel, rest, logf)
        eb = jnp.exp(b)
        erel = jnp.exp(brel)
        enrel = jnp.exp(-brel)
        erest = jnp.exp(brest)
        q_rel_f, k_rel_f, q_dec_f, k_dec_f = q * erel, k * enrel, q * eb, k * erest
        q_rel, k_rel = q_rel_f.astype(BF16), k_rel_f.astype(BF16)
        q_dec, k_dec = q_dec_f.astype(BF16), k_dec_f.astype(BF16)
        v = p_ref[:, 2 * HK:3 * HK].astype(BF16)

        gx = p_ref[:, 3 * HK:4 * HK]
        sg = _sigmoid(gx)
        gate = gx * sg
        dy = dy_ref[...]
        ov = o_ref[...]
        on = on_ref[...]
        don = jnp.zeros((1, DK), F32)
        for h in range(H):
            hs = slice(h * DK, (h + 1) * DK)
            oh = ov[:, hs]
            r = _rms_rstd(oh)
            xh = oh * r
            d_on = dy[:, hs] * gate[:, hs]
            don = don + jnp.sum(d_on * xh, axis=0, keepdims=True)
            u = d_on * on
            do_s[:, hs] = r * (u - xh * jnp.mean(u * xh, axis=-1, keepdims=True))
            dp_ref[:, 3 * HK + h * DK:3 * HK + (h + 1) * DK] = (
                dy[:, hs] * xh * on * (sg[:, hs] * (1.0 + gx[:, hs] * (1.0 - sg[:, hs])))).astype(BF16)
        don_ref[...] += don

        for h in range(H):
            hs = slice(h * DK, (h + 1) * DK)
            doh = do_s[:, hs].astype(BF16)
            a = lax.dot_general(q_rel[:, hs], k_rel[:, hs], nt, preferred_element_type=F32)
            a = jnp.where(causal, a, 0.0).astype(BF16)
            da = lax.dot_general(doh, v[:, hs], nt, preferred_element_type=F32)
            da = jnp.where(causal, da, 0.0).astype(BF16)
            dv_s[:, hs] = lax.dot_general(a, doh, tn, preferred_element_type=F32)
            dqr_s[:, hs] = jnp.dot(da, k_rel[:, hs], preferred_element_type=F32)
            dkr_s[:, hs] = lax.dot_general(da, q_rel[:, hs], tn, preferred_element_type=F32)
            for j in reversed(range(ncb)):
                rs = slice(j * C, (j + 1) * C)
                dst = dstate[h]
                dstb = dst.astype(BF16)
                st = st_ref[j, h]
                dkd_s[rs, hs] = jnp.dot(v[rs, hs], dstb, preferred_element_type=F32)
                dv_s[rs, hs] += lax.dot_general(k_dec[rs, hs], dstb, nt, preferred_element_type=F32)
                dec = jnp.exp(jnp.sum(logf[rs, hs], axis=0, keepdims=True))
                e_s[rs, hs] = jnp.broadcast_to(jnp.sum(dst * st, axis=0, keepdims=True) * dec, (C, DK))
                dqd_s[rs, hs] = jnp.dot(doh[rs], st.astype(BF16), preferred_element_type=F32)
                dstate[h] = dec * dst + lax.dot_general(doh[rs], q_dec[rs, hs], tn,
                                                        preferred_element_type=F32)

        dqr, dkr, dqd, dkd = dqr_s[...], dkr_s[...], dqd_s[...], dkd_s[...]
        kdk = dkd * k_dec_f
        db = dqr * q_rel_f - dkr * k_rel_f + dqd * q_dec_f - kdk
        dlogf = _split_dot(rev.astype(BF16), db, 2) + _split_dot(ones.astype(BF16), kdk, 2) + e_s[...]
        dk = dkr * enrel + dkd * erest
        df = dlogf / f - dk
        dlb_ref[...] += jnp.sum(df * (1.0 - sf), axis=0, keepdims=True)
        dq = dqr * erel + dqd * eb
        dp_ref[:, 0:HK] = (dq * (sq * (1.0 + qx * (1.0 - sq)))).astype(BF16)
        dp_ref[:, HK:2 * HK] = (df * (1.0 - lb) * sf * (1.0 - sf)).astype(BF16)
        dp_ref[:, 2 * HK:3 * HK] = dv_s[...].astype(BF16)

    rev_row = lambda w: pl.BlockSpec((tb, w), lambda i: (nb - 1 - i, 0))
    vec = lambda w: pl.BlockSpec((1, w), lambda i: (0, 0))
    scr = pltpu.VMEM((tb, HK), F32)
    return pl.pallas_call(
        body, name="hgrn_bwd", grid=(nb,),
        in_specs=[rev_row(4 * HK), vec(HK), vec(DK), rev_row(HK),
                  pl.BlockSpec((ncb, H, DK, DK), lambda i: (nb - 1 - i, 0, 0, 0)), rev_row(HK)],
        out_specs=(rev_row(4 * HK), vec(HK), vec(DK)),
        out_shape=(jax.ShapeDtypeStruct((T, 4 * HK), BF16), jax.ShapeDtypeStruct((1, HK), F32),
                   jax.ShapeDtypeStruct((1, DK), F32)),
        scratch_shapes=[pltpu.VMEM((H, DK, DK), F32), scr, scr, scr, scr, scr, scr, scr],
        compiler_params=_cparams("arbitrary"))(proj, lb, o_norm, o, states, dy)


def _hgrn_layer_fwd(a, o_norm, lb, wbuf, pk, slot, res):
    D = a.shape[1]
    in_by_n, _ = _col_sharded(pk, "hgrn_w_in", slot, D)
    out_by_k, _, _ = _row_sharded(pk, "hgrn_w_o", slot, D // N_CHIPS)
    proj = _mm(a, wbuf, n=4 * D, b_map=in_by_n, tk=D, tn=D, name="hgrn_in")
    y, o, states = _hgrn_fwd(proj, lb, o_norm)
    m, h_new, a_next = _mm(y, wbuf, n=D, b_map=out_by_k, tm=1024, tk=D // N_CHIPS, tn=D, epi="resnorm",
                           extra=res, name="hgrn_out_res")
    return m, (a, proj, y, o, states), h_new, a_next


def _hgrn_layer_bwd(dm, saved, o_norm, lb, wbuf, gbuf, pk, slot):
    a, proj, y, o, states = saved
    D = a.shape[1]
    in_by_n, in_by_k = _col_sharded(pk, "hgrn_w_in", slot, D)
    _, out_by_n, out_by_m = _row_sharded(pk, "hgrn_w_o", slot, D // N_CHIPS)
    dy = _mm(dm, wbuf, tb=True, n=y.shape[1], b_map=out_by_n, tm=2048, tn=D // N_CHIPS, tk=D,
             name="hgrn_out_dx")
    gbuf = _mm(y, dm, ta=True, into=gbuf, o_map=out_by_m, tm=D // N_CHIPS, tn=D, tk=2048, name="hgrn_out_dw")
    dproj, dlb, don = _hgrn_bwd(proj, lb, o_norm, o, states, dy)
    gbuf = _mm(a, dproj, ta=True, into=gbuf, o_map=in_by_n, tm=D, tn=D, name="hgrn_in_dw")
    da = _mm(dproj, wbuf, tb=True, n=D, b_map=in_by_k, tn=D, tk=D, name="hgrn_in_dx")
    return da, gbuf, dict(o_norm=don, lb=dlb)


def _lower_bounds(lb_logits):
    p = jax.nn.softmax(lb_logits.astype(F32), axis=0)
    return jnp.cumsum(p, axis=0) - p[0]


def _rope_tables(positions):
    inv_freq = jnp.power(ROPE_BASE, -jnp.arange(0, MLA_ROPE, 2, dtype=F32) / MLA_ROPE)
    ang = positions.astype(F32)[:, None] * inv_freq
    cos, sin = jnp.cos(ang), jnp.sin(ang)
    zero = jnp.zeros((positions.shape[0], 128 - MLA_ROPE), F32)
    return (jnp.concatenate([cos, cos, zero], axis=-1), jnp.concatenate([-sin, sin, zero], axis=-1))


def _pad_mla_weights(w_in, w_uq):
    w_in_p = jnp.pad(w_in, ((0, 0), (0, 0), (0, 128 - MLA_ROPE)))
    n, ql, _ = w_uq.shape
    w_uq_p = jnp.pad(w_uq.reshape(n, ql, MLA_HEADS, MLA_NOPE + MLA_ROPE),
                     ((0, 0), (0, 0), (0, 0), (0, MLA_QK_PAD - MLA_NOPE - MLA_ROPE)))
    return w_in_p, w_uq_p.reshape(n, ql, MLA_HEADS * MLA_QK_PAD)


def _local_step(x, positions, target, small, prefetch, fetch, gbufs, emit, emit_mlp):
    T, D = x.shape
    lbounds, lb_vjp = jax.vjp(_lower_bounds, small["hgrn_lb_logits"])
    cc, ss = _rope_tables(positions)
    fetched = {0: fetch(0, None)}
    gains = fetched[0]["gains"]
    tick = [jnp.zeros((), F32)]

    def g(layer, i):
        return gains[layer, i][None, :] + tick[0]

    def mla_weights(layer):
        f = fetched[layer]
        w_in_p, w_uq_p = _pad_mla_weights(f["w_in"][None], f["w_uq"][None])
        slot = layer // 2
        return dict(w_in=w_in_p[0], w_uq=w_uq_p[0], w_ukv=f["w_ukv"],
                    q_norm=small["mla_q_norm"][slot][None, :], kv_norm=small["mla_kv_norm"][slot][None, :])

    saved = []
    h = x
    a = _prenorm_fwd(x, g(0, 0))
    dy = sq = None
    for layer in range(DEPTH):
        slot = layer // 2
        if layer not in fetched:
            fetched[layer] = fetch(layer, a)
        wbuf, pk = fetched[layer]["wbuf"], fetched[layer]["pk"]
        res = (h, g(layer, 1), g(layer, 2))
        if layer % 2 == 0:
            m, mix_saved, h1, a2 = _mla_fwd(a, mla_weights(layer), cc, ss, wbuf, pk, slot, res)
        else:
            m, mix_saved, h1, a2 = _hgrn_layer_fwd(a, small["hgrn_o_norm"][slot][None, :],
                                                   lbounds[layer][None, :], wbuf, pk, slot, res)
        if layer + 1 < DEPTH:
            tick[0] = prefetch(layer + 1, a2)
            u, mlp_saved, h2, a = _mlp_fwd(a2, wbuf, pk, layer, (h1, g(layer, 3), g(layer + 1, 0)))
        else:
            u, mlp_saved, h2, _ = _mlp_fwd(a2, wbuf, pk, layer, None)
            dy, sq = _resnorm_loss(h1, u, g(layer, 3), target)
        saved.append((h, m, h1, u, mix_saved, mlp_saved))
        h = h2

    n_mla, n_hgrn = (DEPTH + 1) // 2, DEPTH // 2
    dgains = [[None] * 4 for _ in range(DEPTH)]
    gw = {k: [None] * n_mla for k in ("mla_w_in", "mla_w_uq", "mla_w_ukv", "mla_q_norm", "mla_kv_norm")}
    gw["hgrn_o_norm"] = [None] * n_hgrn
    dlb = [jnp.zeros((1, lbounds.shape[1]), F32) for _ in range(DEPTH)]
    dh = dy
    da_next = None
    for layer in reversed(range(DEPTH)):
        h0, m, h1, u, mix_saved, mlp_saved = saved[layer]
        slot = layer // 2
        wbuf, pk, gbuf = fetched[layer]["wbuf"], fetched[layer]["pk"], gbufs[layer]
        if da_next is None:
            du, dgains[layer][3] = _resnorm_bwd(u, g(layer, 3), dh, name="resnorm_bwd_last")
            t = dh
        else:
            h2 = saved[layer + 1][0]
            t, du, dgains[layer][3], dgains[layer + 1][0] = _resnorm_bwd(
                u, g(layer, 3), dh, h2, da_next, g(layer + 1, 0), name="resnorm_bwd_mlp")
        da2, gbuf = _mlp_bwd(du, mlp_saved, wbuf, gbuf, pk, layer)
        if layer == 0:
            gbuf = emit_mlp(layer, gbuf)
        t, dm, dgains[layer][1], dgains[layer][2] = _resnorm_bwd(
            m, g(layer, 1), t, h1, da2, g(layer, 2), name="resnorm_bwd_mix")
        if layer % 2 == 0:
            da_next, gbuf, mg = _mla_bwd(dm, mix_saved, mla_weights(layer), cc, ss, wbuf, gbuf, pk, slot)
            ql = mg["q_norm"].shape[-1]
            kvl = mg["kv_norm"].shape[-1]
            gw["mla_w_in"][slot] = mg["w_in"][:, :ql + kvl + MLA_ROPE]
            gw["mla_w_uq"][slot] = mg["w_uq"].reshape(ql, MLA_HEADS, MLA_QK_PAD)[
                :, :, :MLA_NOPE + MLA_ROPE].reshape(ql, MLA_HEADS * (MLA_NOPE + MLA_ROPE))
            gw["mla_w_ukv"][slot] = mg["w_ukv"]
            gw["mla_q_norm"][slot] = mg["q_norm"][0]
            gw["mla_kv_norm"][slot] = mg["kv_norm"][0]
        else:
            da_next, gbuf, hg = _hgrn_layer_bwd(dm, mix_saved, small["hgrn_o_norm"][slot][None, :],
                                                lbounds[layer][None, :], wbuf, gbuf, pk, slot)
            gw["hgrn_o_norm"][slot] = hg["o_norm"][0]
            dlb[layer] = hg["lb"]
        dh = t
        if layer > 0:
            mine = ({k: gw[k][slot] for k in ("mla_w_in", "mla_w_uq", "mla_w_ukv")} if layer % 2 == 0 else {})
            tick[0] = emit(layer, gbuf, mine)
        else:
            gbuf0 = gbuf
    grad_x, dgains[0][0] = _prenorm_bwd(x, g(0, 0), dh, da_next)

    last = {k: gw[k][0] for k in ("mla_w_in", "mla_w_uq", "mla_w_ukv")}
    last.update({k: jnp.stack(gw[k]) for k in ("mla_q_norm", "mla_kv_norm", "hgrn_o_norm")})
    last["norm_gains"] = jnp.stack([jnp.concatenate(row, axis=0) for row in dgains])
    (last["hgrn_lb_logits"],) = lb_vjp(jnp.concatenate(dlb, axis=0))
    emit(0, gbuf0, last)
    return sq, grad_x


def _size(shape):
    n = 1
    for d in shape:
        n *= d
    return n


def _piece_rows(shape):
    return -(-_size(shape) // PACK_W)


def _packed_misc_rows(shapes):
    return sum(_piece_rows(s) for s in shapes)


def _cast_into(src, buf, row, name):
    rows, W = src.shape
    tr = min(256, rows)
    assert rows % tr == 0 and row % tr == 0

    def body(s_ref, b_ref, o_ref):
        o_ref[...] = s_ref[...].astype(BF16)

    return pl.pallas_call(
        body, name=name, grid=(rows // tr,),
        in_specs=[pl.BlockSpec((tr, W), lambda i: (i, 0)), pl.BlockSpec(memory_space=pl.ANY)],
        out_specs=pl.BlockSpec((tr, W), lambda i: (row // tr + i, 0)),
        out_shape=jax.ShapeDtypeStruct(buf.shape, buf.dtype), input_output_aliases={1: 0},
        compiler_params=_cparams("parallel"))(src, buf)


def _pack_blocks(pieces, rows, dtype):
    blocks, used = [], 0
    for p in pieces:
        flat = p.astype(dtype).reshape(-1)
        r = _piece_rows(p.shape)
        if r * PACK_W != flat.shape[0]:
            flat = jnp.pad(flat, (0, r * PACK_W - flat.shape[0]))
        blocks.append(flat.reshape(r, PACK_W))
        used += r
    if rows > used:
        blocks.append(jnp.zeros((rows - used, PACK_W), dtype))
    return blocks


def _unpack(buf, shapes):
    out, off = [], 0
    for shp in shapes:
        r = _piece_rows(shp)
        piece = buf[off:off + r]
        if r * PACK_W != _size(shp):
            piece = piece.reshape(-1)[:_size(shp)]
        out.append(piece.reshape(shp))
        off += r
    return out


def _mesh_place():
    x, y, c = lax.axis_index("x"), lax.axis_index("y"), lax.axis_index("c")
    chips = [(1 - x, y), (x, 1 - y), (1 - x, 1 - y)]
    return x, y, c, chips


_HBM = pl.BlockSpec(memory_space=pltpu.HBM)


def _share_reduced(q, name="grads_share_reduced"):
    rh, W = q.shape

    def body(q_ref, out_ref, send_sem, recv_sem):
        x, y, c, _ = _mesh_place()
        cp = pltpu.make_async_remote_copy(src_ref=q_ref, dst_ref=out_ref.at[c], send_sem=send_sem,
                                          recv_sem=recv_sem, device_id=(x, y, 1 - c), device_id_type=MESH)
        cp.start()
        cp.wait()

    out = pl.pallas_call(
        body, name=name, in_specs=[_HBM], out_specs=_HBM,
        out_shape=jax.ShapeDtypeStruct((2, rh, W), q.dtype),
        scratch_shapes=[pltpu.SemaphoreType.DMA, pltpu.SemaphoreType.DMA],
    )(q)
    return out


def _sum_chips(parts, own, own_row0, which, name, out_dtype=F32):
    n, rh, W = parts.shape
    tr = PACK_TILE
    assert own_row0 % tr == 0
    if own.ndim == 3:
        own_spec = pl.BlockSpec((None, tr, W), lambda i, w_ref: (w_ref[0], own_row0 // tr + i, 0))
    else:
        own_spec = pl.BlockSpec((tr, W), lambda i, w_ref: (own_row0 // tr + i, 0))

    def body(w_ref, p_ref, own_ref, o_ref):
        mine = own_ref[...].astype(F32)
        acc = None
        for j in range(n):
            term = jnp.where(w_ref[0] == j, mine, p_ref[j].astype(F32))
            acc = term if acc is None else acc + term
        o_ref[...] = acc.astype(out_dtype)

    return pl.pallas_call(
        body, name=name,
        grid_spec=pltpu.PrefetchScalarGridSpec(
            num_scalar_prefetch=1, grid=(rh // tr,),
            in_specs=[pl.BlockSpec((n, tr, W), lambda i, w_ref: (0, i, 0)), own_spec],
            out_specs=pl.BlockSpec((tr, W), lambda i, w_ref: (i, 0))),
        out_shape=jax.ShapeDtypeStruct((rh, W), out_dtype),
        compiler_params=_cparams("parallel"))(jnp.reshape(which, (1,)).astype(jnp.int32), parts, own)


_SEM = pl.BlockSpec(memory_space=pltpu.SEMAPHORE)
_ASYNC = pltpu.CompilerParams(has_side_effects=pltpu.SideEffectType.DATAFLOW_SIDE_EFFECTING)


def _hbm(a):
    return pltpu.with_memory_space_constraint(a, pltpu.HBM)


def _gather_copies(w_ref, land_ref, send_sems, recv_sems):
    x, y, c, chips = _mesh_place()
    me = 2 * x + y
    rh = w_ref.shape[0] // 2
    rows = pl.ds(pl.multiple_of(c * rh, 16), rh)
    return [pltpu.make_async_remote_copy(
        src_ref=w_ref.at[rows], dst_ref=land_ref.at[me, rows], send_sem=send_sems.at[r],
        recv_sem=recv_sems.at[r], device_id=(px, py, c), device_id_type=MESH)
        for r, (px, py) in enumerate(chips)]


def _scatter_copies(g_ref, land_ref, send_sems, recv_sems, row0):
    x, y, c, chips = _mesh_place()
    me = 2 * x + y
    rows = pl.ds(row0, land_ref.shape[1])
    return [pltpu.make_async_remote_copy(
        src_ref=g_ref.at[2 * px + py, rows], dst_ref=land_ref.at[me], send_sem=send_sems.at[r],
        recv_sem=recv_sems.at[r], device_id=(px, py, c), device_id_type=MESH)
        for r, (px, py) in enumerate(chips)]


def _halves_copies(land_ref, send_sems, recv_sems):
    x, y, c, chips = _mesh_place()
    rh = land_ref.shape[1] // 2
    rows = pl.ds(pl.multiple_of(c * rh, 16), rh)
    return [pltpu.make_async_remote_copy(
        src_ref=land_ref.at[2 * px + py, rows], dst_ref=land_ref.at[2 * px + py, rows], send_sem=send_sems.at[r],
        recv_sem=recv_sems.at[r], device_id=(x, y, 1 - c), device_id_type=MESH)
        for r, (px, py) in enumerate(chips)]


def _halves_to_sibling(land, name):
    def body(l_ref, o_ref, send_sems, recv_sems):
        copies = _halves_copies(o_ref, send_sems, recv_sems)
        for cp in copies:
            cp.start()
        for cp in copies:
            cp.wait()

    return pl.pallas_call(
        body, name=name, in_specs=[_HBM], out_specs=_HBM, out_shape=jax.ShapeDtypeStruct(land.shape, land.dtype),
        scratch_shapes=[pltpu.SemaphoreType.DMA((3,)), pltpu.SemaphoreType.DMA((3,))],
        input_output_aliases={0: 0})(land)


def _halves_start(land, name):
    def body(l_ref, send_sems, recv_sems, land_thru, token):
        for cp in _halves_copies(l_ref, send_sems, recv_sems):
            cp.start()
        token[...] = jnp.zeros_like(token)

    return pl.pallas_call(
        body, name=name,
        out_shape=(pltpu.SemaphoreType.DMA((3,)), pltpu.SemaphoreType.DMA((3,)), pltpu.HBM(land.shape, land.dtype),
                   jax.ShapeDtypeStruct((8, 128), F32)),
        in_specs=(_HBM,), out_specs=(_SEM, _SEM, _HBM, pl.BlockSpec(memory_space=pltpu.VMEM)),
        input_output_aliases={0: 2}, compiler_params=_ASYNC,
    )(_hbm(land))


def _halves_wait(send_sems, recv_sems, land_thru, after, name):
    rh = land_thru.shape[1] // 2

    def body(land_ref, send_sems, recv_sems, after_ref, got_ref):
        x, y, c, _ = _mesh_place()
        half = land_ref.at[0, pl.ds(0, rh)]
        for k in range(3):
            cp = pltpu.make_async_remote_copy(src_ref=half, dst_ref=half, send_sem=send_sems.at[k],
                                              recv_sem=recv_sems.at[k], device_id=(x, y, 1 - c),
                                              device_id_type=MESH)
            cp.wait_send()
            cp.wait_recv()

    return pl.pallas_call(
        body, name=name, out_shape=pltpu.HBM(land_thru.shape, land_thru.dtype),
        in_specs=(_HBM, _SEM, _SEM, pl.BlockSpec(memory_space=pl.ANY)), out_specs=_HBM,
        input_output_aliases={0: 0}, compiler_params=_ASYNC,
    )(land_thru, send_sems, recv_sems, after)


def _gather_start(wp, name):
    R, W = wp.shape

    def body(w_ref, land_ref, send_sems, recv_sems, w_thru, land_thru, token):
        for cp in _gather_copies(w_ref, land_ref, send_sems, recv_sems):
            cp.start()
        token[...] = jnp.zeros_like(token)

    return pl.pallas_call(
        body, name=name,
        out_shape=(pltpu.SemaphoreType.DMA((3,)), pltpu.SemaphoreType.DMA((3,)), pltpu.HBM(wp.shape, wp.dtype),
                   pltpu.HBM((N_CHIPS, R, W), wp.dtype), jax.ShapeDtypeStruct((8, 128), F32)),
        in_specs=(_HBM, _HBM),
        out_specs=(_SEM, _SEM, _HBM, _HBM, pl.BlockSpec(memory_space=pltpu.VMEM)),
        input_output_aliases={0: 2, 1: 3}, compiler_params=_ASYNC,
    )(_hbm(wp), _hbm(lax.empty((N_CHIPS, R, W), wp.dtype)))


def _gather_wait(send_sems, recv_sems, w_thru, land_thru, after, name):
    R, W = w_thru.shape
    rh = R // 2

    def body(w_ref, land_ref, send_sems, recv_sems, after_ref, w_dead, got_ref):
        x, y, c, _ = _mesh_place()
        half = land_ref.at[0, pl.ds(0, rh)]
        for k in range(3):
            cp = pltpu.make_async_remote_copy(src_ref=half, dst_ref=half, send_sem=send_sems.at[k],
                                              recv_sem=recv_sems.at[k], device_id=(x, y, 1 - c),
                                              device_id_type=MESH)
            cp.wait_send()
            cp.wait_recv()

    return pl.pallas_call(
        body, name=name,
        out_shape=(pltpu.HBM(w_thru.shape, w_thru.dtype), pltpu.HBM(land_thru.shape, land_thru.dtype)),
        in_specs=(_HBM, _HBM, _SEM, _SEM, pl.BlockSpec(memory_space=pl.ANY)), out_specs=(_HBM, _HBM),
        input_output_aliases={0: 0, 1: 1}, compiler_params=_ASYNC,
    )(w_thru, land_thru, send_sems, recv_sems, after)


def _scatter_start(g, row0, nrows, name):
    n, R, W = g.shape
    land_shape = (n, nrows, W)

    def body(g_ref, land_ref, send_sems, recv_sems, g_thru, land_thru, token):
        for cp in _scatter_copies(g_ref, land_ref, send_sems, recv_sems, row0):
            cp.start()
        token[...] = jnp.zeros_like(token)

    return pl.pallas_call(
        body, name=name,
        out_shape=(pltpu.SemaphoreType.DMA((3,)), pltpu.SemaphoreType.DMA((3,)), pltpu.HBM(g.shape, g.dtype),
                   pltpu.HBM(land_shape, g.dtype), jax.ShapeDtypeStruct((8, 128), F32)),
        in_specs=(_HBM, _HBM),
        out_specs=(_SEM, _SEM, _HBM, _HBM, pl.BlockSpec(memory_space=pltpu.VMEM)),
        input_output_aliases={0: 2, 1: 3}, compiler_params=_ASYNC,
    )(_hbm(g), _hbm(lax.empty(land_shape, g.dtype)))


def _scatter_wait(send_sems, recv_sems, g_thru, land_thru, after, name):
    def body(g_ref, land_ref, send_sems, recv_sems, after_ref, g_out, got_ref):
        x, y, c, _ = _mesh_place()
        for k in range(3):
            cp = pltpu.make_async_remote_copy(src_ref=land_ref.at[0], dst_ref=land_ref.at[0], send_sem=send_sems.at[k],
                                              recv_sem=recv_sems.at[k], device_id=(x, y, 1 - c),
                                              device_id_type=MESH)
            cp.wait_send()
            cp.wait_recv()

    return pl.pallas_call(
        body, name=name,
        out_shape=(pltpu.HBM(g_thru.shape, g_thru.dtype), pltpu.HBM(land_thru.shape, land_thru.dtype)),
        in_specs=(_HBM, _HBM, _SEM, _SEM, pl.BlockSpec(memory_space=pl.ANY)), out_specs=(_HBM, _HBM),
        input_output_aliases={0: 0, 1: 1}, compiler_params=_ASYNC,
    )(g_thru, land_thru, send_sems, recv_sems, after)


def _adamw(w, g, m, v, name):
    shape = w.shape
    cols = shape[-1]
    w2, g2, m2, v2 = (t.reshape(-1, cols) for t in (w, g, m, v))
    rows = w2.shape[0]
    tr = rows
    for cand in (512, 256, 128, 64, 32, 16, 8):
        if rows > cand and rows % cand == 0:
            tr = cand
            break
    c1 = 1.0 / (1.0 - ADAM_B1 ** ADAM_STEP)
    c2 = 1.0 / (1.0 - ADAM_B2 ** ADAM_STEP)

    def body(w_ref, g_ref, m_ref, v_ref, d_ref, nm_ref, nv_ref):
        gv = g_ref[...]
        nm = ADAM_B1 * m_ref[...] + (1.0 - ADAM_B1) * gv
        nv = ADAM_B2 * v_ref[...] + (1.0 - ADAM_B2) * (gv * gv)
        nm_ref[...] = nm
        nv_ref[...] = nv
        d_ref[...] = -ADAM_LR * ((nm * c1) / (jnp.sqrt(nv * c2) + ADAM_EPS) + ADAM_WD * w_ref[...])

    blk = pl.BlockSpec((tr, cols), lambda i: (i, 0))
    sds = jax.ShapeDtypeStruct((rows, cols), F32)
    d, nm, nv = pl.pallas_call(body, name=name, grid=(rows // tr,), in_specs=[blk] * 4,
                               out_specs=(blk, blk, blk), out_shape=(sds, sds, sds),
                               compiler_params=_cparams("parallel"))(w2, g2, m2, v2)
    return d.reshape(shape), nm.reshape(shape), nv.reshape(shape)


def kernel(x, positions, norm_gains, mla_w_in, mla_q_norm, mla_kv_norm, mla_w_uq, mla_w_ukv, mla_w_o, hgrn_w_in, hgrn_lb_logits, hgrn_o_norm, hgrn_w_o, mlp_w1, mlp_w2, loss_target, m_norm_gains, m_mla_w_in, m_mla_q_norm, m_mla_kv_norm, m_mla_w_uq, m_mla_w_ukv, m_mla_w_o, m_hgrn_w_in, m_hgrn_lb_logits, m_hgrn_o_norm, m_hgrn_w_o, m_mlp_w1, m_mlp_w2, v_norm_gains, v_mla_w_in, v_mla_q_norm, v_mla_kv_norm, v_mla_w_uq, v_mla_w_ukv, v_mla_w_o, v_hgrn_w_in, v_hgrn_lb_logits, v_hgrn_o_norm, v_hgrn_w_o, v_mlp_w1, v_mlp_w2):
    w = dict(norm_gains=norm_gains, mla_w_in=mla_w_in, mla_q_norm=mla_q_norm, mla_kv_norm=mla_kv_norm,
             mla_w_uq=mla_w_uq, mla_w_ukv=mla_w_ukv, mla_w_o=mla_w_o, hgrn_w_in=hgrn_w_in,
             hgrn_lb_logits=hgrn_lb_logits, hgrn_o_norm=hgrn_o_norm, hgrn_w_o=hgrn_w_o,
             mlp_w1=mlp_w1, mlp_w2=mlp_w2)
    mom_m = dict(norm_gains=m_norm_gains, mla_w_in=m_mla_w_in, mla_q_norm=m_mla_q_norm,
                 mla_kv_norm=m_mla_kv_norm, mla_w_uq=m_mla_w_uq, mla_w_ukv=m_mla_w_ukv,
                 mla_w_o=m_mla_w_o, hgrn_w_in=m_hgrn_w_in, hgrn_lb_logits=m_hgrn_lb_logits,
                 hgrn_o_norm=m_hgrn_o_norm, hgrn_w_o=m_hgrn_w_o, mlp_w1=m_mlp_w1, mlp_w2=m_mlp_w2)
    mom_v = dict(norm_gains=v_norm_gains, mla_w_in=v_mla_w_in, mla_q_norm=v_mla_q_norm,
                 mla_kv_norm=v_mla_kv_norm, mla_w_uq=v_mla_w_uq, mla_w_ukv=v_mla_w_ukv,
                 mla_w_o=v_mla_w_o, hgrn_w_in=v_hgrn_w_in, hgrn_lb_logits=v_hgrn_lb_logits,
                 hgrn_o_norm=v_hgrn_o_norm, hgrn_w_o=v_hgrn_w_o, mlp_w1=v_mlp_w1, mlp_w2=v_mlp_w2)
    c = lax.axis_index("c")

    axis_of = dict(SHARDED)
    me = 2 * lax.axis_index("x") + lax.axis_index("y")
    gain_bits = lax.bitcast_convert_type(norm_gains, jnp.uint32)
    gain_hi = lax.bitcast_convert_type((gain_bits >> 16).astype(jnp.uint16), BF16)
    gain_lo = lax.bitcast_convert_type((gain_bits & 0xFFFF).astype(jnp.uint16), BF16)

    layers = []
    for l in range(DEPTH):
        s = l // 2
        if l % 2 == 0:
            big = [("mlp_w1", l), ("mlp_w2", l), ("mla_w_o", s)]
            tail = [("mla_w_in", s), ("mla_w_uq", s), ("mla_w_ukv", s)]
        else:
            big = [("hgrn_w_in", s), ("mlp_w1", l), ("mlp_w2", l), ("hgrn_w_o", s)]
            tail = []
        w_tail = [w[n][i] for n, i in tail] + ([gain_hi, gain_lo] if l == 0 else [])
        g_tail = tail + ([("norm_gains", None)] + [(n, None) for n in REPLICATED] if l == 0 else [])
        g_shapes = [w[n].shape if i is None else w[n][i].shape for n, i in g_tail]
        tail_rows = max(_packed_misc_rows([t.shape for t in w_tail]), _packed_misc_rows(g_shapes))
        pk = _Packed([(n, w[n].shape[1]) for n, _ in big], tail_rows)
        wpack = jnp.zeros((pk.rows, PACK_W), BF16)
        for n, i in big:
            assert w[n].shape[2] == PACK_W
            wpack = _cast_into(w[n][i], wpack, pk.off[n], name="pack_%s_%d" % (n, l))
        if w_tail:
            wpack = lax.dynamic_update_slice(
                wpack, jnp.concatenate(_pack_blocks(w_tail, 0, BF16), axis=0), (pk.misc, 0))
        layers.append(dict(pk=pk, big=big, tail=tail, w_tail=w_tail, g_tail=g_tail, g_shapes=g_shapes,
                           gather=_gather_start(wpack, name="gather_start_%d" % l)))

    def prefetch(l, after):
        lay = layers[l]
        send_sems, recv_sems, w_thru, land_thru, _ = lay["gather"]
        lay["w_back"], land = _gather_wait(send_sems, recv_sems, w_thru, land_thru, after,
                                           name="gather_wait_%d" % l)
        lay["halves"] = _halves_start(land, name="gather_halves_start_%d" % l)
        return lay["halves"][3][0, 0]

    def fetch(l, after):
        lay = layers[l]
        pk = lay["pk"]
        if l == 0:
            send_sems, recv_sems, w_thru, land_thru, _ = lay["gather"]
            after = sum(layers[k]["gather"][4] for k in range(1, DEPTH))
            w_back, land = _gather_wait(send_sems, recv_sems, w_thru, land_thru, after, name="gather_wait_0")
            land = _halves_to_sibling(land, name="gather_halves_0")
        else:
            send_sems, recv_sems, land_thru, _ = lay["halves"]
            w_back = lay["w_back"]
            land = _halves_wait(send_sems, recv_sems, land_thru, after, name="gather_halves_wait_%d" % l)
        land = lax.dynamic_update_slice(land, w_back[None], (me, 0, 0))
        out = dict(wbuf=land.reshape(N_CHIPS * pk.rows, PACK_W), pk=pk)
        if lay["w_tail"]:
            rows = _packed_misc_rows([t.shape for t in lay["w_tail"]])
            per_chip = [_unpack(land[j, pk.misc:pk.misc + rows], [t.shape for t in lay["w_tail"]])
                        for j in range(N_CHIPS)]
            for i, (n, _) in enumerate(lay["tail"]):
                out[n[4:]] = jnp.concatenate([per_chip[j][i] for j in range(N_CHIPS)], axis=axis_of[n] - 1)
            if l == 0:
                got_hi, got_lo = (lax.bitcast_convert_type(
                    jnp.concatenate([per_chip[j][i] for j in range(N_CHIPS)], axis=2),
                    jnp.uint16).astype(jnp.uint32) for i in (-2, -1))
                out["gains"] = lax.bitcast_convert_type((got_hi << 16) | got_lo, F32)
        return out

    def emit(l, gbuf, grads):
        lay = layers[l]
        pk = lay["pk"]
        if lay["g_tail"]:
            for j in range(N_CHIPS):
                pieces = []
                for n, i in lay["g_tail"]:
                    if n not in axis_of:
                        pieces.append(grads[n])
                    else:
                        pieces.append(jnp.split(grads[n], N_CHIPS, axis=axis_of[n] - (0 if i is None else 1))[j])
                block = jnp.concatenate(_pack_blocks(pieces, 0, BF16), axis=0)
                gbuf = lax.dynamic_update_slice(gbuf, block, (j * pk.rows + pk.misc, 0))
        row0 = lay.get("early_rows", 0)
        lay["scatter"] = _scatter_start(gbuf.reshape(N_CHIPS, pk.rows, PACK_W), row0, pk.rows - row0,
                                        name="scatter_start_%d" % l)
        return lay["scatter"][4][0, 0]

    def emit_mlp(l, gbuf):
        lay = layers[l]
        pk = lay["pk"]
        assert pk.off["mlp_w1"] == 0 and pk.off["mlp_w2"] == w["mlp_w1"].shape[1]
        lay["early_rows"] = w["mlp_w1"].shape[1] + w["mlp_w2"].shape[1]
        lay["scatter_early"] = _scatter_start(gbuf.reshape(N_CHIPS, pk.rows, PACK_W), 0, lay["early_rows"],
                                              name="scatter_start_%d_mlp" % l)
        return lay["scatter_early"][2].reshape(N_CHIPS * pk.rows, PACK_W)

    small = dict(mla_q_norm=mla_q_norm, mla_kv_norm=mla_kv_norm, hgrn_lb_logits=hgrn_lb_logits,
                 hgrn_o_norm=hgrn_o_norm)
    gbufs = [lax.empty((N_CHIPS * lay["pk"].rows, PACK_W), BF16) for lay in layers]
    sq, grad_x = _local_step(x[0], positions[0], loss_target[0], small, prefetch, fetch, gbufs, emit, emit_mlp)
    d_model = x.shape[-1]
    loss = lax.psum(0.5 * jnp.sum(sq) / d_model, ("x", "y", "c"))

    per_name = {}
    behind = grad_x
    for l, lay in reversed(list(enumerate(layers))):
        pk = lay["pk"]
        send_sems, recv_sems, g_thru, land_thru, _ = lay["scatter"]
        row0 = lay.get("early_rows", 0)
        early = None
        if row0:
            e_send, e_recv, _, e_land, _ = lay["scatter_early"]
            g_thru, land = _scatter_wait(e_send, e_recv, g_thru, e_land, behind, name="scatter_wait_%d_mlp" % l)
            early = behind = _sum_chips(land, g_thru, 0, me, name="grads_sum_chips_%d_mlp" % l, out_dtype=BF16)
        g_back, land = _scatter_wait(send_sems, recv_sems, g_thru, land_thru, behind, name="scatter_wait_%d" % l)
        mine = _sum_chips(land, g_back, row0, me, name="grads_sum_chips_%d" % l, out_dtype=BF16)
        if early is not None:
            mine = jnp.concatenate([early, mine], axis=0)
        red = behind = _sum_chips(_share_reduced(mine, name="grads_share_%d" % l), mine, 0, c,
                                  name="grads_sum_cores_%d" % l)
        for n, i in lay["big"]:
            per_name.setdefault(n, {})[i] = red[pk.off[n]:pk.off[n] + w[n].shape[1]]
        for (n, i), piece in zip(lay["g_tail"], _unpack(red[pk.misc:pk.misc + pk.misc_rows], lay["g_shapes"])):
            per_name.setdefault(n, {})[i] = piece
    g_out = {n: (parts[None] if None in parts else jnp.stack([parts[i] for i in sorted(parts)]))
             for n, parts in per_name.items()}

    deltas, new_m, new_v = {}, {}, {}
    for name in WEIGHTS:
        deltas[name], new_m[name], new_v[name] = _adamw(w[name], g_out[name], mom_m[name], mom_v[name],
                                                        name="adamw_" + name)
    return (loss, grad_x[None], *[g_out[n] for n in WEIGHTS], *[deltas[n] for n in WEIGHTS],
            *[new_m[n] for n in WEIGHTS], *[new_v[n] for n in WEIGHTS])
```

```python
import jax
import jax.numpy as jnp
from jax import lax
from jax.experimental import pallas as pl
from jax.experimental.pallas import tpu as pltpu

F32 = jnp.float32
BF16 = jnp.bfloat16
MESH = pl.DeviceIdType.MESH

DEPTH = 4
MLA_HEADS = 8
MLA_NOPE = 128
MLA_ROPE = 64
MLA_V = 128
MLA_QK_PAD = 256
MLA_HEADS_PER_STEP = 2
MLA_SCALE = float(MLA_NOPE + MLA_ROPE) ** -0.5
ROPE_BASE = 10000.0
HGRN_HEADS = 8
HGRN_CHUNK = 32
HGRN_BLOCK = 128
EPS = 1e-6

ADAM_LR = 0.001
ADAM_B1 = 0.9
ADAM_B2 = 0.999
ADAM_EPS = 1e-08
ADAM_WD = 0.01
ADAM_STEP = 10

N_CHIPS = 4
PACK_W = 1024
PACK_ALIGN = 1024
PACK_TILE = 512
V7X_VMEM_LIMIT = 56 * 1024 * 1024

SHARDED = (("norm_gains", 2), ("mla_w_in", 1), ("mla_w_uq", 2), ("mla_w_ukv", 2), ("mla_w_o", 1),
           ("hgrn_w_in", 2), ("hgrn_w_o", 1), ("mlp_w1", 2), ("mlp_w2", 1))
REPLICATED = ("mla_q_norm", "mla_kv_norm", "hgrn_lb_logits", "hgrn_o_norm")
WEIGHTS = ("norm_gains", "mla_w_in", "mla_q_norm", "mla_kv_norm", "mla_w_uq", "mla_w_ukv", "mla_w_o",
           "hgrn_w_in", "hgrn_lb_logits", "hgrn_o_norm", "hgrn_w_o", "mlp_w1", "mlp_w2")


def _cparams(*semantics):
    return pltpu.CompilerParams(dimension_semantics=semantics, vmem_limit_bytes=V7X_VMEM_LIMIT)


def _sigmoid(x):
    return 0.5 * jnp.tanh(0.5 * x) + 0.5


def _mm(a, b, *, ta=False, tb=False, out_dtype=F32, tm=2048, tn=1024, tk=1024, epi=None, extra=None,
        name="mm", n=None, b_map=None, into=None, o_map=None):
    if ta:
        K, M = a.shape
    else:
        M, K = a.shape
    if b_map is not None:
        N = n
    elif tb:
        N, Kb = b.shape
    else:
        Kb, N = b.shape
    assert b_map is not None or K == Kb, (a.shape, b.shape, ta, tb)
    tm, tn = min(tm, M), min(tn, N)
    if ta and b_map is None:
        tk = max(tk, 4096)
    tk = K if (K <= 1024 and b_map is None) else min(tk, K)
    assert M % tm == 0 and N % tn == 0 and K % tk == 0, (M, N, K, tm, tn, tk)
    nk = K // tk
    a_spec = (pl.BlockSpec((tk, tm), lambda i, j, k: (k, i)) if ta
              else pl.BlockSpec((tm, tk), lambda i, j, k: (i, k)))
    if b_map is None:
        b_map = (lambda i, j, k: (j, k)) if tb else (lambda i, j, k: (k, j))
    b_spec = pl.BlockSpec((tn, tk) if tb else (tk, tn), b_map)
    o_spec = pl.BlockSpec((tm, tn), lambda i, j, k: (i, j))
    dims = (((0 if ta else 1,), (1 if tb else 0,)), ((), ()))
    in_specs = [a_spec, b_spec]
    operands = [a, b]
    aliases = {}
    if epi == "mul2r":
        in_specs.append(o_spec)
        operands.append(extra)
    if epi == "resnorm":
        assert tn == N
        vec = pl.BlockSpec((1, N), lambda i, j, k: (0, 0))
        in_specs += [o_spec, vec, vec]
        operands += list(extra)
    if into is not None:
        assert epi is None
        in_specs.append(pl.BlockSpec(memory_space=pl.ANY))
        operands.append(into)
        aliases = {2: 0}
        out_dtype = into.dtype
        out_shape = jax.ShapeDtypeStruct(into.shape, into.dtype)
        out_specs = pl.BlockSpec((tm, tn), o_map)
    elif epi == "relu2":
        out_shape = (jax.ShapeDtypeStruct((M, N), BF16), jax.ShapeDtypeStruct((M, N), BF16))
        out_specs = (o_spec, o_spec)
    elif epi == "mul2r":
        out_shape = jax.ShapeDtypeStruct((M, N), BF16)
        out_specs = o_spec
    elif epi == "resnorm":
        out_shape = (jax.ShapeDtypeStruct((M, N), F32), jax.ShapeDtypeStruct((M, N), F32),
                     jax.ShapeDtypeStruct((M, N), BF16))
        out_specs = (o_spec, o_spec, o_spec)
    else:
        out_shape = jax.ShapeDtypeStruct((M, N), out_dtype)
        out_specs = o_spec
    n_in = len(operands)
    n_out = {"relu2": 2, "resnorm": 3}.get(epi, 1)

    def body(*refs):
        a_ref, b_ref = refs[0], refs[1]
        outs = refs[n_in:n_in + n_out]
        k = pl.program_id(2)

        def finish(acc):
            if epi == "relu2":
                r = jnp.maximum(acc, 0.0)
                outs[0][...] = (r * r).astype(BF16)
                outs[1][...] = r.astype(BF16)
            elif epi == "mul2r":
                outs[0][...] = (acc * (2.0 * refs[2][...].astype(F32))).astype(BF16)
            elif epi == "resnorm":
                h_ref, gp_ref, gn_ref = refs[2], refs[3], refs[4]
                hn = h_ref[...] + acc * _rms_rstd(acc) * gp_ref[...]
                outs[0][...] = acc
                outs[1][...] = hn
                outs[2][...] = (hn * _rms_rstd(hn) * gn_ref[...]).astype(BF16)
            else:
                outs[0][...] = acc.astype(out_dtype)

        part = lax.dot_general(a_ref[...], b_ref[...], dims, preferred_element_type=F32)
        if nk == 1:
            finish(part)
            return
        acc_ref = refs[-1]

        @pl.when(k == 0)
        def _():
            acc_ref[...] = part

        @pl.when((k > 0) & (k < nk - 1))
        def _():
            acc_ref[...] += part

        @pl.when(k == nk - 1)
        def _():
            finish(acc_ref[...] + part)

    return pl.pallas_call(
        body, name=name, grid=(M // tm, N // tn, nk), in_specs=in_specs, out_specs=out_specs,
        out_shape=out_shape, scratch_shapes=[pltpu.VMEM((tm, tn), F32)] if nk > 1 else [],
        input_output_aliases=aliases,
        compiler_params=_cparams("parallel", "parallel", "arbitrary"))(*operands)


def _rms_rstd(x):
    return lax.rsqrt(jnp.mean(x * x, axis=-1, keepdims=True) + EPS)


def _rms_bwd_tile(x, g, dy):
    r = _rms_rstd(x)
    xh = x * r
    u = dy * g
    dx = r * (u - xh * jnp.mean(u * xh, axis=-1, keepdims=True))
    dg = jnp.sum(dy * xh, axis=0, keepdims=True)
    return dx, dg


def _row_tile(T):
    return min(512, T)


def _mid_tile(T):
    return min(512, T)


def _prenorm_fwd(x, g, name="prenorm_fwd"):
    T, D = x.shape
    tm = _row_tile(T)

    def body(x_ref, g_ref, a_ref):
        xv = x_ref[...]
        a_ref[...] = (xv * _rms_rstd(xv) * g_ref[...]).astype(BF16)

    row = pl.BlockSpec((tm, D), lambda i: (i, 0))
    vec = pl.BlockSpec((1, D), lambda i: (0, 0))
    return pl.pallas_call(body, name=name, grid=(T // tm,), in_specs=[row, vec], out_specs=row,
                          out_shape=jax.ShapeDtypeStruct((T, D), BF16),
                          compiler_params=_cparams("parallel"))(x, g)


def _resnorm_loss(h, z, g_post, target, name="resnorm_loss"):
    T, D = h.shape
    tm = _row_tile(T)

    def body(h_ref, z_ref, gp_ref, t_ref, dy_ref, sq_ref):
        zv = z_ref[...]
        err = h_ref[...] + zv * _rms_rstd(zv) * gp_ref[...] - t_ref[...]
        dy_ref[...] = err * (1.0 / D)

        @pl.when(pl.program_id(0) == 0)
        def _():
            sq_ref[...] = jnp.zeros_like(sq_ref)

        sq_ref[...] += jnp.sum(err * err, axis=0, keepdims=True)

    row = pl.BlockSpec((tm, D), lambda i: (i, 0))
    vec = pl.BlockSpec((1, D), lambda i: (0, 0))
    return pl.pallas_call(body, name=name, grid=(T // tm,), in_specs=[row, row, vec, row],
                          out_specs=(row, vec),
                          out_shape=(jax.ShapeDtypeStruct((T, D), F32), jax.ShapeDtypeStruct((1, D), F32)),
                          compiler_params=_cparams("arbitrary"))(h, z, g_post, target)


def _resnorm_bwd(z, g_post, dh, h_new=None, da=None, g_pre=None, name="resnorm_bwd"):
    T, D = z.shape
    tm = _row_tile(T)
    has_next = h_new is not None
    row = pl.BlockSpec((tm, D), lambda i: (i, 0))
    vec = pl.BlockSpec((1, D), lambda i: (0, 0))

    if has_next:
        def body(z_ref, gp_ref, dh_ref, hn_ref, da_ref, gn_ref, t_ref, dz_ref, dgp_ref, dgn_ref):
            first = pl.program_id(0) == 0

            @pl.when(first)
            def _():
                dgp_ref[...] = jnp.zeros_like(dgp_ref)
                dgn_ref[...] = jnp.zeros_like(dgn_ref)

            dpre, dgn = _rms_bwd_tile(hn_ref[...], gn_ref[...], da_ref[...])
            t = dh_ref[...] + dpre
            t_ref[...] = t
            dz, dgp = _rms_bwd_tile(z_ref[...], gp_ref[...], t)
            dz_ref[...] = dz.astype(BF16)
            dgp_ref[...] += dgp
            dgn_ref[...] += dgn

        return pl.pallas_call(
            body, name=name, grid=(T // tm,), in_specs=[row, vec, row, row, row, vec],
            out_specs=(row, row, vec, vec),
            out_shape=(jax.ShapeDtypeStruct((T, D), F32), jax.ShapeDtypeStruct((T, D), BF16),
                       jax.ShapeDtypeStruct((1, D), F32), jax.ShapeDtypeStruct((1, D), F32)),
            compiler_params=_cparams("arbitrary"))(z, g_post, dh, h_new, da, g_pre)

    def body_last(z_ref, gp_ref, dh_ref, dz_ref, dgp_ref):
        @pl.when(pl.program_id(0) == 0)
        def _():
            dgp_ref[...] = jnp.zeros_like(dgp_ref)

        dz, dgp = _rms_bwd_tile(z_ref[...], gp_ref[...], dh_ref[...])
        dz_ref[...] = dz.astype(BF16)
        dgp_ref[...] += dgp

    return pl.pallas_call(
        body_last, name=name, grid=(T // tm,), in_specs=[row, vec, row], out_specs=(row, vec),
        out_shape=(jax.ShapeDtypeStruct((T, D), BF16), jax.ShapeDtypeStruct((1, D), F32)),
        compiler_params=_cparams("arbitrary"))(z, g_post, dh)


def _prenorm_bwd(x, g, dh, da, name="prenorm_bwd"):
    T, D = x.shape
    tm = _row_tile(T)

    def body(x_ref, g_ref, dh_ref, da_ref, dx_ref, dg_ref):
        @pl.when(pl.program_id(0) == 0)
        def _():
            dg_ref[...] = jnp.zeros_like(dg_ref)

        dpre, dg = _rms_bwd_tile(x_ref[...], g_ref[...], da_ref[...])
        dx_ref[...] = dh_ref[...] + dpre
        dg_ref[...] += dg

    row = pl.BlockSpec((tm, D), lambda i: (i, 0))
    vec = pl.BlockSpec((1, D), lambda i: (0, 0))
    return pl.pallas_call(
        body, name=name, grid=(T // tm,), in_specs=[row, vec, row, row], out_specs=(row, vec),
        out_shape=(jax.ShapeDtypeStruct((T, D), F32), jax.ShapeDtypeStruct((1, D), F32)),
        compiler_params=_cparams("arbitrary"))(x, g, dh, da)


class _Packed:
    def __init__(self, big, misc_rows):
        self.big = tuple(big)
        self.off = {}
        r = 0
        for name, rows in big:
            self.off[name] = r
            r += rows
        self.misc, self.misc_rows = r, misc_rows
        self.rows = -(-(r + misc_rows) // PACK_ALIGN) * PACK_ALIGN

    def block(self, name, layer, unit):
        r = self.off[name]
        assert r % unit == 0 and self.rows % unit == 0
        return r // unit, self.rows // unit


def _col_sharded(pk, name, layer, unit):
    base, stride = pk.block(name, layer, unit)
    return (lambda i, j, k: (j * stride + base, 0)), (lambda i, j, k: (k * stride + base, 0))


def _row_sharded(pk, name, layer, unit):
    base, stride = pk.block(name, layer, unit)
    return ((lambda i, j, k: (k * stride + base, 0)), (lambda i, j, k: (j * stride + base, 0)),
            (lambda i, j, k: (i * stride + base, 0)))


def _mlp_fwd(a, wbuf, pk, layer, res):
    D = a.shape[1]
    by_n, _ = _col_sharded(pk, "mlp_w1", layer, D)
    by_k, _, _ = _row_sharded(pk, "mlp_w2", layer, D)
    act, r = _mm(a, wbuf, n=4 * D, b_map=by_n, tk=D, tn=D, epi="relu2", name="mlp_up")
    if res is None:
        return _mm(act, wbuf, n=D, b_map=by_k, tk=D, tn=D, name="mlp_down"), (a, act, r), None, None
    u, h_new, a_next = _mm(act, wbuf, n=D, b_map=by_k, tm=1024, tk=D, tn=D, epi="resnorm", extra=res,
                           name="mlp_down_res")
    return u, (a, act, r), h_new, a_next


def _mlp_bwd(du, saved, wbuf, gbuf, pk, layer):
    a, act, r = saved
    D = a.shape[1]
    w1_by_n, w1_by_k = _col_sharded(pk, "mlp_w1", layer, D)
    _, w2_by_n, w2_by_m = _row_sharded(pk, "mlp_w2", layer, D)
    dz1 = _mm(du, wbuf, tb=True, n=4 * D, b_map=w2_by_n, tn=D, tk=D, epi="mul2r", extra=r, name="mlp_down_dx")
    gbuf = _mm(act, du, ta=True, into=gbuf, o_map=w2_by_m, tm=D, tn=D, name="mlp_down_dw")
    gbuf = _mm(a, dz1, ta=True, into=gbuf, o_map=w1_by_n, tm=D, tn=D, name="mlp_up_dw")
    da = _mm(dz1, wbuf, tb=True, n=D, b_map=w1_by_k, tn=D, tk=D, name="mlp_up_dx")
    return da, gbuf


def _rope_swap(t):
    n = t.shape[-1]
    lane = lax.broadcasted_iota(jnp.int32, t.shape, t.ndim - 1)
    half = MLA_ROPE // 2
    first = (lane & (MLA_ROPE - 1)) < half
    return jnp.where(first, pltpu.roll(t, n - half, t.ndim - 1), pltpu.roll(t, half, t.ndim - 1))


def _mla_mid_fwd(proj, q_norm, kv_norm, w_uq, w_ukv, cc, ss):
    T, PW = proj.shape
    QL, KVL = q_norm.shape[-1], kv_norm.shape[-1]
    H = MLA_HEADS
    assert PW == QL + KVL + 128
    tm = _mid_tile(T)

    def body(p_ref, qn_ref, kn_ref, wq_ref, wkv_ref, cc_ref, ss_ref,
             cq_ref, ckv_ref, q_ref, k_ref, v_ref):
        cq = p_ref[:, 0:QL]
        ckv = p_ref[:, QL:QL + KVL]
        kr = p_ref[:, QL + KVL:QL + KVL + 128]
        c, s = cc_ref[...], ss_ref[...]
        cqn = (cq * _rms_rstd(cq) * qn_ref[...]).astype(BF16)
        ckvn = (ckv * _rms_rstd(ckv) * kn_ref[...]).astype(BF16)
        cq_ref[...] = cqn
        ckv_ref[...] = ckvn
        q = jnp.dot(cqn, wq_ref[...], preferred_element_type=F32)
        kv = jnp.dot(ckvn, wkv_ref[...], preferred_element_type=F32)
        krf = (kr * c + _rope_swap(kr) * s).astype(BF16)
        for h in range(H):
            o = h * MLA_QK_PAD
            q_ref[:, o:o + MLA_NOPE] = (q[:, o:o + MLA_NOPE] * MLA_SCALE).astype(BF16)
            qr = q[:, o + MLA_NOPE:o + MLA_QK_PAD]
            q_ref[:, o + MLA_NOPE:o + MLA_QK_PAD] = ((qr * c + _rope_swap(qr) * s) * MLA_SCALE).astype(BF16)
            k_ref[:, o:o + MLA_NOPE] = kv[:, o:o + MLA_NOPE].astype(BF16)
            k_ref[:, o + MLA_NOPE:o + MLA_QK_PAD] = krf
            v_ref[:, h * MLA_V:(h + 1) * MLA_V] = kv[:, o + MLA_NOPE:o + MLA_QK_PAD].astype(BF16)

    def row(w):
        return pl.BlockSpec((tm, w), lambda i: (i, 0))

    def full(shape):
        return pl.BlockSpec(shape, lambda i: (0, 0))

    return pl.pallas_call(
        body, name="mla_mid_fwd", grid=(T // tm,),
        in_specs=[row(PW), full((1, QL)), full((1, KVL)), full(w_uq.shape), full(w_ukv.shape),
                  row(128), row(128)],
        out_specs=(row(QL), row(KVL), row(H * MLA_QK_PAD), row(H * MLA_QK_PAD), row(H * MLA_V)),
        out_shape=(jax.ShapeDtypeStruct((T, QL), BF16), jax.ShapeDtypeStruct((T, KVL), BF16),
                   jax.ShapeDtypeStruct((T, H * MLA_QK_PAD), BF16),
                   jax.ShapeDtypeStruct((T, H * MLA_QK_PAD), BF16),
                   jax.ShapeDtypeStruct((T, H * MLA_V), BF16)),
        compiler_params=_cparams("parallel"))(proj, q_norm, kv_norm, w_uq, w_ukv, cc, ss)


def _mla_mid_bwd(proj, q_norm, kv_norm, w_uq, w_ukv, cc, ss, dq, dk, dv):
    T, PW = proj.shape
    QL, KVL = q_norm.shape[-1], kv_norm.shape[-1]
    H = MLA_HEADS
    tm = _mid_tile(T)
    nt = (((1,), (1,)), ((), ()))

    def body(p_ref, qn_ref, kn_ref, wq_ref, wkv_ref, cc_ref, ss_ref, dq_ref, dk_ref, dv_ref,
             dqp_ref, dkv_ref, dp_ref, dqn_ref, dkn_ref):
        @pl.when(pl.program_id(0) == 0)
        def _():
            dqn_ref[...] = jnp.zeros_like(dqn_ref)
            dkn_ref[...] = jnp.zeros_like(dkn_ref)

        c, s = cc_ref[...], ss_ref[...]
        dkr = jnp.zeros((tm, 128), F32)
        for h in range(H):
            o = h * MLA_QK_PAD
            dqp_ref[:, o:o + MLA_NOPE] = (dq_ref[:, o:o + MLA_NOPE] * MLA_SCALE).astype(BF16)
            dqr = dq_ref[:, o + MLA_NOPE:o + MLA_QK_PAD] * MLA_SCALE
            dqp_ref[:, o + MLA_NOPE:o + MLA_QK_PAD] = (dqr * c + _rope_swap(dqr * s)).astype(BF16)
            dkv_ref[:, o:o + MLA_NOPE] = dk_ref[:, o:o + MLA_NOPE].astype(BF16)
            dkv_ref[:, o + MLA_NOPE:o + MLA_QK_PAD] = dv_ref[:, h * MLA_V:(h + 1) * MLA_V].astype(BF16)
            dkr = dkr + dk_ref[:, o + MLA_NOPE:o + MLA_QK_PAD]
        dcqn = lax.dot_general(dqp_ref[...], wq_ref[...], nt, preferred_element_type=F32)
        dckvn = lax.dot_general(dkv_ref[...], wkv_ref[...], nt, preferred_element_type=F32)
        dcq, dqn = _rms_bwd_tile(p_ref[:, 0:QL], qn_ref[...], dcqn)
        dckv, dkn = _rms_bwd_tile(p_ref[:, QL:QL + KVL], kn_ref[...], dckvn)
        dp_ref[:, 0:QL] = dcq.astype(BF16)
        dp_ref[:, QL:QL + KVL] = dckv.astype(BF16)
        dp_ref[:, QL + KVL:QL + KVL + 128] = (dkr * c + _rope_swap(dkr * s)).astype(BF16)
        dqn_ref[...] += dqn
        dkn_ref[...] += dkn

    def row(w):
        return pl.BlockSpec((tm, w), lambda i: (i, 0))

    def full(shape):
        return pl.BlockSpec(shape, lambda i: (0, 0))

    return pl.pallas_call(
        body, name="mla_mid_bwd", grid=(T // tm,),
        in_specs=[row(PW), full((1, QL)), full((1, KVL)), full(w_uq.shape), full(w_ukv.shape),
                  row(128), row(128), row(H * MLA_QK_PAD), row(H * MLA_QK_PAD), row(H * MLA_V)],
        out_specs=(row(H * MLA_QK_PAD), row(H * MLA_QK_PAD), row(PW), full((1, QL)), full((1, KVL))),
        out_shape=(jax.ShapeDtypeStruct((T, H * MLA_QK_PAD), BF16),
                   jax.ShapeDtypeStruct((T, H * MLA_QK_PAD), BF16),
                   jax.ShapeDtypeStruct((T, PW), BF16),
                   jax.ShapeDtypeStruct((1, QL), F32), jax.ShapeDtypeStruct((1, KVL), F32)),
        compiler_params=_cparams("arbitrary"))(proj, q_norm, kv_norm, w_uq, w_ukv, cc, ss, dq, dk, dv)


def _attn_tile(T):
    return min(1024, T)


def _attn_pairs(n, by_key):
    if by_key:
        pairs = [(qi, ki) for ki in range(n) for qi in range(ki, n)]
    else:
        pairs = [(qi, ki) for qi in range(n) for ki in range(qi + 1)]
    return (jnp.asarray([p[0] for p in pairs], jnp.int32), jnp.asarray([p[1] for p in pairs], jnp.int32))


def _scores(q, k, diagonal):
    s = lax.dot_general(q, k, (((1,), (1,)), ((), ())), preferred_element_type=F32)
    if diagonal:
        rows = lax.broadcasted_iota(jnp.int32, s.shape, 0)
        cols = lax.broadcasted_iota(jnp.int32, s.shape, 1)
        s = jnp.where(rows >= cols, s, -jnp.inf)
    return s


def _attn_fwd(q, k, v):
    T = q.shape[0]
    H, DQ, DV = MLA_HEADS, MLA_QK_PAD, MLA_V
    tq = _attn_tile(T)
    nq = T // tq
    G = MLA_HEADS_PER_STEP
    qi_tab, ki_tab = _attn_pairs(nq, by_key=False)

    def body(qi_ref, ki_ref, q_ref, k_ref, v_ref, o_ref, lse_ref, *scratch):
        m_refs, l_refs, acc_refs = scratch[0:G], scratch[G:2 * G], scratch[2 * G:3 * G]
        p = pl.program_id(1)
        qi, ki = qi_ref[p], ki_ref[p]

        @pl.when(ki == 0)
        def _():
            for g in range(G):
                m_refs[g][...] = jnp.full_like(m_refs[g], -jnp.inf)
                l_refs[g][...] = jnp.zeros_like(l_refs[g])
                acc_refs[g][...] = jnp.zeros_like(acc_refs[g])

        def update(ks, qr, masked):
            for g in range(G):
                qs, vs = slice(g * DQ, (g + 1) * DQ), slice(g * DV, (g + 1) * DV)
                st = _scores(k_ref[ks, qs], q_ref[qr, qs], False)
                if masked:
                    key = ks.start + lax.broadcasted_iota(jnp.int32, st.shape, 0)
                    qry = qr.start + lax.broadcasted_iota(jnp.int32, st.shape, 1)
                    st = jnp.where(qry >= key, st, -jnp.inf)
                m_prev = m_refs[g][:, qr]
                m_new = jnp.maximum(m_prev, jnp.max(st, axis=0, keepdims=True))
                alpha = jnp.exp(m_prev - m_new)
                pt = jnp.exp(st - m_new)
                l_refs[g][:, qr] = alpha * l_refs[g][:, qr] + jnp.sum(pt, axis=0, keepdims=True)
                acc_refs[g][:, qr] = alpha * acc_refs[g][:, qr] + lax.dot_general(
                    v_ref[ks, vs], pt.astype(BF16), (((0,), (0,)), ((), ())), preferred_element_type=F32)
                m_refs[g][:, qr] = m_new

        whole, half = slice(0, tq), tq // 2

        @pl.when(ki < qi)
        def _():
            update(whole, whole, False)

        @pl.when(ki == qi)
        def _():
            update(slice(0, half), whole, True)
            update(slice(half, tq), slice(half, tq), True)
            for g in range(G):
                vs = slice(g * DV, (g + 1) * DV)
                o_ref[:, vs] = jnp.transpose(acc_refs[g][...] / l_refs[g][...]).astype(BF16)
                lse_ref[g] = m_refs[g][...] + jnp.log(l_refs[g][...])

    return pl.pallas_call(
        body, name="attn_fwd",
        grid_spec=pltpu.PrefetchScalarGridSpec(
            num_scalar_prefetch=2, grid=(H // G, int(qi_tab.shape[0])),
            in_specs=[pl.BlockSpec((tq, G * DQ), lambda h, p, qt, kt: (qt[p], h)),
                      pl.BlockSpec((tq, G * DQ), lambda h, p, qt, kt: (kt[p], h)),
                      pl.BlockSpec((tq, G * DV), lambda h, p, qt, kt: (kt[p], h))],
            out_specs=(pl.BlockSpec((tq, G * DV), lambda h, p, qt, kt: (qt[p], h)),
                       pl.BlockSpec((G, 1, tq), lambda h, p, qt, kt: (h, 0, qt[p]))),
            scratch_shapes=([pltpu.VMEM((1, tq), F32)] * (2 * G) + [pltpu.VMEM((DV, tq), F32)] * G)),
        out_shape=(jax.ShapeDtypeStruct((T, H * DV), BF16), jax.ShapeDtypeStruct((H, 1, T), F32)),
        compiler_params=_cparams("parallel", "arbitrary"))(qi_tab, ki_tab, q, k, v)


def _attn_bwd(q, k, v, o, do, lse):
    T = q.shape[0]
    H, DQ, DV = MLA_HEADS, MLA_QK_PAD, MLA_V
    tq = _attn_tile(T)
    nq = T // tq
    tn = (((0,), (0,)), ((), ()))
    nt = (((1,), (1,)), ((), ()))
    G = MLA_HEADS_PER_STEP
    qi_tab, ki_tab = _attn_pairs(nq, by_key=True)

    def body(qi_ref, ki_ref, q_ref, k_ref, v_ref, o_ref, do_ref, lse_ref, dq_ref, dk_ref, dv_ref,
             dk_acc, dv_acc):
        p = pl.program_id(1)
        qi, ki = qi_ref[p], ki_ref[p]

        @pl.when(p == 0)
        def _():
            dq_ref[...] = jnp.zeros_like(dq_ref)

        @pl.when(qi == ki)
        def _():
            dk_acc[...] = jnp.zeros_like(dk_acc)
            dv_acc[...] = jnp.zeros_like(dv_acc)

        def step(ks, qr, masked):
            rows = pl.ds(pl.multiple_of(qi * tq + qr.start, qr.stop - qr.start), qr.stop - qr.start)
            for g in range(G):
                qs, vs = slice(g * DQ, (g + 1) * DQ), slice(g * DV, (g + 1) * DV)
                dof = do_ref[qr, vs]
                delta = jnp.sum(jnp.transpose(dof.astype(F32) * o_ref[qr, vs].astype(F32)), axis=0,
                                keepdims=True)
                st = _scores(k_ref[ks, qs], q_ref[qr, qs], False)
                if masked:
                    key = ks.start + lax.broadcasted_iota(jnp.int32, st.shape, 0)
                    qry = qr.start + lax.broadcasted_iota(jnp.int32, st.shape, 1)
                    st = jnp.where(qry >= key, st, -jnp.inf)
                pt = jnp.exp(st - lse_ref[g][:, qr])
                dpt = lax.dot_general(v_ref[ks, vs], dof, nt, preferred_element_type=F32)
                dst = (pt * (dpt - delta)).astype(BF16)
                dv_acc[ks, vs] += jnp.dot(pt.astype(BF16), dof, preferred_element_type=F32)
                dk_acc[ks, qs] += jnp.dot(dst, q_ref[qr, qs], preferred_element_type=F32)
                dq_ref[rows, qs] += lax.dot_general(dst, k_ref[ks, qs], tn, preferred_element_type=F32)

        whole, half = slice(0, tq), tq // 2

        @pl.when(qi == ki)
        def _():
            step(slice(0, half), whole, True)
            step(slice(half, tq), slice(half, tq), True)

        @pl.when(qi > ki)
        def _():
            step(whole, whole, False)

        @pl.when(qi == nq - 1)
        def _():
            dk_ref[...] = dk_acc[...]
            dv_ref[...] = dv_acc[...]

    qspec = pl.BlockSpec((tq, G * DQ), lambda h, p, qt, kt: (qt[p], h))
    ospec = pl.BlockSpec((tq, G * DV), lambda h, p, qt, kt: (qt[p], h))
    kspec = pl.BlockSpec((tq, G * DQ), lambda h, p, qt, kt: (kt[p], h))
    vspec = pl.BlockSpec((tq, G * DV), lambda h, p, qt, kt: (kt[p], h))
    return pl.pallas_call(
        body, name="attn_bwd",
        grid_spec=pltpu.PrefetchScalarGridSpec(
            num_scalar_prefetch=2, grid=(H // G, int(qi_tab.shape[0])),
            in_specs=[qspec, kspec, vspec, ospec, ospec,
                      pl.BlockSpec((G, 1, tq), lambda h, p, qt, kt: (h, 0, qt[p]))],
            out_specs=(pl.BlockSpec((T, G * DQ), lambda h, p, qt, kt: (0, h)), kspec, vspec),
            scratch_shapes=[pltpu.VMEM((tq, G * DQ), F32), pltpu.VMEM((tq, G * DV), F32)]),
        out_shape=(jax.ShapeDtypeStruct((T, H * DQ), F32), jax.ShapeDtypeStruct((T, H * DQ), F32),
                   jax.ShapeDtypeStruct((T, H * DV), F32)),
        compiler_params=_cparams("parallel", "arbitrary"))(qi_tab, ki_tab, q, k, v, o, do, lse)


def _mla_fwd(a, w, cc, ss, wbuf, pk, slot, res):
    D = a.shape[1]
    by_k, _, _ = _row_sharded(pk, "mla_w_o", slot, D // N_CHIPS)
    proj = _mm(a, w["w_in"], name="mla_in")
    cqn, ckvn, q, k, v = _mla_mid_fwd(proj, w["q_norm"], w["kv_norm"], w["w_uq"], w["w_ukv"], cc, ss)
    o, lse = _attn_fwd(q, k, v)
    m, h_new, a_next = _mm(o, wbuf, n=D, b_map=by_k, tm=1024, tk=D // N_CHIPS, tn=D, epi="resnorm", extra=res,
                           name="mla_out_res")
    return m, (a, proj, cqn, ckvn, q, k, v, o, lse), h_new, a_next


def _mla_bwd(dm, saved, w, cc, ss, wbuf, gbuf, pk, slot):
    a, proj, cqn, ckvn, q, k, v, o, lse = saved
    D = a.shape[1]
    _, by_n, by_m = _row_sharded(pk, "mla_w_o", slot, D // N_CHIPS)
    do = _mm(dm, wbuf, tb=True, n=o.shape[1], b_map=by_n, tm=2048, tn=D // N_CHIPS, tk=D, out_dtype=BF16,
             name="mla_out_dx")
    gbuf = _mm(o, dm, ta=True, into=gbuf, o_map=by_m, tm=D // N_CHIPS, tn=D, tk=2048, name="mla_out_dw")
    dq, dk, dv = _attn_bwd(q, k, v, o, do, lse)
    dqp, dkv, dproj, dqn, dkn = _mla_mid_bwd(proj, w["q_norm"], w["kv_norm"], w["w_uq"], w["w_ukv"],
                                             cc, ss, dq, dk, dv)
    dw_uq = _mm(cqn, dqp, ta=True, out_dtype=BF16, name="mla_uq_dw")
    dw_ukv = _mm(ckvn, dkv, ta=True, out_dtype=BF16, name="mla_ukv_dw")
    dw_in = _mm(a, dproj, ta=True, out_dtype=BF16, name="mla_in_dw")
    da = _mm(dproj, w["w_in"], tb=True, name="mla_in_dx")
    return da, gbuf, dict(w_in=dw_in, w_uq=dw_uq, w_ukv=dw_ukv, q_norm=dqn, kv_norm=dkn)


def _split_dot(mat, x, parts):
    acc = None
    rem = x
    for _ in range(parts):
        piece = rem.astype(BF16)
        term = jnp.dot(mat, piece, preferred_element_type=F32)
        acc = term if acc is None else acc + term
        rem = rem - piece.astype(F32)
    return acc


def _chunk_sums(cum, rel, rest, logf):
    return tuple(_split_dot(m.astype(BF16), logf, 3) for m in (cum, rel, rest))


def _chunk_mats(tb):
    C = HGRN_CHUNK
    assert C & (C - 1) == 0
    r = lax.broadcasted_iota(jnp.int32, (tb, tb), 0)
    s = lax.broadcasted_iota(jnp.int32, (tb, tb), 1)
    start = r & ~(C - 1)
    same = start == (s & ~(C - 1))
    ref = start + C // 2
    last = start + C - 1
    one, zero = jnp.float32(1.0), jnp.float32(0.0)
    cum = jnp.where(same & (s <= r), one, zero)
    rel = cum - jnp.where(same & (s <= ref), one, zero)
    rest = jnp.where(same & (s > r) & (s <= last), one, zero)
    rev = jnp.where(same & (s >= r), one, zero)
    ones = jnp.where(same, one, zero)
    causal = same & (s <= r)
    return cum, rel, rest, rev, ones, causal


def _hgrn_gates(p_ref, lb, HK):
    qx = p_ref[:, 0:HK]
    fx = p_ref[:, HK:2 * HK]
    sf = _sigmoid(fx)
    f = lb + (1.0 - lb) * sf
    sq = _sigmoid(qx)
    return qx, sq, qx * sq, sf, f, 1.0 - f, jnp.log(f)


def _hgrn_fwd(proj, lb, o_norm):
    T = proj.shape[0]
    H, C = HGRN_HEADS, HGRN_CHUNK
    HK = proj.shape[1] // 4
    DK = HK // H
    tb = min(HGRN_BLOCK, T)
    ncb = tb // C
    nt = (((1,), (1,)), ((), ()))
    tn = (((0,), (0,)), ((), ()))

    def body(p_ref, lb_ref, on_ref, y_ref, o_ref, st_ref, state, oacc):
        @pl.when(pl.program_id(0) == 0)
        def _():
            state[...] = jnp.zeros_like(state)

        cum, rel, rest, _, _, causal = _chunk_mats(tb)
        _, _, q, _, f, k, logf = _hgrn_gates(p_ref, lb_ref[...], HK)
        b, brel, brest = _chunk_sums(cum, rel, rest, logf)
        eb = jnp.exp(b)
        q_rel = (q * jnp.exp(brel)).astype(BF16)
        k_rel = (k * jnp.exp(-brel)).astype(BF16)
        q_dec = (q * eb).astype(BF16)
        k_dec = (k * jnp.exp(brest)).astype(BF16)
        v = p_ref[:, 2 * HK:3 * HK].astype(BF16)
        for h in range(H):
            hs = slice(h * DK, (h + 1) * DK)
            a = lax.dot_general(q_rel[:, hs], k_rel[:, hs], nt, preferred_element_type=F32)
            a = jnp.where(causal, a, 0.0).astype(BF16)
            oacc[:, hs] = jnp.dot(a, v[:, hs], preferred_element_type=F32)
            for j in range(ncb):
                rs = slice(j * C, (j + 1) * C)
                st = state[h]
                st_ref[j, h] = st
                oacc[rs, hs] += lax.dot_general(q_dec[rs, hs], st.astype(BF16), nt,
                                                preferred_element_type=F32)
                dec = jnp.exp(jnp.sum(logf[rs, hs], axis=0, keepdims=True))
                state[h] = dec * st + lax.dot_general(v[rs, hs], k_dec[rs, hs], tn,
                                                      preferred_element_type=F32)
        o = oacc[...]
        o_ref[...] = o
        gx = p_ref[:, 3 * HK:4 * HK]
        gate = gx * _sigmoid(gx)
        for h in range(H):
            hs = slice(h * DK, (h + 1) * DK)
            oh = o[:, hs]
            y_ref[:, hs] = (oh * _rms_rstd(oh) * on_ref[...] * gate[:, hs]).astype(BF16)

    return pl.pallas_call(
        body, name="hgrn_fwd", grid=(T // tb,),
        in_specs=[pl.BlockSpec((tb, 4 * HK), lambda i: (i, 0)),
                  pl.BlockSpec((1, HK), lambda i: (0, 0)),
                  pl.BlockSpec((1, DK), lambda i: (0, 0))],
        out_specs=(pl.BlockSpec((tb, HK), lambda i: (i, 0)),
                   pl.BlockSpec((tb, HK), lambda i: (i, 0)),
                   pl.BlockSpec((ncb, H, DK, DK), lambda i: (i, 0, 0, 0))),
        out_shape=(jax.ShapeDtypeStruct((T, HK), BF16), jax.ShapeDtypeStruct((T, HK), F32),
                   jax.ShapeDtypeStruct((T // C, H, DK, DK), F32)),
        scratch_shapes=[pltpu.VMEM((H, DK, DK), F32), pltpu.VMEM((tb, HK), F32)],
        compiler_params=_cparams("arbitrary"))(proj, lb, o_norm)


def _hgrn_bwd(proj, lb, o_norm, o, states, dy):
    T = proj.shape[0]
    H, C = HGRN_HEADS, HGRN_CHUNK
    HK = proj.shape[1] // 4
    DK = HK // H
    tb = min(HGRN_BLOCK, T)
    ncb = tb // C
    nb = T // tb
    nt = (((1,), (1,)), ((), ()))
    tn = (((0,), (0,)), ((), ()))

    def body(p_ref, lb_ref, on_ref, o_ref, st_ref, dy_ref, dp_ref, dlb_ref, don_ref,
             dstate, dqr_s, dkr_s, dqd_s, dkd_s, dv_s, do_s, e_s):
        @pl.when(pl.program_id(0) == 0)
        def _():
            dstate[...] = jnp.zeros_like(dstate)
            dlb_ref[...] = jnp.zeros_like(dlb_ref)
            don_ref[...] = jnp.zeros_like(don_ref)

        cum, rel, rest, rev, ones, causal = _chunk_mats(tb)
        lb = lb_ref[...]
        qx, sq, q, sf, f, k, logf = _hgrn_gates(p_ref, lb, HK)
        b, brel, brest = _chunk_sums(cum, rel, rest, logf)
        eb = jnp.exp(b)
        erel = jnp.exp(brel)
        enrel = jnp.exp(-brel)
        erest = jnp.exp(brest)
        q_rel_f, k_rel_f, q_dec_f, k_dec_f = q * erel, k * enrel, q * eb, k * erest
        q_rel, k_rel = q_rel_f.astype(BF16), k_rel_f.astype(BF16)
        q_dec, k_dec = q_dec_f.astype(BF16), k_dec_f.astype(BF16)
        v = p_ref[:, 2 * HK:3 * HK].astype(BF16)

        gx = p_ref[:, 3 * HK:4 * HK]
        sg = _sigmoid(gx)
        gate = gx * sg
        dy = dy_ref[...]
        ov = o_ref[...]
        on = on_ref[...]
        don = jnp.zeros((1, DK), F32)
        for h in range(H):
            hs = slice(h * DK, (h + 1) * DK)
            oh = ov[:, hs]
            r = _rms_rstd(oh)
            xh = oh * r
            d_on = dy[:, hs] * gate[:, hs]
            don = don + jnp.sum(d_on * xh, axis=0, keepdims=True)
            u = d_on * on
            do_s[:, hs] = r * (u - xh * jnp.mean(u * xh, axis=-1, keepdims=True))
            dp_ref[:, 3 * HK + h * DK:3 * HK + (h + 1) * DK] = (
                dy[:, hs] * xh * on * (sg[:, hs] * (1.0 + gx[:, hs] * (1.0 - sg[:, hs])))).astype(BF16)
        don_ref[...] += don

        for h in range(H):
            hs = slice(h * DK, (h + 1) * DK)
            doh = do_s[:, hs].astype(BF16)
            a = lax.dot_general(q_rel[:, hs], k_rel[:, hs], nt, preferred_element_type=F32)
            a = jnp.where(causal, a, 0.0).astype(BF16)
            da = lax.dot_general(doh, v[:, hs], nt, preferred_element_type=F32)
            da = jnp.where(causal, da, 0.0).astype(BF16)
            dv_s[:, hs] = lax.dot_general(a, doh, tn, preferred_element_type=F32)
            dqr_s[:, hs] = jnp.dot(da, k_rel[:, hs], preferred_element_type=F32)
            dkr_s[:, hs] = lax.dot_general(da, q_rel[:, hs], tn, preferred_element_type=F32)
            for j in reversed(range(ncb)):
                rs = slice(j * C, (j + 1) * C)
                dst = dstate[h]
                dstb = dst.astype(BF16)
                st = st_ref[j, h]
                dkd_s[rs, hs] = jnp.dot(v[rs, hs], dstb, preferred_element_type=F32)
                dv_s[rs, hs] += lax.dot_general(k_dec[rs, hs], dstb, nt, preferred_element_type=F32)
                dec = jnp.exp(jnp.sum(logf[rs, hs], axis=0, keepdims=True))
                e_s[rs, hs] = jnp.broadcast_to(jnp.sum(dst * st, axis=0, keepdims=True) * dec, (C, DK))
                dqd_s[rs, hs] = jnp.dot(doh[rs], st.astype(BF16), preferred_element_type=F32)
                dstate[h] = dec * dst + lax.dot_general(doh[rs], q_dec[rs, hs], tn,
                                                        preferred_element_type=F32)

        dqr, dkr, dqd, dkd = dqr_s[...], dkr_s[...], dqd_s[...], dkd_s[...]
        kdk = dkd * k_dec_f
        db = dqr * q_rel_f - dkr * k_rel_f + dqd * q_dec_f - kdk
        dlogf = _split_dot(rev.astype(BF16), db, 2) + _split_dot(ones.astype(BF16), kdk, 2) + e_s[...]
        dk = dkr * enrel + dkd * erest
        df = dlogf / f - dk
        dlb_ref[...] += jnp.sum(df * (1.0 - sf), axis=0, keepdims=True)
        dq = dqr * erel + dqd * eb
        dp_ref[:, 0:HK] = (dq * (sq * (1.0 + qx * (1.0 - sq)))).astype(BF16)
        dp_ref[:, HK:2 * HK] = (df * (1.0 - lb) * sf * (1.0 - sf)).astype(BF16)
        dp_ref[:, 2 * HK:3 * HK] = dv_s[...].astype(BF16)

    rev_row = lambda w: pl.BlockSpec((tb, w), lambda i: (nb - 1 - i, 0))
    vec = lambda w: pl.BlockSpec((1, w), lambda i: (0, 0))
    scr = pltpu.VMEM((tb, HK), F32)
    return pl.pallas_call(
        body, name="hgrn_bwd", grid=(nb,),
        in_specs=[rev_row(4 * HK), vec(HK), vec(DK), rev_row(HK),
                  pl.BlockSpec((ncb, H, DK, DK), lambda i: (nb - 1 - i, 0, 0, 0)), rev_row(HK)],
        out_specs=(rev_row(4 * HK), vec(HK), vec(DK)),
        out_shape=(jax.ShapeDtypeStruct((T, 4 * HK), BF16), jax.ShapeDtypeStruct((1, HK), F32),
                   jax.ShapeDtypeStruct((1, DK), F32)),
        scratch_shapes=[pltpu.VMEM((H, DK, DK), F32), scr, scr, scr, scr, scr, scr, scr],
        compiler_params=_cparams("arbitrary"))(proj, lb, o_norm, o, states, dy)


def _hgrn_layer_fwd(a, o_norm, lb, wbuf, pk, slot, res):
    D = a.shape[1]
    in_by_n, _ = _col_sharded(pk, "hgrn_w_in", slot, D)
    out_by_k, _, _ = _row_sharded(pk, "hgrn_w_o", slot, D // N_CHIPS)
    proj = _mm(a, wbuf, n=4 * D, b_map=in_by_n, tk=D, tn=D, name="hgrn_in")
    y, o, states = _hgrn_fwd(proj, lb, o_norm)
    m, h_new, a_next = _mm(y, wbuf, n=D, b_map=out_by_k, tm=1024, tk=D // N_CHIPS, tn=D, epi="resnorm",
                           extra=res, name="hgrn_out_res")
    return m, (a, proj, y, o, states), h_new, a_next


def _hgrn_layer_bwd(dm, saved, o_norm, lb, wbuf, gbuf, pk, slot):
    a, proj, y, o, states = saved
    D = a.shape[1]
    in_by_n, in_by_k = _col_sharded(pk, "hgrn_w_in", slot, D)
    _, out_by_n, out_by_m = _row_sharded(pk, "hgrn_w_o", slot, D // N_CHIPS)
    dy = _mm(dm, wbuf, tb=True, n=y.shape[1], b_map=out_by_n, tm=2048, tn=D // N_CHIPS, tk=D,
             name="hgrn_out_dx")
    gbuf = _mm(y, dm, ta=True, into=gbuf, o_map=out_by_m, tm=D // N_CHIPS, tn=D, tk=2048, name="hgrn_out_dw")
    dproj, dlb, don = _hgrn_bwd(proj, lb, o_norm, o, states, dy)
    gbuf = _mm(a, dproj, ta=True, into=gbuf, o_map=in_by_n, tm=D, tn=D, name="hgrn_in_dw")
    da = _mm(dproj, wbuf, tb=True, n=D, b_map=in_by_k, tn=D, tk=D, name="hgrn_in_dx")
    return da, gbuf, dict(o_norm=don, lb=dlb)


def _lower_bounds(lb_logits):
    p = jax.nn.softmax(lb_logits.astype(F32), axis=0)
    return jnp.cumsum(p, axis=0) - p[0]


def _rope_tables(positions):
    inv_freq = jnp.power(ROPE_BASE, -jnp.arange(0, MLA_ROPE, 2, dtype=F32) / MLA_ROPE)
    ang = positions.astype(F32)[:, None] * inv_freq
    cos, sin = jnp.cos(ang), jnp.sin(ang)
    zero = jnp.zeros((positions.shape[0], 128 - MLA_ROPE), F32)
    return (jnp.concatenate([cos, cos, zero], axis=-1), jnp.concatenate([-sin, sin, zero], axis=-1))


def _pad_mla_weights(w_in, w_uq):
    w_in_p = jnp.pad(w_in, ((0, 0), (0, 0), (0, 128 - MLA_ROPE)))
    n, ql, _ = w_uq.shape
    w_uq_p = jnp.pad(w_uq.reshape(n, ql, MLA_HEADS, MLA_NOPE + MLA_ROPE),
                     ((0, 0), (0, 0), (0, 0), (0, MLA_QK_PAD - MLA_NOPE - MLA_ROPE)))
    return w_in_p, w_uq_p.reshape(n, ql, MLA_HEADS * MLA_QK_PAD)


def _local_step(x, positions, target, small, prefetch, fetch, gbufs, emit, emit_mlp):
    T, D = x.shape
    lbounds, lb_vjp = jax.vjp(_lower_bounds, small["hgrn_lb_logits"])
    cc, ss = _rope_tables(positions)
    fetched = {0: fetch(0, None)}
    gains = fetched[0]["gains"]
    tick = [jnp.zeros((), F32)]

    def g(layer, i):
        return gains[layer, i][None, :] + tick[0]

    def mla_weights(layer):
        f = fetched[layer]
        w_in_p, w_uq_p = _pad_mla_weights(f["w_in"][None], f["w_uq"][None])
        slot = layer // 2
        return dict(w_in=w_in_p[0], w_uq=w_uq_p[0], w_ukv=f["w_ukv"],
                    q_norm=small["mla_q_norm"][slot][None, :], kv_norm=small["mla_kv_norm"][slot][None, :])

    saved = []
    h = x
    a = _prenorm_fwd(x, g(0, 0))
    dy = sq = None
    for layer in range(DEPTH):
        slot = layer // 2
        if layer not in fetched:
            fetched[layer] = fetch(layer, a)
        wbuf, pk = fetched[layer]["wbuf"], fetched[layer]["pk"]
        res = (h, g(layer, 1), g(layer, 2))
        if layer % 2 == 0:
            m, mix_saved, h1, a2 = _mla_fwd(a, mla_weights(layer), cc, ss, wbuf, pk, slot, res)
        else:
            m, mix_saved, h1, a2 = _hgrn_layer_fwd(a, small["hgrn_o_norm"][slot][None, :],
                                                   lbounds[layer][None, :], wbuf, pk, slot, res)
        if layer + 1 < DEPTH:
            tick[0] = prefetch(layer + 1, a2)
            u, mlp_saved, h2, a = _mlp_fwd(a2, wbuf, pk, layer, (h1, g(layer, 3), g(layer + 1, 0)))
        else:
            u, mlp_saved, h2, _ = _mlp_fwd(a2, wbuf, pk, layer, None)
            dy, sq = _resnorm_loss(h1, u, g(layer, 3), target)
        saved.append((h, m, h1, u, mix_saved, mlp_saved))
        h = h2

    n_mla, n_hgrn = (DEPTH + 1) // 2, DEPTH // 2
    dgains = [[None] * 4 for _ in range(DEPTH)]
    gw = {k: [None] * n_mla for k in ("mla_w_in", "mla_w_uq", "mla_w_ukv", "mla_q_norm", "mla_kv_norm")}
    gw["hgrn_o_norm"] = [None] * n_hgrn
    dlb = [jnp.zeros((1, lbounds.shape[1]), F32) for _ in range(DEPTH)]
    dh = dy
    da_next = None
    for layer in reversed(range(DEPTH)):
        h0, m, h1, u, mix_saved, mlp_saved = saved[layer]
        slot = layer // 2
        wbuf, pk, gbuf = fetched[layer]["wbuf"], fetched[layer]["pk"], gbufs[layer]
        if da_next is None:
            du, dgains[layer][3] = _resnorm_bwd(u, g(layer, 3), dh, name="resnorm_bwd_last")
            t = dh
        else:
            h2 = saved[layer + 1][0]
            t, du, dgains[layer][3], dgains[layer + 1][0] = _resnorm_bwd(
                u, g(layer, 3), dh, h2, da_next, g(layer + 1, 0), name="resnorm_bwd_mlp")
        da2, gbuf = _mlp_bwd(du, mlp_saved, wbuf, gbuf, pk, layer)
        if layer == 0:
            gbuf = emit_mlp(layer, gbuf)
        t, dm, dgains[layer][1], dgains[layer][2] = _resnorm_bwd(
            m, g(layer, 1), t, h1, da2, g(layer, 2), name="resnorm_bwd_mix")
        if layer % 2 == 0:
            da_next, gbuf, mg = _mla_bwd(dm, mix_saved, mla_weights(layer), cc, ss, wbuf, gbuf, pk, slot)
            ql = mg["q_norm"].shape[-1]
            kvl = mg["kv_norm"].shape[-1]
            gw["mla_w_in"][slot] = mg["w_in"][:, :ql + kvl + MLA_ROPE]
            gw["mla_w_uq"][slot] = mg["w_uq"].reshape(ql, MLA_HEADS, MLA_QK_PAD)[
                :, :, :MLA_NOPE + MLA_ROPE].reshape(ql, MLA_HEADS * (MLA_NOPE + MLA_ROPE))
            gw["mla_w_ukv"][slot] = mg["w_ukv"]
            gw["mla_q_norm"][slot] = mg["q_norm"][0]
            gw["mla_kv_norm"][slot] = mg["kv_norm"][0]
        else:
            da_next, gbuf, hg = _hgrn_layer_bwd(dm, mix_saved, small["hgrn_o_norm"][slot][None, :],
                                                lbounds[layer][None, :], wbuf, gbuf, pk, slot)
            gw["hgrn_o_norm"][slot] = hg["o_norm"][0]
            dlb[layer] = hg["lb"]
        dh = t
        if layer > 0:
            mine = ({k: gw[k][slot] for k in ("mla_w_in", "mla_w_uq", "mla_w_ukv")} if layer % 2 == 0 else {})
            tick[0] = emit(layer, gbuf, mine)
        else:
            gbuf0 = gbuf
    grad_x, dgains[0][0] = _prenorm_bwd(x, g(0, 0), dh, da_next)

    last = {k: gw[k][0] for k in ("mla_w_in", "mla_w_uq", "mla_w_ukv")}
    last.update({k: jnp.stack(gw[k]) for k in ("mla_q_norm", "mla_kv_norm", "hgrn_o_norm")})
    last["norm_gains"] = jnp.stack([jnp.concatenate(row, axis=0) for row in dgains])
    (last["hgrn_lb_logits"],) = lb_vjp(jnp.concatenate(dlb, axis=0))
    emit(0, gbuf0, last)
    return sq, grad_x


def _size(shape):
    n = 1
    for d in shape:
        n *= d
    return n


def _piece_rows(shape):
    return -(-_size(shape) // PACK_W)


def _packed_misc_rows(shapes):
    return sum(_piece_rows(s) for s in shapes)


def _cast_into(src, buf, row, name):
    rows, W = src.shape
    tr = min(256, rows)
    assert rows % tr == 0 and row % tr == 0

    def body(s_ref, b_ref, o_ref):
        o_ref[...] = s_ref[...].astype(BF16)

    return pl.pallas_call(
        body, name=name, grid=(rows // tr,),
        in_specs=[pl.BlockSpec((tr, W), lambda i: (i, 0)), pl.BlockSpec(memory_space=pl.ANY)],
        out_specs=pl.BlockSpec((tr, W), lambda i: (row // tr + i, 0)),
        out_shape=jax.ShapeDtypeStruct(buf.shape, buf.dtype), input_output_aliases={1: 0},
        compiler_params=_cparams("parallel"))(src, buf)


def _pack_blocks(pieces, rows, dtype):
    blocks, used = [], 0
    for p in pieces:
        flat = p.astype(dtype).reshape(-1)
        r = _piece_rows(p.shape)
        if r * PACK_W != flat.shape[0]:
            flat = jnp.pad(flat, (0, r * PACK_W - flat.shape[0]))
        blocks.append(flat.reshape(r, PACK_W))
        used += r
    if rows > used:
        blocks.append(jnp.zeros((rows - used, PACK_W), dtype))
    return blocks


def _unpack(buf, shapes):
    out, off = [], 0
    for shp in shapes:
        r = _piece_rows(shp)
        piece = buf[off:off + r]
        if r * PACK_W != _size(shp):
            piece = piece.reshape(-1)[:_size(shp)]
        out.append(piece.reshape(shp))
        off += r
    return out


def _mesh_place():
    x, y, c = lax.axis_index("x"), lax.axis_index("y"), lax.axis_index("c")
    chips = [(1 - x, y), (x, 1 - y), (1 - x, 1 - y)]
    return x, y, c, chips


_HBM = pl.BlockSpec(memory_space=pltpu.HBM)


def _share_reduced(q, name="grads_share_reduced"):
    rh, W = q.shape

    def body(q_ref, out_ref, send_sem, recv_sem):
        x, y, c, _ = _mesh_place()
        cp = pltpu.make_async_remote_copy(src_ref=q_ref, dst_ref=out_ref.at[c], send_sem=send_sem,
                                          recv_sem=recv_sem, device_id=(x, y, 1 - c), device_id_type=MESH)
        cp.start()
        cp.wait()

    out = pl.pallas_call(
        body, name=name, in_specs=[_HBM], out_specs=_HBM,
        out_shape=jax.ShapeDtypeStruct((2, rh, W), q.dtype),
        scratch_shapes=[pltpu.SemaphoreType.DMA, pltpu.SemaphoreType.DMA],
    )(q)
    return out


def _sum_chips(parts, own, own_row0, which, name, out_dtype=F32):
    n, rh, W = parts.shape
    tr = PACK_TILE
    assert own_row0 % tr == 0
    if own.ndim == 3:
        own_spec = pl.BlockSpec((None, tr, W), lambda i, w_ref: (w_ref[0], own_row0 // tr + i, 0))
    else:
        own_spec = pl.BlockSpec((tr, W), lambda i, w_ref: (own_row0 // tr + i, 0))

    def body(w_ref, p_ref, own_ref, o_ref):
        mine = own_ref[...].astype(F32)
        acc = None
        for j in range(n):
            term = jnp.where(w_ref[0] == j, mine, p_ref[j].astype(F32))
            acc = term if acc is None else acc + term
        o_ref[...] = acc.astype(out_dtype)

    return pl.pallas_call(
        body, name=name,
        grid_spec=pltpu.PrefetchScalarGridSpec(
            num_scalar_prefetch=1, grid=(rh // tr,),
            in_specs=[pl.BlockSpec((n, tr, W), lambda i, w_ref: (0, i, 0)), own_spec],
            out_specs=pl.BlockSpec((tr, W), lambda i, w_ref: (i, 0))),
        out_shape=jax.ShapeDtypeStruct((rh, W), out_dtype),
        compiler_params=_cparams("parallel"))(jnp.reshape(which, (1,)).astype(jnp.int32), parts, own)


_SEM = pl.BlockSpec(memory_space=pltpu.SEMAPHORE)
_ASYNC = pltpu.CompilerParams(has_side_effects=pltpu.SideEffectType.DATAFLOW_SIDE_EFFECTING)


def _hbm(a):
    return pltpu.with_memory_space_constraint(a, pltpu.HBM)


def _gather_copies(w_ref, land_ref, send_sems, recv_sems):
    x, y, c, chips = _mesh_place()
    me = 2 * x + y
    rh = w_ref.shape[0] // 2
    rows = pl.ds(pl.multiple_of(c * rh, 16), rh)
    return [pltpu.make_async_remote_copy(
        src_ref=w_ref.at[rows], dst_ref=land_ref.at[me, rows], send_sem=send_sems.at[r],
        recv_sem=recv_sems.at[r], device_id=(px, py, c), device_id_type=MESH)
        for r, (px, py) in enumerate(chips)]


def _scatter_copies(g_ref, land_ref, send_sems, recv_sems, row0):
    x, y, c, chips = _mesh_place()
    me = 2 * x + y
    rows = pl.ds(row0, land_ref.shape[1])
    return [pltpu.make_async_remote_copy(
        src_ref=g_ref.at[2 * px + py, rows], dst_ref=land_ref.at[me], send_sem=send_sems.at[r],
        recv_sem=recv_sems.at[r], device_id=(px, py, c), device_id_type=MESH)
        for r, (px, py) in enumerate(chips)]


def _halves_copies(land_ref, send_sems, recv_sems):
    x, y, c, chips = _mesh_place()
    rh = land_ref.shape[1] // 2
    rows = pl.ds(pl.multiple_of(c * rh, 16), rh)
    return [pltpu.make_async_remote_copy(
        src_ref=land_ref.at[2 * px + py, rows], dst_ref=land_ref.at[2 * px + py, rows], send_sem=send_sems.at[r],
        recv_sem=recv_sems.at[r], device_id=(x, y, 1 - c), device_id_type=MESH)
        for r, (px, py) in enumerate(chips)]


def _halves_to_sibling(land, name):
    def body(l_ref, o_ref, send_sems, recv_sems):
        copies = _halves_copies(o_ref, send_sems, recv_sems)
        for cp in copies:
            cp.start()
        for cp in copies:
            cp.wait()

    return pl.pallas_call(
        body, name=name, in_specs=[_HBM], out_specs=_HBM, out_shape=jax.ShapeDtypeStruct(land.shape, land.dtype),
        scratch_shapes=[pltpu.SemaphoreType.DMA((3,)), pltpu.SemaphoreType.DMA((3,))],
        input_output_aliases={0: 0})(land)


def _halves_start(land, name):
    def body(l_ref, send_sems, recv_sems, land_thru, token):
        for cp in _halves_copies(l_ref, send_sems, recv_sems):
            cp.start()
        token[...] = jnp.zeros_like(token)

    return pl.pallas_call(
        body, name=name,
        out_shape=(pltpu.SemaphoreType.DMA((3,)), pltpu.SemaphoreType.DMA((3,)), pltpu.HBM(land.shape, land.dtype),
                   jax.ShapeDtypeStruct((8, 128), F32)),
        in_specs=(_HBM,), out_specs=(_SEM, _SEM, _HBM, pl.BlockSpec(memory_space=pltpu.VMEM)),
        input_output_aliases={0: 2}, compiler_params=_ASYNC,
    )(_hbm(land))


def _halves_wait(send_sems, recv_sems, land_thru, after, name):
    rh = land_thru.shape[1] // 2

    def body(land_ref, send_sems, recv_sems, after_ref, got_ref):
        x, y, c, _ = _mesh_place()
        half = land_ref.at[0, pl.ds(0, rh)]
        for k in range(3):
            cp = pltpu.make_async_remote_copy(src_ref=half, dst_ref=half, send_sem=send_sems.at[k],
                                              recv_sem=recv_sems.at[k], device_id=(x, y, 1 - c),
                                              device_id_type=MESH)
            cp.wait_send()
            cp.wait_recv()

    return pl.pallas_call(
        body, name=name, out_shape=pltpu.HBM(land_thru.shape, land_thru.dtype),
        in_specs=(_HBM, _SEM, _SEM, pl.BlockSpec(memory_space=pl.ANY)), out_specs=_HBM,
        input_output_aliases={0: 0}, compiler_params=_ASYNC,
    )(land_thru, send_sems, recv_sems, after)


def _gather_start(wp, name):
    R, W = wp.shape

    def body(w_ref, land_ref, send_sems, recv_sems, w_thru, land_thru, token):
        for cp in _gather_copies(w_ref, land_ref, send_sems, recv_sems):
            cp.start()
        token[...] = jnp.zeros_like(token)

    return pl.pallas_call(
        body, name=name,
        out_shape=(pltpu.SemaphoreType.DMA((3,)), pltpu.SemaphoreType.DMA((3,)), pltpu.HBM(wp.shape, wp.dtype),
                   pltpu.HBM((N_CHIPS, R, W), wp.dtype), jax.ShapeDtypeStruct((8, 128), F32)),
        in_specs=(_HBM, _HBM),
        out_specs=(_SEM, _SEM, _HBM, _HBM, pl.BlockSpec(memory_space=pltpu.VMEM)),
        input_output_aliases={0: 2, 1: 3}, compiler_params=_ASYNC,
    )(_hbm(wp), _hbm(lax.empty((N_CHIPS, R, W), wp.dtype)))


def _gather_wait(send_sems, recv_sems, w_thru, land_thru, after, name):
    R, W = w_thru.shape
    rh = R // 2

    def body(w_ref, land_ref, send_sems, recv_sems, after_ref, w_dead, got_ref):
        x, y, c, _ = _mesh_place()
        half = land_ref.at[0, pl.ds(0, rh)]
        for k in range(3):
            cp = pltpu.make_async_remote_copy(src_ref=half, dst_ref=half, send_sem=send_sems.at[k],
                                              recv_sem=recv_sems.at[k], device_id=(x, y, 1 - c),
                                              device_id_type=MESH)
            cp.wait_send()
            cp.wait_recv()

    return pl.pallas_call(
        body, name=name,
        out_shape=(pltpu.HBM(w_thru.shape, w_thru.dtype), pltpu.HBM(land_thru.shape, land_thru.dtype)),
        in_specs=(_HBM, _HBM, _SEM, _SEM, pl.BlockSpec(memory_space=pl.ANY)), out_specs=(_HBM, _HBM),
        input_output_aliases={0: 0, 1: 1}, compiler_params=_ASYNC,
    )(w_thru, land_thru, send_sems, recv_sems, after)


def _scatter_start(g, row0, nrows, name):
    n, R, W = g.shape
    land_shape = (n, nrows, W)

    def body(g_ref, land_ref, send_sems, recv_sems, g_thru, land_thru, token):
        for cp in _scatter_copies(g_ref, land_ref, send_sems, recv_sems, row0):
            cp.start()
        token[...] = jnp.zeros_like(token)

    return pl.pallas_call(
        body, name=name,
        out_shape=(pltpu.SemaphoreType.DMA((3,)), pltpu.SemaphoreType.DMA((3,)), pltpu.HBM(g.shape, g.dtype),
                   pltpu.HBM(land_shape, g.dtype), jax.ShapeDtypeStruct((8, 128), F32)),
        in_specs=(_HBM, _HBM),
        out_specs=(_SEM, _SEM, _HBM, _HBM, pl.BlockSpec(memory_space=pltpu.VMEM)),
        input_output_aliases={0: 2, 1: 3}, compiler_params=_ASYNC,
    )(_hbm(g), _hbm(lax.empty(land_shape, g.dtype)))


def _scatter_wait(send_sems, recv_sems, g_thru, land_thru, after, name):
    def body(g_ref, land_ref, send_sems, recv_sems, after_ref, g_out, got_ref):
        x, y, c, _ = _mesh_place()
        for k in range(3):
            cp = pltpu.make_async_remote_copy(src_ref=land_ref.at[0], dst_ref=land_ref.at[0], send_sem=send_sems.at[k],
                                              recv_sem=recv_sems.at[k], device_id=(x, y, 1 - c),
                                              device_id_type=MESH)
            cp.wait_send()
            cp.wait_recv()

    return pl.pallas_call(
        body, name=name,
        out_shape=(pltpu.HBM(g_thru.shape, g_thru.dtype), pltpu.HBM(land_thru.shape, land_thru.dtype)),
        in_specs=(_HBM, _HBM, _SEM, _SEM, pl.BlockSpec(memory_space=pl.ANY)), out_specs=(_HBM, _HBM),
        input_output_aliases={0: 0, 1: 1}, compiler_params=_ASYNC,
    )(g_thru, land_thru, send_sems, recv_sems, after)


def _adamw(w, g, m, v, name):
    shape = w.shape
    cols = shape[-1]
    w2, g2, m2, v2 = (t.reshape(-1, cols) for t in (w, g, m, v))
    rows = w2.shape[0]
    tr = rows
    for cand in (512, 256, 128, 64, 32, 16, 8):
        if rows > cand and rows % cand == 0:
            tr = cand
            break
    c1 = 1.0 / (1.0 - ADAM_B1 ** ADAM_STEP)
    c2 = 1.0 / (1.0 - ADAM_B2 ** ADAM_STEP)

    def body(w_ref, g_ref, m_ref, v_ref, d_ref, nm_ref, nv_ref):
        gv = g_ref[...]
        nm = ADAM_B1 * m_ref[...] + (1.0 - ADAM_B1) * gv
        nv = ADAM_B2 * v_ref[...] + (1.0 - ADAM_B2) * (gv * gv)
        nm_ref[...] = nm
        nv_ref[...] = nv
        d_ref[...] = -ADAM_LR * ((nm * c1) / (jnp.sqrt(nv * c2) + ADAM_EPS) + ADAM_WD * w_ref[...])

    blk = pl.BlockSpec((tr, cols), lambda i: (i, 0))
    sds = jax.ShapeDtypeStruct((rows, cols), F32)
    d, nm, nv = pl.pallas_call(body, name=name, grid=(rows // tr,), in_specs=[blk] * 4,
                               out_specs=(blk, blk, blk), out_shape=(sds, sds, sds),
                               compiler_params=_cparams("parallel"))(w2, g2, m2, v2)
    return d.reshape(shape), nm.reshape(shape), nv.reshape(shape)


def kernel(x, positions, norm_gains, mla_w_in, mla_q_norm, mla_kv_norm, mla_w_uq, mla_w_ukv, mla_w_o, hgrn_w_in, hgrn_lb_logits, hgrn_o_norm, hgrn_w_o, mlp_w1, mlp_w2, loss_target, m_norm_gains, m_mla_w_in, m_mla_q_norm, m_mla_kv_norm, m_mla_w_uq, m_mla_w_ukv, m_mla_w_o, m_hgrn_w_in, m_hgrn_lb_logits, m_hgrn_o_norm, m_hgrn_w_o, m_mlp_w1, m_mlp_w2, v_norm_gains, v_mla_w_in, v_mla_q_norm, v_mla_kv_norm, v_mla_w_uq, v_mla_w_ukv, v_mla_w_o, v_hgrn_w_in, v_hgrn_lb_logits, v_hgrn_o_norm, v_hgrn_w_o, v_mlp_w1, v_mlp_w2):
    w = dict(norm_gains=norm_gains, mla_w_in=mla_w_in, mla_q_norm=mla_q_norm, mla_kv_norm=mla_kv_norm,
             mla_w_uq=mla_w_uq, mla_w_ukv=mla_w_ukv, mla_w_o=mla_w_o, hgrn_w_in=hgrn_w_in,
             hgrn_lb_logits=hgrn_lb_logits, hgrn_o_norm=hgrn_o_norm, hgrn_w_o=hgrn_w_o,
             mlp_w1=mlp_w1, mlp_w2=mlp_w2)
    mom_m = dict(norm_gains=m_norm_gains, mla_w_in=m_mla_w_in, mla_q_norm=m_mla_q_norm,
                 mla_kv_norm=m_mla_kv_norm, mla_w_uq=m_mla_w_uq, mla_w_ukv=m_mla_w_ukv,
                 mla_w_o=m_mla_w_o, hgrn_w_in=m_hgrn_w_in, hgrn_lb_logits=m_hgrn_lb_logits,
                 hgrn_o_norm=m_hgrn_o_norm, hgrn_w_o=m_hgrn_w_o, mlp_w1=m_mlp_w1, mlp_w2=m_mlp_w2)
    mom_v = dict(norm_gains=v_norm_gains, mla_w_in=v_mla_w_in, mla_q_norm=v_mla_q_norm,
                 mla_kv_norm=v_mla_kv_norm, mla_w_uq=v_mla_w_uq, mla_w_ukv=v_mla_w_ukv,
                 mla_w_o=v_mla_w_o, hgrn_w_in=v_hgrn_w_in, hgrn_lb_logits=v_hgrn_lb_logits,
                 hgrn_o_norm=v_hgrn_o_norm, hgrn_w_o=v_hgrn_w_o, mlp_w1=v_mlp_w1, mlp_w2=v_mlp_w2)
    c = lax.axis_index("c")

    axis_of = dict(SHARDED)
    me = 2 * lax.axis_index("x") + lax.axis_index("y")
    gain_bits = lax.bitcast_convert_type(norm_gains, jnp.uint32)
    gain_hi = lax.bitcast_convert_type((gain_bits >> 16).astype(jnp.uint16), BF16)
    gain_lo = lax.bitcast_convert_type((gain_bits & 0xFFFF).astype(jnp.uint16), BF16)

    layers = []
    for l in range(DEPTH):
        s = l // 2
        if l % 2 == 0:
            big = [("mlp_w1", l), ("mlp_w2", l), ("mla_w_o", s)]
            tail = [("mla_w_in", s), ("mla_w_uq", s), ("mla_w_ukv", s)]
        else:
            big = [("hgrn_w_in", s), ("mlp_w1", l), ("mlp_w2", l), ("hgrn_w_o", s)]
            tail = []
        w_tail = [w[n][i] for n, i in tail] + ([gain_hi, gain_lo] if l == 0 else [])
        g_tail = tail + ([("norm_gains", None)] + [(n, None) for n in REPLICATED] if l == 0 else [])
        g_shapes = [w[n].shape if i is None else w[n][i].shape for n, i in g_tail]
        tail_rows = max(_packed_misc_rows([t.shape for t in w_tail]), _packed_misc_rows(g_shapes))
        pk = _Packed([(n, w[n].shape[1]) for n, _ in big], tail_rows)
        wpack = jnp.zeros((pk.rows, PACK_W), BF16)
        for n, i in big:
            assert w[n].shape[2] == PACK_W
            wpack = _cast_into(w[n][i], wpack, pk.off[n], name="pack_%s_%d" % (n, l))
        if w_tail:
            wpack = lax.dynamic_update_slice(
                wpack, jnp.concatenate(_pack_blocks(w_tail, 0, BF16), axis=0), (pk.misc, 0))
        layers.append(dict(pk=pk, big=big, tail=tail, w_tail=w_tail, g_tail=g_tail, g_shapes=g_shapes,
                           gather=_gather_start(wpack, name="gather_start_%d" % l)))

    def prefetch(l, after):
        lay = layers[l]
        send_sems, recv_sems, w_thru, land_thru, _ = lay["gather"]
        lay["w_back"], land = _gather_wait(send_sems, recv_sems, w_thru, land_thru, after,
                                           name="gather_wait_%d" % l)
        lay["halves"] = _halves_start(land, name="gather_halves_start_%d" % l)
        return lay["halves"][3][0, 0]

    def fetch(l, after):
        lay = layers[l]
        pk = lay["pk"]
        if l == 0:
            send_sems, recv_sems, w_thru, land_thru, _ = lay["gather"]
            after = sum(layers[k]["gather"][4] for k in range(1, DEPTH))
            w_back, land = _gather_wait(send_sems, recv_sems, w_thru, land_thru, after, name="gather_wait_0")
            land = _halves_to_sibling(land, name="gather_halves_0")
        else:
            send_sems, recv_sems, land_thru, _ = lay["halves"]
            w_back = lay["w_back"]
            land = _halves_wait(send_sems, recv_sems, land_thru, after, name="gather_halves_wait_%d" % l)
        land = lax.dynamic_update_slice(land, w_back[None], (me, 0, 0))
        out = dict(wbuf=land.reshape(N_CHIPS * pk.rows, PACK_W), pk=pk)
        if lay["w_tail"]:
            rows = _packed_misc_rows([t.shape for t in lay["w_tail"]])
            per_chip = [_unpack(land[j, pk.misc:pk.misc + rows], [t.shape for t in lay["w_tail"]])
                        for j in range(N_CHIPS)]
            for i, (n, _) in enumerate(lay["tail"]):
                out[n[4:]] = jnp.concatenate([per_chip[j][i] for j in range(N_CHIPS)], axis=axis_of[n] - 1)
            if l == 0:
                got_hi, got_lo = (lax.bitcast_convert_type(
                    jnp.concatenate([per_chip[j][i] for j in range(N_CHIPS)], axis=2),
                    jnp.uint16).astype(jnp.uint32) for i in (-2, -1))
                out["gains"] = lax.bitcast_convert_type((got_hi << 16) | got_lo, F32)
        return out

    def emit(l, gbuf, grads):
        lay = layers[l]
        pk = lay["pk"]
        if lay["g_tail"]:
            for j in range(N_CHIPS):
                pieces = []
                for n, i in lay["g_tail"]:
                    if n not in axis_of:
                        pieces.append(grads[n])
                    else:
                        pieces.append(jnp.split(grads[n], N_CHIPS, axis=axis_of[n] - (0 if i is None else 1))[j])
                block = jnp.concatenate(_pack_blocks(pieces, 0, BF16), axis=0)
                gbuf = lax.dynamic_update_slice(gbuf, block, (j * pk.rows + pk.misc, 0))
        row0 = lay.get("early_rows", 0)
        lay["scatter"] = _scatter_start(gbuf.reshape(N_CHIPS, pk.rows, PACK_W), row0, pk.rows - row0,
                                        name="scatter_start_%d" % l)
        return lay["scatter"][4][0, 0]

    def emit_mlp(l, gbuf):
        lay = layers[l]
        pk = lay["pk"]
        assert pk.off["mlp_w1"] == 0 and pk.off["mlp_w2"] == w["mlp_w1"].shape[1]
        lay["early_rows"] = w["mlp_w1"].shape[1] + w["mlp_w2"].shape[1]
        lay["scatter_early"] = _scatter_start(gbuf.reshape(N_CHIPS, pk.rows, PACK_W), 0, lay["early_rows"],
                                              name="scatter_start_%d_mlp" % l)
        return lay["scatter_early"][2].reshape(N_CHIPS * pk.rows, PACK_W)

    small = dict(mla_q_norm=mla_q_norm, mla_kv_norm=mla_kv_norm, hgrn_lb_logits=hgrn_lb_logits,
                 hgrn_o_norm=hgrn_o_norm)
    gbufs = [lax.empty((N_CHIPS * lay["pk"].rows, PACK_W), BF16) for lay in layers]
    sq, grad_x = _local_step(x[0], positions[0], loss_target[0], small, prefetch, fetch, gbufs, emit, emit_mlp)
    d_model = x.shape[-1]
    loss = lax.psum(0.5 * jnp.sum(sq) / d_model, ("x", "y", "c"))

    per_name = {}
    behind = grad_x
    for l, lay in reversed(list(enumerate(layers))):
        pk = lay["pk"]
        send_sems, recv_sems, g_thru, land_thru, _ = lay["scatter"]
        row0 = lay.get("early_rows", 0)
        early = None
        if row0:
            e_send, e_recv, _, e_land, _ = lay["scatter_early"]
            g_thru, land = _scatter_wait(e_send, e_recv, g_thru, e_land, behind, name="scatter_wait_%d_mlp" % l)
            early = behind = _sum_chips(land, g_thru, 0, me, name="grads_sum_chips_%d_mlp" % l, out_dtype=BF16)
        g_back, land = _scatter_wait(send_sems, recv_sems, g_thru, land_thru, behind, name="scatter_wait_%d" % l)
        mine = _sum_chips(land, g_back, row0, me, name="grads_sum_chips_%d" % l, out_dtype=BF16)
        if early is not None:
            mine = jnp.concatenate([early, mine], axis=0)
        red = behind = _sum_chips(_share_reduced(mine, name="grads_share_%d" % l), mine, 0, c,
                                  name="grads_sum_cores_%d" % l)
        for n, i in lay["big"]:
            per_name.setdefault(n, {})[i] = red[pk.off[n]:pk.off[n] + w[n].shape[1]]
        for (n, i), piece in zip(lay["g_tail"], _unpack(red[pk.misc:pk.misc + pk.misc_rows], lay["g_shapes"])):
            per_name.setdefault(n, {})[i] = piece
    g_out = {n: (parts[None] if None in parts else jnp.stack([parts[i] for i in sorted(parts)]))
             for n, parts in per_name.items()}

    deltas, new_m, new_v = {}, {}, {}
    for name in WEIGHTS:
        deltas[name], new_m[name], new_v[name] = _adamw(w[name], g_out[name], mom_m[name], mom_v[name],
                                                        name="adamw_" + name)
    return (loss, grad_x[None], *[g_out[n] for n in WEIGHTS], *[deltas[n] for n in WEIGHTS],
            *[new_m[n] for n in WEIGHTS], *[new_v[n] for n in WEIGHTS])
```

```python
import jax
import jax.numpy as jnp
from jax import lax
from jax.experimental import pallas as pl
from jax.experimental.pallas import tpu as pltpu

F32 = jnp.float32
BF16 = jnp.bfloat16
MESH = pl.DeviceIdType.MESH

DEPTH = 4
MLA_HEADS = 8
MLA_NOPE = 128
MLA_ROPE = 64
MLA_V = 128
MLA_QK_PAD = 256
MLA_HEADS_PER_STEP = 2
MLA_SCALE = float(MLA_NOPE + MLA_ROPE) ** -0.5
ROPE_BASE = 10000.0
HGRN_HEADS = 8
HGRN_CHUNK = 32
HGRN_BLOCK = 128
EPS = 1e-6

ADAM_LR = 0.001
ADAM_B1 = 0.9
ADAM_B2 = 0.999
ADAM_EPS = 1e-08
ADAM_WD = 0.01
ADAM_STEP = 10

N_CHIPS = 4
PACK_W = 1024
PACK_ALIGN = 1024
PACK_TILE = 512
V7X_VMEM_LIMIT = 56 * 1024 * 1024

SHARDED = (("norm_gains", 2), ("mla_w_in", 1), ("mla_w_uq", 2), ("mla_w_ukv", 2), ("mla_w_o", 1),
           ("hgrn_w_in", 2), ("hgrn_w_o", 1), ("mlp_w1", 2), ("mlp_w2", 1))
REPLICATED = ("mla_q_norm", "mla_kv_norm", "hgrn_lb_logits", "hgrn_o_norm")
WEIGHTS = ("norm_gains", "mla_w_in", "mla_q_norm", "mla_kv_norm", "mla_w_uq", "mla_w_ukv", "mla_w_o",
           "hgrn_w_in", "hgrn_lb_logits", "hgrn_o_norm", "hgrn_w_o", "mlp_w1", "mlp_w2")


def _cparams(*semantics):
    return pltpu.CompilerParams(dimension_semantics=semantics, vmem_limit_bytes=V7X_VMEM_LIMIT)


def _sigmoid(x):
    return 0.5 * jnp.tanh(0.5 * x) + 0.5


def _mm(a, b, *, ta=False, tb=False, out_dtype=F32, tm=2048, tn=1024, tk=1024, epi=None, extra=None,
        name="mm", n=None, b_map=None, into=None, o_map=None):
    if ta:
        K, M = a.shape
    else:
        M, K = a.shape
    if b_map is not None:
        N = n
    elif tb:
        N, Kb = b.shape
    else:
        Kb, N = b.shape
    assert b_map is not None or K == Kb, (a.shape, b.shape, ta, tb)
    tm, tn = min(tm, M), min(tn, N)
    if ta and b_map is None:
        tk = max(tk, 4096)
    tk = K if (K <= 1024 and b_map is None) else min(tk, K)
    assert M % tm == 0 and N % tn == 0 and K % tk == 0, (M, N, K, tm, tn, tk)
    nk = K // tk
    a_spec = (pl.BlockSpec((tk, tm), lambda i, j, k: (k, i)) if ta
              else pl.BlockSpec((tm, tk), lambda i, j, k: (i, k)))
    if b_map is None:
        b_map = (lambda i, j, k: (j, k)) if tb else (lambda i, j, k: (k, j))
    b_spec = pl.BlockSpec((tn, tk) if tb else (tk, tn), b_map)
    o_spec = pl.BlockSpec((tm, tn), lambda i, j, k: (i, j))
    dims = (((0 if ta else 1,), (1 if tb else 0,)), ((), ()))
    in_specs = [a_spec, b_spec]
    operands = [a, b]
    aliases = {}
    if epi == "mul2r":
        in_specs.append(o_spec)
        operands.append(extra)
    if epi == "resnorm":
        assert tn == N
        vec = pl.BlockSpec((1, N), lambda i, j, k: (0, 0))
        in_specs += [o_spec, vec, vec]
        operands += list(extra)
    if into is not None:
        assert epi is None
        in_specs.append(pl.BlockSpec(memory_space=pl.ANY))
        operands.append(into)
        aliases = {2: 0}
        out_dtype = into.dtype
        out_shape = jax.ShapeDtypeStruct(into.shape, into.dtype)
        out_specs = pl.BlockSpec((tm, tn), o_map)
    elif epi == "relu2":
        out_shape = (jax.ShapeDtypeStruct((M, N), BF16), jax.ShapeDtypeStruct((M, N), BF16))
        out_specs = (o_spec, o_spec)
    elif epi == "mul2r":
        out_shape = jax.ShapeDtypeStruct((M, N), BF16)
        out_specs = o_spec
    elif epi == "resnorm":
        out_shape = (jax.ShapeDtypeStruct((M, N), F32), jax.ShapeDtypeStruct((M, N), F32),
                     jax.ShapeDtypeStruct((M, N), BF16))
        out_specs = (o_spec, o_spec, o_spec)
    else:
        out_shape = jax.ShapeDtypeStruct((M, N), out_dtype)
        out_specs = o_spec
    n_in = len(operands)
    n_out = {"relu2": 2, "resnorm": 3}.get(epi, 1)

    def body(*refs):
        a_ref, b_ref = refs[0], refs[1]
        outs = refs[n_in:n_in + n_out]
        k = pl.program_id(2)

        def finish(acc):
            if epi == "relu2":
                r = jnp.maximum(acc, 0.0)
                outs[0][...] = (r * r).astype(BF16)
                outs[1][...] = r.astype(BF16)
            elif epi == "mul2r":
                outs[0][...] = (acc * (2.0 * refs[2][...].astype(F32))).astype(BF16)
            elif epi == "resnorm":
                h_ref, gp_ref, gn_ref = refs[2], refs[3], refs[4]
                hn = h_ref[...] + acc * _rms_rstd(acc) * gp_ref[...]
                outs[0][...] = acc
                outs[1][...] = hn
                outs[2][...] = (hn * _rms_rstd(hn) * gn_ref[...]).astype(BF16)
            else:
                outs[0][...] = acc.astype(out_dtype)

        part = lax.dot_general(a_ref[...], b_ref[...], dims, preferred_element_type=F32)
        if nk == 1:
            finish(part)
            return
        acc_ref = refs[-1]

        @pl.when(k == 0)
        def _():
            acc_ref[...] = part

        @pl.when((k > 0) & (k < nk - 1))
        def _():
            acc_ref[...] += part

        @pl.when(k == nk - 1)
        def _():
            finish(acc_ref[...] + part)

    return pl.pallas_call(
        body, name=name, grid=(M // tm, N // tn, nk), in_specs=in_specs, out_specs=out_specs,
        out_shape=out_shape, scratch_shapes=[pltpu.VMEM((tm, tn), F32)] if nk > 1 else [],
        input_output_aliases=aliases,
        compiler_params=_cparams("parallel", "parallel", "arbitrary"))(*operands)


def _rms_rstd(x):
    return lax.rsqrt(jnp.mean(x * x, axis=-1, keepdims=True) + EPS)


def _rms_bwd_tile(x, g, dy):
    r = _rms_rstd(x)
    xh = x * r
    u = dy * g
    dx = r * (u - xh * jnp.mean(u * xh, axis=-1, keepdims=True))
    dg = jnp.sum(dy * xh, axis=0, keepdims=True)
    return dx, dg


def _row_tile(T):
    return min(512, T)


def _mid_tile(T):
    return min(512, T)


def _prenorm_fwd(x, g, name="prenorm_fwd"):
    T, D = x.shape
    tm = _row_tile(T)

    def body(x_ref, g_ref, a_ref):
        xv = x_ref[...]
        a_ref[...] = (xv * _rms_rstd(xv) * g_ref[...]).astype(BF16)

    row = pl.BlockSpec((tm, D), lambda i: (i, 0))
    vec = pl.BlockSpec((1, D), lambda i: (0, 0))
    return pl.pallas_call(body, name=name, grid=(T // tm,), in_specs=[row, vec], out_specs=row,
                          out_shape=jax.ShapeDtypeStruct((T, D), BF16),
                          compiler_params=_cparams("parallel"))(x, g)


def _resnorm_loss(h, z, g_post, target, name="resnorm_loss"):
    T, D = h.shape
    tm = _row_tile(T)

    def body(h_ref, z_ref, gp_ref, t_ref, dy_ref, sq_ref):
        zv = z_ref[...]
        err = h_ref[...] + zv * _rms_rstd(zv) * gp_ref[...] - t_ref[...]
        dy_ref[...] = err * (1.0 / D)

        @pl.when(pl.program_id(0) == 0)
        def _():
            sq_ref[...] = jnp.zeros_like(sq_ref)

        sq_ref[...] += jnp.sum(err * err, axis=0, keepdims=True)

    row = pl.BlockSpec((tm, D), lambda i: (i, 0))
    vec = pl.BlockSpec((1, D), lambda i: (0, 0))
    return pl.pallas_call(body, name=name, grid=(T // tm,), in_specs=[row, row, vec, row],
                          out_specs=(row, vec),
                          out_shape=(jax.ShapeDtypeStruct((T, D), F32), jax.ShapeDtypeStruct((1, D), F32)),
                          compiler_params=_cparams("arbitrary"))(h, z, g_post, target)


def _resnorm_bwd(z, g_post, dh, h_new=None, da=None, g_pre=None, name="resnorm_bwd"):
    T, D = z.shape
    tm = _row_tile(T)
    has_next = h_new is not None
    row = pl.BlockSpec((tm, D), lambda i: (i, 0))
    vec = pl.BlockSpec((1, D), lambda i: (0, 0))

    if has_next:
        def body(z_ref, gp_ref, dh_ref, hn_ref, da_ref, gn_ref, t_ref, dz_ref, dgp_ref, dgn_ref):
            first = pl.program_id(0) == 0

            @pl.when(first)
            def _():
                dgp_ref[...] = jnp.zeros_like(dgp_ref)
                dgn_ref[...] = jnp.zeros_like(dgn_ref)

            dpre, dgn = _rms_bwd_tile(hn_ref[...], gn_ref[...], da_ref[...])
            t = dh_ref[...] + dpre
            t_ref[...] = t
            dz, dgp = _rms_bwd_tile(z_ref[...], gp_ref[...], t)
            dz_ref[...] = dz.astype(BF16)
            dgp_ref[...] += dgp
            dgn_ref[...] += dgn

        return pl.pallas_call(
            body, name=name, grid=(T // tm,), in_specs=[row, vec, row, row, row, vec],
            out_specs=(row, row, vec, vec),
            out_shape=(jax.ShapeDtypeStruct((T, D), F32), jax.ShapeDtypeStruct((T, D), BF16),
                       jax.ShapeDtypeStruct((1, D), F32), jax.ShapeDtypeStruct((1, D), F32)),
            compiler_params=_cparams("arbitrary"))(z, g_post, dh, h_new, da, g_pre)

    def body_last(z_ref, gp_ref, dh_ref, dz_ref, dgp_ref):
        @pl.when(pl.program_id(0) == 0)
        def _():
            dgp_ref[...] = jnp.zeros_like(dgp_ref)

        dz, dgp = _rms_bwd_tile(z_ref[...], gp_ref[...], dh_ref[...])
        dz_ref[...] = dz.astype(BF16)
        dgp_ref[...] += dgp

    return pl.pallas_call(
        body_last, name=name, grid=(T // tm,), in_specs=[row, vec, row], out_specs=(row, vec),
        out_shape=(jax.ShapeDtypeStruct((T, D), BF16), jax.ShapeDtypeStruct((1, D), F32)),
        compiler_params=_cparams("arbitrary"))(z, g_post, dh)


def _prenorm_bwd(x, g, dh, da, name="prenorm_bwd"):
    T, D = x.shape
    tm = _row_tile(T)

    def body(x_ref, g_ref, dh_ref, da_ref, dx_ref, dg_ref):
        @pl.when(pl.program_id(0) == 0)
        def _():
            dg_ref[...] = jnp.zeros_like(dg_ref)

        dpre, dg = _rms_bwd_tile(x_ref[...], g_ref[...], da_ref[...])
        dx_ref[...] = dh_ref[...] + dpre
        dg_ref[...] += dg

    row = pl.BlockSpec((tm, D), lambda i: (i, 0))
    vec = pl.BlockSpec((1, D), lambda i: (0, 0))
    return pl.pallas_call(
        body, name=name, grid=(T // tm,), in_specs=[row, vec, row, row], out_specs=(row, vec),
        out_shape=(jax.ShapeDtypeStruct((T, D), F32), jax.ShapeDtypeStruct((1, D), F32)),
        compiler_params=_cparams("arbitrary"))(x, g, dh, da)


class _Packed:
    def __init__(self, big, misc_rows):
        self.big = tuple(big)
        self.off = {}
        r = 0
        for name, rows in big:
            self.off[name] = r
            r += rows
        self.misc, self.misc_rows = r, misc_rows
        self.rows = -(-(r + misc_rows) // PACK_ALIGN) * PACK_ALIGN

    def block(self, name, layer, unit):
        r = self.off[name]
        assert r % unit == 0 and self.rows % unit == 0
        return r // unit, self.rows // unit


def _col_sharded(pk, name, layer, unit):
    base, stride = pk.block(name, layer, unit)
    return (lambda i, j, k: (j * stride + base, 0)), (lambda i, j, k: (k * stride + base, 0))


def _row_sharded(pk, name, layer, unit):
    base, stride = pk.block(name, layer, unit)
    return ((lambda i, j, k: (k * stride + base, 0)), (lambda i, j, k: (j * stride + base, 0)),
            (lambda i, j, k: (i * stride + base, 0)))


def _mlp_fwd(a, wbuf, pk, layer, res):
    D = a.shape[1]
    by_n, _ = _col_sharded(pk, "mlp_w1", layer, D)
    by_k, _, _ = _row_sharded(pk, "mlp_w2", layer, D)
    act, r = _mm(a, wbuf, n=4 * D, b_map=by_n, tk=D, tn=D, epi="relu2", name="mlp_up")
    if res is None:
        return _mm(act, wbuf, n=D, b_map=by_k, tk=D, tn=D, name="mlp_down"), (a, act, r), None, None
    u, h_new, a_next = _mm(act, wbuf, n=D, b_map=by_k, tm=1024, tk=D, tn=D, epi="resnorm", extra=res,
                           name="mlp_down_res")
    return u, (a, act, r), h_new, a_next


def _mlp_bwd(du, saved, wbuf, gbuf, pk, layer):
    a, act, r = saved
    D = a.shape[1]
    w1_by_n, w1_by_k = _col_sharded(pk, "mlp_w1", layer, D)
    _, w2_by_n, w2_by_m = _row_sharded(pk, "mlp_w2", layer, D)
    dz1 = _mm(du, wbuf, tb=True, n=4 * D, b_map=w2_by_n, tn=D, tk=D, epi="mul2r", extra=r, name="mlp_down_dx")
    gbuf = _mm(act, du, ta=True, into=gbuf, o_map=w2_by_m, tm=D, tn=D, name="mlp_down_dw")
    gbuf = _mm(a, dz1, ta=True, into=gbuf, o_map=w1_by_n, tm=D, tn=D, name="mlp_up_dw")
    da = _mm(dz1, wbuf, tb=True, n=D, b_map=w1_by_k, tn=D, tk=D, name="mlp_up_dx")
    return da, gbuf


def _rope_swap(t):
    n = t.shape[-1]
    lane = lax.broadcasted_iota(jnp.int32, t.shape, t.ndim - 1)
    half = MLA_ROPE // 2
    first = (lane & (MLA_ROPE - 1)) < half
    return jnp.where(first, pltpu.roll(t, n - half, t.ndim - 1), pltpu.roll(t, half, t.ndim - 1))


def _mla_mid_fwd(proj, q_norm, kv_norm, w_uq, w_ukv, cc, ss):
    T, PW = proj.shape
    QL, KVL = q_norm.shape[-1], kv_norm.shape[-1]
    H = MLA_HEADS
    assert PW == QL + KVL + 128
    tm = _mid_tile(T)

    def body(p_ref, qn_ref, kn_ref, wq_ref, wkv_ref, cc_ref, ss_ref,
             cq_ref, ckv_ref, q_ref, k_ref, v_ref):
        cq = p_ref[:, 0:QL]
        ckv = p_ref[:, QL:QL + KVL]
        kr = p_ref[:, QL + KVL:QL + KVL + 128]
        c, s = cc_ref[...], ss_ref[...]
        cqn = (cq * _rms_rstd(cq) * qn_ref[...]).astype(BF16)
        ckvn = (ckv * _rms_rstd(ckv) * kn_ref[...]).astype(BF16)
        cq_ref[...] = cqn
        ckv_ref[...] = ckvn
        q = jnp.dot(cqn, wq_ref[...], preferred_element_type=F32)
        kv = jnp.dot(ckvn, wkv_ref[...], preferred_element_type=F32)
        krf = (kr * c + _rope_swap(kr) * s).astype(BF16)
        for h in range(H):
            o = h * MLA_QK_PAD
            q_ref[:, o:o + MLA_NOPE] = (q[:, o:o + MLA_NOPE] * MLA_SCALE).astype(BF16)
            qr = q[:, o + MLA_NOPE:o + MLA_QK_PAD]
            q_ref[:, o + MLA_NOPE:o + MLA_QK_PAD] = ((qr * c + _rope_swap(qr) * s) * MLA_SCALE).astype(BF16)
            k_ref[:, o:o + MLA_NOPE] = kv[:, o:o + MLA_NOPE].astype(BF16)
            k_ref[:, o + MLA_NOPE:o + MLA_QK_PAD] = krf
            v_ref[:, h * MLA_V:(h + 1) * MLA_V] = kv[:, o + MLA_NOPE:o + MLA_QK_PAD].astype(BF16)

    def row(w):
        return pl.BlockSpec((tm, w), lambda i: (i, 0))

    def full(shape):
        return pl.BlockSpec(shape, lambda i: (0, 0))

    return pl.pallas_call(
        body, name="mla_mid_fwd", grid=(T // tm,),
        in_specs=[row(PW), full((1, QL)), full((1, KVL)), full(w_uq.shape), full(w_ukv.shape),
                  row(128), row(128)],
        out_specs=(row(QL), row(KVL), row(H * MLA_QK_PAD), row(H * MLA_QK_PAD), row(H * MLA_V)),
        out_shape=(jax.ShapeDtypeStruct((T, QL), BF16), jax.ShapeDtypeStruct((T, KVL), BF16),
                   jax.ShapeDtypeStruct((T, H * MLA_QK_PAD), BF16),
                   jax.ShapeDtypeStruct((T, H * MLA_QK_PAD), BF16),
                   jax.ShapeDtypeStruct((T, H * MLA_V), BF16)),
        compiler_params=_cparams("parallel"))(proj, q_norm, kv_norm, w_uq, w_ukv, cc, ss)


def _mla_mid_bwd(proj, q_norm, kv_norm, w_uq, w_ukv, cc, ss, dq, dk, dv):
    T, PW = proj.shape
    QL, KVL = q_norm.shape[-1], kv_norm.shape[-1]
    H = MLA_HEADS
    tm = _mid_tile(T)
    nt = (((1,), (1,)), ((), ()))

    def body(p_ref, qn_ref, kn_ref, wq_ref, wkv_ref, cc_ref, ss_ref, dq_ref, dk_ref, dv_ref,
             dqp_ref, dkv_ref, dp_ref, dqn_ref, dkn_ref):
        @pl.when(pl.program_id(0) == 0)
        def _():
            dqn_ref[...] = jnp.zeros_like(dqn_ref)
            dkn_ref[...] = jnp.zeros_like(dkn_ref)

        c, s = cc_ref[...], ss_ref[...]
        dkr = jnp.zeros((tm, 128), F32)
        for h in range(H):
            o = h * MLA_QK_PAD
            dqp_ref[:, o:o + MLA_NOPE] = (dq_ref[:, o:o + MLA_NOPE] * MLA_SCALE).astype(BF16)
            dqr = dq_ref[:, o + MLA_NOPE:o + MLA_QK_PAD] * MLA_SCALE
            dqp_ref[:, o + MLA_NOPE:o + MLA_QK_PAD] = (dqr * c + _rope_swap(dqr * s)).astype(BF16)
            dkv_ref[:, o:o + MLA_NOPE] = dk_ref[:, o:o + MLA_NOPE].astype(BF16)
            dkv_ref[:, o + MLA_NOPE:o + MLA_QK_PAD] = dv_ref[:, h * MLA_V:(h + 1) * MLA_V].astype(BF16)
            dkr = dkr + dk_ref[:, o + MLA_NOPE:o + MLA_QK_PAD]
        dcqn = lax.dot_general(dqp_ref[...], wq_ref[...], nt, preferred_element_type=F32)
        dckvn = lax.dot_general(dkv_ref[...], wkv_ref[...], nt, preferred_element_type=F32)
        dcq, dqn = _rms_bwd_tile(p_ref[:, 0:QL], qn_ref[...], dcqn)
        dckv, dkn = _rms_bwd_tile(p_ref[:, QL:QL + KVL], kn_ref[...], dckvn)
        dp_ref[:, 0:QL] = dcq.astype(BF16)
        dp_ref[:, QL:QL + KVL] = dckv.astype(BF16)
        dp_ref[:, QL + KVL:QL + KVL + 128] = (dkr * c + _rope_swap(dkr * s)).astype(BF16)
        dqn_ref[...] += dqn
        dkn_ref[...] += dkn

    def row(w):
        return pl.BlockSpec((tm, w), lambda i: (i, 0))

    def full(shape):
        return pl.BlockSpec(shape, lambda i: (0, 0))

    return pl.pallas_call(
        body, name="mla_mid_bwd", grid=(T // tm,),
        in_specs=[row(PW), full((1, QL)), full((1, KVL)), full(w_uq.shape), full(w_ukv.shape),
                  row(128), row(128), row(H * MLA_QK_PAD), row(H * MLA_QK_PAD), row(H * MLA_V)],
        out_specs=(row(H * MLA_QK_PAD), row(H * MLA_QK_PAD), row(PW), full((1, QL)), full((1, KVL))),
        out_shape=(jax.ShapeDtypeStruct((T, H * MLA_QK_PAD), BF16),
                   jax.ShapeDtypeStruct((T, H * MLA_QK_PAD), BF16),
                   jax.ShapeDtypeStruct((T, PW), BF16),
                   jax.ShapeDtypeStruct((1, QL), F32), jax.ShapeDtypeStruct((1, KVL), F32)),
        compiler_params=_cparams("arbitrary"))(proj, q_norm, kv_norm, w_uq, w_ukv, cc, ss, dq, dk, dv)


def _attn_tile(T):
    return min(1024, T)


def _attn_pairs(n, by_key):
    if by_key:
        pairs = [(qi, ki) for ki in range(n) for qi in range(ki, n)]
    else:
        pairs = [(qi, ki) for qi in range(n) for ki in range(qi + 1)]
    return (jnp.asarray([p[0] for p in pairs], jnp.int32), jnp.asarray([p[1] for p in pairs], jnp.int32))


def _scores(q, k, diagonal):
    s = lax.dot_general(q, k, (((1,), (1,)), ((), ())), preferred_element_type=F32)
    if diagonal:
        rows = lax.broadcasted_iota(jnp.int32, s.shape, 0)
        cols = lax.broadcasted_iota(jnp.int32, s.shape, 1)
        s = jnp.where(rows >= cols, s, -jnp.inf)
    return s


def _attn_fwd(q, k, v):
    T = q.shape[0]
    H, DQ, DV = MLA_HEADS, MLA_QK_PAD, MLA_V
    tq = _attn_tile(T)
    nq = T // tq
    G = MLA_HEADS_PER_STEP
    qi_tab, ki_tab = _attn_pairs(nq, by_key=False)

    def body(qi_ref, ki_ref, q_ref, k_ref, v_ref, o_ref, lse_ref, *scratch):
        m_refs, l_refs, acc_refs = scratch[0:G], scratch[G:2 * G], scratch[2 * G:3 * G]
        p = pl.program_id(1)
        qi, ki = qi_ref[p], ki_ref[p]

        @pl.when(ki == 0)
        def _():
            for g in range(G):
                m_refs[g][...] = jnp.full_like(m_refs[g], -jnp.inf)
                l_refs[g][...] = jnp.zeros_like(l_refs[g])
                acc_refs[g][...] = jnp.zeros_like(acc_refs[g])

        def update(ks, qr, masked):
            for g in range(G):
                qs, vs = slice(g * DQ, (g + 1) * DQ), slice(g * DV, (g + 1) * DV)
                st = _scores(k_ref[ks, qs], q_ref[qr, qs], False)
                if masked:
                    key = ks.start + lax.broadcasted_iota(jnp.int32, st.shape, 0)
                    qry = qr.start + lax.broadcasted_iota(jnp.int32, st.shape, 1)
                    st = jnp.where(qry >= key, st, -jnp.inf)
                m_prev = m_refs[g][:, qr]
                m_new = jnp.maximum(m_prev, jnp.max(st, axis=0, keepdims=True))
                alpha = jnp.exp(m_prev - m_new)
                pt = jnp.exp(st - m_new)
                l_refs[g][:, qr] = alpha * l_refs[g][:, qr] + jnp.sum(pt, axis=0, keepdims=True)
                acc_refs[g][:, qr] = alpha * acc_refs[g][:, qr] + lax.dot_general(
                    v_ref[ks, vs], pt.astype(BF16), (((0,), (0,)), ((), ())), preferred_element_type=F32)
                m_refs[g][:, qr] = m_new

        whole, half = slice(0, tq), tq // 2

        @pl.when(ki < qi)
        def _():
            update(whole, whole, False)

        @pl.when(ki == qi)
        def _():
            update(slice(0, half), whole, True)
            update(slice(half, tq), slice(half, tq), True)
            for g in range(G):
                vs = slice(g * DV, (g + 1) * DV)
                o_ref[:, vs] = jnp.transpose(acc_refs[g][...] / l_refs[g][...]).astype(BF16)
                lse_ref[g] = m_refs[g][...] + jnp.log(l_refs[g][...])

    return pl.pallas_call(
        body, name="attn_fwd",
        grid_spec=pltpu.PrefetchScalarGridSpec(
            num_scalar_prefetch=2, grid=(H // G, int(qi_tab.shape[0])),
            in_specs=[pl.BlockSpec((tq, G * DQ), lambda h, p, qt, kt: (qt[p], h)),
                      pl.BlockSpec((tq, G * DQ), lambda h, p, qt, kt: (kt[p], h)),
                      pl.BlockSpec((tq, G * DV), lambda h, p, qt, kt: (kt[p], h))],
            out_specs=(pl.BlockSpec((tq, G * DV), lambda h, p, qt, kt: (qt[p], h)),
                       pl.BlockSpec((G, 1, tq), lambda h, p, qt, kt: (h, 0, qt[p]))),
            scratch_shapes=([pltpu.VMEM((1, tq), F32)] * (2 * G) + [pltpu.VMEM((DV, tq), F32)] * G)),
        out_shape=(jax.ShapeDtypeStruct((T, H * DV), BF16), jax.ShapeDtypeStruct((H, 1, T), F32)),
        compiler_params=_cparams("parallel", "arbitrary"))(qi_tab, ki_tab, q, k, v)


def _attn_bwd(q, k, v, o, do, lse):
    T = q.shape[0]
    H, DQ, DV = MLA_HEADS, MLA_QK_PAD, MLA_V
    tq = _attn_tile(T)
    nq = T // tq
    tn = (((0,), (0,)), ((), ()))
    nt = (((1,), (1,)), ((), ()))
    G = MLA_HEADS_PER_STEP
    qi_tab, ki_tab = _attn_pairs(nq, by_key=True)

    def body(qi_ref, ki_ref, q_ref, k_ref, v_ref, o_ref, do_ref, lse_ref, dq_ref, dk_ref, dv_ref,
             dk_acc, dv_acc):
        p = pl.program_id(1)
        qi, ki = qi_ref[p], ki_ref[p]

        @pl.when(p == 0)
        def _():
            dq_ref[...] = jnp.zeros_like(dq_ref)

        @pl.when(qi == ki)
        def _():
            dk_acc[...] = jnp.zeros_like(dk_acc)
            dv_acc[...] = jnp.zeros_like(dv_acc)

        def step(ks, qr, masked):
            rows = pl.ds(pl.multiple_of(qi * tq + qr.start, qr.stop - qr.start), qr.stop - qr.start)
            for g in range(G):
                qs, vs = slice(g * DQ, (g + 1) * DQ), slice(g * DV, (g + 1) * DV)
                dof = do_ref[qr, vs]
                delta = jnp.sum(jnp.transpose(dof.astype(F32) * o_ref[qr, vs].astype(F32)), axis=0,
                                keepdims=True)
                st = _scores(k_ref[ks, qs], q_ref[qr, qs], False)
                if masked:
                    key = ks.start + lax.broadcasted_iota(jnp.int32, st.shape, 0)
                    qry = qr.start + lax.broadcasted_iota(jnp.int32, st.shape, 1)
                    st = jnp.where(qry >= key, st, -jnp.inf)
                pt = jnp.exp(st - lse_ref[g][:, qr])
                dpt = lax.dot_general(v_ref[ks, vs], dof, nt, preferred_element_type=F32)
                dst = (pt * (dpt - delta)).astype(BF16)
                dv_acc[ks, vs] += jnp.dot(pt.astype(BF16), dof, preferred_element_type=F32)
                dk_acc[ks, qs] += jnp.dot(dst, q_ref[qr, qs], preferred_element_type=F32)
                dq_ref[rows, qs] += lax.dot_general(dst, k_ref[ks, qs], tn, preferred_element_type=F32)

        whole, half = slice(0, tq), tq // 2

        @pl.when(qi == ki)
        def _():
            step(slice(0, half), whole, True)
            step(slice(half, tq), slice(half, tq), True)

        @pl.when(qi > ki)
        def _():
            step(whole, whole, False)

        @pl.when(qi == nq - 1)
        def _():
            dk_ref[...] = dk_acc[...]
            dv_ref[...] = dv_acc[...]

    qspec = pl.BlockSpec((tq, G * DQ), lambda h, p, qt, kt: (qt[p], h))
    ospec = pl.BlockSpec((tq, G * DV), lambda h, p, qt, kt: (qt[p], h))
    kspec = pl.BlockSpec((tq, G * DQ), lambda h, p, qt, kt: (kt[p], h))
    vspec = pl.BlockSpec((tq, G * DV), lambda h, p, qt, kt: (kt[p], h))
    return pl.pallas_call(
        body, name="attn_bwd",
        grid_spec=pltpu.PrefetchScalarGridSpec(
            num_scalar_prefetch=2, grid=(H // G, int(qi_tab.shape[0])),
            in_specs=[qspec, kspec, vspec, ospec, ospec,
                      pl.BlockSpec((G, 1, tq), lambda h, p, qt, kt: (h, 0, qt[p]))],
            out_specs=(pl.BlockSpec((T, G * DQ), lambda h, p, qt, kt: (0, h)), kspec, vspec),
            scratch_shapes=[pltpu.VMEM((tq, G * DQ), F32), pltpu.VMEM((tq, G * DV), F32)]),
        out_shape=(jax.ShapeDtypeStruct((T, H * DQ), F32), jax.ShapeDtypeStruct((T, H * DQ), F32),
                   jax.ShapeDtypeStruct((T, H * DV), F32)),
        compiler_params=_cparams("parallel", "arbitrary"))(qi_tab, ki_tab, q, k, v, o, do, lse)


def _mla_fwd(a, w, cc, ss, wbuf, pk, slot, res):
    D = a.shape[1]
    by_k, _, _ = _row_sharded(pk, "mla_w_o", slot, D // N_CHIPS)
    proj = _mm(a, w["w_in"], name="mla_in")
    cqn, ckvn, q, k, v = _mla_mid_fwd(proj, w["q_norm"], w["kv_norm"], w["w_uq"], w["w_ukv"], cc, ss)
    o, lse = _attn_fwd(q, k, v)
    m, h_new, a_next = _mm(o, wbuf, n=D, b_map=by_k, tm=1024, tk=D // N_CHIPS, tn=D, epi="resnorm", extra=res,
                           name="mla_out_res")
    return m, (a, proj, cqn, ckvn, q, k, v, o, lse), h_new, a_next


def _mla_bwd(dm, saved, w, cc, ss, wbuf, gbuf, pk, slot):
    a, proj, cqn, ckvn, q, k, v, o, lse = saved
    D = a.shape[1]
    _, by_n, by_m = _row_sharded(pk, "mla_w_o", slot, D // N_CHIPS)
    do = _mm(dm, wbuf, tb=True, n=o.shape[1], b_map=by_n, tm=2048, tn=D // N_CHIPS, tk=D, out_dtype=BF16,
             name="mla_out_dx")
    gbuf = _mm(o, dm, ta=True, into=gbuf, o_map=by_m, tm=D // N_CHIPS, tn=D, tk=2048, name="mla_out_dw")
    dq, dk, dv = _attn_bwd(q, k, v, o, do, lse)
    dqp, dkv, dproj, dqn, dkn = _mla_mid_bwd(proj, w["q_norm"], w["kv_norm"], w["w_uq"], w["w_ukv"],
                                             cc, ss, dq, dk, dv)
    dw_uq = _mm(cqn, dqp, ta=True, out_dtype=BF16, name="mla_uq_dw")
    dw_ukv = _mm(ckvn, dkv, ta=True, out_dtype=BF16, name="mla_ukv_dw")
    dw_in = _mm(a, dproj, ta=True, out_dtype=BF16, name="mla_in_dw")
    da = _mm(dproj, w["w_in"], tb=True, name="mla_in_dx")
    return da, gbuf, dict(w_in=dw_in, w_uq=dw_uq, w_ukv=dw_ukv, q_norm=dqn, kv_norm=dkn)


def _split_dot(mat, x, parts):
    acc = None
    rem = x
    for _ in range(parts):
        piece = rem.astype(BF16)
        term = jnp.dot(mat, piece, preferred_element_type=F32)
        acc = term if acc is None else acc + term
        rem = rem - piece.astype(F32)
    return acc


def _chunk_sums(cum, rel, rest, logf):
    return tuple(_split_dot(m.astype(BF16), logf, 3) for m in (cum, rel, rest))


def _chunk_mats(tb):
    C = HGRN_CHUNK
    assert C & (C - 1) == 0
    r = lax.broadcasted_iota(jnp.int32, (tb, tb), 0)
    s = lax.broadcasted_iota(jnp.int32, (tb, tb), 1)
    start = r & ~(C - 1)
    same = start == (s & ~(C - 1))
    ref = start + C // 2
    last = start + C - 1
    one, zero = jnp.float32(1.0), jnp.float32(0.0)
    cum = jnp.where(same & (s <= r), one, zero)
    rel = cum - jnp.where(same & (s <= ref), one, zero)
    rest = jnp.where(same & (s > r) & (s <= last), one, zero)
    rev = jnp.where(same & (s >= r), one, zero)
    ones = jnp.where(same, one, zero)
    causal = same & (s <= r)
    return cum, rel, rest, rev, ones, causal


def _hgrn_gates(p_ref, lb, HK):
    qx = p_ref[:, 0:HK]
    fx = p_ref[:, HK:2 * HK]
    sf = _sigmoid(fx)
    f = lb + (1.0 - lb) * sf
    sq = _sigmoid(qx)
    return qx, sq, qx * sq, sf, f, 1.0 - f, jnp.log(f)


def _hgrn_fwd(proj, lb, o_norm):
    T = proj.shape[0]
    H, C = HGRN_HEADS, HGRN_CHUNK
    HK = proj.shape[1] // 4
    DK = HK // H
    tb = min(HGRN_BLOCK, T)
    ncb = tb // C
    nt = (((1,), (1,)), ((), ()))
    tn = (((0,), (0,)), ((), ()))

    def body(p_ref, lb_ref, on_ref, y_ref, o_ref, st_ref, state, oacc):
        @pl.when(pl.program_id(0) == 0)
        def _():
            state[...] = jnp.zeros_like(state)

        cum, rel, rest, _, _, causal = _chunk_mats(tb)
        cum, rel, rest = cum.astype(BF16), rel.astype(BF16), rest.astype(BF16)
        gh = max(H // 2, 1)
        gw = gh * DK
        for g0 in range(0, H, gh):
            lo = g0 * DK
            qx = p_ref[:, lo:lo + gw]
            lbg = lb_ref[:, lo:lo + gw]
            f = lbg + (1.0 - lbg) * _sigmoid(p_ref[:, HK + lo:HK + lo + gw])
            q = qx * _sigmoid(qx)
            k = 1.0 - f
            logf = jnp.log(f)
            b, brel, brest = (_split_dot(m, logf, 3) for m in (cum, rel, rest))
            q_rel = (q * jnp.exp(brel)).astype(BF16)
            k_rel = (k * jnp.exp(-brel)).astype(BF16)
            q_dec = (q * jnp.exp(b)).astype(BF16)
            k_dec = (k * jnp.exp(brest)).astype(BF16)
            v = p_ref[:, 2 * HK + lo:2 * HK + lo + gw].astype(BF16)
            gx = p_ref[:, 3 * HK + lo:3 * HK + lo + gw]
            gate = gx * _sigmoid(gx)
            for i in range(gh):
                h = g0 + i
                ls = slice(i * DK, (i + 1) * DK)
                hs = slice(h * DK, (h + 1) * DK)
                a = lax.dot_general(q_rel[:, ls], k_rel[:, ls], nt, preferred_element_type=F32)
                a = jnp.where(causal, a, 0.0).astype(BF16)
                oacc[:, hs] = jnp.dot(a, v[:, ls], preferred_element_type=F32)
                for j in range(ncb):
                    rs = slice(j * C, (j + 1) * C)
                    st = state[h]
                    st_ref[j, h] = st
                    oacc[rs, hs] += lax.dot_general(q_dec[rs, ls], st.astype(BF16), nt,
                                                    preferred_element_type=F32)
                    dec = jnp.exp(jnp.sum(logf[rs, ls], axis=0, keepdims=True))
                    state[h] = dec * st + lax.dot_general(v[rs, ls], k_dec[rs, ls], tn,
                                                          preferred_element_type=F32)
                oh = oacc[:, hs]
                o_ref[:, hs] = oh
                y_ref[:, hs] = (oh * _rms_rstd(oh) * on_ref[...] * gate[:, ls]).astype(BF16)

    return pl.pallas_call(
        body, name="hgrn_fwd", grid=(T // tb,),
        in_specs=[pl.BlockSpec((tb, 4 * HK), lambda i: (i, 0)),
                  pl.BlockSpec((1, HK), lambda i: (0, 0)),
                  pl.BlockSpec((1, DK), lambda i: (0, 0))],
        out_specs=(pl.BlockSpec((tb, HK), lambda i: (i, 0)),
                   pl.BlockSpec((tb, HK), lambda i: (i, 0)),
                   pl.BlockSpec((ncb, H, DK, DK), lambda i: (i, 0, 0, 0))),
        out_shape=(jax.ShapeDtypeStruct((T, HK), BF16), jax.ShapeDtypeStruct((T, HK), F32),
                   jax.ShapeDtypeStruct((T // C, H, DK, DK), F32)),
        scratch_shapes=[pltpu.VMEM((H, DK, DK), F32), pltpu.VMEM((tb, HK), F32)],
        compiler_params=_cparams("arbitrary"))(proj, lb, o_norm)


def _hgrn_bwd(proj, lb, o_norm, o, states, dy):
    T = proj.shape[0]
    H, C = HGRN_HEADS, HGRN_CHUNK
    HK = proj.shape[1] // 4
    DK = HK // H
    tb = min(HGRN_BLOCK, T)
    ncb = tb // C
    nb = T // tb
    nt = (((1,), (1,)), ((), ()))
    tn = (((0,), (0,)), ((), ()))

    def body(p_ref, lb_ref, on_ref, o_ref, st_ref, dy_ref, dp_ref, dlb_ref, don_ref,
             dstate, dqr_s, dkr_s, dqd_s, dkd_s, dv_s, do_s, e_s):
        @pl.when(pl.program_id(0) == 0)
        def _():
            dstate[...] = jnp.zeros_like(dstate)
            dlb_ref[...] = jnp.zeros_like(dlb_ref)
            don_ref[...] = jnp.zeros_like(don_ref)

        cum, rel, rest, rev, ones, causal = _chunk_mats(tb)
        lb = lb_ref[...]
        qx, sq, q, sf, f, k, logf = _hgrn_gates(p_ref, lb, HK)
        b, brel, brest = _chunk_sums(cum, rel, rest, logf)
        eb = jnp.exp(b)
        erel = jnp.exp(brel)
        enrel = jnp.exp(-brel)
        erest = jnp.exp(brest)
        q_rel_f, k_rel_f, q_dec_f, k_dec_f = q * erel, k * enrel, q * eb, k * erest
        q_rel, k_rel = q_rel_f.astype(BF16), k_rel_f.astype(BF16)
        q_dec, k_dec = q_dec_f.astype(BF16), k_dec_f.astype(BF16)
        v = p_ref[:, 2 * HK:3 * HK].astype(BF16)

        gx = p_ref[:, 3 * HK:4 * HK]
        sg = _sigmoid(gx)
        gate = gx * sg
        dy = dy_ref[...]
        ov = o_ref[...]
        on = on_ref[...]
        don = jnp.zeros((1, DK), F32)
        for h in range(H):
            hs = slice(h * DK, (h + 1) * DK)
            oh = ov[:, hs]
            r = _rms_rstd(oh)
            xh = oh * r
            d_on = dy[:, hs] * gate[:, hs]
            don = don + jnp.sum(d_on * xh, axis=0, keepdims=True)
            u = d_on * on
            do_s[:, hs] = r * (u - xh * jnp.mean(u * xh, axis=-1, keepdims=True))
            dp_ref[:, 3 * HK + h * DK:3 * HK + (h + 1) * DK] = (
                dy[:, hs] * xh * on * (sg[:, hs] * (1.0 + gx[:, hs] * (1.0 - sg[:, hs])))).astype(BF16)
        don_ref[...] += don

        for h in range(H):
            hs = slice(h * DK, (h + 1) * DK)
            doh = do_s[:, hs].astype(BF16)
            a = lax.dot_general(q_rel[:, hs], k_rel[:, hs], nt, preferred_element_type=F32)
            a = jnp.where(causal, a, 0.0).astype(BF16)
            da = lax.dot_general(doh, v[:, hs], nt, preferred_element_type=F32)
            da = jnp.where(causal, da, 0.0).astype(BF16)
            dv_s[:, hs] = lax.dot_general(a, doh, tn, preferred_element_type=F32)
            dqr_s[:, hs] = jnp.dot(da, k_rel[:, hs], preferred_element_type=F32)
            dkr_s[:, hs] = lax.dot_general(da, q_rel[:, hs], tn, preferred_element_type=F32)
            for j in reversed(range(ncb)):
                rs = slice(j * C, (j + 1) * C)
                dst = dstate[h]
                dstb = dst.astype(BF16)
                st = st_ref[j, h]
                dkd_s[rs, hs] = jnp.dot(v[rs, hs], dstb, preferred_element_type=F32)
                dv_s[rs, hs] += lax.dot_general(k_dec[rs, hs], dstb, nt, preferred_element_type=F32)
                dec = jnp.exp(jnp.sum(logf[rs, hs], axis=0, keepdims=True))
                e_s[rs, hs] = jnp.broadcast_to(jnp.sum(dst * st, axis=0, keepdims=True) * dec, (C, DK))
                dqd_s[rs, hs] = jnp.dot(doh[rs], st.astype(BF16), preferred_element_type=F32)
                dstate[h] = dec * dst + lax.dot_general(doh[rs], q_dec[rs, hs], tn,
                                                        preferred_element_type=F32)

        dqr, dkr, dqd, dkd = dqr_s[...], dkr_s[...], dqd_s[...], dkd_s[...]
        kdk = dkd * k_dec_f
        db = dqr * q_rel_f - dkr * k_rel_f + dqd * q_dec_f - kdk
        dlogf = _split_dot(rev.astype(BF16), db, 2) + _split_dot(ones.astype(BF16), kdk, 2) + e_s[...]
        dk = dkr * enrel + dkd * erest
        df = dlogf / f - dk
        dlb_ref[...] += jnp.sum(df * (1.0 - sf), axis=0, keepdims=True)
        dq = dqr * erel + dqd * eb
        dp_ref[:, 0:HK] = (dq * (sq * (1.0 + qx * (1.0 - sq)))).astype(BF16)
        dp_ref[:, HK:2 * HK] = (df * (1.0 - lb) * sf * (1.0 - sf)).astype(BF16)
        dp_ref[:, 2 * HK:3 * HK] = dv_s[...].astype(BF16)

    rev_row = lambda w: pl.BlockSpec((tb, w), lambda i: (nb - 1 - i, 0))
    vec = lambda w: pl.BlockSpec((1, w), lambda i: (0, 0))
    scr = pltpu.VMEM((tb, HK), F32)
    return pl.pallas_call(
        body, name="hgrn_bwd", grid=(nb,),
        in_specs=[rev_row(4 * HK), vec(HK), vec(DK), rev_row(HK),
                  pl.BlockSpec((ncb, H, DK, DK), lambda i: (nb - 1 - i, 0, 0, 0)), rev_row(HK)],
        out_specs=(rev_row(4 * HK), vec(HK), vec(DK)),
        out_shape=(jax.ShapeDtypeStruct((T, 4 * HK), BF16), jax.ShapeDtypeStruct((1, HK), F32),
                   jax.ShapeDtypeStruct((1, DK), F32)),
        scratch_shapes=[pltpu.VMEM((H, DK, DK), F32), scr, scr, scr, scr, scr, scr, scr],
        compiler_params=_cparams("arbitrary"))(proj, lb, o_norm, o, states, dy)


def _hgrn_layer_fwd(a, o_norm, lb, wbuf, pk, slot, res):
    D = a.shape[1]
    in_by_n, _ = _col_sharded(pk, "hgrn_w_in", slot, D)
    out_by_k, _, _ = _row_sharded(pk, "hgrn_w_o", slot, D // N_CHIPS)
    proj = _mm(a, wbuf, n=4 * D, b_map=in_by_n, tk=D, tn=D, name="hgrn_in")
    y, o, states = _hgrn_fwd(proj, lb, o_norm)
    m, h_new, a_next = _mm(y, wbuf, n=D, b_map=out_by_k, tm=1024, tk=D // N_CHIPS, tn=D, epi="resnorm",
                           extra=res, name="hgrn_out_res")
    return m, (a, proj, y, o, states), h_new, a_next


def _hgrn_layer_bwd(dm, saved, o_norm, lb, wbuf, gbuf, pk, slot):
    a, proj, y, o, states = saved
    D = a.shape[1]
    in_by_n, in_by_k = _col_sharded(pk, "hgrn_w_in", slot, D)
    _, out_by_n, out_by_m = _row_sharded(pk, "hgrn_w_o", slot, D // N_CHIPS)
    dy = _mm(dm, wbuf, tb=True, n=y.shape[1], b_map=out_by_n, tm=2048, tn=D // N_CHIPS, tk=D,
             name="hgrn_out_dx")
    gbuf = _mm(y, dm, ta=True, into=gbuf, o_map=out_by_m, tm=D // N_CHIPS, tn=D, tk=2048, name="hgrn_out_dw")
    dproj, dlb, don = _hgrn_bwd(proj, lb, o_norm, o, states, dy)
    gbuf = _mm(a, dproj, ta=True, into=gbuf, o_map=in_by_n, tm=D, tn=D, name="hgrn_in_dw")
    da = _mm(dproj, wbuf, tb=True, n=D, b_map=in_by_k, tn=D, tk=D, name="hgrn_in_dx")
    return da, gbuf, dict(o_norm=don, lb=dlb)


def _lower_bounds(lb_logits):
    p = jax.nn.softmax(lb_logits.astype(F32), axis=0)
    return jnp.cumsum(p, axis=0) - p[0]


def _rope_tables(positions):
    inv_freq = jnp.power(ROPE_BASE, -jnp.arange(0, MLA_ROPE, 2, dtype=F32) / MLA_ROPE)
    ang = positions.astype(F32)[:, None] * inv_freq
    cos, sin = jnp.cos(ang), jnp.sin(ang)
    zero = jnp.zeros((positions.shape[0], 128 - MLA_ROPE), F32)
    return (jnp.concatenate([cos, cos, zero], axis=-1), jnp.concatenate([-sin, sin, zero], axis=-1))


def _pad_mla_weights(w_in, w_uq):
    w_in_p = jnp.pad(w_in, ((0, 0), (0, 0), (0, 128 - MLA_ROPE)))
    n, ql, _ = w_uq.shape
    w_uq_p = jnp.pad(w_uq.reshape(n, ql, MLA_HEADS, MLA_NOPE + MLA_ROPE),
                     ((0, 0), (0, 0), (0, 0), (0, MLA_QK_PAD - MLA_NOPE - MLA_ROPE)))
    return w_in_p, w_uq_p.reshape(n, ql, MLA_HEADS * MLA_QK_PAD)


def _local_step(x, positions, target, small, prefetch, fetch, gbufs, emit, emit_mlp):
    T, D = x.shape
    lbounds, lb_vjp = jax.vjp(_lower_bounds, small["hgrn_lb_logits"])
    cc, ss = _rope_tables(positions)
    fetched = {0: fetch(0, None)}
    gains = fetched[0]["gains"]
    tick = [jnp.zeros((), F32)]

    def g(layer, i):
        return gains[layer, i][None, :] + tick[0]

    def mla_weights(layer):
        f = fetched[layer]
        w_in_p, w_uq_p = _pad_mla_weights(f["w_in"][None], f["w_uq"][None])
        slot = layer // 2
        return dict(w_in=w_in_p[0], w_uq=w_uq_p[0], w_ukv=f["w_ukv"],
                    q_norm=small["mla_q_norm"][slot][None, :], kv_norm=small["mla_kv_norm"][slot][None, :])

    saved = []
    h = x
    a = _prenorm_fwd(x, g(0, 0))
    dy = sq = None
    for layer in range(DEPTH):
        slot = layer // 2
        if layer not in fetched:
            fetched[layer] = fetch(layer, a)
        wbuf, pk = fetched[layer]["wbuf"], fetched[layer]["pk"]
        res = (h, g(layer, 1), g(layer, 2))
        if layer % 2 == 0:
            m, mix_saved, h1, a2 = _mla_fwd(a, mla_weights(layer), cc, ss, wbuf, pk, slot, res)
        else:
            m, mix_saved, h1, a2 = _hgrn_layer_fwd(a, small["hgrn_o_norm"][slot][None, :],
                                                   lbounds[layer][None, :], wbuf, pk, slot, res)
        if layer + 1 < DEPTH:
            tick[0] = prefetch(layer + 1, a2)
            u, mlp_saved, h2, a = _mlp_fwd(a2, wbuf, pk, layer, (h1, g(layer, 3), g(layer + 1, 0)))
        else:
            u, mlp_saved, h2, _ = _mlp_fwd(a2, wbuf, pk, layer, None)
            dy, sq = _resnorm_loss(h1, u, g(layer, 3), target)
        saved.append((h, m, h1, u, mix_saved, mlp_saved))
        h = h2

    n_mla, n_hgrn = (DEPTH + 1) // 2, DEPTH // 2
    dgains = [[None] * 4 for _ in range(DEPTH)]
    gw = {k: [None] * n_mla for k in ("mla_w_in", "mla_w_uq", "mla_w_ukv", "mla_q_norm", "mla_kv_norm")}
    gw["hgrn_o_norm"] = [None] * n_hgrn
    dlb = [jnp.zeros((1, lbounds.shape[1]), F32) for _ in range(DEPTH)]
    dh = dy
    da_next = None
    for layer in reversed(range(DEPTH)):
        h0, m, h1, u, mix_saved, mlp_saved = saved[layer]
        slot = layer // 2
        wbuf, pk, gbuf = fetched[layer]["wbuf"], fetched[layer]["pk"], gbufs[layer]
        if da_next is None:
            du, dgains[layer][3] = _resnorm_bwd(u, g(layer, 3), dh, name="resnorm_bwd_last")
            t = dh
        else:
            h2 = saved[layer + 1][0]
            t, du, dgains[layer][3], dgains[layer + 1][0] = _resnorm_bwd(
                u, g(layer, 3), dh, h2, da_next, g(layer + 1, 0), name="resnorm_bwd_mlp")
        da2, gbuf = _mlp_bwd(du, mlp_saved, wbuf, gbuf, pk, layer)
        if layer == 0:
            gbuf = emit_mlp(layer, gbuf)
        t, dm, dgains[layer][1], dgains[layer][2] = _resnorm_bwd(
            m, g(layer, 1), t, h1, da2, g(layer, 2), name="resnorm_bwd_mix")
        if layer % 2 == 0:
            da_next, gbuf, mg = _mla_bwd(dm, mix_saved, mla_weights(layer), cc, ss, wbuf, gbuf, pk, slot)
            ql = mg["q_norm"].shape[-1]
            kvl = mg["kv_norm"].shape[-1]
            gw["mla_w_in"][slot] = mg["w_in"][:, :ql + kvl + MLA_ROPE]
            gw["mla_w_uq"][slot] = mg["w_uq"].reshape(ql, MLA_HEADS, MLA_QK_PAD)[
                :, :, :MLA_NOPE + MLA_ROPE].reshape(ql, MLA_HEADS * (MLA_NOPE + MLA_ROPE))
            gw["mla_w_ukv"][slot] = mg["w_ukv"]
            gw["mla_q_norm"][slot] = mg["q_norm"][0]
            gw["mla_kv_norm"][slot] = mg["kv_norm"][0]
        else:
            da_next, gbuf, hg = _hgrn_layer_bwd(dm, mix_saved, small["hgrn_o_norm"][slot][None, :],
                                                lbounds[layer][None, :], wbuf, gbuf, pk, slot)
            gw["hgrn_o_norm"][slot] = hg["o_norm"][0]
            dlb[layer] = hg["lb"]
        dh = t
        if layer > 0:
            mine = ({k: gw[k][slot] for k in ("mla_w_in", "mla_w_uq", "mla_w_ukv")} if layer % 2 == 0 else {})
            tick[0] = emit(layer, gbuf, mine)
        else:
            gbuf0 = gbuf
    grad_x, dgains[0][0] = _prenorm_bwd(x, g(0, 0), dh, da_next)

    last = {k: gw[k][0] for k in ("mla_w_in", "mla_w_uq", "mla_w_ukv")}
    last.update({k: jnp.stack(gw[k]) for k in ("mla_q_norm", "mla_kv_norm", "hgrn_o_norm")})
    last["norm_gains"] = jnp.stack([jnp.concatenate(row, axis=0) for row in dgains])
    (last["hgrn_lb_logits"],) = lb_vjp(jnp.concatenate(dlb, axis=0))
    emit(0, gbuf0, last)
    return sq, grad_x


def _size(shape):
    n = 1
    for d in shape:
        n *= d
    return n


def _piece_rows(shape):
    return -(-_size(shape) // PACK_W)


def _packed_misc_rows(shapes):
    return sum(_piece_rows(s) for s in shapes)


def _cast_into(src, buf, row, name):
    rows, W = src.shape
    tr = min(256, rows)
    assert rows % tr == 0 and row % tr == 0

    def body(s_ref, b_ref, o_ref):
        o_ref[...] = s_ref[...].astype(BF16)

    return pl.pallas_call(
        body, name=name, grid=(rows // tr,),
        in_specs=[pl.BlockSpec((tr, W), lambda i: (i, 0)), pl.BlockSpec(memory_space=pl.ANY)],
        out_specs=pl.BlockSpec((tr, W), lambda i: (row // tr + i, 0)),
        out_shape=jax.ShapeDtypeStruct(buf.shape, buf.dtype), input_output_aliases={1: 0},
        compiler_params=_cparams("parallel"))(src, buf)


def _pack_blocks(pieces, rows, dtype):
    blocks, used = [], 0
    for p in pieces:
        flat = p.astype(dtype).reshape(-1)
        r = _piece_rows(p.shape)
        if r * PACK_W != flat.shape[0]:
            flat = jnp.pad(flat, (0, r * PACK_W - flat.shape[0]))
        blocks.append(flat.reshape(r, PACK_W))
        used += r
    if rows > used:
        blocks.append(jnp.zeros((rows - used, PACK_W), dtype))
    return blocks


def _unpack(buf, shapes):
    out, off = [], 0
    for shp in shapes:
        r = _piece_rows(shp)
        piece = buf[off:off + r]
        if r * PACK_W != _size(shp):
            piece = piece.reshape(-1)[:_size(shp)]
        out.append(piece.reshape(shp))
        off += r
    return out


def _mesh_place():
    x, y, c = lax.axis_index("x"), lax.axis_index("y"), lax.axis_index("c")
    chips = [(1 - x, y), (x, 1 - y), (1 - x, 1 - y)]
    return x, y, c, chips


_HBM = pl.BlockSpec(memory_space=pltpu.HBM)


def _share_reduced(q, name="grads_share_reduced"):
    rh, W = q.shape

    def body(q_ref, out_ref, send_sem, recv_sem):
        x, y, c, _ = _mesh_place()
        cp = pltpu.make_async_remote_copy(src_ref=q_ref, dst_ref=out_ref.at[c], send_sem=send_sem,
                                          recv_sem=recv_sem, device_id=(x, y, 1 - c), device_id_type=MESH)
        cp.start()
        cp.wait()

    out = pl.pallas_call(
        body, name=name, in_specs=[_HBM], out_specs=_HBM,
        out_shape=jax.ShapeDtypeStruct((2, rh, W), q.dtype),
        scratch_shapes=[pltpu.SemaphoreType.DMA, pltpu.SemaphoreType.DMA],
    )(q)
    return out


def _sum_chips(parts, own, own_row0, which, name, out_dtype=F32):
    n, rh, W = parts.shape
    tr = PACK_TILE
    assert own_row0 % tr == 0
    if own.ndim == 3:
        own_spec = pl.BlockSpec((None, tr, W), lambda i, w_ref: (w_ref[0], own_row0 // tr + i, 0))
    else:
        own_spec = pl.BlockSpec((tr, W), lambda i, w_ref: (own_row0 // tr + i, 0))

    def body(w_ref, p_ref, own_ref, o_ref):
        mine = own_ref[...].astype(F32)
        acc = None
        for j in range(n):
            term = jnp.where(w_ref[0] == j, mine, p_ref[j].astype(F32))
            acc = term if acc is None else acc + term
        o_ref[...] = acc.astype(out_dtype)

    return pl.pallas_call(
        body, name=name,
        grid_spec=pltpu.PrefetchScalarGridSpec(
            num_scalar_prefetch=1, grid=(rh // tr,),
            in_specs=[pl.BlockSpec((n, tr, W), lambda i, w_ref: (0, i, 0)), own_spec],
            out_specs=pl.BlockSpec((tr, W), lambda i, w_ref: (i, 0))),
        out_shape=jax.ShapeDtypeStruct((rh, W), out_dtype),
        compiler_params=_cparams("parallel"))(jnp.reshape(which, (1,)).astype(jnp.int32), parts, own)


_SEM = pl.BlockSpec(memory_space=pltpu.SEMAPHORE)
_ASYNC = pltpu.CompilerParams(has_side_effects=pltpu.SideEffectType.DATAFLOW_SIDE_EFFECTING)


def _hbm(a):
    return pltpu.with_memory_space_constraint(a, pltpu.HBM)


def _gather_copies(w_ref, land_ref, send_sems, recv_sems):
    x, y, c, chips = _mesh_place()
    me = 2 * x + y
    rh = w_ref.shape[0] // 2
    rows = pl.ds(pl.multiple_of(c * rh, 16), rh)
    return [pltpu.make_async_remote_copy(
        src_ref=w_ref.at[rows], dst_ref=land_ref.at[me, rows], send_sem=send_sems.at[r],
        recv_sem=recv_sems.at[r], device_id=(px, py, c), device_id_type=MESH)
        for r, (px, py) in enumerate(chips)]


def _scatter_copies(g_ref, land_ref, send_sems, recv_sems, row0):
    x, y, c, chips = _mesh_place()
    me = 2 * x + y
    rows = pl.ds(row0, land_ref.shape[1])
    return [pltpu.make_async_remote_copy(
        src_ref=g_ref.at[2 * px + py, rows], dst_ref=land_ref.at[me], send_sem=send_sems.at[r],
        recv_sem=recv_sems.at[r], device_id=(px, py, c), device_id_type=MESH)
        for r, (px, py) in enumerate(chips)]


def _halves_copies(land_ref, send_sems, recv_sems):
    x, y, c, chips = _mesh_place()
    rh = land_ref.shape[1] // 2
    rows = pl.ds(pl.multiple_of(c * rh, 16), rh)
    return [pltpu.make_async_remote_copy(
        src_ref=land_ref.at[2 * px + py, rows], dst_ref=land_ref.at[2 * px + py, rows], send_sem=send_sems.at[r],
        recv_sem=recv_sems.at[r], device_id=(x, y, 1 - c), device_id_type=MESH)
        for r, (px, py) in enumerate(chips)]


def _halves_to_sibling(land, name):
    def body(l_ref, o_ref, send_sems, recv_sems):
        copies = _halves_copies(o_ref, send_sems, recv_sems)
        for cp in copies:
            cp.start()
        for cp in copies:
            cp.wait()

    return pl.pallas_call(
        body, name=name, in_specs=[_HBM], out_specs=_HBM, out_shape=jax.ShapeDtypeStruct(land.shape, land.dtype),
        scratch_shapes=[pltpu.SemaphoreType.DMA((3,)), pltpu.SemaphoreType.DMA((3,))],
        input_output_aliases={0: 0})(land)


def _halves_start(land, name):
    def body(l_ref, send_sems, recv_sems, land_thru, token):
        for cp in _halves_copies(l_ref, send_sems, recv_sems):
            cp.start()
        token[...] = jnp.zeros_like(token)

    return pl.pallas_call(
        body, name=name,
        out_shape=(pltpu.SemaphoreType.DMA((3,)), pltpu.SemaphoreType.DMA((3,)), pltpu.HBM(land.shape, land.dtype),
                   jax.ShapeDtypeStruct((8, 128), F32)),
        in_specs=(_HBM,), out_specs=(_SEM, _SEM, _HBM, pl.BlockSpec(memory_space=pltpu.VMEM)),
        input_output_aliases={0: 2}, compiler_params=_ASYNC,
    )(_hbm(land))


def _halves_wait(send_sems, recv_sems, land_thru, after, name):
    rh = land_thru.shape[1] // 2

    def body(land_ref, send_sems, recv_sems, after_ref, got_ref):
        x, y, c, _ = _mesh_place()
        half = land_ref.at[0, pl.ds(0, rh)]
        for k in range(3):
            cp = pltpu.make_async_remote_copy(src_ref=half, dst_ref=half, send_sem=send_sems.at[k],
                                              recv_sem=recv_sems.at[k], device_id=(x, y, 1 - c),
                                              device_id_type=MESH)
            cp.wait_send()
            cp.wait_recv()

    return pl.pallas_call(
        body, name=name, out_shape=pltpu.HBM(land_thru.shape, land_thru.dtype),
        in_specs=(_HBM, _SEM, _SEM, pl.BlockSpec(memory_space=pl.ANY)), out_specs=_HBM,
        input_output_aliases={0: 0}, compiler_params=_ASYNC,
    )(land_thru, send_sems, recv_sems, after)


def _gather_start(wp, name):
    R, W = wp.shape

    def body(w_ref, land_ref, send_sems, recv_sems, w_thru, land_thru, token):
        for cp in _gather_copies(w_ref, land_ref, send_sems, recv_sems):
            cp.start()
        token[...] = jnp.zeros_like(token)

    return pl.pallas_call(
        body, name=name,
        out_shape=(pltpu.SemaphoreType.DMA((3,)), pltpu.SemaphoreType.DMA((3,)), pltpu.HBM(wp.shape, wp.dtype),
                   pltpu.HBM((N_CHIPS, R, W), wp.dtype), jax.ShapeDtypeStruct((8, 128), F32)),
        in_specs=(_HBM, _HBM),
        out_specs=(_SEM, _SEM, _HBM, _HBM, pl.BlockSpec(memory_space=pltpu.VMEM)),
        input_output_aliases={0: 2, 1: 3}, compiler_params=_ASYNC,
    )(_hbm(wp), _hbm(lax.empty((N_CHIPS, R, W), wp.dtype)))


def _gather_wait(send_sems, recv_sems, w_thru, land_thru, after, name):
    R, W = w_thru.shape
    rh = R // 2

    def body(w_ref, land_ref, send_sems, recv_sems, after_ref, w_dead, got_ref):
        x, y, c, _ = _mesh_place()
        half = land_ref.at[0, pl.ds(0, rh)]
        for k in range(3):
            cp = pltpu.make_async_remote_copy(src_ref=half, dst_ref=half, send_sem=send_sems.at[k],
                                              recv_sem=recv_sems.at[k], device_id=(x, y, 1 - c),
                                              device_id_type=MESH)
            cp.wait_send()
            cp.wait_recv()

    return pl.pallas_call(
        body, name=name,
        out_shape=(pltpu.HBM(w_thru.shape, w_thru.dtype), pltpu.HBM(land_thru.shape, land_thru.dtype)),
        in_specs=(_HBM, _HBM, _SEM, _SEM, pl.BlockSpec(memory_space=pl.ANY)), out_specs=(_HBM, _HBM),
        input_output_aliases={0: 0, 1: 1}, compiler_params=_ASYNC,
    )(w_thru, land_thru, send_sems, recv_sems, after)


def _scatter_start(g, row0, nrows, name):
    n, R, W = g.shape
    land_shape = (n, nrows, W)

    def body(g_ref, land_ref, send_sems, recv_sems, g_thru, land_thru, token):
        for cp in _scatter_copies(g_ref, land_ref, send_sems, recv_sems, row0):
            cp.start()
        token[...] = jnp.zeros_like(token)

    return pl.pallas_call(
        body, name=name,
        out_shape=(pltpu.SemaphoreType.DMA((3,)), pltpu.SemaphoreType.DMA((3,)), pltpu.HBM(g.shape, g.dtype),
                   pltpu.HBM(land_shape, g.dtype), jax.ShapeDtypeStruct((8, 128), F32)),
        in_specs=(_HBM, _HBM),
        out_specs=(_SEM, _SEM, _HBM, _HBM, pl.BlockSpec(memory_space=pltpu.VMEM)),
        input_output_aliases={0: 2, 1: 3}, compiler_params=_ASYNC,
    )(_hbm(g), _hbm(lax.empty(land_shape, g.dtype)))


def _scatter_wait(send_sems, recv_sems, g_thru, land_thru, after, name):
    def body(g_ref, land_ref, send_sems, recv_sems, after_ref, g_out, got_ref):
        x, y, c, _ = _mesh_place()
        for k in range(3):
            cp = pltpu.make_async_remote_copy(src_ref=land_ref.at[0], dst_ref=land_ref.at[0], send_sem=send_sems.at[k],
                                              recv_sem=recv_sems.at[k], device_id=(x, y, 1 - c),
                                              device_id_type=MESH)
            cp.wait_send()
            cp.wait_recv()

    return pl.pallas_call(
        body, name=name,
        out_shape=(pltpu.HBM(g_thru.shape, g_thru.dtype), pltpu.HBM(land_thru.shape, land_thru.dtype)),
        in_specs=(_HBM, _HBM, _SEM, _SEM, pl.BlockSpec(memory_space=pl.ANY)), out_specs=(_HBM, _HBM),
        input_output_aliases={0: 0, 1: 1}, compiler_params=_ASYNC,
    )(g_thru, land_thru, send_sems, recv_sems, after)


def _adamw(w, g, m, v, name):
    shape = w.shape
    cols = shape[-1]
    w2, g2, m2, v2 = (t.reshape(-1, cols) for t in (w, g, m, v))
    rows = w2.shape[0]
    tr = rows
    for cand in (512, 256, 128, 64, 32, 16, 8):
        if rows > cand and rows % cand == 0:
            tr = cand
            break
    c1 = 1.0 / (1.0 - ADAM_B1 ** ADAM_STEP)
    c2 = 1.0 / (1.0 - ADAM_B2 ** ADAM_STEP)

    def body(w_ref, g_ref, m_ref, v_ref, d_ref, nm_ref, nv_ref):
        gv = g_ref[...]
        nm = ADAM_B1 * m_ref[...] + (1.0 - ADAM_B1) * gv
        nv = ADAM_B2 * v_ref[...] + (1.0 - ADAM_B2) * (gv * gv)
        nm_ref[...] = nm
        nv_ref[...] = nv
        d_ref[...] = -ADAM_LR * ((nm * c1) / (jnp.sqrt(nv * c2) + ADAM_EPS) + ADAM_WD * w_ref[...])

    blk = pl.BlockSpec((tr, cols), lambda i: (i, 0))
    sds = jax.ShapeDtypeStruct((rows, cols), F32)
    d, nm, nv = pl.pallas_call(body, name=name, grid=(rows // tr,), in_specs=[blk] * 4,
                               out_specs=(blk, blk, blk), out_shape=(sds, sds, sds),
                               compiler_params=_cparams("parallel"))(w2, g2, m2, v2)
    return d.reshape(shape), nm.reshape(shape), nv.reshape(shape)


def kernel(x, positions, norm_gains, mla_w_in, mla_q_norm, mla_kv_norm, mla_w_uq, mla_w_ukv, mla_w_o, hgrn_w_in, hgrn_lb_logits, hgrn_o_norm, hgrn_w_o, mlp_w1, mlp_w2, loss_target, m_norm_gains, m_mla_w_in, m_mla_q_norm, m_mla_kv_norm, m_mla_w_uq, m_mla_w_ukv, m_mla_w_o, m_hgrn_w_in, m_hgrn_lb_logits, m_hgrn_o_norm, m_hgrn_w_o, m_mlp_w1, m_mlp_w2, v_norm_gains, v_mla_w_in, v_mla_q_norm, v_mla_kv_norm, v_mla_w_uq, v_mla_w_ukv, v_mla_w_o, v_hgrn_w_in, v_hgrn_lb_logits, v_hgrn_o_norm, v_hgrn_w_o, v_mlp_w1, v_mlp_w2):
    w = dict(norm_gains=norm_gains, mla_w_in=mla_w_in, mla_q_norm=mla_q_norm, mla_kv_norm=mla_kv_norm,
             mla_w_uq=mla_w_uq, mla_w_ukv=mla_w_ukv, mla_w_o=mla_w_o, hgrn_w_in=hgrn_w_in,
             hgrn_lb_logits=hgrn_lb_logits, hgrn_o_norm=hgrn_o_norm, hgrn_w_o=hgrn_w_o,
             mlp_w1=mlp_w1, mlp_w2=mlp_w2)
    mom_m = dict(norm_gains=m_norm_gains, mla_w_in=m_mla_w_in, mla_q_norm=m_mla_q_norm,
                 mla_kv_norm=m_mla_kv_norm, mla_w_uq=m_mla_w_uq, mla_w_ukv=m_mla_w_ukv,
                 mla_w_o=m_mla_w_o, hgrn_w_in=m_hgrn_w_in, hgrn_lb_logits=m_hgrn_lb_logits,
                 hgrn_o_norm=m_hgrn_o_norm, hgrn_w_o=m_hgrn_w_o, mlp_w1=m_mlp_w1, mlp_w2=m_mlp_w2)
    mom_v = dict(norm_gains=v_norm_gains, mla_w_in=v_mla_w_in, mla_q_norm=v_mla_q_norm,
                 mla_kv_norm=v_mla_kv_norm, mla_w_uq=v_mla_w_uq, mla_w_ukv=v_mla_w_ukv,
                 mla_w_o=v_mla_w_o, hgrn_w_in=v_hgrn_w_in, hgrn_lb_logits=v_hgrn_lb_logits,
                 hgrn_o_norm=v_hgrn_o_norm, hgrn_w_o=v_hgrn_w_o, mlp_w1=v_mlp_w1, mlp_w2=v_mlp_w2)
    c = lax.axis_index("c")

    axis_of = dict(SHARDED)
    me = 2 * lax.axis_index("x") + lax.axis_index("y")
    gain_bits = lax.bitcast_convert_type(norm_gains, jnp.uint32)
    gain_hi = lax.bitcast_convert_type((gain_bits >> 16).astype(jnp.uint16), BF16)
    gain_lo = lax.bitcast_convert_type((gain_bits & 0xFFFF).astype(jnp.uint16), BF16)

    layers = []
    for l in range(DEPTH):
        s = l // 2
        if l % 2 == 0:
            big = [("mlp_w1", l), ("mlp_w2", l), ("mla_w_o", s)]
            tail = [("mla_w_in", s), ("mla_w_uq", s), ("mla_w_ukv", s)]
        else:
            big = [("hgrn_w_in", s), ("mlp_w1", l), ("mlp_w2", l), ("hgrn_w_o", s)]
            tail = []
        w_tail = [w[n][i] for n, i in tail] + ([gain_hi, gain_lo] if l == 0 else [])
        g_tail = tail + ([("norm_gains", None)] + [(n, None) for n in REPLICATED] if l == 0 else [])
        g_shapes = [w[n].shape if i is None else w[n][i].shape for n, i in g_tail]
        tail_rows = max(_packed_misc_rows([t.shape for t in w_tail]), _packed_misc_rows(g_shapes))
        pk = _Packed([(n, w[n].shape[1]) for n, _ in big], tail_rows)
        wpack = jnp.zeros((pk.rows, PACK_W), BF16)
        for n, i in big:
            assert w[n].shape[2] == PACK_W
            wpack = _cast_into(w[n][i], wpack, pk.off[n], name="pack_%s_%d" % (n, l))
        if w_tail:
            wpack = lax.dynamic_update_slice(
                wpack, jnp.concatenate(_pack_blocks(w_tail, 0, BF16), axis=0), (pk.misc, 0))
        layers.append(dict(pk=pk, big=big, tail=tail, w_tail=w_tail, g_tail=g_tail, g_shapes=g_shapes,
                           gather=_gather_start(wpack, name="gather_start_%d" % l)))

    def prefetch(l, after):
        lay = layers[l]
        send_sems, recv_sems, w_thru, land_thru, _ = lay["gather"]
        lay["w_back"], land = _gather_wait(send_sems, recv_sems, w_thru, land_thru, after,
                                           name="gather_wait_%d" % l)
        lay["halves"] = _halves_start(land, name="gather_halves_start_%d" % l)
        return lay["halves"][3][0, 0]

    def fetch(l, after):
        lay = layers[l]
        pk = lay["pk"]
        if l == 0:
            send_sems, recv_sems, w_thru, land_thru, _ = lay["gather"]
            after = sum(layers[k]["gather"][4] for k in range(1, DEPTH))
            w_back, land = _gather_wait(send_sems, recv_sems, w_thru, land_thru, after, name="gather_wait_0")
            land = _halves_to_sibling(land, name="gather_halves_0")
        else:
            send_sems, recv_sems, land_thru, _ = lay["halves"]
            w_back = lay["w_back"]
            land = _halves_wait(send_sems, recv_sems, land_thru, after, name="gather_halves_wait_%d" % l)
        land = lax.dynamic_update_slice(land, w_back[None], (me, 0, 0))
        out = dict(wbuf=land.reshape(N_CHIPS * pk.rows, PACK_W), pk=pk)
        if lay["w_tail"]:
            rows = _packed_misc_rows([t.shape for t in lay["w_tail"]])
            per_chip = [_unpack(land[j, pk.misc:pk.misc + rows], [t.shape for t in lay["w_tail"]])
                        for j in range(N_CHIPS)]
            for i, (n, _) in enumerate(lay["tail"]):
                out[n[4:]] = jnp.concatenate([per_chip[j][i] for j in range(N_CHIPS)], axis=axis_of[n] - 1)
            if l == 0:
                got_hi, got_lo = (lax.bitcast_convert_type(
                    jnp.concatenate([per_chip[j][i] for j in range(N_CHIPS)], axis=2),
                    jnp.uint16).astype(jnp.uint32) for i in (-2, -1))
                out["gains"] = lax.bitcast_convert_type((got_hi << 16) | got_lo, F32)
        return out

    def emit(l, gbuf, grads):
        lay = layers[l]
        pk = lay["pk"]
        if lay["g_tail"]:
            for j in range(N_CHIPS):
                pieces = []
                for n, i in lay["g_tail"]:
                    if n not in axis_of:
                        pieces.append(grads[n])
                    else:
                        pieces.append(jnp.split(grads[n], N_CHIPS, axis=axis_of[n] - (0 if i is None else 1))[j])
                block = jnp.concatenate(_pack_blocks(pieces, 0, BF16), axis=0)
                gbuf = lax.dynamic_update_slice(gbuf, block, (j * pk.rows + pk.misc, 0))
        row0 = lay.get("early_rows", 0)
        lay["scatter"] = _scatter_start(gbuf.reshape(N_CHIPS, pk.rows, PACK_W), row0, pk.rows - row0,
                                        name="scatter_start_%d" % l)
        return lay["scatter"][4][0, 0]

    def emit_mlp(l, gbuf):
        lay = layers[l]
        pk = lay["pk"]
        assert pk.off["mlp_w1"] == 0 and pk.off["mlp_w2"] == w["mlp_w1"].shape[1]
        lay["early_rows"] = w["mlp_w1"].shape[1] + w["mlp_w2"].shape[1]
        lay["scatter_early"] = _scatter_start(gbuf.reshape(N_CHIPS, pk.rows, PACK_W), 0, lay["early_rows"],
                                              name="scatter_start_%d_mlp" % l)
        return lay["scatter_early"][2].reshape(N_CHIPS * pk.rows, PACK_W)

    small = dict(mla_q_norm=mla_q_norm, mla_kv_norm=mla_kv_norm, hgrn_lb_logits=hgrn_lb_logits,
                 hgrn_o_norm=hgrn_o_norm)
    gbufs = [lax.empty((N_CHIPS * lay["pk"].rows, PACK_W), BF16) for lay in layers]
    sq, grad_x = _local_step(x[0], positions[0], loss_target[0], small, prefetch, fetch, gbufs, emit, emit_mlp)
    d_model = x.shape[-1]
    loss = lax.psum(0.5 * jnp.sum(sq) / d_model, ("x", "y", "c"))

    per_name = {}
    behind = grad_x
    for l, lay in reversed(list(enumerate(layers))):
        pk = lay["pk"]
        send_sems, recv_sems, g_thru, land_thru, _ = lay["scatter"]
        row0 = lay.get("early_rows", 0)
        early = None
        if row0:
            e_send, e_recv, _, e_land, _ = lay["scatter_early"]
            g_thru, land = _scatter_wait(e_send, e_recv, g_thru, e_land, behind, name="scatter_wait_%d_mlp" % l)
            early = behind = _sum_chips(land, g_thru, 0, me, name="grads_sum_chips_%d_mlp" % l, out_dtype=BF16)
        g_back, land = _scatter_wait(send_sems, recv_sems, g_thru, land_thru, behind, name="scatter_wait_%d" % l)
        mine = _sum_chips(land, g_back, row0, me, name="grads_sum_chips_%d" % l, out_dtype=BF16)
        if early is not None:
            mine = jnp.concatenate([early, mine], axis=0)
        red = behind = _sum_chips(_share_reduced(mine, name="grads_share_%d" % l), mine, 0, c,
                                  name="grads_sum_cores_%d" % l)
        for n, i in lay["big"]:
            per_name.setdefault(n, {})[i] = red[pk.off[n]:pk.off[n] + w[n].shape[1]]
        for (n, i), piece in zip(lay["g_tail"], _unpack(red[pk.misc:pk.misc + pk.misc_rows], lay["g_shapes"])):
            per_name.setdefault(n, {})[i] = piece
    g_out = {n: (parts[None] if None in parts else jnp.stack([parts[i] for i in sorted(parts)]))
             for n, parts in per_name.items()}

    deltas, new_m, new_v = {}, {}, {}
    for name in WEIGHTS:
        deltas[name], new_m[name], new_v[name] = _adamw(w[name], g_out[name], mom_m[name], mom_v[name],
                                                        name="adamw_" + name)
    return (loss, grad_x[None], *[g_out[n] for n in WEIGHTS], *[deltas[n] for n in WEIGHTS],
            *[new_m[n] for n in WEIGHTS], *[new_v[n] for n in WEIGHTS])
```
